```python
import math
import jax, jax.numpy as jnp
from jax import lax
import numpy as np

D_MODEL = 1024
BATCH = 16
SEQ = 256
DEPTH = 2
DEC_BATCH = 4
DEC_SEQ = 1024
PAST_LEN = 512

GRID_W = 64
N_EVEN = (DEPTH + 1) // 2
N_ODD = DEPTH // 2
S5_WIDTH = D_MODEL // 2
S5_GROUP_CH = 16
S5_GROUPS = S5_WIDTH // S5_GROUP_CH
S5_STATE = 64
GLA_HEADS = 4
GLA_VW = D_MODEL // 2
GLA_DV = GLA_VW // GLA_HEADS
GLA_DK = GLA_DV // 2
GLA_QK = GLA_HEADS * GLA_DK
GLA_RANK = 16
GLA_TAU = 16.0
GLA_CHUNK = 64
IN_EVEN = S5_WIDTH + 2 * GLA_QK + 2 * GLA_VW + 2 * GLA_RANK
MIX_EVEN = S5_WIDTH + GLA_VW
HEAD_DIM = 128
N_HEADS = D_MODEL // HEAD_DIM
KV_HEADS = 2
QKV_W = (N_HEADS + 2 * KV_HEADS) * HEAD_DIM
Q_BLOCK = 128
AXIS_DIM = HEAD_DIM // 2
ROPE_THETA = 10000.0
D_FF = 4 * D_MODEL
EPS = 1e-6

kernel_name = "hybrid_s5_gla_gqa_context_prefix_step"


def rms_norm(x, g):
    xf = x.astype(jnp.float32)
    y = xf * lax.rsqrt(jnp.mean(xf * xf, axis=-1, keepdims=True) + EPS)
    return (y * g.astype(jnp.float32)).astype(x.dtype)


def ada_mod(cond, w, b):
    m = jax.nn.silu(cond) @ w + b
    return jnp.split(m[:, None, :], 6, axis=-1)


def modulate(h, shift, scale):
    return h * (1 + scale) + shift


def _lin_rec(e1, e2):
    a1, b1 = e1
    a2, b2 = e2
    return a1 * a2, a2 * b1 + b2


def s5_bidir(u, h0_re, h0_im, lam_re, lam_im, log_dt, b_re, b_im, c_re, c_im, d_skip, w_glu, b_glu):
    f32 = jnp.float32
    B, L, _ = u.shape
    uf = u.astype(f32).reshape(B, L, S5_GROUPS, S5_GROUP_CH)
    uc = uf.astype(jnp.complex64)
    Bm = lax.complex(b_re.astype(f32), b_im.astype(f32))
    Cm = lax.complex(c_re.astype(f32), c_im.astype(f32))
    y = uf * d_skip.astype(f32).reshape(S5_GROUPS, S5_GROUP_CH)
    fin_re, fin_im = [], []
    for d, rev in ((0, False), (1, True)):
        lam = lax.complex(lam_re[d].astype(f32), lam_im[d].astype(f32))
        dt = jnp.exp(log_dt[d].astype(f32))[:, None]
        lam_bar = jnp.exp(lam * dt)
        b_bar = ((lam_bar - 1.0) / lam)[..., None] * Bm
        bu = jnp.einsum('blgc,gpc->blgp', uc, b_bar)
        h0 = lax.complex(h0_re[:, d].astype(f32), h0_im[:, d].astype(f32))
        edge = L - 1 if rev else 0
        bu = bu.at[:, edge].add(lam_bar * h0)
        a = jnp.broadcast_to(lam_bar, bu.shape)
        _, hs = lax.associative_scan(_lin_rec, (a, bu), reverse=rev, axis=1)
        fin = hs[:, edge]
        fin_re.append(jnp.real(fin))
        fin_im.append(jnp.imag(fin))
        y = y + jnp.real(jnp.einsum('blgp,gcp->blgc', hs, Cm))
    g = jax.nn.gelu(y.reshape(B, L, S5_WIDTH))
    out = g * jax.nn.sigmoid(g @ w_glu.astype(f32) + b_glu.astype(f32))
    return out.astype(u.dtype), jnp.stack(fin_re, axis=1), jnp.stack(fin_im, axis=1)


def gla_chunked(q, k, v, g, s0):
    B, L, H, K = q.shape
    n = L // GLA_CHUNK

    def to_chunks(t):
        return jnp.moveaxis(t.reshape(B, n, GLA_CHUNK, H, t.shape[-1]), 1, 0)

    mask = jnp.tril(jnp.ones((GLA_CHUNK, GLA_CHUNK), dtype=bool))

    def step(S, inp):
        qc, kc, vc, gc = inp
        b = jnp.cumsum(gc, axis=1)
        b_last = b[:, -1]
        qd = qc * jnp.exp(b)
        kd = kc * jnp.exp(-b)
        att = jnp.where(mask, jnp.einsum('bthk,bshk->bhts', qd, kd), 0.0)
        o = jnp.einsum('bthk,bhkv->bthv', qd, S) + jnp.einsum('bhts,bshv->bthv', att, vc)
        S = S * jnp.exp(b_last)[..., None] + jnp.einsum('bshk,bshv->bhkv', kc * jnp.exp(b_last[:, None] - b), vc)
        return S, o

    S, o = lax.scan(step, s0, (to_chunks(q), to_chunks(k), to_chunks(v), to_chunks(g)))
    o = jnp.moveaxis(o, 0, 1).reshape(B, L, H, v.shape[-1])
    return o, S


def gla_bidir(q, k, v, r, glr, s0, w_gate2, b_gate, gla_norm):
    f32 = jnp.float32
    B, L, _ = q.shape
    qf = q.astype(f32).reshape(B, L, GLA_HEADS, GLA_DK) * (GLA_DK ** -0.5)
    kf = k.astype(f32).reshape(B, L, GLA_HEADS, GLA_DK)
    vf = v.astype(f32).reshape(B, L, GLA_HEADS, GLA_DV)
    glr = glr.astype(f32).reshape(B, L, 2, GLA_RANK)
    o_sum = jnp.zeros((B, L, GLA_HEADS, GLA_DV), f32)
    finals = []
    for d in (0, 1):
        gd = jax.nn.log_sigmoid(glr[:, :, d] @ w_gate2[d].astype(f32) + b_gate[d].astype(f32))
        gd = gd.reshape(B, L, GLA_HEADS, GLA_DK) / GLA_TAU
        if d == 0:
            od, sd = gla_chunked(qf, kf, vf, gd, s0[:, d].astype(f32))
        else:
            od, sd = gla_chunked(jnp.flip(qf, 1), jnp.flip(kf, 1), jnp.flip(vf, 1), jnp.flip(gd, 1), s0[:, d].astype(f32))
            od = jnp.flip(od, 1)
        o_sum = o_sum + od
        finals.append(sd)
    o = rms_norm(o_sum, gla_norm) * jax.nn.silu(r.astype(f32)).reshape(B, L, GLA_HEADS, GLA_DV)
    return o.reshape(B, L, GLA_VW).astype(q.dtype), jnp.stack(finals, axis=1)


def even_mixer(h, h0_re, h0_im, gla_s0, w_in, w_out, lam_re, lam_im, log_dt, b_re, b_im, c_re, c_im,
               d_skip, w_glu, b_glu, w_gate2, b_gate, gla_norm):
    z = h @ w_in
    cuts = np.cumsum([S5_WIDTH, GLA_QK, GLA_QK, GLA_VW, GLA_VW]).tolist()
    u, q, k, v, r, glr = jnp.split(z, cuts, axis=-1)
    s5_out, s5_re, s5_im = s5_bidir(u, h0_re, h0_im, lam_re, lam_im, log_dt, b_re, b_im, c_re, c_im,
                                    d_skip, w_glu, b_glu)
    gla_out, gla_fin = gla_bidir(q, k, v, r, glr, gla_s0, w_gate2, b_gate, gla_norm)
    out = jnp.concatenate([s5_out, gla_out.astype(s5_out.dtype)], axis=-1) @ w_out
    return out, s5_re, s5_im, gla_fin


def attn_project(h, w_qkv, q_norm, k_norm):
    B, L, _ = h.shape
    z = h @ w_qkv
    q, k, v = jnp.split(z, [N_HEADS * HEAD_DIM, (N_HEADS + KV_HEADS) * HEAD_DIM], axis=-1)
    q = rms_norm(q.reshape(B, L, N_HEADS, HEAD_DIM), q_norm)
    k = rms_norm(k.reshape(B, L, KV_HEADS, HEAD_DIM), k_norm)
    v = v.reshape(B, L, KV_HEADS, HEAD_DIM)
    return q, k, v


def grid_rope(L):
    rows = L // GRID_W
    row = jnp.repeat(jnp.arange(rows, dtype=jnp.float32), GRID_W)
    col = jnp.tile(jnp.arange(GRID_W, dtype=jnp.float32), rows)
    inv = ROPE_THETA ** (-jnp.arange(0, AXIS_DIM, 2, dtype=jnp.float32) / AXIS_DIM)
    ang = jnp.concatenate([row[:, None] * inv, col[:, None] * inv], axis=-1)
    return jnp.cos(ang), jnp.sin(ang)


def apply_rope(x, cos, sin):
    xf = x.astype(jnp.float32).reshape(*x.shape[:-1], HEAD_DIM // 2, 2)
    x1, x2 = xf[..., 0], xf[..., 1]
    c = cos[None, :, None, :]
    s = sin[None, :, None, :]
    out = jnp.stack([x1 * c - x2 * s, x1 * s + x2 * c], axis=-1)
    return out.reshape(x.shape).astype(x.dtype)


def attend(q, k, v):
    B, Lq, H, Dh = q.shape
    rep = H // KV_HEADS
    nb = Lq // Q_BLOCK
    qb = jnp.moveaxis(q.reshape(B, nb, Q_BLOCK, KV_HEADS, rep, Dh), 1, 0)
    scale = Dh ** -0.5

    def block(qc):
        s = jnp.einsum('bqgrd,bkgd->bgrqk', qc, k).astype(jnp.float32) * scale
        p = jax.nn.softmax(s, axis=-1).astype(v.dtype)
        return jnp.einsum('bgrqk,bkgd->bqgrd', p, v)

    o = lax.map(block, qb)
    return jnp.moveaxis(o, 0, 1).reshape(B, Lq, H * Dh)


def sq_relu_mlp(h, w1, w2):
    return jnp.square(jax.nn.relu(h @ w1)) @ w2


def setup_inputs(seed: int = 0) -> dict:
    key = jax.random.key(seed)
    ks = iter(jax.random.split(key, 48))
    f32 = jnp.float32

    def nrm(shape, scale=1.0):
        return jax.random.normal(next(ks), shape, f32) * scale

    lam_im_base = jnp.pi * jnp.arange(S5_STATE, dtype=f32)
    return {
        "x_prompt": nrm((BATCH, SEQ, D_MODEL)),
        "x_sample": nrm((DEC_BATCH, DEC_SEQ, D_MODEL)),
        "state_s5_re": nrm((DEC_BATCH, N_EVEN, 2, S5_GROUPS, S5_STATE), 0.1),
        "state_s5_im": nrm((DEC_BATCH, N_EVEN, 2, S5_GROUPS, S5_STATE), 0.1),
        "state_gla": nrm((DEC_BATCH, N_EVEN, 2, GLA_HEADS, GLA_DK, GLA_DV), 0.1),
        "cache_k": nrm((DEC_BATCH, N_ODD, PAST_LEN, KV_HEADS, HEAD_DIM)),
        "cache_v": nrm((DEC_BATCH, N_ODD, PAST_LEN, KV_HEADS, HEAD_DIM)),
        "c": nrm((DEC_BATCH, D_MODEL)),
        "c_ctx": nrm((D_MODEL,)),
        "norm_mix": 1.0 + nrm((DEPTH, D_MODEL), 0.01),
        "norm_mlp": 1.0 + nrm((DEPTH, D_MODEL), 0.01),
        "w_ada": nrm((DEPTH, D_MODEL, 6 * D_MODEL), 0.5 * D_MODEL ** -0.5),
        "b_ada": nrm((DEPTH, 6 * D_MODEL), 0.01),
        "w_mlp_in": nrm((DEPTH, D_MODEL, D_FF), D_MODEL ** -0.5),
        "w_mlp_out": nrm((DEPTH, D_FF, D_MODEL), D_FF ** -0.5),
        "w_in_e": nrm((N_EVEN, D_MODEL, IN_EVEN), D_MODEL ** -0.5),
        "w_out_e": nrm((N_EVEN, MIX_EVEN, D_MODEL), MIX_EVEN ** -0.5),
        "s5_lambda_re": -0.5 + nrm((N_EVEN, 2, S5_GROUPS, S5_STATE), 0.01),
        "s5_lambda_im": lam_im_base + nrm((N_EVEN, 2, S5_GROUPS, S5_STATE), 0.01),
        "s5_log_dt": jax.random.uniform(next(ks), (N_EVEN, 2, S5_GROUPS), f32,
                                        minval=math.log(1e-3), maxval=math.log(1e-1)),
        "s5_b_re": nrm((N_EVEN, S5_GROUPS, S5_STATE, S5_GROUP_CH), (2 * S5_GROUP_CH) ** -0.5),
        "s5_b_im": nrm((N_EVEN, S5_GROUPS, S5_STATE, S5_GROUP_CH), (2 * S5_GROUP_CH) ** -0.5),
        "s5_c_re": nrm((N_EVEN, S5_GROUPS, S5_GROUP_CH, S5_STATE), 0.5 ** 0.5),
        "s5_c_im": nrm((N_EVEN, S5_GROUPS, S5_GROUP_CH, S5_STATE), 0.5 ** 0.5),
        "s5_d": nrm((N_EVEN, S5_WIDTH)),
        "s5_w_glu": nrm((N_EVEN, S5_WIDTH, S5_WIDTH), S5_WIDTH ** -0.5),
        "s5_b_glu": nrm((N_EVEN, S5_WIDTH), 0.01),
        "gla_w_gate2": nrm((N_EVEN, 2, GLA_RANK, GLA_QK), GLA_RANK ** -0.5),
        "gla_b_gate": nrm((N_EVEN, 2, GLA_QK), 0.01),
        "gla_norm": 1.0 + nrm((N_EVEN, GLA_DV), 0.01),
        "w_qkv_o": nrm((N_ODD, D_MODEL, QKV_W), D_MODEL ** -0.5),
        "w_o_o": nrm((N_ODD, N_HEADS * HEAD_DIM, D_MODEL), (N_HEADS * HEAD_DIM) ** -0.5),
        "q_norm": 1.0 + nrm((N_ODD, HEAD_DIM), 0.01),
        "k_norm": 1.0 + nrm((N_ODD, HEAD_DIM), 0.01),
    }


def reference(x_prompt, x_sample, state_s5_re, state_s5_im, state_gla, cache_k, cache_v, c, c_ctx,
              norm_mix, norm_mlp, w_ada, b_ada, w_mlp_in, w_mlp_out, w_in_e, w_out_e,
              s5_lambda_re, s5_lambda_im, s5_log_dt, s5_b_re, s5_b_im, s5_c_re, s5_c_im, s5_d,
              s5_w_glu, s5_b_glu, gla_w_gate2, gla_b_gate, gla_norm, w_qkv_o, w_o_o, q_norm, k_norm):
    yp, ys = x_prompt, x_sample
    bp = x_prompt.shape[0]
    new_s5_re, new_s5_im, new_gla, new_k, new_v = [], [], [], [], []
    for i in range(DEPTH):
        j = i // 2
        sp1, cp1, gp1, sp2, cp2, gp2 = ada_mod(c_ctx[None, :], w_ada[i], b_ada[i])
        sl1, cl1, gl1, sl2, cl2, gl2 = ada_mod(c, w_ada[i], b_ada[i])
        hp = modulate(rms_norm(yp, norm_mix[i]), sp1, cp1)
        hs = modulate(rms_norm(ys, norm_mix[i]), sl1, cl1)
        if i % 2 == 0:
            ev = (w_in_e[j], w_out_e[j], s5_lambda_re[j], s5_lambda_im[j], s5_log_dt[j], s5_b_re[j],
                  s5_b_im[j], s5_c_re[j], s5_c_im[j], s5_d[j], s5_w_glu[j], s5_b_glu[j],
                  gla_w_gate2[j], gla_b_gate[j], gla_norm[j])
            z_s5 = jnp.zeros((bp, 2, S5_GROUPS, S5_STATE), jnp.float32)
            z_gla = jnp.zeros((bp, 2, GLA_HEADS, GLA_DK, GLA_DV), jnp.float32)
            mp, s5r, s5i, gfin = even_mixer(hp, z_s5, z_s5, z_gla, *ev)
            new_s5_re.append(s5r)
            new_s5_im.append(s5i)
            new_gla.append(gfin)
            ms, _, _, _ = even_mixer(hs, state_s5_re[:, j], state_s5_im[:, j], state_gla[:, j], *ev)
        else:
            qp, kp, vp = attn_project(hp, w_qkv_o[j], q_norm[j], k_norm[j])
            mp = attend(qp, kp, vp) @ w_o_o[j]
            new_k.append(kp)
            new_v.append(vp)
            qs, ks_, vs = attn_project(hs, w_qkv_o[j], q_norm[j], k_norm[j])
            cos, sin = grid_rope(hs.shape[1])
            qs = apply_rope(qs, cos, sin)
            ks_ = apply_rope(ks_, cos, sin)
            k_all = jnp.concatenate([ks_, cache_k[:, j].astype(ks_.dtype)], axis=1)
            v_all = jnp.concatenate([vs, cache_v[:, j].astype(vs.dtype)], axis=1)
            ms = attend(qs, k_all, v_all) @ w_o_o[j]
        yp = yp + gp1 * mp
        ys = ys + gl1 * ms
        hp = modulate(rms_norm(yp, norm_mlp[i]), sp2, cp2)
        hs = modulate(rms_norm(ys, norm_mlp[i]), sl2, cl2)
        yp = yp + gp2 * sq_relu_mlp(hp, w_mlp_in[i], w_mlp_out[i])
        ys = ys + gl2 * sq_relu_mlp(hs, w_mlp_in[i], w_mlp_out[i])
    s5_re_out = jnp.stack(new_s5_re, axis=1)
    s5_im_out = jnp.stack(new_s5_im, axis=1)
    gla_out = jnp.stack(new_gla, axis=1)
    k_out = jnp.stack(new_k, axis=1)
    v_out = jnp.stack(new_v, axis=1)
    return (yp, ys, s5_re_out, s5_im_out, gla_out, k_out, v_out)
```

```python
import functools
import math

import jax
import jax.numpy as jnp
import numpy as np
from jax import lax
from jax.experimental import pallas as pl
from jax.experimental.pallas import tpu as pltpu

F32 = jnp.float32
BF16 = jnp.bfloat16

D_MODEL = 1024
BATCH = 16
SEQ = 256
DEPTH = 2
DEC_BATCH = 4
DEC_SEQ = 1024
PAST_LEN = 512
GRID_W = 64
S5_WIDTH = 512
S5_GROUP_CH = 16
S5_GROUPS = 32
S5_STATE = 64
GLA_HEADS = 4
GLA_VW = 512
GLA_DV = 128
GLA_DK = 64
GLA_QK = 256
GLA_RANK = 16
GLA_TAU = 16.0
GLA_CHUNK = 64
HEAD_DIM = 128
N_HEADS = 8
KV_HEADS = 2
Q_PER_KV = N_HEADS // KV_HEADS
AXIS_DIM = 64
ROPE_THETA = 10000.0
D_FF = 4096
EPS = 1e-6

T_PROMPT = BATCH * SEQ
T_SAMPLE = DEC_BATCH * DEC_SEQ
T_TOK = T_PROMPT + T_SAMPLE
COND_ROWS = 8
COND_SPAN = 1024
PROMPT_SPANS = T_PROMPT // COND_SPAN

S5_Q = 16
S5_W = S5_Q * S5_GROUP_CH
S5_ROWS = T_TOK // S5_Q
S5_PROMPT_ROWS = T_PROMPT // S5_Q
S5_PROMPT_CHUNKS = SEQ // S5_Q
S5_SAMPLE_CHUNKS = DEC_SEQ // S5_Q

VMEM_LIMIT = 56 * 1024 * 1024

NT_DIMS = (((1,), (1,)), ((), ()))
TN_DIMS = (((0,), (0,)), ((), ()))


def _cond_row(i, tile):
    return jnp.maximum((i * tile) // COND_SPAN - (PROMPT_SPANS - 1), 0)


def _norm_mod(x, gain, shift, scale):
    y = x * lax.rsqrt(jnp.mean(x * x, axis=-1, keepdims=True) + EPS)
    return (y * gain) * (1.0 + scale) + shift


def _dot(a, b):
    return jnp.dot(a, b, preferred_element_type=F32)


def _ada_kernel(cond_ref, w_ref, b_ref, o_ref):
    s = jax.nn.silu(cond_ref[...]).astype(BF16)
    o_ref[...] = _dot(s, w_ref[...].astype(BF16)) + b_ref[...]


def _ada_call(cond8, w_ada, b_ada):
    tn = 2048
    nj = 6 * D_MODEL // tn
    return pl.pallas_call(
        _ada_kernel,
        out_shape=jax.ShapeDtypeStruct((DEPTH, COND_ROWS, 6 * D_MODEL), F32),
        grid=(DEPTH, nj),
        in_specs=[
            pl.BlockSpec((COND_ROWS, D_MODEL), lambda l, j: (0, 0)),
            pl.BlockSpec((None, D_MODEL, tn), lambda l, j: (l, 0, j)),
            pl.BlockSpec((None, 1, tn), lambda l, j: (l, 0, j)),
        ],
        out_specs=pl.BlockSpec((None, COND_ROWS, tn), lambda l, j: (l, 0, j)),
        compiler_params=pltpu.CompilerParams(vmem_limit_bytes=VMEM_LIMIT),
        name="ada_mod",
    )(cond8, w_ada, b_ada.reshape(DEPTH, 1, 6 * D_MODEL))


def _inproj_kernel(x_ref, gn_ref, m_ref, w_ref, wglr_ref, wg_ref, bg_ref, z_ref, g_ref):
    h = _norm_mod(x_ref[...], gn_ref[...], m_ref[:, 0:D_MODEL], m_ref[:, D_MODEL:2 * D_MODEL]).astype(BF16)
    z_ref[...] = _dot(h, w_ref[...])
    glr = _dot(h, wglr_ref[...]).astype(BF16)
    pre = _dot(glr, wg_ref[...]) + bg_ref[...]
    g_ref[...] = jax.nn.log_sigmoid(pre) * (1.0 / GLA_TAU)


def _inproj_call(x, gn, mods, layer, w_main, w_glr, w_gate, b_gate):
    tm = 512
    nz = w_main.shape[1]
    return pl.pallas_call(
        _inproj_kernel,
        out_shape=(jax.ShapeDtypeStruct((T_TOK, nz), F32),
                   jax.ShapeDtypeStruct((T_TOK, 2 * GLA_QK), F32)),
        grid=(T_TOK // tm,),
        in_specs=[
            pl.BlockSpec((tm, D_MODEL), lambda i: (i, 0)),
            pl.BlockSpec((1, D_MODEL), lambda i: (0, 0)),
            pl.BlockSpec((None, 1, 6 * D_MODEL), lambda i: (layer * COND_ROWS + _cond_row(i, tm), 0, 0)),
            pl.BlockSpec((D_MODEL, nz), lambda i: (0, 0)),
            pl.BlockSpec((D_MODEL, 128), lambda i: (0, 0)),
            pl.BlockSpec((128, 2 * GLA_QK), lambda i: (0, 0)),
            pl.BlockSpec((1, 2 * GLA_QK), lambda i: (0, 0)),
        ],
        out_specs=(pl.BlockSpec((tm, nz), lambda i: (i, 0)),
                   pl.BlockSpec((tm, 2 * GLA_QK), lambda i: (i, 0))),
        compiler_params=pltpu.CompilerParams(vmem_limit_bytes=VMEM_LIMIT),
        name="even_inproj",
    )(x, gn, mods, w_main, w_glr, w_gate, b_gate)


def _s5_prep_kernel(lre_ref, lim_ref, ldt_ref, btre_ref, btim_ref, cre_ref, cim_ref, ccf_ref, ccb_ref,
                    t_ref, bq_ref, cqt_ref, be_ref, a_ref, t_scr):
    lre = lre_ref[...]
    lim = lim_ref[...]
    dt = jnp.exp(ldt_ref[...])
    a = lre * dt
    th = lim * dt

    def lam_pow(k):
        mag = jnp.exp(k * a)
        return mag * jnp.cos(k * th), mag * jnp.sin(k * th)

    lb_re, lb_im = lam_pow(1.0)
    nr = lb_re - 1.0
    den = lre * lre + lim * lim
    cf_re = (nr * lre + lb_im * lim) / den
    cf_im = (lb_im * lre - nr * lim) / den
    bt_re = btre_ref[...]
    bt_im = btim_ref[...]
    bb_re = jnp.tile(cf_re * bt_re - cf_im * bt_im, (S5_Q, 1))
    bb_im = jnp.tile(cf_re * bt_im + cf_im * bt_re, (S5_Q, 1))

    shape = (S5_W, 128)
    pos = lax.shift_right_logical(lax.broadcasted_iota(jnp.int32, shape, 0), 4)
    is_f = lax.broadcasted_iota(jnp.int32, shape, 1) < S5_STATE
    posf = pos.astype(F32)

    p_re, p_im = lam_pow(jnp.where(is_f, (S5_Q - 1.0) - posf, posf))
    w_re = p_re * bb_re - p_im * bb_im
    w_im = p_re * bb_im + p_im * bb_re
    bq = jnp.concatenate([w_re, w_im], axis=1)
    bq_ref[...] = bq.astype(BF16)

    edge = pos == jnp.where(is_f, 0, S5_Q - 1)
    be_ref[...] = jnp.concatenate([jnp.where(edge, bb_re, 0.0), jnp.where(edge, bb_im, 0.0)], axis=1).astype(BF16)

    q_re, q_im = lam_pow(jnp.where(is_f, posf + 1.0, S5_Q - posf))
    ct_re = jnp.tile(cre_ref[...], (S5_Q, 1))
    ct_im = jnp.tile(cim_ref[...], (S5_Q, 1))
    g_re = q_re * ct_re - q_im * ct_im
    g_im = q_re * ct_im + q_im * ct_re
    cqt_ref[...] = jnp.concatenate([g_re, -g_im], axis=1).astype(BF16)

    a_re, a_im = lam_pow(float(S5_Q))
    a_ref[...] = jnp.concatenate([a_re, a_im], axis=1)

    kf = lax.dot_general(bq, ccf_ref[...], NT_DIMS, precision=lax.Precision.HIGHEST, preferred_element_type=F32)
    kb = lax.dot_general(bq, ccb_ref[...], NT_DIMS, precision=lax.Precision.HIGHEST, preferred_element_type=F32)
    lo = S5_W - S5_GROUP_CH
    dd = jnp.concatenate([kf[0:lo], kf[lo:S5_W] + kb[0:S5_GROUP_CH], kb[S5_GROUP_CH:S5_W]], axis=0)
    for t in range(S5_Q):
        r0 = (S5_Q - 1 - t) * S5_GROUP_CH
        t_scr[:, t * S5_GROUP_CH:(t + 1) * S5_GROUP_CH] = dd[r0:r0 + S5_W]
    t_ref[...] = t_scr[...].astype(BF16)


def _s5_prep_call(lam_re, lam_im, log_dt, b_re, b_im, c_re, c_im):
    def fb(p):
        return jnp.transpose(p, (1, 0, 2)).reshape(S5_GROUPS, 1, 2 * S5_STATE)

    def dup(p):
        return jnp.concatenate([p, p], axis=-1)

    ldt = fb(jnp.broadcast_to(log_dt[:, :, None], (2, S5_GROUPS, S5_STATE)))
    bt_re = dup(jnp.transpose(b_re, (0, 2, 1)))
    bt_im = dup(jnp.transpose(b_im, (0, 2, 1)))
    zero = jnp.zeros_like(c_re)
    ccf = jnp.concatenate([c_re, zero, -c_im, zero], axis=-1)
    ccb = jnp.concatenate([zero, c_re, zero, -c_im], axis=-1)

    row = pl.BlockSpec((None, 1, 128), lambda g: (g, 0, 0))
    mat16 = pl.BlockSpec((None, S5_GROUP_CH, 128), lambda g: (g, 0, 0))
    mat16w = pl.BlockSpec((None, S5_GROUP_CH, S5_W), lambda g: (g, 0, 0))
    sq = pl.BlockSpec((None, S5_W, S5_W), lambda g: (g, 0, 0))
    sq_shape = jax.ShapeDtypeStruct((S5_GROUPS, S5_W, S5_W), BF16)
    return pl.pallas_call(
        _s5_prep_kernel,
        out_shape=(sq_shape, sq_shape, sq_shape, sq_shape,
                   jax.ShapeDtypeStruct((S5_GROUPS, 1, S5_W), F32)),
        grid=(S5_GROUPS,),
        in_specs=[row, row, row, mat16, mat16, mat16, mat16, mat16w, mat16w],
        out_specs=(sq, sq, sq, sq, pl.BlockSpec((None, 1, S5_W), lambda g: (g, 0, 0))),
        scratch_shapes=[pltpu.VMEM((S5_W, S5_W), F32)],
        name="s5_prep",
    )(fb(lam_re), fb(lam_im), ldt, bt_re, bt_im, dup(c_re), dup(c_im), ccf, ccb)


def _s5_kernel(u_ref, t_ref, bq_ref, cqt_ref, be_ref, a_ref, h0_ref, y_ref, ns_ref,
               x_scr, spf_scr, spb_scr, ne_scr):
    u = u_ref[...]
    x = _dot(u, bq_ref[...])
    ne = _dot(u[0:S5_PROMPT_ROWS], be_ref[...])
    for part in range(2):
        x_scr[part] = x[:, part * 128:(part + 1) * 128]
        ne_scr[part] = ne[:, part * 128:(part + 1) * 128]

    is_f = lax.broadcasted_iota(jnp.int32, (1, 128), 1) < S5_STATE
    a_re = a_ref[:, 0:128]
    a_im = a_ref[:, 128:256]

    def scan(base, nseq, nchunk, s_init):
        def body(i, s):
            s_re, s_im = s
            rows_f = pl.ds(base + i, nseq, stride=nchunk)
            rows_b = pl.ds(base + (nchunk - 1 - i), nseq, stride=nchunk)
            spf_scr[0, rows_f, :] = s_re
            spf_scr[1, rows_f, :] = s_im
            spb_scr[0, rows_b, :] = s_re
            spb_scr[1, rows_b, :] = s_im
            x_re = jnp.where(is_f, x_scr[0, rows_f, :], x_scr[0, rows_b, :])
            x_im = jnp.where(is_f, x_scr[1, rows_f, :], x_scr[1, rows_b, :])
            return (a_re * s_re - a_im * s_im + x_re, a_re * s_im + a_im * s_re + x_im)

        lax.fori_loop(0, nchunk, body, s_init)

    zero = jnp.zeros((BATCH, 128), F32)
    scan(0, BATCH, S5_PROMPT_CHUNKS, (zero, zero))
    scan(S5_PROMPT_ROWS, DEC_BATCH, S5_SAMPLE_CHUNKS, (h0_ref[:, 0:128], h0_ref[:, 128:256]))

    carried = jnp.concatenate([jnp.where(is_f, spf_scr[p], spb_scr[p]) for p in range(2)], axis=1).astype(BF16)
    y_ref[...] = _dot(u, t_ref[...]) + lax.dot_general(carried, cqt_ref[...], NT_DIMS, preferred_element_type=F32)

    first = pl.ds(0, BATCH, stride=S5_PROMPT_CHUNKS)
    last = pl.ds(S5_PROMPT_CHUNKS - 1, BATCH, stride=S5_PROMPT_CHUNKS)
    for part in range(2):
        ns_ref[:, part * 128:(part + 1) * 128] = jnp.where(is_f, ne_scr[part, first, :], ne_scr[part, last, :])


def _s5_call(u_r, mats, h0):
    t_m, bq_m, cqt_m, be_m, a_m = mats
    sq = pl.BlockSpec((None, S5_W, S5_W), lambda g: (g, 0, 0))
    return pl.pallas_call(
        _s5_kernel,
        out_shape=(jax.ShapeDtypeStruct((S5_GROUPS, S5_ROWS, S5_W), F32),
                   jax.ShapeDtypeStruct((S5_GROUPS, BATCH, S5_W), F32)),
        grid=(S5_GROUPS,),
        in_specs=[
            pl.BlockSpec((None, S5_ROWS, S5_W), lambda g: (g, 0, 0)),
            sq, sq, sq, sq,
            pl.BlockSpec((None, 1, S5_W), lambda g: (g, 0, 0)),
            pl.BlockSpec((None, DEC_BATCH, S5_W), lambda g: (g, 0, 0)),
        ],
        out_specs=(pl.BlockSpec((None, S5_ROWS, S5_W), lambda g: (g, 0, 0)),
                   pl.BlockSpec((None, BATCH, S5_W), lambda g: (g, 0, 0))),
        scratch_shapes=[pltpu.VMEM((2, S5_ROWS, 128), F32), pltpu.VMEM((2, S5_ROWS, 128), F32),
                        pltpu.VMEM((2, S5_ROWS, 128), F32), pltpu.VMEM((2, S5_PROMPT_ROWS, 128), F32)],
        name="s5_scan",
    )(u_r, t_m, bq_m, cqt_m, be_m, a_m, h0)


def _split_bf16(x):
    hi = x.astype(BF16)
    r1 = x - hi.astype(F32)
    mid = r1.astype(BF16)
    lo = (r1 - mid.astype(F32)).astype(BF16)
    return hi, mid, lo


def _gla_kernel(*refs, seq_len, has_s0):
    if has_s0:
        q_ref, k_ref, v_ref, gf_ref, gb_ref, r_ref, gn_ref, s0_ref, o_ref, sfin_ref, of_scr = refs
    else:
        q_ref, k_ref, v_ref, gf_ref, gb_ref, r_ref, gn_ref, o_ref, sfin_ref, of_scr = refs
        s0_ref = None
    nchunk = seq_len // GLA_CHUNK
    c_shape = (GLA_CHUNK, GLA_CHUNK)
    row = lax.broadcasted_iota(jnp.int32, c_shape, 0)
    col = lax.broadcasted_iota(jnp.int32, c_shape, 1)

    def run(direction, g_ref, emit):
        reverse = direction == 1
        keep = (row <= col) if reverse else (row >= col)
        tri = keep.astype(BF16)

        def body(i, s):
            c = (nchunk - 1 - i) if reverse else i
            sl = pl.ds(pl.multiple_of(c * GLA_CHUNK, GLA_CHUNK), GLA_CHUNK)
            b = sum(_dot(tri, part) for part in _split_bf16(g_ref[sl, :]))
            b_last = b[0:1] if reverse else b[GLA_CHUNK - 1:GLA_CHUNK]
            q = q_ref[sl, :] * (GLA_DK ** -0.5)
            k = k_ref[sl, :]
            v = v_ref[sl, :].astype(BF16)
            qd = (q * jnp.exp(b)).astype(BF16)
            kd = (k * jnp.exp(-b)).astype(BF16)
            k2 = (k * jnp.exp(b_last - b)).astype(BF16)
            dec = jnp.exp(jnp.transpose(jnp.broadcast_to(b_last, (GLA_DV, GLA_QK))))
            outs, new_s = [], []
            for h in range(GLA_HEADS):
                ks = slice(h * GLA_DK, (h + 1) * GLA_DK)
                vs = slice(h * GLA_DV, (h + 1) * GLA_DV)
                att = lax.dot_general(qd[:, ks], kd[:, ks], NT_DIMS, preferred_element_type=F32)
                att = jnp.where(keep, att, 0.0).astype(BF16)
                outs.append(_dot(qd[:, ks], s[ks].astype(BF16)) + _dot(att, v[:, vs]))
                kv = lax.dot_general(k2[:, ks], v[:, vs], TN_DIMS, preferred_element_type=F32)
                new_s.append(s[ks] * dec[ks] + kv)
            emit(sl, jnp.concatenate(outs, axis=1))
            return jnp.concatenate(new_s, axis=0)

        s_init = s0_ref[direction] if has_s0 else jnp.zeros((GLA_QK, GLA_DV), F32)
        sfin_ref[direction] = lax.fori_loop(0, nchunk, body, s_init)

    def emit_fwd(sl, o):
        of_scr[sl, :] = o

    def emit_bwd(sl, o):
        o = of_scr[sl, :] + o
        gate = jax.nn.silu(r_ref[sl, :])
        for h in range(GLA_HEADS):
            vs = slice(h * GLA_DV, (h + 1) * GLA_DV)
            oh = o[:, vs]
            oh = oh * lax.rsqrt(jnp.mean(oh * oh, axis=-1, keepdims=True) + EPS) * gn_ref[...]
            o_ref[sl, vs] = oh * gate[:, vs]

    run(0, gf_ref, emit_fwd)
    run(1, gb_ref, emit_bwd)


def _gla_call(z, g, gla_norm, s0, seq_len, nseq, row0):
    assert row0 % seq_len == 0
    r0 = row0 // seq_len
    has_s0 = s0 is not None
    qk_off = S5_WIDTH // GLA_QK
    v_off = (S5_WIDTH + 2 * GLA_QK) // GLA_VW
    in_specs = [
        pl.BlockSpec((seq_len, GLA_QK), lambda i: (r0 + i, qk_off)),
        pl.BlockSpec((seq_len, GLA_QK), lambda i: (r0 + i, qk_off + 1)),
        pl.BlockSpec((seq_len, GLA_VW), lambda i: (r0 + i, v_off)),
        pl.BlockSpec((seq_len, GLA_QK), lambda i: (r0 + i, 0)),
        pl.BlockSpec((seq_len, GLA_QK), lambda i: (r0 + i, 1)),
        pl.BlockSpec((seq_len, GLA_VW), lambda i: (r0 + i, v_off + 1)),
        pl.BlockSpec((1, GLA_DV), lambda i: (0, 0)),
    ]
    args = [z, z, z, g, g, z, gla_norm]
    if has_s0:
        in_specs.append(pl.BlockSpec((None, 2, GLA_QK, GLA_DV), lambda i: (i, 0, 0, 0)))
        args.append(s0)
    return pl.pallas_call(
        functools.partial(_gla_kernel, seq_len=seq_len, has_s0=has_s0),
        out_shape=(jax.ShapeDtypeStruct((nseq * seq_len, GLA_VW), F32),
                   jax.ShapeDtypeStruct((nseq, 2, GLA_QK, GLA_DV), F32)),
        grid=(nseq,),
        in_specs=in_specs,
        out_specs=(pl.BlockSpec((seq_len, GLA_VW), lambda i: (i, 0)),
                   pl.BlockSpec((None, 2, GLA_QK, GLA_DV), lambda i: (i, 0, 0, 0))),
        scratch_shapes=[pltpu.VMEM((seq_len, GLA_VW), F32)],
        compiler_params=pltpu.CompilerParams(vmem_limit_bytes=VMEM_LIMIT),
        name=f"gla_len{seq_len}",
    )(*args)


def _mlp_tail(x, mix, m_ref, gn2_ref, w1_ref, w2_ref, o_ref):
    y1 = x + m_ref[:, 2 * D_MODEL:3 * D_MODEL] * mix
    h = _norm_mod(y1, gn2_ref[...], m_ref[:, 3 * D_MODEL:4 * D_MODEL], m_ref[:, 4 * D_MODEL:5 * D_MODEL]).astype(BF16)
    tf = 512
    acc = jnp.zeros(y1.shape, F32)
    for c in range(D_FF // tf):
        a = _dot(h, w1_ref[:, c * tf:(c + 1) * tf])
        a = jnp.square(jnp.maximum(a, 0.0)).astype(BF16)
        acc = acc + _dot(a, w2_ref[c * tf:(c + 1) * tf, :])
    o_ref[...] = y1 + m_ref[:, 5 * D_MODEL:6 * D_MODEL] * acc


def _even_out_kernel(x_ref, y5_ref, u_ref, dskip_ref, wglu_ref, bglu_ref, gla_ref, wout_ref,
                     m_ref, gn2_ref, w1_ref, w2_ref, o_ref):
    ys = y5_ref[...] + u_ref[...] * dskip_ref[...]
    gl = jax.nn.gelu(ys)
    s5o = gl * jax.nn.sigmoid(_dot(gl.astype(BF16), wglu_ref[...]) + bglu_ref[...])
    mix = _dot(s5o.astype(BF16), wout_ref[0:S5_WIDTH, :]) + _dot(gla_ref[...].astype(BF16), wout_ref[S5_WIDTH:, :])
    _mlp_tail(x_ref[...], mix, m_ref, gn2_ref, w1_ref, w2_ref, o_ref)


def _odd_out_kernel(x_ref, att_ref, wo_ref, m_ref, gn2_ref, w1_ref, w2_ref, o_ref):
    mix = _dot(att_ref[...], wo_ref[...])
    _mlp_tail(x_ref[...], mix, m_ref, gn2_ref, w1_ref, w2_ref, o_ref)


_OUT_TM = 512


def _const_spec(shape):
    return pl.BlockSpec(shape, lambda i: (0,) * len(shape), pipeline_mode=pl.Buffered(1))


def _tail_specs(layer):
    tm = _OUT_TM
    return [
        pl.BlockSpec((None, 1, 6 * D_MODEL), lambda i: (layer * COND_ROWS + _cond_row(i, tm), 0, 0)),
        _const_spec((1, D_MODEL)),
        _const_spec((D_MODEL, D_FF)),
        _const_spec((D_FF, D_MODEL)),
    ]


def _even_out_call(x, y5, z, d_skip, w_glu, b_glu, gla, w_out, mods, layer, gn2, w1, w2):
    tm = _OUT_TM
    return pl.pallas_call(
        _even_out_kernel,
        out_shape=jax.ShapeDtypeStruct((T_TOK, D_MODEL), F32),
        grid=(T_TOK // tm,),
        in_specs=[
            pl.BlockSpec((tm, D_MODEL), lambda i: (i, 0)),
            pl.BlockSpec((tm, S5_WIDTH), lambda i: (i, 0)),
            pl.BlockSpec((tm, S5_WIDTH), lambda i: (i, 0)),
            _const_spec((1, S5_WIDTH)),
            _const_spec((S5_WIDTH, S5_WIDTH)),
            _const_spec((1, S5_WIDTH)),
            pl.BlockSpec((tm, GLA_VW), lambda i: (i, 0)),
            _const_spec((S5_WIDTH + GLA_VW, D_MODEL)),
        ] + _tail_specs(layer),
        out_specs=pl.BlockSpec((tm, D_MODEL), lambda i: (i, 0)),
        compiler_params=pltpu.CompilerParams(vmem_limit_bytes=VMEM_LIMIT),
        name="even_out_mlp",
    )(x, y5, z, d_skip, w_glu, b_glu, gla, w_out, mods, gn2, w1, w2)


def _odd_out_call(x, att, w_o, mods, layer, gn2, w1, w2):
    tm = _OUT_TM
    return pl.pallas_call(
        _odd_out_kernel,
        out_shape=jax.ShapeDtypeStruct((T_TOK, D_MODEL), F32),
        grid=(T_TOK // tm,),
        in_specs=[
            pl.BlockSpec((tm, D_MODEL), lambda i: (i, 0)),
            pl.BlockSpec((tm, D_MODEL), lambda i: (i, 0)),
            _const_spec((D_MODEL, D_MODEL)),
        ] + _tail_specs(layer),
        out_specs=pl.BlockSpec((tm, D_MODEL), lambda i: (i, 0)),
        compiler_params=pltpu.CompilerParams(vmem_limit_bytes=VMEM_LIMIT),
        name="odd_out_mlp",
    )(x, att, w_o, mods, gn2, w1, w2)


def _qkv_kernel(x_ref, gn_ref, m_ref, w_ref, qn_ref, kn_ref, cos_ref, sin_ref, q_ref, k_ref, v_ref, *, tile):
    h = _norm_mod(x_ref[...], gn_ref[...], m_ref[:, 0:D_MODEL], m_ref[:, D_MODEL:2 * D_MODEL]).astype(BF16)
    z = _dot(h, w_ref[...])
    v_ref[...] = z[:, (N_HEADS + KV_HEADS) * HEAD_DIM:]
    even_lane = (lax.broadcasted_iota(jnp.int32, (1, HEAD_DIM), 1) & 1) == 0

    def heads(rope):
        for hd in range(N_HEADS + KV_HEADS):
            xh = z[:, hd * HEAD_DIM:(hd + 1) * HEAD_DIM]
            gain = qn_ref[...] if hd < N_HEADS else kn_ref[...]
            xh = xh * lax.rsqrt(jnp.mean(xh * xh, axis=-1, keepdims=True) + EPS) * gain
            if rope:
                partner = jnp.where(even_lane, pltpu.roll(xh, HEAD_DIM - 1, 1), pltpu.roll(xh, 1, 1))
                xh = xh * cos_ref[...] + partner * sin_ref[...]
            if hd < N_HEADS:
                q_ref[:, hd * HEAD_DIM:(hd + 1) * HEAD_DIM] = xh.astype(BF16)
            else:
                k_ref[:, (hd - N_HEADS) * HEAD_DIM:(hd - N_HEADS + 1) * HEAD_DIM] = xh

    is_sample = pl.program_id(0) >= T_PROMPT // tile

    @pl.when(is_sample)
    def _():
        heads(True)

    @pl.when(jnp.logical_not(is_sample))
    def _():
        heads(False)


def _qkv_call(x, gn, mods, layer, w_qkv, q_norm, k_norm, cos_t, sin_t):
    tm = 512
    pos_tiles = DEC_SEQ // tm
    first_sample = T_PROMPT // tm

    def pos_map(i):
        return (jnp.maximum(i - first_sample, 0) % pos_tiles, 0)

    return pl.pallas_call(
        functools.partial(_qkv_kernel, tile=tm),
        out_shape=(jax.ShapeDtypeStruct((T_TOK, N_HEADS * HEAD_DIM), BF16),
                   jax.ShapeDtypeStruct((T_TOK, KV_HEADS * HEAD_DIM), F32),
                   jax.ShapeDtypeStruct((T_TOK, KV_HEADS * HEAD_DIM), F32)),
        grid=(T_TOK // tm,),
        in_specs=[
            pl.BlockSpec((tm, D_MODEL), lambda i: (i, 0)),
            pl.BlockSpec((1, D_MODEL), lambda i: (0, 0)),
            pl.BlockSpec((None, 1, 6 * D_MODEL), lambda i: (layer * COND_ROWS + _cond_row(i, tm), 0, 0)),
            pl.BlockSpec(w_qkv.shape, lambda i: (0, 0)),
            pl.BlockSpec((1, HEAD_DIM), lambda i: (0, 0)),
            pl.BlockSpec((1, HEAD_DIM), lambda i: (0, 0)),
            pl.BlockSpec((tm, HEAD_DIM), pos_map),
            pl.BlockSpec((tm, HEAD_DIM), pos_map),
        ],
        out_specs=(pl.BlockSpec((tm, N_HEADS * HEAD_DIM), lambda i: (i, 0)),
                   pl.BlockSpec((tm, KV_HEADS * HEAD_DIM), lambda i: (i, 0)),
                   pl.BlockSpec((tm, KV_HEADS * HEAD_DIM), lambda i: (i, 0))),
        compiler_params=pltpu.CompilerParams(vmem_limit_bytes=VMEM_LIMIT),
        name="odd_qkv",
    )(x, gn, mods, w_qkv, q_norm, k_norm, cos_t, sin_t)


def _rope_tables():
    rows = DEC_SEQ // GRID_W
    row = jnp.repeat(jnp.arange(rows, dtype=F32), GRID_W)
    col = jnp.tile(jnp.arange(GRID_W, dtype=F32), rows)
    inv = ROPE_THETA ** (-jnp.arange(0, AXIS_DIM, 2, dtype=F32) / AXIS_DIM)
    ang = jnp.concatenate([row[:, None] * inv, col[:, None] * inv], axis=-1)
    cos_t = jnp.repeat(jnp.cos(ang), 2, axis=-1)
    sin = jnp.sin(ang)
    sin_t = jnp.stack([-sin, sin], axis=-1).reshape(DEC_SEQ, HEAD_DIM)
    return cos_t, sin_t


def _attn_kernel(*refs, has_cache):
    if has_cache:
        q_ref, k_ref, v_ref, ck_ref, cv_ref, o_ref = refs
    else:
        q_ref, k_ref, v_ref, o_ref = refs
    scale = HEAD_DIM ** -0.5
    k = k_ref[...].astype(BF16)
    v = v_ref[...].astype(BF16)
    if has_cache:
        ck = ck_ref[...].astype(BF16)
        cv = cv_ref[...].astype(BF16)
    for r in range(Q_PER_KV):
        cs = slice(r * HEAD_DIM, (r + 1) * HEAD_DIM)
        q = q_ref[:, cs]
        s = lax.dot_general(q, k, NT_DIMS, preferred_element_type=F32) * scale
        m = jnp.max(s, axis=-1, keepdims=True)
        if has_cache:
            sc = lax.dot_general(q, ck, NT_DIMS, preferred_element_type=F32) * scale
            m = jnp.maximum(m, jnp.max(sc, axis=-1, keepdims=True))
        p = jnp.exp(s - m)
        den = jnp.sum(p, axis=-1, keepdims=True)
        o = _dot(p.astype(BF16), v)
        if has_cache:
            pc = jnp.exp(sc - m)
            den = den + jnp.sum(pc, axis=-1, keepdims=True)
            o = o + _dot(pc.astype(BF16), cv)
        o_ref[:, cs] = (o / den).astype(BF16)


def _attn_call(q, k, v, cache_k, cache_v, seq_len, nseq, row0, tq):
    assert row0 % seq_len == 0 and seq_len % tq == 0
    has_cache = cache_k is not None
    kv0 = row0 // seq_len
    q0 = row0 // tq
    nq = seq_len // tq
    gw = Q_PER_KV * HEAD_DIM
    in_specs = [
        pl.BlockSpec((tq, gw), lambda b, g, i: (q0 + b * nq + i, g)),
        pl.BlockSpec((seq_len, HEAD_DIM), lambda b, g, i: (kv0 + b, g)),
        pl.BlockSpec((seq_len, HEAD_DIM), lambda b, g, i: (kv0 + b, g)),
    ]
    args = [q, k, v]
    if has_cache:
        in_specs += [pl.BlockSpec((PAST_LEN, HEAD_DIM), lambda b, g, i: (b, g)),
                     pl.BlockSpec((PAST_LEN, HEAD_DIM), lambda b, g, i: (b, g))]
        args += [cache_k, cache_v]
    return pl.pallas_call(
        functools.partial(_attn_kernel, has_cache=has_cache),
        out_shape=jax.ShapeDtypeStruct((nseq * seq_len, N_HEADS * HEAD_DIM), BF16),
        grid=(nseq, KV_HEADS, nq),
        in_specs=in_specs,
        out_specs=pl.BlockSpec((tq, gw), lambda b, g, i: (b * nq + i, g)),
        compiler_params=pltpu.CompilerParams(vmem_limit_bytes=VMEM_LIMIT),
        name=f"attn_len{seq_len}",
    )(*args)


def kernel(x_prompt, x_sample, state_s5_re, state_s5_im, state_gla, cache_k, cache_v, c, c_ctx, norm_mix, norm_mlp, w_ada, b_ada, w_mlp_in, w_mlp_out, w_in_e, w_out_e, s5_lambda_re, s5_lambda_im, s5_log_dt, s5_b_re, s5_b_im, s5_c_re, s5_c_im, s5_d, s5_w_glu, s5_b_glu, gla_w_gate2, gla_b_gate, gla_norm, w_qkv_o, w_o_o, q_norm, k_norm):
    x = jnp.concatenate([x_prompt.reshape(T_PROMPT, D_MODEL), x_sample.reshape(T_SAMPLE, D_MODEL)], axis=0)
    cond8 = jnp.concatenate([c_ctx[None, :], c, jnp.zeros((COND_ROWS - 1 - DEC_BATCH, D_MODEL), F32)], axis=0)
    mods = _ada_call(cond8, w_ada, b_ada).reshape(DEPTH * COND_ROWS, 1, 6 * D_MODEL)

    n_main = S5_WIDTH + 2 * GLA_QK + 2 * GLA_VW
    w_in = w_in_e[0]
    w_main = w_in[:, :n_main].astype(BF16)
    w_glr = jnp.pad(w_in[:, n_main:], ((0, 0), (0, 128 - 2 * GLA_RANK))).astype(BF16)
    w_gate = jnp.zeros((128, 2 * GLA_QK), F32)
    w_gate = w_gate.at[0:GLA_RANK, 0:GLA_QK].set(gla_w_gate2[0, 0])
    w_gate = w_gate.at[GLA_RANK:2 * GLA_RANK, GLA_QK:].set(gla_w_gate2[0, 1]).astype(BF16)
    b_gate = gla_b_gate[0].reshape(1, 2 * GLA_QK)
    z, g = _inproj_call(x, norm_mix[0:1], mods, 0, w_main, w_glr, w_gate, b_gate)

    mats = _s5_prep_call(s5_lambda_re[0], s5_lambda_im[0], s5_log_dt[0], s5_b_re[0], s5_b_im[0],
                         s5_c_re[0], s5_c_im[0])
    u_r = z[:, :S5_WIDTH].astype(BF16).reshape(S5_ROWS, S5_Q, S5_GROUPS, S5_GROUP_CH)
    u_r = jnp.transpose(u_r, (2, 0, 1, 3)).reshape(S5_GROUPS, S5_ROWS, S5_W)

    def state_rows(s):
        return jnp.transpose(s, (2, 0, 1, 3)).reshape(S5_GROUPS, DEC_BATCH, 2 * S5_STATE)

    h0 = jnp.concatenate([state_rows(state_s5_re[:, 0]), state_rows(state_s5_im[:, 0])], axis=-1)
    y5_r, ns = _s5_call(u_r, mats, h0)
    y5 = jnp.transpose(y5_r.reshape(S5_GROUPS, S5_ROWS, S5_Q, S5_GROUP_CH), (1, 2, 0, 3)).reshape(T_TOK, S5_WIDTH)

    def state_out(n):
        return jnp.transpose(n.reshape(S5_GROUPS, BATCH, 2, S5_STATE), (1, 2, 0, 3))[:, None]

    new_s5_re = state_out(ns[:, :, :2 * S5_STATE])
    new_s5_im = state_out(ns[:, :, 2 * S5_STATE:])

    gn_gla = gla_norm[0].reshape(1, GLA_DV)
    gla_p, sfin = _gla_call(z, g, gn_gla, None, SEQ, BATCH, 0)
    s0 = state_gla[:, 0].reshape(DEC_BATCH, 2, GLA_QK, GLA_DV)
    gla_s, _ = _gla_call(z, g, gn_gla, s0, DEC_SEQ, DEC_BATCH, T_PROMPT)
    gla = jnp.concatenate([gla_p, gla_s], axis=0)
    new_gla = sfin.reshape(BATCH, 1, 2, GLA_HEADS, GLA_DK, GLA_DV)

    x = _even_out_call(x, y5, z, s5_d[0].reshape(1, S5_WIDTH), s5_w_glu[0].astype(BF16),
                       s5_b_glu[0].reshape(1, S5_WIDTH), gla, w_out_e[0].astype(BF16), mods, 0,
                       norm_mlp[0:1], w_mlp_in[0].astype(BF16), w_mlp_out[0].astype(BF16))

    cos_t, sin_t = _rope_tables()
    q, k, v = _qkv_call(x, norm_mix[1:2], mods, 1, w_qkv_o[0].astype(BF16), q_norm[0].reshape(1, HEAD_DIM),
                        k_norm[0].reshape(1, HEAD_DIM), cos_t, sin_t)
    att_p = _attn_call(q, k, v, None, None, SEQ, BATCH, 0, SEQ)
    ck = cache_k[:, 0].reshape(DEC_BATCH * PAST_LEN, KV_HEADS * HEAD_DIM)
    cv = cache_v[:, 0].reshape(DEC_BATCH * PAST_LEN, KV_HEADS * HEAD_DIM)
    att_s = _attn_call(q, k, v, ck, cv, DEC_SEQ, DEC_BATCH, T_PROMPT, 256)
    att = jnp.concatenate([att_p, att_s], axis=0)
    x = _odd_out_call(x, att, w_o_o[0].astype(BF16), mods, 1, norm_mlp[1:2],
                      w_mlp_in[1].astype(BF16), w_mlp_out[1].astype(BF16))

    new_k = k[:T_PROMPT].reshape(BATCH, 1, SEQ, KV_HEADS, HEAD_DIM)
    new_v = v[:T_PROMPT].reshape(BATCH, 1, SEQ, KV_HEADS, HEAD_DIM)
    y_prompt = x[:T_PROMPT].reshape(BATCH, SEQ, D_MODEL)
    y_sample = x[T_PROMPT:].reshape(DEC_BATCH, DEC_SEQ, D_MODEL)
    return (y_prompt, y_sample, new_s5_re, new_s5_im, new_gla, new_k, new_v)
```

```python
import functools
import math

import jax
import jax.numpy as jnp
import numpy as np
from jax import lax
from jax.experimental import pallas as pl
from jax.experimental.pallas import tpu as pltpu

F32 = jnp.float32
BF16 = jnp.bfloat16

D_MODEL = 1024
BATCH = 16
SEQ = 256
DEPTH = 2
DEC_BATCH = 4
DEC_SEQ = 1024
PAST_LEN = 512
GRID_W = 64
S5_WIDTH = 512
S5_GROUP_CH = 16
S5_GROUPS = 32
S5_STATE = 64
GLA_HEADS = 4
GLA_VW = 512
GLA_DV = 128
GLA_DK = 64
GLA_QK = 256
GLA_RANK = 16
GLA_TAU = 16.0
GLA_CHUNK = 64
HEAD_DIM = 128
N_HEADS = 8
KV_HEADS = 2
Q_PER_KV = N_HEADS // KV_HEADS
AXIS_DIM = 64
ROPE_THETA = 10000.0
D_FF = 4096
EPS = 1e-6

T_PROMPT = BATCH * SEQ
T_SAMPLE = DEC_BATCH * DEC_SEQ
T_TOK = T_PROMPT + T_SAMPLE
COND_ROWS = 8
COND_SPAN = 1024
PROMPT_SPANS = T_PROMPT // COND_SPAN

S5_Q = 16
S5_W = S5_Q * S5_GROUP_CH
S5_GPB = 128 // S5_GROUP_CH
S5_ROWS = T_TOK // S5_Q
S5_PROMPT_ROWS = T_PROMPT // S5_Q
S5_PROMPT_CHUNKS = SEQ // S5_Q
S5_SAMPLE_CHUNKS = DEC_SEQ // S5_Q

VMEM_LIMIT = 56 * 1024 * 1024

NT_DIMS = (((1,), (1,)), ((), ()))
TN_DIMS = (((0,), (0,)), ((), ()))


def _cond_row(i, tile):
    return jnp.maximum((i * tile) // COND_SPAN - (PROMPT_SPANS - 1), 0)


def _norm_mod(x, gain, shift, scale):
    y = x * lax.rsqrt(jnp.mean(x * x, axis=-1, keepdims=True) + EPS)
    return (y * gain) * (1.0 + scale) + shift


def _dot(a, b):
    return jnp.dot(a, b, preferred_element_type=F32)


def _ada_kernel(cond_ref, w_ref, b_ref, o_ref):
    s = jax.nn.silu(cond_ref[...]).astype(BF16)
    o_ref[...] = _dot(s, w_ref[...].astype(BF16)) + b_ref[...]


def _ada_call(cond8, w_ada, b_ada):
    tn = 2048
    nj = 6 * D_MODEL // tn
    return pl.pallas_call(
        _ada_kernel,
        out_shape=jax.ShapeDtypeStruct((DEPTH, COND_ROWS, 6 * D_MODEL), F32),
        grid=(DEPTH, nj),
        in_specs=[
            pl.BlockSpec((COND_ROWS, D_MODEL), lambda l, j: (0, 0)),
            pl.BlockSpec((None, D_MODEL, tn), lambda l, j: (l, 0, j)),
            pl.BlockSpec((None, 1, tn), lambda l, j: (l, 0, j)),
        ],
        out_specs=pl.BlockSpec((None, COND_ROWS, tn), lambda l, j: (l, 0, j)),
        compiler_params=pltpu.CompilerParams(vmem_limit_bytes=VMEM_LIMIT),
        name="ada_mod",
    )(cond8, w_ada, b_ada.reshape(DEPTH, 1, 6 * D_MODEL))


def _inproj_kernel(x_ref, gn_ref, m_ref, w_ref, wglr_ref, wg_ref, bg_ref, z_ref, g_ref):
    h = _norm_mod(x_ref[...], gn_ref[...], m_ref[:, 0:D_MODEL], m_ref[:, D_MODEL:2 * D_MODEL]).astype(BF16)
    z_ref[...] = _dot(h, w_ref[...])
    glr = _dot(h, wglr_ref[...]).astype(BF16)
    pre = _dot(glr, wg_ref[...]) + bg_ref[...]
    g_ref[...] = jax.nn.log_sigmoid(pre) * (1.0 / GLA_TAU)


def _inproj_call(x, gn, mods, layer, w_main, w_glr, w_gate, b_gate):
    tm = 512
    nz = w_main.shape[1]
    return pl.pallas_call(
        _inproj_kernel,
        out_shape=(jax.ShapeDtypeStruct((T_TOK, nz), F32),
                   jax.ShapeDtypeStruct((T_TOK, 2 * GLA_QK), F32)),
        grid=(T_TOK // tm,),
        in_specs=[
            pl.BlockSpec((tm, D_MODEL), lambda i: (i, 0)),
            pl.BlockSpec((1, D_MODEL), lambda i: (0, 0)),
            pl.BlockSpec((None, 1, 6 * D_MODEL), lambda i: (layer * COND_ROWS + _cond_row(i, tm), 0, 0)),
            pl.BlockSpec((D_MODEL, nz), lambda i: (0, 0)),
            pl.BlockSpec((D_MODEL, 128), lambda i: (0, 0)),
            pl.BlockSpec((128, 2 * GLA_QK), lambda i: (0, 0)),
            pl.BlockSpec((1, 2 * GLA_QK), lambda i: (0, 0)),
        ],
        out_specs=(pl.BlockSpec((tm, nz), lambda i: (i, 0)),
                   pl.BlockSpec((tm, 2 * GLA_QK), lambda i: (i, 0))),
        compiler_params=pltpu.CompilerParams(vmem_limit_bytes=VMEM_LIMIT),
        name="even_inproj",
    )(x, gn, mods, w_main, w_glr, w_gate, b_gate)


def _s5_prep_kernel(lre_ref, lim_ref, ldt_ref, btre_ref, btim_ref, cre_ref, cim_ref, ccf_ref, ccb_ref,
                    t_ref, bq_ref, cqt_ref, be_ref, a_ref, t_scr):
    lre = lre_ref[...]
    lim = lim_ref[...]
    dt = jnp.exp(ldt_ref[...])
    a = lre * dt
    th = lim * dt

    def lam_pow(k):
        mag = jnp.exp(k * a)
        return mag * jnp.cos(k * th), mag * jnp.sin(k * th)

    lb_re, lb_im = lam_pow(1.0)
    nr = lb_re - 1.0
    den = lre * lre + lim * lim
    cf_re = (nr * lre + lb_im * lim) / den
    cf_im = (lb_im * lre - nr * lim) / den
    bt_re = btre_ref[...]
    bt_im = btim_ref[...]
    bb_re = jnp.tile(cf_re * bt_re - cf_im * bt_im, (S5_Q, 1))
    bb_im = jnp.tile(cf_re * bt_im + cf_im * bt_re, (S5_Q, 1))

    shape = (S5_W, 128)
    pos = lax.shift_right_logical(lax.broadcasted_iota(jnp.int32, shape, 0), 4)
    is_f = lax.broadcasted_iota(jnp.int32, shape, 1) < S5_STATE
    posq = lax.broadcasted_iota(jnp.int32, (S5_Q, 128), 0).astype(F32)
    is_fq = lax.broadcasted_iota(jnp.int32, (S5_Q, 128), 1) < S5_STATE

    def per_channel(tbl):
        return jnp.broadcast_to(tbl[:, None, :], (S5_Q, S5_GROUP_CH, 128)).reshape(shape)

    p_re, p_im = map(per_channel, lam_pow(jnp.where(is_fq, (S5_Q - 1.0) - posq, posq)))
    w_re = p_re * bb_re - p_im * bb_im
    w_im = p_re * bb_im + p_im * bb_re
    bq = jnp.concatenate([w_re, w_im], axis=1)
    bq_ref[...] = bq.astype(BF16)

    edge = pos == jnp.where(is_f, 0, S5_Q - 1)
    be_ref[...] = jnp.concatenate([jnp.where(edge, bb_re, 0.0), jnp.where(edge, bb_im, 0.0)], axis=1).astype(BF16)

    q_re, q_im = map(per_channel, lam_pow(jnp.where(is_fq, posq + 1.0, S5_Q - posq)))
    ct_re = jnp.tile(cre_ref[...], (S5_Q, 1))
    ct_im = jnp.tile(cim_ref[...], (S5_Q, 1))
    g_re = q_re * ct_re - q_im * ct_im
    g_im = q_re * ct_im + q_im * ct_re
    cqt_ref[...] = jnp.concatenate([g_re, -g_im], axis=1).astype(BF16)

    a_re, a_im = lam_pow(float(S5_Q))
    a_ref[...] = jnp.concatenate([a_re, a_im], axis=1)

    kf = lax.dot_general(bq, ccf_ref[...], NT_DIMS, precision=lax.Precision.HIGHEST, preferred_element_type=F32)
    kb = lax.dot_general(bq, ccb_ref[...], NT_DIMS, precision=lax.Precision.HIGHEST, preferred_element_type=F32)
    lo = S5_W - S5_GROUP_CH
    dd = jnp.concatenate([kf[0:lo], kf[lo:S5_W] + kb[0:S5_GROUP_CH], kb[S5_GROUP_CH:S5_W]], axis=0)
    for t in range(S5_Q):
        r0 = (S5_Q - 1 - t) * S5_GROUP_CH
        t_scr[:, t * S5_GROUP_CH:(t + 1) * S5_GROUP_CH] = dd[r0:r0 + S5_W]
    t_ref[...] = t_scr[...].astype(BF16)


def _s5_prep_call(lam_re, lam_im, log_dt, b_re, b_im, c_re, c_im):
    def fb(p):
        return jnp.transpose(p, (1, 0, 2)).reshape(S5_GROUPS, 1, 2 * S5_STATE)

    def dup(p):
        return jnp.concatenate([p, p], axis=-1)

    ldt = fb(jnp.broadcast_to(log_dt[:, :, None], (2, S5_GROUPS, S5_STATE)))
    bt_re = dup(jnp.transpose(b_re, (0, 2, 1)))
    bt_im = dup(jnp.transpose(b_im, (0, 2, 1)))
    zero = jnp.zeros_like(c_re)
    ccf = jnp.concatenate([c_re, zero, -c_im, zero], axis=-1)
    ccb = jnp.concatenate([zero, c_re, zero, -c_im], axis=-1)

    row = pl.BlockSpec((None, 1, 128), lambda g: (g, 0, 0))
    mat16 = pl.BlockSpec((None, S5_GROUP_CH, 128), lambda g: (g, 0, 0))
    mat16w = pl.BlockSpec((None, S5_GROUP_CH, S5_W), lambda g: (g, 0, 0))
    sq = pl.BlockSpec((None, S5_W, S5_W), lambda g: (g, 0, 0))
    sq_shape = jax.ShapeDtypeStruct((S5_GROUPS, S5_W, S5_W), BF16)
    return pl.pallas_call(
        _s5_prep_kernel,
        out_shape=(sq_shape, sq_shape, sq_shape, sq_shape,
                   jax.ShapeDtypeStruct((S5_GROUPS, 1, S5_W), F32)),
        grid=(S5_GROUPS,),
        in_specs=[row, row, row, mat16, mat16, mat16, mat16, mat16w, mat16w],
        out_specs=(sq, sq, sq, sq, pl.BlockSpec((None, 1, S5_W), lambda g: (g, 0, 0))),
        scratch_shapes=[pltpu.VMEM((S5_W, S5_W), F32)],
        name="s5_prep",
    )(fb(lam_re), fb(lam_im), ldt, bt_re, bt_im, dup(c_re), dup(c_im), ccf, ccb)


def _s5_kernel(u_ref, t_ref, bq_ref, cqt_ref, be_ref, a_ref, h0_ref, y_ref, ns_ref,
               ur_scr, x_scr, spf_scr, spb_scr, ne_scr, yt_scr):
    gch = S5_GROUP_CH
    for s in range(S5_Q):
        rows = u_ref[pl.ds(s, S5_ROWS, stride=S5_Q), :]
        for gl in range(S5_GPB):
            ur_scr[gl, :, s * gch:(s + 1) * gch] = rows[:, gl * gch:(gl + 1) * gch]

    for gl in range(S5_GPB):
        u = ur_scr[gl].astype(BF16)
        x = _dot(u, bq_ref[gl])
        ne = _dot(u[0:S5_PROMPT_ROWS], be_ref[gl])
        for part in range(2):
            x_scr[part, gl] = x[:, part * 128:(part + 1) * 128]
            ne_scr[part, gl] = ne[:, part * 128:(part + 1) * 128]

    is_f = lax.broadcasted_iota(jnp.int32, (1, 128), 1) < S5_STATE

    def scan(base, nseq, nchunk, s_init):
        def body(i, state):
            rows_f = pl.ds(base + i, nseq, stride=nchunk)
            rows_b = pl.ds(base + (nchunk - 1 - i), nseq, stride=nchunk)
            new = []
            for gl in range(S5_GPB):
                s_re, s_im = state[gl]
                a_re = a_ref[gl, :, 0:128]
                a_im = a_ref[gl, :, 128:256]
                spf_scr[0, gl, rows_f, :] = s_re
                spf_scr[1, gl, rows_f, :] = s_im
                spb_scr[0, gl, rows_b, :] = s_re
                spb_scr[1, gl, rows_b, :] = s_im
                x_re = jnp.where(is_f, x_scr[0, gl, rows_f, :], x_scr[0, gl, rows_b, :])
                x_im = jnp.where(is_f, x_scr[1, gl, rows_f, :], x_scr[1, gl, rows_b, :])
                new.append((a_re * s_re - a_im * s_im + x_re, a_re * s_im + a_im * s_re + x_im))
            return tuple(new)

        lax.fori_loop(0, nchunk, body, tuple(s_init))

    zero = jnp.zeros((BATCH, 128), F32)
    scan(0, BATCH, S5_PROMPT_CHUNKS, [(zero, zero)] * S5_GPB)
    scan(S5_PROMPT_ROWS, DEC_BATCH, S5_SAMPLE_CHUNKS,
         [(h0_ref[gl, :, 0:128], h0_ref[gl, :, 128:256]) for gl in range(S5_GPB)])

    first = pl.ds(0, BATCH, stride=S5_PROMPT_CHUNKS)
    last = pl.ds(S5_PROMPT_CHUNKS - 1, BATCH, stride=S5_PROMPT_CHUNKS)
    for gl in range(S5_GPB):
        u = ur_scr[gl].astype(BF16)
        carried = jnp.concatenate([jnp.where(is_f, spf_scr[p, gl], spb_scr[p, gl]) for p in range(2)], axis=1)
        y = _dot(u, t_ref[gl]) + lax.dot_general(carried.astype(BF16), cqt_ref[gl], NT_DIMS,
                                                 preferred_element_type=F32)
        for s in range(S5_Q):
            yt_scr[s, :, gl * gch:(gl + 1) * gch] = y[:, s * gch:(s + 1) * gch]
        for part in range(2):
            ns_ref[gl, :, part * 128:(part + 1) * 128] = jnp.where(
                is_f, ne_scr[part, gl, first, :], ne_scr[part, gl, last, :])
    for s in range(S5_Q):
        y_ref[pl.ds(s, S5_ROWS, stride=S5_Q), :] = yt_scr[s]


def _s5_call(z, mats, h0):
    t_m, bq_m, cqt_m, be_m, a_m = mats
    sq = pl.BlockSpec((S5_GPB, S5_W, S5_W), lambda g: (g, 0, 0))
    state_scr = pltpu.VMEM((2, S5_GPB, S5_ROWS, 128), F32)
    return pl.pallas_call(
        _s5_kernel,
        out_shape=(jax.ShapeDtypeStruct((T_TOK, S5_WIDTH), F32),
                   jax.ShapeDtypeStruct((S5_GROUPS, BATCH, S5_W), F32)),
        grid=(S5_GROUPS // S5_GPB,),
        in_specs=[
            pl.BlockSpec((T_TOK, 128), lambda g: (0, g)),
            sq, sq, sq, sq,
            pl.BlockSpec((S5_GPB, 1, S5_W), lambda g: (g, 0, 0)),
            pl.BlockSpec((S5_GPB, DEC_BATCH, S5_W), lambda g: (g, 0, 0)),
        ],
        out_specs=(pl.BlockSpec((T_TOK, 128), lambda g: (0, g)),
                   pl.BlockSpec((S5_GPB, BATCH, S5_W), lambda g: (g, 0, 0))),
        scratch_shapes=[pltpu.VMEM((S5_GPB, S5_ROWS, S5_W), F32), state_scr, state_scr, state_scr,
                        pltpu.VMEM((2, S5_GPB, S5_PROMPT_ROWS, 128), F32),
                        pltpu.VMEM((S5_Q, S5_ROWS, 128), F32)],
        compiler_params=pltpu.CompilerParams(vmem_limit_bytes=VMEM_LIMIT),
        name="s5_scan",
    )(z, t_m, bq_m, cqt_m, be_m, a_m, h0)


def _split_bf16(x):
    hi = x.astype(BF16)
    r1 = x - hi.astype(F32)
    mid = r1.astype(BF16)
    lo = (r1 - mid.astype(F32)).astype(BF16)
    return hi, mid, lo


def _gla_kernel(*refs, seq_len, has_s0):
    if has_s0:
        q_ref, k_ref, v_ref, gf_ref, gb_ref, r_ref, gn_ref, s0_ref, o_ref, sfin_ref, of_scr, ob_scr = refs
    else:
        q_ref, k_ref, v_ref, gf_ref, gb_ref, r_ref, gn_ref, o_ref, sfin_ref, of_scr, ob_scr = refs
        s0_ref = None
    nchunk = seq_len // GLA_CHUNK
    static_loop = nchunk <= 4
    c_shape = (GLA_CHUNK, GLA_CHUNK)
    row = lax.broadcasted_iota(jnp.int32, c_shape, 0)
    col = lax.broadcasted_iota(jnp.int32, c_shape, 1)

    def rows_of(c):
        if static_loop:
            return slice(c * GLA_CHUNK, (c + 1) * GLA_CHUNK)
        return pl.ds(pl.multiple_of(c * GLA_CHUNK, GLA_CHUNK), GLA_CHUNK)

    def chunk(direction, c, s):
        reverse = direction == 1
        g_ref, out_scr = (gb_ref, ob_scr) if reverse else (gf_ref, of_scr)
        keep = (row <= col) if reverse else (row >= col)
        tri = keep.astype(BF16)
        sl = rows_of(c)
        b = sum(_dot(tri, part) for part in _split_bf16(g_ref[sl, :]))
        b_last = b[0:1] if reverse else b[GLA_CHUNK - 1:GLA_CHUNK]
        q = q_ref[sl, :] * (GLA_DK ** -0.5)
        k = k_ref[sl, :]
        v = v_ref[sl, :].astype(BF16)
        qd = (q * jnp.exp(b)).astype(BF16)
        kd = (k * jnp.exp(-b)).astype(BF16)
        k2 = (k * jnp.exp(b_last - b)).astype(BF16)
        dec = jnp.exp(jnp.transpose(jnp.broadcast_to(b_last, (GLA_DV, GLA_QK))))
        new_s = []
        for h in range(GLA_HEADS):
            ks = slice(h * GLA_DK, (h + 1) * GLA_DK)
            vs = slice(h * GLA_DV, (h + 1) * GLA_DV)
            att = lax.dot_general(qd[:, ks], kd[:, ks], NT_DIMS, preferred_element_type=F32)
            att = jnp.where(keep, att, 0.0).astype(BF16)
            out_scr[sl, vs] = _dot(qd[:, ks], s[ks].astype(BF16)) + _dot(att, v[:, vs])
            kv = lax.dot_general(k2[:, ks], v[:, vs], TN_DIMS, preferred_element_type=F32)
            new_s.append(s[ks] * dec[ks] + kv)
        return jnp.concatenate(new_s, axis=0)

    def step(i, state):
        return chunk(0, i, state[0]), chunk(1, nchunk - 1 - i, state[1])

    def finish(c, carry):
        sl = rows_of(c)
        o = of_scr[sl, :] + ob_scr[sl, :]
        gate = jax.nn.silu(r_ref[sl, :])
        for h in range(GLA_HEADS):
            vs = slice(h * GLA_DV, (h + 1) * GLA_DV)
            oh = o[:, vs]
            oh = oh * lax.rsqrt(jnp.mean(oh * oh, axis=-1, keepdims=True) + EPS) * gn_ref[...]
            o_ref[sl, vs] = oh * gate[:, vs]
        return carry

    zero = jnp.zeros((GLA_QK, GLA_DV), F32)
    state = (s0_ref[0], s0_ref[1]) if has_s0 else (zero, zero)
    if static_loop:
        for i in range(nchunk):
            state = step(i, state)
        for c in range(nchunk):
            finish(c, 0)
    else:
        state = lax.fori_loop(0, nchunk, step, state, unroll=2)
        lax.fori_loop(0, nchunk, finish, 0, unroll=2)
    sfin_ref[0] = state[0]
    sfin_ref[1] = state[1]


def _gla_call(z, g, gla_norm, s0, seq_len, nseq, row0):
    assert row0 % seq_len == 0
    r0 = row0 // seq_len
    has_s0 = s0 is not None
    qk_off = S5_WIDTH // GLA_QK
    v_off = (S5_WIDTH + 2 * GLA_QK) // GLA_VW
    in_specs = [
        pl.BlockSpec((seq_len, GLA_QK), lambda i: (r0 + i, qk_off)),
        pl.BlockSpec((seq_len, GLA_QK), lambda i: (r0 + i, qk_off + 1)),
        pl.BlockSpec((seq_len, GLA_VW), lambda i: (r0 + i, v_off)),
        pl.BlockSpec((seq_len, GLA_QK), lambda i: (r0 + i, 0)),
        pl.BlockSpec((seq_len, GLA_QK), lambda i: (r0 + i, 1)),
        pl.BlockSpec((seq_len, GLA_VW), lambda i: (r0 + i, v_off + 1)),
        pl.BlockSpec((1, GLA_DV), lambda i: (0, 0)),
    ]
    args = [z, z, z, g, g, z, gla_norm]
    if has_s0:
        in_specs.append(pl.BlockSpec((None, 2, GLA_QK, GLA_DV), lambda i: (i, 0, 0, 0)))
        args.append(s0)
    return pl.pallas_call(
        functools.partial(_gla_kernel, seq_len=seq_len, has_s0=has_s0),
        out_shape=(jax.ShapeDtypeStruct((nseq * seq_len, GLA_VW), F32),
                   jax.ShapeDtypeStruct((nseq, 2, GLA_QK, GLA_DV), F32)),
        grid=(nseq,),
        in_specs=in_specs,
        out_specs=(pl.BlockSpec((seq_len, GLA_VW), lambda i: (i, 0)),
                   pl.BlockSpec((None, 2, GLA_QK, GLA_DV), lambda i: (i, 0, 0, 0))),
        scratch_shapes=[pltpu.VMEM((seq_len, GLA_VW), F32), pltpu.VMEM((seq_len, GLA_VW), F32)],
        compiler_params=pltpu.CompilerParams(vmem_limit_bytes=VMEM_LIMIT),
        name=f"gla_len{seq_len}",
    )(*args)


def _mlp_tail(x, mix, m_ref, gn2_ref, w1_ref, w2_ref, o_ref):
    y1 = x + m_ref[:, 2 * D_MODEL:3 * D_MODEL] * mix
    h = _norm_mod(y1, gn2_ref[...], m_ref[:, 3 * D_MODEL:4 * D_MODEL], m_ref[:, 4 * D_MODEL:5 * D_MODEL]).astype(BF16)
    tf = 512
    acc = jnp.zeros(y1.shape, F32)
    for c in range(D_FF // tf):
        a = _dot(h, w1_ref[:, c * tf:(c + 1) * tf])
        a = jnp.square(jnp.maximum(a, 0.0)).astype(BF16)
        acc = acc + _dot(a, w2_ref[c * tf:(c + 1) * tf, :])
    o_ref[...] = y1 + m_ref[:, 5 * D_MODEL:6 * D_MODEL] * acc


def _even_out_kernel(x_ref, y5_ref, u_ref, dskip_ref, wglu_ref, bglu_ref, gla_ref, wout_ref,
                     m_ref, gn2_ref, w1_ref, w2_ref, o_ref):
    ys = y5_ref[...] + u_ref[...] * dskip_ref[...]
    gl = jax.nn.gelu(ys)
    s5o = gl * jax.nn.sigmoid(_dot(gl.astype(BF16), wglu_ref[...]) + bglu_ref[...])
    mix = _dot(s5o.astype(BF16), wout_ref[0:S5_WIDTH, :]) + _dot(gla_ref[...].astype(BF16), wout_ref[S5_WIDTH:, :])
    _mlp_tail(x_ref[...], mix, m_ref, gn2_ref, w1_ref, w2_ref, o_ref)


def _odd_out_kernel(x_ref, att_ref, wo_ref, m_ref, gn2_ref, w1_ref, w2_ref, o_ref):
    mix = _dot(att_ref[...], wo_ref[...])
    _mlp_tail(x_ref[...], mix, m_ref, gn2_ref, w1_ref, w2_ref, o_ref)


_OUT_TM = 512


def _const_spec(shape):
    return pl.BlockSpec(shape, lambda i: (0,) * len(shape), pipeline_mode=pl.Buffered(1))


def _tail_specs(layer):
    tm = _OUT_TM
    return [
        pl.BlockSpec((None, 1, 6 * D_MODEL), lambda i: (layer * COND_ROWS + _cond_row(i, tm), 0, 0)),
        _const_spec((1, D_MODEL)),
        _const_spec((D_MODEL, D_FF)),
        _const_spec((D_FF, D_MODEL)),
    ]


def _even_out_call(x, y5, z, d_skip, w_glu, b_glu, gla, w_out, mods, layer, gn2, w1, w2):
    tm = _OUT_TM
    return pl.pallas_call(
        _even_out_kernel,
        out_shape=jax.ShapeDtypeStruct((T_TOK, D_MODEL), F32),
        grid=(T_TOK // tm,),
        in_specs=[
            pl.BlockSpec((tm, D_MODEL), lambda i: (i, 0)),
            pl.BlockSpec((tm, S5_WIDTH), lambda i: (i, 0)),
            pl.BlockSpec((tm, S5_WIDTH), lambda i: (i, 0)),
            _const_spec((1, S5_WIDTH)),
            _const_spec((S5_WIDTH, S5_WIDTH)),
            _const_spec((1, S5_WIDTH)),
            pl.BlockSpec((tm, GLA_VW), lambda i: (i, 0)),
            _const_spec((S5_WIDTH + GLA_VW, D_MODEL)),
        ] + _tail_specs(layer),
        out_specs=pl.BlockSpec((tm, D_MODEL), lambda i: (i, 0)),
        compiler_params=pltpu.CompilerParams(vmem_limit_bytes=VMEM_LIMIT),
        name="even_out_mlp",
    )(x, y5, z, d_skip, w_glu, b_glu, gla, w_out, mods, gn2, w1, w2)


def _odd_out_call(x, att, w_o, mods, layer, gn2, w1, w2):
    tm = _OUT_TM
    return pl.pallas_call(
        _odd_out_kernel,
        out_shape=jax.ShapeDtypeStruct((T_TOK, D_MODEL), F32),
        grid=(T_TOK // tm,),
        in_specs=[
            pl.BlockSpec((tm, D_MODEL), lambda i: (i, 0)),
            pl.BlockSpec((tm, D_MODEL), lambda i: (i, 0)),
            _const_spec((D_MODEL, D_MODEL)),
        ] + _tail_specs(layer),
        out_specs=pl.BlockSpec((tm, D_MODEL), lambda i: (i, 0)),
        compiler_params=pltpu.CompilerParams(vmem_limit_bytes=VMEM_LIMIT),
        name="odd_out_mlp",
    )(x, att, w_o, mods, gn2, w1, w2)


def _qkv_kernel(x_ref, gn_ref, m_ref, w_ref, qn_ref, kn_ref, cos_ref, sin_ref, q_ref, k_ref, v_ref, *, tile):
    h = _norm_mod(x_ref[...], gn_ref[...], m_ref[:, 0:D_MODEL], m_ref[:, D_MODEL:2 * D_MODEL]).astype(BF16)
    z = _dot(h, w_ref[...])
    v_ref[...] = z[:, (N_HEADS + KV_HEADS) * HEAD_DIM:]
    even_lane = (lax.broadcasted_iota(jnp.int32, (1, HEAD_DIM), 1) & 1) == 0

    def heads(rope):
        for hd in range(N_HEADS + KV_HEADS):
            xh = z[:, hd * HEAD_DIM:(hd + 1) * HEAD_DIM]
            gain = qn_ref[...] if hd < N_HEADS else kn_ref[...]
            xh = xh * lax.rsqrt(jnp.mean(xh * xh, axis=-1, keepdims=True) + EPS) * gain
            if rope:
                partner = jnp.where(even_lane, pltpu.roll(xh, HEAD_DIM - 1, 1), pltpu.roll(xh, 1, 1))
                xh = xh * cos_ref[...] + partner * sin_ref[...]
            if hd < N_HEADS:
                q_ref[:, hd * HEAD_DIM:(hd + 1) * HEAD_DIM] = xh.astype(BF16)
            else:
                k_ref[:, (hd - N_HEADS) * HEAD_DIM:(hd - N_HEADS + 1) * HEAD_DIM] = xh

    is_sample = pl.program_id(0) >= T_PROMPT // tile

    @pl.when(is_sample)
    def _():
        heads(True)

    @pl.when(jnp.logical_not(is_sample))
    def _():
        heads(False)


def _qkv_call(x, gn, mods, layer, w_qkv, q_norm, k_norm, cos_t, sin_t):
    tm = 512
    pos_tiles = DEC_SEQ // tm
    first_sample = T_PROMPT // tm

    def pos_map(i):
        return (jnp.maximum(i - first_sample, 0) % pos_tiles, 0)

    return pl.pallas_call(
        functools.partial(_qkv_kernel, tile=tm),
        out_shape=(jax.ShapeDtypeStruct((T_TOK, N_HEADS * HEAD_DIM), BF16),
                   jax.ShapeDtypeStruct((T_TOK, KV_HEADS * HEAD_DIM), F32),
                   jax.ShapeDtypeStruct((T_TOK, KV_HEADS * HEAD_DIM), F32)),
        grid=(T_TOK // tm,),
        in_specs=[
            pl.BlockSpec((tm, D_MODEL), lambda i: (i, 0)),
            pl.BlockSpec((1, D_MODEL), lambda i: (0, 0)),
            pl.BlockSpec((None, 1, 6 * D_MODEL), lambda i: (layer * COND_ROWS + _cond_row(i, tm), 0, 0)),
            pl.BlockSpec(w_qkv.shape, lambda i: (0, 0)),
            pl.BlockSpec((1, HEAD_DIM), lambda i: (0, 0)),
            pl.BlockSpec((1, HEAD_DIM), lambda i: (0, 0)),
            pl.BlockSpec((tm, HEAD_DIM), pos_map),
            pl.BlockSpec((tm, HEAD_DIM), pos_map),
        ],
        out_specs=(pl.BlockSpec((tm, N_HEADS * HEAD_DIM), lambda i: (i, 0)),
                   pl.BlockSpec((tm, KV_HEADS * HEAD_DIM), lambda i: (i, 0)),
                   pl.BlockSpec((tm, KV_HEADS * HEAD_DIM), lambda i: (i, 0))),
        compiler_params=pltpu.CompilerParams(vmem_limit_bytes=VMEM_LIMIT),
        name="odd_qkv",
    )(x, gn, mods, w_qkv, q_norm, k_norm, cos_t, sin_t)


def _rope_tables():
    rows = DEC_SEQ // GRID_W
    row = jnp.repeat(jnp.arange(rows, dtype=F32), GRID_W)
    col = jnp.tile(jnp.arange(GRID_W, dtype=F32), rows)
    inv = ROPE_THETA ** (-jnp.arange(0, AXIS_DIM, 2, dtype=F32) / AXIS_DIM)
    ang = jnp.concatenate([row[:, None] * inv, col[:, None] * inv], axis=-1)
    cos_t = jnp.repeat(jnp.cos(ang), 2, axis=-1)
    sin = jnp.sin(ang)
    sin_t = jnp.stack([-sin, sin], axis=-1).reshape(DEC_SEQ, HEAD_DIM)
    return cos_t, sin_t


def _attn_kernel(*refs, has_cache):
    if has_cache:
        q_ref, k_ref, v_ref, ck_ref, cv_ref, o_ref = refs
    else:
        q_ref, k_ref, v_ref, o_ref = refs
    scale = HEAD_DIM ** -0.5
    k = k_ref[...].astype(BF16)
    v = v_ref[...].astype(BF16)
    if has_cache:
        ck = ck_ref[...].astype(BF16)
        cv = cv_ref[...].astype(BF16)
    for r in range(Q_PER_KV):
        cs = slice(r * HEAD_DIM, (r + 1) * HEAD_DIM)
        q = q_ref[:, cs]
        s = lax.dot_general(q, k, NT_DIMS, preferred_element_type=F32) * scale
        m = jnp.max(s, axis=-1, keepdims=True)
        if has_cache:
            sc = lax.dot_general(q, ck, NT_DIMS, preferred_element_type=F32) * scale
            m = jnp.maximum(m, jnp.max(sc, axis=-1, keepdims=True))
        p = jnp.exp(s - m)
        den = jnp.sum(p, axis=-1, keepdims=True)
        o = _dot(p.astype(BF16), v)
        if has_cache:
            pc = jnp.exp(sc - m)
            den = den + jnp.sum(pc, axis=-1, keepdims=True)
            o = o + _dot(pc.astype(BF16), cv)
        o_ref[:, cs] = (o / den).astype(BF16)


def _attn_call(q, k, v, cache_k, cache_v, seq_len, nseq, row0, tq):
    assert row0 % seq_len == 0 and seq_len % tq == 0
    has_cache = cache_k is not None
    kv0 = row0 // seq_len
    q0 = row0 // tq
    nq = seq_len // tq
    gw = Q_PER_KV * HEAD_DIM
    in_specs = [
        pl.BlockSpec((tq, gw), lambda b, g, i: (q0 + b * nq + i, g)),
        pl.BlockSpec((seq_len, HEAD_DIM), lambda b, g, i: (kv0 + b, g)),
        pl.BlockSpec((seq_len, HEAD_DIM), lambda b, g, i: (kv0 + b, g)),
    ]
    args = [q, k, v]
    if has_cache:
        in_specs += [pl.BlockSpec((PAST_LEN, HEAD_DIM), lambda b, g, i: (b, g)),
                     pl.BlockSpec((PAST_LEN, HEAD_DIM), lambda b, g, i: (b, g))]
        args += [cache_k, cache_v]
    return pl.pallas_call(
        functools.partial(_attn_kernel, has_cache=has_cache),
        out_shape=jax.ShapeDtypeStruct((nseq * seq_len, N_HEADS * HEAD_DIM), BF16),
        grid=(nseq, KV_HEADS, nq),
        in_specs=in_specs,
        out_specs=pl.BlockSpec((tq, gw), lambda b, g, i: (b * nq + i, g)),
        compiler_params=pltpu.CompilerParams(vmem_limit_bytes=VMEM_LIMIT),
        name=f"attn_len{seq_len}",
    )(*args)


def kernel(x_prompt, x_sample, state_s5_re, state_s5_im, state_gla, cache_k, cache_v, c, c_ctx, norm_mix, norm_mlp, w_ada, b_ada, w_mlp_in, w_mlp_out, w_in_e, w_out_e, s5_lambda_re, s5_lambda_im, s5_log_dt, s5_b_re, s5_b_im, s5_c_re, s5_c_im, s5_d, s5_w_glu, s5_b_glu, gla_w_gate2, gla_b_gate, gla_norm, w_qkv_o, w_o_o, q_norm, k_norm):
    x = jnp.concatenate([x_prompt.reshape(T_PROMPT, D_MODEL), x_sample.reshape(T_SAMPLE, D_MODEL)], axis=0)
    cond8 = jnp.concatenate([c_ctx[None, :], c, jnp.zeros((COND_ROWS - 1 - DEC_BATCH, D_MODEL), F32)], axis=0)
    mods = _ada_call(cond8, w_ada, b_ada).reshape(DEPTH * COND_ROWS, 1, 6 * D_MODEL)

    n_main = S5_WIDTH + 2 * GLA_QK + 2 * GLA_VW
    w_in = w_in_e[0]
    w_main = w_in[:, :n_main].astype(BF16)
    w_glr = jnp.pad(w_in[:, n_main:], ((0, 0), (0, 128 - 2 * GLA_RANK))).astype(BF16)
    w_gate = jnp.zeros((128, 2 * GLA_QK), F32)
    w_gate = w_gate.at[0:GLA_RANK, 0:GLA_QK].set(gla_w_gate2[0, 0])
    w_gate = w_gate.at[GLA_RANK:2 * GLA_RANK, GLA_QK:].set(gla_w_gate2[0, 1]).astype(BF16)
    b_gate = gla_b_gate[0].reshape(1, 2 * GLA_QK)
    z, g = _inproj_call(x, norm_mix[0:1], mods, 0, w_main, w_glr, w_gate, b_gate)

    mats = _s5_prep_call(s5_lambda_re[0], s5_lambda_im[0], s5_log_dt[0], s5_b_re[0], s5_b_im[0],
                         s5_c_re[0], s5_c_im[0])

    def state_rows(s):
        return jnp.transpose(s, (2, 0, 1, 3)).reshape(S5_GROUPS, DEC_BATCH, 2 * S5_STATE)

    h0 = jnp.concatenate([state_rows(state_s5_re[:, 0]), state_rows(state_s5_im[:, 0])], axis=-1)
    y5, ns = _s5_call(z, mats, h0)

    def state_out(n):
        return jnp.transpose(n.reshape(S5_GROUPS, BATCH, 2, S5_STATE), (1, 2, 0, 3))[:, None]

    new_s5_re = state_out(ns[:, :, :2 * S5_STATE])
    new_s5_im = state_out(ns[:, :, 2 * S5_STATE:])

    gn_gla = gla_norm[0].reshape(1, GLA_DV)
    gla_p, sfin = _gla_call(z, g, gn_gla, None, SEQ, BATCH, 0)
    s0 = state_gla[:, 0].reshape(DEC_BATCH, 2, GLA_QK, GLA_DV)
    gla_s, _ = _gla_call(z, g, gn_gla, s0, DEC_SEQ, DEC_BATCH, T_PROMPT)
    gla = jnp.concatenate([gla_p, gla_s], axis=0)
    new_gla = sfin.reshape(BATCH, 1, 2, GLA_HEADS, GLA_DK, GLA_DV)

    x = _even_out_call(x, y5, z, s5_d[0].reshape(1, S5_WIDTH), s5_w_glu[0].astype(BF16),
                       s5_b_glu[0].reshape(1, S5_WIDTH), gla, w_out_e[0].astype(BF16), mods, 0,
                       norm_mlp[0:1], w_mlp_in[0].astype(BF16), w_mlp_out[0].astype(BF16))

    cos_t, sin_t = _rope_tables()
    q, k, v = _qkv_call(x, norm_mix[1:2], mods, 1, w_qkv_o[0].astype(BF16), q_norm[0].reshape(1, HEAD_DIM),
                        k_norm[0].reshape(1, HEAD_DIM), cos_t, sin_t)
    att_p = _attn_call(q, k, v, None, None, SEQ, BATCH, 0, SEQ)
    ck = cache_k[:, 0].reshape(DEC_BATCH * PAST_LEN, KV_HEADS * HEAD_DIM)
    cv = cache_v[:, 0].reshape(DEC_BATCH * PAST_LEN, KV_HEADS * HEAD_DIM)
    att_s = _attn_call(q, k, v, ck, cv, DEC_SEQ, DEC_BATCH, T_PROMPT, 256)
    att = jnp.concatenate([att_p, att_s], axis=0)
    x = _odd_out_call(x, att, w_o_o[0].astype(BF16), mods, 1, norm_mlp[1:2],
                      w_mlp_in[1].astype(BF16), w_mlp_out[1].astype(BF16))

    new_k = k[:T_PROMPT].reshape(BATCH, 1, SEQ, KV_HEADS, HEAD_DIM)
    new_v = v[:T_PROMPT].reshape(BATCH, 1, SEQ, KV_HEADS, HEAD_DIM)
    y_prompt = x[:T_PROMPT].reshape(BATCH, SEQ, D_MODEL)
    y_sample = x[T_PROMPT:].reshape(DEC_BATCH, DEC_SEQ, D_MODEL)
    return (y_prompt, y_sample, new_s5_re, new_s5_im, new_gla, new_k, new_v)
```

```python
import functools
import math

import jax
import jax.numpy as jnp
import numpy as np
from jax import lax
from jax.experimental import pallas as pl
from jax.experimental.pallas import tpu as pltpu

F32 = jnp.float32
BF16 = jnp.bfloat16

D_MODEL = 1024
BATCH = 16
SEQ = 256
DEPTH = 2
DEC_BATCH = 4
DEC_SEQ = 1024
PAST_LEN = 512
GRID_W = 64
S5_WIDTH = 512
S5_GROUP_CH = 16
S5_GROUPS = 32
S5_STATE = 64
GLA_HEADS = 4
GLA_VW = 512
GLA_DV = 128
GLA_DK = 64
GLA_QK = 256
GLA_RANK = 16
GLA_TAU = 16.0
GLA_CHUNK = 64
GLA_CPB = 4
GLA_BLK = GLA_CPB * GLA_CHUNK
HEAD_DIM = 128
N_HEADS = 8
KV_HEADS = 2
Q_PER_KV = N_HEADS // KV_HEADS
AXIS_DIM = 64
ROPE_THETA = 10000.0
D_FF = 4096
EPS = 1e-6

T_PROMPT = BATCH * SEQ
T_SAMPLE = DEC_BATCH * DEC_SEQ
T_TOK = T_PROMPT + T_SAMPLE
COND_ROWS = 8
COND_SPAN = 1024
PROMPT_SPANS = T_PROMPT // COND_SPAN

S5_Q = 16
S5_W = S5_Q * S5_GROUP_CH
S5_GPB = 128 // S5_GROUP_CH
S5_ROWS = T_TOK // S5_Q
S5_PROMPT_ROWS = T_PROMPT // S5_Q
S5_PROMPT_CHUNKS = SEQ // S5_Q
S5_SAMPLE_CHUNKS = DEC_SEQ // S5_Q

VMEM_LIMIT = 56 * 1024 * 1024

NT_DIMS = (((1,), (1,)), ((), ()))
TN_DIMS = (((0,), (0,)), ((), ()))


def _cond_row(i, tile):
    return jnp.maximum((i * tile) // COND_SPAN - (PROMPT_SPANS - 1), 0)


def _norm_mod(x, gain, shift, scale):
    y = x * lax.rsqrt(jnp.mean(x * x, axis=-1, keepdims=True) + EPS)
    return (y * gain) * (1.0 + scale) + shift


def _dot(a, b):
    return jnp.dot(a, b, preferred_element_type=F32)


def _ada_kernel(cond_ref, w_ref, b_ref, o_ref):
    s = jax.nn.silu(cond_ref[...]).astype(BF16)
    o_ref[...] = _dot(s, w_ref[...].astype(BF16)) + b_ref[...]


def _ada_call(cond8, w_ada, b_ada):
    tn = 2048
    nj = 6 * D_MODEL // tn
    return pl.pallas_call(
        _ada_kernel,
        out_shape=jax.ShapeDtypeStruct((DEPTH, COND_ROWS, 6 * D_MODEL), F32),
        grid=(DEPTH, nj),
        in_specs=[
            pl.BlockSpec((COND_ROWS, D_MODEL), lambda l, j: (0, 0)),
            pl.BlockSpec((None, D_MODEL, tn), lambda l, j: (l, 0, j)),
            pl.BlockSpec((None, 1, tn), lambda l, j: (l, 0, j)),
        ],
        out_specs=pl.BlockSpec((None, COND_ROWS, tn), lambda l, j: (l, 0, j)),
        compiler_params=pltpu.CompilerParams(vmem_limit_bytes=VMEM_LIMIT),
        name="ada_mod",
    )(cond8, w_ada, b_ada.reshape(DEPTH, 1, 6 * D_MODEL))


def _inproj_kernel(x_ref, gn_ref, m_ref, w_ref, wglr_ref, wg_ref, bg_ref, z_ref, g_ref):
    h = _norm_mod(x_ref[...], gn_ref[...], m_ref[:, 0:D_MODEL], m_ref[:, D_MODEL:2 * D_MODEL]).astype(BF16)
    z_ref[...] = _dot(h, w_ref[...])
    glr = _dot(h, wglr_ref[...]).astype(BF16)
    pre = _dot(glr, wg_ref[...]) + bg_ref[...]
    g_ref[...] = jax.nn.log_sigmoid(pre) * (1.0 / GLA_TAU)


def _inproj_call(x, gn, mods, layer, w_main, w_glr, w_gate, b_gate):
    tm = 512
    nz = w_main.shape[1]
    return pl.pallas_call(
        _inproj_kernel,
        out_shape=(jax.ShapeDtypeStruct((T_TOK, nz), F32),
                   jax.ShapeDtypeStruct((T_TOK, 2 * GLA_QK), F32)),
        grid=(T_TOK // tm,),
        in_specs=[
            pl.BlockSpec((tm, D_MODEL), lambda i: (i, 0)),
            pl.BlockSpec((1, D_MODEL), lambda i: (0, 0)),
            pl.BlockSpec((None, 1, 6 * D_MODEL), lambda i: (layer * COND_ROWS + _cond_row(i, tm), 0, 0)),
            pl.BlockSpec((D_MODEL, nz), lambda i: (0, 0)),
            pl.BlockSpec((D_MODEL, 128), lambda i: (0, 0)),
            pl.BlockSpec((128, 2 * GLA_QK), lambda i: (0, 0)),
            pl.BlockSpec((1, 2 * GLA_QK), lambda i: (0, 0)),
        ],
        out_specs=(pl.BlockSpec((tm, nz), lambda i: (i, 0)),
                   pl.BlockSpec((tm, 2 * GLA_QK), lambda i: (i, 0))),
        compiler_params=pltpu.CompilerParams(vmem_limit_bytes=VMEM_LIMIT),
        name="even_inproj",
    )(x, gn, mods, w_main, w_glr, w_gate, b_gate)


def _s5_prep_kernel(lre_ref, lim_ref, ldt_ref, btre_ref, btim_ref, cre_ref, cim_ref, ccf_ref, ccb_ref,
                    t_ref, bq_ref, cqt_ref, be_ref, a_ref, t_scr):
    lre = lre_ref[...]
    lim = lim_ref[...]
    dt = jnp.exp(ldt_ref[...])
    a = lre * dt
    th = lim * dt

    def lam_pow(k):
        mag = jnp.exp(k * a)
        return mag * jnp.cos(k * th), mag * jnp.sin(k * th)

    lb_re, lb_im = lam_pow(1.0)
    nr = lb_re - 1.0
    den = lre * lre + lim * lim
    cf_re = (nr * lre + lb_im * lim) / den
    cf_im = (lb_im * lre - nr * lim) / den
    bt_re = btre_ref[...]
    bt_im = btim_ref[...]
    bb_re = jnp.tile(cf_re * bt_re - cf_im * bt_im, (S5_Q, 1))
    bb_im = jnp.tile(cf_re * bt_im + cf_im * bt_re, (S5_Q, 1))

    shape = (S5_W, 128)
    pos = lax.shift_right_logical(lax.broadcasted_iota(jnp.int32, shape, 0), 4)
    is_f = lax.broadcasted_iota(jnp.int32, shape, 1) < S5_STATE
    posq = lax.broadcasted_iota(jnp.int32, (S5_Q, 128), 0).astype(F32)
    is_fq = lax.broadcasted_iota(jnp.int32, (S5_Q, 128), 1) < S5_STATE

    def per_channel(tbl):
        return jnp.broadcast_to(tbl[:, None, :], (S5_Q, S5_GROUP_CH, 128)).reshape(shape)

    p_re, p_im = map(per_channel, lam_pow(jnp.where(is_fq, (S5_Q - 1.0) - posq, posq)))
    w_re = p_re * bb_re - p_im * bb_im
    w_im = p_re * bb_im + p_im * bb_re
    bq = jnp.concatenate([w_re, w_im], axis=1)
    bq_ref[...] = jnp.transpose(bq).astype(BF16)

    edge = pos == jnp.where(is_f, 0, S5_Q - 1)
    be = jnp.concatenate([jnp.where(edge, bb_re, 0.0), jnp.where(edge, bb_im, 0.0)], axis=1)
    be_ref[...] = jnp.transpose(be).astype(BF16)

    q_re, q_im = map(per_channel, lam_pow(jnp.where(is_fq, posq + 1.0, S5_Q - posq)))
    ct_re = jnp.tile(cre_ref[...], (S5_Q, 1))
    ct_im = jnp.tile(cim_ref[...], (S5_Q, 1))
    g_re = q_re * ct_re - q_im * ct_im
    g_im = q_re * ct_im + q_im * ct_re
    cqt_ref[...] = jnp.concatenate([g_re, -g_im], axis=1).astype(BF16)

    a_re, a_im = lam_pow(float(S5_Q))
    a_ref[...] = jnp.concatenate([a_re, a_im], axis=1)

    kf = lax.dot_general(bq, ccf_ref[...], NT_DIMS, precision=lax.Precision.HIGHEST, preferred_element_type=F32)
    kb = lax.dot_general(bq, ccb_ref[...], NT_DIMS, precision=lax.Precision.HIGHEST, preferred_element_type=F32)
    lo = S5_W - S5_GROUP_CH
    dd = jnp.concatenate([kf[0:lo], kf[lo:S5_W] + kb[0:S5_GROUP_CH], kb[S5_GROUP_CH:S5_W]], axis=0)
    for t in range(S5_Q):
        r0 = (S5_Q - 1 - t) * S5_GROUP_CH
        t_scr[:, t * S5_GROUP_CH:(t + 1) * S5_GROUP_CH] = dd[r0:r0 + S5_W]
    t_ref[...] = jnp.transpose(t_scr[...]).astype(BF16)


def _s5_prep_call(lam_re, lam_im, log_dt, b_re, b_im, c_re, c_im):
    def fb(p):
        return jnp.transpose(p, (1, 0, 2)).reshape(S5_GROUPS, 1, 2 * S5_STATE)

    def dup(p):
        return jnp.concatenate([p, p], axis=-1)

    ldt = fb(jnp.broadcast_to(log_dt[:, :, None], (2, S5_GROUPS, S5_STATE)))
    bt_re = dup(jnp.transpose(b_re, (0, 2, 1)))
    bt_im = dup(jnp.transpose(b_im, (0, 2, 1)))
    zero = jnp.zeros_like(c_re)
    ccf = jnp.concatenate([c_re, zero, -c_im, zero], axis=-1)
    ccb = jnp.concatenate([zero, c_re, zero, -c_im], axis=-1)

    row = pl.BlockSpec((None, 1, 128), lambda g: (g, 0, 0))
    mat16 = pl.BlockSpec((None, S5_GROUP_CH, 128), lambda g: (g, 0, 0))
    mat16w = pl.BlockSpec((None, S5_GROUP_CH, S5_W), lambda g: (g, 0, 0))
    sq = pl.BlockSpec((None, S5_W, S5_W), lambda g: (g, 0, 0))
    sq_shape = jax.ShapeDtypeStruct((S5_GROUPS, S5_W, S5_W), BF16)
    return pl.pallas_call(
        _s5_prep_kernel,
        out_shape=(sq_shape, sq_shape, sq_shape, sq_shape,
                   jax.ShapeDtypeStruct((S5_GROUPS, 1, S5_W), F32)),
        grid=(S5_GROUPS,),
        in_specs=[row, row, row, mat16, mat16, mat16, mat16, mat16w, mat16w],
        out_specs=(sq, sq, sq, sq, pl.BlockSpec((None, 1, S5_W), lambda g: (g, 0, 0))),
        scratch_shapes=[pltpu.VMEM((S5_W, S5_W), F32)],
        name="s5_prep",
    )(fb(lam_re), fb(lam_im), ldt, bt_re, bt_im, dup(c_re), dup(c_im), ccf, ccb)


def _s5_kernel(u_ref, tt_ref, bqt_ref, cqt_ref, bet_ref, a_ref, h0_ref, y_ref, ns_ref,
               ut_scr, x_scr, spf_scr, spb_scr, ne_scr, yt_scr, xt_scr):
    gch = S5_GROUP_CH
    for s in range(S5_Q):
        rows = u_ref[pl.ds(s, S5_ROWS, stride=S5_Q), :]
        rows_t = jnp.transpose(rows).astype(BF16)
        for gl in range(S5_GPB):
            ut_scr[gl, s * gch:(s + 1) * gch, :] = rows_t[gl * gch:(gl + 1) * gch, :]

    for gl in range(S5_GPB):
        ut = ut_scr[gl]
        xt_scr[...] = _dot(bqt_ref[gl], ut)
        x = jnp.transpose(xt_scr[...])
        xt_scr[:, 0:S5_PROMPT_ROWS] = _dot(bet_ref[gl], ut[:, 0:S5_PROMPT_ROWS])
        ne = jnp.transpose(xt_scr[:, 0:S5_PROMPT_ROWS])
        for part in range(2):
            x_scr[part, pl.ds(gl, S5_ROWS, stride=S5_GPB), :] = x[:, part * 128:(part + 1) * 128]
            ne_scr[part, pl.ds(gl, S5_PROMPT_ROWS, stride=S5_GPB), :] = ne[:, part * 128:(part + 1) * 128]

    is_f = lax.broadcasted_iota(jnp.int32, (1, 128), 1) < S5_STATE
    a_re = a_ref[:, 0:128]
    a_im = a_ref[:, 128:256]

    def tile(row):
        return pl.ds(pl.multiple_of(row * S5_GPB, S5_GPB), S5_GPB)

    def scan(base, nseq, nchunk, s_init):
        def body(i, state):
            new = []
            for b in range(nseq):
                s_re, s_im = state[b]
                rows_f = tile(base + b * nchunk + i)
                rows_b = tile(base + b * nchunk + (nchunk - 1 - i))
                spf_scr[0, rows_f, :] = s_re
                spf_scr[1, rows_f, :] = s_im
                spb_scr[0, rows_b, :] = s_re
                spb_scr[1, rows_b, :] = s_im
                x_re = jnp.where(is_f, x_scr[0, rows_f, :], x_scr[0, rows_b, :])
                x_im = jnp.where(is_f, x_scr[1, rows_f, :], x_scr[1, rows_b, :])
                new.append((a_re * s_re - a_im * s_im + x_re, a_re * s_im + a_im * s_re + x_im))
            return tuple(new)

        lax.fori_loop(0, nchunk, body, tuple(s_init))

    zero = jnp.zeros((S5_GPB, 128), F32)
    scan(0, BATCH, S5_PROMPT_CHUNKS, [(zero, zero)] * BATCH)
    scan(S5_PROMPT_ROWS, DEC_BATCH, S5_SAMPLE_CHUNKS,
         [(h0_ref[b, :, 0:128], h0_ref[b, :, 128:256]) for b in range(DEC_BATCH)])

    for b in range(BATCH):
        first = pl.ds(b * S5_PROMPT_CHUNKS * S5_GPB, S5_GPB)
        last = pl.ds(((b + 1) * S5_PROMPT_CHUNKS - 1) * S5_GPB, S5_GPB)
        for part in range(2):
            ns_ref[b, :, part * 128:(part + 1) * 128] = jnp.where(is_f, ne_scr[part, first, :], ne_scr[part, last, :])

    for gl in range(S5_GPB):
        rows = pl.ds(gl, S5_ROWS, stride=S5_GPB)
        carried = jnp.concatenate([jnp.where(is_f, spf_scr[p, rows, :], spb_scr[p, rows, :]) for p in range(2)],
                                  axis=1).astype(BF16)
        yt = _dot(tt_ref[gl], ut_scr[gl]) + lax.dot_general(cqt_ref[gl], carried, NT_DIMS,
                                                            preferred_element_type=F32)
        for t in range(S5_Q):
            yt_scr[t, gl * gch:(gl + 1) * gch, :] = yt[t * gch:(t + 1) * gch, :]
    for t in range(S5_Q):
        y_ref[pl.ds(t, S5_ROWS, stride=S5_Q), :] = jnp.transpose(yt_scr[t])


def _s5_call(z, mats, h0):
    tt_m, bqt_m, cqt_m, bet_m, a_m = mats
    nsteps = S5_GROUPS // S5_GPB
    sq = pl.BlockSpec((S5_GPB, S5_W, S5_W), lambda g: (g, 0, 0))
    state_scr = pltpu.VMEM((2, S5_ROWS * S5_GPB, 128), F32)
    return pl.pallas_call(
        _s5_kernel,
        out_shape=(jax.ShapeDtypeStruct((T_TOK, S5_WIDTH), F32),
                   jax.ShapeDtypeStruct((nsteps, BATCH, S5_GPB, S5_W), F32)),
        grid=(nsteps,),
        in_specs=[
            pl.BlockSpec((T_TOK, 128), lambda g: (0, g)),
            sq, sq, sq, sq,
            pl.BlockSpec((S5_GPB, S5_W), lambda g: (g, 0)),
            pl.BlockSpec((None, DEC_BATCH, S5_GPB, S5_W), lambda g: (g, 0, 0, 0)),
        ],
        out_specs=(pl.BlockSpec((T_TOK, 128), lambda g: (0, g)),
                   pl.BlockSpec((None, BATCH, S5_GPB, S5_W), lambda g: (g, 0, 0, 0))),
        scratch_shapes=[pltpu.VMEM((S5_GPB, S5_W, S5_ROWS), BF16), state_scr, state_scr, state_scr,
                        pltpu.VMEM((2, S5_PROMPT_ROWS * S5_GPB, 128), F32),
                        pltpu.VMEM((S5_Q, 128, S5_ROWS), F32), pltpu.VMEM((S5_W, S5_ROWS), F32)],
        compiler_params=pltpu.CompilerParams(vmem_limit_bytes=VMEM_LIMIT),
        name="s5_scan",
    )(z, tt_m, bqt_m, cqt_m, bet_m, a_m.reshape(S5_GROUPS, S5_W), h0)


def _split_bf16(x):
    hi = x.astype(BF16)
    r1 = x - hi.astype(F32)
    mid = r1.astype(BF16)
    lo = (r1 - mid.astype(F32)).astype(BF16)
    return hi, mid, lo


def _gla_kernel(*refs, seq_len, has_s0):
    if has_s0:
        (q_ref, k_ref, v_ref, gf_ref, gb_ref, r_ref, gn_ref, s0_ref, o_ref, sfin_ref,
         oi_scr, qd_scr, kv_scr, dec_scr, ss_scr) = refs
    else:
        (q_ref, k_ref, v_ref, gf_ref, gb_ref, r_ref, gn_ref, o_ref, sfin_ref,
         oi_scr, qd_scr, kv_scr, dec_scr, ss_scr) = refs
        s0_ref = None
    nblk = seq_len // GLA_BLK
    nchunk = seq_len // GLA_CHUNK
    cl = GLA_CHUNK
    ti = lax.broadcasted_iota(jnp.int32, (GLA_BLK, GLA_BLK), 0)
    si = lax.broadcasted_iota(jnp.int32, (GLA_BLK, GLA_BLK), 1)
    same = lax.shift_right_logical(ti, 6) == lax.shift_right_logical(si, 6)
    keep = (same & (ti >= si), same & (ti <= si))
    tri = tuple(kp.astype(BF16) for kp in keep)
    lane_head = lax.shift_right_logical(lax.broadcasted_iota(jnp.int32, (cl, GLA_QK), 1), 6)
    zeros_v = jnp.zeros((cl, GLA_DV), BF16)
    heads = [(slice(h * GLA_DK, (h + 1) * GLA_DK), slice(h * GLA_DV, (h + 1) * GLA_DV)) for h in range(GLA_HEADS)]

    for j in range(nblk):
        rows = slice(j * GLA_BLK, (j + 1) * GLA_BLK)
        q = q_ref[rows, :] * (GLA_DK ** -0.5)
        k = k_ref[rows, :]
        v = v_ref[rows, :].astype(BF16)
        qd, kd, k2t = [], [], []
        for d, g_ref in enumerate((gf_ref, gb_ref)):
            b = sum(_dot(tri[d], part) for part in _split_bf16(g_ref[rows, :]))
            last = cl - 1 if d == 0 else 0
            b_last = [b[c * cl + last:c * cl + last + 1] for c in range(GLA_CPB)]
            bl = jnp.concatenate([jnp.broadcast_to(x, (cl, GLA_QK)) for x in b_last], axis=0)
            qd_d = (q * jnp.exp(b)).astype(BF16)
            qd_scr[d, rows, :] = qd_d
            qd.append(qd_d)
            kd.append((k * jnp.exp(-b)).astype(BF16))
            k2t.append(jnp.transpose(k * jnp.exp(bl - b)).astype(BF16))
            for c in range(GLA_CPB):
                dec_scr[d, j * GLA_CPB + c] = jnp.exp(jnp.transpose(jnp.broadcast_to(b_last[c], (GLA_DV, GLA_QK))))
        for h, (ks, vs) in enumerate(heads):
            att = [jnp.where(keep[d], lax.dot_general(qd[d][:, ks], kd[d][:, ks], NT_DIMS,
                                                      preferred_element_type=F32), 0.0) for d in range(2)]
            oi_scr[rows, vs] = _dot((att[0] + att[1]).astype(BF16), v[:, vs])
            vh = v[:, vs]
            vexp = jnp.concatenate(
                [jnp.concatenate([vh[c * cl:(c + 1) * cl] if c2 == c else zeros_v for c2 in range(GLA_CPB)], axis=1)
                 for c in range(GLA_CPB)], axis=0)
            for d in range(2):
                kv_scr[d, j, h] = _dot(k2t[d][ks, :], vexp)

    for d in range(2):
        s = s0_ref[d] if has_s0 else jnp.zeros((GLA_QK, GLA_DV), F32)
        for cg in (range(nchunk) if d == 0 else range(nchunk - 1, -1, -1)):
            j, c = divmod(cg, GLA_CPB)
            ss_scr[d, cg] = s.astype(BF16)
            kv = jnp.concatenate([kv_scr[d, j, h, :, c * GLA_DV:(c + 1) * GLA_DV] for h in range(GLA_HEADS)], axis=0)
            s = s * dec_scr[d, cg] + kv
        sfin_ref[d] = s

    for cg in range(nchunk):
        rows = slice(cg * cl, (cg + 1) * cl)
        inter = []
        for d in range(2):
            qc = qd_scr[d, rows, :]
            qstack = jnp.concatenate([jnp.where(lane_head == h, qc, jnp.zeros_like(qc)) for h in range(GLA_HEADS)],
                                     axis=0)
            inter.append(_dot(qstack, ss_scr[d, cg]))
        gate = jax.nn.silu(r_ref[rows, :])
        for h, (ks, vs) in enumerate(heads):
            hr = slice(h * cl, (h + 1) * cl)
            oh = oi_scr[rows, vs] + inter[0][hr] + inter[1][hr]
            oh = oh * lax.rsqrt(jnp.mean(oh * oh, axis=-1, keepdims=True) + EPS) * gn_ref[...]
            o_ref[rows, vs] = oh * gate[:, vs]


def _gla_call(z, g, gla_norm, s0, seq_len, nseq, row0):
    assert row0 % seq_len == 0
    r0 = row0 // seq_len
    has_s0 = s0 is not None
    qk_off = S5_WIDTH // GLA_QK
    v_off = (S5_WIDTH + 2 * GLA_QK) // GLA_VW
    in_specs = [
        pl.BlockSpec((seq_len, GLA_QK), lambda i: (r0 + i, qk_off)),
        pl.BlockSpec((seq_len, GLA_QK), lambda i: (r0 + i, qk_off + 1)),
        pl.BlockSpec((seq_len, GLA_VW), lambda i: (r0 + i, v_off)),
        pl.BlockSpec((seq_len, GLA_QK), lambda i: (r0 + i, 0)),
        pl.BlockSpec((seq_len, GLA_QK), lambda i: (r0 + i, 1)),
        pl.BlockSpec((seq_len, GLA_VW), lambda i: (r0 + i, v_off + 1)),
        pl.BlockSpec((1, GLA_DV), lambda i: (0, 0)),
    ]
    args = [z, z, z, g, g, z, gla_norm]
    if has_s0:
        in_specs.append(pl.BlockSpec((None, 2, GLA_QK, GLA_DV), lambda i: (i, 0, 0, 0)))
        args.append(s0)
    return pl.pallas_call(
        functools.partial(_gla_kernel, seq_len=seq_len, has_s0=has_s0),
        out_shape=(jax.ShapeDtypeStruct((nseq * seq_len, GLA_VW), F32),
                   jax.ShapeDtypeStruct((nseq, 2, GLA_QK, GLA_DV), F32)),
        grid=(nseq,),
        in_specs=in_specs,
        out_specs=(pl.BlockSpec((seq_len, GLA_VW), lambda i: (i, 0)),
                   pl.BlockSpec((None, 2, GLA_QK, GLA_DV), lambda i: (i, 0, 0, 0))),
        scratch_shapes=[
            pltpu.VMEM((seq_len, GLA_VW), F32),
            pltpu.VMEM((2, seq_len, GLA_QK), BF16),
            pltpu.VMEM((2, seq_len // GLA_BLK, GLA_HEADS, GLA_DK, GLA_CPB * GLA_DV), F32),
            pltpu.VMEM((2, seq_len // GLA_CHUNK, GLA_QK, GLA_DV), F32),
            pltpu.VMEM((2, seq_len // GLA_CHUNK, GLA_QK, GLA_DV), BF16),
        ],
        compiler_params=pltpu.CompilerParams(vmem_limit_bytes=VMEM_LIMIT),
        name=f"gla_len{seq_len}",
    )(*args)


def _mlp_tail(x, mix, m_ref, gn2_ref, w1_ref, w2_ref, o_ref):
    y1 = x + m_ref[:, 2 * D_MODEL:3 * D_MODEL] * mix
    h = _norm_mod(y1, gn2_ref[...], m_ref[:, 3 * D_MODEL:4 * D_MODEL], m_ref[:, 4 * D_MODEL:5 * D_MODEL]).astype(BF16)
    tf = 512
    acc = jnp.zeros(y1.shape, F32)
    for c in range(D_FF // tf):
        a = _dot(h, w1_ref[:, c * tf:(c + 1) * tf])
        a = jnp.square(jnp.maximum(a, 0.0)).astype(BF16)
        acc = acc + _dot(a, w2_ref[c * tf:(c + 1) * tf, :])
    o_ref[...] = y1 + m_ref[:, 5 * D_MODEL:6 * D_MODEL] * acc


def _even_out_kernel(x_ref, y5_ref, u_ref, dskip_ref, wglu_ref, bglu_ref, gla_ref, wout_ref,
                     m_ref, gn2_ref, w1_ref, w2_ref, o_ref):
    ys = y5_ref[...] + u_ref[...] * dskip_ref[...]
    gl = jax.nn.gelu(ys)
    s5o = gl * jax.nn.sigmoid(_dot(gl.astype(BF16), wglu_ref[...]) + bglu_ref[...])
    mix = _dot(s5o.astype(BF16), wout_ref[0:S5_WIDTH, :]) + _dot(gla_ref[...].astype(BF16), wout_ref[S5_WIDTH:, :])
    _mlp_tail(x_ref[...], mix, m_ref, gn2_ref, w1_ref, w2_ref, o_ref)


def _odd_out_kernel(x_ref, att_ref, wo_ref, m_ref, gn2_ref, w1_ref, w2_ref, o_ref):
    mix = _dot(att_ref[...], wo_ref[...])
    _mlp_tail(x_ref[...], mix, m_ref, gn2_ref, w1_ref, w2_ref, o_ref)


_OUT_TM = 512


def _const_spec(shape):
    return pl.BlockSpec(shape, lambda i: (0,) * len(shape), pipeline_mode=pl.Buffered(1))


def _tail_specs(layer):
    tm = _OUT_TM
    return [
        pl.BlockSpec((None, 1, 6 * D_MODEL), lambda i: (layer * COND_ROWS + _cond_row(i, tm), 0, 0)),
        _const_spec((1, D_MODEL)),
        _const_spec((D_MODEL, D_FF)),
        _const_spec((D_FF, D_MODEL)),
    ]


def _even_out_call(x, y5, z, d_skip, w_glu, b_glu, gla, w_out, mods, layer, gn2, w1, w2):
    tm = _OUT_TM
    return pl.pallas_call(
        _even_out_kernel,
        out_shape=jax.ShapeDtypeStruct((T_TOK, D_MODEL), F32),
        grid=(T_TOK // tm,),
        in_specs=[
            pl.BlockSpec((tm, D_MODEL), lambda i: (i, 0)),
            pl.BlockSpec((tm, S5_WIDTH), lambda i: (i, 0)),
            pl.BlockSpec((tm, S5_WIDTH), lambda i: (i, 0)),
            _const_spec((1, S5_WIDTH)),
            _const_spec((S5_WIDTH, S5_WIDTH)),
            _const_spec((1, S5_WIDTH)),
            pl.BlockSpec((tm, GLA_VW), lambda i: (i, 0)),
            _const_spec((S5_WIDTH + GLA_VW, D_MODEL)),
        ] + _tail_specs(layer),
        out_specs=pl.BlockSpec((tm, D_MODEL), lambda i: (i, 0)),
        compiler_params=pltpu.CompilerParams(vmem_limit_bytes=VMEM_LIMIT),
        name="even_out_mlp",
    )(x, y5, z, d_skip, w_glu, b_glu, gla, w_out, mods, gn2, w1, w2)


def _odd_out_call(x, att, w_o, mods, layer, gn2, w1, w2):
    tm = _OUT_TM
    return pl.pallas_call(
        _odd_out_kernel,
        out_shape=jax.ShapeDtypeStruct((T_TOK, D_MODEL), F32),
        grid=(T_TOK // tm,),
        in_specs=[
            pl.BlockSpec((tm, D_MODEL), lambda i: (i, 0)),
            pl.BlockSpec((tm, D_MODEL), lambda i: (i, 0)),
            _const_spec((D_MODEL, D_MODEL)),
        ] + _tail_specs(layer),
        out_specs=pl.BlockSpec((tm, D_MODEL), lambda i: (i, 0)),
        compiler_params=pltpu.CompilerParams(vmem_limit_bytes=VMEM_LIMIT),
        name="odd_out_mlp",
    )(x, att, w_o, mods, gn2, w1, w2)


def _qkv_kernel(x_ref, gn_ref, m_ref, w_ref, qn_ref, kn_ref, cos_ref, sin_ref, q_ref, k_ref, v_ref, *, tile):
    h = _norm_mod(x_ref[...], gn_ref[...], m_ref[:, 0:D_MODEL], m_ref[:, D_MODEL:2 * D_MODEL]).astype(BF16)
    z = _dot(h, w_ref[...])
    v_ref[...] = z[:, (N_HEADS + KV_HEADS) * HEAD_DIM:]
    even_lane = (lax.broadcasted_iota(jnp.int32, (1, HEAD_DIM), 1) & 1) == 0

    def heads(rope):
        for hd in range(N_HEADS + KV_HEADS):
            xh = z[:, hd * HEAD_DIM:(hd + 1) * HEAD_DIM]
            gain = qn_ref[...] if hd < N_HEADS else kn_ref[...]
            xh = xh * lax.rsqrt(jnp.mean(xh * xh, axis=-1, keepdims=True) + EPS) * gain
            if rope:
                partner = jnp.where(even_lane, pltpu.roll(xh, HEAD_DIM - 1, 1), pltpu.roll(xh, 1, 1))
                xh = xh * cos_ref[...] + partner * sin_ref[...]
            if hd < N_HEADS:
                q_ref[:, hd * HEAD_DIM:(hd + 1) * HEAD_DIM] = xh.astype(BF16)
            else:
                k_ref[:, (hd - N_HEADS) * HEAD_DIM:(hd - N_HEADS + 1) * HEAD_DIM] = xh

    is_sample = pl.program_id(0) >= T_PROMPT // tile

    @pl.when(is_sample)
    def _():
        heads(True)

    @pl.when(jnp.logical_not(is_sample))
    def _():
        heads(False)


def _qkv_call(x, gn, mods, layer, w_qkv, q_norm, k_norm, cos_t, sin_t):
    tm = 512
    pos_tiles = DEC_SEQ // tm
    first_sample = T_PROMPT // tm

    def pos_map(i):
        return (jnp.maximum(i - first_sample, 0) % pos_tiles, 0)

    return pl.pallas_call(
        functools.partial(_qkv_kernel, tile=tm),
        out_shape=(jax.ShapeDtypeStruct((T_TOK, N_HEADS * HEAD_DIM), BF16),
                   jax.ShapeDtypeStruct((T_TOK, KV_HEADS * HEAD_DIM), F32),
                   jax.ShapeDtypeStruct((T_TOK, KV_HEADS * HEAD_DIM), F32)),
        grid=(T_TOK // tm,),
        in_specs=[
            pl.BlockSpec((tm, D_MODEL), lambda i: (i, 0)),
            pl.BlockSpec((1, D_MODEL), lambda i: (0, 0)),
            pl.BlockSpec((None, 1, 6 * D_MODEL), lambda i: (layer * COND_ROWS + _cond_row(i, tm), 0, 0)),
            pl.BlockSpec(w_qkv.shape, lambda i: (0, 0)),
            pl.BlockSpec((1, HEAD_DIM), lambda i: (0, 0)),
            pl.BlockSpec((1, HEAD_DIM), lambda i: (0, 0)),
            pl.BlockSpec((tm, HEAD_DIM), pos_map),
            pl.BlockSpec((tm, HEAD_DIM), pos_map),
        ],
        out_specs=(pl.BlockSpec((tm, N_HEADS * HEAD_DIM), lambda i: (i, 0)),
                   pl.BlockSpec((tm, KV_HEADS * HEAD_DIM), lambda i: (i, 0)),
                   pl.BlockSpec((tm, KV_HEADS * HEAD_DIM), lambda i: (i, 0))),
        compiler_params=pltpu.CompilerParams(vmem_limit_bytes=VMEM_LIMIT),
        name="odd_qkv",
    )(x, gn, mods, w_qkv, q_norm, k_norm, cos_t, sin_t)


def _rope_tables():
    rows = DEC_SEQ // GRID_W
    row = jnp.repeat(jnp.arange(rows, dtype=F32), GRID_W)
    col = jnp.tile(jnp.arange(GRID_W, dtype=F32), rows)
    inv = ROPE_THETA ** (-jnp.arange(0, AXIS_DIM, 2, dtype=F32) / AXIS_DIM)
    ang = jnp.concatenate([row[:, None] * inv, col[:, None] * inv], axis=-1)
    cos_t = jnp.repeat(jnp.cos(ang), 2, axis=-1)
    sin = jnp.sin(ang)
    sin_t = jnp.stack([-sin, sin], axis=-1).reshape(DEC_SEQ, HEAD_DIM)
    return cos_t, sin_t


def _attn_kernel(*refs, has_cache):
    if has_cache:
        q_ref, k_ref, v_ref, ck_ref, cv_ref, o_ref = refs
    else:
        q_ref, k_ref, v_ref, o_ref = refs
    scale = HEAD_DIM ** -0.5
    k = k_ref[...].astype(BF16)
    v = v_ref[...].astype(BF16)
    if has_cache:
        ck = ck_ref[...].astype(BF16)
        cv = cv_ref[...].astype(BF16)
    for r in range(Q_PER_KV):
        cs = slice(r * HEAD_DIM, (r + 1) * HEAD_DIM)
        q = q_ref[:, cs]
        s = lax.dot_general(q, k, NT_DIMS, preferred_element_type=F32) * scale
        m = jnp.max(s, axis=-1, keepdims=True)
        if has_cache:
            sc = lax.dot_general(q, ck, NT_DIMS, preferred_element_type=F32) * scale
            m = jnp.maximum(m, jnp.max(sc, axis=-1, keepdims=True))
        p = jnp.exp(s - m)
        den = jnp.sum(p, axis=-1, keepdims=True)
        o = _dot(p.astype(BF16), v)
        if has_cache:
            pc = jnp.exp(sc - m)
            den = den + jnp.sum(pc, axis=-1, keepdims=True)
            o = o + _dot(pc.astype(BF16), cv)
        o_ref[:, cs] = (o / den).astype(BF16)


def _attn_call(q, k, v, cache_k, cache_v, seq_len, nseq, row0, tq):
    assert row0 % seq_len == 0 and seq_len % tq == 0
    has_cache = cache_k is not None
    kv0 = row0 // seq_len
    q0 = row0 // tq
    nq = seq_len // tq
    gw = Q_PER_KV * HEAD_DIM
    in_specs = [
        pl.BlockSpec((tq, gw), lambda b, g, i: (q0 + b * nq + i, g)),
        pl.BlockSpec((seq_len, HEAD_DIM), lambda b, g, i: (kv0 + b, g)),
        pl.BlockSpec((seq_len, HEAD_DIM), lambda b, g, i: (kv0 + b, g)),
    ]
    args = [q, k, v]
    if has_cache:
        in_specs += [pl.BlockSpec((PAST_LEN, HEAD_DIM), lambda b, g, i: (b, g)),
                     pl.BlockSpec((PAST_LEN, HEAD_DIM), lambda b, g, i: (b, g))]
        args += [cache_k, cache_v]
    return pl.pallas_call(
        functools.partial(_attn_kernel, has_cache=has_cache),
        out_shape=jax.ShapeDtypeStruct((nseq * seq_len, N_HEADS * HEAD_DIM), BF16),
        grid=(nseq, KV_HEADS, nq),
        in_specs=in_specs,
        out_specs=pl.BlockSpec((tq, gw), lambda b, g, i: (b * nq + i, g)),
        compiler_params=pltpu.CompilerParams(vmem_limit_bytes=VMEM_LIMIT),
        name=f"attn_len{seq_len}",
    )(*args)


def kernel(x_prompt, x_sample, state_s5_re, state_s5_im, state_gla, cache_k, cache_v, c, c_ctx, norm_mix, norm_mlp, w_ada, b_ada, w_mlp_in, w_mlp_out, w_in_e, w_out_e, s5_lambda_re, s5_lambda_im, s5_log_dt, s5_b_re, s5_b_im, s5_c_re, s5_c_im, s5_d, s5_w_glu, s5_b_glu, gla_w_gate2, gla_b_gate, gla_norm, w_qkv_o, w_o_o, q_norm, k_norm):
    x = jnp.concatenate([x_prompt.reshape(T_PROMPT, D_MODEL), x_sample.reshape(T_SAMPLE, D_MODEL)], axis=0)
    cond8 = jnp.concatenate([c_ctx[None, :], c, jnp.zeros((COND_ROWS - 1 - DEC_BATCH, D_MODEL), F32)], axis=0)
    mods = _ada_call(cond8, w_ada, b_ada).reshape(DEPTH * COND_ROWS, 1, 6 * D_MODEL)

    n_main = S5_WIDTH + 2 * GLA_QK + 2 * GLA_VW
    w_in = w_in_e[0]
    w_main = w_in[:, :n_main].astype(BF16)
    w_glr = jnp.pad(w_in[:, n_main:], ((0, 0), (0, 128 - 2 * GLA_RANK))).astype(BF16)
    w_gate = jnp.zeros((128, 2 * GLA_QK), F32)
    w_gate = w_gate.at[0:GLA_RANK, 0:GLA_QK].set(gla_w_gate2[0, 0])
    w_gate = w_gate.at[GLA_RANK:2 * GLA_RANK, GLA_QK:].set(gla_w_gate2[0, 1]).astype(BF16)
    b_gate = gla_b_gate[0].reshape(1, 2 * GLA_QK)
    z, g = _inproj_call(x, norm_mix[0:1], mods, 0, w_main, w_glr, w_gate, b_gate)

    mats = _s5_prep_call(s5_lambda_re[0], s5_lambda_im[0], s5_log_dt[0], s5_b_re[0], s5_b_im[0],
                         s5_c_re[0], s5_c_im[0])

    def state_rows(s):
        return jnp.transpose(s, (2, 0, 1, 3)).reshape(S5_GROUPS, DEC_BATCH, 2 * S5_STATE)

    h0 = jnp.concatenate([state_rows(state_s5_re[:, 0]), state_rows(state_s5_im[:, 0])], axis=-1)
    nsteps = S5_GROUPS // S5_GPB
    h0 = jnp.transpose(h0.reshape(nsteps, S5_GPB, DEC_BATCH, S5_W), (0, 2, 1, 3))
    y5, ns = _s5_call(z, mats, h0)
    ns = jnp.transpose(ns, (0, 2, 1, 3)).reshape(S5_GROUPS, BATCH, S5_W)

    def state_out(n):
        return jnp.transpose(n.reshape(S5_GROUPS, BATCH, 2, S5_STATE), (1, 2, 0, 3))[:, None]

    new_s5_re = state_out(ns[:, :, :2 * S5_STATE])
    new_s5_im = state_out(ns[:, :, 2 * S5_STATE:])

    gn_gla = gla_norm[0].reshape(1, GLA_DV)
    gla_p, sfin = _gla_call(z, g, gn_gla, None, SEQ, BATCH, 0)
    s0 = state_gla[:, 0].reshape(DEC_BATCH, 2, GLA_QK, GLA_DV)
    gla_s, _ = _gla_call(z, g, gn_gla, s0, DEC_SEQ, DEC_BATCH, T_PROMPT)
    gla = jnp.concatenate([gla_p, gla_s], axis=0)
    new_gla = sfin.reshape(BATCH, 1, 2, GLA_HEADS, GLA_DK, GLA_DV)

    x = _even_out_call(x, y5, z, s5_d[0].reshape(1, S5_WIDTH), s5_w_glu[0].astype(BF16),
                       s5_b_glu[0].reshape(1, S5_WIDTH), gla, w_out_e[0].astype(BF16), mods, 0,
                       norm_mlp[0:1], w_mlp_in[0].astype(BF16), w_mlp_out[0].astype(BF16))

    cos_t, sin_t = _rope_tables()
    q, k, v = _qkv_call(x, norm_mix[1:2], mods, 1, w_qkv_o[0].astype(BF16), q_norm[0].reshape(1, HEAD_DIM),
                        k_norm[0].reshape(1, HEAD_DIM), cos_t, sin_t)
    att_p = _attn_call(q, k, v, None, None, SEQ, BATCH, 0, SEQ)
    ck = cache_k[:, 0].reshape(DEC_BATCH * PAST_LEN, KV_HEADS * HEAD_DIM)
    cv = cache_v[:, 0].reshape(DEC_BATCH * PAST_LEN, KV_HEADS * HEAD_DIM)
    att_s = _attn_call(q, k, v, ck, cv, DEC_SEQ, DEC_BATCH, T_PROMPT, 256)
    att = jnp.concatenate([att_p, att_s], axis=0)
    x = _odd_out_call(x, att, w_o_o[0].astype(BF16), mods, 1, norm_mlp[1:2],
                      w_mlp_in[1].astype(BF16), w_mlp_out[1].astype(BF16))

    new_k = k[:T_PROMPT].reshape(BATCH, 1, SEQ, KV_HEADS, HEAD_DIM)
    new_v = v[:T_PROMPT].reshape(BATCH, 1, SEQ, KV_HEADS, HEAD_DIM)
    y_prompt = x[:T_PROMPT].reshape(BATCH, SEQ, D_MODEL)
    y_sample = x[T_PROMPT:].reshape(DEC_BATCH, DEC_SEQ, D_MODEL)
    return (y_prompt, y_sample, new_s5_re, new_s5_im, new_gla, new_k, new_v)
```

```python
import functools
import math

import jax
import jax.numpy as jnp
import numpy as np
from jax import lax
from jax.experimental import pallas as pl
from jax.experimental.pallas import tpu as pltpu

F32 = jnp.float32
BF16 = jnp.bfloat16

D_MODEL = 1024
BATCH = 16
SEQ = 256
DEPTH = 2
DEC_BATCH = 4
DEC_SEQ = 1024
PAST_LEN = 512
GRID_W = 64
S5_WIDTH = 512
S5_GROUP_CH = 16
S5_GROUPS = 32
S5_STATE = 64
GLA_HEADS = 4
GLA_VW = 512
GLA_DV = 128
GLA_DK = 64
GLA_QK = 256
GLA_RANK = 16
GLA_TAU = 16.0
GLA_CHUNK = 64
GLA_CPB = 4
GLA_BLK = GLA_CPB * GLA_CHUNK
HEAD_DIM = 128
N_HEADS = 8
KV_HEADS = 2
Q_PER_KV = N_HEADS // KV_HEADS
AXIS_DIM = 64
ROPE_THETA = 10000.0
D_FF = 4096
EPS = 1e-6

T_PROMPT = BATCH * SEQ
T_SAMPLE = DEC_BATCH * DEC_SEQ
T_TOK = T_PROMPT + T_SAMPLE
COND_ROWS = 8
COND_SPAN = 1024
PROMPT_SPANS = T_PROMPT // COND_SPAN

S5_Q = 16
S5_W = S5_Q * S5_GROUP_CH
S5_GPB = 128 // S5_GROUP_CH
S5_ROWS = T_TOK // S5_Q
S5_PROMPT_ROWS = T_PROMPT // S5_Q
S5_PROMPT_CHUNKS = SEQ // S5_Q
S5_SAMPLE_CHUNKS = DEC_SEQ // S5_Q

VMEM_LIMIT = 56 * 1024 * 1024

NT_DIMS = (((1,), (1,)), ((), ()))
TN_DIMS = (((0,), (0,)), ((), ()))


def _cond_row(i, tile):
    return jnp.maximum((i * tile) // COND_SPAN - (PROMPT_SPANS - 1), 0)


def _norm_mod(x, gain, shift, scale):
    y = x * lax.rsqrt(jnp.mean(x * x, axis=-1, keepdims=True) + EPS)
    return (y * gain) * (1.0 + scale) + shift


def _dot(a, b):
    return jnp.dot(a, b, preferred_element_type=F32)


def _ada_kernel(cond_ref, w_ref, b_ref, o_ref):
    s = jax.nn.silu(cond_ref[...]).astype(BF16)
    o_ref[...] = _dot(s, w_ref[...].astype(BF16)) + b_ref[...]


def _ada_call(cond8, w_ada, b_ada):
    tn = 2048
    nj = 6 * D_MODEL // tn
    return pl.pallas_call(
        _ada_kernel,
        out_shape=jax.ShapeDtypeStruct((DEPTH, COND_ROWS, 6 * D_MODEL), F32),
        grid=(DEPTH, nj),
        in_specs=[
            pl.BlockSpec((COND_ROWS, D_MODEL), lambda l, j: (0, 0)),
            pl.BlockSpec((None, D_MODEL, tn), lambda l, j: (l, 0, j)),
            pl.BlockSpec((None, 1, tn), lambda l, j: (l, 0, j)),
        ],
        out_specs=pl.BlockSpec((None, COND_ROWS, tn), lambda l, j: (l, 0, j)),
        compiler_params=pltpu.CompilerParams(vmem_limit_bytes=VMEM_LIMIT),
        name="ada_mod",
    )(cond8, w_ada, b_ada.reshape(DEPTH, 1, 6 * D_MODEL))


def _token_specs(tile):
    n_prompt = T_PROMPT // tile
    return [pl.BlockSpec((tile, D_MODEL), lambda i: (jnp.minimum(i, n_prompt - 1), 0)),
            pl.BlockSpec((tile, D_MODEL), lambda i: (jnp.maximum(i - n_prompt, 0), 0))]


def _token_tile(xp_ref, xs_ref, tile):
    return jnp.where(pl.program_id(0) < T_PROMPT // tile, xp_ref[...], xs_ref[...])


def _inproj_kernel(xp_ref, xs_ref, gn_ref, m_ref, w_ref, wglr_ref, wg_ref, bg_ref, z_ref, g_ref, *, tile):
    x = _token_tile(xp_ref, xs_ref, tile)
    h = _norm_mod(x, gn_ref[...], m_ref[:, 0:D_MODEL], m_ref[:, D_MODEL:2 * D_MODEL]).astype(BF16)
    z_ref[...] = _dot(h, w_ref[...])
    glr = _dot(h, wglr_ref[...]).astype(BF16)
    pre = _dot(glr, wg_ref[...]) + bg_ref[...]
    g_ref[...] = jax.nn.log_sigmoid(pre) * (1.0 / GLA_TAU)


def _inproj_call(xp, xs, gn, mods, layer, w_main, w_glr, w_gate, b_gate):
    tm = 512
    nz = w_main.shape[1]
    return pl.pallas_call(
        functools.partial(_inproj_kernel, tile=tm),
        out_shape=(jax.ShapeDtypeStruct((T_TOK, nz), F32),
                   jax.ShapeDtypeStruct((T_TOK, 2 * GLA_QK), F32)),
        grid=(T_TOK // tm,),
        in_specs=_token_specs(tm) + [
            pl.BlockSpec((1, D_MODEL), lambda i: (0, 0)),
            pl.BlockSpec((None, 1, 6 * D_MODEL), lambda i: (layer * COND_ROWS + _cond_row(i, tm), 0, 0)),
            pl.BlockSpec((D_MODEL, nz), lambda i: (0, 0)),
            pl.BlockSpec((D_MODEL, 128), lambda i: (0, 0)),
            pl.BlockSpec((128, 2 * GLA_QK), lambda i: (0, 0)),
            pl.BlockSpec((1, 2 * GLA_QK), lambda i: (0, 0)),
        ],
        out_specs=(pl.BlockSpec((tm, nz), lambda i: (i, 0)),
                   pl.BlockSpec((tm, 2 * GLA_QK), lambda i: (i, 0))),
        compiler_params=pltpu.CompilerParams(vmem_limit_bytes=VMEM_LIMIT),
        name="even_inproj",
    )(xp, xs, gn, mods, w_main, w_glr, w_gate, b_gate)


def _s5_prep_kernel(lre_ref, lim_ref, ldt_ref, btre_ref, btim_ref, cre_ref, cim_ref, ccf_ref, ccb_ref,
                    t_ref, bq_ref, cqt_ref, be_ref, a_ref, t_scr, dd_scr):
    lre = lre_ref[...]
    lim = lim_ref[...]
    dt = jnp.exp(ldt_ref[...])
    a = lre * dt
    th = lim * dt

    def lam_pow(k):
        mag = jnp.exp(k * a)
        return mag * jnp.cos(k * th), mag * jnp.sin(k * th)

    lb_re, lb_im = lam_pow(1.0)
    nr = lb_re - 1.0
    den = lre * lre + lim * lim
    cf_re = (nr * lre + lb_im * lim) / den
    cf_im = (lb_im * lre - nr * lim) / den
    bt_re = btre_ref[...]
    bt_im = btim_ref[...]
    bb_re = jnp.tile(cf_re * bt_re - cf_im * bt_im, (S5_Q, 1))
    bb_im = jnp.tile(cf_re * bt_im + cf_im * bt_re, (S5_Q, 1))

    shape = (S5_W, 128)
    pos = lax.shift_right_logical(lax.broadcasted_iota(jnp.int32, shape, 0), 4)
    is_f = lax.broadcasted_iota(jnp.int32, shape, 1) < S5_STATE
    posq = lax.broadcasted_iota(jnp.int32, (S5_Q, 128), 0).astype(F32)
    is_fq = lax.broadcasted_iota(jnp.int32, (S5_Q, 128), 1) < S5_STATE

    def per_channel(tbl):
        return jnp.broadcast_to(tbl[:, None, :], (S5_Q, S5_GROUP_CH, 128)).reshape(shape)

    p_re, p_im = map(per_channel, lam_pow(jnp.where(is_fq, (S5_Q - 1.0) - posq, posq)))
    w_re = p_re * bb_re - p_im * bb_im
    w_im = p_re * bb_im + p_im * bb_re
    bq = jnp.concatenate([w_re, w_im], axis=1)
    bqt = jnp.transpose(bq)
    bq_ref[...] = bqt.astype(BF16)

    edge = pos == jnp.where(is_f, 0, S5_Q - 1)
    be = jnp.concatenate([jnp.where(edge, bb_re, 0.0), jnp.where(edge, bb_im, 0.0)], axis=1)
    be_ref[...] = jnp.transpose(be).astype(BF16)

    q_re, q_im = map(per_channel, lam_pow(jnp.where(is_fq, posq + 1.0, S5_Q - posq)))
    ct_re = jnp.tile(cre_ref[...], (S5_Q, 1))
    ct_im = jnp.tile(cim_ref[...], (S5_Q, 1))
    g_re = q_re * ct_re - q_im * ct_im
    g_im = q_re * ct_im + q_im * ct_re
    cqt_ref[...] = jnp.concatenate([g_re, -g_im], axis=1).astype(BF16)

    a_re, a_im = lam_pow(float(S5_Q))
    a_ref[...] = jnp.concatenate([a_re, a_im], axis=1)

    kf = jnp.dot(ccf_ref[...], bqt, precision=lax.Precision.HIGHEST, preferred_element_type=F32)
    kb = jnp.dot(ccb_ref[...], bqt, precision=lax.Precision.HIGHEST, preferred_element_type=F32)
    gch = S5_GROUP_CH
    lo = S5_W - gch
    dd_scr[:, 0:S5_W] = kf
    dd_scr[:, lo:lo + S5_W] = kb
    dd_scr[:, lo:S5_W] = kf[:, lo:S5_W] + kb[:, 0:gch]
    for t in range(S5_Q):
        c0 = (S5_Q - 1 - t) * gch
        t_scr[t * gch:(t + 1) * gch, :] = dd_scr[:, c0:c0 + S5_W]
    t_ref[...] = t_scr[...].astype(BF16)


def _s5_prep_call(lam_re, lam_im, log_dt, b_re, b_im, c_re, c_im):
    def fb(p):
        return jnp.transpose(p, (1, 0, 2)).reshape(S5_GROUPS, 1, 2 * S5_STATE)

    def dup(p):
        return jnp.concatenate([p, p], axis=-1)

    ldt = fb(jnp.broadcast_to(log_dt[:, :, None], (2, S5_GROUPS, S5_STATE)))
    bt_re = dup(jnp.transpose(b_re, (0, 2, 1)))
    bt_im = dup(jnp.transpose(b_im, (0, 2, 1)))
    zero = jnp.zeros_like(c_re)
    ccf = jnp.concatenate([c_re, zero, -c_im, zero], axis=-1)
    ccb = jnp.concatenate([zero, c_re, zero, -c_im], axis=-1)

    row = pl.BlockSpec((None, 1, 128), lambda g: (g, 0, 0))
    mat16 = pl.BlockSpec((None, S5_GROUP_CH, 128), lambda g: (g, 0, 0))
    mat16w = pl.BlockSpec((None, S5_GROUP_CH, S5_W), lambda g: (g, 0, 0))
    sq = pl.BlockSpec((None, S5_W, S5_W), lambda g: (g, 0, 0))
    sq_shape = jax.ShapeDtypeStruct((S5_GROUPS, S5_W, S5_W), BF16)
    return pl.pallas_call(
        _s5_prep_kernel,
        out_shape=(sq_shape, sq_shape, sq_shape, sq_shape,
                   jax.ShapeDtypeStruct((S5_GROUPS, 1, S5_W), F32)),
        grid=(S5_GROUPS,),
        in_specs=[row, row, row, mat16, mat16, mat16, mat16, mat16w, mat16w],
        out_specs=(sq, sq, sq, sq, pl.BlockSpec((None, 1, S5_W), lambda g: (g, 0, 0))),
        scratch_shapes=[pltpu.VMEM((S5_W, S5_W), F32), pltpu.VMEM((S5_GROUP_CH, 2 * S5_W), F32)],
        name="s5_prep",
    )(fb(lam_re), fb(lam_im), ldt, bt_re, bt_im, dup(c_re), dup(c_im), ccf, ccb)


def _s5_kernel(u_ref, tt_ref, bqt_ref, cqt_ref, bet_ref, a_ref, h0_ref, y_ref, ns_ref,
               ut_scr, x_scr, spf_scr, spb_scr, ne_scr, yt_scr, xt_scr):
    gch = S5_GROUP_CH
    for s in range(S5_Q):
        rows = u_ref[pl.ds(s, S5_ROWS, stride=S5_Q), :]
        rows_t = jnp.transpose(rows).astype(BF16)
        for gl in range(S5_GPB):
            ut_scr[gl, s * gch:(s + 1) * gch, :] = rows_t[gl * gch:(gl + 1) * gch, :]

    for gl in range(S5_GPB):
        ut = ut_scr[gl]
        xt_scr[...] = _dot(bqt_ref[gl], ut)
        x = jnp.transpose(xt_scr[...])
        xt_scr[:, 0:S5_PROMPT_ROWS] = _dot(bet_ref[gl], ut[:, 0:S5_PROMPT_ROWS])
        ne = jnp.transpose(xt_scr[:, 0:S5_PROMPT_ROWS])
        for part in range(2):
            x_scr[part, pl.ds(gl, S5_ROWS, stride=S5_GPB), :] = x[:, part * 128:(part + 1) * 128]
            ne_scr[part, pl.ds(gl, S5_PROMPT_ROWS, stride=S5_GPB), :] = ne[:, part * 128:(part + 1) * 128]

    is_f = lax.broadcasted_iota(jnp.int32, (1, 128), 1) < S5_STATE
    a_re = a_ref[:, 0:128]
    a_im = a_ref[:, 128:256]

    def tile(row):
        return pl.ds(pl.multiple_of(row * S5_GPB, S5_GPB), S5_GPB)

    def scan(base, nseq, nchunk, s_init):
        def body(i, state):
            new = []
            for b in range(nseq):
                s_re, s_im = state[b]
                rows_f = tile(base + b * nchunk + i)
                rows_b = tile(base + b * nchunk + (nchunk - 1 - i))
                spf_scr[0, rows_f, :] = s_re
                spf_scr[1, rows_f, :] = s_im
                spb_scr[0, rows_b, :] = s_re
                spb_scr[1, rows_b, :] = s_im
                x_re = jnp.where(is_f, x_scr[0, rows_f, :], x_scr[0, rows_b, :])
                x_im = jnp.where(is_f, x_scr[1, rows_f, :], x_scr[1, rows_b, :])
                new.append((a_re * s_re - a_im * s_im + x_re, a_re * s_im + a_im * s_re + x_im))
            return tuple(new)

        lax.fori_loop(0, nchunk, body, tuple(s_init))

    zero = jnp.zeros((S5_GPB, 128), F32)
    scan(0, BATCH, S5_PROMPT_CHUNKS, [(zero, zero)] * BATCH)
    scan(S5_PROMPT_ROWS, DEC_BATCH, S5_SAMPLE_CHUNKS,
         [(h0_ref[b, :, 0:128], h0_ref[b, :, 128:256]) for b in range(DEC_BATCH)])

    for b in range(BATCH):
        first = pl.ds(b * S5_PROMPT_CHUNKS * S5_GPB, S5_GPB)
        last = pl.ds(((b + 1) * S5_PROMPT_CHUNKS - 1) * S5_GPB, S5_GPB)
        for part in range(2):
            ns_ref[b, :, part * 128:(part + 1) * 128] = jnp.where(is_f, ne_scr[part, first, :], ne_scr[part, last, :])

    for gl in range(S5_GPB):
        rows = pl.ds(gl, S5_ROWS, stride=S5_GPB)
        carried = jnp.concatenate([jnp.where(is_f, spf_scr[p, rows, :], spb_scr[p, rows, :]) for p in range(2)],
                                  axis=1).astype(BF16)
        yt = _dot(tt_ref[gl], ut_scr[gl]) + lax.dot_general(cqt_ref[gl], carried, NT_DIMS,
                                                            preferred_element_type=F32)
        for t in range(S5_Q):
            yt_scr[t, gl * gch:(gl + 1) * gch, :] = yt[t * gch:(t + 1) * gch, :]
    for t in range(S5_Q):
        y_ref[pl.ds(t, S5_ROWS, stride=S5_Q), :] = jnp.transpose(yt_scr[t])


def _s5_call(z, mats, h0):
    tt_m, bqt_m, cqt_m, bet_m, a_m = mats
    nsteps = S5_GROUPS // S5_GPB
    sq = pl.BlockSpec((S5_GPB, S5_W, S5_W), lambda g: (g, 0, 0))
    state_scr = pltpu.VMEM((2, S5_ROWS * S5_GPB, 128), F32)
    return pl.pallas_call(
        _s5_kernel,
        out_shape=(jax.ShapeDtypeStruct((T_TOK, S5_WIDTH), F32),
                   jax.ShapeDtypeStruct((nsteps, BATCH, S5_GPB, S5_W), F32)),
        grid=(nsteps,),
        in_specs=[
            pl.BlockSpec((T_TOK, 128), lambda g: (0, g)),
            sq, sq, sq, sq,
            pl.BlockSpec((S5_GPB, S5_W), lambda g: (g, 0)),
            pl.BlockSpec((None, DEC_BATCH, S5_GPB, S5_W), lambda g: (g, 0, 0, 0)),
        ],
        out_specs=(pl.BlockSpec((T_TOK, 128), lambda g: (0, g)),
                   pl.BlockSpec((None, BATCH, S5_GPB, S5_W), lambda g: (g, 0, 0, 0))),
        scratch_shapes=[pltpu.VMEM((S5_GPB, S5_W, S5_ROWS), BF16), state_scr, state_scr, state_scr,
                        pltpu.VMEM((2, S5_PROMPT_ROWS * S5_GPB, 128), F32),
                        pltpu.VMEM((S5_Q, 128, S5_ROWS), F32), pltpu.VMEM((S5_W, S5_ROWS), F32)],
        compiler_params=pltpu.CompilerParams(vmem_limit_bytes=VMEM_LIMIT),
        name="s5_scan",
    )(z, tt_m, bqt_m, cqt_m, bet_m, a_m.reshape(S5_GROUPS, S5_W), h0)


def _split_bf16(x):
    hi = x.astype(BF16)
    r1 = x - hi.astype(F32)
    mid = r1.astype(BF16)
    lo = (r1 - mid.astype(F32)).astype(BF16)
    return hi, mid, lo


def _gla_kernel(*refs, seq_len, has_s0, has_prev):
    q_ref, k_ref, v_ref, gf_ref, gb_ref, r_ref, gn_ref = refs[:7]
    s0_ref = refs[7] if has_s0 else None
    o_ref, sfin_ref, oi_scr, qd_scr, kv_scr, dec_scr, ss_scr = refs[7 + has_s0 + has_prev:]
    nblk = seq_len // GLA_BLK
    nchunk = seq_len // GLA_CHUNK
    cl = GLA_CHUNK
    ti = lax.broadcasted_iota(jnp.int32, (GLA_BLK, GLA_BLK), 0)
    si = lax.broadcasted_iota(jnp.int32, (GLA_BLK, GLA_BLK), 1)
    same = lax.shift_right_logical(ti, 6) == lax.shift_right_logical(si, 6)
    keep = (same & (ti >= si), same & (ti <= si))
    tri = tuple(kp.astype(BF16) for kp in keep)
    lane_head = lax.shift_right_logical(lax.broadcasted_iota(jnp.int32, (cl, GLA_QK), 1), 6)
    zeros_v = jnp.zeros((cl, GLA_DV), BF16)
    heads = [(slice(h * GLA_DK, (h + 1) * GLA_DK), slice(h * GLA_DV, (h + 1) * GLA_DV)) for h in range(GLA_HEADS)]

    for j in range(nblk):
        rows = slice(j * GLA_BLK, (j + 1) * GLA_BLK)
        q = q_ref[rows, :] * (GLA_DK ** -0.5)
        k = k_ref[rows, :]
        v = v_ref[rows, :].astype(BF16)
        qd, kd, k2t = [], [], []
        for d, g_ref in enumerate((gf_ref, gb_ref)):
            b = sum(_dot(tri[d], part) for part in _split_bf16(g_ref[rows, :]))
            last = cl - 1 if d == 0 else 0
            b_last = [b[c * cl + last:c * cl + last + 1] for c in range(GLA_CPB)]
            bl = jnp.concatenate([jnp.broadcast_to(x, (cl, GLA_QK)) for x in b_last], axis=0)
            qd_d = (q * jnp.exp(b)).astype(BF16)
            qd_scr[d, rows, :] = qd_d
            qd.append(qd_d)
            kd.append((k * jnp.exp(-b)).astype(BF16))
            k2t.append(jnp.transpose(k * jnp.exp(bl - b)).astype(BF16))
            for c in range(GLA_CPB):
                dec_scr[d, j * GLA_CPB + c] = jnp.exp(jnp.transpose(jnp.broadcast_to(b_last[c], (GLA_DV, GLA_QK))))
        for h, (ks, vs) in enumerate(heads):
            att = [jnp.where(keep[d], lax.dot_general(qd[d][:, ks], kd[d][:, ks], NT_DIMS,
                                                      preferred_element_type=F32), 0.0) for d in range(2)]
            oi_scr[rows, vs] = _dot((att[0] + att[1]).astype(BF16), v[:, vs])
            vh = v[:, vs]
            vexp = jnp.concatenate(
                [jnp.concatenate([vh[c * cl:(c + 1) * cl] if c2 == c else zeros_v for c2 in range(GLA_CPB)], axis=1)
                 for c in range(GLA_CPB)], axis=0)
            for d in range(2):
                kv_scr[d, j, h] = _dot(k2t[d][ks, :], vexp)

    for d in range(2):
        s = s0_ref[d] if has_s0 else jnp.zeros((GLA_QK, GLA_DV), F32)
        for cg in (range(nchunk) if d == 0 else range(nchunk - 1, -1, -1)):
            j, c = divmod(cg, GLA_CPB)
            ss_scr[d, cg] = s.astype(BF16)
            kv = jnp.concatenate([kv_scr[d, j, h, :, c * GLA_DV:(c + 1) * GLA_DV] for h in range(GLA_HEADS)], axis=0)
            s = s * dec_scr[d, cg] + kv
        sfin_ref[d] = s

    for cg in range(nchunk):
        rows = slice(cg * cl, (cg + 1) * cl)
        inter = []
        for d in range(2):
            qc = qd_scr[d, rows, :]
            qstack = jnp.concatenate([jnp.where(lane_head == h, qc, jnp.zeros_like(qc)) for h in range(GLA_HEADS)],
                                     axis=0)
            inter.append(_dot(qstack, ss_scr[d, cg]))
        gate = jax.nn.silu(r_ref[rows, :])
        for h, (ks, vs) in enumerate(heads):
            hr = slice(h * cl, (h + 1) * cl)
            oh = oi_scr[rows, vs] + inter[0][hr] + inter[1][hr]
            oh = oh * lax.rsqrt(jnp.mean(oh * oh, axis=-1, keepdims=True) + EPS) * gn_ref[...]
            o_ref[rows, vs] = oh * gate[:, vs]


def _gla_call(z, g, gla_norm, s0, seq_len, nseq, row0, prev=None):
    assert row0 % seq_len == 0
    r0 = row0 // seq_len
    has_s0 = s0 is not None
    has_prev = prev is not None
    qk_off = S5_WIDTH // GLA_QK
    v_off = (S5_WIDTH + 2 * GLA_QK) // GLA_VW
    in_specs = [
        pl.BlockSpec((seq_len, GLA_QK), lambda i: (r0 + i, qk_off)),
        pl.BlockSpec((seq_len, GLA_QK), lambda i: (r0 + i, qk_off + 1)),
        pl.BlockSpec((seq_len, GLA_VW), lambda i: (r0 + i, v_off)),
        pl.BlockSpec((seq_len, GLA_QK), lambda i: (r0 + i, 0)),
        pl.BlockSpec((seq_len, GLA_QK), lambda i: (r0 + i, 1)),
        pl.BlockSpec((seq_len, GLA_VW), lambda i: (r0 + i, v_off + 1)),
        pl.BlockSpec((1, GLA_DV), lambda i: (0, 0)),
    ]
    args = [z, z, z, g, g, z, gla_norm]
    if has_s0:
        in_specs.append(pl.BlockSpec((None, 2, GLA_QK, GLA_DV), lambda i: (i, 0, 0, 0)))
        args.append(s0)
    if has_prev:
        in_specs.append(pl.BlockSpec(memory_space=pl.ANY))
        args.append(prev)
    return pl.pallas_call(
        functools.partial(_gla_kernel, seq_len=seq_len, has_s0=has_s0, has_prev=has_prev),
        out_shape=(jax.ShapeDtypeStruct((T_TOK, GLA_VW), F32),
                   jax.ShapeDtypeStruct((nseq, 2, GLA_QK, GLA_DV), F32)),
        grid=(nseq,),
        in_specs=in_specs,
        out_specs=(pl.BlockSpec((seq_len, GLA_VW), lambda i: (r0 + i, 0)),
                   pl.BlockSpec((None, 2, GLA_QK, GLA_DV), lambda i: (i, 0, 0, 0))),
        input_output_aliases={len(args) - 1: 0} if has_prev else {},
        scratch_shapes=[
            pltpu.VMEM((seq_len, GLA_VW), F32),
            pltpu.VMEM((2, seq_len, GLA_QK), BF16),
            pltpu.VMEM((2, seq_len // GLA_BLK, GLA_HEADS, GLA_DK, GLA_CPB * GLA_DV), F32),
            pltpu.VMEM((2, seq_len // GLA_CHUNK, GLA_QK, GLA_DV), F32),
            pltpu.VMEM((2, seq_len // GLA_CHUNK, GLA_QK, GLA_DV), BF16),
        ],
        compiler_params=pltpu.CompilerParams(vmem_limit_bytes=VMEM_LIMIT),
        name=f"gla_len{seq_len}",
    )(*args)


def _mlp_tail(x, mix, m_ref, gn2_ref, w1_ref, w2_ref):
    y1 = x + m_ref[:, 2 * D_MODEL:3 * D_MODEL] * mix
    h = _norm_mod(y1, gn2_ref[...], m_ref[:, 3 * D_MODEL:4 * D_MODEL], m_ref[:, 4 * D_MODEL:5 * D_MODEL]).astype(BF16)
    tf = 512
    acc = jnp.zeros(y1.shape, F32)
    for c in range(D_FF // tf):
        a = _dot(h, w1_ref[:, c * tf:(c + 1) * tf])
        a = jnp.square(jnp.maximum(a, 0.0)).astype(BF16)
        acc = acc + _dot(a, w2_ref[c * tf:(c + 1) * tf, :])
    return y1 + m_ref[:, 5 * D_MODEL:6 * D_MODEL] * acc


def _even_out_kernel(xp_ref, xs_ref, y5_ref, u_ref, dskip_ref, wglu_ref, bglu_ref, gla_ref, wout_ref,
                     m_ref, gn2_ref, w1_ref, w2_ref, o_ref):
    ys = y5_ref[...] + u_ref[...] * dskip_ref[...]
    gl = jax.nn.gelu(ys)
    s5o = gl * jax.nn.sigmoid(_dot(gl.astype(BF16), wglu_ref[...]) + bglu_ref[...])
    mix = _dot(s5o.astype(BF16), wout_ref[0:S5_WIDTH, :]) + _dot(gla_ref[...].astype(BF16), wout_ref[S5_WIDTH:, :])
    o_ref[...] = _mlp_tail(_token_tile(xp_ref, xs_ref, _OUT_TM), mix, m_ref, gn2_ref, w1_ref, w2_ref)


def _odd_out_kernel(x_ref, att_ref, wo_ref, m_ref, gn2_ref, w1_ref, w2_ref, op_ref, os_ref):
    mix = _dot(att_ref[...], wo_ref[...])
    y = _mlp_tail(x_ref[...], mix, m_ref, gn2_ref, w1_ref, w2_ref)
    is_prompt = pl.program_id(0) < T_PROMPT // _OUT_TM

    @pl.when(is_prompt)
    def _():
        op_ref[...] = y

    @pl.when(jnp.logical_not(is_prompt))
    def _():
        os_ref[...] = y


_OUT_TM = 512


def _const_spec(shape):
    return pl.BlockSpec(shape, lambda i: (0,) * len(shape), pipeline_mode=pl.Buffered(1))


def _tail_specs(layer):
    tm = _OUT_TM
    return [
        pl.BlockSpec((None, 1, 6 * D_MODEL), lambda i: (layer * COND_ROWS + _cond_row(i, tm), 0, 0)),
        _const_spec((1, D_MODEL)),
        _const_spec((D_MODEL, D_FF)),
        _const_spec((D_FF, D_MODEL)),
    ]


def _even_out_call(xp, xs, y5, z, d_skip, w_glu, b_glu, gla, w_out, mods, layer, gn2, w1, w2):
    tm = _OUT_TM
    return pl.pallas_call(
        _even_out_kernel,
        out_shape=jax.ShapeDtypeStruct((T_TOK, D_MODEL), F32),
        grid=(T_TOK // tm,),
        in_specs=_token_specs(tm) + [
            pl.BlockSpec((tm, S5_WIDTH), lambda i: (i, 0)),
            pl.BlockSpec((tm, S5_WIDTH), lambda i: (i, 0)),
            _const_spec((1, S5_WIDTH)),
            _const_spec((S5_WIDTH, S5_WIDTH)),
            _const_spec((1, S5_WIDTH)),
            pl.BlockSpec((tm, GLA_VW), lambda i: (i, 0)),
            _const_spec((S5_WIDTH + GLA_VW, D_MODEL)),
        ] + _tail_specs(layer),
        out_specs=pl.BlockSpec((tm, D_MODEL), lambda i: (i, 0)),
        compiler_params=pltpu.CompilerParams(vmem_limit_bytes=VMEM_LIMIT),
        name="even_out_mlp",
    )(xp, xs, y5, z, d_skip, w_glu, b_glu, gla, w_out, mods, gn2, w1, w2)


def _odd_out_call(x, att, w_o, mods, layer, gn2, w1, w2):
    tm = _OUT_TM
    n_prompt = T_PROMPT // tm
    return pl.pallas_call(
        _odd_out_kernel,
        out_shape=(jax.ShapeDtypeStruct((T_PROMPT, D_MODEL), F32),
                   jax.ShapeDtypeStruct((T_SAMPLE, D_MODEL), F32)),
        grid=(T_TOK // tm,),
        in_specs=[
            pl.BlockSpec((tm, D_MODEL), lambda i: (i, 0)),
            pl.BlockSpec((tm, D_MODEL), lambda i: (i, 0)),
            _const_spec((D_MODEL, D_MODEL)),
        ] + _tail_specs(layer),
        out_specs=(pl.BlockSpec((tm, D_MODEL), lambda i: (jnp.minimum(i, n_prompt - 1), 0)),
                   pl.BlockSpec((tm, D_MODEL), lambda i: (jnp.maximum(i - n_prompt, 0), 0))),
        compiler_params=pltpu.CompilerParams(vmem_limit_bytes=VMEM_LIMIT),
        name="odd_out_mlp",
    )(x, att, w_o, mods, gn2, w1, w2)


def _qkv_kernel(x_ref, gn_ref, m_ref, w_ref, qn_ref, kn_ref, cos_ref, sin_ref,
                q_ref, kb_ref, vb_ref, k32_ref, v32_ref, *, tile):
    h = _norm_mod(x_ref[...], gn_ref[...], m_ref[:, 0:D_MODEL], m_ref[:, D_MODEL:2 * D_MODEL]).astype(BF16)
    z = _dot(h, w_ref[...])
    v = z[:, (N_HEADS + KV_HEADS) * HEAD_DIM:]
    vb_ref[...] = v.astype(BF16)
    even_lane = (lax.broadcasted_iota(jnp.int32, (1, HEAD_DIM), 1) & 1) == 0

    def heads(rope):
        for hd in range(N_HEADS + KV_HEADS):
            xh = z[:, hd * HEAD_DIM:(hd + 1) * HEAD_DIM]
            gain = qn_ref[...] if hd < N_HEADS else kn_ref[...]
            xh = xh * lax.rsqrt(jnp.mean(xh * xh, axis=-1, keepdims=True) + EPS) * gain
            if rope:
                partner = jnp.where(even_lane, pltpu.roll(xh, HEAD_DIM - 1, 1), pltpu.roll(xh, 1, 1))
                xh = xh * cos_ref[...] + partner * sin_ref[...]
            if hd < N_HEADS:
                q_ref[:, hd * HEAD_DIM:(hd + 1) * HEAD_DIM] = xh.astype(BF16)
            else:
                cols = slice((hd - N_HEADS) * HEAD_DIM, (hd - N_HEADS + 1) * HEAD_DIM)
                kb_ref[:, cols] = xh.astype(BF16)
                if not rope:
                    k32_ref[:, cols] = xh

    is_sample = pl.program_id(0) >= T_PROMPT // tile

    @pl.when(is_sample)
    def _():
        heads(True)

    @pl.when(jnp.logical_not(is_sample))
    def _():
        heads(False)
        v32_ref[...] = v


def _qkv_call(x, gn, mods, layer, w_qkv, q_norm, k_norm, cos_t, sin_t):
    tm = 512
    pos_tiles = DEC_SEQ // tm
    n_prompt = T_PROMPT // tm
    kvw = KV_HEADS * HEAD_DIM

    def pos_map(i):
        return (jnp.maximum(i - n_prompt, 0) % pos_tiles, 0)

    def prompt_map(i):
        return (jnp.minimum(i, n_prompt - 1), 0)

    return pl.pallas_call(
        functools.partial(_qkv_kernel, tile=tm),
        out_shape=(jax.ShapeDtypeStruct((T_TOK, N_HEADS * HEAD_DIM), BF16),
                   jax.ShapeDtypeStruct((T_TOK, kvw), BF16),
                   jax.ShapeDtypeStruct((T_TOK, kvw), BF16),
                   jax.ShapeDtypeStruct((T_PROMPT, kvw), F32),
                   jax.ShapeDtypeStruct((T_PROMPT, kvw), F32)),
        grid=(T_TOK // tm,),
        in_specs=[
            pl.BlockSpec((tm, D_MODEL), lambda i: (i, 0)),
            pl.BlockSpec((1, D_MODEL), lambda i: (0, 0)),
            pl.BlockSpec((None, 1, 6 * D_MODEL), lambda i: (layer * COND_ROWS + _cond_row(i, tm), 0, 0)),
            pl.BlockSpec(w_qkv.shape, lambda i: (0, 0)),
            pl.BlockSpec((1, HEAD_DIM), lambda i: (0, 0)),
            pl.BlockSpec((1, HEAD_DIM), lambda i: (0, 0)),
            pl.BlockSpec((tm, HEAD_DIM), pos_map),
            pl.BlockSpec((tm, HEAD_DIM), pos_map),
        ],
        out_specs=(pl.BlockSpec((tm, N_HEADS * HEAD_DIM), lambda i: (i, 0)),
                   pl.BlockSpec((tm, kvw), lambda i: (i, 0)),
                   pl.BlockSpec((tm, kvw), lambda i: (i, 0)),
                   pl.BlockSpec((tm, kvw), prompt_map),
                   pl.BlockSpec((tm, kvw), prompt_map)),
        compiler_params=pltpu.CompilerParams(vmem_limit_bytes=VMEM_LIMIT),
        name="odd_qkv",
    )(x, gn, mods, w_qkv, q_norm, k_norm, cos_t, sin_t)


def _rope_tables():
    rows = DEC_SEQ // GRID_W
    row = jnp.repeat(jnp.arange(rows, dtype=F32), GRID_W)
    col = jnp.tile(jnp.arange(GRID_W, dtype=F32), rows)
    inv = ROPE_THETA ** (-jnp.arange(0, AXIS_DIM, 2, dtype=F32) / AXIS_DIM)
    ang = jnp.concatenate([row[:, None] * inv, col[:, None] * inv], axis=-1)
    cos_t = jnp.repeat(jnp.cos(ang), 2, axis=-1)
    sin = jnp.sin(ang)
    sin_t = jnp.stack([-sin, sin], axis=-1).reshape(DEC_SEQ, HEAD_DIM)
    return cos_t, sin_t


def _attn_kernel(*refs, has_cache, has_prev):
    q_ref, k_ref, v_ref = refs[:3]
    ck_ref, cv_ref = refs[3:5] if has_cache else (None, None)
    o_ref = refs[-1]
    c = HEAD_DIM ** -0.5 * math.log2(math.e)
    ones_col = (lax.broadcasted_iota(jnp.int32, (1, HEAD_DIM), 1) == 0).astype(BF16)

    def with_ones(v):
        return jnp.concatenate([v, jnp.broadcast_to(ones_col, v.shape)], axis=1)

    k = k_ref[...]
    v = with_ones(v_ref[...])
    if has_cache:
        ck = ck_ref[...].astype(BF16)
        cv = with_ones(cv_ref[...].astype(BF16))
    for r in range(Q_PER_KV):
        cs = slice(r * HEAD_DIM, (r + 1) * HEAD_DIM)
        q = q_ref[:, cs]
        s = lax.dot_general(q, k, NT_DIMS, preferred_element_type=F32)
        m = jnp.max(s, axis=-1, keepdims=True)
        if has_cache:
            sc = lax.dot_general(q, ck, NT_DIMS, preferred_element_type=F32)
            m = jnp.maximum(m, jnp.max(sc, axis=-1, keepdims=True))
        mc = m * c
        o = _dot(jnp.exp2(s * c - mc).astype(BF16), v)
        if has_cache:
            o = o + _dot(jnp.exp2(sc * c - mc).astype(BF16), cv)
        o_ref[:, cs] = (o[:, 0:HEAD_DIM] / o[:, HEAD_DIM:HEAD_DIM + 1]).astype(BF16)


def _attn_call(q, k, v, cache_k, cache_v, seq_len, nseq, row0, tq, prev=None):
    assert row0 % seq_len == 0 and seq_len % tq == 0
    has_cache = cache_k is not None
    has_prev = prev is not None
    kv0 = row0 // seq_len
    q0 = row0 // tq
    nq = seq_len // tq
    gw = Q_PER_KV * HEAD_DIM

    def q_map(b, g, i):
        return (q0 + b * nq + i, g)

    in_specs = [
        pl.BlockSpec((tq, gw), q_map),
        pl.BlockSpec((seq_len, HEAD_DIM), lambda b, g, i: (kv0 + b, g)),
        pl.BlockSpec((seq_len, HEAD_DIM), lambda b, g, i: (kv0 + b, g)),
    ]
    args = [q, k, v]
    if has_cache:
        in_specs += [pl.BlockSpec((PAST_LEN, HEAD_DIM), lambda b, g, i: (b, g)),
                     pl.BlockSpec((PAST_LEN, HEAD_DIM), lambda b, g, i: (b, g))]
        args += [cache_k, cache_v]
    if has_prev:
        in_specs.append(pl.BlockSpec(memory_space=pl.ANY))
        args.append(prev)
    return pl.pallas_call(
        functools.partial(_attn_kernel, has_cache=has_cache, has_prev=has_prev),
        out_shape=jax.ShapeDtypeStruct((T_TOK, N_HEADS * HEAD_DIM), BF16),
        grid=(nseq, KV_HEADS, nq),
        in_specs=in_specs,
        out_specs=pl.BlockSpec((tq, gw), q_map),
        input_output_aliases={len(args) - 1: 0} if has_prev else {},
        compiler_params=pltpu.CompilerParams(vmem_limit_bytes=VMEM_LIMIT),
        name=f"attn_len{seq_len}",
    )(*args)


def kernel(x_prompt, x_sample, state_s5_re, state_s5_im, state_gla, cache_k, cache_v, c, c_ctx, norm_mix, norm_mlp, w_ada, b_ada, w_mlp_in, w_mlp_out, w_in_e, w_out_e, s5_lambda_re, s5_lambda_im, s5_log_dt, s5_b_re, s5_b_im, s5_c_re, s5_c_im, s5_d, s5_w_glu, s5_b_glu, gla_w_gate2, gla_b_gate, gla_norm, w_qkv_o, w_o_o, q_norm, k_norm):
    xp = x_prompt.reshape(T_PROMPT, D_MODEL)
    xs = x_sample.reshape(T_SAMPLE, D_MODEL)
    cond8 = jnp.concatenate([c_ctx[None, :], c, jnp.zeros((COND_ROWS - 1 - DEC_BATCH, D_MODEL), F32)], axis=0)
    mods = _ada_call(cond8, w_ada, b_ada).reshape(DEPTH * COND_ROWS, 1, 6 * D_MODEL)

    n_main = S5_WIDTH + 2 * GLA_QK + 2 * GLA_VW
    w_in = w_in_e[0]
    w_main = w_in[:, :n_main].astype(BF16)
    w_glr = jnp.pad(w_in[:, n_main:], ((0, 0), (0, 128 - 2 * GLA_RANK))).astype(BF16)
    w_gate = jnp.zeros((128, 2 * GLA_QK), F32)
    w_gate = w_gate.at[0:GLA_RANK, 0:GLA_QK].set(gla_w_gate2[0, 0])
    w_gate = w_gate.at[GLA_RANK:2 * GLA_RANK, GLA_QK:].set(gla_w_gate2[0, 1]).astype(BF16)
    b_gate = gla_b_gate[0].reshape(1, 2 * GLA_QK)
    z, g = _inproj_call(xp, xs, norm_mix[0:1], mods, 0, w_main, w_glr, w_gate, b_gate)

    mats = _s5_prep_call(s5_lambda_re[0], s5_lambda_im[0], s5_log_dt[0], s5_b_re[0], s5_b_im[0],
                         s5_c_re[0], s5_c_im[0])

    def state_rows(s):
        return jnp.transpose(s, (2, 0, 1, 3)).reshape(S5_GROUPS, DEC_BATCH, 2 * S5_STATE)

    h0 = jnp.concatenate([state_rows(state_s5_re[:, 0]), state_rows(state_s5_im[:, 0])], axis=-1)
    nsteps = S5_GROUPS // S5_GPB
    h0 = jnp.transpose(h0.reshape(nsteps, S5_GPB, DEC_BATCH, S5_W), (0, 2, 1, 3))
    y5, ns = _s5_call(z, mats, h0)
    ns = jnp.transpose(ns, (0, 2, 1, 3)).reshape(S5_GROUPS, BATCH, S5_W)

    def state_out(n):
        return jnp.transpose(n.reshape(S5_GROUPS, BATCH, 2, S5_STATE), (1, 2, 0, 3))[:, None]

    new_s5_re = state_out(ns[:, :, :2 * S5_STATE])
    new_s5_im = state_out(ns[:, :, 2 * S5_STATE:])

    gn_gla = gla_norm[0].reshape(1, GLA_DV)
    gla, sfin = _gla_call(z, g, gn_gla, None, SEQ, BATCH, 0)
    s0 = state_gla[:, 0].reshape(DEC_BATCH, 2, GLA_QK, GLA_DV)
    gla, _ = _gla_call(z, g, gn_gla, s0, DEC_SEQ, DEC_BATCH, T_PROMPT, prev=gla)
    new_gla = sfin.reshape(BATCH, 1, 2, GLA_HEADS, GLA_DK, GLA_DV)

    x = _even_out_call(xp, xs, y5, z, s5_d[0].reshape(1, S5_WIDTH), s5_w_glu[0].astype(BF16),
                       s5_b_glu[0].reshape(1, S5_WIDTH), gla, w_out_e[0].astype(BF16), mods, 0,
                       norm_mlp[0:1], w_mlp_in[0].astype(BF16), w_mlp_out[0].astype(BF16))

    cos_t, sin_t = _rope_tables()
    q, k, v, k32, v32 = _qkv_call(x, norm_mix[1:2], mods, 1, w_qkv_o[0].astype(BF16),
                                  q_norm[0].reshape(1, HEAD_DIM), k_norm[0].reshape(1, HEAD_DIM), cos_t, sin_t)
    att = _attn_call(q, k, v, None, None, SEQ, BATCH, 0, SEQ)
    ck = cache_k[:, 0].reshape(DEC_BATCH * PAST_LEN, KV_HEADS * HEAD_DIM)
    cv = cache_v[:, 0].reshape(DEC_BATCH * PAST_LEN, KV_HEADS * HEAD_DIM)
    att = _attn_call(q, k, v, ck, cv, DEC_SEQ, DEC_BATCH, T_PROMPT, 256, prev=att)
    yp, ys = _odd_out_call(x, att, w_o_o[0].astype(BF16), mods, 1, norm_mlp[1:2],
                           w_mlp_in[1].astype(BF16), w_mlp_out[1].astype(BF16))

    new_k = k32.reshape(BATCH, 1, SEQ, KV_HEADS, HEAD_DIM)
    new_v = v32.reshape(BATCH, 1, SEQ, KV_HEADS, HEAD_DIM)
    y_prompt = yp.reshape(BATCH, SEQ, D_MODEL)
    y_sample = ys.reshape(DEC_BATCH, DEC_SEQ, D_MODEL)
    return (y_prompt, y_sample, new_s5_re, new_s5_im, new_gla, new_k, new_v)
```

```python
import functools
import math

import jax
import jax.numpy as jnp
import numpy as np
from jax import lax
from jax.experimental import pallas as pl
from jax.experimental.pallas import tpu as pltpu

F32 = jnp.float32
BF16 = jnp.bfloat16

D_MODEL = 1024
BATCH = 16
SEQ = 256
DEPTH = 2
DEC_BATCH = 4
DEC_SEQ = 1024
PAST_LEN = 512
GRID_W = 64
S5_WIDTH = 512
S5_GROUP_CH = 16
S5_GROUPS = 32
S5_STATE = 64
GLA_HEADS = 4
GLA_VW = 512
GLA_DV = 128
GLA_DK = 64
GLA_QK = 256
GLA_RANK = 16
GLA_TAU = 16.0
GLA_CHUNK = 64
GLA_CPB = 4
GLA_BLK = GLA_CPB * GLA_CHUNK
HEAD_DIM = 128
N_HEADS = 8
KV_HEADS = 2
Q_PER_KV = N_HEADS // KV_HEADS
AXIS_DIM = 64
ROPE_THETA = 10000.0
D_FF = 4096
EPS = 1e-6

T_PROMPT = BATCH * SEQ
T_SAMPLE = DEC_BATCH * DEC_SEQ
T_TOK = T_PROMPT + T_SAMPLE
COND_ROWS = 8
COND_SPAN = 1024
PROMPT_SPANS = T_PROMPT // COND_SPAN

S5_Q = 16
S5_W = S5_Q * S5_GROUP_CH
S5_GPB = 128 // S5_GROUP_CH
S5_ROWS = T_TOK // S5_Q
S5_PROMPT_ROWS = T_PROMPT // S5_Q
S5_PROMPT_CHUNKS = SEQ // S5_Q
S5_SAMPLE_CHUNKS = DEC_SEQ // S5_Q

VMEM_LIMIT = 56 * 1024 * 1024

NT_DIMS = (((1,), (1,)), ((), ()))
TN_DIMS = (((0,), (0,)), ((), ()))


def _cond_row(i, tile):
    return jnp.maximum((i * tile) // COND_SPAN - (PROMPT_SPANS - 1), 0)


def _norm_mod(x, gain, shift, scale):
    y = x * lax.rsqrt(jnp.mean(x * x, axis=-1, keepdims=True) + EPS)
    return (y * gain) * (1.0 + scale) + shift


def _dot(a, b):
    return jnp.dot(a, b, preferred_element_type=F32)


def _ada_kernel(cond_ref, w_ref, b_ref, o_ref):
    s = jax.nn.silu(cond_ref[...]).astype(BF16)
    o_ref[...] = _dot(s, w_ref[...].astype(BF16)) + b_ref[...]


def _ada_call(cond8, w_ada, b_ada):
    tn = 2048
    nj = 6 * D_MODEL // tn
    return pl.pallas_call(
        _ada_kernel,
        out_shape=jax.ShapeDtypeStruct((DEPTH, COND_ROWS, 6 * D_MODEL), F32),
        grid=(DEPTH, nj),
        in_specs=[
            pl.BlockSpec((COND_ROWS, D_MODEL), lambda l, j: (0, 0)),
            pl.BlockSpec((None, D_MODEL, tn), lambda l, j: (l, 0, j)),
            pl.BlockSpec((None, 1, tn), lambda l, j: (l, 0, j)),
        ],
        out_specs=pl.BlockSpec((None, COND_ROWS, tn), lambda l, j: (l, 0, j)),
        compiler_params=pltpu.CompilerParams(vmem_limit_bytes=VMEM_LIMIT),
        name="ada_mod",
    )(cond8, w_ada, b_ada.reshape(DEPTH, 1, 6 * D_MODEL))


def _token_specs(tile):
    n_prompt = T_PROMPT // tile
    return [pl.BlockSpec((tile, D_MODEL), lambda i: (jnp.minimum(i, n_prompt - 1), 0)),
            pl.BlockSpec((tile, D_MODEL), lambda i: (jnp.maximum(i - n_prompt, 0), 0))]


def _token_tile(xp_ref, xs_ref, tile):
    return jnp.where(pl.program_id(0) < T_PROMPT // tile, xp_ref[...], xs_ref[...])


def _inproj_kernel(xp_ref, xs_ref, gn_ref, m_ref, w_ref, wglr_ref, wg_ref, bg_ref, z_ref, g_ref, *, tile):
    x = _token_tile(xp_ref, xs_ref, tile)
    h = _norm_mod(x, gn_ref[...], m_ref[:, 0:D_MODEL], m_ref[:, D_MODEL:2 * D_MODEL]).astype(BF16)
    z_ref[...] = _dot(h, w_ref[...])
    glr = _dot(h, wglr_ref[...]).astype(BF16)
    pre = _dot(glr, wg_ref[...]) + bg_ref[...]
    g_ref[...] = jax.nn.log_sigmoid(pre) * (1.0 / GLA_TAU)


def _inproj_call(xp, xs, gn, mods, layer, w_main, w_glr, w_gate, b_gate):
    tm = 512
    nz = w_main.shape[1]
    return pl.pallas_call(
        functools.partial(_inproj_kernel, tile=tm),
        out_shape=(jax.ShapeDtypeStruct((T_TOK, nz), F32),
                   jax.ShapeDtypeStruct((T_TOK, 2 * GLA_QK), F32)),
        grid=(T_TOK // tm,),
        in_specs=_token_specs(tm) + [
            pl.BlockSpec((1, D_MODEL), lambda i: (0, 0)),
            pl.BlockSpec((None, 1, 6 * D_MODEL), lambda i: (layer * COND_ROWS + _cond_row(i, tm), 0, 0)),
            pl.BlockSpec((D_MODEL, nz), lambda i: (0, 0)),
            pl.BlockSpec((D_MODEL, 128), lambda i: (0, 0)),
            pl.BlockSpec((128, 2 * GLA_QK), lambda i: (0, 0)),
            pl.BlockSpec((1, 2 * GLA_QK), lambda i: (0, 0)),
        ],
        out_specs=(pl.BlockSpec((tm, nz), lambda i: (i, 0)),
                   pl.BlockSpec((tm, 2 * GLA_QK), lambda i: (i, 0))),
        compiler_params=pltpu.CompilerParams(vmem_limit_bytes=VMEM_LIMIT),
        name="even_inproj",
    )(xp, xs, gn, mods, w_main, w_glr, w_gate, b_gate)


def _s5_prep_kernel(lre_ref, lim_ref, ldt_ref, btre_ref, btim_ref, cre_ref, cim_ref, ccf_ref, ccb_ref,
                    t_ref, bq_ref, cqt_ref, be_ref, a_ref, t_scr, dd_scr):
    lre = lre_ref[...]
    lim = lim_ref[...]
    dt = jnp.exp(ldt_ref[...])
    a = lre * dt
    th = lim * dt

    def lam_pow(k):
        mag = jnp.exp(k * a)
        return mag * jnp.cos(k * th), mag * jnp.sin(k * th)

    lb_re, lb_im = lam_pow(1.0)
    nr = lb_re - 1.0
    den = lre * lre + lim * lim
    cf_re = (nr * lre + lb_im * lim) / den
    cf_im = (lb_im * lre - nr * lim) / den
    bt_re = btre_ref[...]
    bt_im = btim_ref[...]
    bb_re = jnp.tile(cf_re * bt_re - cf_im * bt_im, (S5_Q, 1))
    bb_im = jnp.tile(cf_re * bt_im + cf_im * bt_re, (S5_Q, 1))

    shape = (S5_W, 128)
    pos = lax.shift_right_logical(lax.broadcasted_iota(jnp.int32, shape, 0), 4)
    is_f = lax.broadcasted_iota(jnp.int32, shape, 1) < S5_STATE
    posq = lax.broadcasted_iota(jnp.int32, (S5_Q, 128), 0).astype(F32)
    is_fq = lax.broadcasted_iota(jnp.int32, (S5_Q, 128), 1) < S5_STATE

    def per_channel(tbl):
        return jnp.broadcast_to(tbl[:, None, :], (S5_Q, S5_GROUP_CH, 128)).reshape(shape)

    p_re, p_im = map(per_channel, lam_pow(jnp.where(is_fq, (S5_Q - 1.0) - posq, posq)))
    w_re = p_re * bb_re - p_im * bb_im
    w_im = p_re * bb_im + p_im * bb_re
    bq = jnp.concatenate([w_re, w_im], axis=1)
    bqt = jnp.transpose(bq)
    bq_ref[...] = bqt.astype(BF16)

    edge = pos == jnp.where(is_f, 0, S5_Q - 1)
    be = jnp.concatenate([jnp.where(edge, bb_re, 0.0), jnp.where(edge, bb_im, 0.0)], axis=1)
    be_ref[...] = jnp.transpose(be).astype(BF16)

    q_re, q_im = map(per_channel, lam_pow(jnp.where(is_fq, posq + 1.0, S5_Q - posq)))
    ct_re = jnp.tile(cre_ref[...], (S5_Q, 1))
    ct_im = jnp.tile(cim_ref[...], (S5_Q, 1))
    g_re = q_re * ct_re - q_im * ct_im
    g_im = q_re * ct_im + q_im * ct_re
    cqt_ref[...] = jnp.concatenate([g_re, -g_im], axis=1).astype(BF16)

    a_re, a_im = lam_pow(float(S5_Q))
    a_ref[...] = jnp.concatenate([a_re, a_im], axis=1)

    kf = jnp.dot(ccf_ref[...], bqt, precision=lax.Precision.HIGHEST, preferred_element_type=F32)
    kb = jnp.dot(ccb_ref[...], bqt, precision=lax.Precision.HIGHEST, preferred_element_type=F32)
    gch = S5_GROUP_CH
    lo = S5_W - gch
    dd_scr[:, 0:S5_W] = kf
    dd_scr[:, lo:lo + S5_W] = kb
    dd_scr[:, lo:S5_W] = kf[:, lo:S5_W] + kb[:, 0:gch]
    for t in range(S5_Q):
        c0 = (S5_Q - 1 - t) * gch
        t_scr[t * gch:(t + 1) * gch, :] = dd_scr[:, c0:c0 + S5_W]
    t_ref[...] = t_scr[...].astype(BF16)


def _s5_prep_call(lam_re, lam_im, log_dt, b_re, b_im, c_re, c_im):
    def fb(p):
        return jnp.transpose(p, (1, 0, 2)).reshape(S5_GROUPS, 1, 2 * S5_STATE)

    def dup(p):
        return jnp.concatenate([p, p], axis=-1)

    ldt = fb(jnp.broadcast_to(log_dt[:, :, None], (2, S5_GROUPS, S5_STATE)))
    bt_re = dup(jnp.transpose(b_re, (0, 2, 1)))
    bt_im = dup(jnp.transpose(b_im, (0, 2, 1)))
    zero = jnp.zeros_like(c_re)
    ccf = jnp.concatenate([c_re, zero, -c_im, zero], axis=-1)
    ccb = jnp.concatenate([zero, c_re, zero, -c_im], axis=-1)

    row = pl.BlockSpec((None, 1, 128), lambda g: (g, 0, 0))
    mat16 = pl.BlockSpec((None, S5_GROUP_CH, 128), lambda g: (g, 0, 0))
    mat16w = pl.BlockSpec((None, S5_GROUP_CH, S5_W), lambda g: (g, 0, 0))
    sq = pl.BlockSpec((None, S5_W, S5_W), lambda g: (g, 0, 0))
    sq_shape = jax.ShapeDtypeStruct((S5_GROUPS, S5_W, S5_W), BF16)
    return pl.pallas_call(
        _s5_prep_kernel,
        out_shape=(sq_shape, sq_shape, sq_shape, sq_shape,
                   jax.ShapeDtypeStruct((S5_GROUPS, 1, S5_W), F32)),
        grid=(S5_GROUPS,),
        in_specs=[row, row, row, mat16, mat16, mat16, mat16, mat16w, mat16w],
        out_specs=(sq, sq, sq, sq, pl.BlockSpec((None, 1, S5_W), lambda g: (g, 0, 0))),
        scratch_shapes=[pltpu.VMEM((S5_W, S5_W), F32), pltpu.VMEM((S5_GROUP_CH, 2 * S5_W), F32)],
        name="s5_prep",
    )(fb(lam_re), fb(lam_im), ldt, bt_re, bt_im, dup(c_re), dup(c_im), ccf, ccb)


def _s5_kernel(u_ref, tt_ref, bqt_ref, cqt_ref, bet_ref, a_ref, h0_ref, y_ref, ns_ref,
               ut_scr, x_scr, spf_scr, spb_scr, ne_scr, yt_scr, xt_scr):
    gch = S5_GROUP_CH
    for s in range(S5_Q):
        rows = u_ref[pl.ds(s, S5_ROWS, stride=S5_Q), :]
        rows_t = jnp.transpose(rows).astype(BF16)
        for gl in range(S5_GPB):
            ut_scr[gl, s * gch:(s + 1) * gch, :] = rows_t[gl * gch:(gl + 1) * gch, :]

    for gl in range(S5_GPB):
        ut = ut_scr[gl]
        xt_scr[...] = _dot(bqt_ref[gl], ut)
        x = jnp.transpose(xt_scr[...])
        xt_scr[:, 0:S5_PROMPT_ROWS] = _dot(bet_ref[gl], ut[:, 0:S5_PROMPT_ROWS])
        ne = jnp.transpose(xt_scr[:, 0:S5_PROMPT_ROWS])
        for part in range(2):
            x_scr[part, pl.ds(gl, S5_ROWS, stride=S5_GPB), :] = x[:, part * 128:(part + 1) * 128]
            ne_scr[part, pl.ds(gl, S5_PROMPT_ROWS, stride=S5_GPB), :] = ne[:, part * 128:(part + 1) * 128]

    is_f = lax.broadcasted_iota(jnp.int32, (1, 128), 1) < S5_STATE
    a_re = a_ref[:, 0:128]
    a_im = a_ref[:, 128:256]

    def tile(row):
        return pl.ds(pl.multiple_of(row * S5_GPB, S5_GPB), S5_GPB)

    def scan(base, nseq, nchunk, s_init):
        def body(i, state):
            new = []
            for b in range(nseq):
                s_re, s_im = state[b]
                rows_f = tile(base + b * nchunk + i)
                rows_b = tile(base + b * nchunk + (nchunk - 1 - i))
                spf_scr[0, rows_f, :] = s_re
                spf_scr[1, rows_f, :] = s_im
                spb_scr[0, rows_b, :] = s_re
                spb_scr[1, rows_b, :] = s_im
                x_re = jnp.where(is_f, x_scr[0, rows_f, :], x_scr[0, rows_b, :])
                x_im = jnp.where(is_f, x_scr[1, rows_f, :], x_scr[1, rows_b, :])
                new.append((a_re * s_re - a_im * s_im + x_re, a_re * s_im + a_im * s_re + x_im))
            return tuple(new)

        lax.fori_loop(0, nchunk, body, tuple(s_init))

    zero = jnp.zeros((S5_GPB, 128), F32)
    scan(0, BATCH, S5_PROMPT_CHUNKS, [(zero, zero)] * BATCH)
    scan(S5_PROMPT_ROWS, DEC_BATCH, S5_SAMPLE_CHUNKS,
         [(h0_ref[b, :, 0:128], h0_ref[b, :, 128:256]) for b in range(DEC_BATCH)])

    for b in range(BATCH):
        first = pl.ds(b * S5_PROMPT_CHUNKS * S5_GPB, S5_GPB)
        last = pl.ds(((b + 1) * S5_PROMPT_CHUNKS - 1) * S5_GPB, S5_GPB)
        for part in range(2):
            ns_ref[b, :, part * 128:(part + 1) * 128] = jnp.where(is_f, ne_scr[part, first, :], ne_scr[part, last, :])

    for gl in range(S5_GPB):
        rows = pl.ds(gl, S5_ROWS, stride=S5_GPB)
        carried = jnp.concatenate([jnp.where(is_f, spf_scr[p, rows, :], spb_scr[p, rows, :]) for p in range(2)],
                                  axis=1).astype(BF16)
        yt = _dot(tt_ref[gl], ut_scr[gl]) + lax.dot_general(cqt_ref[gl], carried, NT_DIMS,
                                                            preferred_element_type=F32)
        for t in range(S5_Q):
            yt_scr[t, gl * gch:(gl + 1) * gch, :] = yt[t * gch:(t + 1) * gch, :]
    for t in range(S5_Q):
        y_ref[pl.ds(t, S5_ROWS, stride=S5_Q), :] = jnp.transpose(yt_scr[t])


def _s5_call(z, mats, h0):
    tt_m, bqt_m, cqt_m, bet_m, a_m = mats
    nsteps = S5_GROUPS // S5_GPB
    sq = pl.BlockSpec((S5_GPB, S5_W, S5_W), lambda g: (g, 0, 0))
    state_scr = pltpu.VMEM((2, S5_ROWS * S5_GPB, 128), F32)
    return pl.pallas_call(
        _s5_kernel,
        out_shape=(jax.ShapeDtypeStruct((T_TOK, S5_WIDTH), F32),
                   jax.ShapeDtypeStruct((nsteps, BATCH, S5_GPB, S5_W), F32)),
        grid=(nsteps,),
        in_specs=[
            pl.BlockSpec((T_TOK, 128), lambda g: (0, g)),
            sq, sq, sq, sq,
            pl.BlockSpec((S5_GPB, S5_W), lambda g: (g, 0)),
            pl.BlockSpec((None, DEC_BATCH, S5_GPB, S5_W), lambda g: (g, 0, 0, 0)),
        ],
        out_specs=(pl.BlockSpec((T_TOK, 128), lambda g: (0, g)),
                   pl.BlockSpec((None, BATCH, S5_GPB, S5_W), lambda g: (g, 0, 0, 0))),
        scratch_shapes=[pltpu.VMEM((S5_GPB, S5_W, S5_ROWS), BF16), state_scr, state_scr, state_scr,
                        pltpu.VMEM((2, S5_PROMPT_ROWS * S5_GPB, 128), F32),
                        pltpu.VMEM((S5_Q, 128, S5_ROWS), F32), pltpu.VMEM((S5_W, S5_ROWS), F32)],
        compiler_params=pltpu.CompilerParams(vmem_limit_bytes=VMEM_LIMIT),
        name="s5_scan",
    )(z, tt_m, bqt_m, cqt_m, bet_m, a_m.reshape(S5_GROUPS, S5_W), h0)


def _split_bf16(x):
    hi = x.astype(BF16)
    r1 = x - hi.astype(F32)
    mid = r1.astype(BF16)
    lo = (r1 - mid.astype(F32)).astype(BF16)
    return hi, mid, lo


def _gla_kernel(*refs, seq_len, has_s0, has_prev):
    q_ref, k_ref, v_ref, gf_ref, gb_ref, r_ref, gn_ref = refs[:7]
    s0_ref = refs[7] if has_s0 else None
    o_ref, sfin_ref, oi_scr, qd_scr, kv_scr, dec_scr, ss_scr = refs[7 + has_s0 + has_prev:]
    nblk = seq_len // GLA_BLK
    nchunk = seq_len // GLA_CHUNK
    cl = GLA_CHUNK
    ti = lax.broadcasted_iota(jnp.int32, (GLA_BLK, GLA_BLK), 0)
    si = lax.broadcasted_iota(jnp.int32, (GLA_BLK, GLA_BLK), 1)
    same = lax.shift_right_logical(ti, 6) == lax.shift_right_logical(si, 6)
    keep = (same & (ti >= si), same & (ti <= si))
    tri = tuple(kp.astype(BF16) for kp in keep)
    lane_head = lax.shift_right_logical(lax.broadcasted_iota(jnp.int32, (cl, GLA_QK), 1), 6)
    zeros_v = jnp.zeros((cl, GLA_DV), BF16)
    heads = [(slice(h * GLA_DK, (h + 1) * GLA_DK), slice(h * GLA_DV, (h + 1) * GLA_DV)) for h in range(GLA_HEADS)]

    for j in range(nblk):
        rows = slice(j * GLA_BLK, (j + 1) * GLA_BLK)
        q = q_ref[rows, :] * (GLA_DK ** -0.5)
        k = k_ref[rows, :]
        v = v_ref[rows, :].astype(BF16)
        qd, kd, k2t = [], [], []
        for d, g_ref in enumerate((gf_ref, gb_ref)):
            b = sum(_dot(tri[d], part) for part in _split_bf16(g_ref[rows, :]))
            last = cl - 1 if d == 0 else 0
            b_last = [b[c * cl + last:c * cl + last + 1] for c in range(GLA_CPB)]
            bl = jnp.concatenate([jnp.broadcast_to(x, (cl, GLA_QK)) for x in b_last], axis=0)
            qd_d = (q * jnp.exp(b)).astype(BF16)
            qd_scr[d, rows, :] = qd_d
            qd.append(qd_d)
            kd.append((k * jnp.exp(-b)).astype(BF16))
            k2t.append(jnp.transpose(k * jnp.exp(bl - b)).astype(BF16))
            for c in range(GLA_CPB):
                dec_scr[d, j * GLA_CPB + c] = jnp.exp(jnp.transpose(jnp.broadcast_to(b_last[c], (GLA_DV, GLA_QK))))
        for h, (ks, vs) in enumerate(heads):
            att = [jnp.where(keep[d], lax.dot_general(qd[d][:, ks], kd[d][:, ks], NT_DIMS,
                                                      preferred_element_type=F32), 0.0) for d in range(2)]
            oi_scr[rows, vs] = _dot((att[0] + att[1]).astype(BF16), v[:, vs])
            vh = v[:, vs]
            vexp = jnp.concatenate(
                [jnp.concatenate([vh[c * cl:(c + 1) * cl] if c2 == c else zeros_v for c2 in range(GLA_CPB)], axis=1)
                 for c in range(GLA_CPB)], axis=0)
            for d in range(2):
                kv_scr[d, j, h] = _dot(k2t[d][ks, :], vexp)

    for d in range(2):
        s = s0_ref[d] if has_s0 else jnp.zeros((GLA_QK, GLA_DV), F32)
        for cg in (range(nchunk) if d == 0 else range(nchunk - 1, -1, -1)):
            j, c = divmod(cg, GLA_CPB)
            ss_scr[d, cg] = s.astype(BF16)
            kv = jnp.concatenate([kv_scr[d, j, h, :, c * GLA_DV:(c + 1) * GLA_DV] for h in range(GLA_HEADS)], axis=0)
            s = s * dec_scr[d, cg] + kv
        sfin_ref[d] = s

    for cg in range(nchunk):
        rows = slice(cg * cl, (cg + 1) * cl)
        inter = []
        for d in range(2):
            qc = qd_scr[d, rows, :]
            qstack = jnp.concatenate([jnp.where(lane_head == h, qc, jnp.zeros_like(qc)) for h in range(GLA_HEADS)],
                                     axis=0)
            inter.append(_dot(qstack, ss_scr[d, cg]))
        gate = jax.nn.silu(r_ref[rows, :])
        for h, (ks, vs) in enumerate(heads):
            hr = slice(h * cl, (h + 1) * cl)
            oh = oi_scr[rows, vs] + inter[0][hr] + inter[1][hr]
            oh = oh * lax.rsqrt(jnp.mean(oh * oh, axis=-1, keepdims=True) + EPS) * gn_ref[...]
            o_ref[rows, vs] = oh * gate[:, vs]


def _gla_call(z, g, gla_norm, s0, seq_len, nseq, row0, prev=None):
    assert row0 % seq_len == 0
    r0 = row0 // seq_len
    has_s0 = s0 is not None
    has_prev = prev is not None
    qk_off = S5_WIDTH // GLA_QK
    v_off = (S5_WIDTH + 2 * GLA_QK) // GLA_VW
    in_specs = [
        pl.BlockSpec((seq_len, GLA_QK), lambda i: (r0 + i, qk_off)),
        pl.BlockSpec((seq_len, GLA_QK), lambda i: (r0 + i, qk_off + 1)),
        pl.BlockSpec((seq_len, GLA_VW), lambda i: (r0 + i, v_off)),
        pl.BlockSpec((seq_len, GLA_QK), lambda i: (r0 + i, 0)),
        pl.BlockSpec((seq_len, GLA_QK), lambda i: (r0 + i, 1)),
        pl.BlockSpec((seq_len, GLA_VW), lambda i: (r0 + i, v_off + 1)),
        pl.BlockSpec((1, GLA_DV), lambda i: (0, 0)),
    ]
    args = [z, z, z, g, g, z, gla_norm]
    if has_s0:
        in_specs.append(pl.BlockSpec((None, 2, GLA_QK, GLA_DV), lambda i: (i, 0, 0, 0)))
        args.append(s0)
    if has_prev:
        in_specs.append(pl.BlockSpec(memory_space=pl.ANY))
        args.append(prev)
    return pl.pallas_call(
        functools.partial(_gla_kernel, seq_len=seq_len, has_s0=has_s0, has_prev=has_prev),
        out_shape=(jax.ShapeDtypeStruct((T_TOK, GLA_VW), F32),
                   jax.ShapeDtypeStruct((nseq, 2, GLA_QK, GLA_DV), F32)),
        grid=(nseq,),
        in_specs=in_specs,
        out_specs=(pl.BlockSpec((seq_len, GLA_VW), lambda i: (r0 + i, 0)),
                   pl.BlockSpec((None, 2, GLA_QK, GLA_DV), lambda i: (i, 0, 0, 0))),
        input_output_aliases={len(args) - 1: 0} if has_prev else {},
        scratch_shapes=[
            pltpu.VMEM((seq_len, GLA_VW), F32),
            pltpu.VMEM((2, seq_len, GLA_QK), BF16),
            pltpu.VMEM((2, seq_len // GLA_BLK, GLA_HEADS, GLA_DK, GLA_CPB * GLA_DV), F32),
            pltpu.VMEM((2, seq_len // GLA_CHUNK, GLA_QK, GLA_DV), F32),
            pltpu.VMEM((2, seq_len // GLA_CHUNK, GLA_QK, GLA_DV), BF16),
        ],
        compiler_params=pltpu.CompilerParams(vmem_limit_bytes=VMEM_LIMIT),
        name=f"gla_len{seq_len}",
    )(*args)


def _mlp_tail(x, mix, m_ref, gn2_ref, w1_ref, w2_ref):
    y1 = x + m_ref[:, 2 * D_MODEL:3 * D_MODEL] * mix
    h = _norm_mod(y1, gn2_ref[...], m_ref[:, 3 * D_MODEL:4 * D_MODEL], m_ref[:, 4 * D_MODEL:5 * D_MODEL]).astype(BF16)
    tf = 512
    acc = jnp.zeros(y1.shape, F32)
    for c in range(D_FF // tf):
        a = _dot(h, w1_ref[:, c * tf:(c + 1) * tf])
        a = jnp.square(jnp.maximum(a, 0.0)).astype(BF16)
        acc = acc + _dot(a, w2_ref[c * tf:(c + 1) * tf, :])
    return y1 + m_ref[:, 5 * D_MODEL:6 * D_MODEL] * acc


def _even_out_kernel(xp_ref, xs_ref, y5_ref, u_ref, dskip_ref, wglu_ref, bglu_ref, gla_ref, wout_ref,
                     m_ref, gn2_ref, w1_ref, w2_ref, o_ref):
    ys = y5_ref[...] + u_ref[...] * dskip_ref[...]
    gl = jax.nn.gelu(ys)
    s5o = gl * jax.nn.sigmoid(_dot(gl.astype(BF16), wglu_ref[...]) + bglu_ref[...])
    mix = _dot(s5o.astype(BF16), wout_ref[0:S5_WIDTH, :]) + _dot(gla_ref[...].astype(BF16), wout_ref[S5_WIDTH:, :])
    o_ref[...] = _mlp_tail(_token_tile(xp_ref, xs_ref, _OUT_TM), mix, m_ref, gn2_ref, w1_ref, w2_ref)


def _odd_out_kernel(x_ref, att_ref, wo_ref, m_ref, gn2_ref, w1_ref, w2_ref, op_ref, os_ref):
    mix = _dot(att_ref[...], wo_ref[...])
    y = _mlp_tail(x_ref[...], mix, m_ref, gn2_ref, w1_ref, w2_ref)
    is_prompt = pl.program_id(0) < T_PROMPT // _OUT_TM

    @pl.when(is_prompt)
    def _():
        op_ref[...] = y

    @pl.when(jnp.logical_not(is_prompt))
    def _():
        os_ref[...] = y


_OUT_TM = 512


def _const_spec(shape):
    return pl.BlockSpec(shape, lambda i: (0,) * len(shape), pipeline_mode=pl.Buffered(1))


def _tail_specs(layer):
    tm = _OUT_TM
    return [
        pl.BlockSpec((None, 1, 6 * D_MODEL), lambda i: (layer * COND_ROWS + _cond_row(i, tm), 0, 0)),
        _const_spec((1, D_MODEL)),
        _const_spec((D_MODEL, D_FF)),
        _const_spec((D_FF, D_MODEL)),
    ]


def _even_out_call(xp, xs, y5, z, d_skip, w_glu, b_glu, gla, w_out, mods, layer, gn2, w1, w2):
    tm = _OUT_TM
    return pl.pallas_call(
        _even_out_kernel,
        out_shape=jax.ShapeDtypeStruct((T_TOK, D_MODEL), F32),
        grid=(T_TOK // tm,),
        in_specs=_token_specs(tm) + [
            pl.BlockSpec((tm, S5_WIDTH), lambda i: (i, 0)),
            pl.BlockSpec((tm, S5_WIDTH), lambda i: (i, 0)),
            _const_spec((1, S5_WIDTH)),
            _const_spec((S5_WIDTH, S5_WIDTH)),
            _const_spec((1, S5_WIDTH)),
            pl.BlockSpec((tm, GLA_VW), lambda i: (i, 0)),
            _const_spec((S5_WIDTH + GLA_VW, D_MODEL)),
        ] + _tail_specs(layer),
        out_specs=pl.BlockSpec((tm, D_MODEL), lambda i: (i, 0)),
        compiler_params=pltpu.CompilerParams(vmem_limit_bytes=VMEM_LIMIT),
        name="even_out_mlp",
    )(xp, xs, y5, z, d_skip, w_glu, b_glu, gla, w_out, mods, gn2, w1, w2)


def _odd_out_call(x, att, w_o, mods, layer, gn2, w1, w2):
    tm = _OUT_TM
    n_prompt = T_PROMPT // tm
    return pl.pallas_call(
        _odd_out_kernel,
        out_shape=(jax.ShapeDtypeStruct((T_PROMPT, D_MODEL), F32),
                   jax.ShapeDtypeStruct((T_SAMPLE, D_MODEL), F32)),
        grid=(T_TOK // tm,),
        in_specs=[
            pl.BlockSpec((tm, D_MODEL), lambda i: (i, 0)),
            pl.BlockSpec((tm, D_MODEL), lambda i: (i, 0)),
            _const_spec((D_MODEL, D_MODEL)),
        ] + _tail_specs(layer),
        out_specs=(pl.BlockSpec((tm, D_MODEL), lambda i: (jnp.minimum(i, n_prompt - 1), 0)),
                   pl.BlockSpec((tm, D_MODEL), lambda i: (jnp.maximum(i - n_prompt, 0), 0))),
        compiler_params=pltpu.CompilerParams(vmem_limit_bytes=VMEM_LIMIT),
        name="odd_out_mlp",
    )(x, att, w_o, mods, gn2, w1, w2)


def _qkv_kernel(x_ref, gn_ref, m_ref, w_ref, qn_ref, kn_ref, cos_ref, sin_ref,
                q_ref, kb_ref, vb_ref, k32_ref, v32_ref, *, tile):
    h = _norm_mod(x_ref[...], gn_ref[...], m_ref[:, 0:D_MODEL], m_ref[:, D_MODEL:2 * D_MODEL]).astype(BF16)
    z = _dot(h, w_ref[...])
    v = z[:, (N_HEADS + KV_HEADS) * HEAD_DIM:]
    vb_ref[...] = v.astype(BF16)
    even_lane = (lax.broadcasted_iota(jnp.int32, (1, HEAD_DIM), 1) & 1) == 0

    def heads(rope):
        for hd in range(N_HEADS + KV_HEADS):
            xh = z[:, hd * HEAD_DIM:(hd + 1) * HEAD_DIM]
            gain = qn_ref[...] if hd < N_HEADS else kn_ref[...]
            xh = xh * lax.rsqrt(jnp.mean(xh * xh, axis=-1, keepdims=True) + EPS) * gain
            if rope:
                partner = jnp.where(even_lane, pltpu.roll(xh, HEAD_DIM - 1, 1), pltpu.roll(xh, 1, 1))
                xh = xh * cos_ref[...] + partner * sin_ref[...]
            if hd < N_HEADS:
                q_ref[:, hd * HEAD_DIM:(hd + 1) * HEAD_DIM] = xh.astype(BF16)
            else:
                cols = slice((hd - N_HEADS) * HEAD_DIM, (hd - N_HEADS + 1) * HEAD_DIM)
                kb_ref[:, cols] = xh.astype(BF16)
                if not rope:
                    k32_ref[:, cols] = xh

    is_sample = pl.program_id(0) >= T_PROMPT // tile

    @pl.when(is_sample)
    def _():
        heads(True)

    @pl.when(jnp.logical_not(is_sample))
    def _():
        heads(False)
        v32_ref[...] = v


def _qkv_call(x, gn, mods, layer, w_qkv, q_norm, k_norm, cos_t, sin_t):
    tm = 512
    pos_tiles = DEC_SEQ // tm
    n_prompt = T_PROMPT // tm
    kvw = KV_HEADS * HEAD_DIM

    def pos_map(i):
        return (jnp.maximum(i - n_prompt, 0) % pos_tiles, 0)

    def prompt_map(i):
        return (jnp.minimum(i, n_prompt - 1), 0)

    return pl.pallas_call(
        functools.partial(_qkv_kernel, tile=tm),
        out_shape=(jax.ShapeDtypeStruct((T_TOK, N_HEADS * HEAD_DIM), BF16),
                   jax.ShapeDtypeStruct((T_TOK, kvw), BF16),
                   jax.ShapeDtypeStruct((T_TOK, kvw), BF16),
                   jax.ShapeDtypeStruct((T_PROMPT, kvw), F32),
                   jax.ShapeDtypeStruct((T_PROMPT, kvw), F32)),
        grid=(T_TOK // tm,),
        in_specs=[
            pl.BlockSpec((tm, D_MODEL), lambda i: (i, 0)),
            pl.BlockSpec((1, D_MODEL), lambda i: (0, 0)),
            pl.BlockSpec((None, 1, 6 * D_MODEL), lambda i: (layer * COND_ROWS + _cond_row(i, tm), 0, 0)),
            pl.BlockSpec(w_qkv.shape, lambda i: (0, 0)),
            pl.BlockSpec((1, HEAD_DIM), lambda i: (0, 0)),
            pl.BlockSpec((1, HEAD_DIM), lambda i: (0, 0)),
            pl.BlockSpec((tm, HEAD_DIM), pos_map),
            pl.BlockSpec((tm, HEAD_DIM), pos_map),
        ],
        out_specs=(pl.BlockSpec((tm, N_HEADS * HEAD_DIM), lambda i: (i, 0)),
                   pl.BlockSpec((tm, kvw), lambda i: (i, 0)),
                   pl.BlockSpec((tm, kvw), lambda i: (i, 0)),
                   pl.BlockSpec((tm, kvw), prompt_map),
                   pl.BlockSpec((tm, kvw), prompt_map)),
        compiler_params=pltpu.CompilerParams(vmem_limit_bytes=VMEM_LIMIT),
        name="odd_qkv",
    )(x, gn, mods, w_qkv, q_norm, k_norm, cos_t, sin_t)


def _rope_tables():
    rows = DEC_SEQ // GRID_W
    row = jnp.repeat(jnp.arange(rows, dtype=F32), GRID_W)
    col = jnp.tile(jnp.arange(GRID_W, dtype=F32), rows)
    inv = ROPE_THETA ** (-jnp.arange(0, AXIS_DIM, 2, dtype=F32) / AXIS_DIM)
    ang = jnp.concatenate([row[:, None] * inv, col[:, None] * inv], axis=-1)
    cos_t = jnp.repeat(jnp.cos(ang), 2, axis=-1)
    sin = jnp.sin(ang)
    sin_t = jnp.stack([-sin, sin], axis=-1).reshape(DEC_SEQ, HEAD_DIM)
    return cos_t, sin_t


def _attn_kernel(*refs, seq_len, has_cache, has_prev):
    q_ref, k_ref, v_ref = refs[:3]
    ck_ref, cv_ref = refs[3:5] if has_cache else (None, None)
    o_ref = refs[-1]
    c = HEAD_DIM ** -0.5 * math.log2(math.e)
    ones_col = (lax.broadcasted_iota(jnp.int32, (1, HEAD_DIM), 1) == 0).astype(BF16)

    def with_ones(v):
        return jnp.concatenate([v, jnp.broadcast_to(ones_col, v.shape)], axis=1)

    if has_cache:
        ck = ck_ref[...].astype(BF16)
        cv = with_ones(cv_ref[...].astype(BF16))
    for j in range(q_ref.shape[0] // seq_len):
        rows = slice(j * seq_len, (j + 1) * seq_len)
        k = k_ref[rows, :]
        v = with_ones(v_ref[rows, :])
        for r in range(Q_PER_KV):
            cs = slice(r * HEAD_DIM, (r + 1) * HEAD_DIM)
            q = q_ref[rows, cs]
            s = lax.dot_general(q, k, NT_DIMS, preferred_element_type=F32)
            m = jnp.max(s, axis=-1, keepdims=True)
            if has_cache:
                sc = lax.dot_general(q, ck, NT_DIMS, preferred_element_type=F32)
                m = jnp.maximum(m, jnp.max(sc, axis=-1, keepdims=True))
            mc = m * c
            o = _dot(jnp.exp2(s * c - mc).astype(BF16), v)
            if has_cache:
                o = o + _dot(jnp.exp2(sc * c - mc).astype(BF16), cv)
            o_ref[rows, cs] = (o[:, 0:HEAD_DIM] / o[:, HEAD_DIM:HEAD_DIM + 1]).astype(BF16)


def _attn_call(q, k, v, cache_k, cache_v, seq_len, row0, nrows, prev=None):
    assert row0 % seq_len == 0 and nrows % seq_len == 0
    has_cache = cache_k is not None
    has_prev = prev is not None
    b0 = row0 // seq_len
    gw = Q_PER_KV * HEAD_DIM
    in_specs = [
        pl.BlockSpec((seq_len, gw), lambda b, g: (b0 + b, g)),
        pl.BlockSpec((seq_len, HEAD_DIM), lambda b, g: (b0 + b, g)),
        pl.BlockSpec((seq_len, HEAD_DIM), lambda b, g: (b0 + b, g)),
    ]
    args = [q, k, v]
    if has_cache:
        in_specs += [pl.BlockSpec((PAST_LEN, HEAD_DIM), lambda b, g: (b, g)),
                     pl.BlockSpec((PAST_LEN, HEAD_DIM), lambda b, g: (b, g))]
        args += [cache_k, cache_v]
    if has_prev:
        in_specs.append(pl.BlockSpec(memory_space=pl.ANY))
        args.append(prev)
    return pl.pallas_call(
        functools.partial(_attn_kernel, seq_len=seq_len, has_cache=has_cache, has_prev=has_prev),
        out_shape=jax.ShapeDtypeStruct((T_TOK, N_HEADS * HEAD_DIM), BF16),
        grid=(nrows // seq_len, KV_HEADS),
        in_specs=in_specs,
        out_specs=pl.BlockSpec((seq_len, gw), lambda b, g: (b0 + b, g)),
        input_output_aliases={len(args) - 1: 0} if has_prev else {},
        compiler_params=pltpu.CompilerParams(vmem_limit_bytes=VMEM_LIMIT),
        name=f"attn_len{seq_len}",
    )(*args)


def kernel(x_prompt, x_sample, state_s5_re, state_s5_im, state_gla, cache_k, cache_v, c, c_ctx, norm_mix, norm_mlp, w_ada, b_ada, w_mlp_in, w_mlp_out, w_in_e, w_out_e, s5_lambda_re, s5_lambda_im, s5_log_dt, s5_b_re, s5_b_im, s5_c_re, s5_c_im, s5_d, s5_w_glu, s5_b_glu, gla_w_gate2, gla_b_gate, gla_norm, w_qkv_o, w_o_o, q_norm, k_norm):
    xp = x_prompt.reshape(T_PROMPT, D_MODEL)
    xs = x_sample.reshape(T_SAMPLE, D_MODEL)
    cond8 = jnp.concatenate([c_ctx[None, :], c, jnp.zeros((COND_ROWS - 1 - DEC_BATCH, D_MODEL), F32)], axis=0)
    mods = _ada_call(cond8, w_ada, b_ada).reshape(DEPTH * COND_ROWS, 1, 6 * D_MODEL)

    n_main = S5_WIDTH + 2 * GLA_QK + 2 * GLA_VW
    w_in = w_in_e[0]
    w_main = w_in[:, :n_main].astype(BF16)
    w_glr = jnp.pad(w_in[:, n_main:], ((0, 0), (0, 128 - 2 * GLA_RANK))).astype(BF16)
    w_gate = jnp.zeros((128, 2 * GLA_QK), F32)
    w_gate = w_gate.at[0:GLA_RANK, 0:GLA_QK].set(gla_w_gate2[0, 0])
    w_gate = w_gate.at[GLA_RANK:2 * GLA_RANK, GLA_QK:].set(gla_w_gate2[0, 1]).astype(BF16)
    b_gate = gla_b_gate[0].reshape(1, 2 * GLA_QK)
    z, g = _inproj_call(xp, xs, norm_mix[0:1], mods, 0, w_main, w_glr, w_gate, b_gate)

    mats = _s5_prep_call(s5_lambda_re[0], s5_lambda_im[0], s5_log_dt[0], s5_b_re[0], s5_b_im[0],
                         s5_c_re[0], s5_c_im[0])

    def state_rows(s):
        return jnp.transpose(s, (2, 0, 1, 3)).reshape(S5_GROUPS, DEC_BATCH, 2 * S5_STATE)

    h0 = jnp.concatenate([state_rows(state_s5_re[:, 0]), state_rows(state_s5_im[:, 0])], axis=-1)
    nsteps = S5_GROUPS // S5_GPB
    h0 = jnp.transpose(h0.reshape(nsteps, S5_GPB, DEC_BATCH, S5_W), (0, 2, 1, 3))
    y5, ns = _s5_call(z, mats, h0)
    ns = jnp.transpose(ns, (0, 2, 1, 3)).reshape(S5_GROUPS, BATCH, S5_W)

    def state_out(n):
        return jnp.transpose(n.reshape(S5_GROUPS, BATCH, 2, S5_STATE), (1, 2, 0, 3))[:, None]

    new_s5_re = state_out(ns[:, :, :2 * S5_STATE])
    new_s5_im = state_out(ns[:, :, 2 * S5_STATE:])

    gn_gla = gla_norm[0].reshape(1, GLA_DV)
    gla, sfin = _gla_call(z, g, gn_gla, None, SEQ, BATCH, 0)
    s0 = state_gla[:, 0].reshape(DEC_BATCH, 2, GLA_QK, GLA_DV)
    gla, _ = _gla_call(z, g, gn_gla, s0, DEC_SEQ, DEC_BATCH, T_PROMPT, prev=gla)
    new_gla = sfin.reshape(BATCH, 1, 2, GLA_HEADS, GLA_DK, GLA_DV)

    x = _even_out_call(xp, xs, y5, z, s5_d[0].reshape(1, S5_WIDTH), s5_w_glu[0].astype(BF16),
                       s5_b_glu[0].reshape(1, S5_WIDTH), gla, w_out_e[0].astype(BF16), mods, 0,
                       norm_mlp[0:1], w_mlp_in[0].astype(BF16), w_mlp_out[0].astype(BF16))

    cos_t, sin_t = _rope_tables()
    q, k, v, k32, v32 = _qkv_call(x, norm_mix[1:2], mods, 1, w_qkv_o[0].astype(BF16),
                                  q_norm[0].reshape(1, HEAD_DIM), k_norm[0].reshape(1, HEAD_DIM), cos_t, sin_t)
    att = _attn_call(q, k, v, None, None, SEQ, 0, T_PROMPT)
    ck = cache_k[:, 0].reshape(DEC_BATCH * PAST_LEN, KV_HEADS * HEAD_DIM)
    cv = cache_v[:, 0].reshape(DEC_BATCH * PAST_LEN, KV_HEADS * HEAD_DIM)
    att = _attn_call(q, k, v, ck, cv, DEC_SEQ, T_PROMPT, T_SAMPLE, prev=att)
    yp, ys = _odd_out_call(x, att, w_o_o[0].astype(BF16), mods, 1, norm_mlp[1:2],
                           w_mlp_in[1].astype(BF16), w_mlp_out[1].astype(BF16))

    new_k = k32.reshape(BATCH, 1, SEQ, KV_HEADS, HEAD_DIM)
    new_v = v32.reshape(BATCH, 1, SEQ, KV_HEADS, HEAD_DIM)
    y_prompt = yp.reshape(BATCH, SEQ, D_MODEL)
    y_sample = ys.reshape(DEC_BATCH, DEC_SEQ, D_MODEL)
    return (y_prompt, y_sample, new_s5_re, new_s5_im, new_gla, new_k, new_v)
```

```python
import functools
import math

import jax
import jax.numpy as jnp
import numpy as np
from jax import lax
from jax.experimental import pallas as pl
from jax.experimental.pallas import tpu as pltpu

F32 = jnp.float32
BF16 = jnp.bfloat16

D_MODEL = 1024
BATCH = 16
SEQ = 256
DEPTH = 2
DEC_BATCH = 4
DEC_SEQ = 1024
PAST_LEN = 512
GRID_W = 64
S5_WIDTH = 512
S5_GROUP_CH = 16
S5_GROUPS = 32
S5_STATE = 64
GLA_HEADS = 4
GLA_VW = 512
GLA_DV = 128
GLA_DK = 64
GLA_QK = 256
GLA_RANK = 16
GLA_TAU = 16.0
GLA_CHUNK = 64
GLA_CPB = 4
GLA_BLK = GLA_CPB * GLA_CHUNK
HEAD_DIM = 128
N_HEADS = 8
KV_HEADS = 2
Q_PER_KV = N_HEADS // KV_HEADS
AXIS_DIM = 64
ROPE_THETA = 10000.0
D_FF = 4096
EPS = 1e-6

T_PROMPT = BATCH * SEQ
T_SAMPLE = DEC_BATCH * DEC_SEQ
T_TOK = T_PROMPT + T_SAMPLE
COND_ROWS = 8
COND_SPAN = 1024
PROMPT_SPANS = T_PROMPT // COND_SPAN

S5_Q = 16
S5_W = S5_Q * S5_GROUP_CH
S5_GPB = 128 // S5_GROUP_CH
S5_ROWS = T_TOK // S5_Q
S5_PROMPT_ROWS = T_PROMPT // S5_Q
S5_PROMPT_CHUNKS = SEQ // S5_Q
S5_SAMPLE_CHUNKS = DEC_SEQ // S5_Q

VMEM_LIMIT = 56 * 1024 * 1024

NT_DIMS = (((1,), (1,)), ((), ()))
TN_DIMS = (((0,), (0,)), ((), ()))


def _cond_row(i, tile):
    return jnp.maximum((i * tile) // COND_SPAN - (PROMPT_SPANS - 1), 0)


def _norm_mod(x, gain, shift, scale):
    y = x * lax.rsqrt(jnp.mean(x * x, axis=-1, keepdims=True) + EPS)
    return (y * gain) * (1.0 + scale) + shift


def _dot(a, b):
    return jnp.dot(a, b, preferred_element_type=F32)


def _ada_kernel(cond_ref, w_ref, b_ref, o_ref):
    s = jax.nn.silu(cond_ref[...]).astype(BF16)
    o_ref[...] = _dot(s, w_ref[...].astype(BF16)) + b_ref[...]


def _ada_call(cond8, w_ada, b_ada):
    tn = 2048
    nj = 6 * D_MODEL // tn
    return pl.pallas_call(
        _ada_kernel,
        out_shape=jax.ShapeDtypeStruct((DEPTH, COND_ROWS, 6 * D_MODEL), F32),
        grid=(DEPTH, nj),
        in_specs=[
            pl.BlockSpec((COND_ROWS, D_MODEL), lambda l, j: (0, 0)),
            pl.BlockSpec((None, D_MODEL, tn), lambda l, j: (l, 0, j)),
            pl.BlockSpec((None, 1, tn), lambda l, j: (l, 0, j)),
        ],
        out_specs=pl.BlockSpec((None, COND_ROWS, tn), lambda l, j: (l, 0, j)),
        compiler_params=pltpu.CompilerParams(vmem_limit_bytes=VMEM_LIMIT),
        name="ada_mod",
    )(cond8, w_ada, b_ada.reshape(DEPTH, 1, 6 * D_MODEL))


def _token_specs(tile, width=D_MODEL):
    n_prompt = T_PROMPT // tile
    return [pl.BlockSpec((tile, width), lambda i: (jnp.minimum(i, n_prompt - 1), 0)),
            pl.BlockSpec((tile, width), lambda i: (jnp.maximum(i - n_prompt, 0), 0))]


def _token_tile(xp_ref, xs_ref, tile):
    return jnp.where(pl.program_id(0) < T_PROMPT // tile, xp_ref[...], xs_ref[...])


def _inproj_kernel(xp_ref, xs_ref, gn_ref, m_ref, w_ref, wglr_ref, wg_ref, bg_ref, u_ref, z_ref, g_ref, *, tile):
    x = _token_tile(xp_ref, xs_ref, tile)
    h = _norm_mod(x, gn_ref[...], m_ref[:, 0:D_MODEL], m_ref[:, D_MODEL:2 * D_MODEL]).astype(BF16)
    z = _dot(h, w_ref[...])
    for blk in range(S5_WIDTH // 128):
        u_ref[blk] = z[:, blk * 128:(blk + 1) * 128]
    z_ref[...] = z[:, S5_WIDTH:]
    glr = _dot(h, wglr_ref[...]).astype(BF16)
    pre = _dot(glr, wg_ref[...]) + bg_ref[...]
    g_ref[...] = jax.nn.log_sigmoid(pre) * (1.0 / GLA_TAU)


def _inproj_call(xp, xs, gn, mods, layer, w_main, w_glr, w_gate, b_gate):
    tm = 512
    nz = w_main.shape[1]
    return pl.pallas_call(
        functools.partial(_inproj_kernel, tile=tm),
        out_shape=(jax.ShapeDtypeStruct((S5_WIDTH // 128, T_TOK, 128), F32),
                   jax.ShapeDtypeStruct((T_TOK, nz - S5_WIDTH), F32),
                   jax.ShapeDtypeStruct((T_TOK, 2 * GLA_QK), F32)),
        grid=(T_TOK // tm,),
        in_specs=_token_specs(tm) + [
            pl.BlockSpec((1, D_MODEL), lambda i: (0, 0)),
            pl.BlockSpec((None, 1, 6 * D_MODEL), lambda i: (layer * COND_ROWS + _cond_row(i, tm), 0, 0)),
            pl.BlockSpec((D_MODEL, nz), lambda i: (0, 0)),
            pl.BlockSpec((D_MODEL, 128), lambda i: (0, 0)),
            pl.BlockSpec((128, 2 * GLA_QK), lambda i: (0, 0)),
            pl.BlockSpec((1, 2 * GLA_QK), lambda i: (0, 0)),
        ],
        out_specs=(pl.BlockSpec((S5_WIDTH // 128, tm, 128), lambda i: (0, i, 0)),
                   pl.BlockSpec((tm, nz - S5_WIDTH), lambda i: (i, 0)),
                   pl.BlockSpec((tm, 2 * GLA_QK), lambda i: (i, 0))),
        compiler_params=pltpu.CompilerParams(vmem_limit_bytes=VMEM_LIMIT),
        name="even_inproj",
    )(xp, xs, gn, mods, w_main, w_glr, w_gate, b_gate)


def _s5_prep_kernel(lre_ref, lim_ref, ldt_ref, btre_ref, btim_ref, cre_ref, cim_ref, ccf_ref, ccb_ref,
                    t_ref, bq_ref, cqt_ref, be_ref, a_ref, t_scr, dd_scr):
    lre = lre_ref[...]
    lim = lim_ref[...]
    dt = jnp.exp(ldt_ref[...])
    a = lre * dt
    th = lim * dt

    def lam_pow(k):
        mag = jnp.exp(k * a)
        return mag * jnp.cos(k * th), mag * jnp.sin(k * th)

    lb_re, lb_im = lam_pow(1.0)
    nr = lb_re - 1.0
    den = lre * lre + lim * lim
    cf_re = (nr * lre + lb_im * lim) / den
    cf_im = (lb_im * lre - nr * lim) / den
    bt_re = btre_ref[...]
    bt_im = btim_ref[...]
    bb_re = jnp.tile(cf_re * bt_re - cf_im * bt_im, (S5_Q, 1))
    bb_im = jnp.tile(cf_re * bt_im + cf_im * bt_re, (S5_Q, 1))

    shape = (S5_W, 128)
    pos = lax.shift_right_logical(lax.broadcasted_iota(jnp.int32, shape, 0), 4)
    is_f = lax.broadcasted_iota(jnp.int32, shape, 1) < S5_STATE
    posq = lax.broadcasted_iota(jnp.int32, (S5_Q, 128), 0).astype(F32)
    is_fq = lax.broadcasted_iota(jnp.int32, (S5_Q, 128), 1) < S5_STATE

    def per_channel(tbl):
        return jnp.broadcast_to(tbl[:, None, :], (S5_Q, S5_GROUP_CH, 128)).reshape(shape)

    p_re, p_im = map(per_channel, lam_pow(jnp.where(is_fq, (S5_Q - 1.0) - posq, posq)))
    w_re = p_re * bb_re - p_im * bb_im
    w_im = p_re * bb_im + p_im * bb_re
    bq = jnp.concatenate([w_re, w_im], axis=1)
    bqt = jnp.transpose(bq)
    bq_ref[...] = bqt.astype(BF16)

    edge = pos == jnp.where(is_f, 0, S5_Q - 1)
    be = jnp.concatenate([jnp.where(edge, bb_re, 0.0), jnp.where(edge, bb_im, 0.0)], axis=1)
    be_ref[...] = jnp.transpose(be).astype(BF16)

    q_re, q_im = map(per_channel, lam_pow(jnp.where(is_fq, posq + 1.0, S5_Q - posq)))
    ct_re = jnp.tile(cre_ref[...], (S5_Q, 1))
    ct_im = jnp.tile(cim_ref[...], (S5_Q, 1))
    g_re = q_re * ct_re - q_im * ct_im
    g_im = q_re * ct_im + q_im * ct_re
    cqt_ref[...] = jnp.concatenate([g_re, -g_im], axis=1).astype(BF16)

    a_re, a_im = lam_pow(float(S5_Q))
    a_ref[...] = jnp.concatenate([a_re, a_im], axis=1)

    kf = jnp.dot(ccf_ref[...], bqt, precision=lax.Precision.HIGHEST, preferred_element_type=F32)
    kb = jnp.dot(ccb_ref[...], bqt, precision=lax.Precision.HIGHEST, preferred_element_type=F32)
    gch = S5_GROUP_CH
    lo = S5_W - gch
    dd_scr[:, 0:S5_W] = kf
    dd_scr[:, lo:lo + S5_W] = kb
    dd_scr[:, lo:S5_W] = kf[:, lo:S5_W] + kb[:, 0:gch]
    for t in range(S5_Q):
        c0 = (S5_Q - 1 - t) * gch
        t_scr[t * gch:(t + 1) * gch, :] = dd_scr[:, c0:c0 + S5_W]
    t_ref[...] = t_scr[...].astype(BF16)


def _s5_prep_call(lam_re, lam_im, log_dt, b_re, b_im, c_re, c_im):
    def fb(p):
        return jnp.transpose(p, (1, 0, 2)).reshape(S5_GROUPS, 1, 2 * S5_STATE)

    def dup(p):
        return jnp.concatenate([p, p], axis=-1)

    ldt = fb(jnp.broadcast_to(log_dt[:, :, None], (2, S5_GROUPS, S5_STATE)))
    bt_re = dup(jnp.transpose(b_re, (0, 2, 1)))
    bt_im = dup(jnp.transpose(b_im, (0, 2, 1)))
    zero = jnp.zeros_like(c_re)
    ccf = jnp.concatenate([c_re, zero, -c_im, zero], axis=-1)
    ccb = jnp.concatenate([zero, c_re, zero, -c_im], axis=-1)

    row = pl.BlockSpec((None, 1, 128), lambda g: (g, 0, 0))
    mat16 = pl.BlockSpec((None, S5_GROUP_CH, 128), lambda g: (g, 0, 0))
    mat16w = pl.BlockSpec((None, S5_GROUP_CH, S5_W), lambda g: (g, 0, 0))
    sq = pl.BlockSpec((None, S5_W, S5_W), lambda g: (g, 0, 0))
    sq_shape = jax.ShapeDtypeStruct((S5_GROUPS, S5_W, S5_W), BF16)
    return pl.pallas_call(
        _s5_prep_kernel,
        out_shape=(sq_shape, sq_shape, sq_shape, sq_shape,
                   jax.ShapeDtypeStruct((S5_GROUPS, 1, S5_W), F32)),
        grid=(S5_GROUPS,),
        in_specs=[row, row, row, mat16, mat16, mat16, mat16, mat16w, mat16w],
        out_specs=(sq, sq, sq, sq, pl.BlockSpec((None, 1, S5_W), lambda g: (g, 0, 0))),
        scratch_shapes=[pltpu.VMEM((S5_W, S5_W), F32), pltpu.VMEM((S5_GROUP_CH, 2 * S5_W), F32)],
        name="s5_prep",
    )(fb(lam_re), fb(lam_im), ldt, bt_re, bt_im, dup(c_re), dup(c_im), ccf, ccb)


def _s5_kernel(u_ref, tt_ref, bqt_ref, cqt_ref, bet_ref, a_ref, h0_ref, y_ref, ns_ref,
               ut_scr, x_scr, spf_scr, spb_scr, ne_scr, yt_scr, xt_scr):
    gch = S5_GROUP_CH
    for s in range(S5_Q):
        rows = u_ref[pl.ds(s, S5_ROWS, stride=S5_Q), :]
        rows_t = jnp.transpose(rows).astype(BF16)
        for gl in range(S5_GPB):
            ut_scr[gl, s * gch:(s + 1) * gch, :] = rows_t[gl * gch:(gl + 1) * gch, :]

    for gl in range(S5_GPB):
        ut = ut_scr[gl]
        xt_scr[...] = _dot(bqt_ref[gl], ut)
        x = jnp.transpose(xt_scr[...])
        xt_scr[:, 0:S5_PROMPT_ROWS] = _dot(bet_ref[gl], ut[:, 0:S5_PROMPT_ROWS])
        ne = jnp.transpose(xt_scr[:, 0:S5_PROMPT_ROWS])
        for part in range(2):
            x_scr[part, pl.ds(gl, S5_ROWS, stride=S5_GPB), :] = x[:, part * 128:(part + 1) * 128]
            ne_scr[part, pl.ds(gl, S5_PROMPT_ROWS, stride=S5_GPB), :] = ne[:, part * 128:(part + 1) * 128]

    is_f = lax.broadcasted_iota(jnp.int32, (1, 128), 1) < S5_STATE
    a_re = a_ref[:, 0:128]
    a_im = a_ref[:, 128:256]

    def tile(row):
        return pl.ds(pl.multiple_of(row * S5_GPB, S5_GPB), S5_GPB)

    def scan(base, nseq, nchunk, s_init):
        def body(i, state):
            new = []
            for b in range(nseq):
                s_re, s_im = state[b]
                rows_f = tile(base + b * nchunk + i)
                rows_b = tile(base + b * nchunk + (nchunk - 1 - i))
                spf_scr[0, rows_f, :] = s_re
                spf_scr[1, rows_f, :] = s_im
                spb_scr[0, rows_b, :] = s_re
                spb_scr[1, rows_b, :] = s_im
                x_re = jnp.where(is_f, x_scr[0, rows_f, :], x_scr[0, rows_b, :])
                x_im = jnp.where(is_f, x_scr[1, rows_f, :], x_scr[1, rows_b, :])
                new.append((a_re * s_re - a_im * s_im + x_re, a_re * s_im + a_im * s_re + x_im))
            return tuple(new)

        lax.fori_loop(0, nchunk, body, tuple(s_init))

    zero = jnp.zeros((S5_GPB, 128), F32)
    scan(0, BATCH, S5_PROMPT_CHUNKS, [(zero, zero)] * BATCH)
    scan(S5_PROMPT_ROWS, DEC_BATCH, S5_SAMPLE_CHUNKS,
         [(h0_ref[b, :, 0:128], h0_ref[b, :, 128:256]) for b in range(DEC_BATCH)])

    for b in range(BATCH):
        first = pl.ds(b * S5_PROMPT_CHUNKS * S5_GPB, S5_GPB)
        last = pl.ds(((b + 1) * S5_PROMPT_CHUNKS - 1) * S5_GPB, S5_GPB)
        for part in range(2):
            ns_ref[b, :, part * 128:(part + 1) * 128] = jnp.where(is_f, ne_scr[part, first, :], ne_scr[part, last, :])

    for gl in range(S5_GPB):
        rows = pl.ds(gl, S5_ROWS, stride=S5_GPB)
        carried = jnp.concatenate([jnp.where(is_f, spf_scr[p, rows, :], spb_scr[p, rows, :]) for p in range(2)],
                                  axis=1).astype(BF16)
        yt = _dot(tt_ref[gl], ut_scr[gl]) + lax.dot_general(cqt_ref[gl], carried, NT_DIMS,
                                                            preferred_element_type=F32)
        for t in range(S5_Q):
            yt_scr[t, gl * gch:(gl + 1) * gch, :] = yt[t * gch:(t + 1) * gch, :]
    for t in range(S5_Q):
        y_ref[pl.ds(t, S5_ROWS, stride=S5_Q), :] = jnp.transpose(yt_scr[t])


def _s5_call(u, mats, h0):
    tt_m, bqt_m, cqt_m, bet_m, a_m = mats
    nsteps = S5_GROUPS // S5_GPB
    sq = pl.BlockSpec((S5_GPB, S5_W, S5_W), lambda g: (g, 0, 0))
    state_scr = pltpu.VMEM((2, S5_ROWS * S5_GPB, 128), F32)
    return pl.pallas_call(
        _s5_kernel,
        out_shape=(jax.ShapeDtypeStruct((nsteps, T_TOK, 128), F32),
                   jax.ShapeDtypeStruct((nsteps, BATCH, S5_GPB, S5_W), F32)),
        grid=(nsteps,),
        in_specs=[
            pl.BlockSpec((None, T_TOK, 128), lambda g: (g, 0, 0)),
            sq, sq, sq, sq,
            pl.BlockSpec((S5_GPB, S5_W), lambda g: (g, 0)),
            pl.BlockSpec((None, DEC_BATCH, S5_GPB, S5_W), lambda g: (g, 0, 0, 0)),
        ],
        out_specs=(pl.BlockSpec((None, T_TOK, 128), lambda g: (g, 0, 0)),
                   pl.BlockSpec((None, BATCH, S5_GPB, S5_W), lambda g: (g, 0, 0, 0))),
        scratch_shapes=[pltpu.VMEM((S5_GPB, S5_W, S5_ROWS), BF16), state_scr, state_scr, state_scr,
                        pltpu.VMEM((2, S5_PROMPT_ROWS * S5_GPB, 128), F32),
                        pltpu.VMEM((S5_Q, 128, S5_ROWS), F32), pltpu.VMEM((S5_W, S5_ROWS), F32)],
        compiler_params=pltpu.CompilerParams(vmem_limit_bytes=VMEM_LIMIT),
        name="s5_scan",
    )(u, tt_m, bqt_m, cqt_m, bet_m, a_m.reshape(S5_GROUPS, S5_W), h0)


def _split_bf16(x):
    hi = x.astype(BF16)
    r1 = x - hi.astype(F32)
    mid = r1.astype(BF16)
    lo = (r1 - mid.astype(F32)).astype(BF16)
    return hi, mid, lo


def _gla_kernel(*refs, seq_len, has_s0):
    q_ref, k_ref, v_ref, gf_ref, gb_ref, r_ref, gn_ref = refs[:7]
    s0_ref = refs[7] if has_s0 else None
    o_ref, sfin_ref, oi_scr, qd_scr, kv_scr, dec_scr, ss_scr = refs[7 + has_s0:]
    nblk = seq_len // GLA_BLK
    nchunk = seq_len // GLA_CHUNK
    cl = GLA_CHUNK
    ti = lax.broadcasted_iota(jnp.int32, (GLA_BLK, GLA_BLK), 0)
    si = lax.broadcasted_iota(jnp.int32, (GLA_BLK, GLA_BLK), 1)
    same = lax.shift_right_logical(ti, 6) == lax.shift_right_logical(si, 6)
    keep = (same & (ti >= si), same & (ti <= si))
    tri = tuple(kp.astype(BF16) for kp in keep)
    lane_head = lax.shift_right_logical(lax.broadcasted_iota(jnp.int32, (cl, GLA_QK), 1), 6)
    zeros_v = jnp.zeros((cl, GLA_DV), BF16)
    heads = [(slice(h * GLA_DK, (h + 1) * GLA_DK), slice(h * GLA_DV, (h + 1) * GLA_DV)) for h in range(GLA_HEADS)]

    for j in range(nblk):
        rows = slice(j * GLA_BLK, (j + 1) * GLA_BLK)
        q = q_ref[rows, :] * (GLA_DK ** -0.5)
        k = k_ref[rows, :]
        v = v_ref[rows, :].astype(BF16)
        qd, kd, k2t = [], [], []
        for d, g_ref in enumerate((gf_ref, gb_ref)):
            b = sum(_dot(tri[d], part) for part in _split_bf16(g_ref[rows, :]))
            last = cl - 1 if d == 0 else 0
            b_last = [b[c * cl + last:c * cl + last + 1] for c in range(GLA_CPB)]
            bl = jnp.concatenate([jnp.broadcast_to(x, (cl, GLA_QK)) for x in b_last], axis=0)
            qd_d = (q * jnp.exp(b)).astype(BF16)
            qd_scr[d, rows, :] = qd_d
            qd.append(qd_d)
            kd.append((k * jnp.exp(-b)).astype(BF16))
            k2t.append(jnp.transpose(k * jnp.exp(bl - b)).astype(BF16))
            for c in range(GLA_CPB):
                dec_scr[d, j * GLA_CPB + c] = jnp.exp(jnp.transpose(jnp.broadcast_to(b_last[c], (GLA_DV, GLA_QK))))
        for h, (ks, vs) in enumerate(heads):
            att = [jnp.where(keep[d], lax.dot_general(qd[d][:, ks], kd[d][:, ks], NT_DIMS,
                                                      preferred_element_type=F32), 0.0) for d in range(2)]
            oi_scr[rows, vs] = _dot((att[0] + att[1]).astype(BF16), v[:, vs])
            vh = v[:, vs]
            vexp = jnp.concatenate(
                [jnp.concatenate([vh[c * cl:(c + 1) * cl] if c2 == c else zeros_v for c2 in range(GLA_CPB)], axis=1)
                 for c in range(GLA_CPB)], axis=0)
            for d in range(2):
                kv_scr[d, j, h] = _dot(k2t[d][ks, :], vexp)

    for d in range(2):
        s = s0_ref[d] if has_s0 else jnp.zeros((GLA_QK, GLA_DV), F32)
        for cg in (range(nchunk) if d == 0 else range(nchunk - 1, -1, -1)):
            j, c = divmod(cg, GLA_CPB)
            ss_scr[d, cg] = s.astype(BF16)
            kv = jnp.concatenate([kv_scr[d, j, h, :, c * GLA_DV:(c + 1) * GLA_DV] for h in range(GLA_HEADS)], axis=0)
            s = s * dec_scr[d, cg] + kv
        sfin_ref[d] = s

    for cg in range(nchunk):
        rows = slice(cg * cl, (cg + 1) * cl)
        inter = []
        for d in range(2):
            qc = qd_scr[d, rows, :]
            qstack = jnp.concatenate([jnp.where(lane_head == h, qc, jnp.zeros_like(qc)) for h in range(GLA_HEADS)],
                                     axis=0)
            inter.append(_dot(qstack, ss_scr[d, cg]))
        gate = jax.nn.silu(r_ref[rows, :])
        for h, (ks, vs) in enumerate(heads):
            hr = slice(h * cl, (h + 1) * cl)
            oh = oi_scr[rows, vs] + inter[0][hr] + inter[1][hr]
            oh = oh * lax.rsqrt(jnp.mean(oh * oh, axis=-1, keepdims=True) + EPS) * gn_ref[...]
            o_ref[rows, vs] = oh * gate[:, vs]


def _gla_call(z, g, gla_norm, s0, seq_len, nseq, row0):
    assert row0 % seq_len == 0
    r0 = row0 // seq_len
    has_s0 = s0 is not None
    qk_off = 0
    v_off = 2 * GLA_QK // GLA_VW
    in_specs = [
        pl.BlockSpec((seq_len, GLA_QK), lambda i: (r0 + i, qk_off)),
        pl.BlockSpec((seq_len, GLA_QK), lambda i: (r0 + i, qk_off + 1)),
        pl.BlockSpec((seq_len, GLA_VW), lambda i: (r0 + i, v_off)),
        pl.BlockSpec((seq_len, GLA_QK), lambda i: (r0 + i, 0)),
        pl.BlockSpec((seq_len, GLA_QK), lambda i: (r0 + i, 1)),
        pl.BlockSpec((seq_len, GLA_VW), lambda i: (r0 + i, v_off + 1)),
        pl.BlockSpec((1, GLA_DV), lambda i: (0, 0)),
    ]
    args = [z, z, z, g, g, z, gla_norm]
    if has_s0:
        in_specs.append(pl.BlockSpec((None, 2, GLA_QK, GLA_DV), lambda i: (i, 0, 0, 0)))
        args.append(s0)
    return pl.pallas_call(
        functools.partial(_gla_kernel, seq_len=seq_len, has_s0=has_s0),
        out_shape=(jax.ShapeDtypeStruct((nseq * seq_len, GLA_VW), F32),
                   jax.ShapeDtypeStruct((nseq, 2, GLA_QK, GLA_DV), F32)),
        grid=(nseq,),
        in_specs=in_specs,
        out_specs=(pl.BlockSpec((seq_len, GLA_VW), lambda i: (i, 0)),
                   pl.BlockSpec((None, 2, GLA_QK, GLA_DV), lambda i: (i, 0, 0, 0))),
        scratch_shapes=[
            pltpu.VMEM((seq_len, GLA_VW), F32),
            pltpu.VMEM((2, seq_len, GLA_QK), BF16),
            pltpu.VMEM((2, seq_len // GLA_BLK, GLA_HEADS, GLA_DK, GLA_CPB * GLA_DV), F32),
            pltpu.VMEM((2, seq_len // GLA_CHUNK, GLA_QK, GLA_DV), F32),
            pltpu.VMEM((2, seq_len // GLA_CHUNK, GLA_QK, GLA_DV), BF16),
        ],
        compiler_params=pltpu.CompilerParams(vmem_limit_bytes=VMEM_LIMIT),
        name=f"gla_len{seq_len}",
    )(*args)


def _mlp_tail(x, mix, m_ref, gn2_ref, w1_ref, w2_ref):
    y1 = x + m_ref[:, 2 * D_MODEL:3 * D_MODEL] * mix
    h = _norm_mod(y1, gn2_ref[...], m_ref[:, 3 * D_MODEL:4 * D_MODEL], m_ref[:, 4 * D_MODEL:5 * D_MODEL]).astype(BF16)
    tf = 512
    acc = jnp.zeros(y1.shape, F32)
    for c in range(D_FF // tf):
        a = _dot(h, w1_ref[:, c * tf:(c + 1) * tf])
        a = jnp.square(jnp.maximum(a, 0.0)).astype(BF16)
        acc = acc + _dot(a, w2_ref[c * tf:(c + 1) * tf, :])
    return y1 + m_ref[:, 5 * D_MODEL:6 * D_MODEL] * acc


def _even_out_kernel(xp_ref, xs_ref, y5_ref, u_ref, dskip_ref, wglu_ref, bglu_ref, glap_ref, glas_ref, wout_ref,
                     m_ref, gn2_ref, w1_ref, w2_ref, o_ref):
    nblk = S5_WIDTH // 128
    ys = (jnp.concatenate([y5_ref[b] for b in range(nblk)], axis=1)
          + jnp.concatenate([u_ref[b] for b in range(nblk)], axis=1) * dskip_ref[...])
    gl = jax.nn.gelu(ys)
    s5o = gl * jax.nn.sigmoid(_dot(gl.astype(BF16), wglu_ref[...]) + bglu_ref[...])
    gla = _token_tile(glap_ref, glas_ref, _OUT_TM).astype(BF16)
    mix = _dot(s5o.astype(BF16), wout_ref[0:S5_WIDTH, :]) + _dot(gla, wout_ref[S5_WIDTH:, :])
    o_ref[...] = _mlp_tail(_token_tile(xp_ref, xs_ref, _OUT_TM), mix, m_ref, gn2_ref, w1_ref, w2_ref)


def _odd_out_kernel(x_ref, attp_ref, atts_ref, wo_ref, m_ref, gn2_ref, w1_ref, w2_ref, op_ref, os_ref):
    mix = _dot(_token_tile(attp_ref, atts_ref, _OUT_TM), wo_ref[...])
    y = _mlp_tail(x_ref[...], mix, m_ref, gn2_ref, w1_ref, w2_ref)
    is_prompt = pl.program_id(0) < T_PROMPT // _OUT_TM

    @pl.when(is_prompt)
    def _():
        op_ref[...] = y

    @pl.when(jnp.logical_not(is_prompt))
    def _():
        os_ref[...] = y


_OUT_TM = 512


def _const_spec(shape):
    return pl.BlockSpec(shape, lambda i: (0,) * len(shape), pipeline_mode=pl.Buffered(1))


def _tail_specs(layer):
    tm = _OUT_TM
    return [
        pl.BlockSpec((None, 1, 6 * D_MODEL), lambda i: (layer * COND_ROWS + _cond_row(i, tm), 0, 0)),
        _const_spec((1, D_MODEL)),
        _const_spec((D_MODEL, D_FF)),
        _const_spec((D_FF, D_MODEL)),
    ]


def _even_out_call(xp, xs, y5, u, d_skip, w_glu, b_glu, gla_p, gla_s, w_out, mods, layer, gn2, w1, w2):
    tm = _OUT_TM
    return pl.pallas_call(
        _even_out_kernel,
        out_shape=jax.ShapeDtypeStruct((T_TOK, D_MODEL), F32),
        grid=(T_TOK // tm,),
        in_specs=_token_specs(tm) + [
            pl.BlockSpec((S5_WIDTH // 128, tm, 128), lambda i: (0, i, 0)),
            pl.BlockSpec((S5_WIDTH // 128, tm, 128), lambda i: (0, i, 0)),
            _const_spec((1, S5_WIDTH)),
            _const_spec((S5_WIDTH, S5_WIDTH)),
            _const_spec((1, S5_WIDTH)),
        ] + _token_specs(tm, GLA_VW) + [
            _const_spec((S5_WIDTH + GLA_VW, D_MODEL)),
        ] + _tail_specs(layer),
        out_specs=pl.BlockSpec((tm, D_MODEL), lambda i: (i, 0)),
        compiler_params=pltpu.CompilerParams(vmem_limit_bytes=VMEM_LIMIT),
        name="even_out_mlp",
    )(xp, xs, y5, u, d_skip, w_glu, b_glu, gla_p, gla_s, w_out, mods, gn2, w1, w2)


def _odd_out_call(x, att_p, att_s, w_o, mods, layer, gn2, w1, w2):
    tm = _OUT_TM
    return pl.pallas_call(
        _odd_out_kernel,
        out_shape=(jax.ShapeDtypeStruct((T_PROMPT, D_MODEL), F32),
                   jax.ShapeDtypeStruct((T_SAMPLE, D_MODEL), F32)),
        grid=(T_TOK // tm,),
        in_specs=[pl.BlockSpec((tm, D_MODEL), lambda i: (i, 0))] + _token_specs(tm) + [
            _const_spec((D_MODEL, D_MODEL)),
        ] + _tail_specs(layer),
        out_specs=tuple(_token_specs(tm)),
        compiler_params=pltpu.CompilerParams(vmem_limit_bytes=VMEM_LIMIT),
        name="odd_out_mlp",
    )(x, att_p, att_s, w_o, mods, gn2, w1, w2)


def _qkv_kernel(x_ref, gn_ref, m_ref, w_ref, qn_ref, kn_ref, cos_ref, sin_ref,
                q_ref, kb_ref, vb_ref, k32_ref, v32_ref, *, tile):
    h = _norm_mod(x_ref[...], gn_ref[...], m_ref[:, 0:D_MODEL], m_ref[:, D_MODEL:2 * D_MODEL]).astype(BF16)
    z = _dot(h, w_ref[...])
    v = z[:, (N_HEADS + KV_HEADS) * HEAD_DIM:]
    vb_ref[...] = v.astype(BF16)
    even_lane = (lax.broadcasted_iota(jnp.int32, (1, HEAD_DIM), 1) & 1) == 0

    pw = 2 * HEAD_DIM
    head_of = lambda a: lax.shift_right_logical(lax.broadcasted_iota(jnp.int32, (pw, pw), a), 7)
    same_head = (head_of(0) == head_of(1)).astype(BF16)
    inv_rms = []
    for p in range((N_HEADS + KV_HEADS) // 2):
        sq = jnp.square(z[:, p * pw:(p + 1) * pw])
        hi = sq.astype(BF16)
        lo = (sq - hi.astype(F32)).astype(BF16)
        ss = _dot(hi, same_head) + _dot(lo, same_head)
        inv_rms.append(lax.rsqrt(ss * (1.0 / HEAD_DIM) + EPS))

    def heads(rope):
        for hd in range(N_HEADS + KV_HEADS):
            xh = z[:, hd * HEAD_DIM:(hd + 1) * HEAD_DIM]
            gain = qn_ref[...] if hd < N_HEADS else kn_ref[...]
            xh = xh * inv_rms[hd // 2][:, (hd % 2) * HEAD_DIM:(hd % 2 + 1) * HEAD_DIM] * gain
            if rope:
                partner = jnp.where(even_lane, pltpu.roll(xh, HEAD_DIM - 1, 1), pltpu.roll(xh, 1, 1))
                xh = xh * cos_ref[...] + partner * sin_ref[...]
            if hd < N_HEADS:
                q_ref[:, hd * HEAD_DIM:(hd + 1) * HEAD_DIM] = xh.astype(BF16)
            else:
                cols = slice((hd - N_HEADS) * HEAD_DIM, (hd - N_HEADS + 1) * HEAD_DIM)
                kb_ref[:, cols] = xh.astype(BF16)
                if not rope:
                    k32_ref[:, cols] = xh

    is_sample = pl.program_id(0) >= T_PROMPT // tile

    @pl.when(is_sample)
    def _():
        heads(True)

    @pl.when(jnp.logical_not(is_sample))
    def _():
        heads(False)
        v32_ref[...] = v


def _qkv_call(x, gn, mods, layer, w_qkv, q_norm, k_norm, cos_t, sin_t):
    tm = 512
    pos_tiles = DEC_SEQ // tm
    n_prompt = T_PROMPT // tm
    kvw = KV_HEADS * HEAD_DIM

    def pos_map(i):
        return (jnp.maximum(i - n_prompt, 0) % pos_tiles, 0)

    def prompt_map(i):
        return (jnp.minimum(i, n_prompt - 1), 0)

    return pl.pallas_call(
        functools.partial(_qkv_kernel, tile=tm),
        out_shape=(jax.ShapeDtypeStruct((T_TOK, N_HEADS * HEAD_DIM), BF16),
                   jax.ShapeDtypeStruct((T_TOK, kvw), BF16),
                   jax.ShapeDtypeStruct((T_TOK, kvw), BF16),
                   jax.ShapeDtypeStruct((T_PROMPT, kvw), F32),
                   jax.ShapeDtypeStruct((T_PROMPT, kvw), F32)),
        grid=(T_TOK // tm,),
        in_specs=[
            pl.BlockSpec((tm, D_MODEL), lambda i: (i, 0)),
            pl.BlockSpec((1, D_MODEL), lambda i: (0, 0)),
            pl.BlockSpec((None, 1, 6 * D_MODEL), lambda i: (layer * COND_ROWS + _cond_row(i, tm), 0, 0)),
            pl.BlockSpec(w_qkv.shape, lambda i: (0, 0)),
            pl.BlockSpec((1, HEAD_DIM), lambda i: (0, 0)),
            pl.BlockSpec((1, HEAD_DIM), lambda i: (0, 0)),
            pl.BlockSpec((tm, HEAD_DIM), pos_map),
            pl.BlockSpec((tm, HEAD_DIM), pos_map),
        ],
        out_specs=(pl.BlockSpec((tm, N_HEADS * HEAD_DIM), lambda i: (i, 0)),
                   pl.BlockSpec((tm, kvw), lambda i: (i, 0)),
                   pl.BlockSpec((tm, kvw), lambda i: (i, 0)),
                   pl.BlockSpec((tm, kvw), prompt_map),
                   pl.BlockSpec((tm, kvw), prompt_map)),
        compiler_params=pltpu.CompilerParams(vmem_limit_bytes=VMEM_LIMIT),
        name="odd_qkv",
    )(x, gn, mods, w_qkv, q_norm, k_norm, cos_t, sin_t)


def _rope_tables():
    rows = DEC_SEQ // GRID_W
    row = jnp.repeat(jnp.arange(rows, dtype=F32), GRID_W)
    col = jnp.tile(jnp.arange(GRID_W, dtype=F32), rows)
    inv = ROPE_THETA ** (-jnp.arange(0, AXIS_DIM, 2, dtype=F32) / AXIS_DIM)
    ang = jnp.concatenate([row[:, None] * inv, col[:, None] * inv], axis=-1)
    cos_t = jnp.repeat(jnp.cos(ang), 2, axis=-1)
    sin = jnp.sin(ang)
    sin_t = jnp.stack([-sin, sin], axis=-1).reshape(DEC_SEQ, HEAD_DIM)
    return cos_t, sin_t


def _attn_kernel(*refs, has_cache):
    q_ref, k_ref, v_ref = refs[:3]
    ck_ref, cv_ref = refs[3:5] if has_cache else (None, None)
    o_ref = refs[-1]
    c = HEAD_DIM ** -0.5 * math.log2(math.e)
    ones_col = (lax.broadcasted_iota(jnp.int32, (1, HEAD_DIM), 1) == 0).astype(BF16)

    def with_ones(v):
        return jnp.concatenate([v, jnp.broadcast_to(ones_col, v.shape)], axis=1)

    k = k_ref[...]
    v = with_ones(v_ref[...])
    if has_cache:
        ck = ck_ref[...].astype(BF16)
        cv = with_ones(cv_ref[...].astype(BF16))
    for r in range(Q_PER_KV):
        cs = slice(r * HEAD_DIM, (r + 1) * HEAD_DIM)
        q = q_ref[:, cs]
        s = lax.dot_general(q, k, NT_DIMS, preferred_element_type=F32)
        m = jnp.max(s, axis=-1, keepdims=True)
        if has_cache:
            sc = lax.dot_general(q, ck, NT_DIMS, preferred_element_type=F32)
            m = jnp.maximum(m, jnp.max(sc, axis=-1, keepdims=True))
        mc = m * c
        o = _dot(jnp.exp2(s * c - mc).astype(BF16), v)
        if has_cache:
            o = o + _dot(jnp.exp2(sc * c - mc).astype(BF16), cv)
        o_ref[:, cs] = (o[:, 0:HEAD_DIM] / o[:, HEAD_DIM:HEAD_DIM + 1]).astype(BF16)


def _attn_call(q, k, v, cache_k, cache_v, seq_len, row0, nrows):
    assert row0 % seq_len == 0 and nrows % seq_len == 0
    has_cache = cache_k is not None
    b0 = row0 // seq_len
    gw = Q_PER_KV * HEAD_DIM
    in_specs = [
        pl.BlockSpec((seq_len, gw), lambda b, g: (b0 + b, g)),
        pl.BlockSpec((seq_len, HEAD_DIM), lambda b, g: (b0 + b, g)),
        pl.BlockSpec((seq_len, HEAD_DIM), lambda b, g: (b0 + b, g)),
    ]
    args = [q, k, v]
    if has_cache:
        in_specs += [pl.BlockSpec((PAST_LEN, HEAD_DIM), lambda b, g: (b, g)),
                     pl.BlockSpec((PAST_LEN, HEAD_DIM), lambda b, g: (b, g))]
        args += [cache_k, cache_v]
    return pl.pallas_call(
        functools.partial(_attn_kernel, has_cache=has_cache),
        out_shape=jax.ShapeDtypeStruct((nrows, N_HEADS * HEAD_DIM), BF16),
        grid=(nrows // seq_len, KV_HEADS),
        in_specs=in_specs,
        out_specs=pl.BlockSpec((seq_len, gw), lambda b, g: (b, g)),
        compiler_params=pltpu.CompilerParams(vmem_limit_bytes=VMEM_LIMIT),
        name=f"attn_len{seq_len}",
    )(*args)


def kernel(x_prompt, x_sample, state_s5_re, state_s5_im, state_gla, cache_k, cache_v, c, c_ctx, norm_mix, norm_mlp, w_ada, b_ada, w_mlp_in, w_mlp_out, w_in_e, w_out_e, s5_lambda_re, s5_lambda_im, s5_log_dt, s5_b_re, s5_b_im, s5_c_re, s5_c_im, s5_d, s5_w_glu, s5_b_glu, gla_w_gate2, gla_b_gate, gla_norm, w_qkv_o, w_o_o, q_norm, k_norm):
    xp = x_prompt.reshape(T_PROMPT, D_MODEL)
    xs = x_sample.reshape(T_SAMPLE, D_MODEL)
    cond8 = jnp.concatenate([c_ctx[None, :], c, jnp.zeros((COND_ROWS - 1 - DEC_BATCH, D_MODEL), F32)], axis=0)
    mods = _ada_call(cond8, w_ada, b_ada).reshape(DEPTH * COND_ROWS, 1, 6 * D_MODEL)

    n_main = S5_WIDTH + 2 * GLA_QK + 2 * GLA_VW
    w_in = w_in_e[0]
    w_main = w_in[:, :n_main].astype(BF16)
    w_glr = jnp.pad(w_in[:, n_main:], ((0, 0), (0, 128 - 2 * GLA_RANK))).astype(BF16)
    w_gate = jnp.zeros((128, 2 * GLA_QK), F32)
    w_gate = w_gate.at[0:GLA_RANK, 0:GLA_QK].set(gla_w_gate2[0, 0])
    w_gate = w_gate.at[GLA_RANK:2 * GLA_RANK, GLA_QK:].set(gla_w_gate2[0, 1]).astype(BF16)
    b_gate = gla_b_gate[0].reshape(1, 2 * GLA_QK)
    u, z, g = _inproj_call(xp, xs, norm_mix[0:1], mods, 0, w_main, w_glr, w_gate, b_gate)

    mats = _s5_prep_call(s5_lambda_re[0], s5_lambda_im[0], s5_log_dt[0], s5_b_re[0], s5_b_im[0],
                         s5_c_re[0], s5_c_im[0])

    def state_rows(s):
        return jnp.transpose(s, (2, 0, 1, 3)).reshape(S5_GROUPS, DEC_BATCH, 2 * S5_STATE)

    h0 = jnp.concatenate([state_rows(state_s5_re[:, 0]), state_rows(state_s5_im[:, 0])], axis=-1)
    nsteps = S5_GROUPS // S5_GPB
    h0 = jnp.transpose(h0.reshape(nsteps, S5_GPB, DEC_BATCH, S5_W), (0, 2, 1, 3))
    y5, ns = _s5_call(u, mats, h0)
    ns = jnp.transpose(ns, (0, 2, 1, 3)).reshape(S5_GROUPS, BATCH, S5_W)

    def state_out(n):
        return jnp.transpose(n.reshape(S5_GROUPS, BATCH, 2, S5_STATE), (1, 2, 0, 3))[:, None]

    new_s5_re = state_out(ns[:, :, :2 * S5_STATE])
    new_s5_im = state_out(ns[:, :, 2 * S5_STATE:])

    gn_gla = gla_norm[0].reshape(1, GLA_DV)
    gla_p, sfin = _gla_call(z, g, gn_gla, None, SEQ, BATCH, 0)
    s0 = state_gla[:, 0].reshape(DEC_BATCH, 2, GLA_QK, GLA_DV)
    gla_s, _ = _gla_call(z, g, gn_gla, s0, DEC_SEQ, DEC_BATCH, T_PROMPT)
    new_gla = sfin.reshape(BATCH, 1, 2, GLA_HEADS, GLA_DK, GLA_DV)

    x = _even_out_call(xp, xs, y5, u, s5_d[0].reshape(1, S5_WIDTH), s5_w_glu[0].astype(BF16),
                       s5_b_glu[0].reshape(1, S5_WIDTH), gla_p, gla_s, w_out_e[0].astype(BF16), mods, 0,
                       norm_mlp[0:1], w_mlp_in[0].astype(BF16), w_mlp_out[0].astype(BF16))

    cos_t, sin_t = _rope_tables()
    q, k, v, k32, v32 = _qkv_call(x, norm_mix[1:2], mods, 1, w_qkv_o[0].astype(BF16),
                                  q_norm[0].reshape(1, HEAD_DIM), k_norm[0].reshape(1, HEAD_DIM), cos_t, sin_t)
    att_p = _attn_call(q, k, v, None, None, SEQ, 0, T_PROMPT)
    ck = cache_k[:, 0].reshape(DEC_BATCH * PAST_LEN, KV_HEADS * HEAD_DIM)
    cv = cache_v[:, 0].reshape(DEC_BATCH * PAST_LEN, KV_HEADS * HEAD_DIM)
    att_s = _attn_call(q, k, v, ck, cv, DEC_SEQ, T_PROMPT, T_SAMPLE)
    yp, ys = _odd_out_call(x, att_p, att_s, w_o_o[0].astype(BF16), mods, 1, norm_mlp[1:2],
                           w_mlp_in[1].astype(BF16), w_mlp_out[1].astype(BF16))

    new_k = k32.reshape(BATCH, 1, SEQ, KV_HEADS, HEAD_DIM)
    new_v = v32.reshape(BATCH, 1, SEQ, KV_HEADS, HEAD_DIM)
    y_prompt = yp.reshape(BATCH, SEQ, D_MODEL)
    y_sample = ys.reshape(DEC_BATCH, DEC_SEQ, D_MODEL)
    return (y_prompt, y_sample, new_s5_re, new_s5_im, new_gla, new_k, new_v)
```

```python
import functools
import math

import jax
import jax.numpy as jnp
import numpy as np
from jax import lax
from jax.experimental import pallas as pl
from jax.experimental.pallas import tpu as pltpu

F32 = jnp.float32
BF16 = jnp.bfloat16

D_MODEL = 1024
BATCH = 16
SEQ = 256
DEPTH = 2
DEC_BATCH = 4
DEC_SEQ = 1024
PAST_LEN = 512
GRID_W = 64
S5_WIDTH = 512
S5_GROUP_CH = 16
S5_GROUPS = 32
S5_STATE = 64
GLA_HEADS = 4
GLA_VW = 512
GLA_DV = 128
GLA_DK = 64
GLA_QK = 256
GLA_RANK = 16
GLA_TAU = 16.0
GLA_CHUNK = 64
GLA_CPB = 4
GLA_BLK = GLA_CPB * GLA_CHUNK
HEAD_DIM = 128
N_HEADS = 8
KV_HEADS = 2
Q_PER_KV = N_HEADS // KV_HEADS
AXIS_DIM = 64
ROPE_THETA = 10000.0
D_FF = 4096
EPS = 1e-6

T_PROMPT = BATCH * SEQ
T_SAMPLE = DEC_BATCH * DEC_SEQ
T_TOK = T_PROMPT + T_SAMPLE
COND_ROWS = 8
COND_SPAN = 1024
PROMPT_SPANS = T_PROMPT // COND_SPAN

S5_Q = 16
S5_W = S5_Q * S5_GROUP_CH
S5_GPB = 128 // S5_GROUP_CH
S5_ROWS = T_TOK // S5_Q
S5_PROMPT_ROWS = T_PROMPT // S5_Q
S5_PROMPT_CHUNKS = SEQ // S5_Q
S5_SAMPLE_CHUNKS = DEC_SEQ // S5_Q

VMEM_LIMIT = 56 * 1024 * 1024

NT_DIMS = (((1,), (1,)), ((), ()))
TN_DIMS = (((0,), (0,)), ((), ()))


def _cond_row(i, tile):
    return jnp.maximum((i * tile) // COND_SPAN - (PROMPT_SPANS - 1), 0)


def _norm_mod(x, gain, shift, scale):
    y = x * lax.rsqrt(jnp.mean(x * x, axis=-1, keepdims=True) + EPS)
    return (y * gain) * (1.0 + scale) + shift


def _dot(a, b):
    return jnp.dot(a, b, preferred_element_type=F32)


def _ada_kernel(cond_ref, w_ref, b_ref, o_ref):
    s = jax.nn.silu(cond_ref[...]).astype(BF16)
    o_ref[...] = _dot(s, w_ref[...].astype(BF16)) + b_ref[...]


def _ada_call(cond8, w_ada, b_ada):
    tn = 2048
    nj = 6 * D_MODEL // tn
    return pl.pallas_call(
        _ada_kernel,
        out_shape=jax.ShapeDtypeStruct((DEPTH, COND_ROWS, 6 * D_MODEL), F32),
        grid=(DEPTH, nj),
        in_specs=[
            pl.BlockSpec((COND_ROWS, D_MODEL), lambda l, j: (0, 0)),
            pl.BlockSpec((None, D_MODEL, tn), lambda l, j: (l, 0, j)),
            pl.BlockSpec((None, 1, tn), lambda l, j: (l, 0, j)),
        ],
        out_specs=pl.BlockSpec((None, COND_ROWS, tn), lambda l, j: (l, 0, j)),
        compiler_params=pltpu.CompilerParams(vmem_limit_bytes=VMEM_LIMIT),
        name="ada_mod",
    )(cond8, w_ada, b_ada.reshape(DEPTH, 1, 6 * D_MODEL))


def _token_specs(tile, width=D_MODEL):
    n_prompt = T_PROMPT // tile
    return [pl.BlockSpec((tile, width), lambda i: (jnp.minimum(i, n_prompt - 1), 0)),
            pl.BlockSpec((tile, width), lambda i: (jnp.maximum(i - n_prompt, 0), 0))]


def _token_tile(xp_ref, xs_ref, tile):
    return jnp.where(pl.program_id(0) < T_PROMPT // tile, xp_ref[...], xs_ref[...])


def _inproj_kernel(xp_ref, xs_ref, gn_ref, m_ref, w_ref, wglr_ref, wg_ref, bg_ref, u_ref, z_ref, g_ref, *, tile):
    x = _token_tile(xp_ref, xs_ref, tile)
    h = _norm_mod(x, gn_ref[...], m_ref[:, 0:D_MODEL], m_ref[:, D_MODEL:2 * D_MODEL]).astype(BF16)
    z = _dot(h, w_ref[...])
    for blk in range(S5_WIDTH // 128):
        u_ref[blk] = z[:, blk * 128:(blk + 1) * 128]
    z_ref[...] = z[:, S5_WIDTH:]
    glr = _dot(h, wglr_ref[...]).astype(BF16)
    pre = _dot(glr, wg_ref[...]) + bg_ref[...]
    g_ref[...] = jax.nn.log_sigmoid(pre) * (1.0 / GLA_TAU)


def _inproj_call(xp, xs, gn, mods, layer, w_main, w_glr, w_gate, b_gate):
    tm = 512
    nz = w_main.shape[1]
    return pl.pallas_call(
        functools.partial(_inproj_kernel, tile=tm),
        out_shape=(jax.ShapeDtypeStruct((S5_WIDTH // 128, T_TOK, 128), F32),
                   jax.ShapeDtypeStruct((T_TOK, nz - S5_WIDTH), F32),
                   jax.ShapeDtypeStruct((T_TOK, 2 * GLA_QK), F32)),
        grid=(T_TOK // tm,),
        in_specs=_token_specs(tm) + [
            pl.BlockSpec((1, D_MODEL), lambda i: (0, 0)),
            pl.BlockSpec((None, 1, 6 * D_MODEL), lambda i: (layer * COND_ROWS + _cond_row(i, tm), 0, 0)),
            pl.BlockSpec((D_MODEL, nz), lambda i: (0, 0)),
            pl.BlockSpec((D_MODEL, 128), lambda i: (0, 0)),
            pl.BlockSpec((128, 2 * GLA_QK), lambda i: (0, 0)),
            pl.BlockSpec((1, 2 * GLA_QK), lambda i: (0, 0)),
        ],
        out_specs=(pl.BlockSpec((S5_WIDTH // 128, tm, 128), lambda i: (0, i, 0)),
                   pl.BlockSpec((tm, nz - S5_WIDTH), lambda i: (i, 0)),
                   pl.BlockSpec((tm, 2 * GLA_QK), lambda i: (i, 0))),
        compiler_params=pltpu.CompilerParams(vmem_limit_bytes=VMEM_LIMIT),
        name="even_inproj",
    )(xp, xs, gn, mods, w_main, w_glr, w_gate, b_gate)


def _s5_prep_kernel(lre_ref, lim_ref, ldt_ref, btre_ref, btim_ref, cre_ref, cim_ref, ccf_ref, ccb_ref,
                    t_ref, bq_ref, cqt_ref, be_ref, a_ref, t_scr, dd_scr):
    lre = lre_ref[...]
    lim = lim_ref[...]
    dt = jnp.exp(ldt_ref[...])
    a = lre * dt
    th = lim * dt

    def lam_pow(k):
        mag = jnp.exp(k * a)
        return mag * jnp.cos(k * th), mag * jnp.sin(k * th)

    lb_re, lb_im = lam_pow(1.0)
    nr = lb_re - 1.0
    den = lre * lre + lim * lim
    cf_re = (nr * lre + lb_im * lim) / den
    cf_im = (lb_im * lre - nr * lim) / den
    bt_re = btre_ref[...]
    bt_im = btim_ref[...]
    bb_re = jnp.tile(cf_re * bt_re - cf_im * bt_im, (S5_Q, 1))
    bb_im = jnp.tile(cf_re * bt_im + cf_im * bt_re, (S5_Q, 1))

    shape = (S5_W, 128)
    pos = lax.shift_right_logical(lax.broadcasted_iota(jnp.int32, shape, 0), 4)
    is_f = lax.broadcasted_iota(jnp.int32, shape, 1) < S5_STATE
    posq = lax.broadcasted_iota(jnp.int32, (S5_Q, 128), 0).astype(F32)
    is_fq = lax.broadcasted_iota(jnp.int32, (S5_Q, 128), 1) < S5_STATE

    def per_channel(tbl):
        return jnp.broadcast_to(tbl[:, None, :], (S5_Q, S5_GROUP_CH, 128)).reshape(shape)

    p_re, p_im = map(per_channel, lam_pow(jnp.where(is_fq, (S5_Q - 1.0) - posq, posq)))
    w_re = p_re * bb_re - p_im * bb_im
    w_im = p_re * bb_im + p_im * bb_re
    bq = jnp.concatenate([w_re, w_im], axis=1)
    bqt = jnp.transpose(bq)
    bq_ref[...] = bqt.astype(BF16)

    edge = pos == jnp.where(is_f, 0, S5_Q - 1)
    be = jnp.concatenate([jnp.where(edge, bb_re, 0.0), jnp.where(edge, bb_im, 0.0)], axis=1)
    be_ref[...] = jnp.transpose(be).astype(BF16)

    q_re, q_im = map(per_channel, lam_pow(jnp.where(is_fq, posq + 1.0, S5_Q - posq)))
    ct_re = jnp.tile(cre_ref[...], (S5_Q, 1))
    ct_im = jnp.tile(cim_ref[...], (S5_Q, 1))
    g_re = q_re * ct_re - q_im * ct_im
    g_im = q_re * ct_im + q_im * ct_re
    cqt_ref[...] = jnp.concatenate([g_re, -g_im], axis=1).astype(BF16)

    a_re, a_im = lam_pow(float(S5_Q))
    a_ref[...] = jnp.concatenate([a_re, a_im], axis=1)

    kf = jnp.dot(ccf_ref[...], bqt, precision=lax.Precision.HIGHEST, preferred_element_type=F32)
    kb = jnp.dot(ccb_ref[...], bqt, precision=lax.Precision.HIGHEST, preferred_element_type=F32)
    gch = S5_GROUP_CH
    lo = S5_W - gch
    dd_scr[:, 0:S5_W] = kf
    dd_scr[:, lo:lo + S5_W] = kb
    dd_scr[:, lo:S5_W] = kf[:, lo:S5_W] + kb[:, 0:gch]
    for t in range(S5_Q):
        c0 = (S5_Q - 1 - t) * gch
        t_scr[t * gch:(t + 1) * gch, :] = dd_scr[:, c0:c0 + S5_W]
    t_ref[...] = t_scr[...].astype(BF16)


def _s5_prep_call(lam_re, lam_im, log_dt, b_re, b_im, c_re, c_im):
    def fb(p):
        return jnp.transpose(p, (1, 0, 2)).reshape(S5_GROUPS, 1, 2 * S5_STATE)

    def dup(p):
        return jnp.concatenate([p, p], axis=-1)

    ldt = fb(jnp.broadcast_to(log_dt[:, :, None], (2, S5_GROUPS, S5_STATE)))
    bt_re = dup(jnp.transpose(b_re, (0, 2, 1)))
    bt_im = dup(jnp.transpose(b_im, (0, 2, 1)))
    zero = jnp.zeros_like(c_re)
    ccf = jnp.concatenate([c_re, zero, -c_im, zero], axis=-1)
    ccb = jnp.concatenate([zero, c_re, zero, -c_im], axis=-1)

    row = pl.BlockSpec((None, 1, 128), lambda g: (g, 0, 0))
    mat16 = pl.BlockSpec((None, S5_GROUP_CH, 128), lambda g: (g, 0, 0))
    mat16w = pl.BlockSpec((None, S5_GROUP_CH, S5_W), lambda g: (g, 0, 0))
    sq = pl.BlockSpec((None, S5_W, S5_W), lambda g: (g, 0, 0))
    sq_shape = jax.ShapeDtypeStruct((S5_GROUPS, S5_W, S5_W), BF16)
    return pl.pallas_call(
        _s5_prep_kernel,
        out_shape=(sq_shape, sq_shape, sq_shape, sq_shape,
                   jax.ShapeDtypeStruct((S5_GROUPS, 1, S5_W), F32)),
        grid=(S5_GROUPS,),
        in_specs=[row, row, row, mat16, mat16, mat16, mat16, mat16w, mat16w],
        out_specs=(sq, sq, sq, sq, pl.BlockSpec((None, 1, S5_W), lambda g: (g, 0, 0))),
        scratch_shapes=[pltpu.VMEM((S5_W, S5_W), F32), pltpu.VMEM((S5_GROUP_CH, 2 * S5_W), F32)],
        name="s5_prep",
    )(fb(lam_re), fb(lam_im), ldt, bt_re, bt_im, dup(c_re), dup(c_im), ccf, ccb)


def _s5_kernel(u_ref, tt_ref, bqt_ref, cqt_ref, bet_ref, a_ref, h0_ref, y_ref, ns_ref,
               ut_scr, x_scr, spf_scr, spb_scr, ne_scr, yt_scr, xt_scr):
    gch = S5_GROUP_CH
    for s in range(S5_Q):
        rows = u_ref[pl.ds(s, S5_ROWS, stride=S5_Q), :]
        rows_t = jnp.transpose(rows).astype(BF16)
        for gl in range(S5_GPB):
            ut_scr[gl, s * gch:(s + 1) * gch, :] = rows_t[gl * gch:(gl + 1) * gch, :]

    for gl in range(S5_GPB):
        ut = ut_scr[gl]
        xt_scr[...] = _dot(bqt_ref[gl], ut)
        x = jnp.transpose(xt_scr[...])
        xt_scr[:, 0:S5_PROMPT_ROWS] = _dot(bet_ref[gl], ut[:, 0:S5_PROMPT_ROWS])
        ne = jnp.transpose(xt_scr[:, 0:S5_PROMPT_ROWS])
        for part in range(2):
            x_scr[part, pl.ds(gl, S5_ROWS, stride=S5_GPB), :] = x[:, part * 128:(part + 1) * 128]
            ne_scr[part, pl.ds(gl, S5_PROMPT_ROWS, stride=S5_GPB), :] = ne[:, part * 128:(part + 1) * 128]

    is_f = lax.broadcasted_iota(jnp.int32, (1, 128), 1) < S5_STATE
    a_re = a_ref[:, 0:128]
    a_im = a_ref[:, 128:256]

    def tile(row):
        return pl.ds(pl.multiple_of(row * S5_GPB, S5_GPB), S5_GPB)

    def scan(base, nseq, nchunk, s_init):
        def body(i, state):
            new = []
            for b in range(nseq):
                s_re, s_im = state[b]
                rows_f = tile(base + b * nchunk + i)
                rows_b = tile(base + b * nchunk + (nchunk - 1 - i))
                spf_scr[0, rows_f, :] = s_re
                spf_scr[1, rows_f, :] = s_im
                spb_scr[0, rows_b, :] = s_re
                spb_scr[1, rows_b, :] = s_im
                x_re = jnp.where(is_f, x_scr[0, rows_f, :], x_scr[0, rows_b, :])
                x_im = jnp.where(is_f, x_scr[1, rows_f, :], x_scr[1, rows_b, :])
                new.append((a_re * s_re - a_im * s_im + x_re, a_re * s_im + a_im * s_re + x_im))
            return tuple(new)

        lax.fori_loop(0, nchunk, body, tuple(s_init))

    zero = jnp.zeros((S5_GPB, 128), F32)
    scan(0, BATCH, S5_PROMPT_CHUNKS, [(zero, zero)] * BATCH)
    scan(S5_PROMPT_ROWS, DEC_BATCH, S5_SAMPLE_CHUNKS,
         [(h0_ref[b, :, 0:128], h0_ref[b, :, 128:256]) for b in range(DEC_BATCH)])

    for b in range(BATCH):
        first = pl.ds(b * S5_PROMPT_CHUNKS * S5_GPB, S5_GPB)
        last = pl.ds(((b + 1) * S5_PROMPT_CHUNKS - 1) * S5_GPB, S5_GPB)
        for part in range(2):
            ns_ref[b, :, part * 128:(part + 1) * 128] = jnp.where(is_f, ne_scr[part, first, :], ne_scr[part, last, :])

    for gl in range(S5_GPB):
        rows = pl.ds(gl, S5_ROWS, stride=S5_GPB)
        carried = jnp.concatenate([jnp.where(is_f, spf_scr[p, rows, :], spb_scr[p, rows, :]) for p in range(2)],
                                  axis=1).astype(BF16)
        yt = _dot(tt_ref[gl], ut_scr[gl]) + lax.dot_general(cqt_ref[gl], carried, NT_DIMS,
                                                            preferred_element_type=F32)
        for t in range(S5_Q):
            yt_scr[t, gl * gch:(gl + 1) * gch, :] = yt[t * gch:(t + 1) * gch, :]
    for t in range(S5_Q):
        y_ref[pl.ds(t, S5_ROWS, stride=S5_Q), :] = jnp.transpose(yt_scr[t])


def _s5_call(u, mats, h0):
    tt_m, bqt_m, cqt_m, bet_m, a_m = mats
    nsteps = S5_GROUPS // S5_GPB
    sq = pl.BlockSpec((S5_GPB, S5_W, S5_W), lambda g: (g, 0, 0))
    state_scr = pltpu.VMEM((2, S5_ROWS * S5_GPB, 128), F32)
    return pl.pallas_call(
        _s5_kernel,
        out_shape=(jax.ShapeDtypeStruct((nsteps, T_TOK, 128), F32),
                   jax.ShapeDtypeStruct((nsteps, BATCH, S5_GPB, S5_W), F32)),
        grid=(nsteps,),
        in_specs=[
            pl.BlockSpec((None, T_TOK, 128), lambda g: (g, 0, 0)),
            sq, sq, sq, sq,
            pl.BlockSpec((S5_GPB, S5_W), lambda g: (g, 0)),
            pl.BlockSpec((None, DEC_BATCH, S5_GPB, S5_W), lambda g: (g, 0, 0, 0)),
        ],
        out_specs=(pl.BlockSpec((None, T_TOK, 128), lambda g: (g, 0, 0)),
                   pl.BlockSpec((None, BATCH, S5_GPB, S5_W), lambda g: (g, 0, 0, 0))),
        scratch_shapes=[pltpu.VMEM((S5_GPB, S5_W, S5_ROWS), BF16), state_scr, state_scr, state_scr,
                        pltpu.VMEM((2, S5_PROMPT_ROWS * S5_GPB, 128), F32),
                        pltpu.VMEM((S5_Q, 128, S5_ROWS), F32), pltpu.VMEM((S5_W, S5_ROWS), F32)],
        compiler_params=pltpu.CompilerParams(vmem_limit_bytes=VMEM_LIMIT),
        name="s5_scan",
    )(u, tt_m, bqt_m, cqt_m, bet_m, a_m.reshape(S5_GROUPS, S5_W), h0)


def _split_bf16(x):
    hi = x.astype(BF16)
    r1 = x - hi.astype(F32)
    mid = r1.astype(BF16)
    lo = (r1 - mid.astype(F32)).astype(BF16)
    return hi, mid, lo


def _gla_kernel(*refs, seq_len, has_s0):
    q_ref, k_ref, v_ref, gf_ref, gb_ref, r_ref, gn_ref = refs[:7]
    s0_ref = refs[7] if has_s0 else None
    o_ref, sfin_ref, oi_scr, qd_scr, kv_scr, dec_scr, ss_scr = refs[7 + has_s0:]
    nblk = seq_len // GLA_BLK
    nchunk = seq_len // GLA_CHUNK
    cl = GLA_CHUNK
    ti = lax.broadcasted_iota(jnp.int32, (GLA_BLK, GLA_BLK), 0)
    si = lax.broadcasted_iota(jnp.int32, (GLA_BLK, GLA_BLK), 1)
    same = lax.shift_right_logical(ti, 6) == lax.shift_right_logical(si, 6)
    keep = (same & (ti >= si), same & (ti <= si))
    tri = tuple(kp.astype(BF16) for kp in keep)
    lane_head = lax.shift_right_logical(lax.broadcasted_iota(jnp.int32, (cl, GLA_QK), 1), 6)
    zeros_v = jnp.zeros((cl, GLA_DV), BF16)
    heads = [(slice(h * GLA_DK, (h + 1) * GLA_DK), slice(h * GLA_DV, (h + 1) * GLA_DV)) for h in range(GLA_HEADS)]

    for j in range(nblk):
        rows = slice(j * GLA_BLK, (j + 1) * GLA_BLK)
        q = q_ref[rows, :] * (GLA_DK ** -0.5)
        k = k_ref[rows, :]
        v = v_ref[rows, :].astype(BF16)
        qd, kd, k2t = [], [], []
        for d, g_ref in enumerate((gf_ref, gb_ref)):
            b = sum(_dot(tri[d], part) for part in _split_bf16(g_ref[rows, :]))
            last = cl - 1 if d == 0 else 0
            b_last = [b[c * cl + last:c * cl + last + 1] for c in range(GLA_CPB)]
            bl = jnp.concatenate([jnp.broadcast_to(x, (cl, GLA_QK)) for x in b_last], axis=0)
            qd_d = (q * jnp.exp(b)).astype(BF16)
            qd_scr[d, rows, :] = qd_d
            qd.append(qd_d)
            kd.append((k * jnp.exp(-b)).astype(BF16))
            k2t.append(jnp.transpose(k * jnp.exp(bl - b)).astype(BF16))
            for c in range(GLA_CPB):
                dec_scr[d, j * GLA_CPB + c] = jnp.exp(jnp.transpose(jnp.broadcast_to(b_last[c], (GLA_DV, GLA_QK))))
        for h, (ks, vs) in enumerate(heads):
            att = [jnp.where(keep[d], lax.dot_general(qd[d][:, ks], kd[d][:, ks], NT_DIMS,
                                                      preferred_element_type=F32), 0.0) for d in range(2)]
            oi_scr[rows, vs] = _dot((att[0] + att[1]).astype(BF16), v[:, vs])
            vh = v[:, vs]
            vexp = jnp.concatenate(
                [jnp.concatenate([vh[c * cl:(c + 1) * cl] if c2 == c else zeros_v for c2 in range(GLA_CPB)], axis=1)
                 for c in range(GLA_CPB)], axis=0)
            for d in range(2):
                kv_scr[d, j, h] = _dot(k2t[d][ks, :], vexp)

    for d in range(2):
        s = s0_ref[d] if has_s0 else jnp.zeros((GLA_QK, GLA_DV), F32)
        for cg in (range(nchunk) if d == 0 else range(nchunk - 1, -1, -1)):
            j, c = divmod(cg, GLA_CPB)
            ss_scr[d, cg] = s.astype(BF16)
            kv = jnp.concatenate([kv_scr[d, j, h, :, c * GLA_DV:(c + 1) * GLA_DV] for h in range(GLA_HEADS)], axis=0)
            s = s * dec_scr[d, cg] + kv
        sfin_ref[d] = s

    for cg in range(nchunk):
        rows = slice(cg * cl, (cg + 1) * cl)
        inter = []
        for d in range(2):
            qc = qd_scr[d, rows, :]
            qstack = jnp.concatenate([jnp.where(lane_head == h, qc, jnp.zeros_like(qc)) for h in range(GLA_HEADS)],
                                     axis=0)
            inter.append(_dot(qstack, ss_scr[d, cg]))
        gate = jax.nn.silu(r_ref[rows, :])
        for h, (ks, vs) in enumerate(heads):
            hr = slice(h * cl, (h + 1) * cl)
            oh = oi_scr[rows, vs] + inter[0][hr] + inter[1][hr]
            oh = oh * lax.rsqrt(jnp.mean(oh * oh, axis=-1, keepdims=True) + EPS) * gn_ref[...]
            o_ref[rows, vs] = oh * gate[:, vs]


def _gla_call(z, g, gla_norm, s0, seq_len, nseq, row0):
    assert row0 % seq_len == 0
    r0 = row0 // seq_len
    has_s0 = s0 is not None
    qk_off = 0
    v_off = 2 * GLA_QK // GLA_VW
    in_specs = [
        pl.BlockSpec((seq_len, GLA_QK), lambda i: (r0 + i, qk_off)),
        pl.BlockSpec((seq_len, GLA_QK), lambda i: (r0 + i, qk_off + 1)),
        pl.BlockSpec((seq_len, GLA_VW), lambda i: (r0 + i, v_off)),
        pl.BlockSpec((seq_len, GLA_QK), lambda i: (r0 + i, 0)),
        pl.BlockSpec((seq_len, GLA_QK), lambda i: (r0 + i, 1)),
        pl.BlockSpec((seq_len, GLA_VW), lambda i: (r0 + i, v_off + 1)),
        pl.BlockSpec((1, GLA_DV), lambda i: (0, 0)),
    ]
    args = [z, z, z, g, g, z, gla_norm]
    if has_s0:
        in_specs.append(pl.BlockSpec((None, 2, GLA_QK, GLA_DV), lambda i: (i, 0, 0, 0)))
        args.append(s0)
    return pl.pallas_call(
        functools.partial(_gla_kernel, seq_len=seq_len, has_s0=has_s0),
        out_shape=(jax.ShapeDtypeStruct((nseq * seq_len, GLA_VW), F32),
                   jax.ShapeDtypeStruct((nseq, 2, GLA_QK, GLA_DV), F32)),
        grid=(nseq,),
        in_specs=in_specs,
        out_specs=(pl.BlockSpec((seq_len, GLA_VW), lambda i: (i, 0)),
                   pl.BlockSpec((None, 2, GLA_QK, GLA_DV), lambda i: (i, 0, 0, 0))),
        scratch_shapes=[
            pltpu.VMEM((seq_len, GLA_VW), F32),
            pltpu.VMEM((2, seq_len, GLA_QK), BF16),
            pltpu.VMEM((2, seq_len // GLA_BLK, GLA_HEADS, GLA_DK, GLA_CPB * GLA_DV), F32),
            pltpu.VMEM((2, seq_len // GLA_CHUNK, GLA_QK, GLA_DV), F32),
            pltpu.VMEM((2, seq_len // GLA_CHUNK, GLA_QK, GLA_DV), BF16),
        ],
        compiler_params=pltpu.CompilerParams(vmem_limit_bytes=VMEM_LIMIT),
        name=f"gla_len{seq_len}",
    )(*args)


def _mlp_tail(x, mix, m_ref, gn2_ref, w1_ref, w2_ref):
    y1 = x + m_ref[:, 2 * D_MODEL:3 * D_MODEL] * mix
    h = _norm_mod(y1, gn2_ref[...], m_ref[:, 3 * D_MODEL:4 * D_MODEL], m_ref[:, 4 * D_MODEL:5 * D_MODEL]).astype(BF16)
    tf = 512
    acc = jnp.zeros(y1.shape, F32)
    for c in range(D_FF // tf):
        a = _dot(h, w1_ref[:, c * tf:(c + 1) * tf])
        a = jnp.square(jnp.maximum(a, 0.0)).astype(BF16)
        acc = acc + _dot(a, w2_ref[c * tf:(c + 1) * tf, :])
    return y1 + m_ref[:, 5 * D_MODEL:6 * D_MODEL] * acc


def _even_out_kernel(xp_ref, xs_ref, y5_ref, u_ref, dskip_ref, wglu_ref, bglu_ref, glap_ref, glas_ref, wout_ref,
                     m_ref, gn2_ref, w1_ref, w2_ref, o_ref):
    nblk = S5_WIDTH // 128
    ys = (jnp.concatenate([y5_ref[b] for b in range(nblk)], axis=1)
          + jnp.concatenate([u_ref[b] for b in range(nblk)], axis=1) * dskip_ref[...])
    gl = jax.nn.gelu(ys)
    s5o = gl * jax.nn.sigmoid(_dot(gl.astype(BF16), wglu_ref[...]) + bglu_ref[...])
    gla = _token_tile(glap_ref, glas_ref, _OUT_TM).astype(BF16)
    mix = _dot(s5o.astype(BF16), wout_ref[0:S5_WIDTH, :]) + _dot(gla, wout_ref[S5_WIDTH:, :])
    o_ref[...] = _mlp_tail(_token_tile(xp_ref, xs_ref, _OUT_TM), mix, m_ref, gn2_ref, w1_ref, w2_ref)


def _odd_out_kernel(x_ref, attp_ref, atts_ref, wo_ref, m_ref, gn2_ref, w1_ref, w2_ref, op_ref, os_ref):
    mix = _dot(_token_tile(attp_ref, atts_ref, _OUT_TM), wo_ref[...])
    y = _mlp_tail(x_ref[...], mix, m_ref, gn2_ref, w1_ref, w2_ref)
    is_prompt = pl.program_id(0) < T_PROMPT // _OUT_TM

    @pl.when(is_prompt)
    def _():
        op_ref[...] = y

    @pl.when(jnp.logical_not(is_prompt))
    def _():
        os_ref[...] = y


_OUT_TM = 512


def _const_spec(shape):
    return pl.BlockSpec(shape, lambda i: (0,) * len(shape), pipeline_mode=pl.Buffered(1))


def _tail_specs(layer):
    tm = _OUT_TM
    return [
        pl.BlockSpec((None, 1, 6 * D_MODEL), lambda i: (layer * COND_ROWS + _cond_row(i, tm), 0, 0)),
        _const_spec((1, D_MODEL)),
        pl.BlockSpec((None, D_MODEL, D_FF), lambda i: (layer, 0, 0), pipeline_mode=pl.Buffered(1)),
        pl.BlockSpec((None, D_FF, D_MODEL), lambda i: (layer, 0, 0), pipeline_mode=pl.Buffered(1)),
    ]


def _even_out_call(xp, xs, y5, u, d_skip, w_glu, b_glu, gla_p, gla_s, w_out, mods, layer, gn2, w1, w2):
    tm = _OUT_TM
    return pl.pallas_call(
        _even_out_kernel,
        out_shape=jax.ShapeDtypeStruct((T_TOK, D_MODEL), F32),
        grid=(T_TOK // tm,),
        in_specs=_token_specs(tm) + [
            pl.BlockSpec((S5_WIDTH // 128, tm, 128), lambda i: (0, i, 0)),
            pl.BlockSpec((S5_WIDTH // 128, tm, 128), lambda i: (0, i, 0)),
            _const_spec((1, S5_WIDTH)),
            _const_spec((S5_WIDTH, S5_WIDTH)),
            _const_spec((1, S5_WIDTH)),
        ] + _token_specs(tm, GLA_VW) + [
            _const_spec((S5_WIDTH + GLA_VW, D_MODEL)),
        ] + _tail_specs(layer),
        out_specs=pl.BlockSpec((tm, D_MODEL), lambda i: (i, 0)),
        compiler_params=pltpu.CompilerParams(vmem_limit_bytes=VMEM_LIMIT),
        name="even_out_mlp",
    )(xp, xs, y5, u, d_skip, w_glu, b_glu, gla_p, gla_s, w_out, mods, gn2, w1, w2)


def _odd_out_call(x, att_p, att_s, w_o, mods, layer, gn2, w1, w2):
    tm = _OUT_TM
    return pl.pallas_call(
        _odd_out_kernel,
        out_shape=(jax.ShapeDtypeStruct((T_PROMPT, D_MODEL), F32),
                   jax.ShapeDtypeStruct((T_SAMPLE, D_MODEL), F32)),
        grid=(T_TOK // tm,),
        in_specs=[pl.BlockSpec((tm, D_MODEL), lambda i: (i, 0))] + _token_specs(tm) + [
            _const_spec((D_MODEL, D_MODEL)),
        ] + _tail_specs(layer),
        out_specs=tuple(_token_specs(tm)),
        compiler_params=pltpu.CompilerParams(vmem_limit_bytes=VMEM_LIMIT),
        name="odd_out_mlp",
    )(x, att_p, att_s, w_o, mods, gn2, w1, w2)


def _qkv_kernel(x_ref, gn_ref, m_ref, w_ref, qn_ref, kn_ref, cos_ref, sin_ref,
                q_ref, kb_ref, vb_ref, k32_ref, v32_ref, *, tile):
    h = _norm_mod(x_ref[...], gn_ref[...], m_ref[:, 0:D_MODEL], m_ref[:, D_MODEL:2 * D_MODEL]).astype(BF16)
    z = _dot(h, w_ref[...])
    v = z[:, (N_HEADS + KV_HEADS) * HEAD_DIM:]
    vb_ref[...] = v.astype(BF16)
    even_lane = (lax.broadcasted_iota(jnp.int32, (1, HEAD_DIM), 1) & 1) == 0

    def heads(rope):
        for hd in range(N_HEADS + KV_HEADS):
            xh = z[:, hd * HEAD_DIM:(hd + 1) * HEAD_DIM]
            gain = qn_ref[...] if hd < N_HEADS else kn_ref[...]
            xh = xh * lax.rsqrt(jnp.mean(xh * xh, axis=-1, keepdims=True) + EPS) * gain
            if rope:
                partner = jnp.where(even_lane, pltpu.roll(xh, HEAD_DIM - 1, 1), pltpu.roll(xh, 1, 1))
                xh = xh * cos_ref[...] + partner * sin_ref[...]
            if hd < N_HEADS:
                q_ref[:, hd * HEAD_DIM:(hd + 1) * HEAD_DIM] = xh.astype(BF16)
            else:
                cols = slice((hd - N_HEADS) * HEAD_DIM, (hd - N_HEADS + 1) * HEAD_DIM)
                kb_ref[:, cols] = xh.astype(BF16)
                if not rope:
                    k32_ref[:, cols] = xh

    is_sample = pl.program_id(0) >= T_PROMPT // tile

    @pl.when(is_sample)
    def _():
        heads(True)

    @pl.when(jnp.logical_not(is_sample))
    def _():
        heads(False)
        v32_ref[...] = v


def _qkv_call(x, gn, mods, layer, w_qkv, q_norm, k_norm, cos_t, sin_t):
    tm = 512
    pos_tiles = DEC_SEQ // tm
    n_prompt = T_PROMPT // tm
    kvw = KV_HEADS * HEAD_DIM

    def pos_map(i):
        return (jnp.maximum(i - n_prompt, 0) % pos_tiles, 0)

    def prompt_map(i):
        return (jnp.minimum(i, n_prompt - 1), 0)

    return pl.pallas_call(
        functools.partial(_qkv_kernel, tile=tm),
        out_shape=(jax.ShapeDtypeStruct((T_TOK, N_HEADS * HEAD_DIM), BF16),
                   jax.ShapeDtypeStruct((T_TOK, kvw), BF16),
                   jax.ShapeDtypeStruct((T_TOK, kvw), BF16),
                   jax.ShapeDtypeStruct((T_PROMPT, kvw), F32),
                   jax.ShapeDtypeStruct((T_PROMPT, kvw), F32)),
        grid=(T_TOK // tm,),
        in_specs=[
            pl.BlockSpec((tm, D_MODEL), lambda i: (i, 0)),
            pl.BlockSpec((1, D_MODEL), lambda i: (0, 0)),
            pl.BlockSpec((None, 1, 6 * D_MODEL), lambda i: (layer * COND_ROWS + _cond_row(i, tm), 0, 0)),
            pl.BlockSpec(w_qkv.shape, lambda i: (0, 0)),
            pl.BlockSpec((1, HEAD_DIM), lambda i: (0, 0)),
            pl.BlockSpec((1, HEAD_DIM), lambda i: (0, 0)),
            pl.BlockSpec((tm, HEAD_DIM), pos_map),
            pl.BlockSpec((tm, HEAD_DIM), pos_map),
        ],
        out_specs=(pl.BlockSpec((tm, N_HEADS * HEAD_DIM), lambda i: (i, 0)),
                   pl.BlockSpec((tm, kvw), lambda i: (i, 0)),
                   pl.BlockSpec((tm, kvw), lambda i: (i, 0)),
                   pl.BlockSpec((tm, kvw), prompt_map),
                   pl.BlockSpec((tm, kvw), prompt_map)),
        compiler_params=pltpu.CompilerParams(vmem_limit_bytes=VMEM_LIMIT),
        name="odd_qkv",
    )(x, gn, mods, w_qkv, q_norm, k_norm, cos_t, sin_t)


def _rope_tables():
    rows = DEC_SEQ // GRID_W
    row = jnp.repeat(jnp.arange(rows, dtype=F32), GRID_W)
    col = jnp.tile(jnp.arange(GRID_W, dtype=F32), rows)
    inv = ROPE_THETA ** (-jnp.arange(0, AXIS_DIM, 2, dtype=F32) / AXIS_DIM)
    ang = jnp.concatenate([row[:, None] * inv, col[:, None] * inv], axis=-1)
    cos_t = jnp.repeat(jnp.cos(ang), 2, axis=-1)
    sin = jnp.sin(ang)
    sin_t = jnp.stack([-sin, sin], axis=-1).reshape(DEC_SEQ, HEAD_DIM)
    return cos_t, sin_t


def _attn_kernel(*refs, has_cache):
    q_ref, k_ref, v_ref = refs[:3]
    ck_ref, cv_ref = refs[3:5] if has_cache else (None, None)
    o_ref = refs[-1]
    c = HEAD_DIM ** -0.5 * math.log2(math.e)
    ones_col = (lax.broadcasted_iota(jnp.int32, (1, HEAD_DIM), 1) == 0).astype(BF16)

    def with_ones(v):
        return jnp.concatenate([v, jnp.broadcast_to(ones_col, v.shape)], axis=1)

    k = k_ref[...]
    v = with_ones(v_ref[...])
    if has_cache:
        ck = ck_ref[...].astype(BF16)
        cv = with_ones(cv_ref[...].astype(BF16))
    for r in range(Q_PER_KV):
        cs = slice(r * HEAD_DIM, (r + 1) * HEAD_DIM)
        q = q_ref[:, cs]
        s = lax.dot_general(q, k, NT_DIMS, preferred_element_type=F32)
        m = jnp.max(s, axis=-1, keepdims=True)
        if has_cache:
            sc = lax.dot_general(q, ck, NT_DIMS, preferred_element_type=F32)
            m = jnp.maximum(m, jnp.max(sc, axis=-1, keepdims=True))
        mc = m * c
        o = _dot(jnp.exp2(s * c - mc).astype(BF16), v)
        if has_cache:
            o = o + _dot(jnp.exp2(sc * c - mc).astype(BF16), cv)
        o_ref[:, cs] = (o[:, 0:HEAD_DIM] / o[:, HEAD_DIM:HEAD_DIM + 1]).astype(BF16)


def _attn_call(q, k, v, cache_k, cache_v, seq_len, row0, nrows):
    assert row0 % seq_len == 0 and nrows % seq_len == 0
    has_cache = cache_k is not None
    b0 = row0 // seq_len
    gw = Q_PER_KV * HEAD_DIM
    in_specs = [
        pl.BlockSpec((seq_len, gw), lambda b, g: (b0 + b, g)),
        pl.BlockSpec((seq_len, HEAD_DIM), lambda b, g: (b0 + b, g)),
        pl.BlockSpec((seq_len, HEAD_DIM), lambda b, g: (b0 + b, g)),
    ]
    args = [q, k, v]
    if has_cache:
        in_specs += [pl.BlockSpec((PAST_LEN, HEAD_DIM), lambda b, g: (b, g)),
                     pl.BlockSpec((PAST_LEN, HEAD_DIM), lambda b, g: (b, g))]
        args += [cache_k, cache_v]
    return pl.pallas_call(
        functools.partial(_attn_kernel, has_cache=has_cache),
        out_shape=jax.ShapeDtypeStruct((nrows, N_HEADS * HEAD_DIM), BF16),
        grid=(nrows // seq_len, KV_HEADS),
        in_specs=in_specs,
        out_specs=pl.BlockSpec((seq_len, gw), lambda b, g: (b, g)),
        compiler_params=pltpu.CompilerParams(vmem_limit_bytes=VMEM_LIMIT),
        name=f"attn_len{seq_len}",
    )(*args)


def kernel(x_prompt, x_sample, state_s5_re, state_s5_im, state_gla, cache_k, cache_v, c, c_ctx, norm_mix, norm_mlp, w_ada, b_ada, w_mlp_in, w_mlp_out, w_in_e, w_out_e, s5_lambda_re, s5_lambda_im, s5_log_dt, s5_b_re, s5_b_im, s5_c_re, s5_c_im, s5_d, s5_w_glu, s5_b_glu, gla_w_gate2, gla_b_gate, gla_norm, w_qkv_o, w_o_o, q_norm, k_norm):
    xp = x_prompt.reshape(T_PROMPT, D_MODEL)
    xs = x_sample.reshape(T_SAMPLE, D_MODEL)
    cond8 = jnp.concatenate([c_ctx[None, :], c, jnp.zeros((COND_ROWS - 1 - DEC_BATCH, D_MODEL), F32)], axis=0)
    mods = _ada_call(cond8, w_ada, b_ada).reshape(DEPTH * COND_ROWS, 1, 6 * D_MODEL)
    w1_all = w_mlp_in.astype(BF16)
    w2_all = w_mlp_out.astype(BF16)

    n_main = S5_WIDTH + 2 * GLA_QK + 2 * GLA_VW
    w_in = w_in_e[0]
    w_main = w_in[:, :n_main].astype(BF16)
    w_glr = jnp.pad(w_in[:, n_main:], ((0, 0), (0, 128 - 2 * GLA_RANK))).astype(BF16)
    w_gate = jnp.zeros((128, 2 * GLA_QK), F32)
    w_gate = w_gate.at[0:GLA_RANK, 0:GLA_QK].set(gla_w_gate2[0, 0])
    w_gate = w_gate.at[GLA_RANK:2 * GLA_RANK, GLA_QK:].set(gla_w_gate2[0, 1]).astype(BF16)
    b_gate = gla_b_gate[0].reshape(1, 2 * GLA_QK)
    u, z, g = _inproj_call(xp, xs, norm_mix[0:1], mods, 0, w_main, w_glr, w_gate, b_gate)

    mats = _s5_prep_call(s5_lambda_re[0], s5_lambda_im[0], s5_log_dt[0], s5_b_re[0], s5_b_im[0],
                         s5_c_re[0], s5_c_im[0])

    def state_rows(s):
        return jnp.transpose(s, (2, 0, 1, 3)).reshape(S5_GROUPS, DEC_BATCH, 2 * S5_STATE)

    h0 = jnp.concatenate([state_rows(state_s5_re[:, 0]), state_rows(state_s5_im[:, 0])], axis=-1)
    nsteps = S5_GROUPS // S5_GPB
    h0 = jnp.transpose(h0.reshape(nsteps, S5_GPB, DEC_BATCH, S5_W), (0, 2, 1, 3))
    y5, ns = _s5_call(u, mats, h0)
    ns = jnp.transpose(ns, (0, 2, 1, 3)).reshape(S5_GROUPS, BATCH, S5_W)

    def state_out(n):
        return jnp.transpose(n.reshape(S5_GROUPS, BATCH, 2, S5_STATE), (1, 2, 0, 3))[:, None]

    new_s5_re = state_out(ns[:, :, :2 * S5_STATE])
    new_s5_im = state_out(ns[:, :, 2 * S5_STATE:])

    gn_gla = gla_norm[0].reshape(1, GLA_DV)
    gla_p, sfin = _gla_call(z, g, gn_gla, None, SEQ, BATCH, 0)
    s0 = state_gla[:, 0].reshape(DEC_BATCH, 2, GLA_QK, GLA_DV)
    gla_s, _ = _gla_call(z, g, gn_gla, s0, DEC_SEQ, DEC_BATCH, T_PROMPT)
    new_gla = sfin.reshape(BATCH, 1, 2, GLA_HEADS, GLA_DK, GLA_DV)

    x = _even_out_call(xp, xs, y5, u, s5_d[0].reshape(1, S5_WIDTH), s5_w_glu[0].astype(BF16),
                       s5_b_glu[0].reshape(1, S5_WIDTH), gla_p, gla_s, w_out_e[0].astype(BF16), mods, 0,
                       norm_mlp[0:1], w1_all, w2_all)

    cos_t, sin_t = _rope_tables()
    q, k, v, k32, v32 = _qkv_call(x, norm_mix[1:2], mods, 1, w_qkv_o[0].astype(BF16),
                                  q_norm[0].reshape(1, HEAD_DIM), k_norm[0].reshape(1, HEAD_DIM), cos_t, sin_t)
    att_p = _attn_call(q, k, v, None, None, SEQ, 0, T_PROMPT)
    ck = cache_k[:, 0].reshape(DEC_BATCH * PAST_LEN, KV_HEADS * HEAD_DIM)
    cv = cache_v[:, 0].reshape(DEC_BATCH * PAST_LEN, KV_HEADS * HEAD_DIM)
    att_s = _attn_call(q, k, v, ck, cv, DEC_SEQ, T_PROMPT, T_SAMPLE)
    yp, ys = _odd_out_call(x, att_p, att_s, w_o_o[0].astype(BF16), mods, 1, norm_mlp[1:2],
                           w1_all, w2_all)

    new_k = k32.reshape(BATCH, 1, SEQ, KV_HEADS, HEAD_DIM)
    new_v = v32.reshape(BATCH, 1, SEQ, KV_HEADS, HEAD_DIM)
    y_prompt = yp.reshape(BATCH, SEQ, D_MODEL)
    y_sample = ys.reshape(DEC_BATCH, DEC_SEQ, D_MODEL)
    return (y_prompt, y_sample, new_s5_re, new_s5_im, new_gla, new_k, new_v)
```

```python
import functools
import math

import jax
import jax.numpy as jnp
import numpy as np
from jax import lax
from jax.experimental import pallas as pl
from jax.experimental.pallas import tpu as pltpu

F32 = jnp.float32
BF16 = jnp.bfloat16

D_MODEL = 1024
BATCH = 16
SEQ = 256
DEPTH = 2
DEC_BATCH = 4
DEC_SEQ = 1024
PAST_LEN = 512
GRID_W = 64
S5_WIDTH = 512
S5_GROUP_CH = 16
S5_GROUPS = 32
S5_STATE = 64
GLA_HEADS = 4
GLA_VW = 512
GLA_DV = 128
GLA_DK = 64
GLA_QK = 256
GLA_RANK = 16
GLA_TAU = 16.0
GLA_CHUNK = 64
GLA_CPB = 4
GLA_BLK = GLA_CPB * GLA_CHUNK
HEAD_DIM = 128
N_HEADS = 8
KV_HEADS = 2
Q_PER_KV = N_HEADS // KV_HEADS
AXIS_DIM = 64
ROPE_THETA = 10000.0
D_FF = 4096
EPS = 1e-6

T_PROMPT = BATCH * SEQ
T_SAMPLE = DEC_BATCH * DEC_SEQ
T_TOK = T_PROMPT + T_SAMPLE
COND_ROWS = 8
COND_SPAN = 1024
PROMPT_SPANS = T_PROMPT // COND_SPAN

S5_Q = 16
S5_W = S5_Q * S5_GROUP_CH
S5_GPB = 128 // S5_GROUP_CH
S5_ROWS = T_TOK // S5_Q
S5_PROMPT_ROWS = T_PROMPT // S5_Q
S5_PROMPT_CHUNKS = SEQ // S5_Q
S5_SAMPLE_CHUNKS = DEC_SEQ // S5_Q

VMEM_LIMIT = 56 * 1024 * 1024

NT_DIMS = (((1,), (1,)), ((), ()))
TN_DIMS = (((0,), (0,)), ((), ()))


def _cond_row(i, tile):
    return jnp.maximum((i * tile) // COND_SPAN - (PROMPT_SPANS - 1), 0)


def _norm_mod(x, gain, shift, scale):
    y = x * lax.rsqrt(jnp.mean(x * x, axis=-1, keepdims=True) + EPS)
    return (y * gain) * (1.0 + scale) + shift


def _dot(a, b):
    return jnp.dot(a, b, preferred_element_type=F32)


def _ada_kernel(cond_ref, w_ref, b_ref, o_ref):
    s = jax.nn.silu(cond_ref[...]).astype(BF16)
    o_ref[...] = _dot(s, w_ref[...].astype(BF16)) + b_ref[...]


def _ada_call(cond8, w_ada, b_ada):
    tn = 2048
    nj = 6 * D_MODEL // tn
    return pl.pallas_call(
        _ada_kernel,
        out_shape=jax.ShapeDtypeStruct((DEPTH, COND_ROWS, 6 * D_MODEL), F32),
        grid=(DEPTH, nj),
        in_specs=[
            pl.BlockSpec((COND_ROWS, D_MODEL), lambda l, j: (0, 0)),
            pl.BlockSpec((None, D_MODEL, tn), lambda l, j: (l, 0, j)),
            pl.BlockSpec((None, 1, tn), lambda l, j: (l, 0, j)),
        ],
        out_specs=pl.BlockSpec((None, COND_ROWS, tn), lambda l, j: (l, 0, j)),
        compiler_params=pltpu.CompilerParams(vmem_limit_bytes=VMEM_LIMIT),
        name="ada_mod",
    )(cond8, w_ada, b_ada.reshape(DEPTH, 1, 6 * D_MODEL))


def _token_specs(tile, width=D_MODEL):
    n_prompt = T_PROMPT // tile
    return [pl.BlockSpec((tile, width), lambda i: (jnp.minimum(i, n_prompt - 1), 0)),
            pl.BlockSpec((tile, width), lambda i: (jnp.maximum(i - n_prompt, 0), 0))]


def _token_tile(xp_ref, xs_ref, tile):
    return jnp.where(pl.program_id(0) < T_PROMPT // tile, xp_ref[...], xs_ref[...])


def _inproj_kernel(xp_ref, xs_ref, gn_ref, m_ref, w_ref, wglr_ref, wg_ref, bg_ref, u_ref, z_ref, g_ref, *, tile):
    x = _token_tile(xp_ref, xs_ref, tile)
    h = _norm_mod(x, gn_ref[...], m_ref[:, 0:D_MODEL], m_ref[:, D_MODEL:2 * D_MODEL]).astype(BF16)
    z = _dot(h, w_ref[...])
    for blk in range(S5_WIDTH // 128):
        u_ref[blk] = z[:, blk * 128:(blk + 1) * 128]
    z_ref[...] = z[:, S5_WIDTH:]
    glr = _dot(h, wglr_ref[...]).astype(BF16)
    pre = _dot(glr, wg_ref[...]) + bg_ref[...]
    g_ref[...] = jax.nn.log_sigmoid(pre) * (1.0 / GLA_TAU)


def _inproj_call(xp, xs, gn, mods, layer, w_main, w_glr, w_gate, b_gate):
    tm = 512
    nz = w_main.shape[1]
    return pl.pallas_call(
        functools.partial(_inproj_kernel, tile=tm),
        out_shape=(jax.ShapeDtypeStruct((S5_WIDTH // 128, T_TOK, 128), F32),
                   jax.ShapeDtypeStruct((T_TOK, nz - S5_WIDTH), F32),
                   jax.ShapeDtypeStruct((T_TOK, 2 * GLA_QK), F32)),
        grid=(T_TOK // tm,),
        in_specs=_token_specs(tm) + [
            pl.BlockSpec((1, D_MODEL), lambda i: (0, 0)),
            pl.BlockSpec((None, 1, 6 * D_MODEL), lambda i: (layer * COND_ROWS + _cond_row(i, tm), 0, 0)),
            pl.BlockSpec((D_MODEL, nz), lambda i: (0, 0)),
            pl.BlockSpec((D_MODEL, 128), lambda i: (0, 0)),
            pl.BlockSpec((128, 2 * GLA_QK), lambda i: (0, 0)),
            pl.BlockSpec((1, 2 * GLA_QK), lambda i: (0, 0)),
        ],
        out_specs=(pl.BlockSpec((S5_WIDTH // 128, tm, 128), lambda i: (0, i, 0)),
                   pl.BlockSpec((tm, nz - S5_WIDTH), lambda i: (i, 0)),
                   pl.BlockSpec((tm, 2 * GLA_QK), lambda i: (i, 0))),
        compiler_params=pltpu.CompilerParams(vmem_limit_bytes=VMEM_LIMIT),
        name="even_inproj",
    )(xp, xs, gn, mods, w_main, w_glr, w_gate, b_gate)


def _s5_prep_kernel(lre_ref, lim_ref, ldt_ref, btre_ref, btim_ref, cre_ref, cim_ref, ccf_ref, ccb_ref,
                    t_ref, bq_ref, cqt_ref, be_ref, a_ref, t_scr, dd_scr):
    lre = lre_ref[...]
    lim = lim_ref[...]
    dt = jnp.exp(ldt_ref[...])
    a = lre * dt
    th = lim * dt

    def lam_pow(k):
        mag = jnp.exp(k * a)
        return mag * jnp.cos(k * th), mag * jnp.sin(k * th)

    lb_re, lb_im = lam_pow(1.0)
    nr = lb_re - 1.0
    den = lre * lre + lim * lim
    cf_re = (nr * lre + lb_im * lim) / den
    cf_im = (lb_im * lre - nr * lim) / den
    bt_re = btre_ref[...]
    bt_im = btim_ref[...]
    bb_re = jnp.tile(cf_re * bt_re - cf_im * bt_im, (S5_Q, 1))
    bb_im = jnp.tile(cf_re * bt_im + cf_im * bt_re, (S5_Q, 1))

    shape = (S5_W, 128)
    pos = lax.shift_right_logical(lax.broadcasted_iota(jnp.int32, shape, 0), 4)
    is_f = lax.broadcasted_iota(jnp.int32, shape, 1) < S5_STATE
    posq = lax.broadcasted_iota(jnp.int32, (S5_Q, 128), 0).astype(F32)
    is_fq = lax.broadcasted_iota(jnp.int32, (S5_Q, 128), 1) < S5_STATE

    def per_channel(tbl):
        return jnp.broadcast_to(tbl[:, None, :], (S5_Q, S5_GROUP_CH, 128)).reshape(shape)

    p_re, p_im = map(per_channel, lam_pow(jnp.where(is_fq, (S5_Q - 1.0) - posq, posq)))
    w_re = p_re * bb_re - p_im * bb_im
    w_im = p_re * bb_im + p_im * bb_re
    bq = jnp.concatenate([w_re, w_im], axis=1)
    bqt = jnp.transpose(bq)
    bq_ref[...] = bqt.astype(BF16)

    edge = pos == jnp.where(is_f, 0, S5_Q - 1)
    be = jnp.concatenate([jnp.where(edge, bb_re, 0.0), jnp.where(edge, bb_im, 0.0)], axis=1)
    be_ref[...] = jnp.transpose(be).astype(BF16)

    q_re, q_im = map(per_channel, lam_pow(jnp.where(is_fq, posq + 1.0, S5_Q - posq)))
    ct_re = jnp.tile(cre_ref[...], (S5_Q, 1))
    ct_im = jnp.tile(cim_ref[...], (S5_Q, 1))
    g_re = q_re * ct_re - q_im * ct_im
    g_im = q_re * ct_im + q_im * ct_re
    cqt_ref[...] = jnp.concatenate([g_re, -g_im], axis=1).astype(BF16)

    a_re, a_im = lam_pow(float(S5_Q))
    a_ref[...] = jnp.concatenate([a_re, a_im], axis=1)

    kf = jnp.dot(ccf_ref[...], bqt, precision=lax.Precision.HIGHEST, preferred_element_type=F32)
    kb = jnp.dot(ccb_ref[...], bqt, precision=lax.Precision.HIGHEST, preferred_element_type=F32)
    gch = S5_GROUP_CH
    lo = S5_W - gch
    dd_scr[:, 0:S5_W] = kf
    dd_scr[:, lo:lo + S5_W] = kb
    dd_scr[:, lo:S5_W] = kf[:, lo:S5_W] + kb[:, 0:gch]
    for t in range(S5_Q):
        c0 = (S5_Q - 1 - t) * gch
        t_scr[t * gch:(t + 1) * gch, :] = dd_scr[:, c0:c0 + S5_W]
    t_ref[...] = t_scr[...].astype(BF16)


def _s5_prep_call(lam_re, lam_im, log_dt, b_re, b_im, c_re, c_im):
    def fb(p):
        return jnp.transpose(p, (1, 0, 2)).reshape(S5_GROUPS, 1, 2 * S5_STATE)

    def dup(p):
        return jnp.concatenate([p, p], axis=-1)

    ldt = fb(jnp.broadcast_to(log_dt[:, :, None], (2, S5_GROUPS, S5_STATE)))
    bt_re = dup(jnp.transpose(b_re, (0, 2, 1)))
    bt_im = dup(jnp.transpose(b_im, (0, 2, 1)))
    zero = jnp.zeros_like(c_re)
    ccf = jnp.concatenate([c_re, zero, -c_im, zero], axis=-1)
    ccb = jnp.concatenate([zero, c_re, zero, -c_im], axis=-1)

    row = pl.BlockSpec((None, 1, 128), lambda g: (g, 0, 0))
    mat16 = pl.BlockSpec((None, S5_GROUP_CH, 128), lambda g: (g, 0, 0))
    mat16w = pl.BlockSpec((None, S5_GROUP_CH, S5_W), lambda g: (g, 0, 0))
    sq = pl.BlockSpec((None, S5_W, S5_W), lambda g: (g, 0, 0))
    sq_shape = jax.ShapeDtypeStruct((S5_GROUPS, S5_W, S5_W), BF16)
    return pl.pallas_call(
        _s5_prep_kernel,
        out_shape=(sq_shape, sq_shape, sq_shape, sq_shape,
                   jax.ShapeDtypeStruct((S5_GROUPS, 1, S5_W), F32)),
        grid=(S5_GROUPS,),
        in_specs=[row, row, row, mat16, mat16, mat16, mat16, mat16w, mat16w],
        out_specs=(sq, sq, sq, sq, pl.BlockSpec((None, 1, S5_W), lambda g: (g, 0, 0))),
        scratch_shapes=[pltpu.VMEM((S5_W, S5_W), F32), pltpu.VMEM((S5_GROUP_CH, 2 * S5_W), F32)],
        name="s5_prep",
    )(fb(lam_re), fb(lam_im), ldt, bt_re, bt_im, dup(c_re), dup(c_im), ccf, ccb)


def _s5_kernel(u_ref, tt_ref, bqt_ref, cqt_ref, bet_ref, a_ref, h0_ref, y_ref, ns_ref,
               ut_scr, x_scr, spf_scr, spb_scr, ne_scr, yt_scr, xt_scr):
    gch = S5_GROUP_CH
    for s in range(S5_Q):
        rows = u_ref[pl.ds(s, S5_ROWS, stride=S5_Q), :]
        rows_t = jnp.transpose(rows).astype(BF16)
        for gl in range(S5_GPB):
            ut_scr[gl, s * gch:(s + 1) * gch, :] = rows_t[gl * gch:(gl + 1) * gch, :]

    for gl in range(S5_GPB):
        ut = ut_scr[gl]
        xt_scr[...] = _dot(bqt_ref[gl], ut)
        x = jnp.transpose(xt_scr[...])
        xt_scr[:, 0:S5_PROMPT_ROWS] = _dot(bet_ref[gl], ut[:, 0:S5_PROMPT_ROWS])
        ne = jnp.transpose(xt_scr[:, 0:S5_PROMPT_ROWS])
        for part in range(2):
            x_scr[part, pl.ds(gl, S5_ROWS, stride=S5_GPB), :] = x[:, part * 128:(part + 1) * 128]
            ne_scr[part, pl.ds(gl, S5_PROMPT_ROWS, stride=S5_GPB), :] = ne[:, part * 128:(part + 1) * 128]

    is_f = lax.broadcasted_iota(jnp.int32, (1, 128), 1) < S5_STATE
    a_re = a_ref[:, 0:128]
    a_im = a_ref[:, 128:256]

    def tile(row):
        return pl.ds(pl.multiple_of(row * S5_GPB, S5_GPB), S5_GPB)

    def scan(base, nseq, nchunk, s_init):
        def body(i, state):
            new = []
            for b in range(nseq):
                s_re, s_im = state[b]
                rows_f = tile(base + b * nchunk + i)
                rows_b = tile(base + b * nchunk + (nchunk - 1 - i))
                spf_scr[0, rows_f, :] = s_re
                spf_scr[1, rows_f, :] = s_im
                spb_scr[0, rows_b, :] = s_re
                spb_scr[1, rows_b, :] = s_im
                x_re = jnp.where(is_f, x_scr[0, rows_f, :], x_scr[0, rows_b, :])
                x_im = jnp.where(is_f, x_scr[1, rows_f, :], x_scr[1, rows_b, :])
                new.append((a_re * s_re - a_im * s_im + x_re, a_re * s_im + a_im * s_re + x_im))
            return tuple(new)

        lax.fori_loop(0, nchunk, body, tuple(s_init))

    zero = jnp.zeros((S5_GPB, 128), F32)
    scan(0, BATCH, S5_PROMPT_CHUNKS, [(zero, zero)] * BATCH)
    scan(S5_PROMPT_ROWS, DEC_BATCH, S5_SAMPLE_CHUNKS,
         [(h0_ref[b, :, 0:128], h0_ref[b, :, 128:256]) for b in range(DEC_BATCH)])

    for b in range(BATCH):
        first = pl.ds(b * S5_PROMPT_CHUNKS * S5_GPB, S5_GPB)
        last = pl.ds(((b + 1) * S5_PROMPT_CHUNKS - 1) * S5_GPB, S5_GPB)
        for part in range(2):
            ns_ref[b, :, part * 128:(part + 1) * 128] = jnp.where(is_f, ne_scr[part, first, :], ne_scr[part, last, :])

    for gl in range(S5_GPB):
        rows = pl.ds(gl, S5_ROWS, stride=S5_GPB)
        carried = jnp.concatenate([jnp.where(is_f, spf_scr[p, rows, :], spb_scr[p, rows, :]) for p in range(2)],
                                  axis=1).astype(BF16)
        yt = _dot(tt_ref[gl], ut_scr[gl]) + lax.dot_general(cqt_ref[gl], carried, NT_DIMS,
                                                            preferred_element_type=F32)
        for t in range(S5_Q):
            yt_scr[t, gl * gch:(gl + 1) * gch, :] = yt[t * gch:(t + 1) * gch, :]
    for t in range(S5_Q):
        y_ref[pl.ds(t, S5_ROWS, stride=S5_Q), :] = jnp.transpose(yt_scr[t])


def _s5_call(u, mats, h0):
    tt_m, bqt_m, cqt_m, bet_m, a_m = mats
    nsteps = S5_GROUPS // S5_GPB
    sq = pl.BlockSpec((S5_GPB, S5_W, S5_W), lambda g: (g, 0, 0))
    state_scr = pltpu.VMEM((2, S5_ROWS * S5_GPB, 128), F32)
    return pl.pallas_call(
        _s5_kernel,
        out_shape=(jax.ShapeDtypeStruct((nsteps, T_TOK, 128), F32),
                   jax.ShapeDtypeStruct((nsteps, BATCH, S5_GPB, S5_W), F32)),
        grid=(nsteps,),
        in_specs=[
            pl.BlockSpec((None, T_TOK, 128), lambda g: (g, 0, 0)),
            sq, sq, sq, sq,
            pl.BlockSpec((S5_GPB, S5_W), lambda g: (g, 0)),
            pl.BlockSpec((None, DEC_BATCH, S5_GPB, S5_W), lambda g: (g, 0, 0, 0)),
        ],
        out_specs=(pl.BlockSpec((None, T_TOK, 128), lambda g: (g, 0, 0)),
                   pl.BlockSpec((None, BATCH, S5_GPB, S5_W), lambda g: (g, 0, 0, 0))),
        scratch_shapes=[pltpu.VMEM((S5_GPB, S5_W, S5_ROWS), BF16), state_scr, state_scr, state_scr,
                        pltpu.VMEM((2, S5_PROMPT_ROWS * S5_GPB, 128), F32),
                        pltpu.VMEM((S5_Q, 128, S5_ROWS), F32), pltpu.VMEM((S5_W, S5_ROWS), F32)],
        compiler_params=pltpu.CompilerParams(vmem_limit_bytes=VMEM_LIMIT),
        name="s5_scan",
    )(u, tt_m, bqt_m, cqt_m, bet_m, a_m.reshape(S5_GROUPS, S5_W), h0)


def _split_bf16(x):
    hi = x.astype(BF16)
    r1 = x - hi.astype(F32)
    mid = r1.astype(BF16)
    lo = (r1 - mid.astype(F32)).astype(BF16)
    return hi, mid, lo


def _gla_kernel(*refs, seq_len, has_s0):
    q_ref, k_ref, v_ref, gf_ref, gb_ref, r_ref, gn_ref = refs[:7]
    s0_ref = refs[7] if has_s0 else None
    o_ref, sfin_ref, oi_scr, qd_scr, kv_scr, dec_scr, ss_scr = refs[7 + has_s0:]
    nblk = seq_len // GLA_BLK
    nchunk = seq_len // GLA_CHUNK
    cl = GLA_CHUNK
    ti = lax.broadcasted_iota(jnp.int32, (GLA_BLK, GLA_BLK), 0)
    si = lax.broadcasted_iota(jnp.int32, (GLA_BLK, GLA_BLK), 1)
    same = lax.shift_right_logical(ti, 6) == lax.shift_right_logical(si, 6)
    keep = (same & (ti >= si), same & (ti <= si))
    tri = tuple(kp.astype(BF16) for kp in keep)
    lane_head = lax.shift_right_logical(lax.broadcasted_iota(jnp.int32, (cl, GLA_QK), 1), 6)
    zeros_v = jnp.zeros((cl, GLA_DV), BF16)
    heads = [(slice(h * GLA_DK, (h + 1) * GLA_DK), slice(h * GLA_DV, (h + 1) * GLA_DV)) for h in range(GLA_HEADS)]

    for j in range(nblk):
        rows = slice(j * GLA_BLK, (j + 1) * GLA_BLK)
        q = q_ref[rows, :] * (GLA_DK ** -0.5)
        k = k_ref[rows, :]
        v = v_ref[rows, :].astype(BF16)
        qd, kd, k2t = [], [], []
        for d, g_ref in enumerate((gf_ref, gb_ref)):
            b = sum(_dot(tri[d], part) for part in _split_bf16(g_ref[rows, :]))
            last = cl - 1 if d == 0 else 0
            b_last = [b[c * cl + last:c * cl + last + 1] for c in range(GLA_CPB)]
            bl = jnp.concatenate([jnp.broadcast_to(x, (cl, GLA_QK)) for x in b_last], axis=0)
            qd_d = (q * jnp.exp(b)).astype(BF16)
            qd_scr[d, rows, :] = qd_d
            qd.append(qd_d)
            kd.append((k * jnp.exp(-b)).astype(BF16))
            k2t.append(jnp.transpose(k * jnp.exp(bl - b)).astype(BF16))
            for c in range(GLA_CPB):
                dec_scr[d, j * GLA_CPB + c] = jnp.exp(jnp.transpose(jnp.broadcast_to(b_last[c], (GLA_DV, GLA_QK))))
        for h, (ks, vs) in enumerate(heads):
            att = [jnp.where(keep[d], lax.dot_general(qd[d][:, ks], kd[d][:, ks], NT_DIMS,
                                                      preferred_element_type=F32), 0.0) for d in range(2)]
            oi_scr[rows, vs] = _dot((att[0] + att[1]).astype(BF16), v[:, vs])
            vh = v[:, vs]
            vexp = jnp.concatenate(
                [jnp.concatenate([vh[c * cl:(c + 1) * cl] if c2 == c else zeros_v for c2 in range(GLA_CPB)], axis=1)
                 for c in range(GLA_CPB)], axis=0)
            for d in range(2):
                kv_scr[d, j, h] = _dot(k2t[d][ks, :], vexp)

    for d in range(2):
        s = s0_ref[d] if has_s0 else jnp.zeros((GLA_QK, GLA_DV), F32)
        for cg in (range(nchunk) if d == 0 else range(nchunk - 1, -1, -1)):
            j, c = divmod(cg, GLA_CPB)
            ss_scr[d, cg] = s.astype(BF16)
            kv = jnp.concatenate([kv_scr[d, j, h, :, c * GLA_DV:(c + 1) * GLA_DV] for h in range(GLA_HEADS)], axis=0)
            s = s * dec_scr[d, cg] + kv
        sfin_ref[d] = s

    for cg in range(nchunk):
        rows = slice(cg * cl, (cg + 1) * cl)
        inter = []
        for d in range(2):
            qc = qd_scr[d, rows, :]
            qstack = jnp.concatenate([jnp.where(lane_head == h, qc, jnp.zeros_like(qc)) for h in range(GLA_HEADS)],
                                     axis=0)
            inter.append(_dot(qstack, ss_scr[d, cg]))
        gate = jax.nn.silu(r_ref[rows, :])
        for h, (ks, vs) in enumerate(heads):
            hr = slice(h * cl, (h + 1) * cl)
            oh = oi_scr[rows, vs] + inter[0][hr] + inter[1][hr]
            oh = oh * lax.rsqrt(jnp.mean(oh * oh, axis=-1, keepdims=True) + EPS) * gn_ref[...]
            o_ref[rows, vs] = oh * gate[:, vs]


def _gla_call(z, g, gla_norm, s0, seq_len, nseq, row0):
    assert row0 % seq_len == 0
    r0 = row0 // seq_len
    has_s0 = s0 is not None
    qk_off = 0
    v_off = 2 * GLA_QK // GLA_VW
    in_specs = [
        pl.BlockSpec((seq_len, GLA_QK), lambda i: (r0 + i, qk_off)),
        pl.BlockSpec((seq_len, GLA_QK), lambda i: (r0 + i, qk_off + 1)),
        pl.BlockSpec((seq_len, GLA_VW), lambda i: (r0 + i, v_off)),
        pl.BlockSpec((seq_len, GLA_QK), lambda i: (r0 + i, 0)),
        pl.BlockSpec((seq_len, GLA_QK), lambda i: (r0 + i, 1)),
        pl.BlockSpec((seq_len, GLA_VW), lambda i: (r0 + i, v_off + 1)),
        pl.BlockSpec((1, GLA_DV), lambda i: (0, 0)),
    ]
    args = [z, z, z, g, g, z, gla_norm]
    if has_s0:
        in_specs.append(pl.BlockSpec((None, 2, GLA_QK, GLA_DV), lambda i: (i, 0, 0, 0)))
        args.append(s0)
    return pl.pallas_call(
        functools.partial(_gla_kernel, seq_len=seq_len, has_s0=has_s0),
        out_shape=(jax.ShapeDtypeStruct((nseq * seq_len, GLA_VW), F32),
                   jax.ShapeDtypeStruct((nseq, 2, GLA_QK, GLA_DV), F32)),
        grid=(nseq,),
        in_specs=in_specs,
        out_specs=(pl.BlockSpec((seq_len, GLA_VW), lambda i: (i, 0)),
                   pl.BlockSpec((None, 2, GLA_QK, GLA_DV), lambda i: (i, 0, 0, 0))),
        scratch_shapes=[
            pltpu.VMEM((seq_len, GLA_VW), F32),
            pltpu.VMEM((2, seq_len, GLA_QK), BF16),
            pltpu.VMEM((2, seq_len // GLA_BLK, GLA_HEADS, GLA_DK, GLA_CPB * GLA_DV), F32),
            pltpu.VMEM((2, seq_len // GLA_CHUNK, GLA_QK, GLA_DV), F32),
            pltpu.VMEM((2, seq_len // GLA_CHUNK, GLA_QK, GLA_DV), BF16),
        ],
        compiler_params=pltpu.CompilerParams(vmem_limit_bytes=VMEM_LIMIT),
        name=f"gla_len{seq_len}",
    )(*args)


MLP_LOAD_TILE = (256, 1024)


def _load_mlp_weights(w1_hbm, w2_hbm, w1_scr, w2_scr, stage, sem, layer):
    tr, tc = MLP_LOAD_TILE
    tiles = ([(w1_hbm, w1_scr, r, c) for r in range(D_MODEL // tr) for c in range(D_FF // tc)]
             + [(w2_hbm, w2_scr, r, c) for r in range(D_FF // tr) for c in range(D_MODEL // tc)])

    def copy(t):
        src, _, r, c = tiles[t]
        return pltpu.make_async_copy(src.at[layer, pl.ds(r * tr, tr), pl.ds(c * tc, tc)],
                                     stage.at[t % 2], sem.at[t % 2])

    copy(0).start()
    for t in range(len(tiles)):
        if t + 1 < len(tiles):
            copy(t + 1).start()
        copy(t).wait()
        _, dst, r, c = tiles[t]
        dst[r * tr:(r + 1) * tr, c * tc:(c + 1) * tc] = stage[t % 2].astype(BF16)


def _mlp_tail(x, mix, m_ref, gn2_ref, w1_ref, w2_ref):
    y1 = x + m_ref[:, 2 * D_MODEL:3 * D_MODEL] * mix
    h = _norm_mod(y1, gn2_ref[...], m_ref[:, 3 * D_MODEL:4 * D_MODEL], m_ref[:, 4 * D_MODEL:5 * D_MODEL]).astype(BF16)
    tf = 512
    acc = jnp.zeros(y1.shape, F32)
    for c in range(D_FF // tf):
        a = _dot(h, w1_ref[:, c * tf:(c + 1) * tf])
        a = jnp.square(jnp.maximum(a, 0.0)).astype(BF16)
        acc = acc + _dot(a, w2_ref[c * tf:(c + 1) * tf, :])
    return y1 + m_ref[:, 5 * D_MODEL:6 * D_MODEL] * acc


def _even_out_kernel(xp_ref, xs_ref, y5_ref, u_ref, dskip_ref, wglu_ref, bglu_ref, glap_ref, glas_ref, wout_ref,
                     m_ref, gn2_ref, w1_hbm, w2_hbm, o_ref, w1_ref, w2_ref, stage, sem, *, layer):
    @pl.when(pl.program_id(0) == 0)
    def _():
        _load_mlp_weights(w1_hbm, w2_hbm, w1_ref, w2_ref, stage, sem, layer)

    nblk = S5_WIDTH // 128
    ys = (jnp.concatenate([y5_ref[b] for b in range(nblk)], axis=1)
          + jnp.concatenate([u_ref[b] for b in range(nblk)], axis=1) * dskip_ref[...])
    gl = jax.nn.gelu(ys)
    s5o = gl * jax.nn.sigmoid(_dot(gl.astype(BF16), wglu_ref[...]) + bglu_ref[...])
    gla = _token_tile(glap_ref, glas_ref, _OUT_TM).astype(BF16)
    mix = _dot(s5o.astype(BF16), wout_ref[0:S5_WIDTH, :]) + _dot(gla, wout_ref[S5_WIDTH:, :])
    o_ref[...] = _mlp_tail(_token_tile(xp_ref, xs_ref, _OUT_TM), mix, m_ref, gn2_ref, w1_ref, w2_ref)


def _odd_out_kernel(x_ref, attp_ref, atts_ref, wo_ref, m_ref, gn2_ref, w1_hbm, w2_hbm, op_ref, os_ref,
                    w1_ref, w2_ref, stage, sem, *, layer):
    @pl.when(pl.program_id(0) == 0)
    def _():
        _load_mlp_weights(w1_hbm, w2_hbm, w1_ref, w2_ref, stage, sem, layer)

    mix = _dot(_token_tile(attp_ref, atts_ref, _OUT_TM), wo_ref[...])
    y = _mlp_tail(x_ref[...], mix, m_ref, gn2_ref, w1_ref, w2_ref)
    is_prompt = pl.program_id(0) < T_PROMPT // _OUT_TM

    @pl.when(is_prompt)
    def _():
        op_ref[...] = y

    @pl.when(jnp.logical_not(is_prompt))
    def _():
        os_ref[...] = y


_OUT_TM = 512


def _const_spec(shape):
    return pl.BlockSpec(shape, lambda i: (0,) * len(shape), pipeline_mode=pl.Buffered(1))


def _tail_specs(layer):
    tm = _OUT_TM
    return [
        pl.BlockSpec((None, 1, 6 * D_MODEL), lambda i: (layer * COND_ROWS + _cond_row(i, tm), 0, 0)),
        _const_spec((1, D_MODEL)),
        pl.BlockSpec(memory_space=pl.ANY),
        pl.BlockSpec(memory_space=pl.ANY),
    ]


def _tail_scratch():
    return [pltpu.VMEM((D_MODEL, D_FF), BF16), pltpu.VMEM((D_FF, D_MODEL), BF16),
            pltpu.VMEM((2,) + MLP_LOAD_TILE, F32), pltpu.SemaphoreType.DMA((2,))]


_TAIL_PARAMS = dict(dimension_semantics=("arbitrary",), vmem_limit_bytes=VMEM_LIMIT)


def _even_out_call(xp, xs, y5, u, d_skip, w_glu, b_glu, gla_p, gla_s, w_out, mods, layer, gn2, w1, w2):
    tm = _OUT_TM
    return pl.pallas_call(
        functools.partial(_even_out_kernel, layer=layer),
        out_shape=jax.ShapeDtypeStruct((T_TOK, D_MODEL), F32),
        grid=(T_TOK // tm,),
        in_specs=_token_specs(tm) + [
            pl.BlockSpec((S5_WIDTH // 128, tm, 128), lambda i: (0, i, 0)),
            pl.BlockSpec((S5_WIDTH // 128, tm, 128), lambda i: (0, i, 0)),
            _const_spec((1, S5_WIDTH)),
            _const_spec((S5_WIDTH, S5_WIDTH)),
            _const_spec((1, S5_WIDTH)),
        ] + _token_specs(tm, GLA_VW) + [
            _const_spec((S5_WIDTH + GLA_VW, D_MODEL)),
        ] + _tail_specs(layer),
        out_specs=pl.BlockSpec((tm, D_MODEL), lambda i: (i, 0)),
        scratch_shapes=_tail_scratch(),
        compiler_params=pltpu.CompilerParams(**_TAIL_PARAMS),
        name="even_out_mlp",
    )(xp, xs, y5, u, d_skip, w_glu, b_glu, gla_p, gla_s, w_out, mods, gn2, w1, w2)


def _odd_out_call(x, att_p, att_s, w_o, mods, layer, gn2, w1, w2):
    tm = _OUT_TM
    return pl.pallas_call(
        functools.partial(_odd_out_kernel, layer=layer),
        out_shape=(jax.ShapeDtypeStruct((T_PROMPT, D_MODEL), F32),
                   jax.ShapeDtypeStruct((T_SAMPLE, D_MODEL), F32)),
        grid=(T_TOK // tm,),
        in_specs=[pl.BlockSpec((tm, D_MODEL), lambda i: (i, 0))] + _token_specs(tm) + [
            _const_spec((D_MODEL, D_MODEL)),
        ] + _tail_specs(layer),
        out_specs=tuple(_token_specs(tm)),
        scratch_shapes=_tail_scratch(),
        compiler_params=pltpu.CompilerParams(**_TAIL_PARAMS),
        name="odd_out_mlp",
    )(x, att_p, att_s, w_o, mods, gn2, w1, w2)


def _qkv_kernel(x_ref, gn_ref, m_ref, w_ref, qn_ref, kn_ref, cos_ref, sin_ref,
                q_ref, kb_ref, vb_ref, k32_ref, v32_ref, *, tile):
    h = _norm_mod(x_ref[...], gn_ref[...], m_ref[:, 0:D_MODEL], m_ref[:, D_MODEL:2 * D_MODEL]).astype(BF16)
    z = _dot(h, w_ref[...])
    v = z[:, (N_HEADS + KV_HEADS) * HEAD_DIM:]
    vb_ref[...] = v.astype(BF16)
    even_lane = (lax.broadcasted_iota(jnp.int32, (1, HEAD_DIM), 1) & 1) == 0

    def heads(rope):
        for hd in range(N_HEADS + KV_HEADS):
            xh = z[:, hd * HEAD_DIM:(hd + 1) * HEAD_DIM]
            gain = qn_ref[...] if hd < N_HEADS else kn_ref[...]
            xh = xh * lax.rsqrt(jnp.mean(xh * xh, axis=-1, keepdims=True) + EPS) * gain
            if rope:
                partner = jnp.where(even_lane, pltpu.roll(xh, HEAD_DIM - 1, 1), pltpu.roll(xh, 1, 1))
                xh = xh * cos_ref[...] + partner * sin_ref[...]
            if hd < N_HEADS:
                q_ref[:, hd * HEAD_DIM:(hd + 1) * HEAD_DIM] = xh.astype(BF16)
            else:
                cols = slice((hd - N_HEADS) * HEAD_DIM, (hd - N_HEADS + 1) * HEAD_DIM)
                kb_ref[:, cols] = xh.astype(BF16)
                if not rope:
                    k32_ref[:, cols] = xh

    is_sample = pl.program_id(0) >= T_PROMPT // tile

    @pl.when(is_sample)
    def _():
        heads(True)

    @pl.when(jnp.logical_not(is_sample))
    def _():
        heads(False)
        v32_ref[...] = v


def _qkv_call(x, gn, mods, layer, w_qkv, q_norm, k_norm, cos_t, sin_t):
    tm = 512
    pos_tiles = DEC_SEQ // tm
    n_prompt = T_PROMPT // tm
    kvw = KV_HEADS * HEAD_DIM

    def pos_map(i):
        return (jnp.maximum(i - n_prompt, 0) % pos_tiles, 0)

    def prompt_map(i):
        return (jnp.minimum(i, n_prompt - 1), 0)

    return pl.pallas_call(
        functools.partial(_qkv_kernel, tile=tm),
        out_shape=(jax.ShapeDtypeStruct((T_TOK, N_HEADS * HEAD_DIM), BF16),
                   jax.ShapeDtypeStruct((T_TOK, kvw), BF16),
                   jax.ShapeDtypeStruct((T_TOK, kvw), BF16),
                   jax.ShapeDtypeStruct((T_PROMPT, kvw), F32),
                   jax.ShapeDtypeStruct((T_PROMPT, kvw), F32)),
        grid=(T_TOK // tm,),
        in_specs=[
            pl.BlockSpec((tm, D_MODEL), lambda i: (i, 0)),
            pl.BlockSpec((1, D_MODEL), lambda i: (0, 0)),
            pl.BlockSpec((None, 1, 6 * D_MODEL), lambda i: (layer * COND_ROWS + _cond_row(i, tm), 0, 0)),
            pl.BlockSpec(w_qkv.shape, lambda i: (0, 0)),
            pl.BlockSpec((1, HEAD_DIM), lambda i: (0, 0)),
            pl.BlockSpec((1, HEAD_DIM), lambda i: (0, 0)),
            pl.BlockSpec((tm, HEAD_DIM), pos_map),
            pl.BlockSpec((tm, HEAD_DIM), pos_map),
        ],
        out_specs=(pl.BlockSpec((tm, N_HEADS * HEAD_DIM), lambda i: (i, 0)),
                   pl.BlockSpec((tm, kvw), lambda i: (i, 0)),
                   pl.BlockSpec((tm, kvw), lambda i: (i, 0)),
                   pl.BlockSpec((tm, kvw), prompt_map),
                   pl.BlockSpec((tm, kvw), prompt_map)),
        compiler_params=pltpu.CompilerParams(vmem_limit_bytes=VMEM_LIMIT),
        name="odd_qkv",
    )(x, gn, mods, w_qkv, q_norm, k_norm, cos_t, sin_t)


def _rope_tables():
    rows = DEC_SEQ // GRID_W
    row = jnp.repeat(jnp.arange(rows, dtype=F32), GRID_W)
    col = jnp.tile(jnp.arange(GRID_W, dtype=F32), rows)
    inv = ROPE_THETA ** (-jnp.arange(0, AXIS_DIM, 2, dtype=F32) / AXIS_DIM)
    ang = jnp.concatenate([row[:, None] * inv, col[:, None] * inv], axis=-1)
    cos_t = jnp.repeat(jnp.cos(ang), 2, axis=-1)
    sin = jnp.sin(ang)
    sin_t = jnp.stack([-sin, sin], axis=-1).reshape(DEC_SEQ, HEAD_DIM)
    return cos_t, sin_t


def _attn_kernel(*refs, has_cache):
    q_ref, k_ref, v_ref = refs[:3]
    ck_ref, cv_ref = refs[3:5] if has_cache else (None, None)
    o_ref = refs[-1]
    c = HEAD_DIM ** -0.5 * math.log2(math.e)
    ones_col = (lax.broadcasted_iota(jnp.int32, (1, HEAD_DIM), 1) == 0).astype(BF16)

    def with_ones(v):
        return jnp.concatenate([v, jnp.broadcast_to(ones_col, v.shape)], axis=1)

    k = k_ref[...]
    v = with_ones(v_ref[...])
    if has_cache:
        ck = ck_ref[...].astype(BF16)
        cv = with_ones(cv_ref[...].astype(BF16))
    for r in range(Q_PER_KV):
        cs = slice(r * HEAD_DIM, (r + 1) * HEAD_DIM)
        q = q_ref[:, cs]
        s = lax.dot_general(q, k, NT_DIMS, preferred_element_type=F32)
        m = jnp.max(s, axis=-1, keepdims=True)
        if has_cache:
            sc = lax.dot_general(q, ck, NT_DIMS, preferred_element_type=F32)
            m = jnp.maximum(m, jnp.max(sc, axis=-1, keepdims=True))
        mc = m * c
        o = _dot(jnp.exp2(s * c - mc).astype(BF16), v)
        if has_cache:
            o = o + _dot(jnp.exp2(sc * c - mc).astype(BF16), cv)
        o_ref[:, cs] = (o[:, 0:HEAD_DIM] / o[:, HEAD_DIM:HEAD_DIM + 1]).astype(BF16)


def _attn_call(q, k, v, cache_k, cache_v, seq_len, row0, nrows):
    assert row0 % seq_len == 0 and nrows % seq_len == 0
    has_cache = cache_k is not None
    b0 = row0 // seq_len
    gw = Q_PER_KV * HEAD_DIM
    in_specs = [
        pl.BlockSpec((seq_len, gw), lambda b, g: (b0 + b, g)),
        pl.BlockSpec((seq_len, HEAD_DIM), lambda b, g: (b0 + b, g)),
        pl.BlockSpec((seq_len, HEAD_DIM), lambda b, g: (b0 + b, g)),
    ]
    args = [q, k, v]
    if has_cache:
        in_specs += [pl.BlockSpec((PAST_LEN, HEAD_DIM), lambda b, g: (b, g)),
                     pl.BlockSpec((PAST_LEN, HEAD_DIM), lambda b, g: (b, g))]
        args += [cache_k, cache_v]
    return pl.pallas_call(
        functools.partial(_attn_kernel, has_cache=has_cache),
        out_shape=jax.ShapeDtypeStruct((nrows, N_HEADS * HEAD_DIM), BF16),
        grid=(nrows // seq_len, KV_HEADS),
        in_specs=in_specs,
        out_specs=pl.BlockSpec((seq_len, gw), lambda b, g: (b, g)),
        compiler_params=pltpu.CompilerParams(vmem_limit_bytes=VMEM_LIMIT),
        name=f"attn_len{seq_len}",
    )(*args)


def kernel(x_prompt, x_sample, state_s5_re, state_s5_im, state_gla, cache_k, cache_v, c, c_ctx, norm_mix, norm_mlp, w_ada, b_ada, w_mlp_in, w_mlp_out, w_in_e, w_out_e, s5_lambda_re, s5_lambda_im, s5_log_dt, s5_b_re, s5_b_im, s5_c_re, s5_c_im, s5_d, s5_w_glu, s5_b_glu, gla_w_gate2, gla_b_gate, gla_norm, w_qkv_o, w_o_o, q_norm, k_norm):
    xp = x_prompt.reshape(T_PROMPT, D_MODEL)
    xs = x_sample.reshape(T_SAMPLE, D_MODEL)
    cond8 = jnp.concatenate([c_ctx[None, :], c, jnp.zeros((COND_ROWS - 1 - DEC_BATCH, D_MODEL), F32)], axis=0)
    mods = _ada_call(cond8, w_ada, b_ada).reshape(DEPTH * COND_ROWS, 1, 6 * D_MODEL)
    w1_all, w2_all = w_mlp_in, w_mlp_out

    n_main = S5_WIDTH + 2 * GLA_QK + 2 * GLA_VW
    w_in = w_in_e[0]
    w_main = w_in[:, :n_main].astype(BF16)
    w_glr = jnp.pad(w_in[:, n_main:], ((0, 0), (0, 128 - 2 * GLA_RANK))).astype(BF16)
    w_gate = jnp.zeros((128, 2 * GLA_QK), F32)
    w_gate = w_gate.at[0:GLA_RANK, 0:GLA_QK].set(gla_w_gate2[0, 0])
    w_gate = w_gate.at[GLA_RANK:2 * GLA_RANK, GLA_QK:].set(gla_w_gate2[0, 1]).astype(BF16)
    b_gate = gla_b_gate[0].reshape(1, 2 * GLA_QK)
    u, z, g = _inproj_call(xp, xs, norm_mix[0:1], mods, 0, w_main, w_glr, w_gate, b_gate)

    mats = _s5_prep_call(s5_lambda_re[0], s5_lambda_im[0], s5_log_dt[0], s5_b_re[0], s5_b_im[0],
                         s5_c_re[0], s5_c_im[0])

    def state_rows(s):
        return jnp.transpose(s, (2, 0, 1, 3)).reshape(S5_GROUPS, DEC_BATCH, 2 * S5_STATE)

    h0 = jnp.concatenate([state_rows(state_s5_re[:, 0]), state_rows(state_s5_im[:, 0])], axis=-1)
    nsteps = S5_GROUPS // S5_GPB
    h0 = jnp.transpose(h0.reshape(nsteps, S5_GPB, DEC_BATCH, S5_W), (0, 2, 1, 3))
    y5, ns = _s5_call(u, mats, h0)
    ns = jnp.transpose(ns, (0, 2, 1, 3)).reshape(S5_GROUPS, BATCH, S5_W)

    def state_out(n):
        return jnp.transpose(n.reshape(S5_GROUPS, BATCH, 2, S5_STATE), (1, 2, 0, 3))[:, None]

    new_s5_re = state_out(ns[:, :, :2 * S5_STATE])
    new_s5_im = state_out(ns[:, :, 2 * S5_STATE:])

    gn_gla = gla_norm[0].reshape(1, GLA_DV)
    gla_p, sfin = _gla_call(z, g, gn_gla, None, SEQ, BATCH, 0)
    s0 = state_gla[:, 0].reshape(DEC_BATCH, 2, GLA_QK, GLA_DV)
    gla_s, _ = _gla_call(z, g, gn_gla, s0, DEC_SEQ, DEC_BATCH, T_PROMPT)
    new_gla = sfin.reshape(BATCH, 1, 2, GLA_HEADS, GLA_DK, GLA_DV)

    x = _even_out_call(xp, xs, y5, u, s5_d[0].reshape(1, S5_WIDTH), s5_w_glu[0].astype(BF16),
                       s5_b_glu[0].reshape(1, S5_WIDTH), gla_p, gla_s, w_out_e[0].astype(BF16), mods, 0,
                       norm_mlp[0:1], w1_all, w2_all)

    cos_t, sin_t = _rope_tables()
    q, k, v, k32, v32 = _qkv_call(x, norm_mix[1:2], mods, 1, w_qkv_o[0].astype(BF16),
                                  q_norm[0].reshape(1, HEAD_DIM), k_norm[0].reshape(1, HEAD_DIM), cos_t, sin_t)
    att_p = _attn_call(q, k, v, None, None, SEQ, 0, T_PROMPT)
    ck = cache_k[:, 0].reshape(DEC_BATCH * PAST_LEN, KV_HEADS * HEAD_DIM)
    cv = cache_v[:, 0].reshape(DEC_BATCH * PAST_LEN, KV_HEADS * HEAD_DIM)
    att_s = _attn_call(q, k, v, ck, cv, DEC_SEQ, T_PROMPT, T_SAMPLE)
    yp, ys = _odd_out_call(x, att_p, att_s, w_o_o[0].astype(BF16), mods, 1, norm_mlp[1:2],
                           w1_all, w2_all)

    new_k = k32.reshape(BATCH, 1, SEQ, KV_HEADS, HEAD_DIM)
    new_v = v32.reshape(BATCH, 1, SEQ, KV_HEADS, HEAD_DIM)
    y_prompt = yp.reshape(BATCH, SEQ, D_MODEL)
    y_sample = ys.reshape(DEC_BATCH, DEC_SEQ, D_MODEL)
    return (y_prompt, y_sample, new_s5_re, new_s5_im, new_gla, new_k, new_v)
```

```python
import functools
import math

import jax
import jax.numpy as jnp
import numpy as np
from jax import lax
from jax.experimental import pallas as pl
from jax.experimental.pallas import tpu as pltpu

F32 = jnp.float32
BF16 = jnp.bfloat16

D_MODEL = 1024
BATCH = 16
SEQ = 256
DEPTH = 2
DEC_BATCH = 4
DEC_SEQ = 1024
PAST_LEN = 512
GRID_W = 64
S5_WIDTH = 512
S5_GROUP_CH = 16
S5_GROUPS = 32
S5_STATE = 64
GLA_HEADS = 4
GLA_VW = 512
GLA_DV = 128
GLA_DK = 64
GLA_QK = 256
GLA_RANK = 16
GLA_TAU = 16.0
GLA_CHUNK = 64
GLA_CPB = 4
GLA_BLK = GLA_CPB * GLA_CHUNK
HEAD_DIM = 128
N_HEADS = 8
KV_HEADS = 2
Q_PER_KV = N_HEADS // KV_HEADS
AXIS_DIM = 64
ROPE_THETA = 10000.0
D_FF = 4096
EPS = 1e-6

T_PROMPT = BATCH * SEQ
T_SAMPLE = DEC_BATCH * DEC_SEQ
T_TOK = T_PROMPT + T_SAMPLE
COND_ROWS = 8
COND_SPAN = 1024
PROMPT_SPANS = T_PROMPT // COND_SPAN

S5_Q = 16
S5_W = S5_Q * S5_GROUP_CH
S5_GPB = 128 // S5_GROUP_CH
S5_ROWS = T_TOK // S5_Q
S5_PROMPT_ROWS = T_PROMPT // S5_Q
S5_PROMPT_CHUNKS = SEQ // S5_Q
S5_SAMPLE_CHUNKS = DEC_SEQ // S5_Q

VMEM_LIMIT = 56 * 1024 * 1024

NT_DIMS = (((1,), (1,)), ((), ()))
TN_DIMS = (((0,), (0,)), ((), ()))


def _cond_row(i, tile):
    return jnp.maximum((i * tile) // COND_SPAN - (PROMPT_SPANS - 1), 0)


def _norm_mod(x, gain, shift, scale):
    y = x * lax.rsqrt(jnp.mean(x * x, axis=-1, keepdims=True) + EPS)
    return (y * gain) * (1.0 + scale) + shift


def _dot(a, b):
    return jnp.dot(a, b, preferred_element_type=F32)


def _ada_kernel(cond_ref, w_ref, b_ref, o_ref):
    s = jax.nn.silu(cond_ref[...]).astype(BF16)
    o_ref[...] = _dot(s, w_ref[...].astype(BF16)) + b_ref[...]


def _ada_call(cond8, w_ada, b_ada):
    tn = 2048
    nj = 6 * D_MODEL // tn
    return pl.pallas_call(
        _ada_kernel,
        out_shape=jax.ShapeDtypeStruct((DEPTH, COND_ROWS, 6 * D_MODEL), F32),
        grid=(DEPTH, nj),
        in_specs=[
            pl.BlockSpec((COND_ROWS, D_MODEL), lambda l, j: (0, 0)),
            pl.BlockSpec((None, D_MODEL, tn), lambda l, j: (l, 0, j)),
            pl.BlockSpec((None, 1, tn), lambda l, j: (l, 0, j)),
        ],
        out_specs=pl.BlockSpec((None, COND_ROWS, tn), lambda l, j: (l, 0, j)),
        compiler_params=pltpu.CompilerParams(vmem_limit_bytes=VMEM_LIMIT),
        name="ada_mod",
    )(cond8, w_ada, b_ada.reshape(DEPTH, 1, 6 * D_MODEL))


def _token_specs(tile, width=D_MODEL):
    n_prompt = T_PROMPT // tile
    return [pl.BlockSpec((tile, width), lambda i: (jnp.minimum(i, n_prompt - 1), 0)),
            pl.BlockSpec((tile, width), lambda i: (jnp.maximum(i - n_prompt, 0), 0))]


def _token_tile(xp_ref, xs_ref, tile):
    return jnp.where(pl.program_id(0) < T_PROMPT // tile, xp_ref[...], xs_ref[...])


def _inproj_kernel(xp_ref, xs_ref, gn_ref, m_ref, w_ref, wglr_ref, wg_ref, bg_ref, u_ref, z_ref, g_ref, *, tile):
    x = _token_tile(xp_ref, xs_ref, tile)
    h = _norm_mod(x, gn_ref[...], m_ref[:, 0:D_MODEL], m_ref[:, D_MODEL:2 * D_MODEL]).astype(BF16)
    z = _dot(h, w_ref[...])
    for blk in range(S5_WIDTH // 128):
        u_ref[blk] = z[:, blk * 128:(blk + 1) * 128]
    z_ref[...] = z[:, S5_WIDTH:]
    glr = _dot(h, wglr_ref[...]).astype(BF16)
    pre = _dot(glr, wg_ref[...]) + bg_ref[...]
    g_ref[...] = jax.nn.log_sigmoid(pre) * (1.0 / GLA_TAU)


def _inproj_call(xp, xs, gn, mods, layer, w_main, w_glr, w_gate, b_gate):
    tm = 512
    nz = w_main.shape[1]
    return pl.pallas_call(
        functools.partial(_inproj_kernel, tile=tm),
        out_shape=(jax.ShapeDtypeStruct((S5_WIDTH // 128, T_TOK, 128), F32),
                   jax.ShapeDtypeStruct((T_TOK, nz - S5_WIDTH), F32),
                   jax.ShapeDtypeStruct((T_TOK, 2 * GLA_QK), F32)),
        grid=(T_TOK // tm,),
        in_specs=_token_specs(tm) + [
            pl.BlockSpec((1, D_MODEL), lambda i: (0, 0)),
            pl.BlockSpec((None, 1, 6 * D_MODEL), lambda i: (layer * COND_ROWS + _cond_row(i, tm), 0, 0)),
            pl.BlockSpec((D_MODEL, nz), lambda i: (0, 0)),
            pl.BlockSpec((D_MODEL, 128), lambda i: (0, 0)),
            pl.BlockSpec((128, 2 * GLA_QK), lambda i: (0, 0)),
            pl.BlockSpec((1, 2 * GLA_QK), lambda i: (0, 0)),
        ],
        out_specs=(pl.BlockSpec((S5_WIDTH // 128, tm, 128), lambda i: (0, i, 0)),
                   pl.BlockSpec((tm, nz - S5_WIDTH), lambda i: (i, 0)),
                   pl.BlockSpec((tm, 2 * GLA_QK), lambda i: (i, 0))),
        compiler_params=pltpu.CompilerParams(vmem_limit_bytes=VMEM_LIMIT),
        name="even_inproj",
    )(xp, xs, gn, mods, w_main, w_glr, w_gate, b_gate)


def _s5_prep_kernel(lre_ref, lim_ref, ldt_ref, btre_ref, btim_ref, cre_ref, cim_ref, ccf_ref, ccb_ref,
                    t_ref, bq_ref, cqt_ref, be_ref, a_ref, t_scr, dd_scr):
    lre = lre_ref[...]
    lim = lim_ref[...]
    dt = jnp.exp(ldt_ref[...])
    a = lre * dt
    th = lim * dt

    def lam_pow(k):
        mag = jnp.exp(k * a)
        return mag * jnp.cos(k * th), mag * jnp.sin(k * th)

    lb_re, lb_im = lam_pow(1.0)
    nr = lb_re - 1.0
    den = lre * lre + lim * lim
    cf_re = (nr * lre + lb_im * lim) / den
    cf_im = (lb_im * lre - nr * lim) / den
    bt_re = btre_ref[...]
    bt_im = btim_ref[...]
    bb_re = jnp.tile(cf_re * bt_re - cf_im * bt_im, (S5_Q, 1))
    bb_im = jnp.tile(cf_re * bt_im + cf_im * bt_re, (S5_Q, 1))

    shape = (S5_W, 128)
    pos = lax.shift_right_logical(lax.broadcasted_iota(jnp.int32, shape, 0), 4)
    is_f = lax.broadcasted_iota(jnp.int32, shape, 1) < S5_STATE
    posq = lax.broadcasted_iota(jnp.int32, (S5_Q, 128), 0).astype(F32)
    is_fq = lax.broadcasted_iota(jnp.int32, (S5_Q, 128), 1) < S5_STATE

    def per_channel(tbl):
        return jnp.broadcast_to(tbl[:, None, :], (S5_Q, S5_GROUP_CH, 128)).reshape(shape)

    p_re, p_im = map(per_channel, lam_pow(jnp.where(is_fq, (S5_Q - 1.0) - posq, posq)))
    w_re = p_re * bb_re - p_im * bb_im
    w_im = p_re * bb_im + p_im * bb_re
    bq = jnp.concatenate([w_re, w_im], axis=1)
    bqt = jnp.transpose(bq)
    bq_ref[...] = bqt.astype(BF16)

    edge = pos == jnp.where(is_f, 0, S5_Q - 1)
    be = jnp.concatenate([jnp.where(edge, bb_re, 0.0), jnp.where(edge, bb_im, 0.0)], axis=1)
    be_ref[...] = jnp.transpose(be).astype(BF16)

    q_re, q_im = map(per_channel, lam_pow(jnp.where(is_fq, posq + 1.0, S5_Q - posq)))
    ct_re = jnp.tile(cre_ref[...], (S5_Q, 1))
    ct_im = jnp.tile(cim_ref[...], (S5_Q, 1))
    g_re = q_re * ct_re - q_im * ct_im
    g_im = q_re * ct_im + q_im * ct_re
    cqt_ref[...] = jnp.concatenate([g_re, -g_im], axis=1).astype(BF16)

    a_re, a_im = lam_pow(float(S5_Q))
    a_ref[...] = jnp.concatenate([a_re, a_im], axis=1)

    kf = jnp.dot(ccf_ref[...], bqt, precision=lax.Precision.HIGHEST, preferred_element_type=F32)
    kb = jnp.dot(ccb_ref[...], bqt, precision=lax.Precision.HIGHEST, preferred_element_type=F32)
    gch = S5_GROUP_CH
    lo = S5_W - gch
    dd_scr[:, 0:S5_W] = kf
    dd_scr[:, lo:lo + S5_W] = kb
    dd_scr[:, lo:S5_W] = kf[:, lo:S5_W] + kb[:, 0:gch]
    for t in range(S5_Q):
        c0 = (S5_Q - 1 - t) * gch
        t_scr[t * gch:(t + 1) * gch, :] = dd_scr[:, c0:c0 + S5_W]
    t_ref[...] = t_scr[...].astype(BF16)


def _s5_prep_call(lam_re, lam_im, log_dt, b_re, b_im, c_re, c_im):
    def fb(p):
        return jnp.transpose(p, (1, 0, 2)).reshape(S5_GROUPS, 1, 2 * S5_STATE)

    def dup(p):
        return jnp.concatenate([p, p], axis=-1)

    ldt = fb(jnp.broadcast_to(log_dt[:, :, None], (2, S5_GROUPS, S5_STATE)))
    bt_re = dup(jnp.transpose(b_re, (0, 2, 1)))
    bt_im = dup(jnp.transpose(b_im, (0, 2, 1)))
    zero = jnp.zeros_like(c_re)
    ccf = jnp.concatenate([c_re, zero, -c_im, zero], axis=-1)
    ccb = jnp.concatenate([zero, c_re, zero, -c_im], axis=-1)

    row = pl.BlockSpec((None, 1, 128), lambda g: (g, 0, 0))
    mat16 = pl.BlockSpec((None, S5_GROUP_CH, 128), lambda g: (g, 0, 0))
    mat16w = pl.BlockSpec((None, S5_GROUP_CH, S5_W), lambda g: (g, 0, 0))
    sq = pl.BlockSpec((None, S5_W, S5_W), lambda g: (g, 0, 0))
    sq_shape = jax.ShapeDtypeStruct((S5_GROUPS, S5_W, S5_W), BF16)
    return pl.pallas_call(
        _s5_prep_kernel,
        out_shape=(sq_shape, sq_shape, sq_shape, sq_shape,
                   jax.ShapeDtypeStruct((S5_GROUPS, 1, S5_W), F32)),
        grid=(S5_GROUPS,),
        in_specs=[row, row, row, mat16, mat16, mat16, mat16, mat16w, mat16w],
        out_specs=(sq, sq, sq, sq, pl.BlockSpec((None, 1, S5_W), lambda g: (g, 0, 0))),
        scratch_shapes=[pltpu.VMEM((S5_W, S5_W), F32), pltpu.VMEM((S5_GROUP_CH, 2 * S5_W), F32)],
        name="s5_prep",
    )(fb(lam_re), fb(lam_im), ldt, bt_re, bt_im, dup(c_re), dup(c_im), ccf, ccb)


def _s5_kernel(u_ref, tt_ref, bqt_ref, cqt_ref, bet_ref, a_ref, h0_ref, y_ref, ns_ref,
               ut_scr, x_scr, spf_scr, spb_scr, ne_scr, yt_scr, xt_scr):
    gch = S5_GROUP_CH
    for s in range(S5_Q):
        rows = u_ref[pl.ds(s, S5_ROWS, stride=S5_Q), :]
        rows_t = jnp.transpose(rows).astype(BF16)
        for gl in range(S5_GPB):
            ut_scr[gl, s * gch:(s + 1) * gch, :] = rows_t[gl * gch:(gl + 1) * gch, :]

    for gl in range(S5_GPB):
        ut = ut_scr[gl]
        xt_scr[...] = _dot(bqt_ref[gl], ut)
        x = jnp.transpose(xt_scr[...])
        xt_scr[:, 0:S5_PROMPT_ROWS] = _dot(bet_ref[gl], ut[:, 0:S5_PROMPT_ROWS])
        ne = jnp.transpose(xt_scr[:, 0:S5_PROMPT_ROWS])
        for part in range(2):
            x_scr[part, pl.ds(gl, S5_ROWS, stride=S5_GPB), :] = x[:, part * 128:(part + 1) * 128]
            ne_scr[part, pl.ds(gl, S5_PROMPT_ROWS, stride=S5_GPB), :] = ne[:, part * 128:(part + 1) * 128]

    is_f = lax.broadcasted_iota(jnp.int32, (1, 128), 1) < S5_STATE
    a_re = a_ref[:, 0:128]
    a_im = a_ref[:, 128:256]

    def tile(row):
        return pl.ds(pl.multiple_of(row * S5_GPB, S5_GPB), S5_GPB)

    def scan(base, nseq, nchunk, s_init):
        def body(i, state):
            new = []
            for b in range(nseq):
                s_re, s_im = state[b]
                rows_f = tile(base + b * nchunk + i)
                rows_b = tile(base + b * nchunk + (nchunk - 1 - i))
                spf_scr[0, rows_f, :] = s_re
                spf_scr[1, rows_f, :] = s_im
                spb_scr[0, rows_b, :] = s_re
                spb_scr[1, rows_b, :] = s_im
                x_re = jnp.where(is_f, x_scr[0, rows_f, :], x_scr[0, rows_b, :])
                x_im = jnp.where(is_f, x_scr[1, rows_f, :], x_scr[1, rows_b, :])
                new.append((a_re * s_re - a_im * s_im + x_re, a_re * s_im + a_im * s_re + x_im))
            return tuple(new)

        lax.fori_loop(0, nchunk, body, tuple(s_init))

    zero = jnp.zeros((S5_GPB, 128), F32)
    scan(0, BATCH, S5_PROMPT_CHUNKS, [(zero, zero)] * BATCH)
    scan(S5_PROMPT_ROWS, DEC_BATCH, S5_SAMPLE_CHUNKS,
         [(h0_ref[b, :, 0:128], h0_ref[b, :, 128:256]) for b in range(DEC_BATCH)])

    for b in range(BATCH):
        first = pl.ds(b * S5_PROMPT_CHUNKS * S5_GPB, S5_GPB)
        last = pl.ds(((b + 1) * S5_PROMPT_CHUNKS - 1) * S5_GPB, S5_GPB)
        for part in range(2):
            ns_ref[b, :, part * 128:(part + 1) * 128] = jnp.where(is_f, ne_scr[part, first, :], ne_scr[part, last, :])

    for gl in range(S5_GPB):
        rows = pl.ds(gl, S5_ROWS, stride=S5_GPB)
        carried = jnp.concatenate([jnp.where(is_f, spf_scr[p, rows, :], spb_scr[p, rows, :]) for p in range(2)],
                                  axis=1).astype(BF16)
        yt = _dot(tt_ref[gl], ut_scr[gl]) + lax.dot_general(cqt_ref[gl], carried, NT_DIMS,
                                                            preferred_element_type=F32)
        for t in range(S5_Q):
            yt_scr[t, gl * gch:(gl + 1) * gch, :] = yt[t * gch:(t + 1) * gch, :]
    for t in range(S5_Q):
        y_ref[pl.ds(t, S5_ROWS, stride=S5_Q), :] = jnp.transpose(yt_scr[t])


def _s5_call(u, mats, h0):
    tt_m, bqt_m, cqt_m, bet_m, a_m = mats
    nsteps = S5_GROUPS // S5_GPB
    sq = pl.BlockSpec((S5_GPB, S5_W, S5_W), lambda g: (g, 0, 0))
    state_scr = pltpu.VMEM((2, S5_ROWS * S5_GPB, 128), F32)
    return pl.pallas_call(
        _s5_kernel,
        out_shape=(jax.ShapeDtypeStruct((nsteps, T_TOK, 128), F32),
                   jax.ShapeDtypeStruct((nsteps, BATCH, S5_GPB, S5_W), F32)),
        grid=(nsteps,),
        in_specs=[
            pl.BlockSpec((None, T_TOK, 128), lambda g: (g, 0, 0)),
            sq, sq, sq, sq,
            pl.BlockSpec((S5_GPB, S5_W), lambda g: (g, 0)),
            pl.BlockSpec((None, DEC_BATCH, S5_GPB, S5_W), lambda g: (g, 0, 0, 0)),
        ],
        out_specs=(pl.BlockSpec((None, T_TOK, 128), lambda g: (g, 0, 0)),
                   pl.BlockSpec((None, BATCH, S5_GPB, S5_W), lambda g: (g, 0, 0, 0))),
        scratch_shapes=[pltpu.VMEM((S5_GPB, S5_W, S5_ROWS), BF16), state_scr, state_scr, state_scr,
                        pltpu.VMEM((2, S5_PROMPT_ROWS * S5_GPB, 128), F32),
                        pltpu.VMEM((S5_Q, 128, S5_ROWS), F32), pltpu.VMEM((S5_W, S5_ROWS), F32)],
        compiler_params=pltpu.CompilerParams(vmem_limit_bytes=VMEM_LIMIT),
        name="s5_scan",
    )(u, tt_m, bqt_m, cqt_m, bet_m, a_m.reshape(S5_GROUPS, S5_W), h0)


def _split_bf16(x):
    hi = x.astype(BF16)
    r1 = x - hi.astype(F32)
    mid = r1.astype(BF16)
    lo = (r1 - mid.astype(F32)).astype(BF16)
    return hi, mid, lo


def _gla_kernel(*refs, seq_len, has_s0):
    q_ref, k_ref, v_ref, gf_ref, gb_ref, r_ref, gn_ref = refs[:7]
    s0_ref = refs[7] if has_s0 else None
    o_ref, sfin_ref, oi_scr, qd_scr, kv_scr, dec_scr, ss_scr = refs[7 + has_s0:]
    nblk = seq_len // GLA_BLK
    nchunk = seq_len // GLA_CHUNK
    cl = GLA_CHUNK
    ti = lax.broadcasted_iota(jnp.int32, (GLA_BLK, GLA_BLK), 0)
    si = lax.broadcasted_iota(jnp.int32, (GLA_BLK, GLA_BLK), 1)
    same = lax.shift_right_logical(ti, 6) == lax.shift_right_logical(si, 6)
    keep = (same & (ti >= si), same & (ti <= si))
    tri = tuple(kp.astype(BF16) for kp in keep)
    lane_head = lax.shift_right_logical(lax.broadcasted_iota(jnp.int32, (cl, GLA_QK), 1), 6)
    zeros_v = jnp.zeros((cl, GLA_DV), BF16)
    heads = [(slice(h * GLA_DK, (h + 1) * GLA_DK), slice(h * GLA_DV, (h + 1) * GLA_DV)) for h in range(GLA_HEADS)]

    for j in range(nblk):
        rows = slice(j * GLA_BLK, (j + 1) * GLA_BLK)
        q = q_ref[rows, :] * (GLA_DK ** -0.5)
        k = k_ref[rows, :]
        v = v_ref[rows, :].astype(BF16)
        qd, kd, k2t = [], [], []
        for d, g_ref in enumerate((gf_ref, gb_ref)):
            b = sum(_dot(tri[d], part) for part in _split_bf16(g_ref[rows, :]))
            last = cl - 1 if d == 0 else 0
            b_last = [b[c * cl + last:c * cl + last + 1] for c in range(GLA_CPB)]
            bl = jnp.concatenate([jnp.broadcast_to(x, (cl, GLA_QK)) for x in b_last], axis=0)
            qd_d = (q * jnp.exp(b)).astype(BF16)
            qd_scr[d, rows, :] = qd_d
            qd.append(qd_d)
            kd.append((k * jnp.exp(-b)).astype(BF16))
            k2t.append(jnp.transpose(k * jnp.exp(bl - b)).astype(BF16))
            for c in range(GLA_CPB):
                dec_scr[d, j * GLA_CPB + c] = jnp.exp(jnp.transpose(jnp.broadcast_to(b_last[c], (GLA_DV, GLA_QK))))
        for h, (ks, vs) in enumerate(heads):
            att = [jnp.where(keep[d], lax.dot_general(qd[d][:, ks], kd[d][:, ks], NT_DIMS,
                                                      preferred_element_type=F32), 0.0) for d in range(2)]
            oi_scr[rows, vs] = _dot((att[0] + att[1]).astype(BF16), v[:, vs])
            vh = v[:, vs]
            vexp = jnp.concatenate(
                [jnp.concatenate([vh[c * cl:(c + 1) * cl] if c2 == c else zeros_v for c2 in range(GLA_CPB)], axis=1)
                 for c in range(GLA_CPB)], axis=0)
            for d in range(2):
                kv_scr[d, j, h] = _dot(k2t[d][ks, :], vexp)

    for d in range(2):
        s = s0_ref[d] if has_s0 else jnp.zeros((GLA_QK, GLA_DV), F32)
        for cg in (range(nchunk) if d == 0 else range(nchunk - 1, -1, -1)):
            j, c = divmod(cg, GLA_CPB)
            ss_scr[d, cg] = s.astype(BF16)
            kv = jnp.concatenate([kv_scr[d, j, h, :, c * GLA_DV:(c + 1) * GLA_DV] for h in range(GLA_HEADS)], axis=0)
            s = s * dec_scr[d, cg] + kv
        sfin_ref[d] = s

    for cg in range(nchunk):
        rows = slice(cg * cl, (cg + 1) * cl)
        inter = []
        for d in range(2):
            qc = qd_scr[d, rows, :]
            qstack = jnp.concatenate([jnp.where(lane_head == h, qc, jnp.zeros_like(qc)) for h in range(GLA_HEADS)],
                                     axis=0)
            inter.append(_dot(qstack, ss_scr[d, cg]))
        gate = jax.nn.silu(r_ref[rows, :])
        for h, (ks, vs) in enumerate(heads):
            hr = slice(h * cl, (h + 1) * cl)
            oh = oi_scr[rows, vs] + inter[0][hr] + inter[1][hr]
            oh = oh * lax.rsqrt(jnp.mean(oh * oh, axis=-1, keepdims=True) + EPS) * gn_ref[...]
            o_ref[rows, vs] = oh * gate[:, vs]


def _gla_call(z, g, gla_norm, s0, seq_len, nseq, row0):
    assert row0 % seq_len == 0
    r0 = row0 // seq_len
    has_s0 = s0 is not None
    qk_off = 0
    v_off = 2 * GLA_QK // GLA_VW
    in_specs = [
        pl.BlockSpec((seq_len, GLA_QK), lambda i: (r0 + i, qk_off)),
        pl.BlockSpec((seq_len, GLA_QK), lambda i: (r0 + i, qk_off + 1)),
        pl.BlockSpec((seq_len, GLA_VW), lambda i: (r0 + i, v_off)),
        pl.BlockSpec((seq_len, GLA_QK), lambda i: (r0 + i, 0)),
        pl.BlockSpec((seq_len, GLA_QK), lambda i: (r0 + i, 1)),
        pl.BlockSpec((seq_len, GLA_VW), lambda i: (r0 + i, v_off + 1)),
        pl.BlockSpec((1, GLA_DV), lambda i: (0, 0)),
    ]
    args = [z, z, z, g, g, z, gla_norm]
    if has_s0:
        in_specs.append(pl.BlockSpec((None, 2, GLA_QK, GLA_DV), lambda i: (i, 0, 0, 0)))
        args.append(s0)
    return pl.pallas_call(
        functools.partial(_gla_kernel, seq_len=seq_len, has_s0=has_s0),
        out_shape=(jax.ShapeDtypeStruct((nseq * seq_len, GLA_VW), F32),
                   jax.ShapeDtypeStruct((nseq, 2, GLA_QK, GLA_DV), F32)),
        grid=(nseq,),
        in_specs=in_specs,
        out_specs=(pl.BlockSpec((seq_len, GLA_VW), lambda i: (i, 0)),
                   pl.BlockSpec((None, 2, GLA_QK, GLA_DV), lambda i: (i, 0, 0, 0))),
        scratch_shapes=[
            pltpu.VMEM((seq_len, GLA_VW), F32),
            pltpu.VMEM((2, seq_len, GLA_QK), BF16),
            pltpu.VMEM((2, seq_len // GLA_BLK, GLA_HEADS, GLA_DK, GLA_CPB * GLA_DV), F32),
            pltpu.VMEM((2, seq_len // GLA_CHUNK, GLA_QK, GLA_DV), F32),
            pltpu.VMEM((2, seq_len // GLA_CHUNK, GLA_QK, GLA_DV), BF16),
        ],
        compiler_params=pltpu.CompilerParams(vmem_limit_bytes=VMEM_LIMIT),
        name=f"gla_len{seq_len}",
    )(*args)


MLP_CHUNK = 512
MLP_LOAD = 256


class _MlpWeights:
    def __init__(self, w1_hbm, w2_hbm, w1_scr, w2_scr, stage1, stage2, sem, layer):
        self.refs = (w1_hbm, w2_hbm, w1_scr, w2_scr, stage1, stage2, sem)
        self.layer = layer

    def _copies(self, c):
        w1_hbm, w2_hbm, _, _, stage1, stage2, sem = self.refs
        cols = pl.ds(c * MLP_LOAD, MLP_LOAD)
        return (pltpu.make_async_copy(w1_hbm.at[self.layer, :, cols], stage1.at[c % 2], sem.at[0, c % 2]),
                pltpu.make_async_copy(w2_hbm.at[self.layer, cols, :], stage2.at[c % 2], sem.at[1, c % 2]))

    def start(self, c):
        for cp in self._copies(c):
            cp.start()

    def finish(self, c):
        _, _, w1_scr, w2_scr, stage1, stage2, _ = self.refs
        for cp in self._copies(c):
            cp.wait()
        cols = slice(c * MLP_LOAD, (c + 1) * MLP_LOAD)
        w1_scr[:, cols] = stage1[c % 2].astype(BF16)
        w2_scr[cols, :] = stage2[c % 2].astype(BF16)


def _mlp_tail(x, mix, m_ref, gn2_ref, w1_ref, w2_ref, loading=None):
    y1 = x + m_ref[:, 2 * D_MODEL:3 * D_MODEL] * mix
    h = _norm_mod(y1, gn2_ref[...], m_ref[:, 3 * D_MODEL:4 * D_MODEL], m_ref[:, 4 * D_MODEL:5 * D_MODEL]).astype(BF16)
    nchunk = D_FF // MLP_CHUNK
    acc = jnp.zeros(y1.shape, F32)
    for c in range(nchunk):
        cols = slice(c * MLP_CHUNK, (c + 1) * MLP_CHUNK)
        if loading is not None:
            per = MLP_CHUNK // MLP_LOAD
            for p in range(c * per, (c + 1) * per):
                if p + 1 < D_FF // MLP_LOAD:
                    loading.start(p + 1)
                loading.finish(p)
        a = _dot(h, w1_ref[:, cols])
        a = jnp.square(jnp.maximum(a, 0.0)).astype(BF16)
        acc = acc + _dot(a, w2_ref[cols, :])
    return y1 + m_ref[:, 5 * D_MODEL:6 * D_MODEL] * acc


def _run_tail(x, mix, m_ref, gn2_ref, weights, w1_ref, w2_ref, emit):
    first = pl.program_id(0) == 0

    @pl.when(first)
    def _():
        emit(_mlp_tail(x, mix, m_ref, gn2_ref, w1_ref, w2_ref, loading=weights))

    @pl.when(jnp.logical_not(first))
    def _():
        emit(_mlp_tail(x, mix, m_ref, gn2_ref, w1_ref, w2_ref))


def _even_out_kernel(xp_ref, xs_ref, y5_ref, u_ref, dskip_ref, wglu_ref, bglu_ref, glap_ref, glas_ref, wout_ref,
                     m_ref, gn2_ref, w1_hbm, w2_hbm, o_ref, w1_ref, w2_ref, stage1, stage2, sem, *, layer):
    weights = _MlpWeights(w1_hbm, w2_hbm, w1_ref, w2_ref, stage1, stage2, sem, layer)

    @pl.when(pl.program_id(0) == 0)
    def _():
        weights.start(0)

    nblk = S5_WIDTH // 128
    ys = (jnp.concatenate([y5_ref[b] for b in range(nblk)], axis=1)
          + jnp.concatenate([u_ref[b] for b in range(nblk)], axis=1) * dskip_ref[...])
    gl = jax.nn.gelu(ys)
    s5o = gl * jax.nn.sigmoid(_dot(gl.astype(BF16), wglu_ref[...]) + bglu_ref[...])
    gla = _token_tile(glap_ref, glas_ref, _OUT_TM).astype(BF16)
    mix = _dot(s5o.astype(BF16), wout_ref[0:S5_WIDTH, :]) + _dot(gla, wout_ref[S5_WIDTH:, :])

    def emit(y):
        o_ref[...] = y

    _run_tail(_token_tile(xp_ref, xs_ref, _OUT_TM), mix, m_ref, gn2_ref, weights, w1_ref, w2_ref, emit)


def _odd_out_kernel(x_ref, attp_ref, atts_ref, wo_ref, m_ref, gn2_ref, w1_hbm, w2_hbm, op_ref, os_ref,
                    w1_ref, w2_ref, stage1, stage2, sem, *, layer):
    weights = _MlpWeights(w1_hbm, w2_hbm, w1_ref, w2_ref, stage1, stage2, sem, layer)

    @pl.when(pl.program_id(0) == 0)
    def _():
        weights.start(0)

    mix = _dot(_token_tile(attp_ref, atts_ref, _OUT_TM), wo_ref[...])
    is_prompt = pl.program_id(0) < T_PROMPT // _OUT_TM

    def emit(y):
        @pl.when(is_prompt)
        def _():
            op_ref[...] = y

        @pl.when(jnp.logical_not(is_prompt))
        def _():
            os_ref[...] = y

    _run_tail(x_ref[...], mix, m_ref, gn2_ref, weights, w1_ref, w2_ref, emit)


_OUT_TM = 512


def _const_spec(shape):
    return pl.BlockSpec(shape, lambda i: (0,) * len(shape), pipeline_mode=pl.Buffered(1))


def _tail_specs(layer):
    tm = _OUT_TM
    return [
        pl.BlockSpec((None, 1, 6 * D_MODEL), lambda i: (layer * COND_ROWS + _cond_row(i, tm), 0, 0)),
        _const_spec((1, D_MODEL)),
        pl.BlockSpec(memory_space=pl.ANY),
        pl.BlockSpec(memory_space=pl.ANY),
    ]


def _tail_scratch():
    return [pltpu.VMEM((D_MODEL, D_FF), BF16), pltpu.VMEM((D_FF, D_MODEL), BF16),
            pltpu.VMEM((2, D_MODEL, MLP_LOAD), F32), pltpu.VMEM((2, MLP_LOAD, D_MODEL), F32),
            pltpu.SemaphoreType.DMA((2, 2))]


_TAIL_PARAMS = dict(dimension_semantics=("arbitrary",), vmem_limit_bytes=VMEM_LIMIT)


def _even_out_call(xp, xs, y5, u, d_skip, w_glu, b_glu, gla_p, gla_s, w_out, mods, layer, gn2, w1, w2):
    tm = _OUT_TM
    return pl.pallas_call(
        functools.partial(_even_out_kernel, layer=layer),
        out_shape=jax.ShapeDtypeStruct((T_TOK, D_MODEL), F32),
        grid=(T_TOK // tm,),
        in_specs=_token_specs(tm) + [
            pl.BlockSpec((S5_WIDTH // 128, tm, 128), lambda i: (0, i, 0)),
            pl.BlockSpec((S5_WIDTH // 128, tm, 128), lambda i: (0, i, 0)),
            _const_spec((1, S5_WIDTH)),
            _const_spec((S5_WIDTH, S5_WIDTH)),
            _const_spec((1, S5_WIDTH)),
        ] + _token_specs(tm, GLA_VW) + [
            _const_spec((S5_WIDTH + GLA_VW, D_MODEL)),
        ] + _tail_specs(layer),
        out_specs=pl.BlockSpec((tm, D_MODEL), lambda i: (i, 0)),
        scratch_shapes=_tail_scratch(),
        compiler_params=pltpu.CompilerParams(**_TAIL_PARAMS),
        name="even_out_mlp",
    )(xp, xs, y5, u, d_skip, w_glu, b_glu, gla_p, gla_s, w_out, mods, gn2, w1, w2)


def _odd_out_call(x, att_p, att_s, w_o, mods, layer, gn2, w1, w2):
    tm = _OUT_TM
    return pl.pallas_call(
        functools.partial(_odd_out_kernel, layer=layer),
        out_shape=(jax.ShapeDtypeStruct((T_PROMPT, D_MODEL), F32),
                   jax.ShapeDtypeStruct((T_SAMPLE, D_MODEL), F32)),
        grid=(T_TOK // tm,),
        in_specs=[pl.BlockSpec((tm, D_MODEL), lambda i: (i, 0))] + _token_specs(tm) + [
            _const_spec((D_MODEL, D_MODEL)),
        ] + _tail_specs(layer),
        out_specs=tuple(_token_specs(tm)),
        scratch_shapes=_tail_scratch(),
        compiler_params=pltpu.CompilerParams(**_TAIL_PARAMS),
        name="odd_out_mlp",
    )(x, att_p, att_s, w_o, mods, gn2, w1, w2)


def _qkv_kernel(x_ref, gn_ref, m_ref, w_ref, qn_ref, kn_ref, cos_ref, sin_ref,
                q_ref, kb_ref, vb_ref, k32_ref, v32_ref, *, tile):
    h = _norm_mod(x_ref[...], gn_ref[...], m_ref[:, 0:D_MODEL], m_ref[:, D_MODEL:2 * D_MODEL]).astype(BF16)
    z = _dot(h, w_ref[...])
    v = z[:, (N_HEADS + KV_HEADS) * HEAD_DIM:]
    vb_ref[...] = v.astype(BF16)
    even_lane = (lax.broadcasted_iota(jnp.int32, (1, HEAD_DIM), 1) & 1) == 0

    def heads(rope):
        for hd in range(N_HEADS + KV_HEADS):
            xh = z[:, hd * HEAD_DIM:(hd + 1) * HEAD_DIM]
            gain = qn_ref[...] if hd < N_HEADS else kn_ref[...]
            xh = xh * lax.rsqrt(jnp.mean(xh * xh, axis=-1, keepdims=True) + EPS) * gain
            if rope:
                partner = jnp.where(even_lane, pltpu.roll(xh, HEAD_DIM - 1, 1), pltpu.roll(xh, 1, 1))
                xh = xh * cos_ref[...] + partner * sin_ref[...]
            if hd < N_HEADS:
                q_ref[:, hd * HEAD_DIM:(hd + 1) * HEAD_DIM] = xh.astype(BF16)
            else:
                cols = slice((hd - N_HEADS) * HEAD_DIM, (hd - N_HEADS + 1) * HEAD_DIM)
                kb_ref[:, cols] = xh.astype(BF16)
                if not rope:
                    k32_ref[:, cols] = xh

    is_sample = pl.program_id(0) >= T_PROMPT // tile

    @pl.when(is_sample)
    def _():
        heads(True)

    @pl.when(jnp.logical_not(is_sample))
    def _():
        heads(False)
        v32_ref[...] = v


def _qkv_call(x, gn, mods, layer, w_qkv, q_norm, k_norm, cos_t, sin_t):
    tm = 512
    pos_tiles = DEC_SEQ // tm
    n_prompt = T_PROMPT // tm
    kvw = KV_HEADS * HEAD_DIM

    def pos_map(i):
        return (jnp.maximum(i - n_prompt, 0) % pos_tiles, 0)

    def prompt_map(i):
        return (jnp.minimum(i, n_prompt - 1), 0)

    return pl.pallas_call(
        functools.partial(_qkv_kernel, tile=tm),
        out_shape=(jax.ShapeDtypeStruct((T_TOK, N_HEADS * HEAD_DIM), BF16),
                   jax.ShapeDtypeStruct((T_TOK, kvw), BF16),
                   jax.ShapeDtypeStruct((T_TOK, kvw), BF16),
                   jax.ShapeDtypeStruct((T_PROMPT, kvw), F32),
                   jax.ShapeDtypeStruct((T_PROMPT, kvw), F32)),
        grid=(T_TOK // tm,),
        in_specs=[
            pl.BlockSpec((tm, D_MODEL), lambda i: (i, 0)),
            pl.BlockSpec((1, D_MODEL), lambda i: (0, 0)),
            pl.BlockSpec((None, 1, 6 * D_MODEL), lambda i: (layer * COND_ROWS + _cond_row(i, tm), 0, 0)),
            pl.BlockSpec(w_qkv.shape, lambda i: (0, 0)),
            pl.BlockSpec((1, HEAD_DIM), lambda i: (0, 0)),
            pl.BlockSpec((1, HEAD_DIM), lambda i: (0, 0)),
            pl.BlockSpec((tm, HEAD_DIM), pos_map),
            pl.BlockSpec((tm, HEAD_DIM), pos_map),
        ],
        out_specs=(pl.BlockSpec((tm, N_HEADS * HEAD_DIM), lambda i: (i, 0)),
                   pl.BlockSpec((tm, kvw), lambda i: (i, 0)),
                   pl.BlockSpec((tm, kvw), lambda i: (i, 0)),
                   pl.BlockSpec((tm, kvw), prompt_map),
                   pl.BlockSpec((tm, kvw), prompt_map)),
        compiler_params=pltpu.CompilerParams(vmem_limit_bytes=VMEM_LIMIT),
        name="odd_qkv",
    )(x, gn, mods, w_qkv, q_norm, k_norm, cos_t, sin_t)


def _rope_tables():
    rows = DEC_SEQ // GRID_W
    row = jnp.repeat(jnp.arange(rows, dtype=F32), GRID_W)
    col = jnp.tile(jnp.arange(GRID_W, dtype=F32), rows)
    inv = ROPE_THETA ** (-jnp.arange(0, AXIS_DIM, 2, dtype=F32) / AXIS_DIM)
    ang = jnp.concatenate([row[:, None] * inv, col[:, None] * inv], axis=-1)
    cos_t = jnp.repeat(jnp.cos(ang), 2, axis=-1)
    sin = jnp.sin(ang)
    sin_t = jnp.stack([-sin, sin], axis=-1).reshape(DEC_SEQ, HEAD_DIM)
    return cos_t, sin_t


def _attn_kernel(*refs, has_cache):
    q_ref, k_ref, v_ref = refs[:3]
    ck_ref, cv_ref = refs[3:5] if has_cache else (None, None)
    o_ref = refs[-1]
    c = HEAD_DIM ** -0.5 * math.log2(math.e)
    ones_col = (lax.broadcasted_iota(jnp.int32, (1, HEAD_DIM), 1) == 0).astype(BF16)

    def with_ones(v):
        return jnp.concatenate([v, jnp.broadcast_to(ones_col, v.shape)], axis=1)

    k = k_ref[...]
    v = with_ones(v_ref[...])
    if has_cache:
        ck = ck_ref[...].astype(BF16)
        cv = with_ones(cv_ref[...].astype(BF16))
    for r in range(Q_PER_KV):
        cs = slice(r * HEAD_DIM, (r + 1) * HEAD_DIM)
        q = q_ref[:, cs]
        s = lax.dot_general(q, k, NT_DIMS, preferred_element_type=F32)
        m = jnp.max(s, axis=-1, keepdims=True)
        if has_cache:
            sc = lax.dot_general(q, ck, NT_DIMS, preferred_element_type=F32)
            m = jnp.maximum(m, jnp.max(sc, axis=-1, keepdims=True))
        mc = m * c
        o = _dot(jnp.exp2(s * c - mc).astype(BF16), v)
        if has_cache:
            o = o + _dot(jnp.exp2(sc * c - mc).astype(BF16), cv)
        o_ref[:, cs] = (o[:, 0:HEAD_DIM] / o[:, HEAD_DIM:HEAD_DIM + 1]).astype(BF16)


def _attn_call(q, k, v, cache_k, cache_v, seq_len, row0, nrows):
    assert row0 % seq_len == 0 and nrows % seq_len == 0
    has_cache = cache_k is not None
    b0 = row0 // seq_len
    gw = Q_PER_KV * HEAD_DIM
    in_specs = [
        pl.BlockSpec((seq_len, gw), lambda b, g: (b0 + b, g)),
        pl.BlockSpec((seq_len, HEAD_DIM), lambda b, g: (b0 + b, g)),
        pl.BlockSpec((seq_len, HEAD_DIM), lambda b, g: (b0 + b, g)),
    ]
    args = [q, k, v]
    if has_cache:
        in_specs += [pl.BlockSpec((PAST_LEN, HEAD_DIM), lambda b, g: (b, g)),
                     pl.BlockSpec((PAST_LEN, HEAD_DIM), lambda b, g: (b, g))]
        args += [cache_k, cache_v]
    return pl.pallas_call(
        functools.partial(_attn_kernel, has_cache=has_cache),
        out_shape=jax.ShapeDtypeStruct((nrows, N_HEADS * HEAD_DIM), BF16),
        grid=(nrows // seq_len, KV_HEADS),
        in_specs=in_specs,
        out_specs=pl.BlockSpec((seq_len, gw), lambda b, g: (b, g)),
        compiler_params=pltpu.CompilerParams(vmem_limit_bytes=VMEM_LIMIT),
        name=f"attn_len{seq_len}",
    )(*args)


def kernel(x_prompt, x_sample, state_s5_re, state_s5_im, state_gla, cache_k, cache_v, c, c_ctx, norm_mix, norm_mlp, w_ada, b_ada, w_mlp_in, w_mlp_out, w_in_e, w_out_e, s5_lambda_re, s5_lambda_im, s5_log_dt, s5_b_re, s5_b_im, s5_c_re, s5_c_im, s5_d, s5_w_glu, s5_b_glu, gla_w_gate2, gla_b_gate, gla_norm, w_qkv_o, w_o_o, q_norm, k_norm):
    xp = x_prompt.reshape(T_PROMPT, D_MODEL)
    xs = x_sample.reshape(T_SAMPLE, D_MODEL)
    cond8 = jnp.concatenate([c_ctx[None, :], c, jnp.zeros((COND_ROWS - 1 - DEC_BATCH, D_MODEL), F32)], axis=0)
    mods = _ada_call(cond8, w_ada, b_ada).reshape(DEPTH * COND_ROWS, 1, 6 * D_MODEL)
    w1_all, w2_all = w_mlp_in, w_mlp_out

    n_main = S5_WIDTH + 2 * GLA_QK + 2 * GLA_VW
    w_in = w_in_e[0]
    w_main = w_in[:, :n_main].astype(BF16)
    w_glr = jnp.pad(w_in[:, n_main:], ((0, 0), (0, 128 - 2 * GLA_RANK))).astype(BF16)
    w_gate = jnp.zeros((128, 2 * GLA_QK), F32)
    w_gate = w_gate.at[0:GLA_RANK, 0:GLA_QK].set(gla_w_gate2[0, 0])
    w_gate = w_gate.at[GLA_RANK:2 * GLA_RANK, GLA_QK:].set(gla_w_gate2[0, 1]).astype(BF16)
    b_gate = gla_b_gate[0].reshape(1, 2 * GLA_QK)
    u, z, g = _inproj_call(xp, xs, norm_mix[0:1], mods, 0, w_main, w_glr, w_gate, b_gate)

    mats = _s5_prep_call(s5_lambda_re[0], s5_lambda_im[0], s5_log_dt[0], s5_b_re[0], s5_b_im[0],
                         s5_c_re[0], s5_c_im[0])

    def state_rows(s):
        return jnp.transpose(s, (2, 0, 1, 3)).reshape(S5_GROUPS, DEC_BATCH, 2 * S5_STATE)

    h0 = jnp.concatenate([state_rows(state_s5_re[:, 0]), state_rows(state_s5_im[:, 0])], axis=-1)
    nsteps = S5_GROUPS // S5_GPB
    h0 = jnp.transpose(h0.reshape(nsteps, S5_GPB, DEC_BATCH, S5_W), (0, 2, 1, 3))
    y5, ns = _s5_call(u, mats, h0)
    ns = jnp.transpose(ns, (0, 2, 1, 3)).reshape(S5_GROUPS, BATCH, S5_W)

    def state_out(n):
        return jnp.transpose(n.reshape(S5_GROUPS, BATCH, 2, S5_STATE), (1, 2, 0, 3))[:, None]

    new_s5_re = state_out(ns[:, :, :2 * S5_STATE])
    new_s5_im = state_out(ns[:, :, 2 * S5_STATE:])

    gn_gla = gla_norm[0].reshape(1, GLA_DV)
    gla_p, sfin = _gla_call(z, g, gn_gla, None, SEQ, BATCH, 0)
    s0 = state_gla[:, 0].reshape(DEC_BATCH, 2, GLA_QK, GLA_DV)
    gla_s, _ = _gla_call(z, g, gn_gla, s0, DEC_SEQ, DEC_BATCH, T_PROMPT)
    new_gla = sfin.reshape(BATCH, 1, 2, GLA_HEADS, GLA_DK, GLA_DV)

    x = _even_out_call(xp, xs, y5, u, s5_d[0].reshape(1, S5_WIDTH), s5_w_glu[0].astype(BF16),
                       s5_b_glu[0].reshape(1, S5_WIDTH), gla_p, gla_s, w_out_e[0].astype(BF16), mods, 0,
                       norm_mlp[0:1], w1_all, w2_all)

    cos_t, sin_t = _rope_tables()
    q, k, v, k32, v32 = _qkv_call(x, norm_mix[1:2], mods, 1, w_qkv_o[0].astype(BF16),
                                  q_norm[0].reshape(1, HEAD_DIM), k_norm[0].reshape(1, HEAD_DIM), cos_t, sin_t)
    att_p = _attn_call(q, k, v, None, None, SEQ, 0, T_PROMPT)
    ck = cache_k[:, 0].reshape(DEC_BATCH * PAST_LEN, KV_HEADS * HEAD_DIM)
    cv = cache_v[:, 0].reshape(DEC_BATCH * PAST_LEN, KV_HEADS * HEAD_DIM)
    att_s = _attn_call(q, k, v, ck, cv, DEC_SEQ, T_PROMPT, T_SAMPLE)
    yp, ys = _odd_out_call(x, att_p, att_s, w_o_o[0].astype(BF16), mods, 1, norm_mlp[1:2],
                           w1_all, w2_all)

    new_k = k32.reshape(BATCH, 1, SEQ, KV_HEADS, HEAD_DIM)
    new_v = v32.reshape(BATCH, 1, SEQ, KV_HEADS, HEAD_DIM)
    y_prompt = yp.reshape(BATCH, SEQ, D_MODEL)
    y_sample = ys.reshape(DEC_BATCH, DEC_SEQ, D_MODEL)
    return (y_prompt, y_sample, new_s5_re, new_s5_im, new_gla, new_k, new_v)
```

```python
import functools
import math

import jax
import jax.numpy as jnp
import numpy as np
from jax import lax
from jax.experimental import pallas as pl
from jax.experimental.pallas import tpu as pltpu

F32 = jnp.float32
BF16 = jnp.bfloat16

D_MODEL = 1024
BATCH = 16
SEQ = 256
DEPTH = 2
DEC_BATCH = 4
DEC_SEQ = 1024
PAST_LEN = 512
GRID_W = 64
S5_WIDTH = 512
S5_GROUP_CH = 16
S5_GROUPS = 32
S5_STATE = 64
GLA_HEADS = 4
GLA_VW = 512
GLA_DV = 128
GLA_DK = 64
GLA_QK = 256
GLA_RANK = 16
GLA_TAU = 16.0
GLA_CHUNK = 64
GLA_CPB = 4
GLA_BLK = GLA_CPB * GLA_CHUNK
HEAD_DIM = 128
N_HEADS = 8
KV_HEADS = 2
Q_PER_KV = N_HEADS // KV_HEADS
AXIS_DIM = 64
ROPE_THETA = 10000.0
D_FF = 4096
EPS = 1e-6

T_PROMPT = BATCH * SEQ
T_SAMPLE = DEC_BATCH * DEC_SEQ
T_TOK = T_PROMPT + T_SAMPLE
COND_ROWS = 8
COND_SPAN = 1024
PROMPT_SPANS = T_PROMPT // COND_SPAN

S5_Q = 16
S5_W = S5_Q * S5_GROUP_CH
S5_GPB = 128 // S5_GROUP_CH
S5_ROWS = T_TOK // S5_Q
S5_PROMPT_ROWS = T_PROMPT // S5_Q
S5_PROMPT_CHUNKS = SEQ // S5_Q
S5_SAMPLE_CHUNKS = DEC_SEQ // S5_Q

VMEM_LIMIT = 56 * 1024 * 1024

NT_DIMS = (((1,), (1,)), ((), ()))
TN_DIMS = (((0,), (0,)), ((), ()))


def _cond_row(i, tile):
    return jnp.maximum((i * tile) // COND_SPAN - (PROMPT_SPANS - 1), 0)


def _norm_mod(x, gain, shift, scale):
    y = x * lax.rsqrt(jnp.mean(x * x, axis=-1, keepdims=True) + EPS)
    return (y * gain) * (1.0 + scale) + shift


def _dot(a, b):
    return jnp.dot(a, b, preferred_element_type=F32)


def _ada_kernel(cond_ref, w_ref, b_ref, o_ref):
    s = jax.nn.silu(cond_ref[...]).astype(BF16)
    o_ref[...] = _dot(s, w_ref[...].astype(BF16)) + b_ref[...]


def _ada_call(cond8, w_ada, b_ada):
    tn = 2048
    nj = 6 * D_MODEL // tn
    return pl.pallas_call(
        _ada_kernel,
        out_shape=jax.ShapeDtypeStruct((DEPTH, COND_ROWS, 6 * D_MODEL), F32),
        grid=(DEPTH, nj),
        in_specs=[
            pl.BlockSpec((COND_ROWS, D_MODEL), lambda l, j: (0, 0)),
            pl.BlockSpec((None, D_MODEL, tn), lambda l, j: (l, 0, j)),
            pl.BlockSpec((None, 1, tn), lambda l, j: (l, 0, j)),
        ],
        out_specs=pl.BlockSpec((None, COND_ROWS, tn), lambda l, j: (l, 0, j)),
        compiler_params=pltpu.CompilerParams(vmem_limit_bytes=VMEM_LIMIT),
        name="ada_mod",
    )(cond8, w_ada, b_ada.reshape(DEPTH, 1, 6 * D_MODEL))


def _token_specs(tile, width=D_MODEL):
    n_prompt = T_PROMPT // tile
    return [pl.BlockSpec((tile, width), lambda i: (jnp.minimum(i, n_prompt - 1), 0)),
            pl.BlockSpec((tile, width), lambda i: (jnp.maximum(i - n_prompt, 0), 0))]


def _token_tile(xp_ref, xs_ref, tile):
    return jnp.where(pl.program_id(0) < T_PROMPT // tile, xp_ref[...], xs_ref[...])


def _inproj_kernel(xp_ref, xs_ref, gn_ref, m_ref, w_ref, wglr_ref, wg_ref, bg_ref, u_ref, z_ref, g_ref, *, tile):
    x = _token_tile(xp_ref, xs_ref, tile)
    h = _norm_mod(x, gn_ref[...], m_ref[:, 0:D_MODEL], m_ref[:, D_MODEL:2 * D_MODEL]).astype(BF16)
    z = _dot(h, w_ref[...])
    for blk in range(S5_WIDTH // 128):
        u_ref[blk] = z[:, blk * 128:(blk + 1) * 128]
    z_ref[...] = z[:, S5_WIDTH:]
    glr = _dot(h, wglr_ref[...]).astype(BF16)
    pre = _dot(glr, wg_ref[...]) + bg_ref[...]
    g_ref[...] = jax.nn.log_sigmoid(pre) * (1.0 / GLA_TAU)


def _inproj_call(xp, xs, gn, mods, layer, w_main, w_glr, w_gate, b_gate):
    tm = 512
    nz = w_main.shape[1]
    return pl.pallas_call(
        functools.partial(_inproj_kernel, tile=tm),
        out_shape=(jax.ShapeDtypeStruct((S5_WIDTH // 128, T_TOK, 128), F32),
                   jax.ShapeDtypeStruct((T_TOK, nz - S5_WIDTH), F32),
                   jax.ShapeDtypeStruct((T_TOK, 2 * GLA_QK), F32)),
        grid=(T_TOK // tm,),
        in_specs=_token_specs(tm) + [
            pl.BlockSpec((1, D_MODEL), lambda i: (0, 0)),
            pl.BlockSpec((None, 1, 6 * D_MODEL), lambda i: (layer * COND_ROWS + _cond_row(i, tm), 0, 0)),
            pl.BlockSpec((D_MODEL, nz), lambda i: (0, 0)),
            pl.BlockSpec((D_MODEL, 128), lambda i: (0, 0)),
            pl.BlockSpec((128, 2 * GLA_QK), lambda i: (0, 0)),
            pl.BlockSpec((1, 2 * GLA_QK), lambda i: (0, 0)),
        ],
        out_specs=(pl.BlockSpec((S5_WIDTH // 128, tm, 128), lambda i: (0, i, 0)),
                   pl.BlockSpec((tm, nz - S5_WIDTH), lambda i: (i, 0)),
                   pl.BlockSpec((tm, 2 * GLA_QK), lambda i: (i, 0))),
        compiler_params=pltpu.CompilerParams(vmem_limit_bytes=VMEM_LIMIT),
        name="even_inproj",
    )(xp, xs, gn, mods, w_main, w_glr, w_gate, b_gate)


def _s5_prep_kernel(lre_ref, lim_ref, ldt_ref, btre_ref, btim_ref, cre_ref, cim_ref, ccf_ref, ccb_ref,
                    t_ref, bq_ref, cqt_ref, be_ref, a_ref, t_scr, dd_scr):
    lre = lre_ref[...]
    lim = lim_ref[...]
    dt = jnp.exp(ldt_ref[...])
    a = lre * dt
    th = lim * dt

    def lam_pow(k):
        mag = jnp.exp(k * a)
        return mag * jnp.cos(k * th), mag * jnp.sin(k * th)

    lb_re, lb_im = lam_pow(1.0)
    nr = lb_re - 1.0
    den = lre * lre + lim * lim
    cf_re = (nr * lre + lb_im * lim) / den
    cf_im = (lb_im * lre - nr * lim) / den
    bt_re = btre_ref[...]
    bt_im = btim_ref[...]
    bb_re = jnp.tile(cf_re * bt_re - cf_im * bt_im, (S5_Q, 1))
    bb_im = jnp.tile(cf_re * bt_im + cf_im * bt_re, (S5_Q, 1))

    shape = (S5_W, 128)
    pos = lax.shift_right_logical(lax.broadcasted_iota(jnp.int32, shape, 0), 4)
    is_f = lax.broadcasted_iota(jnp.int32, shape, 1) < S5_STATE
    posq = lax.broadcasted_iota(jnp.int32, (S5_Q, 128), 0).astype(F32)
    is_fq = lax.broadcasted_iota(jnp.int32, (S5_Q, 128), 1) < S5_STATE

    def per_channel(tbl):
        return jnp.broadcast_to(tbl[:, None, :], (S5_Q, S5_GROUP_CH, 128)).reshape(shape)

    p_re, p_im = map(per_channel, lam_pow(jnp.where(is_fq, (S5_Q - 1.0) - posq, posq)))
    w_re = p_re * bb_re - p_im * bb_im
    w_im = p_re * bb_im + p_im * bb_re
    bq = jnp.concatenate([w_re, w_im], axis=1)
    bqt = jnp.transpose(bq)
    bq_ref[...] = bqt.astype(BF16)

    edge = pos == jnp.where(is_f, 0, S5_Q - 1)
    be = jnp.concatenate([jnp.where(edge, bb_re, 0.0), jnp.where(edge, bb_im, 0.0)], axis=1)
    be_ref[...] = jnp.transpose(be).astype(BF16)

    q_re, q_im = map(per_channel, lam_pow(jnp.where(is_fq, posq + 1.0, S5_Q - posq)))
    ct_re = jnp.tile(cre_ref[...], (S5_Q, 1))
    ct_im = jnp.tile(cim_ref[...], (S5_Q, 1))
    g_re = q_re * ct_re - q_im * ct_im
    g_im = q_re * ct_im + q_im * ct_re
    cqt_ref[...] = jnp.concatenate([g_re, -g_im], axis=1).astype(BF16)

    a_re, a_im = lam_pow(float(S5_Q))
    a_ref[...] = jnp.concatenate([a_re, a_im], axis=1)

    kf = jnp.dot(ccf_ref[...], bqt, precision=lax.Precision.HIGHEST, preferred_element_type=F32)
    kb = jnp.dot(ccb_ref[...], bqt, precision=lax.Precision.HIGHEST, preferred_element_type=F32)
    gch = S5_GROUP_CH
    lo = S5_W - gch
    dd_scr[:, 0:S5_W] = kf
    dd_scr[:, lo:lo + S5_W] = kb
    dd_scr[:, lo:S5_W] = kf[:, lo:S5_W] + kb[:, 0:gch]
    for t in range(S5_Q):
        c0 = (S5_Q - 1 - t) * gch
        t_scr[t * gch:(t + 1) * gch, :] = dd_scr[:, c0:c0 + S5_W]
    t_ref[...] = t_scr[...].astype(BF16)


def _s5_prep_call(lam_re, lam_im, log_dt, b_re, b_im, c_re, c_im):
    def fb(p):
        return jnp.transpose(p, (1, 0, 2)).reshape(S5_GROUPS, 1, 2 * S5_STATE)

    def dup(p):
        return jnp.concatenate([p, p], axis=-1)

    ldt = fb(jnp.broadcast_to(log_dt[:, :, None], (2, S5_GROUPS, S5_STATE)))
    bt_re = dup(jnp.transpose(b_re, (0, 2, 1)))
    bt_im = dup(jnp.transpose(b_im, (0, 2, 1)))
    zero = jnp.zeros_like(c_re)
    ccf = jnp.concatenate([c_re, zero, -c_im, zero], axis=-1)
    ccb = jnp.concatenate([zero, c_re, zero, -c_im], axis=-1)

    row = pl.BlockSpec((None, 1, 128), lambda g: (g, 0, 0))
    mat16 = pl.BlockSpec((None, S5_GROUP_CH, 128), lambda g: (g, 0, 0))
    mat16w = pl.BlockSpec((None, S5_GROUP_CH, S5_W), lambda g: (g, 0, 0))
    sq = pl.BlockSpec((None, S5_W, S5_W), lambda g: (g, 0, 0))
    sq_shape = jax.ShapeDtypeStruct((S5_GROUPS, S5_W, S5_W), BF16)
    return pl.pallas_call(
        _s5_prep_kernel,
        out_shape=(sq_shape, sq_shape, sq_shape, sq_shape,
                   jax.ShapeDtypeStruct((S5_GROUPS, 1, S5_W), F32)),
        grid=(S5_GROUPS,),
        in_specs=[row, row, row, mat16, mat16, mat16, mat16, mat16w, mat16w],
        out_specs=(sq, sq, sq, sq, pl.BlockSpec((None, 1, S5_W), lambda g: (g, 0, 0))),
        scratch_shapes=[pltpu.VMEM((S5_W, S5_W), F32), pltpu.VMEM((S5_GROUP_CH, 2 * S5_W), F32)],
        name="s5_prep",
    )(fb(lam_re), fb(lam_im), ldt, bt_re, bt_im, dup(c_re), dup(c_im), ccf, ccb)


def _s5_kernel(u_ref, tt_ref, bqt_ref, cqt_ref, bet_ref, a_ref, h0_ref, y_ref, ns_ref,
               ut_scr, x_scr, spf_scr, spb_scr, ne_scr, yt_scr, xt_scr):
    gch = S5_GROUP_CH
    for s in range(S5_Q):
        rows = u_ref[pl.ds(s, S5_ROWS, stride=S5_Q), :]
        rows_t = jnp.transpose(rows).astype(BF16)
        for gl in range(S5_GPB):
            ut_scr[gl, s * gch:(s + 1) * gch, :] = rows_t[gl * gch:(gl + 1) * gch, :]

    for gl in range(S5_GPB):
        ut = ut_scr[gl]
        xt_scr[...] = _dot(bqt_ref[gl], ut)
        x = jnp.transpose(xt_scr[...])
        xt_scr[:, 0:S5_PROMPT_ROWS] = _dot(bet_ref[gl], ut[:, 0:S5_PROMPT_ROWS])
        ne = jnp.transpose(xt_scr[:, 0:S5_PROMPT_ROWS])
        for part in range(2):
            x_scr[part, pl.ds(gl, S5_ROWS, stride=S5_GPB), :] = x[:, part * 128:(part + 1) * 128]
            ne_scr[part, pl.ds(gl, S5_PROMPT_ROWS, stride=S5_GPB), :] = ne[:, part * 128:(part + 1) * 128]

    is_f = lax.broadcasted_iota(jnp.int32, (1, 128), 1) < S5_STATE
    a_re = a_ref[:, 0:128]
    a_im = a_ref[:, 128:256]

    def tile(row):
        return pl.ds(pl.multiple_of(row * S5_GPB, S5_GPB), S5_GPB)

    def scan(base, nseq, nchunk, s_init):
        def body(i, state):
            new = []
            for b in range(nseq):
                s_re, s_im = state[b]
                rows_f = tile(base + b * nchunk + i)
                rows_b = tile(base + b * nchunk + (nchunk - 1 - i))
                spf_scr[0, rows_f, :] = s_re
                spf_scr[1, rows_f, :] = s_im
                spb_scr[0, rows_b, :] = s_re
                spb_scr[1, rows_b, :] = s_im
                x_re = jnp.where(is_f, x_scr[0, rows_f, :], x_scr[0, rows_b, :])
                x_im = jnp.where(is_f, x_scr[1, rows_f, :], x_scr[1, rows_b, :])
                new.append((a_re * s_re - a_im * s_im + x_re, a_re * s_im + a_im * s_re + x_im))
            return tuple(new)

        lax.fori_loop(0, nchunk, body, tuple(s_init))

    zero = jnp.zeros((S5_GPB, 128), F32)
    scan(0, BATCH, S5_PROMPT_CHUNKS, [(zero, zero)] * BATCH)
    scan(S5_PROMPT_ROWS, DEC_BATCH, S5_SAMPLE_CHUNKS,
         [(h0_ref[b, :, 0:128], h0_ref[b, :, 128:256]) for b in range(DEC_BATCH)])

    for b in range(BATCH):
        first = pl.ds(b * S5_PROMPT_CHUNKS * S5_GPB, S5_GPB)
        last = pl.ds(((b + 1) * S5_PROMPT_CHUNKS - 1) * S5_GPB, S5_GPB)
        for part in range(2):
            ns_ref[b, :, part * 128:(part + 1) * 128] = jnp.where(is_f, ne_scr[part, first, :], ne_scr[part, last, :])

    for gl in range(S5_GPB):
        rows = pl.ds(gl, S5_ROWS, stride=S5_GPB)
        carried = jnp.concatenate([jnp.where(is_f, spf_scr[p, rows, :], spb_scr[p, rows, :]) for p in range(2)],
                                  axis=1).astype(BF16)
        yt = _dot(tt_ref[gl], ut_scr[gl]) + lax.dot_general(cqt_ref[gl], carried, NT_DIMS,
                                                            preferred_element_type=F32)
        for t in range(S5_Q):
            yt_scr[t, gl * gch:(gl + 1) * gch, :] = yt[t * gch:(t + 1) * gch, :]
    for t in range(S5_Q):
        y_ref[pl.ds(t, S5_ROWS, stride=S5_Q), :] = jnp.transpose(yt_scr[t])


def _s5_call(u, mats, h0):
    tt_m, bqt_m, cqt_m, bet_m, a_m = mats
    nsteps = S5_GROUPS // S5_GPB
    sq = pl.BlockSpec((S5_GPB, S5_W, S5_W), lambda g: (g, 0, 0))
    state_scr = pltpu.VMEM((2, S5_ROWS * S5_GPB, 128), F32)
    return pl.pallas_call(
        _s5_kernel,
        out_shape=(jax.ShapeDtypeStruct((nsteps, T_TOK, 128), F32),
                   jax.ShapeDtypeStruct((nsteps, BATCH, S5_GPB, S5_W), F32)),
        grid=(nsteps,),
        in_specs=[
            pl.BlockSpec((None, T_TOK, 128), lambda g: (g, 0, 0)),
            sq, sq, sq, sq,
            pl.BlockSpec((S5_GPB, S5_W), lambda g: (g, 0)),
            pl.BlockSpec((None, DEC_BATCH, S5_GPB, S5_W), lambda g: (g, 0, 0, 0)),
        ],
        out_specs=(pl.BlockSpec((None, T_TOK, 128), lambda g: (g, 0, 0)),
                   pl.BlockSpec((None, BATCH, S5_GPB, S5_W), lambda g: (g, 0, 0, 0))),
        scratch_shapes=[pltpu.VMEM((S5_GPB, S5_W, S5_ROWS), BF16), state_scr, state_scr, state_scr,
                        pltpu.VMEM((2, S5_PROMPT_ROWS * S5_GPB, 128), F32),
                        pltpu.VMEM((S5_Q, 128, S5_ROWS), F32), pltpu.VMEM((S5_W, S5_ROWS), F32)],
        compiler_params=pltpu.CompilerParams(vmem_limit_bytes=VMEM_LIMIT),
        name="s5_scan",
    )(u, tt_m, bqt_m, cqt_m, bet_m, a_m.reshape(S5_GROUPS, S5_W), h0)


def _split_bf16(x):
    hi = x.astype(BF16)
    r1 = x - hi.astype(F32)
    mid = r1.astype(BF16)
    lo = (r1 - mid.astype(F32)).astype(BF16)
    return hi, mid, lo


def _gla_kernel(*refs, seq_len, has_s0):
    q_ref, k_ref, v_ref, gf_ref, gb_ref, r_ref, gn_ref = refs[:7]
    s0_ref = refs[7] if has_s0 else None
    o_ref, sfin_ref, oi_scr, qd_scr, kv_scr, dec_scr, ss_scr = refs[7 + has_s0:]
    nblk = seq_len // GLA_BLK
    nchunk = seq_len // GLA_CHUNK
    cl = GLA_CHUNK
    ti = lax.broadcasted_iota(jnp.int32, (GLA_BLK, GLA_BLK), 0)
    si = lax.broadcasted_iota(jnp.int32, (GLA_BLK, GLA_BLK), 1)
    same = lax.shift_right_logical(ti, 6) == lax.shift_right_logical(si, 6)
    keep = (same & (ti >= si), same & (ti <= si))
    tri = tuple(kp.astype(BF16) for kp in keep)
    lane_head = lax.shift_right_logical(lax.broadcasted_iota(jnp.int32, (cl, GLA_QK), 1), 6)
    zeros_v = jnp.zeros((cl, GLA_DV), BF16)
    heads = [(slice(h * GLA_DK, (h + 1) * GLA_DK), slice(h * GLA_DV, (h + 1) * GLA_DV)) for h in range(GLA_HEADS)]

    for j in range(nblk):
        rows = slice(j * GLA_BLK, (j + 1) * GLA_BLK)
        q = q_ref[rows, :] * (GLA_DK ** -0.5)
        k = k_ref[rows, :]
        v = v_ref[rows, :].astype(BF16)
        qd, kd, k2t = [], [], []
        for d, g_ref in enumerate((gf_ref, gb_ref)):
            b = sum(_dot(tri[d], part) for part in _split_bf16(g_ref[rows, :]))
            last = cl - 1 if d == 0 else 0
            b_last = [b[c * cl + last:c * cl + last + 1] for c in range(GLA_CPB)]
            bl = jnp.concatenate([jnp.broadcast_to(x, (cl, GLA_QK)) for x in b_last], axis=0)
            qd_d = (q * jnp.exp(b)).astype(BF16)
            qd_scr[d, rows, :] = qd_d
            qd.append(qd_d)
            kd.append((k * jnp.exp(-b)).astype(BF16))
            k2t.append(jnp.transpose(k * jnp.exp(bl - b)).astype(BF16))
            for c in range(GLA_CPB):
                dec_scr[d, j * GLA_CPB + c] = jnp.exp(jnp.transpose(jnp.broadcast_to(b_last[c], (GLA_DV, GLA_QK))))
        for h, (ks, vs) in enumerate(heads):
            att = [jnp.where(keep[d], lax.dot_general(qd[d][:, ks], kd[d][:, ks], NT_DIMS,
                                                      preferred_element_type=F32), 0.0) for d in range(2)]
            oi_scr[rows, vs] = _dot((att[0] + att[1]).astype(BF16), v[:, vs])
            vh = v[:, vs]
            vexp = jnp.concatenate(
                [jnp.concatenate([vh[c * cl:(c + 1) * cl] if c2 == c else zeros_v for c2 in range(GLA_CPB)], axis=1)
                 for c in range(GLA_CPB)], axis=0)
            for d in range(2):
                kv_scr[d, j, h] = _dot(k2t[d][ks, :], vexp)

    for d in range(2):
        s = s0_ref[d] if has_s0 else jnp.zeros((GLA_QK, GLA_DV), F32)
        for cg in (range(nchunk) if d == 0 else range(nchunk - 1, -1, -1)):
            j, c = divmod(cg, GLA_CPB)
            ss_scr[d, cg] = s.astype(BF16)
            kv = jnp.concatenate([kv_scr[d, j, h, :, c * GLA_DV:(c + 1) * GLA_DV] for h in range(GLA_HEADS)], axis=0)
            s = s * dec_scr[d, cg] + kv
        sfin_ref[d] = s

    for cg in range(nchunk):
        rows = slice(cg * cl, (cg + 1) * cl)
        inter = []
        for d in range(2):
            qc = qd_scr[d, rows, :]
            qstack = jnp.concatenate([jnp.where(lane_head == h, qc, jnp.zeros_like(qc)) for h in range(GLA_HEADS)],
                                     axis=0)
            inter.append(_dot(qstack, ss_scr[d, cg]))
        gate = jax.nn.silu(r_ref[rows, :])
        for h, (ks, vs) in enumerate(heads):
            hr = slice(h * cl, (h + 1) * cl)
            oh = oi_scr[rows, vs] + inter[0][hr] + inter[1][hr]
            oh = oh * lax.rsqrt(jnp.mean(oh * oh, axis=-1, keepdims=True) + EPS) * gn_ref[...]
            o_ref[rows, vs] = oh * gate[:, vs]


def _gla_call(z, g, gla_norm, s0, seq_len, nseq, row0):
    assert row0 % seq_len == 0
    r0 = row0 // seq_len
    has_s0 = s0 is not None
    qk_off = 0
    v_off = 2 * GLA_QK // GLA_VW
    in_specs = [
        pl.BlockSpec((seq_len, GLA_QK), lambda i: (r0 + i, qk_off)),
        pl.BlockSpec((seq_len, GLA_QK), lambda i: (r0 + i, qk_off + 1)),
        pl.BlockSpec((seq_len, GLA_VW), lambda i: (r0 + i, v_off)),
        pl.BlockSpec((seq_len, GLA_QK), lambda i: (r0 + i, 0)),
        pl.BlockSpec((seq_len, GLA_QK), lambda i: (r0 + i, 1)),
        pl.BlockSpec((seq_len, GLA_VW), lambda i: (r0 + i, v_off + 1)),
        pl.BlockSpec((1, GLA_DV), lambda i: (0, 0)),
    ]
    args = [z, z, z, g, g, z, gla_norm]
    if has_s0:
        in_specs.append(pl.BlockSpec((None, 2, GLA_QK, GLA_DV), lambda i: (i, 0, 0, 0)))
        args.append(s0)
    return pl.pallas_call(
        functools.partial(_gla_kernel, seq_len=seq_len, has_s0=has_s0),
        out_shape=(jax.ShapeDtypeStruct((nseq * seq_len, GLA_VW), F32),
                   jax.ShapeDtypeStruct((nseq, 2, GLA_QK, GLA_DV), F32)),
        grid=(nseq,),
        in_specs=in_specs,
        out_specs=(pl.BlockSpec((seq_len, GLA_VW), lambda i: (i, 0)),
                   pl.BlockSpec((None, 2, GLA_QK, GLA_DV), lambda i: (i, 0, 0, 0))),
        scratch_shapes=[
            pltpu.VMEM((seq_len, GLA_VW), F32),
            pltpu.VMEM((2, seq_len, GLA_QK), BF16),
            pltpu.VMEM((2, seq_len // GLA_BLK, GLA_HEADS, GLA_DK, GLA_CPB * GLA_DV), F32),
            pltpu.VMEM((2, seq_len // GLA_CHUNK, GLA_QK, GLA_DV), F32),
            pltpu.VMEM((2, seq_len // GLA_CHUNK, GLA_QK, GLA_DV), BF16),
        ],
        compiler_params=pltpu.CompilerParams(vmem_limit_bytes=VMEM_LIMIT),
        name=f"gla_len{seq_len}",
    )(*args)


MLP_CHUNK = 512
MLP_LOAD = 256


class _MlpWeights:
    def __init__(self, w1_hbm, w2_hbm, w1_scr, w2_scr, stage1, stage2, sem, layer):
        self.refs = (w1_hbm, w2_hbm, w1_scr, w2_scr, stage1, stage2, sem)
        self.layer = layer

    def _copies(self, c):
        w1_hbm, w2_hbm, _, _, stage1, stage2, sem = self.refs
        cols = pl.ds(c * MLP_LOAD, MLP_LOAD)
        return (pltpu.make_async_copy(w1_hbm.at[self.layer, :, cols], stage1.at[c % 2], sem.at[0, c % 2]),
                pltpu.make_async_copy(w2_hbm.at[self.layer, cols, :], stage2.at[c % 2], sem.at[1, c % 2]))

    def start(self, c):
        for cp in self._copies(c):
            cp.start()

    def finish(self, c):
        _, _, w1_scr, w2_scr, stage1, stage2, _ = self.refs
        for cp in self._copies(c):
            cp.wait()
        cols = slice(c * MLP_LOAD, (c + 1) * MLP_LOAD)
        w1_scr[:, cols] = stage1[c % 2].astype(BF16)
        w2_scr[cols, :] = stage2[c % 2].astype(BF16)


def _mlp_tail(x, mix, m_ref, gn2_ref, w1_ref, w2_ref, loading=None):
    y1 = x + m_ref[:, 2 * D_MODEL:3 * D_MODEL] * mix
    h = _norm_mod(y1, gn2_ref[...], m_ref[:, 3 * D_MODEL:4 * D_MODEL], m_ref[:, 4 * D_MODEL:5 * D_MODEL]).astype(BF16)
    nchunk = D_FF // MLP_CHUNK
    acc = jnp.zeros(y1.shape, F32)
    for c in range(nchunk):
        cols = slice(c * MLP_CHUNK, (c + 1) * MLP_CHUNK)
        if loading is not None:
            per = MLP_CHUNK // MLP_LOAD
            for p in range(c * per, (c + 1) * per):
                if p + 1 < D_FF // MLP_LOAD:
                    loading.start(p + 1)
                loading.finish(p)
        a = _dot(h, w1_ref[:, cols])
        a = jnp.square(jnp.maximum(a, 0.0)).astype(BF16)
        acc = acc + _dot(a, w2_ref[cols, :])
    return y1 + m_ref[:, 5 * D_MODEL:6 * D_MODEL] * acc


def _run_tail(x, mix, m_ref, gn2_ref, weights, w1_ref, w2_ref, emit):
    first = pl.program_id(0) == 0

    @pl.when(first)
    def _():
        emit(_mlp_tail(x, mix, m_ref, gn2_ref, w1_ref, w2_ref, loading=weights))

    @pl.when(jnp.logical_not(first))
    def _():
        emit(_mlp_tail(x, mix, m_ref, gn2_ref, w1_ref, w2_ref))


def _even_out_kernel(xp_ref, xs_ref, y5_ref, u_ref, dskip_ref, wglu_ref, bglu_ref, glap_ref, glas_ref, wout_ref,
                     m_ref, gn2_ref, w1_hbm, w2_hbm, o_ref, w1_ref, w2_ref, stage1, stage2, sem, *, layer):
    weights = _MlpWeights(w1_hbm, w2_hbm, w1_ref, w2_ref, stage1, stage2, sem, layer)

    @pl.when(pl.program_id(0) == 0)
    def _():
        weights.start(0)

    nblk = S5_WIDTH // 128
    ys = (jnp.concatenate([y5_ref[b] for b in range(nblk)], axis=1)
          + jnp.concatenate([u_ref[b] for b in range(nblk)], axis=1) * dskip_ref[...])
    gl = jax.nn.gelu(ys)
    s5o = gl * jax.nn.sigmoid(_dot(gl.astype(BF16), wglu_ref[...]) + bglu_ref[...])
    gla = _token_tile(glap_ref, glas_ref, _OUT_TM).astype(BF16)
    mix = _dot(s5o.astype(BF16), wout_ref[0:S5_WIDTH, :]) + _dot(gla, wout_ref[S5_WIDTH:, :])

    def emit(y):
        o_ref[...] = y

    _run_tail(_token_tile(xp_ref, xs_ref, _OUT_TM), mix, m_ref, gn2_ref, weights, w1_ref, w2_ref, emit)


def _odd_out_kernel(x_ref, attp_ref, atts_ref, wo_ref, m_ref, gn2_ref, w1_hbm, w2_hbm, op_ref, os_ref,
                    w1_ref, w2_ref, stage1, stage2, sem, *, layer):
    weights = _MlpWeights(w1_hbm, w2_hbm, w1_ref, w2_ref, stage1, stage2, sem, layer)

    @pl.when(pl.program_id(0) == 0)
    def _():
        weights.start(0)

    mix = _dot(_token_tile(attp_ref, atts_ref, _OUT_TM), wo_ref[...])
    is_prompt = pl.program_id(0) < T_PROMPT // _OUT_TM

    def emit(y):
        @pl.when(is_prompt)
        def _():
            op_ref[...] = y

        @pl.when(jnp.logical_not(is_prompt))
        def _():
            os_ref[...] = y

    _run_tail(x_ref[...], mix, m_ref, gn2_ref, weights, w1_ref, w2_ref, emit)


_OUT_TM = 512


def _const_spec(shape):
    return pl.BlockSpec(shape, lambda i: (0,) * len(shape), pipeline_mode=pl.Buffered(1))


def _tail_specs(layer):
    tm = _OUT_TM
    return [
        pl.BlockSpec((None, 1, 6 * D_MODEL), lambda i: (layer * COND_ROWS + _cond_row(i, tm), 0, 0)),
        _const_spec((1, D_MODEL)),
        pl.BlockSpec(memory_space=pl.ANY),
        pl.BlockSpec(memory_space=pl.ANY),
    ]


def _tail_scratch():
    return [pltpu.VMEM((D_MODEL, D_FF), BF16), pltpu.VMEM((D_FF, D_MODEL), BF16),
            pltpu.VMEM((2, D_MODEL, MLP_LOAD), F32), pltpu.VMEM((2, MLP_LOAD, D_MODEL), F32),
            pltpu.SemaphoreType.DMA((2, 2))]


_TAIL_PARAMS = dict(dimension_semantics=("arbitrary",), vmem_limit_bytes=VMEM_LIMIT)


def _even_out_call(xp, xs, y5, u, d_skip, w_glu, b_glu, gla_p, gla_s, w_out, mods, layer, gn2, w1, w2):
    tm = _OUT_TM
    return pl.pallas_call(
        functools.partial(_even_out_kernel, layer=layer),
        out_shape=jax.ShapeDtypeStruct((T_TOK, D_MODEL), F32),
        grid=(T_TOK // tm,),
        in_specs=_token_specs(tm) + [
            pl.BlockSpec((S5_WIDTH // 128, tm, 128), lambda i: (0, i, 0)),
            pl.BlockSpec((S5_WIDTH // 128, tm, 128), lambda i: (0, i, 0)),
            _const_spec((1, S5_WIDTH)),
            _const_spec((S5_WIDTH, S5_WIDTH)),
            _const_spec((1, S5_WIDTH)),
        ] + _token_specs(tm, GLA_VW) + [
            _const_spec((S5_WIDTH + GLA_VW, D_MODEL)),
        ] + _tail_specs(layer),
        out_specs=pl.BlockSpec((tm, D_MODEL), lambda i: (i, 0)),
        scratch_shapes=_tail_scratch(),
        compiler_params=pltpu.CompilerParams(**_TAIL_PARAMS),
        name="even_out_mlp",
    )(xp, xs, y5, u, d_skip, w_glu, b_glu, gla_p, gla_s, w_out, mods, gn2, w1, w2)


def _odd_out_call(x, att_p, att_s, w_o, mods, layer, gn2, w1, w2):
    tm = _OUT_TM
    return pl.pallas_call(
        functools.partial(_odd_out_kernel, layer=layer),
        out_shape=(jax.ShapeDtypeStruct((T_PROMPT, D_MODEL), F32),
                   jax.ShapeDtypeStruct((T_SAMPLE, D_MODEL), F32)),
        grid=(T_TOK // tm,),
        in_specs=[pl.BlockSpec((tm, D_MODEL), lambda i: (i, 0))] + _token_specs(tm) + [
            _const_spec((D_MODEL, D_MODEL)),
        ] + _tail_specs(layer),
        out_specs=tuple(_token_specs(tm)),
        scratch_shapes=_tail_scratch(),
        compiler_params=pltpu.CompilerParams(**_TAIL_PARAMS),
        name="odd_out_mlp",
    )(x, att_p, att_s, w_o, mods, gn2, w1, w2)


def _qkv_kernel(x_ref, gn_ref, m_ref, w_ref, qn_ref, kn_ref, cos_ref, sin_ref,
                q_ref, kb_ref, vb_ref, k32_ref, v32_ref, *, tile):
    h = _norm_mod(x_ref[...], gn_ref[...], m_ref[:, 0:D_MODEL], m_ref[:, D_MODEL:2 * D_MODEL]).astype(BF16)
    z = _dot(h, w_ref[...])
    v = z[:, (N_HEADS + KV_HEADS) * HEAD_DIM:]
    vb_ref[...] = v.astype(BF16)
    even_lane = (lax.broadcasted_iota(jnp.int32, (1, HEAD_DIM), 1) & 1) == 0

    def heads(rope):
        for hd in range(N_HEADS + KV_HEADS):
            xh = z[:, hd * HEAD_DIM:(hd + 1) * HEAD_DIM]
            gain = qn_ref[...] if hd < N_HEADS else kn_ref[...]
            xh = xh * lax.rsqrt(jnp.mean(xh * xh, axis=-1, keepdims=True) + EPS) * gain
            if rope:
                partner = jnp.where(even_lane, pltpu.roll(xh, HEAD_DIM - 1, 1), pltpu.roll(xh, 1, 1))
                xh = xh * cos_ref[...] + partner * sin_ref[...]
            if hd < N_HEADS:
                q_ref[:, hd * HEAD_DIM:(hd + 1) * HEAD_DIM] = xh.astype(BF16)
            else:
                cols = slice((hd - N_HEADS) * HEAD_DIM, (hd - N_HEADS + 1) * HEAD_DIM)
                kb_ref[:, cols] = xh.astype(BF16)
                if not rope:
                    k32_ref[:, hd - N_HEADS, :] = xh

    is_sample = pl.program_id(0) >= T_PROMPT // tile

    @pl.when(is_sample)
    def _():
        heads(True)

    @pl.when(jnp.logical_not(is_sample))
    def _():
        heads(False)
        for kh in range(KV_HEADS):
            v32_ref[:, kh, :] = v[:, kh * HEAD_DIM:(kh + 1) * HEAD_DIM]


def _qkv_call(x, gn, mods, layer, w_qkv, q_norm, k_norm, cos_t, sin_t):
    tm = 512
    pos_tiles = DEC_SEQ // tm
    n_prompt = T_PROMPT // tm
    kvw = KV_HEADS * HEAD_DIM

    def pos_map(i):
        return (jnp.maximum(i - n_prompt, 0) % pos_tiles, 0)

    def prompt_map(i):
        return (jnp.minimum(i, n_prompt - 1), 0, 0)

    return pl.pallas_call(
        functools.partial(_qkv_kernel, tile=tm),
        out_shape=(jax.ShapeDtypeStruct((T_TOK, N_HEADS * HEAD_DIM), BF16),
                   jax.ShapeDtypeStruct((T_TOK, kvw), BF16),
                   jax.ShapeDtypeStruct((T_TOK, kvw), BF16),
                   jax.ShapeDtypeStruct((T_PROMPT, KV_HEADS, HEAD_DIM), F32),
                   jax.ShapeDtypeStruct((T_PROMPT, KV_HEADS, HEAD_DIM), F32)),
        grid=(T_TOK // tm,),
        in_specs=[
            pl.BlockSpec((tm, D_MODEL), lambda i: (i, 0)),
            pl.BlockSpec((1, D_MODEL), lambda i: (0, 0)),
            pl.BlockSpec((None, 1, 6 * D_MODEL), lambda i: (layer * COND_ROWS + _cond_row(i, tm), 0, 0)),
            pl.BlockSpec(w_qkv.shape, lambda i: (0, 0)),
            pl.BlockSpec((1, HEAD_DIM), lambda i: (0, 0)),
            pl.BlockSpec((1, HEAD_DIM), lambda i: (0, 0)),
            pl.BlockSpec((tm, HEAD_DIM), pos_map),
            pl.BlockSpec((tm, HEAD_DIM), pos_map),
        ],
        out_specs=(pl.BlockSpec((tm, N_HEADS * HEAD_DIM), lambda i: (i, 0)),
                   pl.BlockSpec((tm, kvw), lambda i: (i, 0)),
                   pl.BlockSpec((tm, kvw), lambda i: (i, 0)),
                   pl.BlockSpec((tm, KV_HEADS, HEAD_DIM), prompt_map),
                   pl.BlockSpec((tm, KV_HEADS, HEAD_DIM), prompt_map)),
        compiler_params=pltpu.CompilerParams(vmem_limit_bytes=VMEM_LIMIT),
        name="odd_qkv",
    )(x, gn, mods, w_qkv, q_norm, k_norm, cos_t, sin_t)


def _rope_tables():
    rows = DEC_SEQ // GRID_W
    row = jnp.repeat(jnp.arange(rows, dtype=F32), GRID_W)
    col = jnp.tile(jnp.arange(GRID_W, dtype=F32), rows)
    inv = ROPE_THETA ** (-jnp.arange(0, AXIS_DIM, 2, dtype=F32) / AXIS_DIM)
    ang = jnp.concatenate([row[:, None] * inv, col[:, None] * inv], axis=-1)
    cos_t = jnp.repeat(jnp.cos(ang), 2, axis=-1)
    sin = jnp.sin(ang)
    sin_t = jnp.stack([-sin, sin], axis=-1).reshape(DEC_SEQ, HEAD_DIM)
    return cos_t, sin_t


def _attn_kernel(*refs, has_cache):
    q_ref, k_ref, v_ref = refs[:3]
    ck_ref, cv_ref = refs[3:5] if has_cache else (None, None)
    o_ref = refs[-1]
    c = HEAD_DIM ** -0.5 * math.log2(math.e)
    ones_col = (lax.broadcasted_iota(jnp.int32, (1, HEAD_DIM), 1) == 0).astype(BF16)

    def with_ones(v):
        return jnp.concatenate([v, jnp.broadcast_to(ones_col, v.shape)], axis=1)

    k = k_ref[...]
    v = with_ones(v_ref[...])
    if has_cache:
        ck = ck_ref[...].astype(BF16)
        cv = with_ones(cv_ref[...].astype(BF16))
    for r in range(Q_PER_KV):
        cs = slice(r * HEAD_DIM, (r + 1) * HEAD_DIM)
        q = q_ref[:, cs]
        s = lax.dot_general(q, k, NT_DIMS, preferred_element_type=F32)
        m = jnp.max(s, axis=-1, keepdims=True)
        if has_cache:
            sc = lax.dot_general(q, ck, NT_DIMS, preferred_element_type=F32)
            m = jnp.maximum(m, jnp.max(sc, axis=-1, keepdims=True))
        mc = m * c
        o = _dot(jnp.exp2(s * c - mc).astype(BF16), v)
        if has_cache:
            o = o + _dot(jnp.exp2(sc * c - mc).astype(BF16), cv)
        o_ref[:, cs] = (o[:, 0:HEAD_DIM] / o[:, HEAD_DIM:HEAD_DIM + 1]).astype(BF16)


def _attn_call(q, k, v, cache_k, cache_v, seq_len, row0, nrows):
    assert row0 % seq_len == 0 and nrows % seq_len == 0
    has_cache = cache_k is not None
    b0 = row0 // seq_len
    gw = Q_PER_KV * HEAD_DIM
    in_specs = [
        pl.BlockSpec((seq_len, gw), lambda b, g: (b0 + b, g)),
        pl.BlockSpec((seq_len, HEAD_DIM), lambda b, g: (b0 + b, g)),
        pl.BlockSpec((seq_len, HEAD_DIM), lambda b, g: (b0 + b, g)),
    ]
    args = [q, k, v]
    if has_cache:
        in_specs += [pl.BlockSpec((PAST_LEN, HEAD_DIM), lambda b, g: (b, g)),
                     pl.BlockSpec((PAST_LEN, HEAD_DIM), lambda b, g: (b, g))]
        args += [cache_k, cache_v]
    return pl.pallas_call(
        functools.partial(_attn_kernel, has_cache=has_cache),
        out_shape=jax.ShapeDtypeStruct((nrows, N_HEADS * HEAD_DIM), BF16),
        grid=(nrows // seq_len, KV_HEADS),
        in_specs=in_specs,
        out_specs=pl.BlockSpec((seq_len, gw), lambda b, g: (b, g)),
        compiler_params=pltpu.CompilerParams(vmem_limit_bytes=VMEM_LIMIT),
        name=f"attn_len{seq_len}",
    )(*args)


def kernel(x_prompt, x_sample, state_s5_re, state_s5_im, state_gla, cache_k, cache_v, c, c_ctx, norm_mix, norm_mlp, w_ada, b_ada, w_mlp_in, w_mlp_out, w_in_e, w_out_e, s5_lambda_re, s5_lambda_im, s5_log_dt, s5_b_re, s5_b_im, s5_c_re, s5_c_im, s5_d, s5_w_glu, s5_b_glu, gla_w_gate2, gla_b_gate, gla_norm, w_qkv_o, w_o_o, q_norm, k_norm):
    xp = x_prompt.reshape(T_PROMPT, D_MODEL)
    xs = x_sample.reshape(T_SAMPLE, D_MODEL)
    cond8 = jnp.concatenate([c_ctx[None, :], c, jnp.zeros((COND_ROWS - 1 - DEC_BATCH, D_MODEL), F32)], axis=0)
    mods = _ada_call(cond8, w_ada, b_ada).reshape(DEPTH * COND_ROWS, 1, 6 * D_MODEL)
    w1_all, w2_all = w_mlp_in, w_mlp_out

    n_main = S5_WIDTH + 2 * GLA_QK + 2 * GLA_VW
    w_in = w_in_e[0]
    w_main = w_in[:, :n_main].astype(BF16)
    w_glr = jnp.pad(w_in[:, n_main:], ((0, 0), (0, 128 - 2 * GLA_RANK))).astype(BF16)
    w_gate = jnp.zeros((128, 2 * GLA_QK), F32)
    w_gate = w_gate.at[0:GLA_RANK, 0:GLA_QK].set(gla_w_gate2[0, 0])
    w_gate = w_gate.at[GLA_RANK:2 * GLA_RANK, GLA_QK:].set(gla_w_gate2[0, 1]).astype(BF16)
    b_gate = gla_b_gate[0].reshape(1, 2 * GLA_QK)
    u, z, g = _inproj_call(xp, xs, norm_mix[0:1], mods, 0, w_main, w_glr, w_gate, b_gate)

    mats = _s5_prep_call(s5_lambda_re[0], s5_lambda_im[0], s5_log_dt[0], s5_b_re[0], s5_b_im[0],
                         s5_c_re[0], s5_c_im[0])

    def state_rows(s):
        return jnp.transpose(s, (2, 0, 1, 3)).reshape(S5_GROUPS, DEC_BATCH, 2 * S5_STATE)

    h0 = jnp.concatenate([state_rows(state_s5_re[:, 0]), state_rows(state_s5_im[:, 0])], axis=-1)
    nsteps = S5_GROUPS // S5_GPB
    h0 = jnp.transpose(h0.reshape(nsteps, S5_GPB, DEC_BATCH, S5_W), (0, 2, 1, 3))
    y5, ns = _s5_call(u, mats, h0)
    ns = jnp.transpose(ns, (0, 2, 1, 3)).reshape(S5_GROUPS, BATCH, S5_W)

    def state_out(n):
        return jnp.transpose(n.reshape(S5_GROUPS, BATCH, 2, S5_STATE), (1, 2, 0, 3))[:, None]

    new_s5_re = state_out(ns[:, :, :2 * S5_STATE])
    new_s5_im = state_out(ns[:, :, 2 * S5_STATE:])

    gn_gla = gla_norm[0].reshape(1, GLA_DV)
    gla_p, sfin = _gla_call(z, g, gn_gla, None, SEQ, BATCH, 0)
    s0 = state_gla[:, 0].reshape(DEC_BATCH, 2, GLA_QK, GLA_DV)
    gla_s, _ = _gla_call(z, g, gn_gla, s0, DEC_SEQ, DEC_BATCH, T_PROMPT)
    new_gla = sfin.reshape(BATCH, 1, 2, GLA_HEADS, GLA_DK, GLA_DV)

    x = _even_out_call(xp, xs, y5, u, s5_d[0].reshape(1, S5_WIDTH), s5_w_glu[0].astype(BF16),
                       s5_b_glu[0].reshape(1, S5_WIDTH), gla_p, gla_s, w_out_e[0].astype(BF16), mods, 0,
                       norm_mlp[0:1], w1_all, w2_all)

    cos_t, sin_t = _rope_tables()
    q, k, v, k32, v32 = _qkv_call(x, norm_mix[1:2], mods, 1, w_qkv_o[0].astype(BF16),
                                  q_norm[0].reshape(1, HEAD_DIM), k_norm[0].reshape(1, HEAD_DIM), cos_t, sin_t)
    att_p = _attn_call(q, k, v, None, None, SEQ, 0, T_PROMPT)
    ck = cache_k[:, 0].reshape(DEC_BATCH * PAST_LEN, KV_HEADS * HEAD_DIM)
    cv = cache_v[:, 0].reshape(DEC_BATCH * PAST_LEN, KV_HEADS * HEAD_DIM)
    att_s = _attn_call(q, k, v, ck, cv, DEC_SEQ, T_PROMPT, T_SAMPLE)
    yp, ys = _odd_out_call(x, att_p, att_s, w_o_o[0].astype(BF16), mods, 1, norm_mlp[1:2],
                           w1_all, w2_all)

    new_k = k32.reshape(BATCH, 1, SEQ, KV_HEADS, HEAD_DIM)
    new_v = v32.reshape(BATCH, 1, SEQ, KV_HEADS, HEAD_DIM)
    y_prompt = yp.reshape(BATCH, SEQ, D_MODEL)
    y_sample = ys.reshape(DEC_BATCH, DEC_SEQ, D_MODEL)
    return (y_prompt, y_sample, new_s5_re, new_s5_im, new_gla, new_k, new_v)
```

```python
import functools
import math

import jax
import jax.numpy as jnp
import numpy as np
from jax import lax
from jax.experimental import pallas as pl
from jax.experimental.pallas import tpu as pltpu

F32 = jnp.float32
BF16 = jnp.bfloat16

D_MODEL = 1024
BATCH = 16
SEQ = 256
DEPTH = 2
DEC_BATCH = 4
DEC_SEQ = 1024
PAST_LEN = 512
GRID_W = 64
S5_WIDTH = 512
S5_GROUP_CH = 16
S5_GROUPS = 32
S5_STATE = 64
GLA_HEADS = 4
GLA_VW = 512
GLA_DV = 128
GLA_DK = 64
GLA_QK = 256
GLA_RANK = 16
GLA_TAU = 16.0
GLA_CHUNK = 64
GLA_CPB = 4
GLA_BLK = GLA_CPB * GLA_CHUNK
HEAD_DIM = 128
N_HEADS = 8
KV_HEADS = 2
Q_PER_KV = N_HEADS // KV_HEADS
AXIS_DIM = 64
ROPE_THETA = 10000.0
D_FF = 4096
EPS = 1e-6

T_PROMPT = BATCH * SEQ
T_SAMPLE = DEC_BATCH * DEC_SEQ
T_TOK = T_PROMPT + T_SAMPLE
COND_ROWS = 8
COND_SPAN = 1024
PROMPT_SPANS = T_PROMPT // COND_SPAN

S5_Q = 16
S5_W = S5_Q * S5_GROUP_CH
S5_GPB = 128 // S5_GROUP_CH
S5_ROWS = T_TOK // S5_Q
S5_PROMPT_ROWS = T_PROMPT // S5_Q
S5_PROMPT_CHUNKS = SEQ // S5_Q
S5_SAMPLE_CHUNKS = DEC_SEQ // S5_Q

VMEM_LIMIT = 56 * 1024 * 1024

NT_DIMS = (((1,), (1,)), ((), ()))
TN_DIMS = (((0,), (0,)), ((), ()))


def _cond_row(i, tile):
    return jnp.maximum((i * tile) // COND_SPAN - (PROMPT_SPANS - 1), 0)


def _norm_mod(x, gain, shift, scale):
    y = x * lax.rsqrt(jnp.mean(x * x, axis=-1, keepdims=True) + EPS)
    return (y * gain) * (1.0 + scale) + shift


def _dot(a, b):
    return jnp.dot(a, b, preferred_element_type=F32)


def _ada_kernel(cond_ref, w_ref, b_ref, o_ref):
    s = jax.nn.silu(cond_ref[...]).astype(BF16)
    o_ref[...] = _dot(s, w_ref[...].astype(BF16)) + b_ref[...]


def _ada_call(cond8, w_ada, b_ada):
    tn = 2048
    nj = 6 * D_MODEL // tn
    return pl.pallas_call(
        _ada_kernel,
        out_shape=jax.ShapeDtypeStruct((DEPTH, COND_ROWS, 6 * D_MODEL), F32),
        grid=(DEPTH, nj),
        in_specs=[
            pl.BlockSpec((COND_ROWS, D_MODEL), lambda l, j: (0, 0)),
            pl.BlockSpec((None, D_MODEL, tn), lambda l, j: (l, 0, j)),
            pl.BlockSpec((None, 1, tn), lambda l, j: (l, 0, j)),
        ],
        out_specs=pl.BlockSpec((None, COND_ROWS, tn), lambda l, j: (l, 0, j)),
        compiler_params=pltpu.CompilerParams(vmem_limit_bytes=VMEM_LIMIT),
        name="ada_mod",
    )(cond8, w_ada, b_ada.reshape(DEPTH, 1, 6 * D_MODEL))


def _token_specs(tile, width=D_MODEL):
    n_prompt = T_PROMPT // tile
    return [pl.BlockSpec((tile, width), lambda i: (jnp.minimum(i, n_prompt - 1), 0)),
            pl.BlockSpec((tile, width), lambda i: (jnp.maximum(i - n_prompt, 0), 0))]


def _token_tile(xp_ref, xs_ref, tile):
    return jnp.where(pl.program_id(0) < T_PROMPT // tile, xp_ref[...], xs_ref[...])


IN_MAIN = S5_WIDTH + 2 * GLA_QK + 2 * GLA_VW
IN_EVEN = IN_MAIN + 2 * GLA_RANK
GLR_PAD = 128


def _inproj_kernel(xp_ref, xs_ref, gn_ref, m_ref, w_ref, wg_ref, bg_ref, u_ref, z_ref, g_ref,
                   wmain_scr, wglr_scr, *, tile):
    @pl.when(pl.program_id(0) == 0)
    def _():
        wmain_scr[...] = w_ref[:, 0:IN_MAIN].astype(BF16)
        wglr_scr[...] = jnp.zeros(wglr_scr.shape, BF16)
        wglr_scr[:, 0:2 * GLA_RANK] = w_ref[:, IN_MAIN:IN_EVEN].astype(BF16)

    x = _token_tile(xp_ref, xs_ref, tile)
    h = _norm_mod(x, gn_ref[...], m_ref[:, 0:D_MODEL], m_ref[:, D_MODEL:2 * D_MODEL]).astype(BF16)
    z = _dot(h, wmain_scr[...])
    for blk in range(S5_WIDTH // 128):
        u_ref[blk] = z[:, blk * 128:(blk + 1) * 128]
    z_ref[...] = z[:, S5_WIDTH:]
    glr = _dot(h, wglr_scr[...]).astype(BF16)
    pre = _dot(glr, wg_ref[...]) + bg_ref[...]
    g_ref[...] = jax.nn.log_sigmoid(pre) * (1.0 / GLA_TAU)


def _inproj_call(xp, xs, gn, mods, layer, w_in, w_gate, b_gate):
    tm = 512
    return pl.pallas_call(
        functools.partial(_inproj_kernel, tile=tm),
        out_shape=(jax.ShapeDtypeStruct((S5_WIDTH // 128, T_TOK, 128), F32),
                   jax.ShapeDtypeStruct((T_TOK, IN_MAIN - S5_WIDTH), F32),
                   jax.ShapeDtypeStruct((T_TOK, 2 * GLA_QK), F32)),
        grid=(T_TOK // tm,),
        in_specs=_token_specs(tm) + [
            pl.BlockSpec((1, D_MODEL), lambda i: (0, 0)),
            pl.BlockSpec((None, 1, 6 * D_MODEL), lambda i: (layer * COND_ROWS + _cond_row(i, tm), 0, 0)),
            pl.BlockSpec((D_MODEL, IN_EVEN), lambda i: (0, 0), pipeline_mode=pl.Buffered(1)),
            pl.BlockSpec((GLR_PAD, 2 * GLA_QK), lambda i: (0, 0)),
            pl.BlockSpec((1, 2 * GLA_QK), lambda i: (0, 0)),
        ],
        out_specs=(pl.BlockSpec((S5_WIDTH // 128, tm, 128), lambda i: (0, i, 0)),
                   pl.BlockSpec((tm, IN_MAIN - S5_WIDTH), lambda i: (i, 0)),
                   pl.BlockSpec((tm, 2 * GLA_QK), lambda i: (i, 0))),
        scratch_shapes=[pltpu.VMEM((D_MODEL, IN_MAIN), BF16), pltpu.VMEM((D_MODEL, GLR_PAD), BF16)],
        compiler_params=pltpu.CompilerParams(dimension_semantics=("arbitrary",), vmem_limit_bytes=VMEM_LIMIT),
        name="even_inproj",
    )(xp, xs, gn, mods, w_in, w_gate, b_gate)


def _s5_prep_kernel(lre_ref, lim_ref, ldt_ref, btre_ref, btim_ref, cre_ref, cim_ref, ccf_ref, ccb_ref,
                    t_ref, bq_ref, cqt_ref, be_ref, a_ref, t_scr, dd_scr):
    lre = lre_ref[...]
    lim = lim_ref[...]
    dt = jnp.exp(ldt_ref[...])
    a = lre * dt
    th = lim * dt

    def lam_pow(k):
        mag = jnp.exp(k * a)
        return mag * jnp.cos(k * th), mag * jnp.sin(k * th)

    lb_re, lb_im = lam_pow(1.0)
    nr = lb_re - 1.0
    den = lre * lre + lim * lim
    cf_re = (nr * lre + lb_im * lim) / den
    cf_im = (lb_im * lre - nr * lim) / den
    bt_re = btre_ref[...]
    bt_im = btim_ref[...]
    bb_re = jnp.tile(cf_re * bt_re - cf_im * bt_im, (S5_Q, 1))
    bb_im = jnp.tile(cf_re * bt_im + cf_im * bt_re, (S5_Q, 1))

    shape = (S5_W, 128)
    pos = lax.shift_right_logical(lax.broadcasted_iota(jnp.int32, shape, 0), 4)
    is_f = lax.broadcasted_iota(jnp.int32, shape, 1) < S5_STATE
    posq = lax.broadcasted_iota(jnp.int32, (S5_Q, 128), 0).astype(F32)
    is_fq = lax.broadcasted_iota(jnp.int32, (S5_Q, 128), 1) < S5_STATE

    def per_channel(tbl):
        return jnp.broadcast_to(tbl[:, None, :], (S5_Q, S5_GROUP_CH, 128)).reshape(shape)

    p_re, p_im = map(per_channel, lam_pow(jnp.where(is_fq, (S5_Q - 1.0) - posq, posq)))
    w_re = p_re * bb_re - p_im * bb_im
    w_im = p_re * bb_im + p_im * bb_re
    bq = jnp.concatenate([w_re, w_im], axis=1)
    bqt = jnp.transpose(bq)
    bq_ref[...] = bqt.astype(BF16)

    edge = pos == jnp.where(is_f, 0, S5_Q - 1)
    be = jnp.concatenate([jnp.where(edge, bb_re, 0.0), jnp.where(edge, bb_im, 0.0)], axis=1)
    be_ref[...] = jnp.transpose(be).astype(BF16)

    q_re, q_im = map(per_channel, lam_pow(jnp.where(is_fq, posq + 1.0, S5_Q - posq)))
    ct_re = jnp.tile(cre_ref[...], (S5_Q, 1))
    ct_im = jnp.tile(cim_ref[...], (S5_Q, 1))
    g_re = q_re * ct_re - q_im * ct_im
    g_im = q_re * ct_im + q_im * ct_re
    cqt_ref[...] = jnp.concatenate([g_re, -g_im], axis=1).astype(BF16)

    a_re, a_im = lam_pow(float(S5_Q))
    a_ref[...] = jnp.concatenate([a_re, a_im], axis=1)

    kf = jnp.dot(ccf_ref[...], bqt, precision=lax.Precision.HIGHEST, preferred_element_type=F32)
    kb = jnp.dot(ccb_ref[...], bqt, precision=lax.Precision.HIGHEST, preferred_element_type=F32)
    gch = S5_GROUP_CH
    lo = S5_W - gch
    dd_scr[:, 0:S5_W] = kf
    dd_scr[:, lo:lo + S5_W] = kb
    dd_scr[:, lo:S5_W] = kf[:, lo:S5_W] + kb[:, 0:gch]
    for t in range(S5_Q):
        c0 = (S5_Q - 1 - t) * gch
        t_scr[t * gch:(t + 1) * gch, :] = dd_scr[:, c0:c0 + S5_W]
    t_ref[...] = t_scr[...].astype(BF16)


def _s5_prep_call(lam_re, lam_im, log_dt, b_re, b_im, c_re, c_im):
    def fb(p):
        return jnp.transpose(p, (1, 0, 2)).reshape(S5_GROUPS, 1, 2 * S5_STATE)

    def dup(p):
        return jnp.concatenate([p, p], axis=-1)

    ldt = fb(jnp.broadcast_to(log_dt[:, :, None], (2, S5_GROUPS, S5_STATE)))
    bt_re = dup(jnp.transpose(b_re, (0, 2, 1)))
    bt_im = dup(jnp.transpose(b_im, (0, 2, 1)))
    zero = jnp.zeros_like(c_re)
    ccf = jnp.concatenate([c_re, zero, -c_im, zero], axis=-1)
    ccb = jnp.concatenate([zero, c_re, zero, -c_im], axis=-1)

    row = pl.BlockSpec((None, 1, 128), lambda g: (g, 0, 0))
    mat16 = pl.BlockSpec((None, S5_GROUP_CH, 128), lambda g: (g, 0, 0))
    mat16w = pl.BlockSpec((None, S5_GROUP_CH, S5_W), lambda g: (g, 0, 0))
    sq = pl.BlockSpec((None, S5_W, S5_W), lambda g: (g, 0, 0))
    sq_shape = jax.ShapeDtypeStruct((S5_GROUPS, S5_W, S5_W), BF16)
    return pl.pallas_call(
        _s5_prep_kernel,
        out_shape=(sq_shape, sq_shape, sq_shape, sq_shape,
                   jax.ShapeDtypeStruct((S5_GROUPS, 1, S5_W), F32)),
        grid=(S5_GROUPS,),
        in_specs=[row, row, row, mat16, mat16, mat16, mat16, mat16w, mat16w],
        out_specs=(sq, sq, sq, sq, pl.BlockSpec((None, 1, S5_W), lambda g: (g, 0, 0))),
        scratch_shapes=[pltpu.VMEM((S5_W, S5_W), F32), pltpu.VMEM((S5_GROUP_CH, 2 * S5_W), F32)],
        name="s5_prep",
    )(fb(lam_re), fb(lam_im), ldt, bt_re, bt_im, dup(c_re), dup(c_im), ccf, ccb)


def _s5_kernel(u_ref, tt_ref, bqt_ref, cqt_ref, bet_ref, a_ref, h0_ref, y_ref, ns_ref,
               ut_scr, x_scr, spf_scr, spb_scr, ne_scr, yt_scr, xt_scr):
    gch = S5_GROUP_CH
    for s in range(S5_Q):
        rows = u_ref[pl.ds(s, S5_ROWS, stride=S5_Q), :]
        rows_t = jnp.transpose(rows).astype(BF16)
        for gl in range(S5_GPB):
            ut_scr[gl, s * gch:(s + 1) * gch, :] = rows_t[gl * gch:(gl + 1) * gch, :]

    for gl in range(S5_GPB):
        ut = ut_scr[gl]
        xt_scr[...] = _dot(bqt_ref[gl], ut)
        x = jnp.transpose(xt_scr[...])
        xt_scr[:, 0:S5_PROMPT_ROWS] = _dot(bet_ref[gl], ut[:, 0:S5_PROMPT_ROWS])
        ne = jnp.transpose(xt_scr[:, 0:S5_PROMPT_ROWS])
        for part in range(2):
            x_scr[part, pl.ds(gl, S5_ROWS, stride=S5_GPB), :] = x[:, part * 128:(part + 1) * 128]
            ne_scr[part, pl.ds(gl, S5_PROMPT_ROWS, stride=S5_GPB), :] = ne[:, part * 128:(part + 1) * 128]

    is_f = lax.broadcasted_iota(jnp.int32, (1, 128), 1) < S5_STATE
    a_re = a_ref[:, 0:128]
    a_im = a_ref[:, 128:256]

    def tile(row):
        return pl.ds(pl.multiple_of(row * S5_GPB, S5_GPB), S5_GPB)

    def scan(base, nseq, nchunk, s_init):
        def body(i, state):
            new = []
            for b in range(nseq):
                s_re, s_im = state[b]
                rows_f = tile(base + b * nchunk + i)
                rows_b = tile(base + b * nchunk + (nchunk - 1 - i))
                spf_scr[0, rows_f, :] = s_re
                spf_scr[1, rows_f, :] = s_im
                spb_scr[0, rows_b, :] = s_re
                spb_scr[1, rows_b, :] = s_im
                x_re = jnp.where(is_f, x_scr[0, rows_f, :], x_scr[0, rows_b, :])
                x_im = jnp.where(is_f, x_scr[1, rows_f, :], x_scr[1, rows_b, :])
                new.append((a_re * s_re - a_im * s_im + x_re, a_re * s_im + a_im * s_re + x_im))
            return tuple(new)

        lax.fori_loop(0, nchunk, body, tuple(s_init))

    zero = jnp.zeros((S5_GPB, 128), F32)
    scan(0, BATCH, S5_PROMPT_CHUNKS, [(zero, zero)] * BATCH)
    scan(S5_PROMPT_ROWS, DEC_BATCH, S5_SAMPLE_CHUNKS,
         [(h0_ref[b, :, 0:128], h0_ref[b, :, 128:256]) for b in range(DEC_BATCH)])

    for b in range(BATCH):
        first = pl.ds(b * S5_PROMPT_CHUNKS * S5_GPB, S5_GPB)
        last = pl.ds(((b + 1) * S5_PROMPT_CHUNKS - 1) * S5_GPB, S5_GPB)
        for part in range(2):
            ns_ref[b, :, part * 128:(part + 1) * 128] = jnp.where(is_f, ne_scr[part, first, :], ne_scr[part, last, :])

    for gl in range(S5_GPB):
        rows = pl.ds(gl, S5_ROWS, stride=S5_GPB)
        carried = jnp.concatenate([jnp.where(is_f, spf_scr[p, rows, :], spb_scr[p, rows, :]) for p in range(2)],
                                  axis=1).astype(BF16)
        yt = _dot(tt_ref[gl], ut_scr[gl]) + lax.dot_general(cqt_ref[gl], carried, NT_DIMS,
                                                            preferred_element_type=F32)
        for t in range(S5_Q):
            yt_scr[t, gl * gch:(gl + 1) * gch, :] = yt[t * gch:(t + 1) * gch, :]
    for t in range(S5_Q):
        y_ref[pl.ds(t, S5_ROWS, stride=S5_Q), :] = jnp.transpose(yt_scr[t])


def _s5_call(u, mats, h0):
    tt_m, bqt_m, cqt_m, bet_m, a_m = mats
    nsteps = S5_GROUPS // S5_GPB
    sq = pl.BlockSpec((S5_GPB, S5_W, S5_W), lambda g: (g, 0, 0))
    state_scr = pltpu.VMEM((2, S5_ROWS * S5_GPB, 128), F32)
    return pl.pallas_call(
        _s5_kernel,
        out_shape=(jax.ShapeDtypeStruct((nsteps, T_TOK, 128), F32),
                   jax.ShapeDtypeStruct((nsteps, BATCH, S5_GPB, S5_W), F32)),
        grid=(nsteps,),
        in_specs=[
            pl.BlockSpec((None, T_TOK, 128), lambda g: (g, 0, 0)),
            sq, sq, sq, sq,
            pl.BlockSpec((S5_GPB, S5_W), lambda g: (g, 0)),
            pl.BlockSpec((None, DEC_BATCH, S5_GPB, S5_W), lambda g: (g, 0, 0, 0)),
        ],
        out_specs=(pl.BlockSpec((None, T_TOK, 128), lambda g: (g, 0, 0)),
                   pl.BlockSpec((None, BATCH, S5_GPB, S5_W), lambda g: (g, 0, 0, 0))),
        scratch_shapes=[pltpu.VMEM((S5_GPB, S5_W, S5_ROWS), BF16), state_scr, state_scr, state_scr,
                        pltpu.VMEM((2, S5_PROMPT_ROWS * S5_GPB, 128), F32),
                        pltpu.VMEM((S5_Q, 128, S5_ROWS), F32), pltpu.VMEM((S5_W, S5_ROWS), F32)],
        compiler_params=pltpu.CompilerParams(vmem_limit_bytes=VMEM_LIMIT),
        name="s5_scan",
    )(u, tt_m, bqt_m, cqt_m, bet_m, a_m.reshape(S5_GROUPS, S5_W), h0)


def _split_bf16(x):
    hi = x.astype(BF16)
    r1 = x - hi.astype(F32)
    mid = r1.astype(BF16)
    lo = (r1 - mid.astype(F32)).astype(BF16)
    return hi, mid, lo


def _gla_kernel(*refs, seq_len, has_s0):
    q_ref, k_ref, v_ref, gf_ref, gb_ref, r_ref, gn_ref = refs[:7]
    s0_ref = refs[7] if has_s0 else None
    o_ref, sfin_ref, oi_scr, qd_scr, kv_scr, dec_scr, ss_scr = refs[7 + has_s0:]
    nblk = seq_len // GLA_BLK
    nchunk = seq_len // GLA_CHUNK
    cl = GLA_CHUNK
    ti = lax.broadcasted_iota(jnp.int32, (GLA_BLK, GLA_BLK), 0)
    si = lax.broadcasted_iota(jnp.int32, (GLA_BLK, GLA_BLK), 1)
    same = lax.shift_right_logical(ti, 6) == lax.shift_right_logical(si, 6)
    keep = (same & (ti >= si), same & (ti <= si))
    tri = tuple(kp.astype(BF16) for kp in keep)
    lane_head = lax.shift_right_logical(lax.broadcasted_iota(jnp.int32, (cl, GLA_QK), 1), 6)
    zeros_v = jnp.zeros((cl, GLA_DV), BF16)
    heads = [(slice(h * GLA_DK, (h + 1) * GLA_DK), slice(h * GLA_DV, (h + 1) * GLA_DV)) for h in range(GLA_HEADS)]

    for j in range(nblk):
        rows = slice(j * GLA_BLK, (j + 1) * GLA_BLK)
        q = q_ref[rows, :] * (GLA_DK ** -0.5)
        k = k_ref[rows, :]
        v = v_ref[rows, :].astype(BF16)
        qd, kd, k2t = [], [], []
        for d, g_ref in enumerate((gf_ref, gb_ref)):
            b = sum(_dot(tri[d], part) for part in _split_bf16(g_ref[rows, :]))
            last = cl - 1 if d == 0 else 0
            b_last = [b[c * cl + last:c * cl + last + 1] for c in range(GLA_CPB)]
            bl = jnp.concatenate([jnp.broadcast_to(x, (cl, GLA_QK)) for x in b_last], axis=0)
            qd_d = (q * jnp.exp(b)).astype(BF16)
            qd_scr[d, rows, :] = qd_d
            qd.append(qd_d)
            kd.append((k * jnp.exp(-b)).astype(BF16))
            k2t.append(jnp.transpose(k * jnp.exp(bl - b)).astype(BF16))
            for c in range(GLA_CPB):
                dec_scr[d, j * GLA_CPB + c] = jnp.exp(jnp.transpose(jnp.broadcast_to(b_last[c], (GLA_DV, GLA_QK))))
        for h, (ks, vs) in enumerate(heads):
            att = [jnp.where(keep[d], lax.dot_general(qd[d][:, ks], kd[d][:, ks], NT_DIMS,
                                                      preferred_element_type=F32), 0.0) for d in range(2)]
            oi_scr[rows, vs] = _dot((att[0] + att[1]).astype(BF16), v[:, vs])
            vh = v[:, vs]
            vexp = jnp.concatenate(
                [jnp.concatenate([vh[c * cl:(c + 1) * cl] if c2 == c else zeros_v for c2 in range(GLA_CPB)], axis=1)
                 for c in range(GLA_CPB)], axis=0)
            for d in range(2):
                kv_scr[d, j, h] = _dot(k2t[d][ks, :], vexp)

    for d in range(2):
        s = s0_ref[d] if has_s0 else jnp.zeros((GLA_QK, GLA_DV), F32)
        for cg in (range(nchunk) if d == 0 else range(nchunk - 1, -1, -1)):
            j, c = divmod(cg, GLA_CPB)
            ss_scr[d, cg] = s.astype(BF16)
            kv = jnp.concatenate([kv_scr[d, j, h, :, c * GLA_DV:(c + 1) * GLA_DV] for h in range(GLA_HEADS)], axis=0)
            s = s * dec_scr[d, cg] + kv
        sfin_ref[d] = s

    for cg in range(nchunk):
        rows = slice(cg * cl, (cg + 1) * cl)
        inter = []
        for d in range(2):
            qc = qd_scr[d, rows, :]
            qstack = jnp.concatenate([jnp.where(lane_head == h, qc, jnp.zeros_like(qc)) for h in range(GLA_HEADS)],
                                     axis=0)
            inter.append(_dot(qstack, ss_scr[d, cg]))
        gate = jax.nn.silu(r_ref[rows, :])
        for h, (ks, vs) in enumerate(heads):
            hr = slice(h * cl, (h + 1) * cl)
            oh = oi_scr[rows, vs] + inter[0][hr] + inter[1][hr]
            oh = oh * lax.rsqrt(jnp.mean(oh * oh, axis=-1, keepdims=True) + EPS) * gn_ref[...]
            o_ref[rows, vs] = oh * gate[:, vs]


def _gla_call(z, g, gla_norm, s0, seq_len, nseq, row0):
    assert row0 % seq_len == 0
    r0 = row0 // seq_len
    has_s0 = s0 is not None
    qk_off = 0
    v_off = 2 * GLA_QK // GLA_VW
    in_specs = [
        pl.BlockSpec((seq_len, GLA_QK), lambda i: (r0 + i, qk_off)),
        pl.BlockSpec((seq_len, GLA_QK), lambda i: (r0 + i, qk_off + 1)),
        pl.BlockSpec((seq_len, GLA_VW), lambda i: (r0 + i, v_off)),
        pl.BlockSpec((seq_len, GLA_QK), lambda i: (r0 + i, 0)),
        pl.BlockSpec((seq_len, GLA_QK), lambda i: (r0 + i, 1)),
        pl.BlockSpec((seq_len, GLA_VW), lambda i: (r0 + i, v_off + 1)),
        pl.BlockSpec((1, GLA_DV), lambda i: (0, 0)),
    ]
    args = [z, z, z, g, g, z, gla_norm]
    if has_s0:
        in_specs.append(pl.BlockSpec((None, 2, GLA_QK, GLA_DV), lambda i: (i, 0, 0, 0)))
        args.append(s0)
    return pl.pallas_call(
        functools.partial(_gla_kernel, seq_len=seq_len, has_s0=has_s0),
        out_shape=(jax.ShapeDtypeStruct((nseq * seq_len, GLA_VW), F32),
                   jax.ShapeDtypeStruct((nseq, 2, GLA_QK, GLA_DV), F32)),
        grid=(nseq,),
        in_specs=in_specs,
        out_specs=(pl.BlockSpec((seq_len, GLA_VW), lambda i: (i, 0)),
                   pl.BlockSpec((None, 2, GLA_QK, GLA_DV), lambda i: (i, 0, 0, 0))),
        scratch_shapes=[
            pltpu.VMEM((seq_len, GLA_VW), F32),
            pltpu.VMEM((2, seq_len, GLA_QK), BF16),
            pltpu.VMEM((2, seq_len // GLA_BLK, GLA_HEADS, GLA_DK, GLA_CPB * GLA_DV), F32),
            pltpu.VMEM((2, seq_len // GLA_CHUNK, GLA_QK, GLA_DV), F32),
            pltpu.VMEM((2, seq_len // GLA_CHUNK, GLA_QK, GLA_DV), BF16),
        ],
        compiler_params=pltpu.CompilerParams(vmem_limit_bytes=VMEM_LIMIT),
        name=f"gla_len{seq_len}",
    )(*args)


MLP_CHUNK = 512
MLP_LOAD = 256


class _MlpWeights:
    def __init__(self, w1_hbm, w2_hbm, w1_scr, w2_scr, stage1, stage2, sem, layer):
        self.refs = (w1_hbm, w2_hbm, w1_scr, w2_scr, stage1, stage2, sem)
        self.layer = layer

    def _copies(self, c):
        w1_hbm, w2_hbm, _, _, stage1, stage2, sem = self.refs
        cols = pl.ds(c * MLP_LOAD, MLP_LOAD)
        return (pltpu.make_async_copy(w1_hbm.at[self.layer, :, cols], stage1.at[c % 2], sem.at[0, c % 2]),
                pltpu.make_async_copy(w2_hbm.at[self.layer, cols, :], stage2.at[c % 2], sem.at[1, c % 2]))

    def start(self, c):
        for cp in self._copies(c):
            cp.start()

    def finish(self, c):
        _, _, w1_scr, w2_scr, stage1, stage2, _ = self.refs
        for cp in self._copies(c):
            cp.wait()
        cols = slice(c * MLP_LOAD, (c + 1) * MLP_LOAD)
        w1_scr[:, cols] = stage1[c % 2].astype(BF16)
        w2_scr[cols, :] = stage2[c % 2].astype(BF16)


def _mlp_tail(x, mix, m_ref, gn2_ref, w1_ref, w2_ref, loading=None):
    y1 = x + m_ref[:, 2 * D_MODEL:3 * D_MODEL] * mix
    h = _norm_mod(y1, gn2_ref[...], m_ref[:, 3 * D_MODEL:4 * D_MODEL], m_ref[:, 4 * D_MODEL:5 * D_MODEL]).astype(BF16)
    nchunk = D_FF // MLP_CHUNK
    acc = jnp.zeros(y1.shape, F32)
    for c in range(nchunk):
        cols = slice(c * MLP_CHUNK, (c + 1) * MLP_CHUNK)
        if loading is not None:
            per = MLP_CHUNK // MLP_LOAD
            for p in range(c * per, (c + 1) * per):
                if p + 1 < D_FF // MLP_LOAD:
                    loading.start(p + 1)
                loading.finish(p)
        a = _dot(h, w1_ref[:, cols])
        a = jnp.square(jnp.maximum(a, 0.0)).astype(BF16)
        acc = acc + _dot(a, w2_ref[cols, :])
    return y1 + m_ref[:, 5 * D_MODEL:6 * D_MODEL] * acc


def _run_tail(x, mix, m_ref, gn2_ref, weights, w1_ref, w2_ref, emit):
    first = pl.program_id(0) == 0

    @pl.when(first)
    def _():
        emit(_mlp_tail(x, mix, m_ref, gn2_ref, w1_ref, w2_ref, loading=weights))

    @pl.when(jnp.logical_not(first))
    def _():
        emit(_mlp_tail(x, mix, m_ref, gn2_ref, w1_ref, w2_ref))


def _even_out_kernel(xp_ref, xs_ref, y5_ref, u_ref, dskip_ref, wglu_ref, bglu_ref, glap_ref, glas_ref, wout_ref,
                     m_ref, gn2_ref, w1_hbm, w2_hbm, o_ref, w1_ref, w2_ref, stage1, stage2, sem, *, layer):
    weights = _MlpWeights(w1_hbm, w2_hbm, w1_ref, w2_ref, stage1, stage2, sem, layer)

    @pl.when(pl.program_id(0) == 0)
    def _():
        weights.start(0)

    nblk = S5_WIDTH // 128
    ys = (jnp.concatenate([y5_ref[b] for b in range(nblk)], axis=1)
          + jnp.concatenate([u_ref[b] for b in range(nblk)], axis=1) * dskip_ref[...])
    gl = jax.nn.gelu(ys)
    s5o = gl * jax.nn.sigmoid(_dot(gl.astype(BF16), wglu_ref[...]) + bglu_ref[...])
    gla = _token_tile(glap_ref, glas_ref, _OUT_TM).astype(BF16)
    mix = _dot(s5o.astype(BF16), wout_ref[0:S5_WIDTH, :]) + _dot(gla, wout_ref[S5_WIDTH:, :])

    def emit(y):
        o_ref[...] = y

    _run_tail(_token_tile(xp_ref, xs_ref, _OUT_TM), mix, m_ref, gn2_ref, weights, w1_ref, w2_ref, emit)


def _odd_out_kernel(x_ref, attp_ref, atts_ref, wo_ref, m_ref, gn2_ref, w1_hbm, w2_hbm, op_ref, os_ref,
                    w1_ref, w2_ref, stage1, stage2, sem, *, layer):
    weights = _MlpWeights(w1_hbm, w2_hbm, w1_ref, w2_ref, stage1, stage2, sem, layer)

    @pl.when(pl.program_id(0) == 0)
    def _():
        weights.start(0)

    mix = _dot(_token_tile(attp_ref, atts_ref, _OUT_TM), wo_ref[...])
    is_prompt = pl.program_id(0) < T_PROMPT // _OUT_TM

    def emit(y):
        @pl.when(is_prompt)
        def _():
            op_ref[...] = y

        @pl.when(jnp.logical_not(is_prompt))
        def _():
            os_ref[...] = y

    _run_tail(x_ref[...], mix, m_ref, gn2_ref, weights, w1_ref, w2_ref, emit)


_OUT_TM = 512


def _const_spec(shape):
    return pl.BlockSpec(shape, lambda i: (0,) * len(shape), pipeline_mode=pl.Buffered(1))


def _tail_specs(layer):
    tm = _OUT_TM
    return [
        pl.BlockSpec((None, 1, 6 * D_MODEL), lambda i: (layer * COND_ROWS + _cond_row(i, tm), 0, 0)),
        _const_spec((1, D_MODEL)),
        pl.BlockSpec(memory_space=pl.ANY),
        pl.BlockSpec(memory_space=pl.ANY),
    ]


def _tail_scratch():
    return [pltpu.VMEM((D_MODEL, D_FF), BF16), pltpu.VMEM((D_FF, D_MODEL), BF16),
            pltpu.VMEM((2, D_MODEL, MLP_LOAD), F32), pltpu.VMEM((2, MLP_LOAD, D_MODEL), F32),
            pltpu.SemaphoreType.DMA((2, 2))]


_TAIL_PARAMS = dict(dimension_semantics=("arbitrary",), vmem_limit_bytes=VMEM_LIMIT)


def _even_out_call(xp, xs, y5, u, d_skip, w_glu, b_glu, gla_p, gla_s, w_out, mods, layer, gn2, w1, w2):
    tm = _OUT_TM
    return pl.pallas_call(
        functools.partial(_even_out_kernel, layer=layer),
        out_shape=jax.ShapeDtypeStruct((T_TOK, D_MODEL), F32),
        grid=(T_TOK // tm,),
        in_specs=_token_specs(tm) + [
            pl.BlockSpec((S5_WIDTH // 128, tm, 128), lambda i: (0, i, 0)),
            pl.BlockSpec((S5_WIDTH // 128, tm, 128), lambda i: (0, i, 0)),
            _const_spec((1, S5_WIDTH)),
            _const_spec((S5_WIDTH, S5_WIDTH)),
            _const_spec((1, S5_WIDTH)),
        ] + _token_specs(tm, GLA_VW) + [
            _const_spec((S5_WIDTH + GLA_VW, D_MODEL)),
        ] + _tail_specs(layer),
        out_specs=pl.BlockSpec((tm, D_MODEL), lambda i: (i, 0)),
        scratch_shapes=_tail_scratch(),
        compiler_params=pltpu.CompilerParams(**_TAIL_PARAMS),
        name="even_out_mlp",
    )(xp, xs, y5, u, d_skip, w_glu, b_glu, gla_p, gla_s, w_out, mods, gn2, w1, w2)


def _odd_out_call(x, att_p, att_s, w_o, mods, layer, gn2, w1, w2):
    tm = _OUT_TM
    return pl.pallas_call(
        functools.partial(_odd_out_kernel, layer=layer),
        out_shape=(jax.ShapeDtypeStruct((T_PROMPT, D_MODEL), F32),
                   jax.ShapeDtypeStruct((T_SAMPLE, D_MODEL), F32)),
        grid=(T_TOK // tm,),
        in_specs=[pl.BlockSpec((tm, D_MODEL), lambda i: (i, 0))] + _token_specs(tm) + [
            _const_spec((D_MODEL, D_MODEL)),
        ] + _tail_specs(layer),
        out_specs=tuple(_token_specs(tm)),
        scratch_shapes=_tail_scratch(),
        compiler_params=pltpu.CompilerParams(**_TAIL_PARAMS),
        name="odd_out_mlp",
    )(x, att_p, att_s, w_o, mods, gn2, w1, w2)


def _qkv_kernel(x_ref, gn_ref, m_ref, w_ref, qn_ref, kn_ref, cos_ref, sin_ref,
                q_ref, kb_ref, vb_ref, k32_ref, v32_ref, *, tile):
    h = _norm_mod(x_ref[...], gn_ref[...], m_ref[:, 0:D_MODEL], m_ref[:, D_MODEL:2 * D_MODEL]).astype(BF16)
    z = _dot(h, w_ref[...])
    v = z[:, (N_HEADS + KV_HEADS) * HEAD_DIM:]
    vb_ref[...] = v.astype(BF16)
    even_lane = (lax.broadcasted_iota(jnp.int32, (1, HEAD_DIM), 1) & 1) == 0

    def heads(rope):
        for hd in range(N_HEADS + KV_HEADS):
            xh = z[:, hd * HEAD_DIM:(hd + 1) * HEAD_DIM]
            gain = qn_ref[...] if hd < N_HEADS else kn_ref[...]
            xh = xh * lax.rsqrt(jnp.mean(xh * xh, axis=-1, keepdims=True) + EPS) * gain
            if rope:
                partner = jnp.where(even_lane, pltpu.roll(xh, HEAD_DIM - 1, 1), pltpu.roll(xh, 1, 1))
                xh = xh * cos_ref[...] + partner * sin_ref[...]
            if hd < N_HEADS:
                q_ref[:, hd * HEAD_DIM:(hd + 1) * HEAD_DIM] = xh.astype(BF16)
            else:
                cols = slice((hd - N_HEADS) * HEAD_DIM, (hd - N_HEADS + 1) * HEAD_DIM)
                kb_ref[:, cols] = xh.astype(BF16)
                if not rope:
                    k32_ref[:, hd - N_HEADS, :] = xh

    is_sample = pl.program_id(0) >= T_PROMPT // tile

    @pl.when(is_sample)
    def _():
        heads(True)

    @pl.when(jnp.logical_not(is_sample))
    def _():
        heads(False)
        for kh in range(KV_HEADS):
            v32_ref[:, kh, :] = v[:, kh * HEAD_DIM:(kh + 1) * HEAD_DIM]


def _qkv_call(x, gn, mods, layer, w_qkv, q_norm, k_norm, cos_t, sin_t):
    tm = 512
    pos_tiles = DEC_SEQ // tm
    n_prompt = T_PROMPT // tm
    kvw = KV_HEADS * HEAD_DIM

    def pos_map(i):
        return (jnp.maximum(i - n_prompt, 0) % pos_tiles, 0)

    def prompt_map(i):
        return (jnp.minimum(i, n_prompt - 1), 0, 0)

    return pl.pallas_call(
        functools.partial(_qkv_kernel, tile=tm),
        out_shape=(jax.ShapeDtypeStruct((T_TOK, N_HEADS * HEAD_DIM), BF16),
                   jax.ShapeDtypeStruct((T_TOK, kvw), BF16),
                   jax.ShapeDtypeStruct((T_TOK, kvw), BF16),
                   jax.ShapeDtypeStruct((T_PROMPT, KV_HEADS, HEAD_DIM), F32),
                   jax.ShapeDtypeStruct((T_PROMPT, KV_HEADS, HEAD_DIM), F32)),
        grid=(T_TOK // tm,),
        in_specs=[
            pl.BlockSpec((tm, D_MODEL), lambda i: (i, 0)),
            pl.BlockSpec((1, D_MODEL), lambda i: (0, 0)),
            pl.BlockSpec((None, 1, 6 * D_MODEL), lambda i: (layer * COND_ROWS + _cond_row(i, tm), 0, 0)),
            pl.BlockSpec(w_qkv.shape, lambda i: (0, 0)),
            pl.BlockSpec((1, HEAD_DIM), lambda i: (0, 0)),
            pl.BlockSpec((1, HEAD_DIM), lambda i: (0, 0)),
            pl.BlockSpec((tm, HEAD_DIM), pos_map),
            pl.BlockSpec((tm, HEAD_DIM), pos_map),
        ],
        out_specs=(pl.BlockSpec((tm, N_HEADS * HEAD_DIM), lambda i: (i, 0)),
                   pl.BlockSpec((tm, kvw), lambda i: (i, 0)),
                   pl.BlockSpec((tm, kvw), lambda i: (i, 0)),
                   pl.BlockSpec((tm, KV_HEADS, HEAD_DIM), prompt_map),
                   pl.BlockSpec((tm, KV_HEADS, HEAD_DIM), prompt_map)),
        compiler_params=pltpu.CompilerParams(vmem_limit_bytes=VMEM_LIMIT),
        name="odd_qkv",
    )(x, gn, mods, w_qkv, q_norm, k_norm, cos_t, sin_t)


def _rope_tables():
    rows = DEC_SEQ // GRID_W
    row = jnp.repeat(jnp.arange(rows, dtype=F32), GRID_W)
    col = jnp.tile(jnp.arange(GRID_W, dtype=F32), rows)
    inv = ROPE_THETA ** (-jnp.arange(0, AXIS_DIM, 2, dtype=F32) / AXIS_DIM)
    ang = jnp.concatenate([row[:, None] * inv, col[:, None] * inv], axis=-1)
    cos_t = jnp.repeat(jnp.cos(ang), 2, axis=-1)
    sin = jnp.sin(ang)
    sin_t = jnp.stack([-sin, sin], axis=-1).reshape(DEC_SEQ, HEAD_DIM)
    return cos_t, sin_t


def _attn_kernel(*refs, has_cache):
    q_ref, k_ref, v_ref = refs[:3]
    ck_ref, cv_ref = refs[3:5] if has_cache else (None, None)
    o_ref = refs[-1]
    c = HEAD_DIM ** -0.5 * math.log2(math.e)
    ones_col = (lax.broadcasted_iota(jnp.int32, (1, HEAD_DIM), 1) == 0).astype(BF16)

    def with_ones(v):
        return jnp.concatenate([v, jnp.broadcast_to(ones_col, v.shape)], axis=1)

    k = k_ref[...]
    v = with_ones(v_ref[...])
    if has_cache:
        ck = ck_ref[...].astype(BF16)
        cv = with_ones(cv_ref[...].astype(BF16))
    for r in range(Q_PER_KV):
        cs = slice(r * HEAD_DIM, (r + 1) * HEAD_DIM)
        q = q_ref[:, cs]
        s = lax.dot_general(q, k, NT_DIMS, preferred_element_type=F32)
        m = jnp.max(s, axis=-1, keepdims=True)
        if has_cache:
            sc = lax.dot_general(q, ck, NT_DIMS, preferred_element_type=F32)
            m = jnp.maximum(m, jnp.max(sc, axis=-1, keepdims=True))
        mc = m * c
        o = _dot(jnp.exp2(s * c - mc).astype(BF16), v)
        if has_cache:
            o = o + _dot(jnp.exp2(sc * c - mc).astype(BF16), cv)
        o_ref[:, cs] = (o[:, 0:HEAD_DIM] / o[:, HEAD_DIM:HEAD_DIM + 1]).astype(BF16)


def _attn_call(q, k, v, cache_k, cache_v, seq_len, row0, nrows):
    assert row0 % seq_len == 0 and nrows % seq_len == 0
    has_cache = cache_k is not None
    b0 = row0 // seq_len
    gw = Q_PER_KV * HEAD_DIM
    in_specs = [
        pl.BlockSpec((seq_len, gw), lambda b, g: (b0 + b, g)),
        pl.BlockSpec((seq_len, HEAD_DIM), lambda b, g: (b0 + b, g)),
        pl.BlockSpec((seq_len, HEAD_DIM), lambda b, g: (b0 + b, g)),
    ]
    args = [q, k, v]
    if has_cache:
        in_specs += [pl.BlockSpec((PAST_LEN, HEAD_DIM), lambda b, g: (b, g)),
                     pl.BlockSpec((PAST_LEN, HEAD_DIM), lambda b, g: (b, g))]
        args += [cache_k, cache_v]
    return pl.pallas_call(
        functools.partial(_attn_kernel, has_cache=has_cache),
        out_shape=jax.ShapeDtypeStruct((nrows, N_HEADS * HEAD_DIM), BF16),
        grid=(nrows // seq_len, KV_HEADS),
        in_specs=in_specs,
        out_specs=pl.BlockSpec((seq_len, gw), lambda b, g: (b, g)),
        compiler_params=pltpu.CompilerParams(vmem_limit_bytes=VMEM_LIMIT),
        name=f"attn_len{seq_len}",
    )(*args)


def kernel(x_prompt, x_sample, state_s5_re, state_s5_im, state_gla, cache_k, cache_v, c, c_ctx, norm_mix, norm_mlp, w_ada, b_ada, w_mlp_in, w_mlp_out, w_in_e, w_out_e, s5_lambda_re, s5_lambda_im, s5_log_dt, s5_b_re, s5_b_im, s5_c_re, s5_c_im, s5_d, s5_w_glu, s5_b_glu, gla_w_gate2, gla_b_gate, gla_norm, w_qkv_o, w_o_o, q_norm, k_norm):
    xp = x_prompt.reshape(T_PROMPT, D_MODEL)
    xs = x_sample.reshape(T_SAMPLE, D_MODEL)
    cond8 = jnp.concatenate([c_ctx[None, :], c, jnp.zeros((COND_ROWS - 1 - DEC_BATCH, D_MODEL), F32)], axis=0)
    mods = _ada_call(cond8, w_ada, b_ada).reshape(DEPTH * COND_ROWS, 1, 6 * D_MODEL)
    w1_all, w2_all = w_mlp_in, w_mlp_out

    w_gate = jnp.zeros((GLR_PAD, 2 * GLA_QK), F32)
    w_gate = w_gate.at[0:GLA_RANK, 0:GLA_QK].set(gla_w_gate2[0, 0])
    w_gate = w_gate.at[GLA_RANK:2 * GLA_RANK, GLA_QK:].set(gla_w_gate2[0, 1]).astype(BF16)
    b_gate = gla_b_gate[0].reshape(1, 2 * GLA_QK)
    u, z, g = _inproj_call(xp, xs, norm_mix[0:1], mods, 0, w_in_e[0], w_gate, b_gate)

    mats = _s5_prep_call(s5_lambda_re[0], s5_lambda_im[0], s5_log_dt[0], s5_b_re[0], s5_b_im[0],
                         s5_c_re[0], s5_c_im[0])

    def state_rows(s):
        return jnp.transpose(s, (2, 0, 1, 3)).reshape(S5_GROUPS, DEC_BATCH, 2 * S5_STATE)

    h0 = jnp.concatenate([state_rows(state_s5_re[:, 0]), state_rows(state_s5_im[:, 0])], axis=-1)
    nsteps = S5_GROUPS // S5_GPB
    h0 = jnp.transpose(h0.reshape(nsteps, S5_GPB, DEC_BATCH, S5_W), (0, 2, 1, 3))
    y5, ns = _s5_call(u, mats, h0)
    ns = jnp.transpose(ns, (0, 2, 1, 3)).reshape(S5_GROUPS, BATCH, S5_W)

    def state_out(n):
        return jnp.transpose(n.reshape(S5_GROUPS, BATCH, 2, S5_STATE), (1, 2, 0, 3))[:, None]

    new_s5_re = state_out(ns[:, :, :2 * S5_STATE])
    new_s5_im = state_out(ns[:, :, 2 * S5_STATE:])

    gn_gla = gla_norm[0].reshape(1, GLA_DV)
    gla_p, sfin = _gla_call(z, g, gn_gla, None, SEQ, BATCH, 0)
    s0 = state_gla[:, 0].reshape(DEC_BATCH, 2, GLA_QK, GLA_DV)
    gla_s, _ = _gla_call(z, g, gn_gla, s0, DEC_SEQ, DEC_BATCH, T_PROMPT)
    new_gla = sfin.reshape(BATCH, 1, 2, GLA_HEADS, GLA_DK, GLA_DV)

    x = _even_out_call(xp, xs, y5, u, s5_d[0].reshape(1, S5_WIDTH), s5_w_glu[0].astype(BF16),
                       s5_b_glu[0].reshape(1, S5_WIDTH), gla_p, gla_s, w_out_e[0].astype(BF16), mods, 0,
                       norm_mlp[0:1], w1_all, w2_all)

    cos_t, sin_t = _rope_tables()
    q, k, v, k32, v32 = _qkv_call(x, norm_mix[1:2], mods, 1, w_qkv_o[0].astype(BF16),
                                  q_norm[0].reshape(1, HEAD_DIM), k_norm[0].reshape(1, HEAD_DIM), cos_t, sin_t)
    att_p = _attn_call(q, k, v, None, None, SEQ, 0, T_PROMPT)
    ck = cache_k[:, 0].reshape(DEC_BATCH * PAST_LEN, KV_HEADS * HEAD_DIM)
    cv = cache_v[:, 0].reshape(DEC_BATCH * PAST_LEN, KV_HEADS * HEAD_DIM)
    att_s = _attn_call(q, k, v, ck, cv, DEC_SEQ, T_PROMPT, T_SAMPLE)
    yp, ys = _odd_out_call(x, att_p, att_s, w_o_o[0].astype(BF16), mods, 1, norm_mlp[1:2],
                           w1_all, w2_all)

    new_k = k32.reshape(BATCH, 1, SEQ, KV_HEADS, HEAD_DIM)
    new_v = v32.reshape(BATCH, 1, SEQ, KV_HEADS, HEAD_DIM)
    y_prompt = yp.reshape(BATCH, SEQ, D_MODEL)
    y_sample = ys.reshape(DEC_BATCH, DEC_SEQ, D_MODEL)
    return (y_prompt, y_sample, new_s5_re, new_s5_im, new_gla, new_k, new_v)
```

```python
import functools
import math

import jax
import jax.numpy as jnp
import numpy as np
from jax import lax
from jax.experimental import pallas as pl
from jax.experimental.pallas import tpu as pltpu

F32 = jnp.float32
BF16 = jnp.bfloat16

D_MODEL = 1024
BATCH = 16
SEQ = 256
DEPTH = 2
DEC_BATCH = 4
DEC_SEQ = 1024
PAST_LEN = 512
GRID_W = 64
S5_WIDTH = 512
S5_GROUP_CH = 16
S5_GROUPS = 32
S5_STATE = 64
GLA_HEADS = 4
GLA_VW = 512
GLA_DV = 128
GLA_DK = 64
GLA_QK = 256
GLA_RANK = 16
GLA_TAU = 16.0
GLA_CHUNK = 64
GLA_CPB = 4
GLA_BLK = GLA_CPB * GLA_CHUNK
HEAD_DIM = 128
N_HEADS = 8
KV_HEADS = 2
Q_PER_KV = N_HEADS // KV_HEADS
AXIS_DIM = 64
ROPE_THETA = 10000.0
D_FF = 4096
EPS = 1e-6

T_PROMPT = BATCH * SEQ
T_SAMPLE = DEC_BATCH * DEC_SEQ
T_TOK = T_PROMPT + T_SAMPLE
COND_ROWS = 8
COND_SPAN = 1024
PROMPT_SPANS = T_PROMPT // COND_SPAN

S5_Q = 16
S5_W = S5_Q * S5_GROUP_CH
S5_GPB = 128 // S5_GROUP_CH
S5_ROWS = T_TOK // S5_Q
S5_PROMPT_ROWS = T_PROMPT // S5_Q
S5_PROMPT_CHUNKS = SEQ // S5_Q
S5_SAMPLE_CHUNKS = DEC_SEQ // S5_Q

VMEM_LIMIT = 56 * 1024 * 1024

NT_DIMS = (((1,), (1,)), ((), ()))
TN_DIMS = (((0,), (0,)), ((), ()))


def _cond_row(i, tile):
    return jnp.maximum((i * tile) // COND_SPAN - (PROMPT_SPANS - 1), 0)


def _norm_mod(x, gain, shift, scale):
    y = x * lax.rsqrt(jnp.mean(x * x, axis=-1, keepdims=True) + EPS)
    return (y * gain) * (1.0 + scale) + shift


def _dot(a, b):
    return jnp.dot(a, b, preferred_element_type=F32)


def _ada_kernel(cond_ref, w_ref, b_ref, o_ref):
    s = jax.nn.silu(cond_ref[...]).astype(BF16)
    o_ref[:, 0, :] = _dot(s, w_ref[...].astype(BF16)) + b_ref[...]


def _ada_call(cond8, w_ada, b_ada):
    tn = 2048
    nj = 6 * D_MODEL // tn
    return pl.pallas_call(
        _ada_kernel,
        out_shape=jax.ShapeDtypeStruct((DEPTH * COND_ROWS, 1, 6 * D_MODEL), F32),
        grid=(DEPTH, nj),
        in_specs=[
            pl.BlockSpec((COND_ROWS, D_MODEL), lambda l, j: (0, 0)),
            pl.BlockSpec((None, D_MODEL, tn), lambda l, j: (l, 0, j)),
            pl.BlockSpec((None, 1, tn), lambda l, j: (l, 0, j)),
        ],
        out_specs=pl.BlockSpec((COND_ROWS, 1, tn), lambda l, j: (l, 0, j)),
        compiler_params=pltpu.CompilerParams(vmem_limit_bytes=VMEM_LIMIT),
        name="ada_mod",
    )(cond8, w_ada, b_ada.reshape(DEPTH, 1, 6 * D_MODEL))


def _token_specs(tile, width=D_MODEL):
    n_prompt = T_PROMPT // tile
    return [pl.BlockSpec((tile, width), lambda i: (jnp.minimum(i, n_prompt - 1), 0)),
            pl.BlockSpec((tile, width), lambda i: (jnp.maximum(i - n_prompt, 0), 0))]


def _token_tile(xp_ref, xs_ref, tile):
    return jnp.where(pl.program_id(0) < T_PROMPT // tile, xp_ref[...], xs_ref[...])


def _inproj_kernel(xp_ref, xs_ref, gn_ref, m_ref, w_ref, wglr_ref, wg_ref, bg_ref, u_ref, z_ref, g_ref, *, tile):
    x = _token_tile(xp_ref, xs_ref, tile)
    h = _norm_mod(x, gn_ref[...], m_ref[:, 0:D_MODEL], m_ref[:, D_MODEL:2 * D_MODEL]).astype(BF16)
    z = _dot(h, w_ref[...])
    for blk in range(S5_WIDTH // 128):
        u_ref[blk] = z[:, blk * 128:(blk + 1) * 128]
    z_ref[...] = z[:, S5_WIDTH:]
    glr = _dot(h, wglr_ref[...]).astype(BF16)
    pre = _dot(glr, wg_ref[...]) + bg_ref[...]
    g_ref[...] = jax.nn.log_sigmoid(pre) * (1.0 / GLA_TAU)


def _inproj_call(xp, xs, gn, mods, layer, w_main, w_glr, w_gate, b_gate):
    tm = 512
    nz = w_main.shape[1]
    return pl.pallas_call(
        functools.partial(_inproj_kernel, tile=tm),
        out_shape=(jax.ShapeDtypeStruct((S5_WIDTH // 128, T_TOK, 128), F32),
                   jax.ShapeDtypeStruct((T_TOK, nz - S5_WIDTH), F32),
                   jax.ShapeDtypeStruct((T_TOK, 2 * GLA_QK), F32)),
        grid=(T_TOK // tm,),
        in_specs=_token_specs(tm) + [
            pl.BlockSpec((1, D_MODEL), lambda i: (0, 0)),
            pl.BlockSpec((None, 1, 6 * D_MODEL), lambda i: (layer * COND_ROWS + _cond_row(i, tm), 0, 0)),
            pl.BlockSpec((D_MODEL, nz), lambda i: (0, 0)),
            pl.BlockSpec((D_MODEL, 128), lambda i: (0, 0)),
            pl.BlockSpec((128, 2 * GLA_QK), lambda i: (0, 0)),
            pl.BlockSpec((1, 2 * GLA_QK), lambda i: (0, 0)),
        ],
        out_specs=(pl.BlockSpec((S5_WIDTH // 128, tm, 128), lambda i: (0, i, 0)),
                   pl.BlockSpec((tm, nz - S5_WIDTH), lambda i: (i, 0)),
                   pl.BlockSpec((tm, 2 * GLA_QK), lambda i: (i, 0))),
        compiler_params=pltpu.CompilerParams(vmem_limit_bytes=VMEM_LIMIT),
        name="even_inproj",
    )(xp, xs, gn, mods, w_main, w_glr, w_gate, b_gate)


S5_PREP_GPB = 8
_PREP_LRE, _PREP_LIM, _PREP_LDT = 0, 1, 2
_PREP_BT_RE, _PREP_BT_IM, _PREP_C_RE, _PREP_C_IM, _PREP_ROWS = 8, 24, 40, 56, 72


def _s5_prep_kernel(p_ref, cc_ref, t_ref, bq_ref, cqt_ref, be_ref, a_ref, t_scr, dd_scr):
    for gi in range(S5_PREP_GPB):
        _s5_prep_group(p_ref.at[gi], cc_ref.at[gi], t_ref.at[gi], bq_ref.at[gi], cqt_ref.at[gi], be_ref.at[gi],
                       a_ref.at[gi], t_scr, dd_scr)


def _s5_prep_group(p_ref, cc_ref, t_ref, bq_ref, cqt_ref, be_ref, a_ref, t_scr, dd_scr):
    gch = S5_GROUP_CH
    lre = p_ref[_PREP_LRE:_PREP_LRE + 1]
    lim = p_ref[_PREP_LIM:_PREP_LIM + 1]
    dt = jnp.exp(p_ref[_PREP_LDT:_PREP_LDT + 1])
    a = lre * dt
    th = lim * dt

    def lam_pow(k):
        mag = jnp.exp(k * a)
        return mag * jnp.cos(k * th), mag * jnp.sin(k * th)

    lb_re, lb_im = lam_pow(1.0)
    nr = lb_re - 1.0
    den = lre * lre + lim * lim
    cf_re = (nr * lre + lb_im * lim) / den
    cf_im = (lb_im * lre - nr * lim) / den
    bt_re = p_ref[_PREP_BT_RE:_PREP_BT_RE + gch]
    bt_im = p_ref[_PREP_BT_IM:_PREP_BT_IM + gch]
    bb_re = jnp.tile(cf_re * bt_re - cf_im * bt_im, (S5_Q, 1))
    bb_im = jnp.tile(cf_re * bt_im + cf_im * bt_re, (S5_Q, 1))

    shape = (S5_W, 128)
    pos = lax.shift_right_logical(lax.broadcasted_iota(jnp.int32, shape, 0), 4)
    is_f = lax.broadcasted_iota(jnp.int32, shape, 1) < S5_STATE
    posq = lax.broadcasted_iota(jnp.int32, (S5_Q, 128), 0).astype(F32)
    is_fq = lax.broadcasted_iota(jnp.int32, (S5_Q, 128), 1) < S5_STATE

    def per_channel(tbl):
        return jnp.broadcast_to(tbl[:, None, :], (S5_Q, S5_GROUP_CH, 128)).reshape(shape)

    p_re, p_im = map(per_channel, lam_pow(jnp.where(is_fq, (S5_Q - 1.0) - posq, posq)))
    w_re = p_re * bb_re - p_im * bb_im
    w_im = p_re * bb_im + p_im * bb_re
    bq = jnp.concatenate([w_re, w_im], axis=1)
    bqt = jnp.transpose(bq)
    bq_ref[...] = bqt.astype(BF16)

    edge = pos == jnp.where(is_f, 0, S5_Q - 1)
    be = jnp.concatenate([jnp.where(edge, bb_re, 0.0), jnp.where(edge, bb_im, 0.0)], axis=1)
    be_ref[...] = jnp.transpose(be).astype(BF16)

    q_re, q_im = map(per_channel, lam_pow(jnp.where(is_fq, posq + 1.0, S5_Q - posq)))
    ct_re = jnp.tile(p_ref[_PREP_C_RE:_PREP_C_RE + gch], (S5_Q, 1))
    ct_im = jnp.tile(p_ref[_PREP_C_IM:_PREP_C_IM + gch], (S5_Q, 1))
    g_re = q_re * ct_re - q_im * ct_im
    g_im = q_re * ct_im + q_im * ct_re
    cqt_ref[...] = jnp.concatenate([g_re, -g_im], axis=1).astype(BF16)

    a_re, a_im = lam_pow(float(S5_Q))
    a_ref[...] = jnp.concatenate([a_re, a_im], axis=1)

    kf = jnp.dot(cc_ref[0:gch], bqt, precision=lax.Precision.HIGHEST, preferred_element_type=F32)
    kb = jnp.dot(cc_ref[gch:2 * gch], bqt, precision=lax.Precision.HIGHEST, preferred_element_type=F32)
    lo = S5_W - gch
    dd_scr[:, 0:S5_W] = kf
    dd_scr[:, lo:lo + S5_W] = kb
    dd_scr[:, lo:S5_W] = kf[:, lo:S5_W] + kb[:, 0:gch]
    for t in range(S5_Q):
        c0 = (S5_Q - 1 - t) * gch
        t_scr[t * gch:(t + 1) * gch, :] = dd_scr[:, c0:c0 + S5_W]
    t_ref[...] = t_scr[...].astype(BF16)


def _s5_prep_call(lam_re, lam_im, log_dt, b_re, b_im, c_re, c_im):
    def fb(p):
        return jnp.transpose(p, (1, 0, 2)).reshape(S5_GROUPS, 1, 2 * S5_STATE)

    def dup(p):
        return jnp.concatenate([p, p], axis=-1)

    ldt = fb(jnp.broadcast_to(log_dt[:, :, None], (2, S5_GROUPS, S5_STATE)))
    pad = jnp.zeros((S5_GROUPS, _PREP_BT_RE - _PREP_LDT - 1, 128), F32)
    packed = jnp.concatenate([fb(lam_re), fb(lam_im), ldt, pad,
                              dup(jnp.transpose(b_re, (0, 2, 1))), dup(jnp.transpose(b_im, (0, 2, 1))),
                              dup(c_re), dup(c_im)], axis=1)
    zero = jnp.zeros_like(c_re)
    cc = jnp.concatenate([jnp.concatenate([c_re, zero, -c_im, zero], axis=-1),
                          jnp.concatenate([zero, c_re, zero, -c_im], axis=-1)], axis=1)

    gpb = S5_PREP_GPB
    sq = pl.BlockSpec((gpb, S5_W, S5_W), lambda g: (g, 0, 0))
    sq_shape = jax.ShapeDtypeStruct((S5_GROUPS, S5_W, S5_W), BF16)
    return pl.pallas_call(
        _s5_prep_kernel,
        out_shape=(sq_shape, sq_shape, sq_shape, sq_shape,
                   jax.ShapeDtypeStruct((S5_GROUPS, 1, S5_W), F32)),
        grid=(S5_GROUPS // gpb,),
        in_specs=[pl.BlockSpec((gpb, _PREP_ROWS, 128), lambda g: (g, 0, 0)),
                  pl.BlockSpec((gpb, 2 * S5_GROUP_CH, S5_W), lambda g: (g, 0, 0))],
        out_specs=(sq, sq, sq, sq, pl.BlockSpec((gpb, 1, S5_W), lambda g: (g, 0, 0))),
        scratch_shapes=[pltpu.VMEM((S5_W, S5_W), F32), pltpu.VMEM((S5_GROUP_CH, 2 * S5_W), F32)],
        name="s5_prep",
    )(packed, cc)


def _s5_kernel(u_ref, tt_ref, bqt_ref, cqt_ref, bet_ref, a_ref, h0_ref, y_ref, ns_ref,
               ut_scr, x_scr, spf_scr, spb_scr, ne_scr, yt_scr, xt_scr):
    gch = S5_GROUP_CH
    for s in range(S5_Q):
        rows = u_ref[pl.ds(s, S5_ROWS, stride=S5_Q), :]
        rows_t = jnp.transpose(rows).astype(BF16)
        for gl in range(S5_GPB):
            ut_scr[gl, s * gch:(s + 1) * gch, :] = rows_t[gl * gch:(gl + 1) * gch, :]

    for gl in range(S5_GPB):
        ut = ut_scr[gl]
        xt_scr[...] = _dot(bqt_ref[gl], ut)
        x = jnp.transpose(xt_scr[...])
        xt_scr[:, 0:S5_PROMPT_ROWS] = _dot(bet_ref[gl], ut[:, 0:S5_PROMPT_ROWS])
        ne = jnp.transpose(xt_scr[:, 0:S5_PROMPT_ROWS])
        for part in range(2):
            x_scr[part, pl.ds(gl, S5_ROWS, stride=S5_GPB), :] = x[:, part * 128:(part + 1) * 128]
            ne_scr[part, pl.ds(gl, S5_PROMPT_ROWS, stride=S5_GPB), :] = ne[:, part * 128:(part + 1) * 128]

    is_f = lax.broadcasted_iota(jnp.int32, (1, 128), 1) < S5_STATE
    a_re = a_ref[:, 0:128]
    a_im = a_ref[:, 128:256]

    def tile(row):
        return pl.ds(pl.multiple_of(row * S5_GPB, S5_GPB), S5_GPB)

    def scan(base, nseq, nchunk, s_init):
        def body(i, state):
            new = []
            for b in range(nseq):
                s_re, s_im = state[b]
                rows_f = tile(base + b * nchunk + i)
                rows_b = tile(base + b * nchunk + (nchunk - 1 - i))
                spf_scr[0, rows_f, :] = s_re
                spf_scr[1, rows_f, :] = s_im
                spb_scr[0, rows_b, :] = s_re
                spb_scr[1, rows_b, :] = s_im
                x_re = jnp.where(is_f, x_scr[0, rows_f, :], x_scr[0, rows_b, :])
                x_im = jnp.where(is_f, x_scr[1, rows_f, :], x_scr[1, rows_b, :])
                new.append((a_re * s_re - a_im * s_im + x_re, a_re * s_im + a_im * s_re + x_im))
            return tuple(new)

        lax.fori_loop(0, nchunk, body, tuple(s_init))

    zero = jnp.zeros((S5_GPB, 128), F32)
    scan(0, BATCH, S5_PROMPT_CHUNKS, [(zero, zero)] * BATCH)
    scan(S5_PROMPT_ROWS, DEC_BATCH, S5_SAMPLE_CHUNKS,
         [(h0_ref[b, :, 0:128], h0_ref[b, :, 128:256]) for b in range(DEC_BATCH)])

    for b in range(BATCH):
        first = pl.ds(b * S5_PROMPT_CHUNKS * S5_GPB, S5_GPB)
        last = pl.ds(((b + 1) * S5_PROMPT_CHUNKS - 1) * S5_GPB, S5_GPB)
        for part in range(2):
            ns_ref[b, :, part * 128:(part + 1) * 128] = jnp.where(is_f, ne_scr[part, first, :], ne_scr[part, last, :])

    for gl in range(S5_GPB):
        rows = pl.ds(gl, S5_ROWS, stride=S5_GPB)
        carried = jnp.concatenate([jnp.where(is_f, spf_scr[p, rows, :], spb_scr[p, rows, :]) for p in range(2)],
                                  axis=1).astype(BF16)
        yt = _dot(tt_ref[gl], ut_scr[gl]) + lax.dot_general(cqt_ref[gl], carried, NT_DIMS,
                                                            preferred_element_type=F32)
        for t in range(S5_Q):
            yt_scr[t, gl * gch:(gl + 1) * gch, :] = yt[t * gch:(t + 1) * gch, :]
    for t in range(S5_Q):
        y_ref[pl.ds(t, S5_ROWS, stride=S5_Q), :] = jnp.transpose(yt_scr[t])


def _s5_call(u, mats, h0):
    tt_m, bqt_m, cqt_m, bet_m, a_m = mats
    nsteps = S5_GROUPS // S5_GPB
    sq = pl.BlockSpec((S5_GPB, S5_W, S5_W), lambda g: (g, 0, 0))
    state_scr = pltpu.VMEM((2, S5_ROWS * S5_GPB, 128), F32)
    return pl.pallas_call(
        _s5_kernel,
        out_shape=(jax.ShapeDtypeStruct((nsteps, T_TOK, 128), F32),
                   jax.ShapeDtypeStruct((nsteps, BATCH, S5_GPB, S5_W), F32)),
        grid=(nsteps,),
        in_specs=[
            pl.BlockSpec((None, T_TOK, 128), lambda g: (g, 0, 0)),
            sq, sq, sq, sq,
            pl.BlockSpec((S5_GPB, S5_W), lambda g: (g, 0)),
            pl.BlockSpec((None, DEC_BATCH, S5_GPB, S5_W), lambda g: (g, 0, 0, 0)),
        ],
        out_specs=(pl.BlockSpec((None, T_TOK, 128), lambda g: (g, 0, 0)),
                   pl.BlockSpec((None, BATCH, S5_GPB, S5_W), lambda g: (g, 0, 0, 0))),
        scratch_shapes=[pltpu.VMEM((S5_GPB, S5_W, S5_ROWS), BF16), state_scr, state_scr, state_scr,
                        pltpu.VMEM((2, S5_PROMPT_ROWS * S5_GPB, 128), F32),
                        pltpu.VMEM((S5_Q, 128, S5_ROWS), F32), pltpu.VMEM((S5_W, S5_ROWS), F32)],
        compiler_params=pltpu.CompilerParams(vmem_limit_bytes=VMEM_LIMIT),
        name="s5_scan",
    )(u, tt_m, bqt_m, cqt_m, bet_m, a_m.reshape(S5_GROUPS, S5_W), h0)


def _split_bf16(x):
    hi = x.astype(BF16)
    r1 = x - hi.astype(F32)
    mid = r1.astype(BF16)
    lo = (r1 - mid.astype(F32)).astype(BF16)
    return hi, mid, lo


def _gla_kernel(*refs, seq_len, has_s0):
    q_ref, k_ref, v_ref, gf_ref, gb_ref, r_ref, gn_ref = refs[:7]
    s0_ref = refs[7] if has_s0 else None
    o_ref, sfin_ref, oi_scr, qd_scr, kv_scr, dec_scr, ss_scr = refs[7 + has_s0:]
    nblk = seq_len // GLA_BLK
    nchunk = seq_len // GLA_CHUNK
    cl = GLA_CHUNK
    ti = lax.broadcasted_iota(jnp.int32, (GLA_BLK, GLA_BLK), 0)
    si = lax.broadcasted_iota(jnp.int32, (GLA_BLK, GLA_BLK), 1)
    same = lax.shift_right_logical(ti, 6) == lax.shift_right_logical(si, 6)
    keep = (same & (ti >= si), same & (ti <= si))
    tri = tuple(kp.astype(BF16) for kp in keep)
    lane_head = lax.shift_right_logical(lax.broadcasted_iota(jnp.int32, (cl, GLA_QK), 1), 6)
    zeros_v = jnp.zeros((cl, GLA_DV), BF16)
    heads = [(slice(h * GLA_DK, (h + 1) * GLA_DK), slice(h * GLA_DV, (h + 1) * GLA_DV)) for h in range(GLA_HEADS)]

    for j in range(nblk):
        rows = slice(j * GLA_BLK, (j + 1) * GLA_BLK)
        q = q_ref[rows, :] * (GLA_DK ** -0.5)
        k = k_ref[rows, :]
        v = v_ref[rows, :].astype(BF16)
        qd, kd, k2t = [], [], []
        for d, g_ref in enumerate((gf_ref, gb_ref)):
            b = sum(_dot(tri[d], part) for part in _split_bf16(g_ref[rows, :]))
            last = cl - 1 if d == 0 else 0
            b_last = [b[c * cl + last:c * cl + last + 1] for c in range(GLA_CPB)]
            bl = jnp.concatenate([jnp.broadcast_to(x, (cl, GLA_QK)) for x in b_last], axis=0)
            qd_d = (q * jnp.exp(b)).astype(BF16)
            qd_scr[d, rows, :] = qd_d
            qd.append(qd_d)
            kd.append((k * jnp.exp(-b)).astype(BF16))
            k2t.append(jnp.transpose(k * jnp.exp(bl - b)).astype(BF16))
            for c in range(GLA_CPB):
                dec_scr[d, j * GLA_CPB + c] = jnp.exp(jnp.transpose(jnp.broadcast_to(b_last[c], (GLA_DV, GLA_QK))))
        for h, (ks, vs) in enumerate(heads):
            att = [jnp.where(keep[d], lax.dot_general(qd[d][:, ks], kd[d][:, ks], NT_DIMS,
                                                      preferred_element_type=F32), 0.0) for d in range(2)]
            oi_scr[rows, vs] = _dot((att[0] + att[1]).astype(BF16), v[:, vs])
            vh = v[:, vs]
            vexp = jnp.concatenate(
                [jnp.concatenate([vh[c * cl:(c + 1) * cl] if c2 == c else zeros_v for c2 in range(GLA_CPB)], axis=1)
                 for c in range(GLA_CPB)], axis=0)
            for d in range(2):
                kv_scr[d, j, h] = _dot(k2t[d][ks, :], vexp)

    for d in range(2):
        s = s0_ref[d] if has_s0 else jnp.zeros((GLA_QK, GLA_DV), F32)
        for cg in (range(nchunk) if d == 0 else range(nchunk - 1, -1, -1)):
            j, c = divmod(cg, GLA_CPB)
            ss_scr[d, cg] = s.astype(BF16)
            kv = jnp.concatenate([kv_scr[d, j, h, :, c * GLA_DV:(c + 1) * GLA_DV] for h in range(GLA_HEADS)], axis=0)
            s = s * dec_scr[d, cg] + kv
        sfin_ref[d] = s

    for cg in range(nchunk):
        rows = slice(cg * cl, (cg + 1) * cl)
        inter = []
        for d in range(2):
            qc = qd_scr[d, rows, :]
            qstack = jnp.concatenate([jnp.where(lane_head == h, qc, jnp.zeros_like(qc)) for h in range(GLA_HEADS)],
                                     axis=0)
            inter.append(_dot(qstack, ss_scr[d, cg]))
        gate = jax.nn.silu(r_ref[rows, :])
        for h, (ks, vs) in enumerate(heads):
            hr = slice(h * cl, (h + 1) * cl)
            oh = oi_scr[rows, vs] + inter[0][hr] + inter[1][hr]
            oh = oh * lax.rsqrt(jnp.mean(oh * oh, axis=-1, keepdims=True) + EPS) * gn_ref[...]
            o_ref[rows, vs] = oh * gate[:, vs]


def _gla_call(z, g, gla_norm, s0, seq_len, nseq, row0):
    assert row0 % seq_len == 0
    r0 = row0 // seq_len
    has_s0 = s0 is not None
    qk_off = 0
    v_off = 2 * GLA_QK // GLA_VW
    in_specs = [
        pl.BlockSpec((seq_len, GLA_QK), lambda i: (r0 + i, qk_off)),
        pl.BlockSpec((seq_len, GLA_QK), lambda i: (r0 + i, qk_off + 1)),
        pl.BlockSpec((seq_len, GLA_VW), lambda i: (r0 + i, v_off)),
        pl.BlockSpec((seq_len, GLA_QK), lambda i: (r0 + i, 0)),
        pl.BlockSpec((seq_len, GLA_QK), lambda i: (r0 + i, 1)),
        pl.BlockSpec((seq_len, GLA_VW), lambda i: (r0 + i, v_off + 1)),
        pl.BlockSpec((1, GLA_DV), lambda i: (0, 0)),
    ]
    args = [z, z, z, g, g, z, gla_norm]
    if has_s0:
        in_specs.append(pl.BlockSpec((None, 2, GLA_QK, GLA_DV), lambda i: (i, 0, 0, 0)))
        args.append(s0)
    return pl.pallas_call(
        functools.partial(_gla_kernel, seq_len=seq_len, has_s0=has_s0),
        out_shape=(jax.ShapeDtypeStruct((nseq * seq_len, GLA_VW), F32),
                   jax.ShapeDtypeStruct((nseq, 2, GLA_QK, GLA_DV), F32)),
        grid=(nseq,),
        in_specs=in_specs,
        out_specs=(pl.BlockSpec((seq_len, GLA_VW), lambda i: (i, 0)),
                   pl.BlockSpec((None, 2, GLA_QK, GLA_DV), lambda i: (i, 0, 0, 0))),
        scratch_shapes=[
            pltpu.VMEM((seq_len, GLA_VW), F32),
            pltpu.VMEM((2, seq_len, GLA_QK), BF16),
            pltpu.VMEM((2, seq_len // GLA_BLK, GLA_HEADS, GLA_DK, GLA_CPB * GLA_DV), F32),
            pltpu.VMEM((2, seq_len // GLA_CHUNK, GLA_QK, GLA_DV), F32),
            pltpu.VMEM((2, seq_len // GLA_CHUNK, GLA_QK, GLA_DV), BF16),
        ],
        compiler_params=pltpu.CompilerParams(vmem_limit_bytes=VMEM_LIMIT),
        name=f"gla_len{seq_len}",
    )(*args)


MLP_CHUNK = 512
MLP_LOAD = 256


class _MlpWeights:
    def __init__(self, w1_hbm, w2_hbm, w1_scr, w2_scr, stage1, stage2, sem, layer):
        self.refs = (w1_hbm, w2_hbm, w1_scr, w2_scr, stage1, stage2, sem)
        self.layer = layer

    def _copies(self, c):
        w1_hbm, w2_hbm, _, _, stage1, stage2, sem = self.refs
        cols = pl.ds(c * MLP_LOAD, MLP_LOAD)
        return (pltpu.make_async_copy(w1_hbm.at[self.layer, :, cols], stage1.at[c % 2], sem.at[0, c % 2]),
                pltpu.make_async_copy(w2_hbm.at[self.layer, cols, :], stage2.at[c % 2], sem.at[1, c % 2]))

    def start(self, c):
        for cp in self._copies(c):
            cp.start()

    def finish(self, c):
        _, _, w1_scr, w2_scr, stage1, stage2, _ = self.refs
        for cp in self._copies(c):
            cp.wait()
        cols = slice(c * MLP_LOAD, (c + 1) * MLP_LOAD)
        w1_scr[:, cols] = stage1[c % 2].astype(BF16)
        w2_scr[cols, :] = stage2[c % 2].astype(BF16)


def _mlp_tail(x, mix, m_ref, gn2_ref, w1_ref, w2_ref, loading=None):
    y1 = x + m_ref[:, 2 * D_MODEL:3 * D_MODEL] * mix
    h = _norm_mod(y1, gn2_ref[...], m_ref[:, 3 * D_MODEL:4 * D_MODEL], m_ref[:, 4 * D_MODEL:5 * D_MODEL]).astype(BF16)
    nchunk = D_FF // MLP_CHUNK
    acc = jnp.zeros(y1.shape, F32)
    for c in range(nchunk):
        cols = slice(c * MLP_CHUNK, (c + 1) * MLP_CHUNK)
        if loading is not None:
            per = MLP_CHUNK // MLP_LOAD
            for p in range(c * per, (c + 1) * per):
                if p + 1 < D_FF // MLP_LOAD:
                    loading.start(p + 1)
                loading.finish(p)
        a = _dot(h, w1_ref[:, cols])
        a = jnp.square(jnp.maximum(a, 0.0)).astype(BF16)
        acc = acc + _dot(a, w2_ref[cols, :])
    return y1 + m_ref[:, 5 * D_MODEL:6 * D_MODEL] * acc


def _run_tail(x, mix, m_ref, gn2_ref, weights, w1_ref, w2_ref, emit):
    first = pl.program_id(0) == 0

    @pl.when(first)
    def _():
        emit(_mlp_tail(x, mix, m_ref, gn2_ref, w1_ref, w2_ref, loading=weights))

    @pl.when(jnp.logical_not(first))
    def _():
        emit(_mlp_tail(x, mix, m_ref, gn2_ref, w1_ref, w2_ref))


def _even_out_kernel(xp_ref, xs_ref, y5_ref, u_ref, dskip_ref, wglu_ref, bglu_ref, glap_ref, glas_ref, wout_ref,
                     m_ref, gn2_ref, w1_hbm, w2_hbm, o_ref, w1_ref, w2_ref, stage1, stage2, sem, *, layer):
    weights = _MlpWeights(w1_hbm, w2_hbm, w1_ref, w2_ref, stage1, stage2, sem, layer)

    @pl.when(pl.program_id(0) == 0)
    def _():
        weights.start(0)

    nblk = S5_WIDTH // 128
    ys = (jnp.concatenate([y5_ref[b] for b in range(nblk)], axis=1)
          + jnp.concatenate([u_ref[b] for b in range(nblk)], axis=1) * dskip_ref[...])
    gl = jax.nn.gelu(ys)
    s5o = gl * jax.nn.sigmoid(_dot(gl.astype(BF16), wglu_ref[...]) + bglu_ref[...])
    gla = _token_tile(glap_ref, glas_ref, _OUT_TM).astype(BF16)
    mix = _dot(s5o.astype(BF16), wout_ref[0:S5_WIDTH, :]) + _dot(gla, wout_ref[S5_WIDTH:, :])

    def emit(y):
        o_ref[...] = y

    _run_tail(_token_tile(xp_ref, xs_ref, _OUT_TM), mix, m_ref, gn2_ref, weights, w1_ref, w2_ref, emit)


def _odd_out_kernel(x_ref, attp_ref, atts_ref, wo_ref, m_ref, gn2_ref, w1_hbm, w2_hbm, op_ref, os_ref,
                    w1_ref, w2_ref, stage1, stage2, sem, *, layer):
    weights = _MlpWeights(w1_hbm, w2_hbm, w1_ref, w2_ref, stage1, stage2, sem, layer)

    @pl.when(pl.program_id(0) == 0)
    def _():
        weights.start(0)

    mix = _dot(_token_tile(attp_ref, atts_ref, _OUT_TM), wo_ref[...])
    is_prompt = pl.program_id(0) < T_PROMPT // _OUT_TM

    def emit(y):
        @pl.when(is_prompt)
        def _():
            op_ref[...] = y

        @pl.when(jnp.logical_not(is_prompt))
        def _():
            os_ref[...] = y

    _run_tail(x_ref[...], mix, m_ref, gn2_ref, weights, w1_ref, w2_ref, emit)


_OUT_TM = 512


def _const_spec(shape):
    return pl.BlockSpec(shape, lambda i: (0,) * len(shape), pipeline_mode=pl.Buffered(1))


def _tail_specs(layer):
    tm = _OUT_TM
    return [
        pl.BlockSpec((None, 1, 6 * D_MODEL), lambda i: (layer * COND_ROWS + _cond_row(i, tm), 0, 0)),
        _const_spec((1, D_MODEL)),
        pl.BlockSpec(memory_space=pl.ANY),
        pl.BlockSpec(memory_space=pl.ANY),
    ]


def _tail_scratch():
    return [pltpu.VMEM((D_MODEL, D_FF), BF16), pltpu.VMEM((D_FF, D_MODEL), BF16),
            pltpu.VMEM((2, D_MODEL, MLP_LOAD), F32), pltpu.VMEM((2, MLP_LOAD, D_MODEL), F32),
            pltpu.SemaphoreType.DMA((2, 2))]


_TAIL_PARAMS = dict(dimension_semantics=("arbitrary",), vmem_limit_bytes=VMEM_LIMIT)


def _even_out_call(xp, xs, y5, u, d_skip, w_glu, b_glu, gla_p, gla_s, w_out, mods, layer, gn2, w1, w2):
    tm = _OUT_TM
    return pl.pallas_call(
        functools.partial(_even_out_kernel, layer=layer),
        out_shape=jax.ShapeDtypeStruct((T_TOK, D_MODEL), F32),
        grid=(T_TOK // tm,),
        in_specs=_token_specs(tm) + [
            pl.BlockSpec((S5_WIDTH // 128, tm, 128), lambda i: (0, i, 0)),
            pl.BlockSpec((S5_WIDTH // 128, tm, 128), lambda i: (0, i, 0)),
            _const_spec((1, S5_WIDTH)),
            _const_spec((S5_WIDTH, S5_WIDTH)),
            _const_spec((1, S5_WIDTH)),
        ] + _token_specs(tm, GLA_VW) + [
            _const_spec((S5_WIDTH + GLA_VW, D_MODEL)),
        ] + _tail_specs(layer),
        out_specs=pl.BlockSpec((tm, D_MODEL), lambda i: (i, 0)),
        scratch_shapes=_tail_scratch(),
        compiler_params=pltpu.CompilerParams(**_TAIL_PARAMS),
        name="even_out_mlp",
    )(xp, xs, y5, u, d_skip, w_glu, b_glu, gla_p, gla_s, w_out, mods, gn2, w1, w2)


def _odd_out_call(x, att_p, att_s, w_o, mods, layer, gn2, w1, w2):
    tm = _OUT_TM
    return pl.pallas_call(
        functools.partial(_odd_out_kernel, layer=layer),
        out_shape=(jax.ShapeDtypeStruct((T_PROMPT, D_MODEL), F32),
                   jax.ShapeDtypeStruct((T_SAMPLE, D_MODEL), F32)),
        grid=(T_TOK // tm,),
        in_specs=[pl.BlockSpec((tm, D_MODEL), lambda i: (i, 0))] + _token_specs(tm) + [
            _const_spec((D_MODEL, D_MODEL)),
        ] + _tail_specs(layer),
        out_specs=tuple(_token_specs(tm)),
        scratch_shapes=_tail_scratch(),
        compiler_params=pltpu.CompilerParams(**_TAIL_PARAMS),
        name="odd_out_mlp",
    )(x, att_p, att_s, w_o, mods, gn2, w1, w2)


def _qkv_kernel(x_ref, gn_ref, m_ref, w_ref, qn_ref, kn_ref, cos_ref, sin_ref,
                q_ref, kb_ref, vb_ref, k32_ref, v32_ref, *, tile):
    h = _norm_mod(x_ref[...], gn_ref[...], m_ref[:, 0:D_MODEL], m_ref[:, D_MODEL:2 * D_MODEL]).astype(BF16)
    z = _dot(h, w_ref[...])
    v = z[:, (N_HEADS + KV_HEADS) * HEAD_DIM:]
    vb_ref[...] = v.astype(BF16)
    even_lane = (lax.broadcasted_iota(jnp.int32, (1, HEAD_DIM), 1) & 1) == 0

    def heads(rope):
        for hd in range(N_HEADS + KV_HEADS):
            xh = z[:, hd * HEAD_DIM:(hd + 1) * HEAD_DIM]
            gain = qn_ref[...] if hd < N_HEADS else kn_ref[...]
            xh = xh * lax.rsqrt(jnp.mean(xh * xh, axis=-1, keepdims=True) + EPS) * gain
            if rope:
                partner = jnp.where(even_lane, pltpu.roll(xh, HEAD_DIM - 1, 1), pltpu.roll(xh, 1, 1))
                xh = xh * cos_ref[...] + partner * sin_ref[...]
            if hd < N_HEADS:
                q_ref[:, hd * HEAD_DIM:(hd + 1) * HEAD_DIM] = xh.astype(BF16)
            else:
                cols = slice((hd - N_HEADS) * HEAD_DIM, (hd - N_HEADS + 1) * HEAD_DIM)
                kb_ref[:, cols] = xh.astype(BF16)
                if not rope:
                    k32_ref[:, hd - N_HEADS, :] = xh

    is_sample = pl.program_id(0) >= T_PROMPT // tile

    @pl.when(is_sample)
    def _():
        heads(True)

    @pl.when(jnp.logical_not(is_sample))
    def _():
        heads(False)
        for kh in range(KV_HEADS):
            v32_ref[:, kh, :] = v[:, kh * HEAD_DIM:(kh + 1) * HEAD_DIM]


def _qkv_call(x, gn, mods, layer, w_qkv, q_norm, k_norm, cos_t, sin_t):
    tm = 512
    pos_tiles = DEC_SEQ // tm
    n_prompt = T_PROMPT // tm
    kvw = KV_HEADS * HEAD_DIM

    def pos_map(i):
        return (jnp.maximum(i - n_prompt, 0) % pos_tiles, 0)

    def prompt_map(i):
        return (jnp.minimum(i, n_prompt - 1), 0, 0)

    return pl.pallas_call(
        functools.partial(_qkv_kernel, tile=tm),
        out_shape=(jax.ShapeDtypeStruct((T_TOK, N_HEADS * HEAD_DIM), BF16),
                   jax.ShapeDtypeStruct((T_TOK, kvw), BF16),
                   jax.ShapeDtypeStruct((T_TOK, kvw), BF16),
                   jax.ShapeDtypeStruct((T_PROMPT, KV_HEADS, HEAD_DIM), F32),
                   jax.ShapeDtypeStruct((T_PROMPT, KV_HEADS, HEAD_DIM), F32)),
        grid=(T_TOK // tm,),
        in_specs=[
            pl.BlockSpec((tm, D_MODEL), lambda i: (i, 0)),
            pl.BlockSpec((1, D_MODEL), lambda i: (0, 0)),
            pl.BlockSpec((None, 1, 6 * D_MODEL), lambda i: (layer * COND_ROWS + _cond_row(i, tm), 0, 0)),
            pl.BlockSpec(w_qkv.shape, lambda i: (0, 0)),
            pl.BlockSpec((1, HEAD_DIM), lambda i: (0, 0)),
            pl.BlockSpec((1, HEAD_DIM), lambda i: (0, 0)),
            pl.BlockSpec((tm, HEAD_DIM), pos_map),
            pl.BlockSpec((tm, HEAD_DIM), pos_map),
        ],
        out_specs=(pl.BlockSpec((tm, N_HEADS * HEAD_DIM), lambda i: (i, 0)),
                   pl.BlockSpec((tm, kvw), lambda i: (i, 0)),
                   pl.BlockSpec((tm, kvw), lambda i: (i, 0)),
                   pl.BlockSpec((tm, KV_HEADS, HEAD_DIM), prompt_map),
                   pl.BlockSpec((tm, KV_HEADS, HEAD_DIM), prompt_map)),
        compiler_params=pltpu.CompilerParams(vmem_limit_bytes=VMEM_LIMIT),
        name="odd_qkv",
    )(x, gn, mods, w_qkv, q_norm, k_norm, cos_t, sin_t)


def _rope_tables():
    f32 = np.float32
    rows = DEC_SEQ // GRID_W
    row = np.repeat(np.arange(rows, dtype=f32), GRID_W)
    col = np.tile(np.arange(GRID_W, dtype=f32), rows)
    inv = np.power(f32(ROPE_THETA), -np.arange(0, AXIS_DIM, 2, dtype=f32) / f32(AXIS_DIM)).astype(f32)
    ang = np.concatenate([row[:, None] * inv, col[:, None] * inv], axis=-1).astype(f32)
    cos_t = np.repeat(np.cos(ang), 2, axis=-1).astype(f32)
    sin = np.sin(ang).astype(f32)
    sin_t = np.stack([-sin, sin], axis=-1).reshape(DEC_SEQ, HEAD_DIM)
    return jnp.asarray(cos_t), jnp.asarray(sin_t)


def _attn_kernel(*refs, has_cache):
    q_ref, k_ref, v_ref = refs[:3]
    ck_ref, cv_ref = refs[3:5] if has_cache else (None, None)
    o_ref = refs[-1]
    c = HEAD_DIM ** -0.5 * math.log2(math.e)
    ones_col = (lax.broadcasted_iota(jnp.int32, (1, HEAD_DIM), 1) == 0).astype(BF16)

    def with_ones(v):
        return jnp.concatenate([v, jnp.broadcast_to(ones_col, v.shape)], axis=1)

    k = k_ref[...]
    v = with_ones(v_ref[...])
    if has_cache:
        ck = ck_ref[...].astype(BF16)
        cv = with_ones(cv_ref[...].astype(BF16))
    for r in range(Q_PER_KV):
        cs = slice(r * HEAD_DIM, (r + 1) * HEAD_DIM)
        q = q_ref[:, cs]
        s = lax.dot_general(q, k, NT_DIMS, preferred_element_type=F32)
        m = jnp.max(s, axis=-1, keepdims=True)
        if has_cache:
            sc = lax.dot_general(q, ck, NT_DIMS, preferred_element_type=F32)
            m = jnp.maximum(m, jnp.max(sc, axis=-1, keepdims=True))
        mc = m * c
        o = _dot(jnp.exp2(s * c - mc).astype(BF16), v)
        if has_cache:
            o = o + _dot(jnp.exp2(sc * c - mc).astype(BF16), cv)
        o_ref[:, cs] = (o[:, 0:HEAD_DIM] / o[:, HEAD_DIM:HEAD_DIM + 1]).astype(BF16)


def _attn_call(q, k, v, cache_k, cache_v, seq_len, row0, nrows):
    assert row0 % seq_len == 0 and nrows % seq_len == 0
    has_cache = cache_k is not None
    b0 = row0 // seq_len
    gw = Q_PER_KV * HEAD_DIM
    in_specs = [
        pl.BlockSpec((seq_len, gw), lambda b, g: (b0 + b, g)),
        pl.BlockSpec((seq_len, HEAD_DIM), lambda b, g: (b0 + b, g)),
        pl.BlockSpec((seq_len, HEAD_DIM), lambda b, g: (b0 + b, g)),
    ]
    args = [q, k, v]
    if has_cache:
        in_specs += [pl.BlockSpec((PAST_LEN, HEAD_DIM), lambda b, g: (b, g)),
                     pl.BlockSpec((PAST_LEN, HEAD_DIM), lambda b, g: (b, g))]
        args += [cache_k, cache_v]
    return pl.pallas_call(
        functools.partial(_attn_kernel, has_cache=has_cache),
        out_shape=jax.ShapeDtypeStruct((nrows, N_HEADS * HEAD_DIM), BF16),
        grid=(nrows // seq_len, KV_HEADS),
        in_specs=in_specs,
        out_specs=pl.BlockSpec((seq_len, gw), lambda b, g: (b, g)),
        compiler_params=pltpu.CompilerParams(vmem_limit_bytes=VMEM_LIMIT),
        name=f"attn_len{seq_len}",
    )(*args)


def kernel(x_prompt, x_sample, state_s5_re, state_s5_im, state_gla, cache_k, cache_v, c, c_ctx, norm_mix, norm_mlp, w_ada, b_ada, w_mlp_in, w_mlp_out, w_in_e, w_out_e, s5_lambda_re, s5_lambda_im, s5_log_dt, s5_b_re, s5_b_im, s5_c_re, s5_c_im, s5_d, s5_w_glu, s5_b_glu, gla_w_gate2, gla_b_gate, gla_norm, w_qkv_o, w_o_o, q_norm, k_norm):
    xp = x_prompt.reshape(T_PROMPT, D_MODEL)
    xs = x_sample.reshape(T_SAMPLE, D_MODEL)
    cond8 = jnp.concatenate([c_ctx[None, :], c, jnp.zeros((COND_ROWS - 1 - DEC_BATCH, D_MODEL), F32)], axis=0)
    mods = _ada_call(cond8, w_ada, b_ada)
    w1_all, w2_all = w_mlp_in, w_mlp_out

    n_main = S5_WIDTH + 2 * GLA_QK + 2 * GLA_VW
    w_in = w_in_e[0]
    w_main = w_in[:, :n_main].astype(BF16)
    w_glr = jnp.pad(w_in[:, n_main:], ((0, 0), (0, 128 - 2 * GLA_RANK))).astype(BF16)
    zg = jnp.zeros((GLA_RANK, GLA_QK), F32)
    w_gate = jnp.concatenate([jnp.concatenate([gla_w_gate2[0, 0], zg], axis=1),
                              jnp.concatenate([zg, gla_w_gate2[0, 1]], axis=1),
                              jnp.zeros((128 - 2 * GLA_RANK, 2 * GLA_QK), F32)], axis=0).astype(BF16)
    b_gate = gla_b_gate[0].reshape(1, 2 * GLA_QK)
    u, z, g = _inproj_call(xp, xs, norm_mix[0:1], mods, 0, w_main, w_glr, w_gate, b_gate)

    mats = _s5_prep_call(s5_lambda_re[0], s5_lambda_im[0], s5_log_dt[0], s5_b_re[0], s5_b_im[0],
                         s5_c_re[0], s5_c_im[0])

    def state_rows(s):
        return jnp.transpose(s, (2, 0, 1, 3)).reshape(S5_GROUPS, DEC_BATCH, 2 * S5_STATE)

    h0 = jnp.concatenate([state_rows(state_s5_re[:, 0]), state_rows(state_s5_im[:, 0])], axis=-1)
    nsteps = S5_GROUPS // S5_GPB
    h0 = jnp.transpose(h0.reshape(nsteps, S5_GPB, DEC_BATCH, S5_W), (0, 2, 1, 3))
    y5, ns = _s5_call(u, mats, h0)
    ns = jnp.transpose(ns, (0, 2, 1, 3)).reshape(S5_GROUPS, BATCH, S5_W)

    def state_out(n):
        return jnp.transpose(n.reshape(S5_GROUPS, BATCH, 2, S5_STATE), (1, 2, 0, 3))[:, None]

    new_s5_re = state_out(ns[:, :, :2 * S5_STATE])
    new_s5_im = state_out(ns[:, :, 2 * S5_STATE:])

    gn_gla = gla_norm[0].reshape(1, GLA_DV)
    gla_p, sfin = _gla_call(z, g, gn_gla, None, SEQ, BATCH, 0)
    s0 = state_gla[:, 0].reshape(DEC_BATCH, 2, GLA_QK, GLA_DV)
    gla_s, _ = _gla_call(z, g, gn_gla, s0, DEC_SEQ, DEC_BATCH, T_PROMPT)
    new_gla = sfin.reshape(BATCH, 1, 2, GLA_HEADS, GLA_DK, GLA_DV)

    x = _even_out_call(xp, xs, y5, u, s5_d[0].reshape(1, S5_WIDTH), s5_w_glu[0].astype(BF16),
                       s5_b_glu[0].reshape(1, S5_WIDTH), gla_p, gla_s, w_out_e[0].astype(BF16), mods, 0,
                       norm_mlp[0:1], w1_all, w2_all)

    cos_t, sin_t = _rope_tables()
    q, k, v, k32, v32 = _qkv_call(x, norm_mix[1:2], mods, 1, w_qkv_o[0].astype(BF16),
                                  q_norm[0].reshape(1, HEAD_DIM), k_norm[0].reshape(1, HEAD_DIM), cos_t, sin_t)
    att_p = _attn_call(q, k, v, None, None, SEQ, 0, T_PROMPT)
    ck = cache_k[:, 0].reshape(DEC_BATCH * PAST_LEN, KV_HEADS * HEAD_DIM)
    cv = cache_v[:, 0].reshape(DEC_BATCH * PAST_LEN, KV_HEADS * HEAD_DIM)
    att_s = _attn_call(q, k, v, ck, cv, DEC_SEQ, T_PROMPT, T_SAMPLE)
    yp, ys = _odd_out_call(x, att_p, att_s, w_o_o[0].astype(BF16), mods, 1, norm_mlp[1:2],
                           w1_all, w2_all)

    new_k = k32.reshape(BATCH, 1, SEQ, KV_HEADS, HEAD_DIM)
    new_v = v32.reshape(BATCH, 1, SEQ, KV_HEADS, HEAD_DIM)
    y_prompt = yp.reshape(BATCH, SEQ, D_MODEL)
    y_sample = ys.reshape(DEC_BATCH, DEC_SEQ, D_MODEL)
    return (y_prompt, y_sample, new_s5_re, new_s5_im, new_gla, new_k, new_v)
```

```python
import functools
import math

import jax
import jax.numpy as jnp
import numpy as np
from jax import lax
from jax.experimental import pallas as pl
from jax.experimental.pallas import tpu as pltpu

F32 = jnp.float32
BF16 = jnp.bfloat16

D_MODEL = 1024
BATCH = 16
SEQ = 256
DEPTH = 2
DEC_BATCH = 4
DEC_SEQ = 1024
PAST_LEN = 512
GRID_W = 64
S5_WIDTH = 512
S5_GROUP_CH = 16
S5_GROUPS = 32
S5_STATE = 64
GLA_HEADS = 4
GLA_VW = 512
GLA_DV = 128
GLA_DK = 64
GLA_QK = 256
GLA_RANK = 16
GLA_TAU = 16.0
GLA_CHUNK = 64
GLA_CPB = 4
GLA_BLK = GLA_CPB * GLA_CHUNK
HEAD_DIM = 128
N_HEADS = 8
KV_HEADS = 2
Q_PER_KV = N_HEADS // KV_HEADS
AXIS_DIM = 64
ROPE_THETA = 10000.0
D_FF = 4096
EPS = 1e-6

T_PROMPT = BATCH * SEQ
T_SAMPLE = DEC_BATCH * DEC_SEQ
T_TOK = T_PROMPT + T_SAMPLE
COND_ROWS = 8
COND_SPAN = 1024
PROMPT_SPANS = T_PROMPT // COND_SPAN

S5_Q = 16
S5_W = S5_Q * S5_GROUP_CH
S5_GPB = 128 // S5_GROUP_CH
S5_ROWS = T_TOK // S5_Q
S5_PROMPT_ROWS = T_PROMPT // S5_Q
S5_PROMPT_CHUNKS = SEQ // S5_Q
S5_SAMPLE_CHUNKS = DEC_SEQ // S5_Q

VMEM_LIMIT = 56 * 1024 * 1024

NT_DIMS = (((1,), (1,)), ((), ()))
TN_DIMS = (((0,), (0,)), ((), ()))


def _cond_row(i, tile):
    return jnp.maximum((i * tile) // COND_SPAN - (PROMPT_SPANS - 1), 0)


def _norm_mod(x, gain, shift, scale):
    y = x * lax.rsqrt(jnp.mean(x * x, axis=-1, keepdims=True) + EPS)
    return (y * gain) * (1.0 + scale) + shift


def _dot(a, b):
    return jnp.dot(a, b, preferred_element_type=F32)


def _ada_kernel(cond_ref, w_ref, b_ref, o_ref):
    s = jax.nn.silu(cond_ref[...]).astype(BF16)
    o_ref[:, 0, :] = _dot(s, w_ref[...].astype(BF16)) + b_ref[...]


def _ada_call(cond8, w_ada, b_ada):
    tn = 2048
    nj = 6 * D_MODEL // tn
    return pl.pallas_call(
        _ada_kernel,
        out_shape=jax.ShapeDtypeStruct((DEPTH * COND_ROWS, 1, 6 * D_MODEL), F32),
        grid=(DEPTH, nj),
        in_specs=[
            pl.BlockSpec((COND_ROWS, D_MODEL), lambda l, j: (0, 0)),
            pl.BlockSpec((None, D_MODEL, tn), lambda l, j: (l, 0, j)),
            pl.BlockSpec((None, 1, tn), lambda l, j: (l, 0, j)),
        ],
        out_specs=pl.BlockSpec((COND_ROWS, 1, tn), lambda l, j: (l, 0, j)),
        compiler_params=pltpu.CompilerParams(vmem_limit_bytes=VMEM_LIMIT),
        name="ada_mod",
    )(cond8, w_ada, b_ada.reshape(DEPTH, 1, 6 * D_MODEL))


def _token_specs(tile, width=D_MODEL):
    n_prompt = T_PROMPT // tile
    return [pl.BlockSpec((tile, width), lambda i: (jnp.minimum(i, n_prompt - 1), 0)),
            pl.BlockSpec((tile, width), lambda i: (jnp.maximum(i - n_prompt, 0), 0))]


def _token_tile(xp_ref, xs_ref, tile):
    return jnp.where(pl.program_id(0) < T_PROMPT // tile, xp_ref[...], xs_ref[...])


def _inproj_kernel(xp_ref, xs_ref, gn_ref, m_ref, w_ref, wglr_ref, wg_ref, bg_ref, u_ref, z_ref, g_ref, *, tile):
    x = _token_tile(xp_ref, xs_ref, tile)
    h = _norm_mod(x, gn_ref[...], m_ref[:, 0:D_MODEL], m_ref[:, D_MODEL:2 * D_MODEL]).astype(BF16)
    z = _dot(h, w_ref[...])
    for blk in range(S5_WIDTH // 128):
        u_ref[blk] = z[:, blk * 128:(blk + 1) * 128]
    z_ref[...] = z[:, S5_WIDTH:]
    glr = _dot(h, wglr_ref[...]).astype(BF16)
    pre = _dot(glr, wg_ref[...]) + bg_ref[...]
    g_ref[...] = jax.nn.log_sigmoid(pre) * (1.0 / GLA_TAU)


def _inproj_call(xp, xs, gn, mods, layer, w_main, w_glr, w_gate, b_gate):
    tm = 512
    nz = w_main.shape[1]
    return pl.pallas_call(
        functools.partial(_inproj_kernel, tile=tm),
        out_shape=(jax.ShapeDtypeStruct((S5_WIDTH // 128, T_TOK, 128), F32),
                   jax.ShapeDtypeStruct((T_TOK, nz - S5_WIDTH), F32),
                   jax.ShapeDtypeStruct((T_TOK, 2 * GLA_QK), F32)),
        grid=(T_TOK // tm,),
        in_specs=_token_specs(tm) + [
            pl.BlockSpec((1, D_MODEL), lambda i: (0, 0)),
            pl.BlockSpec((None, 1, 6 * D_MODEL), lambda i: (layer * COND_ROWS + _cond_row(i, tm), 0, 0)),
            pl.BlockSpec((D_MODEL, nz), lambda i: (0, 0)),
            pl.BlockSpec((D_MODEL, 128), lambda i: (0, 0)),
            pl.BlockSpec((128, 2 * GLA_QK), lambda i: (0, 0)),
            pl.BlockSpec((1, 2 * GLA_QK), lambda i: (0, 0)),
        ],
        out_specs=(pl.BlockSpec((S5_WIDTH // 128, tm, 128), lambda i: (0, i, 0)),
                   pl.BlockSpec((tm, nz - S5_WIDTH), lambda i: (i, 0)),
                   pl.BlockSpec((tm, 2 * GLA_QK), lambda i: (i, 0))),
        compiler_params=pltpu.CompilerParams(vmem_limit_bytes=VMEM_LIMIT),
        name="even_inproj",
    )(xp, xs, gn, mods, w_main, w_glr, w_gate, b_gate)


S5_PREP_GPB = 8
_PREP_LRE, _PREP_LIM, _PREP_LDT = 0, 1, 2
_PREP_BT_RE, _PREP_BT_IM, _PREP_C_RE, _PREP_C_IM, _PREP_ROWS = 8, 24, 40, 56, 72


def _s5_prep_kernel(p_ref, cc_ref, t_ref, bq_ref, cqt_ref, be_ref, a_ref, t_scr, dd_scr):
    for gi in range(S5_PREP_GPB):
        _s5_prep_group(p_ref.at[gi], cc_ref.at[gi], t_ref.at[gi], bq_ref.at[gi], cqt_ref.at[gi], be_ref.at[gi],
                       a_ref.at[gi], t_scr, dd_scr)


def _s5_prep_group(p_ref, cc_ref, t_ref, bq_ref, cqt_ref, be_ref, a_ref, t_scr, dd_scr):
    gch = S5_GROUP_CH
    lre = p_ref[_PREP_LRE:_PREP_LRE + 1]
    lim = p_ref[_PREP_LIM:_PREP_LIM + 1]
    dt = jnp.exp(p_ref[_PREP_LDT:_PREP_LDT + 1])
    a = lre * dt
    th = lim * dt

    def lam_pow(k):
        mag = jnp.exp(k * a)
        return mag * jnp.cos(k * th), mag * jnp.sin(k * th)

    lb_re, lb_im = lam_pow(1.0)
    nr = lb_re - 1.0
    den = lre * lre + lim * lim
    cf_re = (nr * lre + lb_im * lim) / den
    cf_im = (lb_im * lre - nr * lim) / den
    bt_re = p_ref[_PREP_BT_RE:_PREP_BT_RE + gch]
    bt_im = p_ref[_PREP_BT_IM:_PREP_BT_IM + gch]
    bb_re = jnp.tile(cf_re * bt_re - cf_im * bt_im, (S5_Q, 1))
    bb_im = jnp.tile(cf_re * bt_im + cf_im * bt_re, (S5_Q, 1))

    shape = (S5_W, 128)
    pos = lax.shift_right_logical(lax.broadcasted_iota(jnp.int32, shape, 0), 4)
    is_f = lax.broadcasted_iota(jnp.int32, shape, 1) < S5_STATE
    posq = lax.broadcasted_iota(jnp.int32, (S5_Q, 128), 0).astype(F32)
    is_fq = lax.broadcasted_iota(jnp.int32, (S5_Q, 128), 1) < S5_STATE

    def per_channel(tbl):
        return jnp.broadcast_to(tbl[:, None, :], (S5_Q, S5_GROUP_CH, 128)).reshape(shape)

    p_re, p_im = map(per_channel, lam_pow(jnp.where(is_fq, (S5_Q - 1.0) - posq, posq)))
    w_re = p_re * bb_re - p_im * bb_im
    w_im = p_re * bb_im + p_im * bb_re
    bq = jnp.concatenate([w_re, w_im], axis=1)
    bqt = jnp.transpose(bq)
    bq_ref[...] = bqt.astype(BF16)

    edge = pos == jnp.where(is_f, 0, S5_Q - 1)
    be = jnp.concatenate([jnp.where(edge, bb_re, 0.0), jnp.where(edge, bb_im, 0.0)], axis=1)
    be_ref[...] = jnp.transpose(be).astype(BF16)

    q_re, q_im = map(per_channel, lam_pow(jnp.where(is_fq, posq + 1.0, S5_Q - posq)))
    ct_re = jnp.tile(p_ref[_PREP_C_RE:_PREP_C_RE + gch], (S5_Q, 1))
    ct_im = jnp.tile(p_ref[_PREP_C_IM:_PREP_C_IM + gch], (S5_Q, 1))
    g_re = q_re * ct_re - q_im * ct_im
    g_im = q_re * ct_im + q_im * ct_re
    cqt_ref[...] = jnp.concatenate([g_re, -g_im], axis=1).astype(BF16)

    a_re, a_im = lam_pow(float(S5_Q))
    a_ref[...] = jnp.concatenate([a_re, a_im], axis=1)

    kf = jnp.dot(cc_ref[0:gch], bqt, precision=lax.Precision.HIGHEST, preferred_element_type=F32)
    kb = jnp.dot(cc_ref[gch:2 * gch], bqt, precision=lax.Precision.HIGHEST, preferred_element_type=F32)
    lo = S5_W - gch
    dd_scr[:, 0:S5_W] = kf
    dd_scr[:, lo:lo + S5_W] = kb
    dd_scr[:, lo:S5_W] = kf[:, lo:S5_W] + kb[:, 0:gch]
    for t in range(S5_Q):
        c0 = (S5_Q - 1 - t) * gch
        t_scr[t * gch:(t + 1) * gch, :] = dd_scr[:, c0:c0 + S5_W]
    t_ref[...] = t_scr[...].astype(BF16)


def _s5_prep_call(lam_re, lam_im, log_dt, b_re, b_im, c_re, c_im):
    def fb(p):
        return jnp.transpose(p, (1, 0, 2)).reshape(S5_GROUPS, 1, 2 * S5_STATE)

    def dup(p):
        return jnp.concatenate([p, p], axis=-1)

    ldt = fb(jnp.broadcast_to(log_dt[:, :, None], (2, S5_GROUPS, S5_STATE)))
    pad = jnp.zeros((S5_GROUPS, _PREP_BT_RE - _PREP_LDT - 1, 128), F32)
    packed = jnp.concatenate([fb(lam_re), fb(lam_im), ldt, pad,
                              dup(jnp.transpose(b_re, (0, 2, 1))), dup(jnp.transpose(b_im, (0, 2, 1))),
                              dup(c_re), dup(c_im)], axis=1)
    zero = jnp.zeros_like(c_re)
    cc = jnp.concatenate([jnp.concatenate([c_re, zero, -c_im, zero], axis=-1),
                          jnp.concatenate([zero, c_re, zero, -c_im], axis=-1)], axis=1)

    gpb = S5_PREP_GPB
    sq = pl.BlockSpec((gpb, S5_W, S5_W), lambda g: (g, 0, 0))
    sq_shape = jax.ShapeDtypeStruct((S5_GROUPS, S5_W, S5_W), BF16)
    return pl.pallas_call(
        _s5_prep_kernel,
        out_shape=(sq_shape, sq_shape, sq_shape, sq_shape,
                   jax.ShapeDtypeStruct((S5_GROUPS, 1, S5_W), F32)),
        grid=(S5_GROUPS // gpb,),
        in_specs=[pl.BlockSpec((gpb, _PREP_ROWS, 128), lambda g: (g, 0, 0)),
                  pl.BlockSpec((gpb, 2 * S5_GROUP_CH, S5_W), lambda g: (g, 0, 0))],
        out_specs=(sq, sq, sq, sq, pl.BlockSpec((gpb, 1, S5_W), lambda g: (g, 0, 0))),
        scratch_shapes=[pltpu.VMEM((S5_W, S5_W), F32), pltpu.VMEM((S5_GROUP_CH, 2 * S5_W), F32)],
        name="s5_prep",
    )(packed, cc)


def _s5_kernel(u_ref, tt_ref, bqt_ref, cqt_ref, bet_ref, a_ref, h0_ref, y_ref, ns_ref,
               ut_scr, x_scr, spf_scr, spb_scr, ne_scr, yt_scr, xt_scr):
    gch = S5_GROUP_CH
    for s in range(S5_Q):
        rows = u_ref[pl.ds(s, S5_ROWS, stride=S5_Q), :]
        rows_t = jnp.transpose(rows).astype(BF16)
        for gl in range(S5_GPB):
            ut_scr[gl, s * gch:(s + 1) * gch, :] = rows_t[gl * gch:(gl + 1) * gch, :]

    for gl in range(S5_GPB):
        ut = ut_scr[gl]
        xt_scr[...] = _dot(bqt_ref[gl], ut)
        x = jnp.transpose(xt_scr[...])
        xt_scr[:, 0:S5_PROMPT_ROWS] = _dot(bet_ref[gl], ut[:, 0:S5_PROMPT_ROWS])
        ne = jnp.transpose(xt_scr[:, 0:S5_PROMPT_ROWS])
        for part in range(2):
            x_scr[part, pl.ds(gl, S5_ROWS, stride=S5_GPB), :] = x[:, part * 128:(part + 1) * 128]
            ne_scr[part, pl.ds(gl, S5_PROMPT_ROWS, stride=S5_GPB), :] = ne[:, part * 128:(part + 1) * 128]

    is_f = lax.broadcasted_iota(jnp.int32, (1, 128), 1) < S5_STATE
    a_re = a_ref[:, 0:128]
    a_im = a_ref[:, 128:256]

    def tile(row):
        return pl.ds(pl.multiple_of(row * S5_GPB, S5_GPB), S5_GPB)

    def scan(base, nseq, nchunk, s_init):
        def body(i, state):
            new = []
            for b in range(nseq):
                s_re, s_im = state[b]
                rows_f = tile(base + b * nchunk + i)
                rows_b = tile(base + b * nchunk + (nchunk - 1 - i))
                spf_scr[0, rows_f, :] = s_re
                spf_scr[1, rows_f, :] = s_im
                spb_scr[0, rows_b, :] = s_re
                spb_scr[1, rows_b, :] = s_im
                x_re = jnp.where(is_f, x_scr[0, rows_f, :], x_scr[0, rows_b, :])
                x_im = jnp.where(is_f, x_scr[1, rows_f, :], x_scr[1, rows_b, :])
                new.append((a_re * s_re - a_im * s_im + x_re, a_re * s_im + a_im * s_re + x_im))
            return tuple(new)

        lax.fori_loop(0, nchunk, body, tuple(s_init))

    zero = jnp.zeros((S5_GPB, 128), F32)
    scan(0, BATCH, S5_PROMPT_CHUNKS, [(zero, zero)] * BATCH)
    scan(S5_PROMPT_ROWS, DEC_BATCH, S5_SAMPLE_CHUNKS,
         [(h0_ref[b, :, 0:128], h0_ref[b, :, 128:256]) for b in range(DEC_BATCH)])

    for b in range(BATCH):
        first = pl.ds(b * S5_PROMPT_CHUNKS * S5_GPB, S5_GPB)
        last = pl.ds(((b + 1) * S5_PROMPT_CHUNKS - 1) * S5_GPB, S5_GPB)
        for part in range(2):
            ns_ref[b, :, part * 128:(part + 1) * 128] = jnp.where(is_f, ne_scr[part, first, :], ne_scr[part, last, :])

    for gl in range(S5_GPB):
        rows = pl.ds(gl, S5_ROWS, stride=S5_GPB)
        carried = jnp.concatenate([jnp.where(is_f, spf_scr[p, rows, :], spb_scr[p, rows, :]) for p in range(2)],
                                  axis=1).astype(BF16)
        yt = _dot(tt_ref[gl], ut_scr[gl]) + lax.dot_general(cqt_ref[gl], carried, NT_DIMS,
                                                            preferred_element_type=F32)
        for t in range(S5_Q):
            yt_scr[t, gl * gch:(gl + 1) * gch, :] = yt[t * gch:(t + 1) * gch, :]
    for t in range(S5_Q):
        y_ref[pl.ds(t, S5_ROWS, stride=S5_Q), :] = jnp.transpose(yt_scr[t])


def _s5_call(u, mats, h0):
    tt_m, bqt_m, cqt_m, bet_m, a_m = mats
    nsteps = S5_GROUPS // S5_GPB
    sq = pl.BlockSpec((S5_GPB, S5_W, S5_W), lambda g: (g, 0, 0))
    state_scr = pltpu.VMEM((2, S5_ROWS * S5_GPB, 128), F32)
    return pl.pallas_call(
        _s5_kernel,
        out_shape=(jax.ShapeDtypeStruct((nsteps, T_TOK, 128), F32),
                   jax.ShapeDtypeStruct((nsteps, BATCH, S5_GPB, S5_W), F32)),
        grid=(nsteps,),
        in_specs=[
            pl.BlockSpec((None, T_TOK, 128), lambda g: (g, 0, 0)),
            sq, sq, sq, sq,
            pl.BlockSpec((S5_GPB, S5_W), lambda g: (g, 0)),
            pl.BlockSpec((None, DEC_BATCH, S5_GPB, S5_W), lambda g: (g, 0, 0, 0)),
        ],
        out_specs=(pl.BlockSpec((None, T_TOK, 128), lambda g: (g, 0, 0)),
                   pl.BlockSpec((None, BATCH, S5_GPB, S5_W), lambda g: (g, 0, 0, 0))),
        scratch_shapes=[pltpu.VMEM((S5_GPB, S5_W, S5_ROWS), BF16), state_scr, state_scr, state_scr,
                        pltpu.VMEM((2, S5_PROMPT_ROWS * S5_GPB, 128), F32),
                        pltpu.VMEM((S5_Q, 128, S5_ROWS), F32), pltpu.VMEM((S5_W, S5_ROWS), F32)],
        compiler_params=pltpu.CompilerParams(vmem_limit_bytes=VMEM_LIMIT),
        name="s5_scan",
    )(u, tt_m, bqt_m, cqt_m, bet_m, a_m.reshape(S5_GROUPS, S5_W), h0)


def _split_bf16(x):
    hi = x.astype(BF16)
    r1 = x - hi.astype(F32)
    mid = r1.astype(BF16)
    lo = (r1 - mid.astype(F32)).astype(BF16)
    return hi, mid, lo


def _gla_kernel(*refs, seq_len, nsub, has_s0):
    rows_refs, gn_ref = refs[:6], refs[6]
    s0_ref = refs[7] if has_s0 else None
    o_ref, sfin_ref = refs[7 + has_s0:9 + has_s0]
    scratch = refs[9 + has_s0:]
    for j in range(nsub):
        rows = pl.ds(j * seq_len, seq_len)
        _gla_sequence(*[r.at[rows, :] for r in rows_refs], gn_ref, s0_ref.at[j] if has_s0 else None,
                      o_ref.at[rows, :], sfin_ref.at[j], *[s.at[j] for s in scratch], seq_len=seq_len)


def _gla_sequence(q_ref, k_ref, v_ref, gf_ref, gb_ref, r_ref, gn_ref, s0_ref, o_ref, sfin_ref,
                  oi_scr, qd_scr, kv_scr, dec_scr, ss_scr, *, seq_len):
    has_s0 = s0_ref is not None
    nblk = seq_len // GLA_BLK
    nchunk = seq_len // GLA_CHUNK
    cl = GLA_CHUNK
    ti = lax.broadcasted_iota(jnp.int32, (GLA_BLK, GLA_BLK), 0)
    si = lax.broadcasted_iota(jnp.int32, (GLA_BLK, GLA_BLK), 1)
    same = lax.shift_right_logical(ti, 6) == lax.shift_right_logical(si, 6)
    keep = (same & (ti >= si), same & (ti <= si))
    tri = tuple(kp.astype(BF16) for kp in keep)
    lane_head = lax.shift_right_logical(lax.broadcasted_iota(jnp.int32, (cl, GLA_QK), 1), 6)
    zeros_v = jnp.zeros((cl, GLA_DV), BF16)
    heads = [(slice(h * GLA_DK, (h + 1) * GLA_DK), slice(h * GLA_DV, (h + 1) * GLA_DV)) for h in range(GLA_HEADS)]

    for j in range(nblk):
        rows = slice(j * GLA_BLK, (j + 1) * GLA_BLK)
        q = q_ref[rows, :] * (GLA_DK ** -0.5)
        k = k_ref[rows, :]
        v = v_ref[rows, :].astype(BF16)
        qd, kd, k2t = [], [], []
        for d, g_ref in enumerate((gf_ref, gb_ref)):
            b = sum(_dot(tri[d], part) for part in _split_bf16(g_ref[rows, :]))
            last = cl - 1 if d == 0 else 0
            b_last = [b[c * cl + last:c * cl + last + 1] for c in range(GLA_CPB)]
            bl = jnp.concatenate([jnp.broadcast_to(x, (cl, GLA_QK)) for x in b_last], axis=0)
            qd_d = (q * jnp.exp(b)).astype(BF16)
            qd_scr[d, rows, :] = qd_d
            qd.append(qd_d)
            kd.append((k * jnp.exp(-b)).astype(BF16))
            k2t.append(jnp.transpose(k * jnp.exp(bl - b)).astype(BF16))
            for c in range(GLA_CPB):
                dec_scr[d, j * GLA_CPB + c] = jnp.exp(jnp.transpose(jnp.broadcast_to(b_last[c], (GLA_DV, GLA_QK))))
        for h, (ks, vs) in enumerate(heads):
            att = [jnp.where(keep[d], lax.dot_general(qd[d][:, ks], kd[d][:, ks], NT_DIMS,
                                                      preferred_element_type=F32), 0.0) for d in range(2)]
            oi_scr[rows, vs] = _dot((att[0] + att[1]).astype(BF16), v[:, vs])
            vh = v[:, vs]
            vexp = jnp.concatenate(
                [jnp.concatenate([vh[c * cl:(c + 1) * cl] if c2 == c else zeros_v for c2 in range(GLA_CPB)], axis=1)
                 for c in range(GLA_CPB)], axis=0)
            for d in range(2):
                kv_scr[d, j, h] = _dot(k2t[d][ks, :], vexp)

    for d in range(2):
        s = s0_ref[d] if has_s0 else jnp.zeros((GLA_QK, GLA_DV), F32)
        for cg in (range(nchunk) if d == 0 else range(nchunk - 1, -1, -1)):
            j, c = divmod(cg, GLA_CPB)
            ss_scr[d, cg] = s.astype(BF16)
            kv = jnp.concatenate([kv_scr[d, j, h, :, c * GLA_DV:(c + 1) * GLA_DV] for h in range(GLA_HEADS)], axis=0)
            s = s * dec_scr[d, cg] + kv
        sfin_ref[d] = s

    for cg in range(nchunk):
        rows = slice(cg * cl, (cg + 1) * cl)
        inter = []
        for d in range(2):
            qc = qd_scr[d, rows, :]
            qstack = jnp.concatenate([jnp.where(lane_head == h, qc, jnp.zeros_like(qc)) for h in range(GLA_HEADS)],
                                     axis=0)
            inter.append(_dot(qstack, ss_scr[d, cg]))
        gate = jax.nn.silu(r_ref[rows, :])
        for h, (ks, vs) in enumerate(heads):
            hr = slice(h * cl, (h + 1) * cl)
            oh = oi_scr[rows, vs] + inter[0][hr] + inter[1][hr]
            oh = oh * lax.rsqrt(jnp.mean(oh * oh, axis=-1, keepdims=True) + EPS) * gn_ref[...]
            o_ref[rows, vs] = oh * gate[:, vs]


def _gla_call(z, g, gla_norm, s0, seq_len, nseq, row0, nsub):
    blk = nsub * seq_len
    assert row0 % blk == 0 and nseq % nsub == 0
    r0 = row0 // blk
    has_s0 = s0 is not None
    qk_off = 0
    v_off = 2 * GLA_QK // GLA_VW
    in_specs = [
        pl.BlockSpec((blk, GLA_QK), lambda i: (r0 + i, qk_off)),
        pl.BlockSpec((blk, GLA_QK), lambda i: (r0 + i, qk_off + 1)),
        pl.BlockSpec((blk, GLA_VW), lambda i: (r0 + i, v_off)),
        pl.BlockSpec((blk, GLA_QK), lambda i: (r0 + i, 0)),
        pl.BlockSpec((blk, GLA_QK), lambda i: (r0 + i, 1)),
        pl.BlockSpec((blk, GLA_VW), lambda i: (r0 + i, v_off + 1)),
        pl.BlockSpec((1, GLA_DV), lambda i: (0, 0)),
    ]
    args = [z, z, z, g, g, z, gla_norm]
    state_spec = pl.BlockSpec((nsub, 2, GLA_QK, GLA_DV), lambda i: (i, 0, 0, 0))
    if has_s0:
        in_specs.append(state_spec)
        args.append(s0)
    return pl.pallas_call(
        functools.partial(_gla_kernel, seq_len=seq_len, nsub=nsub, has_s0=has_s0),
        out_shape=(jax.ShapeDtypeStruct((nseq * seq_len, GLA_VW), F32),
                   jax.ShapeDtypeStruct((nseq, 2, GLA_QK, GLA_DV), F32)),
        grid=(nseq // nsub,),
        in_specs=in_specs,
        out_specs=(pl.BlockSpec((blk, GLA_VW), lambda i: (i, 0)), state_spec),
        scratch_shapes=[
            pltpu.VMEM((nsub, seq_len, GLA_VW), F32),
            pltpu.VMEM((nsub, 2, seq_len, GLA_QK), BF16),
            pltpu.VMEM((nsub, 2, seq_len // GLA_BLK, GLA_HEADS, GLA_DK, GLA_CPB * GLA_DV), F32),
            pltpu.VMEM((nsub, 2, seq_len // GLA_CHUNK, GLA_QK, GLA_DV), F32),
            pltpu.VMEM((nsub, 2, seq_len // GLA_CHUNK, GLA_QK, GLA_DV), BF16),
        ],
        compiler_params=pltpu.CompilerParams(vmem_limit_bytes=VMEM_LIMIT),
        name=f"gla_len{seq_len}",
    )(*args)


MLP_CHUNK = 512
MLP_LOAD = 256


class _MlpWeights:
    def __init__(self, w1_hbm, w2_hbm, w1_scr, w2_scr, stage1, stage2, sem, layer):
        self.refs = (w1_hbm, w2_hbm, w1_scr, w2_scr, stage1, stage2, sem)
        self.layer = layer

    def _copies(self, c):
        w1_hbm, w2_hbm, _, _, stage1, stage2, sem = self.refs
        cols = pl.ds(c * MLP_LOAD, MLP_LOAD)
        return (pltpu.make_async_copy(w1_hbm.at[self.layer, :, cols], stage1.at[c % 2], sem.at[0, c % 2]),
                pltpu.make_async_copy(w2_hbm.at[self.layer, cols, :], stage2.at[c % 2], sem.at[1, c % 2]))

    def start(self, c):
        for cp in self._copies(c):
            cp.start()

    def finish(self, c):
        _, _, w1_scr, w2_scr, stage1, stage2, _ = self.refs
        for cp in self._copies(c):
            cp.wait()
        cols = slice(c * MLP_LOAD, (c + 1) * MLP_LOAD)
        w1_scr[:, cols] = stage1[c % 2].astype(BF16)
        w2_scr[cols, :] = stage2[c % 2].astype(BF16)


def _mlp_tail(x, mix, m_ref, gn2_ref, w1_ref, w2_ref, loading=None):
    y1 = x + m_ref[:, 2 * D_MODEL:3 * D_MODEL] * mix
    h = _norm_mod(y1, gn2_ref[...], m_ref[:, 3 * D_MODEL:4 * D_MODEL], m_ref[:, 4 * D_MODEL:5 * D_MODEL]).astype(BF16)
    nchunk = D_FF // MLP_CHUNK
    acc = jnp.zeros(y1.shape, F32)
    for c in range(nchunk):
        cols = slice(c * MLP_CHUNK, (c + 1) * MLP_CHUNK)
        if loading is not None:
            per = MLP_CHUNK // MLP_LOAD
            for p in range(c * per, (c + 1) * per):
                if p + 1 < D_FF // MLP_LOAD:
                    loading.start(p + 1)
                loading.finish(p)
        a = _dot(h, w1_ref[:, cols])
        a = jnp.square(jnp.maximum(a, 0.0)).astype(BF16)
        acc = acc + _dot(a, w2_ref[cols, :])
    return y1 + m_ref[:, 5 * D_MODEL:6 * D_MODEL] * acc


def _run_tail(x, mix, m_ref, gn2_ref, weights, w1_ref, w2_ref, emit):
    first = pl.program_id(0) == 0

    @pl.when(first)
    def _():
        emit(_mlp_tail(x, mix, m_ref, gn2_ref, w1_ref, w2_ref, loading=weights))

    @pl.when(jnp.logical_not(first))
    def _():
        emit(_mlp_tail(x, mix, m_ref, gn2_ref, w1_ref, w2_ref))


def _even_out_kernel(xp_ref, xs_ref, y5_ref, u_ref, dskip_ref, wglu_ref, bglu_ref, glap_ref, glas_ref, wout_ref,
                     m_ref, gn2_ref, w1_hbm, w2_hbm, o_ref, w1_ref, w2_ref, stage1, stage2, sem, *, layer):
    weights = _MlpWeights(w1_hbm, w2_hbm, w1_ref, w2_ref, stage1, stage2, sem, layer)

    @pl.when(pl.program_id(0) == 0)
    def _():
        weights.start(0)

    nblk = S5_WIDTH // 128
    ys = (jnp.concatenate([y5_ref[b] for b in range(nblk)], axis=1)
          + jnp.concatenate([u_ref[b] for b in range(nblk)], axis=1) * dskip_ref[...])
    gl = jax.nn.gelu(ys)
    s5o = gl * jax.nn.sigmoid(_dot(gl.astype(BF16), wglu_ref[...]) + bglu_ref[...])
    gla = _token_tile(glap_ref, glas_ref, _OUT_TM).astype(BF16)
    mix = _dot(s5o.astype(BF16), wout_ref[0:S5_WIDTH, :]) + _dot(gla, wout_ref[S5_WIDTH:, :])

    def emit(y):
        o_ref[...] = y

    _run_tail(_token_tile(xp_ref, xs_ref, _OUT_TM), mix, m_ref, gn2_ref, weights, w1_ref, w2_ref, emit)


def _odd_out_kernel(x_ref, attp_ref, atts_ref, wo_ref, m_ref, gn2_ref, w1_hbm, w2_hbm, op_ref, os_ref,
                    w1_ref, w2_ref, stage1, stage2, sem, *, layer):
    weights = _MlpWeights(w1_hbm, w2_hbm, w1_ref, w2_ref, stage1, stage2, sem, layer)

    @pl.when(pl.program_id(0) == 0)
    def _():
        weights.start(0)

    mix = _dot(_token_tile(attp_ref, atts_ref, _OUT_TM), wo_ref[...])
    is_prompt = pl.program_id(0) < T_PROMPT // _OUT_TM

    def emit(y):
        @pl.when(is_prompt)
        def _():
            op_ref[...] = y

        @pl.when(jnp.logical_not(is_prompt))
        def _():
            os_ref[...] = y

    _run_tail(x_ref[...], mix, m_ref, gn2_ref, weights, w1_ref, w2_ref, emit)


_OUT_TM = 512


def _const_spec(shape):
    return pl.BlockSpec(shape, lambda i: (0,) * len(shape), pipeline_mode=pl.Buffered(1))


def _tail_specs(layer):
    tm = _OUT_TM
    return [
        pl.BlockSpec((None, 1, 6 * D_MODEL), lambda i: (layer * COND_ROWS + _cond_row(i, tm), 0, 0)),
        _const_spec((1, D_MODEL)),
        pl.BlockSpec(memory_space=pl.ANY),
        pl.BlockSpec(memory_space=pl.ANY),
    ]


def _tail_scratch():
    return [pltpu.VMEM((D_MODEL, D_FF), BF16), pltpu.VMEM((D_FF, D_MODEL), BF16),
            pltpu.VMEM((2, D_MODEL, MLP_LOAD), F32), pltpu.VMEM((2, MLP_LOAD, D_MODEL), F32),
            pltpu.SemaphoreType.DMA((2, 2))]


_TAIL_PARAMS = dict(dimension_semantics=("arbitrary",), vmem_limit_bytes=VMEM_LIMIT)


def _even_out_call(xp, xs, y5, u, d_skip, w_glu, b_glu, gla_p, gla_s, w_out, mods, layer, gn2, w1, w2):
    tm = _OUT_TM
    return pl.pallas_call(
        functools.partial(_even_out_kernel, layer=layer),
        out_shape=jax.ShapeDtypeStruct((T_TOK, D_MODEL), F32),
        grid=(T_TOK // tm,),
        in_specs=_token_specs(tm) + [
            pl.BlockSpec((S5_WIDTH // 128, tm, 128), lambda i: (0, i, 0)),
            pl.BlockSpec((S5_WIDTH // 128, tm, 128), lambda i: (0, i, 0)),
            _const_spec((1, S5_WIDTH)),
            _const_spec((S5_WIDTH, S5_WIDTH)),
            _const_spec((1, S5_WIDTH)),
        ] + _token_specs(tm, GLA_VW) + [
            _const_spec((S5_WIDTH + GLA_VW, D_MODEL)),
        ] + _tail_specs(layer),
        out_specs=pl.BlockSpec((tm, D_MODEL), lambda i: (i, 0)),
        scratch_shapes=_tail_scratch(),
        compiler_params=pltpu.CompilerParams(**_TAIL_PARAMS),
        name="even_out_mlp",
    )(xp, xs, y5, u, d_skip, w_glu, b_glu, gla_p, gla_s, w_out, mods, gn2, w1, w2)


def _odd_out_call(x, att_p, att_s, w_o, mods, layer, gn2, w1, w2):
    tm = _OUT_TM
    return pl.pallas_call(
        functools.partial(_odd_out_kernel, layer=layer),
        out_shape=(jax.ShapeDtypeStruct((T_PROMPT, D_MODEL), F32),
                   jax.ShapeDtypeStruct((T_SAMPLE, D_MODEL), F32)),
        grid=(T_TOK // tm,),
        in_specs=[pl.BlockSpec((tm, D_MODEL), lambda i: (i, 0))] + _token_specs(tm) + [
            _const_spec((D_MODEL, D_MODEL)),
        ] + _tail_specs(layer),
        out_specs=tuple(_token_specs(tm)),
        scratch_shapes=_tail_scratch(),
        compiler_params=pltpu.CompilerParams(**_TAIL_PARAMS),
        name="odd_out_mlp",
    )(x, att_p, att_s, w_o, mods, gn2, w1, w2)


def _qkv_kernel(x_ref, gn_ref, m_ref, w_ref, qn_ref, kn_ref, cos_ref, sin_ref,
                q_ref, kb_ref, vb_ref, k32_ref, v32_ref, *, tile):
    h = _norm_mod(x_ref[...], gn_ref[...], m_ref[:, 0:D_MODEL], m_ref[:, D_MODEL:2 * D_MODEL]).astype(BF16)
    z = _dot(h, w_ref[...])
    v = z[:, (N_HEADS + KV_HEADS) * HEAD_DIM:]
    vb_ref[...] = v.astype(BF16)
    even_lane = (lax.broadcasted_iota(jnp.int32, (1, HEAD_DIM), 1) & 1) == 0

    def heads(rope):
        for hd in range(N_HEADS + KV_HEADS):
            xh = z[:, hd * HEAD_DIM:(hd + 1) * HEAD_DIM]
            gain = qn_ref[...] if hd < N_HEADS else kn_ref[...]
            xh = xh * lax.rsqrt(jnp.mean(xh * xh, axis=-1, keepdims=True) + EPS) * gain
            if rope:
                partner = jnp.where(even_lane, pltpu.roll(xh, HEAD_DIM - 1, 1), pltpu.roll(xh, 1, 1))
                xh = xh * cos_ref[...] + partner * sin_ref[...]
            if hd < N_HEADS:
                q_ref[:, hd * HEAD_DIM:(hd + 1) * HEAD_DIM] = xh.astype(BF16)
            else:
                cols = slice((hd - N_HEADS) * HEAD_DIM, (hd - N_HEADS + 1) * HEAD_DIM)
                kb_ref[:, cols] = xh.astype(BF16)
                if not rope:
                    k32_ref[:, hd - N_HEADS, :] = xh

    is_sample = pl.program_id(0) >= T_PROMPT // tile

    @pl.when(is_sample)
    def _():
        heads(True)

    @pl.when(jnp.logical_not(is_sample))
    def _():
        heads(False)
        for kh in range(KV_HEADS):
            v32_ref[:, kh, :] = v[:, kh * HEAD_DIM:(kh + 1) * HEAD_DIM]


def _qkv_call(x, gn, mods, layer, w_qkv, q_norm, k_norm, cos_t, sin_t):
    tm = 512
    pos_tiles = DEC_SEQ // tm
    n_prompt = T_PROMPT // tm
    kvw = KV_HEADS * HEAD_DIM

    def pos_map(i):
        return (jnp.maximum(i - n_prompt, 0) % pos_tiles, 0)

    def prompt_map(i):
        return (jnp.minimum(i, n_prompt - 1), 0, 0)

    return pl.pallas_call(
        functools.partial(_qkv_kernel, tile=tm),
        out_shape=(jax.ShapeDtypeStruct((T_TOK, N_HEADS * HEAD_DIM), BF16),
                   jax.ShapeDtypeStruct((T_TOK, kvw), BF16),
                   jax.ShapeDtypeStruct((T_TOK, kvw), BF16),
                   jax.ShapeDtypeStruct((T_PROMPT, KV_HEADS, HEAD_DIM), F32),
                   jax.ShapeDtypeStruct((T_PROMPT, KV_HEADS, HEAD_DIM), F32)),
        grid=(T_TOK // tm,),
        in_specs=[
            pl.BlockSpec((tm, D_MODEL), lambda i: (i, 0)),
            pl.BlockSpec((1, D_MODEL), lambda i: (0, 0)),
            pl.BlockSpec((None, 1, 6 * D_MODEL), lambda i: (layer * COND_ROWS + _cond_row(i, tm), 0, 0)),
            pl.BlockSpec(w_qkv.shape, lambda i: (0, 0)),
            pl.BlockSpec((1, HEAD_DIM), lambda i: (0, 0)),
            pl.BlockSpec((1, HEAD_DIM), lambda i: (0, 0)),
            pl.BlockSpec((tm, HEAD_DIM), pos_map),
            pl.BlockSpec((tm, HEAD_DIM), pos_map),
        ],
        out_specs=(pl.BlockSpec((tm, N_HEADS * HEAD_DIM), lambda i: (i, 0)),
                   pl.BlockSpec((tm, kvw), lambda i: (i, 0)),
                   pl.BlockSpec((tm, kvw), lambda i: (i, 0)),
                   pl.BlockSpec((tm, KV_HEADS, HEAD_DIM), prompt_map),
                   pl.BlockSpec((tm, KV_HEADS, HEAD_DIM), prompt_map)),
        compiler_params=pltpu.CompilerParams(vmem_limit_bytes=VMEM_LIMIT),
        name="odd_qkv",
    )(x, gn, mods, w_qkv, q_norm, k_norm, cos_t, sin_t)


def _rope_tables():
    f32 = np.float32
    rows = DEC_SEQ // GRID_W
    row = np.repeat(np.arange(rows, dtype=f32), GRID_W)
    col = np.tile(np.arange(GRID_W, dtype=f32), rows)
    inv = np.power(f32(ROPE_THETA), -np.arange(0, AXIS_DIM, 2, dtype=f32) / f32(AXIS_DIM)).astype(f32)
    ang = np.concatenate([row[:, None] * inv, col[:, None] * inv], axis=-1).astype(f32)
    cos_t = np.repeat(np.cos(ang), 2, axis=-1).astype(f32)
    sin = np.sin(ang).astype(f32)
    sin_t = np.stack([-sin, sin], axis=-1).reshape(DEC_SEQ, HEAD_DIM)
    return jnp.asarray(cos_t), jnp.asarray(sin_t)


def _attn_kernel(*refs, has_cache):
    q_ref, k_ref, v_ref = refs[:3]
    ck_ref, cv_ref = refs[3:5] if has_cache else (None, None)
    o_ref = refs[-1]
    c = HEAD_DIM ** -0.5 * math.log2(math.e)
    ones_col = (lax.broadcasted_iota(jnp.int32, (1, HEAD_DIM), 1) == 0).astype(BF16)

    def with_ones(v):
        return jnp.concatenate([v, jnp.broadcast_to(ones_col, v.shape)], axis=1)

    k = k_ref[...]
    v = with_ones(v_ref[...])
    if has_cache:
        ck = ck_ref[...].astype(BF16)
        cv = with_ones(cv_ref[...].astype(BF16))
    for r in range(Q_PER_KV):
        cs = slice(r * HEAD_DIM, (r + 1) * HEAD_DIM)
        q = q_ref[:, cs]
        s = lax.dot_general(q, k, NT_DIMS, preferred_element_type=F32)
        m = jnp.max(s, axis=-1, keepdims=True)
        if has_cache:
            sc = lax.dot_general(q, ck, NT_DIMS, preferred_element_type=F32)
            m = jnp.maximum(m, jnp.max(sc, axis=-1, keepdims=True))
        mc = m * c
        o = _dot(jnp.exp2(s * c - mc).astype(BF16), v)
        if has_cache:
            o = o + _dot(jnp.exp2(sc * c - mc).astype(BF16), cv)
        o_ref[:, cs] = (o[:, 0:HEAD_DIM] / o[:, HEAD_DIM:HEAD_DIM + 1]).astype(BF16)


def _attn_call(q, k, v, cache_k, cache_v, seq_len, row0, nrows):
    assert row0 % seq_len == 0 and nrows % seq_len == 0
    has_cache = cache_k is not None
    b0 = row0 // seq_len
    gw = Q_PER_KV * HEAD_DIM
    in_specs = [
        pl.BlockSpec((seq_len, gw), lambda b, g: (b0 + b, g)),
        pl.BlockSpec((seq_len, HEAD_DIM), lambda b, g: (b0 + b, g)),
        pl.BlockSpec((seq_len, HEAD_DIM), lambda b, g: (b0 + b, g)),
    ]
    args = [q, k, v]
    if has_cache:
        in_specs += [pl.BlockSpec((PAST_LEN, HEAD_DIM), lambda b, g: (b, g)),
                     pl.BlockSpec((PAST_LEN, HEAD_DIM), lambda b, g: (b, g))]
        args += [cache_k, cache_v]
    return pl.pallas_call(
        functools.partial(_attn_kernel, has_cache=has_cache),
        out_shape=jax.ShapeDtypeStruct((nrows, N_HEADS * HEAD_DIM), BF16),
        grid=(nrows // seq_len, KV_HEADS),
        in_specs=in_specs,
        out_specs=pl.BlockSpec((seq_len, gw), lambda b, g: (b, g)),
        compiler_params=pltpu.CompilerParams(vmem_limit_bytes=VMEM_LIMIT),
        name=f"attn_len{seq_len}",
    )(*args)


def kernel(x_prompt, x_sample, state_s5_re, state_s5_im, state_gla, cache_k, cache_v, c, c_ctx, norm_mix, norm_mlp, w_ada, b_ada, w_mlp_in, w_mlp_out, w_in_e, w_out_e, s5_lambda_re, s5_lambda_im, s5_log_dt, s5_b_re, s5_b_im, s5_c_re, s5_c_im, s5_d, s5_w_glu, s5_b_glu, gla_w_gate2, gla_b_gate, gla_norm, w_qkv_o, w_o_o, q_norm, k_norm):
    xp = x_prompt.reshape(T_PROMPT, D_MODEL)
    xs = x_sample.reshape(T_SAMPLE, D_MODEL)
    cond8 = jnp.concatenate([c_ctx[None, :], c, jnp.zeros((COND_ROWS - 1 - DEC_BATCH, D_MODEL), F32)], axis=0)
    mods = _ada_call(cond8, w_ada, b_ada)
    w1_all, w2_all = w_mlp_in, w_mlp_out

    n_main = S5_WIDTH + 2 * GLA_QK + 2 * GLA_VW
    w_in = w_in_e[0]
    w_main = w_in[:, :n_main].astype(BF16)
    w_glr = jnp.pad(w_in[:, n_main:], ((0, 0), (0, 128 - 2 * GLA_RANK))).astype(BF16)
    zg = jnp.zeros((GLA_RANK, GLA_QK), F32)
    w_gate = jnp.concatenate([jnp.concatenate([gla_w_gate2[0, 0], zg], axis=1),
                              jnp.concatenate([zg, gla_w_gate2[0, 1]], axis=1),
                              jnp.zeros((128 - 2 * GLA_RANK, 2 * GLA_QK), F32)], axis=0).astype(BF16)
    b_gate = gla_b_gate[0].reshape(1, 2 * GLA_QK)
    u, z, g = _inproj_call(xp, xs, norm_mix[0:1], mods, 0, w_main, w_glr, w_gate, b_gate)

    mats = _s5_prep_call(s5_lambda_re[0], s5_lambda_im[0], s5_log_dt[0], s5_b_re[0], s5_b_im[0],
                         s5_c_re[0], s5_c_im[0])

    def state_rows(s):
        return jnp.transpose(s, (2, 0, 1, 3)).reshape(S5_GROUPS, DEC_BATCH, 2 * S5_STATE)

    h0 = jnp.concatenate([state_rows(state_s5_re[:, 0]), state_rows(state_s5_im[:, 0])], axis=-1)
    nsteps = S5_GROUPS // S5_GPB
    h0 = jnp.transpose(h0.reshape(nsteps, S5_GPB, DEC_BATCH, S5_W), (0, 2, 1, 3))
    y5, ns = _s5_call(u, mats, h0)
    ns = jnp.transpose(ns, (0, 2, 1, 3)).reshape(S5_GROUPS, BATCH, S5_W)

    def state_out(n):
        return jnp.transpose(n.reshape(S5_GROUPS, BATCH, 2, S5_STATE), (1, 2, 0, 3))[:, None]

    new_s5_re = state_out(ns[:, :, :2 * S5_STATE])
    new_s5_im = state_out(ns[:, :, 2 * S5_STATE:])

    gn_gla = gla_norm[0].reshape(1, GLA_DV)
    gla_p, sfin = _gla_call(z, g, gn_gla, None, SEQ, BATCH, 0, nsub=4)
    s0 = state_gla[:, 0].reshape(DEC_BATCH, 2, GLA_QK, GLA_DV)
    gla_s, _ = _gla_call(z, g, gn_gla, s0, DEC_SEQ, DEC_BATCH, T_PROMPT, nsub=1)
    new_gla = sfin.reshape(BATCH, 1, 2, GLA_HEADS, GLA_DK, GLA_DV)

    x = _even_out_call(xp, xs, y5, u, s5_d[0].reshape(1, S5_WIDTH), s5_w_glu[0].astype(BF16),
                       s5_b_glu[0].reshape(1, S5_WIDTH), gla_p, gla_s, w_out_e[0].astype(BF16), mods, 0,
                       norm_mlp[0:1], w1_all, w2_all)

    cos_t, sin_t = _rope_tables()
    q, k, v, k32, v32 = _qkv_call(x, norm_mix[1:2], mods, 1, w_qkv_o[0].astype(BF16),
                                  q_norm[0].reshape(1, HEAD_DIM), k_norm[0].reshape(1, HEAD_DIM), cos_t, sin_t)
    att_p = _attn_call(q, k, v, None, None, SEQ, 0, T_PROMPT)
    ck = cache_k[:, 0].reshape(DEC_BATCH * PAST_LEN, KV_HEADS * HEAD_DIM)
    cv = cache_v[:, 0].reshape(DEC_BATCH * PAST_LEN, KV_HEADS * HEAD_DIM)
    att_s = _attn_call(q, k, v, ck, cv, DEC_SEQ, T_PROMPT, T_SAMPLE)
    yp, ys = _odd_out_call(x, att_p, att_s, w_o_o[0].astype(BF16), mods, 1, norm_mlp[1:2],
                           w1_all, w2_all)

    new_k = k32.reshape(BATCH, 1, SEQ, KV_HEADS, HEAD_DIM)
    new_v = v32.reshape(BATCH, 1, SEQ, KV_HEADS, HEAD_DIM)
    y_prompt = yp.reshape(BATCH, SEQ, D_MODEL)
    y_sample = ys.reshape(DEC_BATCH, DEC_SEQ, D_MODEL)
    return (y_prompt, y_sample, new_s5_re, new_s5_im, new_gla, new_k, new_v)
```

```python
import functools
import math

import jax
import jax.numpy as jnp
import numpy as np
from jax import lax
from jax.experimental import pallas as pl
from jax.experimental.pallas import tpu as pltpu

F32 = jnp.float32
BF16 = jnp.bfloat16

D_MODEL = 1024
BATCH = 16
SEQ = 256
DEPTH = 2
DEC_BATCH = 4
DEC_SEQ = 1024
PAST_LEN = 512
GRID_W = 64
S5_WIDTH = 512
S5_GROUP_CH = 16
S5_GROUPS = 32
S5_STATE = 64
GLA_HEADS = 4
GLA_VW = 512
GLA_DV = 128
GLA_DK = 64
GLA_QK = 256
GLA_RANK = 16
GLA_TAU = 16.0
GLA_CHUNK = 64
GLA_CPB = 4
GLA_BLK = GLA_CPB * GLA_CHUNK
HEAD_DIM = 128
N_HEADS = 8
KV_HEADS = 2
Q_PER_KV = N_HEADS // KV_HEADS
AXIS_DIM = 64
ROPE_THETA = 10000.0
D_FF = 4096
EPS = 1e-6

T_PROMPT = BATCH * SEQ
T_SAMPLE = DEC_BATCH * DEC_SEQ
T_TOK = T_PROMPT + T_SAMPLE
COND_ROWS = 8
COND_SPAN = 1024
PROMPT_SPANS = T_PROMPT // COND_SPAN

S5_Q = 16
S5_W = S5_Q * S5_GROUP_CH
S5_GPB = 128 // S5_GROUP_CH
S5_ROWS = T_TOK // S5_Q
S5_PROMPT_ROWS = T_PROMPT // S5_Q
S5_PROMPT_CHUNKS = SEQ // S5_Q
S5_SAMPLE_CHUNKS = DEC_SEQ // S5_Q

VMEM_LIMIT = 56 * 1024 * 1024
VMEM_LIMIT_TAIL = 60 * 1024 * 1024

NT_DIMS = (((1,), (1,)), ((), ()))
TN_DIMS = (((0,), (0,)), ((), ()))


def _cond_row(i, tile):
    return jnp.maximum((i * tile) // COND_SPAN - (PROMPT_SPANS - 1), 0)


def _norm_mod(x, gain, shift, scale):
    y = x * lax.rsqrt(jnp.mean(x * x, axis=-1, keepdims=True) + EPS)
    return (y * gain) * (1.0 + scale) + shift


def _dot(a, b):
    return jnp.dot(a, b, preferred_element_type=F32)


def _ada_kernel(cond_ref, w_ref, b_ref, o_ref):
    s = jax.nn.silu(cond_ref[...]).astype(BF16)
    o_ref[:, 0, :] = _dot(s, w_ref[...].astype(BF16)) + b_ref[...]


def _ada_call(cond8, w_ada, b_ada):
    tn = 2048
    nj = 6 * D_MODEL // tn
    return pl.pallas_call(
        _ada_kernel,
        out_shape=jax.ShapeDtypeStruct((DEPTH * COND_ROWS, 1, 6 * D_MODEL), F32),
        grid=(DEPTH, nj),
        in_specs=[
            pl.BlockSpec((COND_ROWS, D_MODEL), lambda l, j: (0, 0)),
            pl.BlockSpec((None, D_MODEL, tn), lambda l, j: (l, 0, j)),
            pl.BlockSpec((None, 1, tn), lambda l, j: (l, 0, j)),
        ],
        out_specs=pl.BlockSpec((COND_ROWS, 1, tn), lambda l, j: (l, 0, j)),
        compiler_params=pltpu.CompilerParams(vmem_limit_bytes=VMEM_LIMIT),
        name="ada_mod",
    )(cond8, w_ada, b_ada.reshape(DEPTH, 1, 6 * D_MODEL))


def _token_specs(tile, width=D_MODEL):
    n_prompt = T_PROMPT // tile
    return [pl.BlockSpec((tile, width), lambda i: (jnp.minimum(i, n_prompt - 1), 0)),
            pl.BlockSpec((tile, width), lambda i: (jnp.maximum(i - n_prompt, 0), 0))]


def _token_tile(xp_ref, xs_ref, tile):
    return jnp.where(pl.program_id(0) < T_PROMPT // tile, xp_ref[...], xs_ref[...])


def _inproj_kernel(xp_ref, xs_ref, gn_ref, m_ref, w_ref, wglr_ref, wg_ref, bg_ref, u_ref, z_ref, g_ref, *, tile):
    x = _token_tile(xp_ref, xs_ref, tile)
    h = _norm_mod(x, gn_ref[...], m_ref[:, 0:D_MODEL], m_ref[:, D_MODEL:2 * D_MODEL]).astype(BF16)
    z = _dot(h, w_ref[...])
    for blk in range(S5_WIDTH // 128):
        u_ref[blk] = z[:, blk * 128:(blk + 1) * 128]
    z_ref[...] = z[:, S5_WIDTH:]
    glr = _dot(h, wglr_ref[...]).astype(BF16)
    pre = _dot(glr, wg_ref[...]) + bg_ref[...]
    g_ref[...] = jax.nn.log_sigmoid(pre) * (1.0 / GLA_TAU)


def _inproj_call(xp, xs, gn, mods, layer, w_main, w_glr, w_gate, b_gate):
    tm = 512
    nz = w_main.shape[1]
    return pl.pallas_call(
        functools.partial(_inproj_kernel, tile=tm),
        out_shape=(jax.ShapeDtypeStruct((S5_WIDTH // 128, T_TOK, 128), F32),
                   jax.ShapeDtypeStruct((T_TOK, nz - S5_WIDTH), F32),
                   jax.ShapeDtypeStruct((T_TOK, 2 * GLA_QK), F32)),
        grid=(T_TOK // tm,),
        in_specs=_token_specs(tm) + [
            pl.BlockSpec((1, D_MODEL), lambda i: (0, 0)),
            pl.BlockSpec((None, 1, 6 * D_MODEL), lambda i: (layer * COND_ROWS + _cond_row(i, tm), 0, 0)),
            pl.BlockSpec((D_MODEL, nz), lambda i: (0, 0)),
            pl.BlockSpec((D_MODEL, 128), lambda i: (0, 0)),
            pl.BlockSpec((128, 2 * GLA_QK), lambda i: (0, 0)),
            pl.BlockSpec((1, 2 * GLA_QK), lambda i: (0, 0)),
        ],
        out_specs=(pl.BlockSpec((S5_WIDTH // 128, tm, 128), lambda i: (0, i, 0)),
                   pl.BlockSpec((tm, nz - S5_WIDTH), lambda i: (i, 0)),
                   pl.BlockSpec((tm, 2 * GLA_QK), lambda i: (i, 0))),
        compiler_params=pltpu.CompilerParams(vmem_limit_bytes=VMEM_LIMIT),
        name="even_inproj",
    )(xp, xs, gn, mods, w_main, w_glr, w_gate, b_gate)


S5_PREP_GPB = 8
_PREP_LRE, _PREP_LIM, _PREP_LDT = 0, 1, 2
_PREP_BT_RE, _PREP_BT_IM, _PREP_C_RE, _PREP_C_IM, _PREP_ROWS = 8, 24, 40, 56, 72


def _s5_prep_kernel(p_ref, cc_ref, t_ref, bq_ref, cqt_ref, be_ref, a_ref, t_scr, dd_scr):
    for gi in range(S5_PREP_GPB):
        _s5_prep_group(p_ref.at[gi], cc_ref.at[gi], t_ref.at[gi], bq_ref.at[gi], cqt_ref.at[gi], be_ref.at[gi],
                       a_ref.at[gi], t_scr, dd_scr)


def _s5_prep_group(p_ref, cc_ref, t_ref, bq_ref, cqt_ref, be_ref, a_ref, t_scr, dd_scr):
    gch = S5_GROUP_CH
    lre = p_ref[_PREP_LRE:_PREP_LRE + 1]
    lim = p_ref[_PREP_LIM:_PREP_LIM + 1]
    dt = jnp.exp(p_ref[_PREP_LDT:_PREP_LDT + 1])
    a = lre * dt
    th = lim * dt

    def lam_pow(k):
        mag = jnp.exp(k * a)
        return mag * jnp.cos(k * th), mag * jnp.sin(k * th)

    lb_re, lb_im = lam_pow(1.0)
    nr = lb_re - 1.0
    den = lre * lre + lim * lim
    cf_re = (nr * lre + lb_im * lim) / den
    cf_im = (lb_im * lre - nr * lim) / den
    bt_re = p_ref[_PREP_BT_RE:_PREP_BT_RE + gch]
    bt_im = p_ref[_PREP_BT_IM:_PREP_BT_IM + gch]
    bb_re = jnp.tile(cf_re * bt_re - cf_im * bt_im, (S5_Q, 1))
    bb_im = jnp.tile(cf_re * bt_im + cf_im * bt_re, (S5_Q, 1))

    shape = (S5_W, 128)
    pos = lax.shift_right_logical(lax.broadcasted_iota(jnp.int32, shape, 0), 4)
    is_f = lax.broadcasted_iota(jnp.int32, shape, 1) < S5_STATE
    posq = lax.broadcasted_iota(jnp.int32, (S5_Q, 128), 0).astype(F32)
    is_fq = lax.broadcasted_iota(jnp.int32, (S5_Q, 128), 1) < S5_STATE

    def per_channel(tbl):
        return jnp.broadcast_to(tbl[:, None, :], (S5_Q, S5_GROUP_CH, 128)).reshape(shape)

    p_re, p_im = map(per_channel, lam_pow(jnp.where(is_fq, (S5_Q - 1.0) - posq, posq)))
    w_re = p_re * bb_re - p_im * bb_im
    w_im = p_re * bb_im + p_im * bb_re
    bq = jnp.concatenate([w_re, w_im], axis=1)
    bqt = jnp.transpose(bq)
    bq_ref[...] = bqt.astype(BF16)

    edge = pos == jnp.where(is_f, 0, S5_Q - 1)
    be = jnp.concatenate([jnp.where(edge, bb_re, 0.0), jnp.where(edge, bb_im, 0.0)], axis=1)
    be_ref[...] = jnp.transpose(be).astype(BF16)

    q_re, q_im = map(per_channel, lam_pow(jnp.where(is_fq, posq + 1.0, S5_Q - posq)))
    ct_re = jnp.tile(p_ref[_PREP_C_RE:_PREP_C_RE + gch], (S5_Q, 1))
    ct_im = jnp.tile(p_ref[_PREP_C_IM:_PREP_C_IM + gch], (S5_Q, 1))
    g_re = q_re * ct_re - q_im * ct_im
    g_im = q_re * ct_im + q_im * ct_re
    cqt_ref[...] = jnp.concatenate([g_re, -g_im], axis=1).astype(BF16)

    a_re, a_im = lam_pow(float(S5_Q))
    a_ref[...] = jnp.concatenate([a_re, a_im], axis=1)

    kf = jnp.dot(cc_ref[0:gch], bqt, precision=lax.Precision.HIGHEST, preferred_element_type=F32)
    kb = jnp.dot(cc_ref[gch:2 * gch], bqt, precision=lax.Precision.HIGHEST, preferred_element_type=F32)
    lo = S5_W - gch
    dd_scr[:, 0:S5_W] = kf
    dd_scr[:, lo:lo + S5_W] = kb
    dd_scr[:, lo:S5_W] = kf[:, lo:S5_W] + kb[:, 0:gch]
    for t in range(S5_Q):
        c0 = (S5_Q - 1 - t) * gch
        t_scr[t * gch:(t + 1) * gch, :] = dd_scr[:, c0:c0 + S5_W]
    t_ref[...] = t_scr[...].astype(BF16)


def _s5_prep_call(lam_re, lam_im, log_dt, b_re, b_im, c_re, c_im):
    def fb(p):
        return jnp.transpose(p, (1, 0, 2)).reshape(S5_GROUPS, 1, 2 * S5_STATE)

    def dup(p):
        return jnp.concatenate([p, p], axis=-1)

    ldt = fb(jnp.broadcast_to(log_dt[:, :, None], (2, S5_GROUPS, S5_STATE)))
    pad = jnp.zeros((S5_GROUPS, _PREP_BT_RE - _PREP_LDT - 1, 128), F32)
    packed = jnp.concatenate([fb(lam_re), fb(lam_im), ldt, pad,
                              dup(jnp.transpose(b_re, (0, 2, 1))), dup(jnp.transpose(b_im, (0, 2, 1))),
                              dup(c_re), dup(c_im)], axis=1)
    zero = jnp.zeros_like(c_re)
    cc = jnp.concatenate([jnp.concatenate([c_re, zero, -c_im, zero], axis=-1),
                          jnp.concatenate([zero, c_re, zero, -c_im], axis=-1)], axis=1)

    gpb = S5_PREP_GPB
    sq = pl.BlockSpec((gpb, S5_W, S5_W), lambda g: (g, 0, 0))
    sq_shape = jax.ShapeDtypeStruct((S5_GROUPS, S5_W, S5_W), BF16)
    return pl.pallas_call(
        _s5_prep_kernel,
        out_shape=(sq_shape, sq_shape, sq_shape, sq_shape,
                   jax.ShapeDtypeStruct((S5_GROUPS, 1, S5_W), F32)),
        grid=(S5_GROUPS // gpb,),
        in_specs=[pl.BlockSpec((gpb, _PREP_ROWS, 128), lambda g: (g, 0, 0)),
                  pl.BlockSpec((gpb, 2 * S5_GROUP_CH, S5_W), lambda g: (g, 0, 0))],
        out_specs=(sq, sq, sq, sq, pl.BlockSpec((gpb, 1, S5_W), lambda g: (g, 0, 0))),
        scratch_shapes=[pltpu.VMEM((S5_W, S5_W), F32), pltpu.VMEM((S5_GROUP_CH, 2 * S5_W), F32)],
        name="s5_prep",
    )(packed, cc)


def _s5_kernel(u_ref, tt_ref, bqt_ref, cqt_ref, bet_ref, a_ref, h0_ref, y_ref, ns_ref,
               ut_scr, x_scr, spf_scr, spb_scr, ne_scr, yt_scr, xt_scr):
    gch = S5_GROUP_CH
    for s in range(S5_Q):
        rows = u_ref[pl.ds(s, S5_ROWS, stride=S5_Q), :]
        rows_t = jnp.transpose(rows).astype(BF16)
        for gl in range(S5_GPB):
            ut_scr[gl, s * gch:(s + 1) * gch, :] = rows_t[gl * gch:(gl + 1) * gch, :]

    for gl in range(S5_GPB):
        ut = ut_scr[gl]
        xt_scr[...] = _dot(bqt_ref[gl], ut)
        x = jnp.transpose(xt_scr[...])
        xt_scr[:, 0:S5_PROMPT_ROWS] = _dot(bet_ref[gl], ut[:, 0:S5_PROMPT_ROWS])
        ne = jnp.transpose(xt_scr[:, 0:S5_PROMPT_ROWS])
        for part in range(2):
            x_scr[part, pl.ds(gl, S5_ROWS, stride=S5_GPB), :] = x[:, part * 128:(part + 1) * 128]
            ne_scr[part, pl.ds(gl, S5_PROMPT_ROWS, stride=S5_GPB), :] = ne[:, part * 128:(part + 1) * 128]

    is_f = lax.broadcasted_iota(jnp.int32, (1, 128), 1) < S5_STATE
    a_re = a_ref[:, 0:128]
    a_im = a_ref[:, 128:256]

    def tile(row):
        return pl.ds(pl.multiple_of(row * S5_GPB, S5_GPB), S5_GPB)

    def scan(base, nseq, nchunk, s_init):
        def body(i, state):
            new = []
            for b in range(nseq):
                s_re, s_im = state[b]
                rows_f = tile(base + b * nchunk + i)
                rows_b = tile(base + b * nchunk + (nchunk - 1 - i))
                spf_scr[0, rows_f, :] = s_re
                spf_scr[1, rows_f, :] = s_im
                spb_scr[0, rows_b, :] = s_re
                spb_scr[1, rows_b, :] = s_im
                x_re = jnp.where(is_f, x_scr[0, rows_f, :], x_scr[0, rows_b, :])
                x_im = jnp.where(is_f, x_scr[1, rows_f, :], x_scr[1, rows_b, :])
                new.append((a_re * s_re - a_im * s_im + x_re, a_re * s_im + a_im * s_re + x_im))
            return tuple(new)

        lax.fori_loop(0, nchunk, body, tuple(s_init))

    zero = jnp.zeros((S5_GPB, 128), F32)
    scan(0, BATCH, S5_PROMPT_CHUNKS, [(zero, zero)] * BATCH)
    scan(S5_PROMPT_ROWS, DEC_BATCH, S5_SAMPLE_CHUNKS,
         [(h0_ref[b, :, 0:128], h0_ref[b, :, 128:256]) for b in range(DEC_BATCH)])

    for b in range(BATCH):
        first = pl.ds(b * S5_PROMPT_CHUNKS * S5_GPB, S5_GPB)
        last = pl.ds(((b + 1) * S5_PROMPT_CHUNKS - 1) * S5_GPB, S5_GPB)
        for part in range(2):
            ns_ref[b, :, part * 128:(part + 1) * 128] = jnp.where(is_f, ne_scr[part, first, :], ne_scr[part, last, :])

    for gl in range(S5_GPB):
        rows = pl.ds(gl, S5_ROWS, stride=S5_GPB)
        carried = jnp.concatenate([jnp.where(is_f, spf_scr[p, rows, :], spb_scr[p, rows, :]) for p in range(2)],
                                  axis=1).astype(BF16)
        yt = _dot(tt_ref[gl], ut_scr[gl]) + lax.dot_general(cqt_ref[gl], carried, NT_DIMS,
                                                            preferred_element_type=F32)
        for t in range(S5_Q):
            yt_scr[t, gl * gch:(gl + 1) * gch, :] = yt[t * gch:(t + 1) * gch, :]
    for t in range(S5_Q):
        y_ref[pl.ds(t, S5_ROWS, stride=S5_Q), :] = jnp.transpose(yt_scr[t])


def _s5_call(u, mats, h0):
    tt_m, bqt_m, cqt_m, bet_m, a_m = mats
    nsteps = S5_GROUPS // S5_GPB
    sq = pl.BlockSpec((S5_GPB, S5_W, S5_W), lambda g: (g, 0, 0))
    state_scr = pltpu.VMEM((2, S5_ROWS * S5_GPB, 128), F32)
    return pl.pallas_call(
        _s5_kernel,
        out_shape=(jax.ShapeDtypeStruct((nsteps, T_TOK, 128), F32),
                   jax.ShapeDtypeStruct((nsteps, BATCH, S5_GPB, S5_W), F32)),
        grid=(nsteps,),
        in_specs=[
            pl.BlockSpec((None, T_TOK, 128), lambda g: (g, 0, 0)),
            sq, sq, sq, sq,
            pl.BlockSpec((S5_GPB, S5_W), lambda g: (g, 0)),
            pl.BlockSpec((None, DEC_BATCH, S5_GPB, S5_W), lambda g: (g, 0, 0, 0)),
        ],
        out_specs=(pl.BlockSpec((None, T_TOK, 128), lambda g: (g, 0, 0)),
                   pl.BlockSpec((None, BATCH, S5_GPB, S5_W), lambda g: (g, 0, 0, 0))),
        scratch_shapes=[pltpu.VMEM((S5_GPB, S5_W, S5_ROWS), BF16), state_scr, state_scr, state_scr,
                        pltpu.VMEM((2, S5_PROMPT_ROWS * S5_GPB, 128), F32),
                        pltpu.VMEM((S5_Q, 128, S5_ROWS), F32), pltpu.VMEM((S5_W, S5_ROWS), F32)],
        compiler_params=pltpu.CompilerParams(vmem_limit_bytes=VMEM_LIMIT),
        name="s5_scan",
    )(u, tt_m, bqt_m, cqt_m, bet_m, a_m.reshape(S5_GROUPS, S5_W), h0)


def _split_bf16(x):
    hi = x.astype(BF16)
    r1 = x - hi.astype(F32)
    mid = r1.astype(BF16)
    lo = (r1 - mid.astype(F32)).astype(BF16)
    return hi, mid, lo


def _gla_kernel(*refs, seq_len, nsub, has_s0):
    rows_refs, gn_ref = refs[:6], refs[6]
    s0_ref = refs[7] if has_s0 else None
    o_ref, sfin_ref = refs[7 + has_s0:9 + has_s0]
    scratch = refs[9 + has_s0:]
    for j in range(nsub):
        rows = pl.ds(j * seq_len, seq_len)
        _gla_sequence(*[r.at[rows, :] for r in rows_refs], gn_ref, s0_ref.at[j] if has_s0 else None,
                      o_ref.at[rows, :], sfin_ref.at[j], *[s.at[j] for s in scratch], seq_len=seq_len)


def _gla_sequence(q_ref, k_ref, v_ref, gf_ref, gb_ref, r_ref, gn_ref, s0_ref, o_ref, sfin_ref,
                  oi_scr, qd_scr, kv_scr, dec_scr, ss_scr, *, seq_len):
    has_s0 = s0_ref is not None
    nblk = seq_len // GLA_BLK
    nchunk = seq_len // GLA_CHUNK
    cl = GLA_CHUNK
    ti = lax.broadcasted_iota(jnp.int32, (GLA_BLK, GLA_BLK), 0)
    si = lax.broadcasted_iota(jnp.int32, (GLA_BLK, GLA_BLK), 1)
    same = lax.shift_right_logical(ti, 6) == lax.shift_right_logical(si, 6)
    keep = (same & (ti >= si), same & (ti <= si))
    tri = tuple(kp.astype(BF16) for kp in keep)
    lane_head = lax.shift_right_logical(lax.broadcasted_iota(jnp.int32, (cl, GLA_QK), 1), 6)
    zeros_v = jnp.zeros((cl, GLA_DV), BF16)
    heads = [(slice(h * GLA_DK, (h + 1) * GLA_DK), slice(h * GLA_DV, (h + 1) * GLA_DV)) for h in range(GLA_HEADS)]

    for j in range(nblk):
        rows = slice(j * GLA_BLK, (j + 1) * GLA_BLK)
        q = q_ref[rows, :] * (GLA_DK ** -0.5)
        k = k_ref[rows, :]
        v = v_ref[rows, :].astype(BF16)
        qd, kd, k2t = [], [], []
        for d, g_ref in enumerate((gf_ref, gb_ref)):
            b = sum(_dot(tri[d], part) for part in _split_bf16(g_ref[rows, :]))
            last = cl - 1 if d == 0 else 0
            b_last = [b[c * cl + last:c * cl + last + 1] for c in range(GLA_CPB)]
            bl = jnp.concatenate([jnp.broadcast_to(x, (cl, GLA_QK)) for x in b_last], axis=0)
            qd_d = (q * jnp.exp(b)).astype(BF16)
            qd_scr[d, rows, :] = qd_d
            qd.append(qd_d)
            kd.append((k * jnp.exp(-b)).astype(BF16))
            k2t.append(jnp.transpose(k * jnp.exp(bl - b)).astype(BF16))
            for c in range(GLA_CPB):
                dec_scr[d, j * GLA_CPB + c] = jnp.exp(jnp.transpose(jnp.broadcast_to(b_last[c], (GLA_DV, GLA_QK))))
        for h, (ks, vs) in enumerate(heads):
            att = [jnp.where(keep[d], lax.dot_general(qd[d][:, ks], kd[d][:, ks], NT_DIMS,
                                                      preferred_element_type=F32), 0.0) for d in range(2)]
            oi_scr[rows, vs] = _dot((att[0] + att[1]).astype(BF16), v[:, vs])
            vh = v[:, vs]
            vexp = jnp.concatenate(
                [jnp.concatenate([vh[c * cl:(c + 1) * cl] if c2 == c else zeros_v for c2 in range(GLA_CPB)], axis=1)
                 for c in range(GLA_CPB)], axis=0)
            for d in range(2):
                kv_scr[d, j, h] = _dot(k2t[d][ks, :], vexp)

    for d in range(2):
        s = s0_ref[d] if has_s0 else jnp.zeros((GLA_QK, GLA_DV), F32)
        for cg in (range(nchunk) if d == 0 else range(nchunk - 1, -1, -1)):
            j, c = divmod(cg, GLA_CPB)
            ss_scr[d, cg] = s.astype(BF16)
            kv = jnp.concatenate([kv_scr[d, j, h, :, c * GLA_DV:(c + 1) * GLA_DV] for h in range(GLA_HEADS)], axis=0)
            s = s * dec_scr[d, cg] + kv
        sfin_ref[d] = s

    for cg in range(nchunk):
        rows = slice(cg * cl, (cg + 1) * cl)
        inter = []
        for d in range(2):
            qc = qd_scr[d, rows, :]
            qstack = jnp.concatenate([jnp.where(lane_head == h, qc, jnp.zeros_like(qc)) for h in range(GLA_HEADS)],
                                     axis=0)
            inter.append(_dot(qstack, ss_scr[d, cg]))
        gate = jax.nn.silu(r_ref[rows, :])
        for h, (ks, vs) in enumerate(heads):
            hr = slice(h * cl, (h + 1) * cl)
            oh = oi_scr[rows, vs] + inter[0][hr] + inter[1][hr]
            oh = oh * lax.rsqrt(jnp.mean(oh * oh, axis=-1, keepdims=True) + EPS) * gn_ref[...]
            o_ref[rows, vs] = oh * gate[:, vs]


def _gla_call(z, g, gla_norm, s0, seq_len, nseq, row0, nsub):
    blk = nsub * seq_len
    assert row0 % blk == 0 and nseq % nsub == 0
    r0 = row0 // blk
    has_s0 = s0 is not None
    qk_off = 0
    v_off = 2 * GLA_QK // GLA_VW
    in_specs = [
        pl.BlockSpec((blk, GLA_QK), lambda i: (r0 + i, qk_off)),
        pl.BlockSpec((blk, GLA_QK), lambda i: (r0 + i, qk_off + 1)),
        pl.BlockSpec((blk, GLA_VW), lambda i: (r0 + i, v_off)),
        pl.BlockSpec((blk, GLA_QK), lambda i: (r0 + i, 0)),
        pl.BlockSpec((blk, GLA_QK), lambda i: (r0 + i, 1)),
        pl.BlockSpec((blk, GLA_VW), lambda i: (r0 + i, v_off + 1)),
        pl.BlockSpec((1, GLA_DV), lambda i: (0, 0)),
    ]
    args = [z, z, z, g, g, z, gla_norm]
    state_spec = pl.BlockSpec((nsub, 2, GLA_QK, GLA_DV), lambda i: (i, 0, 0, 0))
    if has_s0:
        in_specs.append(state_spec)
        args.append(s0)
    return pl.pallas_call(
        functools.partial(_gla_kernel, seq_len=seq_len, nsub=nsub, has_s0=has_s0),
        out_shape=(jax.ShapeDtypeStruct((nseq * seq_len, GLA_VW), F32),
                   jax.ShapeDtypeStruct((nseq, 2, GLA_QK, GLA_DV), F32)),
        grid=(nseq // nsub,),
        in_specs=in_specs,
        out_specs=(pl.BlockSpec((blk, GLA_VW), lambda i: (i, 0)), state_spec),
        scratch_shapes=[
            pltpu.VMEM((nsub, seq_len, GLA_VW), F32),
            pltpu.VMEM((nsub, 2, seq_len, GLA_QK), BF16),
            pltpu.VMEM((nsub, 2, seq_len // GLA_BLK, GLA_HEADS, GLA_DK, GLA_CPB * GLA_DV), F32),
            pltpu.VMEM((nsub, 2, seq_len // GLA_CHUNK, GLA_QK, GLA_DV), F32),
            pltpu.VMEM((nsub, 2, seq_len // GLA_CHUNK, GLA_QK, GLA_DV), BF16),
        ],
        compiler_params=pltpu.CompilerParams(vmem_limit_bytes=VMEM_LIMIT),
        name=f"gla_len{seq_len}",
    )(*args)


MLP_CHUNK = 512
MLP_LOAD = 256
MLP_SLOTS = 4


class _MlpWeights:
    def __init__(self, w1_hbm, w2_hbm, w1_scr, w2_scr, stage1, stage2, sem, layer):
        self.refs = (w1_hbm, w2_hbm, w1_scr, w2_scr, stage1, stage2, sem)
        self.layer = layer

    def _copies(self, p):
        w1_hbm, w2_hbm, _, _, stage1, stage2, sem = self.refs
        cols = pl.ds(p * MLP_LOAD, MLP_LOAD)
        slot = p % MLP_SLOTS
        return (pltpu.make_async_copy(w1_hbm.at[self.layer, :, cols], stage1.at[slot], sem.at[0, slot]),
                pltpu.make_async_copy(w2_hbm.at[self.layer, cols, :], stage2.at[slot], sem.at[1, slot]))

    def start(self, p):
        for cp in self._copies(p):
            cp.start()

    def prefetch(self):
        for p in range(MLP_SLOTS - 1):
            self.start(p)

    def finish(self, p):
        _, _, w1_scr, w2_scr, stage1, stage2, _ = self.refs
        ahead = p + MLP_SLOTS - 1
        if ahead < D_FF // MLP_LOAD:
            self.start(ahead)
        for cp in self._copies(p):
            cp.wait()
        cols = slice(p * MLP_LOAD, (p + 1) * MLP_LOAD)
        w1_scr[:, cols] = stage1[p % MLP_SLOTS].astype(BF16)
        w2_scr[cols, :] = stage2[p % MLP_SLOTS].astype(BF16)


def _mlp_tail(x, mix, m_ref, gn2_ref, w1_ref, w2_ref, loading=None):
    y1 = x + m_ref[:, 2 * D_MODEL:3 * D_MODEL] * mix
    h = _norm_mod(y1, gn2_ref[...], m_ref[:, 3 * D_MODEL:4 * D_MODEL], m_ref[:, 4 * D_MODEL:5 * D_MODEL]).astype(BF16)
    nchunk = D_FF // MLP_CHUNK
    acc = jnp.zeros(y1.shape, F32)
    for c in range(nchunk):
        cols = slice(c * MLP_CHUNK, (c + 1) * MLP_CHUNK)
        if loading is not None:
            per = MLP_CHUNK // MLP_LOAD
            for p in range(c * per, (c + 1) * per):
                loading.finish(p)
        a = _dot(h, w1_ref[:, cols])
        a = jnp.square(jnp.maximum(a, 0.0)).astype(BF16)
        acc = acc + _dot(a, w2_ref[cols, :])
    return y1 + m_ref[:, 5 * D_MODEL:6 * D_MODEL] * acc


def _run_tail(x, mix, m_ref, gn2_ref, weights, w1_ref, w2_ref, emit):
    first = pl.program_id(0) == 0

    @pl.when(first)
    def _():
        emit(_mlp_tail(x, mix, m_ref, gn2_ref, w1_ref, w2_ref, loading=weights))

    @pl.when(jnp.logical_not(first))
    def _():
        emit(_mlp_tail(x, mix, m_ref, gn2_ref, w1_ref, w2_ref))


def _even_out_kernel(xp_ref, xs_ref, y5_ref, u_ref, dskip_ref, wglu_ref, bglu_ref, glap_ref, glas_ref, wout_ref,
                     m_ref, gn2_ref, w1_hbm, w2_hbm, o_ref, w1_ref, w2_ref, stage1, stage2, sem, *, layer):
    weights = _MlpWeights(w1_hbm, w2_hbm, w1_ref, w2_ref, stage1, stage2, sem, layer)

    @pl.when(pl.program_id(0) == 0)
    def _():
        weights.prefetch()

    nblk = S5_WIDTH // 128
    ys = (jnp.concatenate([y5_ref[b] for b in range(nblk)], axis=1)
          + jnp.concatenate([u_ref[b] for b in range(nblk)], axis=1) * dskip_ref[...])
    gl = jax.nn.gelu(ys)
    s5o = gl * jax.nn.sigmoid(_dot(gl.astype(BF16), wglu_ref[...]) + bglu_ref[...])
    gla = _token_tile(glap_ref, glas_ref, _OUT_TM).astype(BF16)
    mix = _dot(s5o.astype(BF16), wout_ref[0:S5_WIDTH, :]) + _dot(gla, wout_ref[S5_WIDTH:, :])

    def emit(y):
        o_ref[...] = y

    _run_tail(_token_tile(xp_ref, xs_ref, _OUT_TM), mix, m_ref, gn2_ref, weights, w1_ref, w2_ref, emit)


def _odd_out_kernel(x_ref, attp_ref, atts_ref, wo_ref, m_ref, gn2_ref, w1_hbm, w2_hbm, op_ref, os_ref,
                    w1_ref, w2_ref, stage1, stage2, sem, *, layer):
    weights = _MlpWeights(w1_hbm, w2_hbm, w1_ref, w2_ref, stage1, stage2, sem, layer)

    @pl.when(pl.program_id(0) == 0)
    def _():
        weights.prefetch()

    mix = _dot(_token_tile(attp_ref, atts_ref, _OUT_TM), wo_ref[...])
    is_prompt = pl.program_id(0) < T_PROMPT // _OUT_TM

    def emit(y):
        @pl.when(is_prompt)
        def _():
            op_ref[...] = y

        @pl.when(jnp.logical_not(is_prompt))
        def _():
            os_ref[...] = y

    _run_tail(x_ref[...], mix, m_ref, gn2_ref, weights, w1_ref, w2_ref, emit)


_OUT_TM = 512


def _const_spec(shape):
    return pl.BlockSpec(shape, lambda i: (0,) * len(shape), pipeline_mode=pl.Buffered(1))


def _tail_specs(layer):
    tm = _OUT_TM
    return [
        pl.BlockSpec((None, 1, 6 * D_MODEL), lambda i: (layer * COND_ROWS + _cond_row(i, tm), 0, 0)),
        _const_spec((1, D_MODEL)),
        pl.BlockSpec(memory_space=pl.ANY),
        pl.BlockSpec(memory_space=pl.ANY),
    ]


def _tail_scratch():
    return [pltpu.VMEM((D_MODEL, D_FF), BF16), pltpu.VMEM((D_FF, D_MODEL), BF16),
            pltpu.VMEM((MLP_SLOTS, D_MODEL, MLP_LOAD), F32), pltpu.VMEM((MLP_SLOTS, MLP_LOAD, D_MODEL), F32),
            pltpu.SemaphoreType.DMA((2, MLP_SLOTS))]


_TAIL_PARAMS = dict(dimension_semantics=("arbitrary",), vmem_limit_bytes=VMEM_LIMIT_TAIL)


def _even_out_call(xp, xs, y5, u, d_skip, w_glu, b_glu, gla_p, gla_s, w_out, mods, layer, gn2, w1, w2):
    tm = _OUT_TM
    return pl.pallas_call(
        functools.partial(_even_out_kernel, layer=layer),
        out_shape=jax.ShapeDtypeStruct((T_TOK, D_MODEL), F32),
        grid=(T_TOK // tm,),
        in_specs=_token_specs(tm) + [
            pl.BlockSpec((S5_WIDTH // 128, tm, 128), lambda i: (0, i, 0)),
            pl.BlockSpec((S5_WIDTH // 128, tm, 128), lambda i: (0, i, 0)),
            _const_spec((1, S5_WIDTH)),
            _const_spec((S5_WIDTH, S5_WIDTH)),
            _const_spec((1, S5_WIDTH)),
        ] + _token_specs(tm, GLA_VW) + [
            _const_spec((S5_WIDTH + GLA_VW, D_MODEL)),
        ] + _tail_specs(layer),
        out_specs=pl.BlockSpec((tm, D_MODEL), lambda i: (i, 0)),
        scratch_shapes=_tail_scratch(),
        compiler_params=pltpu.CompilerParams(**_TAIL_PARAMS),
        name="even_out_mlp",
    )(xp, xs, y5, u, d_skip, w_glu, b_glu, gla_p, gla_s, w_out, mods, gn2, w1, w2)


def _odd_out_call(x, att_p, att_s, w_o, mods, layer, gn2, w1, w2):
    tm = _OUT_TM
    return pl.pallas_call(
        functools.partial(_odd_out_kernel, layer=layer),
        out_shape=(jax.ShapeDtypeStruct((T_PROMPT, D_MODEL), F32),
                   jax.ShapeDtypeStruct((T_SAMPLE, D_MODEL), F32)),
        grid=(T_TOK // tm,),
        in_specs=[pl.BlockSpec((tm, D_MODEL), lambda i: (i, 0))] + _token_specs(tm) + [
            _const_spec((D_MODEL, D_MODEL)),
        ] + _tail_specs(layer),
        out_specs=tuple(_token_specs(tm)),
        scratch_shapes=_tail_scratch(),
        compiler_params=pltpu.CompilerParams(**_TAIL_PARAMS),
        name="odd_out_mlp",
    )(x, att_p, att_s, w_o, mods, gn2, w1, w2)


def _qkv_kernel(x_ref, gn_ref, m_ref, w_ref, qn_ref, kn_ref, cos_ref, sin_ref,
                q_ref, kb_ref, vb_ref, k32_ref, v32_ref, *, tile):
    h = _norm_mod(x_ref[...], gn_ref[...], m_ref[:, 0:D_MODEL], m_ref[:, D_MODEL:2 * D_MODEL]).astype(BF16)
    z = _dot(h, w_ref[...])
    v = z[:, (N_HEADS + KV_HEADS) * HEAD_DIM:]
    vb_ref[...] = v.astype(BF16)
    even_lane = (lax.broadcasted_iota(jnp.int32, (1, HEAD_DIM), 1) & 1) == 0

    def heads(rope):
        for hd in range(N_HEADS + KV_HEADS):
            xh = z[:, hd * HEAD_DIM:(hd + 1) * HEAD_DIM]
            gain = qn_ref[...] if hd < N_HEADS else kn_ref[...]
            xh = xh * lax.rsqrt(jnp.mean(xh * xh, axis=-1, keepdims=True) + EPS) * gain
            if rope:
                partner = jnp.where(even_lane, pltpu.roll(xh, HEAD_DIM - 1, 1), pltpu.roll(xh, 1, 1))
                xh = xh * cos_ref[...] + partner * sin_ref[...]
            if hd < N_HEADS:
                q_ref[:, hd * HEAD_DIM:(hd + 1) * HEAD_DIM] = xh.astype(BF16)
            else:
                cols = slice((hd - N_HEADS) * HEAD_DIM, (hd - N_HEADS + 1) * HEAD_DIM)
                kb_ref[:, cols] = xh.astype(BF16)
                if not rope:
                    k32_ref[:, hd - N_HEADS, :] = xh

    is_sample = pl.program_id(0) >= T_PROMPT // tile

    @pl.when(is_sample)
    def _():
        heads(True)

    @pl.when(jnp.logical_not(is_sample))
    def _():
        heads(False)
        for kh in range(KV_HEADS):
            v32_ref[:, kh, :] = v[:, kh * HEAD_DIM:(kh + 1) * HEAD_DIM]


def _qkv_call(x, gn, mods, layer, w_qkv, q_norm, k_norm, cos_t, sin_t):
    tm = 512
    pos_tiles = DEC_SEQ // tm
    n_prompt = T_PROMPT // tm
    kvw = KV_HEADS * HEAD_DIM

    def pos_map(i):
        return (jnp.maximum(i - n_prompt, 0) % pos_tiles, 0)

    def prompt_map(i):
        return (jnp.minimum(i, n_prompt - 1), 0, 0)

    return pl.pallas_call(
        functools.partial(_qkv_kernel, tile=tm),
        out_shape=(jax.ShapeDtypeStruct((T_TOK, N_HEADS * HEAD_DIM), BF16),
                   jax.ShapeDtypeStruct((T_TOK, kvw), BF16),
                   jax.ShapeDtypeStruct((T_TOK, kvw), BF16),
                   jax.ShapeDtypeStruct((T_PROMPT, KV_HEADS, HEAD_DIM), F32),
                   jax.ShapeDtypeStruct((T_PROMPT, KV_HEADS, HEAD_DIM), F32)),
        grid=(T_TOK // tm,),
        in_specs=[
            pl.BlockSpec((tm, D_MODEL), lambda i: (i, 0)),
            pl.BlockSpec((1, D_MODEL), lambda i: (0, 0)),
            pl.BlockSpec((None, 1, 6 * D_MODEL), lambda i: (layer * COND_ROWS + _cond_row(i, tm), 0, 0)),
            pl.BlockSpec(w_qkv.shape, lambda i: (0, 0)),
            pl.BlockSpec((1, HEAD_DIM), lambda i: (0, 0)),
            pl.BlockSpec((1, HEAD_DIM), lambda i: (0, 0)),
            pl.BlockSpec((tm, HEAD_DIM), pos_map),
            pl.BlockSpec((tm, HEAD_DIM), pos_map),
        ],
        out_specs=(pl.BlockSpec((tm, N_HEADS * HEAD_DIM), lambda i: (i, 0)),
                   pl.BlockSpec((tm, kvw), lambda i: (i, 0)),
                   pl.BlockSpec((tm, kvw), lambda i: (i, 0)),
                   pl.BlockSpec((tm, KV_HEADS, HEAD_DIM), prompt_map),
                   pl.BlockSpec((tm, KV_HEADS, HEAD_DIM), prompt_map)),
        compiler_params=pltpu.CompilerParams(vmem_limit_bytes=VMEM_LIMIT),
        name="odd_qkv",
    )(x, gn, mods, w_qkv, q_norm, k_norm, cos_t, sin_t)


def _rope_tables():
    f32 = np.float32
    rows = DEC_SEQ // GRID_W
    row = np.repeat(np.arange(rows, dtype=f32), GRID_W)
    col = np.tile(np.arange(GRID_W, dtype=f32), rows)
    inv = np.power(f32(ROPE_THETA), -np.arange(0, AXIS_DIM, 2, dtype=f32) / f32(AXIS_DIM)).astype(f32)
    ang = np.concatenate([row[:, None] * inv, col[:, None] * inv], axis=-1).astype(f32)
    cos_t = np.repeat(np.cos(ang), 2, axis=-1).astype(f32)
    sin = np.sin(ang).astype(f32)
    sin_t = np.stack([-sin, sin], axis=-1).reshape(DEC_SEQ, HEAD_DIM)
    return jnp.asarray(cos_t), jnp.asarray(sin_t)


def _attn_kernel(*refs, has_cache):
    q_ref, k_ref, v_ref = refs[:3]
    ck_ref, cv_ref = refs[3:5] if has_cache else (None, None)
    o_ref = refs[-1]
    c = HEAD_DIM ** -0.5 * math.log2(math.e)
    ones_col = (lax.broadcasted_iota(jnp.int32, (1, HEAD_DIM), 1) == 0).astype(BF16)

    def with_ones(v):
        return jnp.concatenate([v, jnp.broadcast_to(ones_col, v.shape)], axis=1)

    for g in range(k_ref.shape[1] // HEAD_DIM):
        kv_cols = slice(g * HEAD_DIM, (g + 1) * HEAD_DIM)
        k = k_ref[:, kv_cols]
        v = with_ones(v_ref[:, kv_cols])
        if has_cache:
            ck = ck_ref[:, kv_cols].astype(BF16)
            cv = with_ones(cv_ref[:, kv_cols].astype(BF16))
        for r in range(Q_PER_KV):
            cs = slice((g * Q_PER_KV + r) * HEAD_DIM, (g * Q_PER_KV + r + 1) * HEAD_DIM)
            q = q_ref[:, cs]
            s = lax.dot_general(q, k, NT_DIMS, preferred_element_type=F32)
            m = jnp.max(s, axis=-1, keepdims=True)
            if has_cache:
                sc = lax.dot_general(q, ck, NT_DIMS, preferred_element_type=F32)
                m = jnp.maximum(m, jnp.max(sc, axis=-1, keepdims=True))
            mc = m * c
            o = _dot(jnp.exp2(s * c - mc).astype(BF16), v)
            if has_cache:
                o = o + _dot(jnp.exp2(sc * c - mc).astype(BF16), cv)
            o_ref[:, cs] = (o[:, 0:HEAD_DIM] / o[:, HEAD_DIM:HEAD_DIM + 1]).astype(BF16)


def _attn_call(q, k, v, cache_k, cache_v, seq_len, row0, nrows, kv_per_step):
    assert row0 % seq_len == 0 and nrows % seq_len == 0 and KV_HEADS % kv_per_step == 0
    has_cache = cache_k is not None
    b0 = row0 // seq_len
    qw = kv_per_step * Q_PER_KV * HEAD_DIM
    kw = kv_per_step * HEAD_DIM
    in_specs = [
        pl.BlockSpec((seq_len, qw), lambda b, g: (b0 + b, g)),
        pl.BlockSpec((seq_len, kw), lambda b, g: (b0 + b, g)),
        pl.BlockSpec((seq_len, kw), lambda b, g: (b0 + b, g)),
    ]
    args = [q, k, v]
    if has_cache:
        in_specs += [pl.BlockSpec((PAST_LEN, kw), lambda b, g: (b, g)),
                     pl.BlockSpec((PAST_LEN, kw), lambda b, g: (b, g))]
        args += [cache_k, cache_v]
    return pl.pallas_call(
        functools.partial(_attn_kernel, has_cache=has_cache),
        out_shape=jax.ShapeDtypeStruct((nrows, N_HEADS * HEAD_DIM), BF16),
        grid=(nrows // seq_len, KV_HEADS // kv_per_step),
        in_specs=in_specs,
        out_specs=pl.BlockSpec((seq_len, qw), lambda b, g: (b, g)),
        compiler_params=pltpu.CompilerParams(vmem_limit_bytes=VMEM_LIMIT),
        name=f"attn_len{seq_len}",
    )(*args)


def kernel(x_prompt, x_sample, state_s5_re, state_s5_im, state_gla, cache_k, cache_v, c, c_ctx, norm_mix, norm_mlp, w_ada, b_ada, w_mlp_in, w_mlp_out, w_in_e, w_out_e, s5_lambda_re, s5_lambda_im, s5_log_dt, s5_b_re, s5_b_im, s5_c_re, s5_c_im, s5_d, s5_w_glu, s5_b_glu, gla_w_gate2, gla_b_gate, gla_norm, w_qkv_o, w_o_o, q_norm, k_norm):
    xp = x_prompt.reshape(T_PROMPT, D_MODEL)
    xs = x_sample.reshape(T_SAMPLE, D_MODEL)
    cond8 = jnp.concatenate([c_ctx[None, :], c, jnp.zeros((COND_ROWS - 1 - DEC_BATCH, D_MODEL), F32)], axis=0)
    mods = _ada_call(cond8, w_ada, b_ada)
    w1_all, w2_all = w_mlp_in, w_mlp_out

    n_main = S5_WIDTH + 2 * GLA_QK + 2 * GLA_VW
    w_in = w_in_e[0]
    w_main = w_in[:, :n_main].astype(BF16)
    w_glr = jnp.pad(w_in[:, n_main:], ((0, 0), (0, 128 - 2 * GLA_RANK))).astype(BF16)
    zg = jnp.zeros((GLA_RANK, GLA_QK), F32)
    w_gate = jnp.concatenate([jnp.concatenate([gla_w_gate2[0, 0], zg], axis=1),
                              jnp.concatenate([zg, gla_w_gate2[0, 1]], axis=1),
                              jnp.zeros((128 - 2 * GLA_RANK, 2 * GLA_QK), F32)], axis=0).astype(BF16)
    b_gate = gla_b_gate[0].reshape(1, 2 * GLA_QK)
    u, z, g = _inproj_call(xp, xs, norm_mix[0:1], mods, 0, w_main, w_glr, w_gate, b_gate)

    mats = _s5_prep_call(s5_lambda_re[0], s5_lambda_im[0], s5_log_dt[0], s5_b_re[0], s5_b_im[0],
                         s5_c_re[0], s5_c_im[0])

    def state_rows(s):
        return jnp.transpose(s, (2, 0, 1, 3)).reshape(S5_GROUPS, DEC_BATCH, 2 * S5_STATE)

    h0 = jnp.concatenate([state_rows(state_s5_re[:, 0]), state_rows(state_s5_im[:, 0])], axis=-1)
    nsteps = S5_GROUPS // S5_GPB
    h0 = jnp.transpose(h0.reshape(nsteps, S5_GPB, DEC_BATCH, S5_W), (0, 2, 1, 3))
    y5, ns = _s5_call(u, mats, h0)
    ns = jnp.transpose(ns, (0, 2, 1, 3)).reshape(S5_GROUPS, BATCH, S5_W)

    def state_out(n):
        return jnp.transpose(n.reshape(S5_GROUPS, BATCH, 2, S5_STATE), (1, 2, 0, 3))[:, None]

    new_s5_re = state_out(ns[:, :, :2 * S5_STATE])
    new_s5_im = state_out(ns[:, :, 2 * S5_STATE:])

    gn_gla = gla_norm[0].reshape(1, GLA_DV)
    gla_p, sfin = _gla_call(z, g, gn_gla, None, SEQ, BATCH, 0, nsub=4)
    s0 = state_gla[:, 0].reshape(DEC_BATCH, 2, GLA_QK, GLA_DV)
    gla_s, _ = _gla_call(z, g, gn_gla, s0, DEC_SEQ, DEC_BATCH, T_PROMPT, nsub=1)
    new_gla = sfin.reshape(BATCH, 1, 2, GLA_HEADS, GLA_DK, GLA_DV)

    x = _even_out_call(xp, xs, y5, u, s5_d[0].reshape(1, S5_WIDTH), s5_w_glu[0].astype(BF16),
                       s5_b_glu[0].reshape(1, S5_WIDTH), gla_p, gla_s, w_out_e[0].astype(BF16), mods, 0,
                       norm_mlp[0:1], w1_all, w2_all)

    cos_t, sin_t = _rope_tables()
    q, k, v, k32, v32 = _qkv_call(x, norm_mix[1:2], mods, 1, w_qkv_o[0].astype(BF16),
                                  q_norm[0].reshape(1, HEAD_DIM), k_norm[0].reshape(1, HEAD_DIM), cos_t, sin_t)
    att_p = _attn_call(q, k, v, None, None, SEQ, 0, T_PROMPT, kv_per_step=1)
    ck = cache_k[:, 0].reshape(DEC_BATCH * PAST_LEN, KV_HEADS * HEAD_DIM)
    cv = cache_v[:, 0].reshape(DEC_BATCH * PAST_LEN, KV_HEADS * HEAD_DIM)
    att_s = _attn_call(q, k, v, ck, cv, DEC_SEQ, T_PROMPT, T_SAMPLE, kv_per_step=1)
    yp, ys = _odd_out_call(x, att_p, att_s, w_o_o[0].astype(BF16), mods, 1, norm_mlp[1:2],
                           w1_all, w2_all)

    new_k = k32.reshape(BATCH, 1, SEQ, KV_HEADS, HEAD_DIM)
    new_v = v32.reshape(BATCH, 1, SEQ, KV_HEADS, HEAD_DIM)
    y_prompt = yp.reshape(BATCH, SEQ, D_MODEL)
    y_sample = ys.reshape(DEC_BATCH, DEC_SEQ, D_MODEL)
    return (y_prompt, y_sample, new_s5_re, new_s5_im, new_gla, new_k, new_v)
```

```python
import functools
import math

import jax
import jax.numpy as jnp
import numpy as np
from jax import lax
from jax.experimental import pallas as pl
from jax.experimental.pallas import tpu as pltpu

F32 = jnp.float32
BF16 = jnp.bfloat16

D_MODEL = 1024
BATCH = 16
SEQ = 256
DEPTH = 2
DEC_BATCH = 4
DEC_SEQ = 1024
PAST_LEN = 512
GRID_W = 64
S5_WIDTH = 512
S5_GROUP_CH = 16
S5_GROUPS = 32
S5_STATE = 64
GLA_HEADS = 4
GLA_VW = 512
GLA_DV = 128
GLA_DK = 64
GLA_QK = 256
GLA_RANK = 16
GLA_TAU = 16.0
GLA_CHUNK = 64
GLA_CPB = 4
GLA_BLK = GLA_CPB * GLA_CHUNK
HEAD_DIM = 128
N_HEADS = 8
KV_HEADS = 2
Q_PER_KV = N_HEADS // KV_HEADS
AXIS_DIM = 64
ROPE_THETA = 10000.0
D_FF = 4096
EPS = 1e-6

T_PROMPT = BATCH * SEQ
T_SAMPLE = DEC_BATCH * DEC_SEQ
T_TOK = T_PROMPT + T_SAMPLE
COND_ROWS = 8
COND_SPAN = 1024
PROMPT_SPANS = T_PROMPT // COND_SPAN

S5_Q = 16
S5_W = S5_Q * S5_GROUP_CH
S5_GPB = 128 // S5_GROUP_CH
S5_ROWS = T_TOK // S5_Q
S5_PROMPT_ROWS = T_PROMPT // S5_Q
S5_PROMPT_CHUNKS = SEQ // S5_Q
S5_SAMPLE_CHUNKS = DEC_SEQ // S5_Q

VMEM_LIMIT = 56 * 1024 * 1024
VMEM_LIMIT_TAIL = 60 * 1024 * 1024

NT_DIMS = (((1,), (1,)), ((), ()))
TN_DIMS = (((0,), (0,)), ((), ()))


def _cond_row(i, tile):
    return jnp.maximum((i * tile) // COND_SPAN - (PROMPT_SPANS - 1), 0)


def _norm_mod(x, gain, shift, scale):
    y = x * lax.rsqrt(jnp.mean(x * x, axis=-1, keepdims=True) + EPS)
    return (y * gain) * (1.0 + scale) + shift


def _dot(a, b):
    return jnp.dot(a, b, preferred_element_type=F32)


def _ada_kernel(cond_ref, w_ref, b_ref, o_ref):
    s = jax.nn.silu(cond_ref[...]).astype(BF16)
    o_ref[:, 0, :] = _dot(s, w_ref[...].astype(BF16)) + b_ref[...]


ADA_SHAPE = jax.ShapeDtypeStruct((COND_ROWS, 1, 6 * D_MODEL), F32)


def _ada_specs(layer, tn):
    return ([pl.BlockSpec((COND_ROWS, D_MODEL), lambda *ids: (0, 0)),
             pl.BlockSpec((None, D_MODEL, tn), lambda *ids: (layer, 0, ids[-1])),
             pl.BlockSpec((None, 1, tn), lambda *ids: (layer, 0, ids[-1]))],
            pl.BlockSpec((COND_ROWS, 1, tn), lambda *ids: (0, 0, ids[-1])))


def _ada_call(cond8, w_ada, b_ada, layer):
    tn = 2048
    in_specs, out_spec = _ada_specs(layer, tn)
    return pl.pallas_call(
        _ada_kernel,
        out_shape=ADA_SHAPE,
        grid=(6 * D_MODEL // tn,),
        in_specs=in_specs,
        out_specs=out_spec,
        compiler_params=pltpu.CompilerParams(vmem_limit_bytes=VMEM_LIMIT),
        name="ada_mod",
    )(cond8, w_ada, b_ada)


def _token_specs(tile, width=D_MODEL):
    n_prompt = T_PROMPT // tile
    return [pl.BlockSpec((tile, width), lambda i: (jnp.minimum(i, n_prompt - 1), 0)),
            pl.BlockSpec((tile, width), lambda i: (jnp.maximum(i - n_prompt, 0), 0))]


def _token_tile(xp_ref, xs_ref, tile):
    return jnp.where(pl.program_id(0) < T_PROMPT // tile, xp_ref[...], xs_ref[...])


def _inproj_kernel(xp_ref, xs_ref, gn_ref, m_ref, w_ref, wglr_ref, wg_ref, bg_ref, u_ref, z_ref, g_ref, *, tile):
    x = _token_tile(xp_ref, xs_ref, tile)
    h = _norm_mod(x, gn_ref[...], m_ref[:, 0:D_MODEL], m_ref[:, D_MODEL:2 * D_MODEL]).astype(BF16)
    z = _dot(h, w_ref[...])
    for blk in range(S5_WIDTH // 128):
        u_ref[blk] = z[:, blk * 128:(blk + 1) * 128]
    z_ref[...] = z[:, S5_WIDTH:]
    glr = _dot(h, wglr_ref[...]).astype(BF16)
    pre = _dot(glr, wg_ref[...]) + bg_ref[...]
    g_ref[...] = jax.nn.log_sigmoid(pre) * (1.0 / GLA_TAU)


def _inproj_call(xp, xs, gn, mods, layer, w_main, w_glr, w_gate, b_gate):
    tm = 512
    nz = w_main.shape[1]
    return pl.pallas_call(
        functools.partial(_inproj_kernel, tile=tm),
        out_shape=(jax.ShapeDtypeStruct((S5_WIDTH // 128, T_TOK, 128), F32),
                   jax.ShapeDtypeStruct((T_TOK, nz - S5_WIDTH), F32),
                   jax.ShapeDtypeStruct((T_TOK, 2 * GLA_QK), F32)),
        grid=(T_TOK // tm,),
        in_specs=_token_specs(tm) + [
            pl.BlockSpec((1, D_MODEL), lambda i: (0, 0)),
            pl.BlockSpec((None, 1, 6 * D_MODEL), lambda i: (_cond_row(i, tm), 0, 0)),
            pl.BlockSpec((D_MODEL, nz), lambda i: (0, 0)),
            pl.BlockSpec((D_MODEL, 128), lambda i: (0, 0)),
            pl.BlockSpec((128, 2 * GLA_QK), lambda i: (0, 0)),
            pl.BlockSpec((1, 2 * GLA_QK), lambda i: (0, 0)),
        ],
        out_specs=(pl.BlockSpec((S5_WIDTH // 128, tm, 128), lambda i: (0, i, 0)),
                   pl.BlockSpec((tm, nz - S5_WIDTH), lambda i: (i, 0)),
                   pl.BlockSpec((tm, 2 * GLA_QK), lambda i: (i, 0))),
        compiler_params=pltpu.CompilerParams(vmem_limit_bytes=VMEM_LIMIT),
        name="even_inproj",
    )(xp, xs, gn, mods, w_main, w_glr, w_gate, b_gate)


S5_PREP_GPB = 8
_PREP_LRE, _PREP_LIM, _PREP_LDT = 0, 1, 2
_PREP_BT_RE, _PREP_BT_IM, _PREP_C_RE, _PREP_C_IM, _PREP_ROWS = 8, 24, 40, 56, 72


def _s5_prep_kernel(p_ref, cc_ref, t_ref, bq_ref, cqt_ref, be_ref, a_ref, t_scr, dd_scr):
    for gi in range(S5_PREP_GPB):
        _s5_prep_group(p_ref.at[gi], cc_ref.at[gi], t_ref.at[gi], bq_ref.at[gi], cqt_ref.at[gi], be_ref.at[gi],
                       a_ref.at[gi], t_scr, dd_scr)


def _s5_prep_group(p_ref, cc_ref, t_ref, bq_ref, cqt_ref, be_ref, a_ref, t_scr, dd_scr):
    gch = S5_GROUP_CH
    lre = p_ref[_PREP_LRE:_PREP_LRE + 1]
    lim = p_ref[_PREP_LIM:_PREP_LIM + 1]
    dt = jnp.exp(p_ref[_PREP_LDT:_PREP_LDT + 1])
    a = lre * dt
    th = lim * dt

    def lam_pow(k):
        mag = jnp.exp(k * a)
        return mag * jnp.cos(k * th), mag * jnp.sin(k * th)

    lb_re, lb_im = lam_pow(1.0)
    nr = lb_re - 1.0
    den = lre * lre + lim * lim
    cf_re = (nr * lre + lb_im * lim) / den
    cf_im = (lb_im * lre - nr * lim) / den
    bt_re = p_ref[_PREP_BT_RE:_PREP_BT_RE + gch]
    bt_im = p_ref[_PREP_BT_IM:_PREP_BT_IM + gch]
    bb_re = jnp.tile(cf_re * bt_re - cf_im * bt_im, (S5_Q, 1))
    bb_im = jnp.tile(cf_re * bt_im + cf_im * bt_re, (S5_Q, 1))

    shape = (S5_W, 128)
    pos = lax.shift_right_logical(lax.broadcasted_iota(jnp.int32, shape, 0), 4)
    is_f = lax.broadcasted_iota(jnp.int32, shape, 1) < S5_STATE
    posq = lax.broadcasted_iota(jnp.int32, (S5_Q, 128), 0).astype(F32)
    is_fq = lax.broadcasted_iota(jnp.int32, (S5_Q, 128), 1) < S5_STATE

    def per_channel(tbl):
        return jnp.broadcast_to(tbl[:, None, :], (S5_Q, S5_GROUP_CH, 128)).reshape(shape)

    p_re, p_im = map(per_channel, lam_pow(jnp.where(is_fq, (S5_Q - 1.0) - posq, posq)))
    w_re = p_re * bb_re - p_im * bb_im
    w_im = p_re * bb_im + p_im * bb_re
    bq = jnp.concatenate([w_re, w_im], axis=1)
    bqt = jnp.transpose(bq)
    bq_ref[...] = bqt.astype(BF16)

    edge = pos == jnp.where(is_f, 0, S5_Q - 1)
    be = jnp.concatenate([jnp.where(edge, bb_re, 0.0), jnp.where(edge, bb_im, 0.0)], axis=1)
    be_ref[...] = jnp.transpose(be).astype(BF16)

    q_re, q_im = map(per_channel, lam_pow(jnp.where(is_fq, posq + 1.0, S5_Q - posq)))
    ct_re = jnp.tile(p_ref[_PREP_C_RE:_PREP_C_RE + gch], (S5_Q, 1))
    ct_im = jnp.tile(p_ref[_PREP_C_IM:_PREP_C_IM + gch], (S5_Q, 1))
    g_re = q_re * ct_re - q_im * ct_im
    g_im = q_re * ct_im + q_im * ct_re
    cqt_ref[...] = jnp.concatenate([g_re, -g_im], axis=1).astype(BF16)

    a_re, a_im = lam_pow(float(S5_Q))
    a_ref[...] = jnp.concatenate([a_re, a_im], axis=1)

    kf = jnp.dot(cc_ref[0:gch], bqt, precision=lax.Precision.HIGHEST, preferred_element_type=F32)
    kb = jnp.dot(cc_ref[gch:2 * gch], bqt, precision=lax.Precision.HIGHEST, preferred_element_type=F32)
    lo = S5_W - gch
    dd_scr[:, 0:S5_W] = kf
    dd_scr[:, lo:lo + S5_W] = kb
    dd_scr[:, lo:S5_W] = kf[:, lo:S5_W] + kb[:, 0:gch]
    for t in range(S5_Q):
        c0 = (S5_Q - 1 - t) * gch
        t_scr[t * gch:(t + 1) * gch, :] = dd_scr[:, c0:c0 + S5_W]
    t_ref[...] = t_scr[...].astype(BF16)


def _s5_prep_call(lam_re, lam_im, log_dt, b_re, b_im, c_re, c_im):
    def fb(p):
        return jnp.transpose(p, (1, 0, 2)).reshape(S5_GROUPS, 1, 2 * S5_STATE)

    def dup(p):
        return jnp.concatenate([p, p], axis=-1)

    ldt = fb(jnp.broadcast_to(log_dt[:, :, None], (2, S5_GROUPS, S5_STATE)))
    pad = jnp.zeros((S5_GROUPS, _PREP_BT_RE - _PREP_LDT - 1, 128), F32)
    packed = jnp.concatenate([fb(lam_re), fb(lam_im), ldt, pad,
                              dup(jnp.transpose(b_re, (0, 2, 1))), dup(jnp.transpose(b_im, (0, 2, 1))),
                              dup(c_re), dup(c_im)], axis=1)
    zero = jnp.zeros_like(c_re)
    cc = jnp.concatenate([jnp.concatenate([c_re, zero, -c_im, zero], axis=-1),
                          jnp.concatenate([zero, c_re, zero, -c_im], axis=-1)], axis=1)

    gpb = S5_PREP_GPB
    sq = pl.BlockSpec((gpb, S5_W, S5_W), lambda g: (g, 0, 0))
    sq_shape = jax.ShapeDtypeStruct((S5_GROUPS, S5_W, S5_W), BF16)
    return pl.pallas_call(
        _s5_prep_kernel,
        out_shape=(sq_shape, sq_shape, sq_shape, sq_shape,
                   jax.ShapeDtypeStruct((S5_GROUPS, 1, S5_W), F32)),
        grid=(S5_GROUPS // gpb,),
        in_specs=[pl.BlockSpec((gpb, _PREP_ROWS, 128), lambda g: (g, 0, 0)),
                  pl.BlockSpec((gpb, 2 * S5_GROUP_CH, S5_W), lambda g: (g, 0, 0))],
        out_specs=(sq, sq, sq, sq, pl.BlockSpec((gpb, 1, S5_W), lambda g: (g, 0, 0))),
        scratch_shapes=[pltpu.VMEM((S5_W, S5_W), F32), pltpu.VMEM((S5_GROUP_CH, 2 * S5_W), F32)],
        name="s5_prep",
    )(packed, cc)


def _s5_kernel(u_ref, tt_ref, bqt_ref, cqt_ref, bet_ref, a_ref, h0_ref, y_ref, ns_ref,
               ut_scr, x_scr, spf_scr, spb_scr, ne_scr, yt_scr, xt_scr):
    gch = S5_GROUP_CH
    for s in range(S5_Q):
        rows = u_ref[pl.ds(s, S5_ROWS, stride=S5_Q), :]
        rows_t = jnp.transpose(rows).astype(BF16)
        for gl in range(S5_GPB):
            ut_scr[gl, s * gch:(s + 1) * gch, :] = rows_t[gl * gch:(gl + 1) * gch, :]

    for gl in range(S5_GPB):
        ut = ut_scr[gl]
        xt_scr[...] = _dot(bqt_ref[gl], ut)
        x = jnp.transpose(xt_scr[...])
        xt_scr[:, 0:S5_PROMPT_ROWS] = _dot(bet_ref[gl], ut[:, 0:S5_PROMPT_ROWS])
        ne = jnp.transpose(xt_scr[:, 0:S5_PROMPT_ROWS])
        for part in range(2):
            x_scr[part, pl.ds(gl, S5_ROWS, stride=S5_GPB), :] = x[:, part * 128:(part + 1) * 128]
            ne_scr[part, pl.ds(gl, S5_PROMPT_ROWS, stride=S5_GPB), :] = ne[:, part * 128:(part + 1) * 128]

    is_f = lax.broadcasted_iota(jnp.int32, (1, 128), 1) < S5_STATE
    a_re = a_ref[:, 0:128]
    a_im = a_ref[:, 128:256]

    def tile(row):
        return pl.ds(pl.multiple_of(row * S5_GPB, S5_GPB), S5_GPB)

    def scan(base, nseq, nchunk, s_init):
        def body(i, state):
            new = []
            for b in range(nseq):
                s_re, s_im = state[b]
                rows_f = tile(base + b * nchunk + i)
                rows_b = tile(base + b * nchunk + (nchunk - 1 - i))
                spf_scr[0, rows_f, :] = s_re
                spf_scr[1, rows_f, :] = s_im
                spb_scr[0, rows_b, :] = s_re
                spb_scr[1, rows_b, :] = s_im
                x_re = jnp.where(is_f, x_scr[0, rows_f, :], x_scr[0, rows_b, :])
                x_im = jnp.where(is_f, x_scr[1, rows_f, :], x_scr[1, rows_b, :])
                new.append((a_re * s_re - a_im * s_im + x_re, a_re * s_im + a_im * s_re + x_im))
            return tuple(new)

        lax.fori_loop(0, nchunk, body, tuple(s_init))

    zero = jnp.zeros((S5_GPB, 128), F32)
    scan(0, BATCH, S5_PROMPT_CHUNKS, [(zero, zero)] * BATCH)
    scan(S5_PROMPT_ROWS, DEC_BATCH, S5_SAMPLE_CHUNKS,
         [(h0_ref[b, :, 0:128], h0_ref[b, :, 128:256]) for b in range(DEC_BATCH)])

    for b in range(BATCH):
        first = pl.ds(b * S5_PROMPT_CHUNKS * S5_GPB, S5_GPB)
        last = pl.ds(((b + 1) * S5_PROMPT_CHUNKS - 1) * S5_GPB, S5_GPB)
        for part in range(2):
            ns_ref[b, :, part * 128:(part + 1) * 128] = jnp.where(is_f, ne_scr[part, first, :], ne_scr[part, last, :])

    for gl in range(S5_GPB):
        rows = pl.ds(gl, S5_ROWS, stride=S5_GPB)
        carried = jnp.concatenate([jnp.where(is_f, spf_scr[p, rows, :], spb_scr[p, rows, :]) for p in range(2)],
                                  axis=1).astype(BF16)
        yt = _dot(tt_ref[gl], ut_scr[gl]) + lax.dot_general(cqt_ref[gl], carried, NT_DIMS,
                                                            preferred_element_type=F32)
        for t in range(S5_Q):
            yt_scr[t, gl * gch:(gl + 1) * gch, :] = yt[t * gch:(t + 1) * gch, :]
    for t in range(S5_Q):
        y_ref[pl.ds(t, S5_ROWS, stride=S5_Q), :] = jnp.transpose(yt_scr[t])


def _s5_call(u, mats, h0):
    tt_m, bqt_m, cqt_m, bet_m, a_m = mats
    nsteps = S5_GROUPS // S5_GPB
    sq = pl.BlockSpec((S5_GPB, S5_W, S5_W), lambda g: (g, 0, 0))
    state_scr = pltpu.VMEM((2, S5_ROWS * S5_GPB, 128), F32)
    return pl.pallas_call(
        _s5_kernel,
        out_shape=(jax.ShapeDtypeStruct((nsteps, T_TOK, 128), F32),
                   jax.ShapeDtypeStruct((nsteps, BATCH, S5_GPB, S5_W), F32)),
        grid=(nsteps,),
        in_specs=[
            pl.BlockSpec((None, T_TOK, 128), lambda g: (g, 0, 0)),
            sq, sq, sq, sq,
            pl.BlockSpec((S5_GPB, S5_W), lambda g: (g, 0)),
            pl.BlockSpec((None, DEC_BATCH, S5_GPB, S5_W), lambda g: (g, 0, 0, 0)),
        ],
        out_specs=(pl.BlockSpec((None, T_TOK, 128), lambda g: (g, 0, 0)),
                   pl.BlockSpec((None, BATCH, S5_GPB, S5_W), lambda g: (g, 0, 0, 0))),
        scratch_shapes=[pltpu.VMEM((S5_GPB, S5_W, S5_ROWS), BF16), state_scr, state_scr, state_scr,
                        pltpu.VMEM((2, S5_PROMPT_ROWS * S5_GPB, 128), F32),
                        pltpu.VMEM((S5_Q, 128, S5_ROWS), F32), pltpu.VMEM((S5_W, S5_ROWS), F32)],
        compiler_params=pltpu.CompilerParams(vmem_limit_bytes=VMEM_LIMIT),
        name="s5_scan",
    )(u, tt_m, bqt_m, cqt_m, bet_m, a_m.reshape(S5_GROUPS, S5_W), h0)


def _split_bf16(x):
    hi = x.astype(BF16)
    r1 = x - hi.astype(F32)
    mid = r1.astype(BF16)
    lo = (r1 - mid.astype(F32)).astype(BF16)
    return hi, mid, lo


def _gla_kernel(*refs, seq_len, nsub, has_s0, has_ada):
    rows_refs, gn_ref = refs[:6], refs[6]
    n_in = 7 + has_s0 + 3 * has_ada
    s0_ref = refs[7] if has_s0 else None
    o_ref, sfin_ref = refs[n_in:n_in + 2]
    scratch = refs[n_in + 2 + has_ada:]
    if has_ada:
        _ada_kernel(*refs[7 + has_s0:n_in], refs[n_in + 2])
    for j in range(nsub):
        rows = pl.ds(j * seq_len, seq_len)
        _gla_sequence(*[r.at[rows, :] for r in rows_refs], gn_ref, s0_ref.at[j] if has_s0 else None,
                      o_ref.at[rows, :], sfin_ref.at[j], *[s.at[j] for s in scratch], seq_len=seq_len)


def _gla_sequence(q_ref, k_ref, v_ref, gf_ref, gb_ref, r_ref, gn_ref, s0_ref, o_ref, sfin_ref,
                  oi_scr, qd_scr, kv_scr, dec_scr, ss_scr, *, seq_len):
    has_s0 = s0_ref is not None
    nblk = seq_len // GLA_BLK
    nchunk = seq_len // GLA_CHUNK
    cl = GLA_CHUNK
    ti = lax.broadcasted_iota(jnp.int32, (GLA_BLK, GLA_BLK), 0)
    si = lax.broadcasted_iota(jnp.int32, (GLA_BLK, GLA_BLK), 1)
    same = lax.shift_right_logical(ti, 6) == lax.shift_right_logical(si, 6)
    keep = (same & (ti >= si), same & (ti <= si))
    tri = tuple(kp.astype(BF16) for kp in keep)
    lane_head = lax.shift_right_logical(lax.broadcasted_iota(jnp.int32, (cl, GLA_QK), 1), 6)
    zeros_v = jnp.zeros((cl, GLA_DV), BF16)
    heads = [(slice(h * GLA_DK, (h + 1) * GLA_DK), slice(h * GLA_DV, (h + 1) * GLA_DV)) for h in range(GLA_HEADS)]

    for j in range(nblk):
        rows = slice(j * GLA_BLK, (j + 1) * GLA_BLK)
        q = q_ref[rows, :] * (GLA_DK ** -0.5)
        k = k_ref[rows, :]
        v = v_ref[rows, :].astype(BF16)
        qd, kd, k2t = [], [], []
        for d, g_ref in enumerate((gf_ref, gb_ref)):
            b = sum(_dot(tri[d], part) for part in _split_bf16(g_ref[rows, :]))
            last = cl - 1 if d == 0 else 0
            b_last = [b[c * cl + last:c * cl + last + 1] for c in range(GLA_CPB)]
            bl = jnp.concatenate([jnp.broadcast_to(x, (cl, GLA_QK)) for x in b_last], axis=0)
            qd_d = (q * jnp.exp(b)).astype(BF16)
            qd_scr[d, rows, :] = qd_d
            qd.append(qd_d)
            kd.append((k * jnp.exp(-b)).astype(BF16))
            k2t.append(jnp.transpose(k * jnp.exp(bl - b)).astype(BF16))
            for c in range(GLA_CPB):
                dec_scr[d, j * GLA_CPB + c] = jnp.exp(jnp.transpose(jnp.broadcast_to(b_last[c], (GLA_DV, GLA_QK))))
        for h, (ks, vs) in enumerate(heads):
            att = [jnp.where(keep[d], lax.dot_general(qd[d][:, ks], kd[d][:, ks], NT_DIMS,
                                                      preferred_element_type=F32), 0.0) for d in range(2)]
            oi_scr[rows, vs] = _dot((att[0] + att[1]).astype(BF16), v[:, vs])
            vh = v[:, vs]
            vexp = jnp.concatenate(
                [jnp.concatenate([vh[c * cl:(c + 1) * cl] if c2 == c else zeros_v for c2 in range(GLA_CPB)], axis=1)
                 for c in range(GLA_CPB)], axis=0)
            for d in range(2):
                kv_scr[d, j, h] = _dot(k2t[d][ks, :], vexp)

    for d in range(2):
        s = s0_ref[d] if has_s0 else jnp.zeros((GLA_QK, GLA_DV), F32)
        for cg in (range(nchunk) if d == 0 else range(nchunk - 1, -1, -1)):
            j, c = divmod(cg, GLA_CPB)
            ss_scr[d, cg] = s.astype(BF16)
            kv = jnp.concatenate([kv_scr[d, j, h, :, c * GLA_DV:(c + 1) * GLA_DV] for h in range(GLA_HEADS)], axis=0)
            s = s * dec_scr[d, cg] + kv
        sfin_ref[d] = s

    for cg in range(nchunk):
        rows = slice(cg * cl, (cg + 1) * cl)
        inter = []
        for d in range(2):
            qc = qd_scr[d, rows, :]
            qstack = jnp.concatenate([jnp.where(lane_head == h, qc, jnp.zeros_like(qc)) for h in range(GLA_HEADS)],
                                     axis=0)
            inter.append(_dot(qstack, ss_scr[d, cg]))
        gate = jax.nn.silu(r_ref[rows, :])
        for h, (ks, vs) in enumerate(heads):
            hr = slice(h * cl, (h + 1) * cl)
            oh = oi_scr[rows, vs] + inter[0][hr] + inter[1][hr]
            oh = oh * lax.rsqrt(jnp.mean(oh * oh, axis=-1, keepdims=True) + EPS) * gn_ref[...]
            o_ref[rows, vs] = oh * gate[:, vs]


def _gla_call(z, g, gla_norm, s0, seq_len, nseq, row0, nsub, ada=None):
    blk = nsub * seq_len
    assert row0 % blk == 0 and nseq % nsub == 0
    r0 = row0 // blk
    has_s0 = s0 is not None
    qk_off = 0
    v_off = 2 * GLA_QK // GLA_VW
    in_specs = [
        pl.BlockSpec((blk, GLA_QK), lambda i: (r0 + i, qk_off)),
        pl.BlockSpec((blk, GLA_QK), lambda i: (r0 + i, qk_off + 1)),
        pl.BlockSpec((blk, GLA_VW), lambda i: (r0 + i, v_off)),
        pl.BlockSpec((blk, GLA_QK), lambda i: (r0 + i, 0)),
        pl.BlockSpec((blk, GLA_QK), lambda i: (r0 + i, 1)),
        pl.BlockSpec((blk, GLA_VW), lambda i: (r0 + i, v_off + 1)),
        pl.BlockSpec((1, GLA_DV), lambda i: (0, 0)),
    ]
    args = [z, z, z, g, g, z, gla_norm]
    state_spec = pl.BlockSpec((nsub, 2, GLA_QK, GLA_DV), lambda i: (i, 0, 0, 0))
    if has_s0:
        in_specs.append(state_spec)
        args.append(s0)
    out_shape = [jax.ShapeDtypeStruct((nseq * seq_len, GLA_VW), F32),
                 jax.ShapeDtypeStruct((nseq, 2, GLA_QK, GLA_DV), F32)]
    out_specs = [pl.BlockSpec((blk, GLA_VW), lambda i: (i, 0)), state_spec]
    nsteps = nseq // nsub
    if ada is not None:
        cond8, w_ada, b_ada, ada_layer = ada
        ada_in, ada_out = _ada_specs(ada_layer, 6 * D_MODEL // nsteps)
        in_specs += ada_in
        args += [cond8, w_ada, b_ada]
        out_shape.append(ADA_SHAPE)
        out_specs.append(ada_out)
    return pl.pallas_call(
        functools.partial(_gla_kernel, seq_len=seq_len, nsub=nsub, has_s0=has_s0, has_ada=ada is not None),
        out_shape=tuple(out_shape),
        grid=(nsteps,),
        in_specs=in_specs,
        out_specs=tuple(out_specs),
        scratch_shapes=[
            pltpu.VMEM((nsub, seq_len, GLA_VW), F32),
            pltpu.VMEM((nsub, 2, seq_len, GLA_QK), BF16),
            pltpu.VMEM((nsub, 2, seq_len // GLA_BLK, GLA_HEADS, GLA_DK, GLA_CPB * GLA_DV), F32),
            pltpu.VMEM((nsub, 2, seq_len // GLA_CHUNK, GLA_QK, GLA_DV), F32),
            pltpu.VMEM((nsub, 2, seq_len // GLA_CHUNK, GLA_QK, GLA_DV), BF16),
        ],
        compiler_params=pltpu.CompilerParams(vmem_limit_bytes=VMEM_LIMIT),
        name=f"gla_len{seq_len}",
    )(*args)


MLP_CHUNK = 512
MLP_LOAD = 256
MLP_SLOTS = 4


class _MlpWeights:
    def __init__(self, w1_hbm, w2_hbm, w1_scr, w2_scr, stage1, stage2, sem, layer):
        self.refs = (w1_hbm, w2_hbm, w1_scr, w2_scr, stage1, stage2, sem)
        self.layer = layer

    def _copies(self, p):
        w1_hbm, w2_hbm, _, _, stage1, stage2, sem = self.refs
        cols = pl.ds(p * MLP_LOAD, MLP_LOAD)
        slot = p % MLP_SLOTS
        return (pltpu.make_async_copy(w1_hbm.at[self.layer, :, cols], stage1.at[slot], sem.at[0, slot]),
                pltpu.make_async_copy(w2_hbm.at[self.layer, cols, :], stage2.at[slot], sem.at[1, slot]))

    def start(self, p):
        for cp in self._copies(p):
            cp.start()

    def prefetch(self):
        for p in range(MLP_SLOTS - 1):
            self.start(p)

    def finish(self, p):
        _, _, w1_scr, w2_scr, stage1, stage2, _ = self.refs
        ahead = p + MLP_SLOTS - 1
        if ahead < D_FF // MLP_LOAD:
            self.start(ahead)
        for cp in self._copies(p):
            cp.wait()
        cols = slice(p * MLP_LOAD, (p + 1) * MLP_LOAD)
        w1_scr[:, cols] = stage1[p % MLP_SLOTS].astype(BF16)
        w2_scr[cols, :] = stage2[p % MLP_SLOTS].astype(BF16)


def _mlp_tail(x, mix, m_ref, gn2_ref, w1_ref, w2_ref, loading=None):
    y1 = x + m_ref[:, 2 * D_MODEL:3 * D_MODEL] * mix
    h = _norm_mod(y1, gn2_ref[...], m_ref[:, 3 * D_MODEL:4 * D_MODEL], m_ref[:, 4 * D_MODEL:5 * D_MODEL]).astype(BF16)
    nchunk = D_FF // MLP_CHUNK
    acc = jnp.zeros(y1.shape, F32)
    for c in range(nchunk):
        cols = slice(c * MLP_CHUNK, (c + 1) * MLP_CHUNK)
        if loading is not None:
            per = MLP_CHUNK // MLP_LOAD
            for p in range(c * per, (c + 1) * per):
                loading.finish(p)
        a = _dot(h, w1_ref[:, cols])
        a = jnp.square(jnp.maximum(a, 0.0)).astype(BF16)
        acc = acc + _dot(a, w2_ref[cols, :])
    return y1 + m_ref[:, 5 * D_MODEL:6 * D_MODEL] * acc


def _run_tail(x, mix, m_ref, gn2_ref, weights, w1_ref, w2_ref, emit):
    first = pl.program_id(0) == 0

    @pl.when(first)
    def _():
        emit(_mlp_tail(x, mix, m_ref, gn2_ref, w1_ref, w2_ref, loading=weights))

    @pl.when(jnp.logical_not(first))
    def _():
        emit(_mlp_tail(x, mix, m_ref, gn2_ref, w1_ref, w2_ref))


def _even_out_kernel(xp_ref, xs_ref, y5_ref, u_ref, dskip_ref, wglu_ref, bglu_ref, glap_ref, glas_ref, wout_ref,
                     m_ref, gn2_ref, w1_hbm, w2_hbm, o_ref, w1_ref, w2_ref, stage1, stage2, sem, *, layer):
    weights = _MlpWeights(w1_hbm, w2_hbm, w1_ref, w2_ref, stage1, stage2, sem, layer)

    @pl.when(pl.program_id(0) == 0)
    def _():
        weights.prefetch()

    nblk = S5_WIDTH // 128
    ys = (jnp.concatenate([y5_ref[b] for b in range(nblk)], axis=1)
          + jnp.concatenate([u_ref[b] for b in range(nblk)], axis=1) * dskip_ref[...])
    gl = jax.nn.gelu(ys)
    s5o = gl * jax.nn.sigmoid(_dot(gl.astype(BF16), wglu_ref[...]) + bglu_ref[...])
    gla = _token_tile(glap_ref, glas_ref, _OUT_TM).astype(BF16)
    mix = _dot(s5o.astype(BF16), wout_ref[0:S5_WIDTH, :]) + _dot(gla, wout_ref[S5_WIDTH:, :])

    def emit(y):
        o_ref[...] = y

    _run_tail(_token_tile(xp_ref, xs_ref, _OUT_TM), mix, m_ref, gn2_ref, weights, w1_ref, w2_ref, emit)


def _odd_out_kernel(x_ref, attp_ref, atts_ref, wo_ref, m_ref, gn2_ref, w1_hbm, w2_hbm, op_ref, os_ref,
                    w1_ref, w2_ref, stage1, stage2, sem, *, layer):
    weights = _MlpWeights(w1_hbm, w2_hbm, w1_ref, w2_ref, stage1, stage2, sem, layer)

    @pl.when(pl.program_id(0) == 0)
    def _():
        weights.prefetch()

    mix = _dot(_token_tile(attp_ref, atts_ref, _OUT_TM), wo_ref[...])
    is_prompt = pl.program_id(0) < T_PROMPT // _OUT_TM

    def emit(y):
        @pl.when(is_prompt)
        def _():
            op_ref[...] = y

        @pl.when(jnp.logical_not(is_prompt))
        def _():
            os_ref[...] = y

    _run_tail(x_ref[...], mix, m_ref, gn2_ref, weights, w1_ref, w2_ref, emit)


_OUT_TM = 512


def _const_spec(shape):
    return pl.BlockSpec(shape, lambda i: (0,) * len(shape), pipeline_mode=pl.Buffered(1))


def _tail_specs(layer):
    tm = _OUT_TM
    return [
        pl.BlockSpec((None, 1, 6 * D_MODEL), lambda i: (_cond_row(i, tm), 0, 0)),
        _const_spec((1, D_MODEL)),
        pl.BlockSpec(memory_space=pl.ANY),
        pl.BlockSpec(memory_space=pl.ANY),
    ]


def _tail_scratch():
    return [pltpu.VMEM((D_MODEL, D_FF), BF16), pltpu.VMEM((D_FF, D_MODEL), BF16),
            pltpu.VMEM((MLP_SLOTS, D_MODEL, MLP_LOAD), F32), pltpu.VMEM((MLP_SLOTS, MLP_LOAD, D_MODEL), F32),
            pltpu.SemaphoreType.DMA((2, MLP_SLOTS))]


_TAIL_PARAMS = dict(dimension_semantics=("arbitrary",), vmem_limit_bytes=VMEM_LIMIT_TAIL)


def _even_out_call(xp, xs, y5, u, d_skip, w_glu, b_glu, gla_p, gla_s, w_out, mods, layer, gn2, w1, w2):
    tm = _OUT_TM
    return pl.pallas_call(
        functools.partial(_even_out_kernel, layer=layer),
        out_shape=jax.ShapeDtypeStruct((T_TOK, D_MODEL), F32),
        grid=(T_TOK // tm,),
        in_specs=_token_specs(tm) + [
            pl.BlockSpec((S5_WIDTH // 128, tm, 128), lambda i: (0, i, 0)),
            pl.BlockSpec((S5_WIDTH // 128, tm, 128), lambda i: (0, i, 0)),
            _const_spec((1, S5_WIDTH)),
            _const_spec((S5_WIDTH, S5_WIDTH)),
            _const_spec((1, S5_WIDTH)),
        ] + _token_specs(tm, GLA_VW) + [
            _const_spec((S5_WIDTH + GLA_VW, D_MODEL)),
        ] + _tail_specs(layer),
        out_specs=pl.BlockSpec((tm, D_MODEL), lambda i: (i, 0)),
        scratch_shapes=_tail_scratch(),
        compiler_params=pltpu.CompilerParams(**_TAIL_PARAMS),
        name="even_out_mlp",
    )(xp, xs, y5, u, d_skip, w_glu, b_glu, gla_p, gla_s, w_out, mods, gn2, w1, w2)


def _odd_out_call(x, att_p, att_s, w_o, mods, layer, gn2, w1, w2):
    tm = _OUT_TM
    return pl.pallas_call(
        functools.partial(_odd_out_kernel, layer=layer),
        out_shape=(jax.ShapeDtypeStruct((T_PROMPT, D_MODEL), F32),
                   jax.ShapeDtypeStruct((T_SAMPLE, D_MODEL), F32)),
        grid=(T_TOK // tm,),
        in_specs=[pl.BlockSpec((tm, D_MODEL), lambda i: (i, 0))] + _token_specs(tm) + [
            _const_spec((D_MODEL, D_MODEL)),
        ] + _tail_specs(layer),
        out_specs=tuple(_token_specs(tm)),
        scratch_shapes=_tail_scratch(),
        compiler_params=pltpu.CompilerParams(**_TAIL_PARAMS),
        name="odd_out_mlp",
    )(x, att_p, att_s, w_o, mods, gn2, w1, w2)


def _qkv_kernel(x_ref, gn_ref, m_ref, w_ref, qn_ref, kn_ref, cos_ref, sin_ref,
                q_ref, kb_ref, vb_ref, k32_ref, v32_ref, *, tile):
    h = _norm_mod(x_ref[...], gn_ref[...], m_ref[:, 0:D_MODEL], m_ref[:, D_MODEL:2 * D_MODEL]).astype(BF16)
    z = _dot(h, w_ref[...])
    v = z[:, (N_HEADS + KV_HEADS) * HEAD_DIM:]
    vb_ref[...] = v.astype(BF16)
    even_lane = (lax.broadcasted_iota(jnp.int32, (1, HEAD_DIM), 1) & 1) == 0

    def heads(rope):
        for hd in range(N_HEADS + KV_HEADS):
            xh = z[:, hd * HEAD_DIM:(hd + 1) * HEAD_DIM]
            gain = qn_ref[...] if hd < N_HEADS else kn_ref[...]
            xh = xh * lax.rsqrt(jnp.mean(xh * xh, axis=-1, keepdims=True) + EPS) * gain
            if rope:
                partner = jnp.where(even_lane, pltpu.roll(xh, HEAD_DIM - 1, 1), pltpu.roll(xh, 1, 1))
                xh = xh * cos_ref[...] + partner * sin_ref[...]
            if hd < N_HEADS:
                q_ref[:, hd * HEAD_DIM:(hd + 1) * HEAD_DIM] = xh.astype(BF16)
            else:
                cols = slice((hd - N_HEADS) * HEAD_DIM, (hd - N_HEADS + 1) * HEAD_DIM)
                kb_ref[:, cols] = xh.astype(BF16)
                if not rope:
                    k32_ref[:, hd - N_HEADS, :] = xh

    is_sample = pl.program_id(0) >= T_PROMPT // tile

    @pl.when(is_sample)
    def _():
        heads(True)

    @pl.when(jnp.logical_not(is_sample))
    def _():
        heads(False)
        for kh in range(KV_HEADS):
            v32_ref[:, kh, :] = v[:, kh * HEAD_DIM:(kh + 1) * HEAD_DIM]


def _qkv_call(x, gn, mods, layer, w_qkv, q_norm, k_norm, cos_t, sin_t):
    tm = 512
    pos_tiles = DEC_SEQ // tm
    n_prompt = T_PROMPT // tm
    kvw = KV_HEADS * HEAD_DIM

    def pos_map(i):
        return (jnp.maximum(i - n_prompt, 0) % pos_tiles, 0)

    def prompt_map(i):
        return (jnp.minimum(i, n_prompt - 1), 0, 0)

    return pl.pallas_call(
        functools.partial(_qkv_kernel, tile=tm),
        out_shape=(jax.ShapeDtypeStruct((T_TOK, N_HEADS * HEAD_DIM), BF16),
                   jax.ShapeDtypeStruct((T_TOK, kvw), BF16),
                   jax.ShapeDtypeStruct((T_TOK, kvw), BF16),
                   jax.ShapeDtypeStruct((T_PROMPT, KV_HEADS, HEAD_DIM), F32),
                   jax.ShapeDtypeStruct((T_PROMPT, KV_HEADS, HEAD_DIM), F32)),
        grid=(T_TOK // tm,),
        in_specs=[
            pl.BlockSpec((tm, D_MODEL), lambda i: (i, 0)),
            pl.BlockSpec((1, D_MODEL), lambda i: (0, 0)),
            pl.BlockSpec((None, 1, 6 * D_MODEL), lambda i: (_cond_row(i, tm), 0, 0)),
            pl.BlockSpec(w_qkv.shape, lambda i: (0, 0)),
            pl.BlockSpec((1, HEAD_DIM), lambda i: (0, 0)),
            pl.BlockSpec((1, HEAD_DIM), lambda i: (0, 0)),
            pl.BlockSpec((tm, HEAD_DIM), pos_map),
            pl.BlockSpec((tm, HEAD_DIM), pos_map),
        ],
        out_specs=(pl.BlockSpec((tm, N_HEADS * HEAD_DIM), lambda i: (i, 0)),
                   pl.BlockSpec((tm, kvw), lambda i: (i, 0)),
                   pl.BlockSpec((tm, kvw), lambda i: (i, 0)),
                   pl.BlockSpec((tm, KV_HEADS, HEAD_DIM), prompt_map),
                   pl.BlockSpec((tm, KV_HEADS, HEAD_DIM), prompt_map)),
        compiler_params=pltpu.CompilerParams(vmem_limit_bytes=VMEM_LIMIT),
        name="odd_qkv",
    )(x, gn, mods, w_qkv, q_norm, k_norm, cos_t, sin_t)


def _rope_tables():
    f32 = np.float32
    rows = DEC_SEQ // GRID_W
    row = np.repeat(np.arange(rows, dtype=f32), GRID_W)
    col = np.tile(np.arange(GRID_W, dtype=f32), rows)
    inv = np.power(f32(ROPE_THETA), -np.arange(0, AXIS_DIM, 2, dtype=f32) / f32(AXIS_DIM)).astype(f32)
    ang = np.concatenate([row[:, None] * inv, col[:, None] * inv], axis=-1).astype(f32)
    cos_t = np.repeat(np.cos(ang), 2, axis=-1).astype(f32)
    sin = np.sin(ang).astype(f32)
    sin_t = np.stack([-sin, sin], axis=-1).reshape(DEC_SEQ, HEAD_DIM)
    return jnp.asarray(cos_t), jnp.asarray(sin_t)


def _attn_kernel(*refs, has_cache):
    q_ref, k_ref, v_ref = refs[:3]
    ck_ref, cv_ref = refs[3:5] if has_cache else (None, None)
    o_ref = refs[-1]
    c = HEAD_DIM ** -0.5 * math.log2(math.e)
    ones_col = (lax.broadcasted_iota(jnp.int32, (1, HEAD_DIM), 1) == 0).astype(BF16)

    def with_ones(v):
        return jnp.concatenate([v, jnp.broadcast_to(ones_col, v.shape)], axis=1)

    for g in range(k_ref.shape[1] // HEAD_DIM):
        kv_cols = slice(g * HEAD_DIM, (g + 1) * HEAD_DIM)
        k = k_ref[:, kv_cols]
        v = with_ones(v_ref[:, kv_cols])
        if has_cache:
            ck = ck_ref[:, kv_cols].astype(BF16)
            cv = with_ones(cv_ref[:, kv_cols].astype(BF16))
        for r in range(Q_PER_KV):
            cs = slice((g * Q_PER_KV + r) * HEAD_DIM, (g * Q_PER_KV + r + 1) * HEAD_DIM)
            q = q_ref[:, cs]
            s = lax.dot_general(q, k, NT_DIMS, preferred_element_type=F32)
            m = jnp.max(s, axis=-1, keepdims=True)
            if has_cache:
                sc = lax.dot_general(q, ck, NT_DIMS, preferred_element_type=F32)
                m = jnp.maximum(m, jnp.max(sc, axis=-1, keepdims=True))
            mc = m * c
            o = _dot(jnp.exp2(s * c - mc).astype(BF16), v)
            if has_cache:
                o = o + _dot(jnp.exp2(sc * c - mc).astype(BF16), cv)
            o_ref[:, cs] = (o[:, 0:HEAD_DIM] / o[:, HEAD_DIM:HEAD_DIM + 1]).astype(BF16)


def _attn_call(q, k, v, cache_k, cache_v, seq_len, row0, nrows, kv_per_step):
    assert row0 % seq_len == 0 and nrows % seq_len == 0 and KV_HEADS % kv_per_step == 0
    has_cache = cache_k is not None
    b0 = row0 // seq_len
    qw = kv_per_step * Q_PER_KV * HEAD_DIM
    kw = kv_per_step * HEAD_DIM
    in_specs = [
        pl.BlockSpec((seq_len, qw), lambda b, g: (b0 + b, g)),
        pl.BlockSpec((seq_len, kw), lambda b, g: (b0 + b, g)),
        pl.BlockSpec((seq_len, kw), lambda b, g: (b0 + b, g)),
    ]
    args = [q, k, v]
    if has_cache:
        in_specs += [pl.BlockSpec((PAST_LEN, kw), lambda b, g: (b, g)),
                     pl.BlockSpec((PAST_LEN, kw), lambda b, g: (b, g))]
        args += [cache_k, cache_v]
    return pl.pallas_call(
        functools.partial(_attn_kernel, has_cache=has_cache),
        out_shape=jax.ShapeDtypeStruct((nrows, N_HEADS * HEAD_DIM), BF16),
        grid=(nrows // seq_len, KV_HEADS // kv_per_step),
        in_specs=in_specs,
        out_specs=pl.BlockSpec((seq_len, qw), lambda b, g: (b, g)),
        compiler_params=pltpu.CompilerParams(vmem_limit_bytes=VMEM_LIMIT),
        name=f"attn_len{seq_len}",
    )(*args)


def kernel(x_prompt, x_sample, state_s5_re, state_s5_im, state_gla, cache_k, cache_v, c, c_ctx, norm_mix, norm_mlp, w_ada, b_ada, w_mlp_in, w_mlp_out, w_in_e, w_out_e, s5_lambda_re, s5_lambda_im, s5_log_dt, s5_b_re, s5_b_im, s5_c_re, s5_c_im, s5_d, s5_w_glu, s5_b_glu, gla_w_gate2, gla_b_gate, gla_norm, w_qkv_o, w_o_o, q_norm, k_norm):
    xp = x_prompt.reshape(T_PROMPT, D_MODEL)
    xs = x_sample.reshape(T_SAMPLE, D_MODEL)
    cond8 = jnp.concatenate([c_ctx[None, :], c, jnp.zeros((COND_ROWS - 1 - DEC_BATCH, D_MODEL), F32)], axis=0)
    b_ada3 = b_ada.reshape(DEPTH, 1, 6 * D_MODEL)
    mods0 = _ada_call(cond8, w_ada, b_ada3, 0)
    w1_all, w2_all = w_mlp_in, w_mlp_out

    n_main = S5_WIDTH + 2 * GLA_QK + 2 * GLA_VW
    w_in = w_in_e[0]
    w_main = w_in[:, :n_main].astype(BF16)
    w_glr = jnp.pad(w_in[:, n_main:], ((0, 0), (0, 128 - 2 * GLA_RANK))).astype(BF16)
    zg = jnp.zeros((GLA_RANK, GLA_QK), F32)
    w_gate = jnp.concatenate([jnp.concatenate([gla_w_gate2[0, 0], zg], axis=1),
                              jnp.concatenate([zg, gla_w_gate2[0, 1]], axis=1),
                              jnp.zeros((128 - 2 * GLA_RANK, 2 * GLA_QK), F32)], axis=0).astype(BF16)
    b_gate = gla_b_gate[0].reshape(1, 2 * GLA_QK)
    u, z, g = _inproj_call(xp, xs, norm_mix[0:1], mods0, 0, w_main, w_glr, w_gate, b_gate)

    mats = _s5_prep_call(s5_lambda_re[0], s5_lambda_im[0], s5_log_dt[0], s5_b_re[0], s5_b_im[0],
                         s5_c_re[0], s5_c_im[0])

    def state_rows(s):
        return jnp.transpose(s, (2, 0, 1, 3)).reshape(S5_GROUPS, DEC_BATCH, 2 * S5_STATE)

    h0 = jnp.concatenate([state_rows(state_s5_re[:, 0]), state_rows(state_s5_im[:, 0])], axis=-1)
    nsteps = S5_GROUPS // S5_GPB
    h0 = jnp.transpose(h0.reshape(nsteps, S5_GPB, DEC_BATCH, S5_W), (0, 2, 1, 3))
    y5, ns = _s5_call(u, mats, h0)
    ns = jnp.transpose(ns, (0, 2, 1, 3)).reshape(S5_GROUPS, BATCH, S5_W)

    def state_out(n):
        return jnp.transpose(n.reshape(S5_GROUPS, BATCH, 2, S5_STATE), (1, 2, 0, 3))[:, None]

    new_s5_re = state_out(ns[:, :, :2 * S5_STATE])
    new_s5_im = state_out(ns[:, :, 2 * S5_STATE:])

    gn_gla = gla_norm[0].reshape(1, GLA_DV)
    gla_p, sfin = _gla_call(z, g, gn_gla, None, SEQ, BATCH, 0, nsub=4)
    s0 = state_gla[:, 0].reshape(DEC_BATCH, 2, GLA_QK, GLA_DV)
    gla_s, _, mods1 = _gla_call(z, g, gn_gla, s0, DEC_SEQ, DEC_BATCH, T_PROMPT, nsub=1,
                                ada=(cond8, w_ada, b_ada3, 1))
    new_gla = sfin.reshape(BATCH, 1, 2, GLA_HEADS, GLA_DK, GLA_DV)

    x = _even_out_call(xp, xs, y5, u, s5_d[0].reshape(1, S5_WIDTH), s5_w_glu[0].astype(BF16),
                       s5_b_glu[0].reshape(1, S5_WIDTH), gla_p, gla_s, w_out_e[0].astype(BF16), mods0, 0,
                       norm_mlp[0:1], w1_all, w2_all)

    cos_t, sin_t = _rope_tables()
    q, k, v, k32, v32 = _qkv_call(x, norm_mix[1:2], mods1, 1, w_qkv_o[0].astype(BF16),
                                  q_norm[0].reshape(1, HEAD_DIM), k_norm[0].reshape(1, HEAD_DIM), cos_t, sin_t)
    att_p = _attn_call(q, k, v, None, None, SEQ, 0, T_PROMPT, kv_per_step=1)
    ck = cache_k[:, 0].reshape(DEC_BATCH * PAST_LEN, KV_HEADS * HEAD_DIM)
    cv = cache_v[:, 0].reshape(DEC_BATCH * PAST_LEN, KV_HEADS * HEAD_DIM)
    att_s = _attn_call(q, k, v, ck, cv, DEC_SEQ, T_PROMPT, T_SAMPLE, kv_per_step=1)
    yp, ys = _odd_out_call(x, att_p, att_s, w_o_o[0].astype(BF16), mods1, 1, norm_mlp[1:2],
                           w1_all, w2_all)

    new_k = k32.reshape(BATCH, 1, SEQ, KV_HEADS, HEAD_DIM)
    new_v = v32.reshape(BATCH, 1, SEQ, KV_HEADS, HEAD_DIM)
    y_prompt = yp.reshape(BATCH, SEQ, D_MODEL)
    y_sample = ys.reshape(DEC_BATCH, DEC_SEQ, D_MODEL)
    return (y_prompt, y_sample, new_s5_re, new_s5_im, new_gla, new_k, new_v)
```

```python
import functools
import math

import jax
import jax.numpy as jnp
import numpy as np
from jax import lax
from jax.experimental import pallas as pl
from jax.experimental.pallas import tpu as pltpu

F32 = jnp.float32
BF16 = jnp.bfloat16

D_MODEL = 1024
BATCH = 16
SEQ = 256
DEPTH = 2
DEC_BATCH = 4
DEC_SEQ = 1024
PAST_LEN = 512
GRID_W = 64
S5_WIDTH = 512
S5_GROUP_CH = 16
S5_GROUPS = 32
S5_STATE = 64
GLA_HEADS = 4
GLA_VW = 512
GLA_DV = 128
GLA_DK = 64
GLA_QK = 256
GLA_RANK = 16
GLA_TAU = 16.0
GLA_CHUNK = 64
GLA_CPB = 4
GLA_BLK = GLA_CPB * GLA_CHUNK
HEAD_DIM = 128
N_HEADS = 8
KV_HEADS = 2
Q_PER_KV = N_HEADS // KV_HEADS
AXIS_DIM = 64
ROPE_THETA = 10000.0
D_FF = 4096
EPS = 1e-6

T_PROMPT = BATCH * SEQ
T_SAMPLE = DEC_BATCH * DEC_SEQ
T_TOK = T_PROMPT + T_SAMPLE
COND_ROWS = 8
COND_SPAN = 1024
PROMPT_SPANS = T_PROMPT // COND_SPAN

S5_Q = 16
S5_W = S5_Q * S5_GROUP_CH
S5_GPB = 128 // S5_GROUP_CH
S5_ROWS = T_TOK // S5_Q
S5_PROMPT_ROWS = T_PROMPT // S5_Q
S5_PROMPT_CHUNKS = SEQ // S5_Q
S5_SAMPLE_CHUNKS = DEC_SEQ // S5_Q

VMEM_LIMIT = 56 * 1024 * 1024
VMEM_LIMIT_TAIL = 60 * 1024 * 1024

NT_DIMS = (((1,), (1,)), ((), ()))
TN_DIMS = (((0,), (0,)), ((), ()))


def _cond_row(i, tile):
    return jnp.maximum((i * tile) // COND_SPAN - (PROMPT_SPANS - 1), 0)


def _norm_mod(x, gain, shift, scale):
    y = x * lax.rsqrt(jnp.mean(x * x, axis=-1, keepdims=True) + EPS)
    return (y * gain) * (1.0 + scale) + shift


def _dot(a, b):
    return jnp.dot(a, b, preferred_element_type=F32)


def _ada_kernel(cond_ref, w_ref, b_ref, o_ref):
    s = jax.nn.silu(cond_ref[...]).astype(BF16)
    o_ref[:, 0, :] = _dot(s, w_ref[...].astype(BF16)) + b_ref[...]


ADA_SHAPE = jax.ShapeDtypeStruct((COND_ROWS, 1, 6 * D_MODEL), F32)


def _ada_specs(layer, tn):
    return ([pl.BlockSpec((COND_ROWS, D_MODEL), lambda *ids: (0, 0)),
             pl.BlockSpec((None, D_MODEL, tn), lambda *ids: (layer, 0, ids[-1])),
             pl.BlockSpec((None, 1, tn), lambda *ids: (layer, 0, ids[-1]))],
            pl.BlockSpec((COND_ROWS, 1, tn), lambda *ids: (0, 0, ids[-1])))


def _token_specs(tile, width=D_MODEL):
    n_prompt = T_PROMPT // tile
    return [pl.BlockSpec((tile, width), lambda i: (jnp.minimum(i, n_prompt - 1), 0)),
            pl.BlockSpec((tile, width), lambda i: (jnp.maximum(i - n_prompt, 0), 0))]


def _token_tile(xp_ref, xs_ref, tile):
    return jnp.where(pl.program_id(0) < T_PROMPT // tile, xp_ref[...], xs_ref[...])


def _inproj_kernel(xp_ref, xs_ref, gn_ref, m_ref, w_ref, wglr_ref, wg_ref, bg_ref, u_ref, z_ref, g_ref, *, tile):
    x = _token_tile(xp_ref, xs_ref, tile)
    h = _norm_mod(x, gn_ref[...], m_ref[:, 0:D_MODEL], m_ref[:, D_MODEL:2 * D_MODEL]).astype(BF16)
    z = _dot(h, w_ref[...])
    for blk in range(S5_WIDTH // 128):
        u_ref[blk] = z[:, blk * 128:(blk + 1) * 128]
    z_ref[...] = z[:, S5_WIDTH:]
    glr = _dot(h, wglr_ref[...]).astype(BF16)
    pre = _dot(glr, wg_ref[...]) + bg_ref[...]
    g_ref[...] = jax.nn.log_sigmoid(pre) * (1.0 / GLA_TAU)


def _inproj_call(xp, xs, gn, mods, layer, w_main, w_glr, w_gate, b_gate):
    tm = 512
    nz = w_main.shape[1]
    return pl.pallas_call(
        functools.partial(_inproj_kernel, tile=tm),
        out_shape=(jax.ShapeDtypeStruct((S5_WIDTH // 128, T_TOK, 128), F32),
                   jax.ShapeDtypeStruct((T_TOK, nz - S5_WIDTH), F32),
                   jax.ShapeDtypeStruct((T_TOK, 2 * GLA_QK), F32)),
        grid=(T_TOK // tm,),
        in_specs=_token_specs(tm) + [
            pl.BlockSpec((1, D_MODEL), lambda i: (0, 0)),
            pl.BlockSpec((None, 1, 6 * D_MODEL), lambda i: (_cond_row(i, tm), 0, 0)),
            pl.BlockSpec((D_MODEL, nz), lambda i: (0, 0)),
            pl.BlockSpec((D_MODEL, 128), lambda i: (0, 0)),
            pl.BlockSpec((128, 2 * GLA_QK), lambda i: (0, 0)),
            pl.BlockSpec((1, 2 * GLA_QK), lambda i: (0, 0)),
        ],
        out_specs=(pl.BlockSpec((S5_WIDTH // 128, tm, 128), lambda i: (0, i, 0)),
                   pl.BlockSpec((tm, nz - S5_WIDTH), lambda i: (i, 0)),
                   pl.BlockSpec((tm, 2 * GLA_QK), lambda i: (i, 0))),
        compiler_params=pltpu.CompilerParams(vmem_limit_bytes=VMEM_LIMIT),
        name="even_inproj",
    )(xp, xs, gn, mods, w_main, w_glr, w_gate, b_gate)


S5_PREP_GPB = 8
_PREP_LRE, _PREP_LIM, _PREP_LDT = 0, 1, 2
_PREP_BT_RE, _PREP_BT_IM, _PREP_C_RE, _PREP_C_IM, _PREP_ROWS = 8, 24, 40, 56, 72


def _s5_prep_kernel(p_ref, cc_ref, cond_ref, wada_ref, bada_ref, t_ref, bq_ref, cqt_ref, be_ref, a_ref, mods_ref,
                    t_scr, dd_scr):
    _ada_kernel(cond_ref, wada_ref, bada_ref, mods_ref)
    for gi in range(S5_PREP_GPB):
        _s5_prep_group(p_ref.at[gi], cc_ref.at[gi], t_ref.at[gi], bq_ref.at[gi], cqt_ref.at[gi], be_ref.at[gi],
                       a_ref.at[gi], t_scr, dd_scr)


def _s5_prep_group(p_ref, cc_ref, t_ref, bq_ref, cqt_ref, be_ref, a_ref, t_scr, dd_scr):
    gch = S5_GROUP_CH
    lre = p_ref[_PREP_LRE:_PREP_LRE + 1]
    lim = p_ref[_PREP_LIM:_PREP_LIM + 1]
    dt = jnp.exp(p_ref[_PREP_LDT:_PREP_LDT + 1])
    a = lre * dt
    th = lim * dt

    def lam_pow(k):
        mag = jnp.exp(k * a)
        return mag * jnp.cos(k * th), mag * jnp.sin(k * th)

    lb_re, lb_im = lam_pow(1.0)
    nr = lb_re - 1.0
    den = lre * lre + lim * lim
    cf_re = (nr * lre + lb_im * lim) / den
    cf_im = (lb_im * lre - nr * lim) / den
    bt_re = p_ref[_PREP_BT_RE:_PREP_BT_RE + gch]
    bt_im = p_ref[_PREP_BT_IM:_PREP_BT_IM + gch]
    bb_re = jnp.tile(cf_re * bt_re - cf_im * bt_im, (S5_Q, 1))
    bb_im = jnp.tile(cf_re * bt_im + cf_im * bt_re, (S5_Q, 1))

    shape = (S5_W, 128)
    pos = lax.shift_right_logical(lax.broadcasted_iota(jnp.int32, shape, 0), 4)
    is_f = lax.broadcasted_iota(jnp.int32, shape, 1) < S5_STATE
    posq = lax.broadcasted_iota(jnp.int32, (S5_Q, 128), 0).astype(F32)
    is_fq = lax.broadcasted_iota(jnp.int32, (S5_Q, 128), 1) < S5_STATE

    def per_channel(tbl):
        return jnp.broadcast_to(tbl[:, None, :], (S5_Q, S5_GROUP_CH, 128)).reshape(shape)

    p_re, p_im = map(per_channel, lam_pow(jnp.where(is_fq, (S5_Q - 1.0) - posq, posq)))
    w_re = p_re * bb_re - p_im * bb_im
    w_im = p_re * bb_im + p_im * bb_re
    bq = jnp.concatenate([w_re, w_im], axis=1)
    bqt = jnp.transpose(bq)
    bq_ref[...] = bqt.astype(BF16)

    edge = pos == jnp.where(is_f, 0, S5_Q - 1)
    be = jnp.concatenate([jnp.where(edge, bb_re, 0.0), jnp.where(edge, bb_im, 0.0)], axis=1)
    be_ref[...] = jnp.transpose(be).astype(BF16)

    q_re, q_im = map(per_channel, lam_pow(jnp.where(is_fq, posq + 1.0, S5_Q - posq)))
    ct_re = jnp.tile(p_ref[_PREP_C_RE:_PREP_C_RE + gch], (S5_Q, 1))
    ct_im = jnp.tile(p_ref[_PREP_C_IM:_PREP_C_IM + gch], (S5_Q, 1))
    g_re = q_re * ct_re - q_im * ct_im
    g_im = q_re * ct_im + q_im * ct_re
    cqt_ref[...] = jnp.concatenate([g_re, -g_im], axis=1).astype(BF16)

    a_re, a_im = lam_pow(float(S5_Q))
    a_ref[...] = jnp.concatenate([a_re, a_im], axis=1)

    kf = jnp.dot(cc_ref[0:gch], bqt, precision=lax.Precision.HIGHEST, preferred_element_type=F32)
    kb = jnp.dot(cc_ref[gch:2 * gch], bqt, precision=lax.Precision.HIGHEST, preferred_element_type=F32)
    lo = S5_W - gch
    dd_scr[:, 0:S5_W] = kf
    dd_scr[:, lo:lo + S5_W] = kb
    dd_scr[:, lo:S5_W] = kf[:, lo:S5_W] + kb[:, 0:gch]
    for t in range(S5_Q):
        c0 = (S5_Q - 1 - t) * gch
        t_scr[t * gch:(t + 1) * gch, :] = dd_scr[:, c0:c0 + S5_W]
    t_ref[...] = t_scr[...].astype(BF16)


def _s5_prep_call(lam_re, lam_im, log_dt, b_re, b_im, c_re, c_im, ada):
    def fb(p):
        return jnp.transpose(p, (1, 0, 2)).reshape(S5_GROUPS, 1, 2 * S5_STATE)

    def dup(p):
        return jnp.concatenate([p, p], axis=-1)

    ldt = fb(jnp.broadcast_to(log_dt[:, :, None], (2, S5_GROUPS, S5_STATE)))
    pad = jnp.zeros((S5_GROUPS, _PREP_BT_RE - _PREP_LDT - 1, 128), F32)
    packed = jnp.concatenate([fb(lam_re), fb(lam_im), ldt, pad,
                              dup(jnp.transpose(b_re, (0, 2, 1))), dup(jnp.transpose(b_im, (0, 2, 1))),
                              dup(c_re), dup(c_im)], axis=1)
    zero = jnp.zeros_like(c_re)
    cc = jnp.concatenate([jnp.concatenate([c_re, zero, -c_im, zero], axis=-1),
                          jnp.concatenate([zero, c_re, zero, -c_im], axis=-1)], axis=1)

    gpb = S5_PREP_GPB
    nsteps = S5_GROUPS // gpb
    sq = pl.BlockSpec((gpb, S5_W, S5_W), lambda g: (g, 0, 0))
    sq_shape = jax.ShapeDtypeStruct((S5_GROUPS, S5_W, S5_W), BF16)
    cond8, w_ada, b_ada, ada_layer = ada
    ada_in, ada_out = _ada_specs(ada_layer, 6 * D_MODEL // nsteps)
    *mats, mods = pl.pallas_call(
        _s5_prep_kernel,
        out_shape=(sq_shape, sq_shape, sq_shape, sq_shape,
                   jax.ShapeDtypeStruct((S5_GROUPS, 1, S5_W), F32), ADA_SHAPE),
        grid=(nsteps,),
        in_specs=[pl.BlockSpec((gpb, _PREP_ROWS, 128), lambda g: (g, 0, 0)),
                  pl.BlockSpec((gpb, 2 * S5_GROUP_CH, S5_W), lambda g: (g, 0, 0))] + ada_in,
        out_specs=(sq, sq, sq, sq, pl.BlockSpec((gpb, 1, S5_W), lambda g: (g, 0, 0)), ada_out),
        scratch_shapes=[pltpu.VMEM((S5_W, S5_W), F32), pltpu.VMEM((S5_GROUP_CH, 2 * S5_W), F32)],
        compiler_params=pltpu.CompilerParams(vmem_limit_bytes=VMEM_LIMIT),
        name="s5_prep",
    )(packed, cc, cond8, w_ada, b_ada)
    return mats, mods


def _s5_kernel(u_ref, tt_ref, bqt_ref, cqt_ref, bet_ref, a_ref, h0_ref, y_ref, ns_ref,
               ut_scr, x_scr, spf_scr, spb_scr, ne_scr, yt_scr, xt_scr):
    gch = S5_GROUP_CH
    for s in range(S5_Q):
        rows = u_ref[pl.ds(s, S5_ROWS, stride=S5_Q), :]
        rows_t = jnp.transpose(rows).astype(BF16)
        for gl in range(S5_GPB):
            ut_scr[gl, s * gch:(s + 1) * gch, :] = rows_t[gl * gch:(gl + 1) * gch, :]

    for gl in range(S5_GPB):
        ut = ut_scr[gl]
        xt_scr[...] = _dot(bqt_ref[gl], ut)
        x = jnp.transpose(xt_scr[...])
        xt_scr[:, 0:S5_PROMPT_ROWS] = _dot(bet_ref[gl], ut[:, 0:S5_PROMPT_ROWS])
        ne = jnp.transpose(xt_scr[:, 0:S5_PROMPT_ROWS])
        for part in range(2):
            x_scr[part, pl.ds(gl, S5_ROWS, stride=S5_GPB), :] = x[:, part * 128:(part + 1) * 128]
            ne_scr[part, pl.ds(gl, S5_PROMPT_ROWS, stride=S5_GPB), :] = ne[:, part * 128:(part + 1) * 128]

    is_f = lax.broadcasted_iota(jnp.int32, (1, 128), 1) < S5_STATE
    a_re = a_ref[:, 0:128]
    a_im = a_ref[:, 128:256]

    def tile(row):
        return pl.ds(pl.multiple_of(row * S5_GPB, S5_GPB), S5_GPB)

    def scan(base, nseq, nchunk, s_init):
        def body(i, state):
            new = []
            for b in range(nseq):
                s_re, s_im = state[b]
                rows_f = tile(base + b * nchunk + i)
                rows_b = tile(base + b * nchunk + (nchunk - 1 - i))
                spf_scr[0, rows_f, :] = s_re
                spf_scr[1, rows_f, :] = s_im
                spb_scr[0, rows_b, :] = s_re
                spb_scr[1, rows_b, :] = s_im
                x_re = jnp.where(is_f, x_scr[0, rows_f, :], x_scr[0, rows_b, :])
                x_im = jnp.where(is_f, x_scr[1, rows_f, :], x_scr[1, rows_b, :])
                new.append((a_re * s_re - a_im * s_im + x_re, a_re * s_im + a_im * s_re + x_im))
            return tuple(new)

        lax.fori_loop(0, nchunk, body, tuple(s_init))

    zero = jnp.zeros((S5_GPB, 128), F32)
    scan(0, BATCH, S5_PROMPT_CHUNKS, [(zero, zero)] * BATCH)
    scan(S5_PROMPT_ROWS, DEC_BATCH, S5_SAMPLE_CHUNKS,
         [(h0_ref[b, :, 0:128], h0_ref[b, :, 128:256]) for b in range(DEC_BATCH)])

    for b in range(BATCH):
        first = pl.ds(b * S5_PROMPT_CHUNKS * S5_GPB, S5_GPB)
        last = pl.ds(((b + 1) * S5_PROMPT_CHUNKS - 1) * S5_GPB, S5_GPB)
        for part in range(2):
            ns_ref[b, :, part * 128:(part + 1) * 128] = jnp.where(is_f, ne_scr[part, first, :], ne_scr[part, last, :])

    for gl in range(S5_GPB):
        rows = pl.ds(gl, S5_ROWS, stride=S5_GPB)
        carried = jnp.concatenate([jnp.where(is_f, spf_scr[p, rows, :], spb_scr[p, rows, :]) for p in range(2)],
                                  axis=1).astype(BF16)
        yt = _dot(tt_ref[gl], ut_scr[gl]) + lax.dot_general(cqt_ref[gl], carried, NT_DIMS,
                                                            preferred_element_type=F32)
        for t in range(S5_Q):
            yt_scr[t, gl * gch:(gl + 1) * gch, :] = yt[t * gch:(t + 1) * gch, :]
    for t in range(S5_Q):
        y_ref[pl.ds(t, S5_ROWS, stride=S5_Q), :] = jnp.transpose(yt_scr[t])


def _s5_call(u, mats, h0):
    tt_m, bqt_m, cqt_m, bet_m, a_m = mats
    nsteps = S5_GROUPS // S5_GPB
    sq = pl.BlockSpec((S5_GPB, S5_W, S5_W), lambda g: (g, 0, 0))
    state_scr = pltpu.VMEM((2, S5_ROWS * S5_GPB, 128), F32)
    return pl.pallas_call(
        _s5_kernel,
        out_shape=(jax.ShapeDtypeStruct((nsteps, T_TOK, 128), F32),
                   jax.ShapeDtypeStruct((nsteps, BATCH, S5_GPB, S5_W), F32)),
        grid=(nsteps,),
        in_specs=[
            pl.BlockSpec((None, T_TOK, 128), lambda g: (g, 0, 0)),
            sq, sq, sq, sq,
            pl.BlockSpec((S5_GPB, S5_W), lambda g: (g, 0)),
            pl.BlockSpec((None, DEC_BATCH, S5_GPB, S5_W), lambda g: (g, 0, 0, 0)),
        ],
        out_specs=(pl.BlockSpec((None, T_TOK, 128), lambda g: (g, 0, 0)),
                   pl.BlockSpec((None, BATCH, S5_GPB, S5_W), lambda g: (g, 0, 0, 0))),
        scratch_shapes=[pltpu.VMEM((S5_GPB, S5_W, S5_ROWS), BF16), state_scr, state_scr, state_scr,
                        pltpu.VMEM((2, S5_PROMPT_ROWS * S5_GPB, 128), F32),
                        pltpu.VMEM((S5_Q, 128, S5_ROWS), F32), pltpu.VMEM((S5_W, S5_ROWS), F32)],
        compiler_params=pltpu.CompilerParams(vmem_limit_bytes=VMEM_LIMIT),
        name="s5_scan",
    )(u, tt_m, bqt_m, cqt_m, bet_m, a_m.reshape(S5_GROUPS, S5_W), h0)


def _split_bf16(x):
    hi = x.astype(BF16)
    r1 = x - hi.astype(F32)
    mid = r1.astype(BF16)
    lo = (r1 - mid.astype(F32)).astype(BF16)
    return hi, mid, lo


def _gla_kernel(*refs, seq_len, nsub, has_s0, has_ada):
    rows_refs, gn_ref = refs[:6], refs[6]
    n_in = 7 + has_s0 + 3 * has_ada
    s0_ref = refs[7] if has_s0 else None
    o_ref, sfin_ref = refs[n_in:n_in + 2]
    scratch = refs[n_in + 2 + has_ada:]
    if has_ada:
        _ada_kernel(*refs[7 + has_s0:n_in], refs[n_in + 2])
    for j in range(nsub):
        rows = pl.ds(j * seq_len, seq_len)
        _gla_sequence(*[r.at[rows, :] for r in rows_refs], gn_ref, s0_ref.at[j] if has_s0 else None,
                      o_ref.at[rows, :], sfin_ref.at[j], *[s.at[j] for s in scratch], seq_len=seq_len)


def _gla_sequence(q_ref, k_ref, v_ref, gf_ref, gb_ref, r_ref, gn_ref, s0_ref, o_ref, sfin_ref,
                  oi_scr, qd_scr, kv_scr, dec_scr, ss_scr, *, seq_len):
    has_s0 = s0_ref is not None
    nblk = seq_len // GLA_BLK
    nchunk = seq_len // GLA_CHUNK
    cl = GLA_CHUNK
    ti = lax.broadcasted_iota(jnp.int32, (GLA_BLK, GLA_BLK), 0)
    si = lax.broadcasted_iota(jnp.int32, (GLA_BLK, GLA_BLK), 1)
    same = lax.shift_right_logical(ti, 6) == lax.shift_right_logical(si, 6)
    keep = (same & (ti >= si), same & (ti <= si))
    tri = tuple(kp.astype(BF16) for kp in keep)
    lane_head = lax.shift_right_logical(lax.broadcasted_iota(jnp.int32, (cl, GLA_QK), 1), 6)
    zeros_v = jnp.zeros((cl, GLA_DV), BF16)
    heads = [(slice(h * GLA_DK, (h + 1) * GLA_DK), slice(h * GLA_DV, (h + 1) * GLA_DV)) for h in range(GLA_HEADS)]

    for j in range(nblk):
        rows = slice(j * GLA_BLK, (j + 1) * GLA_BLK)
        q = q_ref[rows, :] * (GLA_DK ** -0.5)
        k = k_ref[rows, :]
        v = v_ref[rows, :].astype(BF16)
        qd, kd, k2t = [], [], []
        for d, g_ref in enumerate((gf_ref, gb_ref)):
            b = sum(_dot(tri[d], part) for part in _split_bf16(g_ref[rows, :]))
            last = cl - 1 if d == 0 else 0
            b_last = [b[c * cl + last:c * cl + last + 1] for c in range(GLA_CPB)]
            bl = jnp.concatenate([jnp.broadcast_to(x, (cl, GLA_QK)) for x in b_last], axis=0)
            qd_d = (q * jnp.exp(b)).astype(BF16)
            qd_scr[d, rows, :] = qd_d
            qd.append(qd_d)
            kd.append((k * jnp.exp(-b)).astype(BF16))
            k2t.append(jnp.transpose(k * jnp.exp(bl - b)).astype(BF16))
            for c in range(GLA_CPB):
                dec_scr[d, j * GLA_CPB + c] = jnp.exp(jnp.transpose(jnp.broadcast_to(b_last[c], (GLA_DV, GLA_QK))))
        for h, (ks, vs) in enumerate(heads):
            att = [jnp.where(keep[d], lax.dot_general(qd[d][:, ks], kd[d][:, ks], NT_DIMS,
                                                      preferred_element_type=F32), 0.0) for d in range(2)]
            oi_scr[rows, vs] = _dot((att[0] + att[1]).astype(BF16), v[:, vs])
            vh = v[:, vs]
            vexp = jnp.concatenate(
                [jnp.concatenate([vh[c * cl:(c + 1) * cl] if c2 == c else zeros_v for c2 in range(GLA_CPB)], axis=1)
                 for c in range(GLA_CPB)], axis=0)
            for d in range(2):
                kv_scr[d, j, h] = _dot(k2t[d][ks, :], vexp)

    for d in range(2):
        s = s0_ref[d] if has_s0 else jnp.zeros((GLA_QK, GLA_DV), F32)
        for cg in (range(nchunk) if d == 0 else range(nchunk - 1, -1, -1)):
            j, c = divmod(cg, GLA_CPB)
            ss_scr[d, cg] = s.astype(BF16)
            kv = jnp.concatenate([kv_scr[d, j, h, :, c * GLA_DV:(c + 1) * GLA_DV] for h in range(GLA_HEADS)], axis=0)
            s = s * dec_scr[d, cg] + kv
        sfin_ref[d] = s

    for cg in range(nchunk):
        rows = slice(cg * cl, (cg + 1) * cl)
        inter = []
        for d in range(2):
            qc = qd_scr[d, rows, :]
            qstack = jnp.concatenate([jnp.where(lane_head == h, qc, jnp.zeros_like(qc)) for h in range(GLA_HEADS)],
                                     axis=0)
            inter.append(_dot(qstack, ss_scr[d, cg]))
        gate = jax.nn.silu(r_ref[rows, :])
        for h, (ks, vs) in enumerate(heads):
            hr = slice(h * cl, (h + 1) * cl)
            oh = oi_scr[rows, vs] + inter[0][hr] + inter[1][hr]
            oh = oh * lax.rsqrt(jnp.mean(oh * oh, axis=-1, keepdims=True) + EPS) * gn_ref[...]
            o_ref[rows, vs] = oh * gate[:, vs]


def _gla_call(z, g, gla_norm, s0, seq_len, nseq, row0, nsub, ada=None):
    blk = nsub * seq_len
    assert row0 % blk == 0 and nseq % nsub == 0
    r0 = row0 // blk
    has_s0 = s0 is not None
    qk_off = 0
    v_off = 2 * GLA_QK // GLA_VW
    in_specs = [
        pl.BlockSpec((blk, GLA_QK), lambda i: (r0 + i, qk_off)),
        pl.BlockSpec((blk, GLA_QK), lambda i: (r0 + i, qk_off + 1)),
        pl.BlockSpec((blk, GLA_VW), lambda i: (r0 + i, v_off)),
        pl.BlockSpec((blk, GLA_QK), lambda i: (r0 + i, 0)),
        pl.BlockSpec((blk, GLA_QK), lambda i: (r0 + i, 1)),
        pl.BlockSpec((blk, GLA_VW), lambda i: (r0 + i, v_off + 1)),
        pl.BlockSpec((1, GLA_DV), lambda i: (0, 0)),
    ]
    args = [z, z, z, g, g, z, gla_norm]
    state_spec = pl.BlockSpec((nsub, 2, GLA_QK, GLA_DV), lambda i: (i, 0, 0, 0))
    if has_s0:
        in_specs.append(state_spec)
        args.append(s0)
    out_shape = [jax.ShapeDtypeStruct((nseq * seq_len, GLA_VW), F32),
                 jax.ShapeDtypeStruct((nseq, 2, GLA_QK, GLA_DV), F32)]
    out_specs = [pl.BlockSpec((blk, GLA_VW), lambda i: (i, 0)), state_spec]
    nsteps = nseq // nsub
    if ada is not None:
        cond8, w_ada, b_ada, ada_layer = ada
        ada_in, ada_out = _ada_specs(ada_layer, 6 * D_MODEL // nsteps)
        in_specs += ada_in
        args += [cond8, w_ada, b_ada]
        out_shape.append(ADA_SHAPE)
        out_specs.append(ada_out)
    return pl.pallas_call(
        functools.partial(_gla_kernel, seq_len=seq_len, nsub=nsub, has_s0=has_s0, has_ada=ada is not None),
        out_shape=tuple(out_shape),
        grid=(nsteps,),
        in_specs=in_specs,
        out_specs=tuple(out_specs),
        scratch_shapes=[
            pltpu.VMEM((nsub, seq_len, GLA_VW), F32),
            pltpu.VMEM((nsub, 2, seq_len, GLA_QK), BF16),
            pltpu.VMEM((nsub, 2, seq_len // GLA_BLK, GLA_HEADS, GLA_DK, GLA_CPB * GLA_DV), F32),
            pltpu.VMEM((nsub, 2, seq_len // GLA_CHUNK, GLA_QK, GLA_DV), F32),
            pltpu.VMEM((nsub, 2, seq_len // GLA_CHUNK, GLA_QK, GLA_DV), BF16),
        ],
        compiler_params=pltpu.CompilerParams(vmem_limit_bytes=VMEM_LIMIT),
        name=f"gla_len{seq_len}",
    )(*args)


MLP_CHUNK = 512
MLP_LOAD = 256
MLP_SLOTS = 4


class _MlpWeights:
    def __init__(self, w1_hbm, w2_hbm, w1_scr, w2_scr, stage1, stage2, sem, layer):
        self.refs = (w1_hbm, w2_hbm, w1_scr, w2_scr, stage1, stage2, sem)
        self.layer = layer

    def _copies(self, p):
        w1_hbm, w2_hbm, _, _, stage1, stage2, sem = self.refs
        cols = pl.ds(p * MLP_LOAD, MLP_LOAD)
        slot = p % MLP_SLOTS
        return (pltpu.make_async_copy(w1_hbm.at[self.layer, :, cols], stage1.at[slot], sem.at[0, slot]),
                pltpu.make_async_copy(w2_hbm.at[self.layer, cols, :], stage2.at[slot], sem.at[1, slot]))

    def start(self, p):
        for cp in self._copies(p):
            cp.start()

    def prefetch(self):
        for p in range(MLP_SLOTS - 1):
            self.start(p)

    def finish(self, p):
        _, _, w1_scr, w2_scr, stage1, stage2, _ = self.refs
        ahead = p + MLP_SLOTS - 1
        if ahead < D_FF // MLP_LOAD:
            self.start(ahead)
        for cp in self._copies(p):
            cp.wait()
        cols = slice(p * MLP_LOAD, (p + 1) * MLP_LOAD)
        w1_scr[:, cols] = stage1[p % MLP_SLOTS].astype(BF16)
        w2_scr[cols, :] = stage2[p % MLP_SLOTS].astype(BF16)


def _mlp_tail(x, mix, m_ref, gn2_ref, w1_ref, w2_ref, loading=None):
    y1 = x + m_ref[:, 2 * D_MODEL:3 * D_MODEL] * mix
    h = _norm_mod(y1, gn2_ref[...], m_ref[:, 3 * D_MODEL:4 * D_MODEL], m_ref[:, 4 * D_MODEL:5 * D_MODEL]).astype(BF16)
    nchunk = D_FF // MLP_CHUNK
    acc = jnp.zeros(y1.shape, F32)
    for c in range(nchunk):
        cols = slice(c * MLP_CHUNK, (c + 1) * MLP_CHUNK)
        if loading is not None:
            per = MLP_CHUNK // MLP_LOAD
            for p in range(c * per, (c + 1) * per):
                loading.finish(p)
        a = _dot(h, w1_ref[:, cols])
        a = jnp.square(jnp.maximum(a, 0.0)).astype(BF16)
        acc = acc + _dot(a, w2_ref[cols, :])
    return y1 + m_ref[:, 5 * D_MODEL:6 * D_MODEL] * acc


def _run_tail(x, mix, m_ref, gn2_ref, weights, w1_ref, w2_ref, emit):
    first = pl.program_id(0) == 0

    @pl.when(first)
    def _():
        emit(_mlp_tail(x, mix, m_ref, gn2_ref, w1_ref, w2_ref, loading=weights))

    @pl.when(jnp.logical_not(first))
    def _():
        emit(_mlp_tail(x, mix, m_ref, gn2_ref, w1_ref, w2_ref))


def _even_out_kernel(xp_ref, xs_ref, y5_ref, u_ref, dskip_ref, wglu_ref, bglu_ref, glap_ref, glas_ref, wout_ref,
                     m_ref, gn2_ref, w1_hbm, w2_hbm, o_ref, w1_ref, w2_ref, stage1, stage2, sem, *, layer):
    weights = _MlpWeights(w1_hbm, w2_hbm, w1_ref, w2_ref, stage1, stage2, sem, layer)

    @pl.when(pl.program_id(0) == 0)
    def _():
        weights.prefetch()

    nblk = S5_WIDTH // 128
    ys = (jnp.concatenate([y5_ref[b] for b in range(nblk)], axis=1)
          + jnp.concatenate([u_ref[b] for b in range(nblk)], axis=1) * dskip_ref[...])
    gl = jax.nn.gelu(ys)
    s5o = gl * jax.nn.sigmoid(_dot(gl.astype(BF16), wglu_ref[...]) + bglu_ref[...])
    gla = _token_tile(glap_ref, glas_ref, _OUT_TM).astype(BF16)
    mix = _dot(s5o.astype(BF16), wout_ref[0:S5_WIDTH, :]) + _dot(gla, wout_ref[S5_WIDTH:, :])

    def emit(y):
        o_ref[...] = y

    _run_tail(_token_tile(xp_ref, xs_ref, _OUT_TM), mix, m_ref, gn2_ref, weights, w1_ref, w2_ref, emit)


def _odd_out_kernel(x_ref, attp_ref, atts_ref, wo_ref, m_ref, gn2_ref, w1_hbm, w2_hbm, op_ref, os_ref,
                    w1_ref, w2_ref, stage1, stage2, sem, *, layer):
    weights = _MlpWeights(w1_hbm, w2_hbm, w1_ref, w2_ref, stage1, stage2, sem, layer)

    @pl.when(pl.program_id(0) == 0)
    def _():
        weights.prefetch()

    mix = _dot(_token_tile(attp_ref, atts_ref, _OUT_TM), wo_ref[...])
    is_prompt = pl.program_id(0) < T_PROMPT // _OUT_TM

    def emit(y):
        @pl.when(is_prompt)
        def _():
            op_ref[...] = y

        @pl.when(jnp.logical_not(is_prompt))
        def _():
            os_ref[...] = y

    _run_tail(x_ref[...], mix, m_ref, gn2_ref, weights, w1_ref, w2_ref, emit)


_OUT_TM = 512


def _const_spec(shape):
    return pl.BlockSpec(shape, lambda i: (0,) * len(shape), pipeline_mode=pl.Buffered(1))


def _tail_specs(layer):
    tm = _OUT_TM
    return [
        pl.BlockSpec((None, 1, 6 * D_MODEL), lambda i: (_cond_row(i, tm), 0, 0)),
        _const_spec((1, D_MODEL)),
        pl.BlockSpec(memory_space=pl.ANY),
        pl.BlockSpec(memory_space=pl.ANY),
    ]


def _tail_scratch():
    return [pltpu.VMEM((D_MODEL, D_FF), BF16), pltpu.VMEM((D_FF, D_MODEL), BF16),
            pltpu.VMEM((MLP_SLOTS, D_MODEL, MLP_LOAD), F32), pltpu.VMEM((MLP_SLOTS, MLP_LOAD, D_MODEL), F32),
            pltpu.SemaphoreType.DMA((2, MLP_SLOTS))]


_TAIL_PARAMS = dict(dimension_semantics=("arbitrary",), vmem_limit_bytes=VMEM_LIMIT_TAIL)


def _even_out_call(xp, xs, y5, u, d_skip, w_glu, b_glu, gla_p, gla_s, w_out, mods, layer, gn2, w1, w2):
    tm = _OUT_TM
    return pl.pallas_call(
        functools.partial(_even_out_kernel, layer=layer),
        out_shape=jax.ShapeDtypeStruct((T_TOK, D_MODEL), F32),
        grid=(T_TOK // tm,),
        in_specs=_token_specs(tm) + [
            pl.BlockSpec((S5_WIDTH // 128, tm, 128), lambda i: (0, i, 0)),
            pl.BlockSpec((S5_WIDTH // 128, tm, 128), lambda i: (0, i, 0)),
            _const_spec((1, S5_WIDTH)),
            _const_spec((S5_WIDTH, S5_WIDTH)),
            _const_spec((1, S5_WIDTH)),
        ] + _token_specs(tm, GLA_VW) + [
            _const_spec((S5_WIDTH + GLA_VW, D_MODEL)),
        ] + _tail_specs(layer),
        out_specs=pl.BlockSpec((tm, D_MODEL), lambda i: (i, 0)),
        scratch_shapes=_tail_scratch(),
        compiler_params=pltpu.CompilerParams(**_TAIL_PARAMS),
        name="even_out_mlp",
    )(xp, xs, y5, u, d_skip, w_glu, b_glu, gla_p, gla_s, w_out, mods, gn2, w1, w2)


def _odd_out_call(x, att_p, att_s, w_o, mods, layer, gn2, w1, w2):
    tm = _OUT_TM
    return pl.pallas_call(
        functools.partial(_odd_out_kernel, layer=layer),
        out_shape=(jax.ShapeDtypeStruct((T_PROMPT, D_MODEL), F32),
                   jax.ShapeDtypeStruct((T_SAMPLE, D_MODEL), F32)),
        grid=(T_TOK // tm,),
        in_specs=[pl.BlockSpec((tm, D_MODEL), lambda i: (i, 0))] + _token_specs(tm) + [
            _const_spec((D_MODEL, D_MODEL)),
        ] + _tail_specs(layer),
        out_specs=tuple(_token_specs(tm)),
        scratch_shapes=_tail_scratch(),
        compiler_params=pltpu.CompilerParams(**_TAIL_PARAMS),
        name="odd_out_mlp",
    )(x, att_p, att_s, w_o, mods, gn2, w1, w2)


def _qkv_kernel(x_ref, gn_ref, m_ref, w_ref, qn_ref, kn_ref, cos_ref, sin_ref,
                q_ref, kb_ref, vb_ref, k32_ref, v32_ref, *, tile):
    h = _norm_mod(x_ref[...], gn_ref[...], m_ref[:, 0:D_MODEL], m_ref[:, D_MODEL:2 * D_MODEL]).astype(BF16)
    z = _dot(h, w_ref[...])
    v = z[:, (N_HEADS + KV_HEADS) * HEAD_DIM:]
    vb_ref[...] = v.astype(BF16)
    even_lane = (lax.broadcasted_iota(jnp.int32, (1, HEAD_DIM), 1) & 1) == 0

    def heads(rope):
        for hd in range(N_HEADS + KV_HEADS):
            xh = z[:, hd * HEAD_DIM:(hd + 1) * HEAD_DIM]
            gain = qn_ref[...] if hd < N_HEADS else kn_ref[...]
            xh = xh * lax.rsqrt(jnp.mean(xh * xh, axis=-1, keepdims=True) + EPS) * gain
            if rope:
                partner = jnp.where(even_lane, pltpu.roll(xh, HEAD_DIM - 1, 1), pltpu.roll(xh, 1, 1))
                xh = xh * cos_ref[...] + partner * sin_ref[...]
            if hd < N_HEADS:
                q_ref[:, hd * HEAD_DIM:(hd + 1) * HEAD_DIM] = xh.astype(BF16)
            else:
                cols = slice((hd - N_HEADS) * HEAD_DIM, (hd - N_HEADS + 1) * HEAD_DIM)
                kb_ref[:, cols] = xh.astype(BF16)
                if not rope:
                    k32_ref[:, hd - N_HEADS, :] = xh

    is_sample = pl.program_id(0) >= T_PROMPT // tile

    @pl.when(is_sample)
    def _():
        heads(True)

    @pl.when(jnp.logical_not(is_sample))
    def _():
        heads(False)
        for kh in range(KV_HEADS):
            v32_ref[:, kh, :] = v[:, kh * HEAD_DIM:(kh + 1) * HEAD_DIM]


def _qkv_call(x, gn, mods, layer, w_qkv, q_norm, k_norm, cos_t, sin_t):
    tm = 512
    pos_tiles = DEC_SEQ // tm
    n_prompt = T_PROMPT // tm
    kvw = KV_HEADS * HEAD_DIM

    def pos_map(i):
        return (jnp.maximum(i - n_prompt, 0) % pos_tiles, 0)

    def prompt_map(i):
        return (jnp.minimum(i, n_prompt - 1), 0, 0)

    return pl.pallas_call(
        functools.partial(_qkv_kernel, tile=tm),
        out_shape=(jax.ShapeDtypeStruct((T_TOK, N_HEADS * HEAD_DIM), BF16),
                   jax.ShapeDtypeStruct((T_TOK, kvw), BF16),
                   jax.ShapeDtypeStruct((T_TOK, kvw), BF16),
                   jax.ShapeDtypeStruct((T_PROMPT, KV_HEADS, HEAD_DIM), F32),
                   jax.ShapeDtypeStruct((T_PROMPT, KV_HEADS, HEAD_DIM), F32)),
        grid=(T_TOK // tm,),
        in_specs=[
            pl.BlockSpec((tm, D_MODEL), lambda i: (i, 0)),
            pl.BlockSpec((1, D_MODEL), lambda i: (0, 0)),
            pl.BlockSpec((None, 1, 6 * D_MODEL), lambda i: (_cond_row(i, tm), 0, 0)),
            pl.BlockSpec(w_qkv.shape, lambda i: (0, 0)),
            pl.BlockSpec((1, HEAD_DIM), lambda i: (0, 0)),
            pl.BlockSpec((1, HEAD_DIM), lambda i: (0, 0)),
            pl.BlockSpec((tm, HEAD_DIM), pos_map),
            pl.BlockSpec((tm, HEAD_DIM), pos_map),
        ],
        out_specs=(pl.BlockSpec((tm, N_HEADS * HEAD_DIM), lambda i: (i, 0)),
                   pl.BlockSpec((tm, kvw), lambda i: (i, 0)),
                   pl.BlockSpec((tm, kvw), lambda i: (i, 0)),
                   pl.BlockSpec((tm, KV_HEADS, HEAD_DIM), prompt_map),
                   pl.BlockSpec((tm, KV_HEADS, HEAD_DIM), prompt_map)),
        compiler_params=pltpu.CompilerParams(vmem_limit_bytes=VMEM_LIMIT),
        name="odd_qkv",
    )(x, gn, mods, w_qkv, q_norm, k_norm, cos_t, sin_t)


def _rope_tables():
    f32 = np.float32
    rows = DEC_SEQ // GRID_W
    row = np.repeat(np.arange(rows, dtype=f32), GRID_W)
    col = np.tile(np.arange(GRID_W, dtype=f32), rows)
    inv = np.power(f32(ROPE_THETA), -np.arange(0, AXIS_DIM, 2, dtype=f32) / f32(AXIS_DIM)).astype(f32)
    ang = np.concatenate([row[:, None] * inv, col[:, None] * inv], axis=-1).astype(f32)
    cos_t = np.repeat(np.cos(ang), 2, axis=-1).astype(f32)
    sin = np.sin(ang).astype(f32)
    sin_t = np.stack([-sin, sin], axis=-1).reshape(DEC_SEQ, HEAD_DIM)
    return jnp.asarray(cos_t), jnp.asarray(sin_t)


def _attn_kernel(*refs, has_cache):
    q_ref, k_ref, v_ref = refs[:3]
    ck_ref, cv_ref = refs[3:5] if has_cache else (None, None)
    o_ref = refs[-1]
    c = HEAD_DIM ** -0.5 * math.log2(math.e)
    ones_col = (lax.broadcasted_iota(jnp.int32, (1, HEAD_DIM), 1) == 0).astype(BF16)

    def with_ones(v):
        return jnp.concatenate([v, jnp.broadcast_to(ones_col, v.shape)], axis=1)

    for g in range(k_ref.shape[1] // HEAD_DIM):
        kv_cols = slice(g * HEAD_DIM, (g + 1) * HEAD_DIM)
        k = k_ref[:, kv_cols]
        v = with_ones(v_ref[:, kv_cols])
        if has_cache:
            ck = ck_ref[:, kv_cols].astype(BF16)
            cv = with_ones(cv_ref[:, kv_cols].astype(BF16))
        for r in range(Q_PER_KV):
            cs = slice((g * Q_PER_KV + r) * HEAD_DIM, (g * Q_PER_KV + r + 1) * HEAD_DIM)
            q = q_ref[:, cs]
            s = lax.dot_general(q, k, NT_DIMS, preferred_element_type=F32)
            m = jnp.max(s, axis=-1, keepdims=True)
            if has_cache:
                sc = lax.dot_general(q, ck, NT_DIMS, preferred_element_type=F32)
                m = jnp.maximum(m, jnp.max(sc, axis=-1, keepdims=True))
            mc = m * c
            o = _dot(jnp.exp2(s * c - mc).astype(BF16), v)
            if has_cache:
                o = o + _dot(jnp.exp2(sc * c - mc).astype(BF16), cv)
            o_ref[:, cs] = (o[:, 0:HEAD_DIM] / o[:, HEAD_DIM:HEAD_DIM + 1]).astype(BF16)


def _attn_call(q, k, v, cache_k, cache_v, seq_len, row0, nrows, kv_per_step):
    assert row0 % seq_len == 0 and nrows % seq_len == 0 and KV_HEADS % kv_per_step == 0
    has_cache = cache_k is not None
    b0 = row0 // seq_len
    qw = kv_per_step * Q_PER_KV * HEAD_DIM
    kw = kv_per_step * HEAD_DIM
    in_specs = [
        pl.BlockSpec((seq_len, qw), lambda b, g: (b0 + b, g)),
        pl.BlockSpec((seq_len, kw), lambda b, g: (b0 + b, g)),
        pl.BlockSpec((seq_len, kw), lambda b, g: (b0 + b, g)),
    ]
    args = [q, k, v]
    if has_cache:
        in_specs += [pl.BlockSpec((PAST_LEN, kw), lambda b, g: (b, g)),
                     pl.BlockSpec((PAST_LEN, kw), lambda b, g: (b, g))]
        args += [cache_k, cache_v]
    return pl.pallas_call(
        functools.partial(_attn_kernel, has_cache=has_cache),
        out_shape=jax.ShapeDtypeStruct((nrows, N_HEADS * HEAD_DIM), BF16),
        grid=(nrows // seq_len, KV_HEADS // kv_per_step),
        in_specs=in_specs,
        out_specs=pl.BlockSpec((seq_len, qw), lambda b, g: (b, g)),
        compiler_params=pltpu.CompilerParams(vmem_limit_bytes=VMEM_LIMIT),
        name=f"attn_len{seq_len}",
    )(*args)


def kernel(x_prompt, x_sample, state_s5_re, state_s5_im, state_gla, cache_k, cache_v, c, c_ctx, norm_mix, norm_mlp, w_ada, b_ada, w_mlp_in, w_mlp_out, w_in_e, w_out_e, s5_lambda_re, s5_lambda_im, s5_log_dt, s5_b_re, s5_b_im, s5_c_re, s5_c_im, s5_d, s5_w_glu, s5_b_glu, gla_w_gate2, gla_b_gate, gla_norm, w_qkv_o, w_o_o, q_norm, k_norm):
    xp = x_prompt.reshape(T_PROMPT, D_MODEL)
    xs = x_sample.reshape(T_SAMPLE, D_MODEL)
    cond8 = jnp.concatenate([c_ctx[None, :], c, jnp.zeros((COND_ROWS - 1 - DEC_BATCH, D_MODEL), F32)], axis=0)
    b_ada3 = b_ada.reshape(DEPTH, 1, 6 * D_MODEL)
    w1_all, w2_all = w_mlp_in, w_mlp_out
    mats, mods0 = _s5_prep_call(s5_lambda_re[0], s5_lambda_im[0], s5_log_dt[0], s5_b_re[0], s5_b_im[0],
                                s5_c_re[0], s5_c_im[0], ada=(cond8, w_ada, b_ada3, 0))

    n_main = S5_WIDTH + 2 * GLA_QK + 2 * GLA_VW
    w_in = w_in_e[0]
    w_main = w_in[:, :n_main].astype(BF16)
    w_glr = jnp.pad(w_in[:, n_main:], ((0, 0), (0, 128 - 2 * GLA_RANK))).astype(BF16)
    zg = jnp.zeros((GLA_RANK, GLA_QK), F32)
    w_gate = jnp.concatenate([jnp.concatenate([gla_w_gate2[0, 0], zg], axis=1),
                              jnp.concatenate([zg, gla_w_gate2[0, 1]], axis=1),
                              jnp.zeros((128 - 2 * GLA_RANK, 2 * GLA_QK), F32)], axis=0).astype(BF16)
    b_gate = gla_b_gate[0].reshape(1, 2 * GLA_QK)
    u, z, g = _inproj_call(xp, xs, norm_mix[0:1], mods0, 0, w_main, w_glr, w_gate, b_gate)

    def state_rows(s):
        return jnp.transpose(s, (2, 0, 1, 3)).reshape(S5_GROUPS, DEC_BATCH, 2 * S5_STATE)

    h0 = jnp.concatenate([state_rows(state_s5_re[:, 0]), state_rows(state_s5_im[:, 0])], axis=-1)
    nsteps = S5_GROUPS // S5_GPB
    h0 = jnp.transpose(h0.reshape(nsteps, S5_GPB, DEC_BATCH, S5_W), (0, 2, 1, 3))
    y5, ns = _s5_call(u, mats, h0)
    ns = jnp.transpose(ns, (0, 2, 1, 3)).reshape(S5_GROUPS, BATCH, S5_W)

    def state_out(n):
        return jnp.transpose(n.reshape(S5_GROUPS, BATCH, 2, S5_STATE), (1, 2, 0, 3))[:, None]

    new_s5_re = state_out(ns[:, :, :2 * S5_STATE])
    new_s5_im = state_out(ns[:, :, 2 * S5_STATE:])

    gn_gla = gla_norm[0].reshape(1, GLA_DV)
    gla_p, sfin = _gla_call(z, g, gn_gla, None, SEQ, BATCH, 0, nsub=4)
    s0 = state_gla[:, 0].reshape(DEC_BATCH, 2, GLA_QK, GLA_DV)
    gla_s, _, mods1 = _gla_call(z, g, gn_gla, s0, DEC_SEQ, DEC_BATCH, T_PROMPT, nsub=1,
                                ada=(cond8, w_ada, b_ada3, 1))
    new_gla = sfin.reshape(BATCH, 1, 2, GLA_HEADS, GLA_DK, GLA_DV)

    x = _even_out_call(xp, xs, y5, u, s5_d[0].reshape(1, S5_WIDTH), s5_w_glu[0].astype(BF16),
                       s5_b_glu[0].reshape(1, S5_WIDTH), gla_p, gla_s, w_out_e[0].astype(BF16), mods0, 0,
                       norm_mlp[0:1], w1_all, w2_all)

    cos_t, sin_t = _rope_tables()
    q, k, v, k32, v32 = _qkv_call(x, norm_mix[1:2], mods1, 1, w_qkv_o[0].astype(BF16),
                                  q_norm[0].reshape(1, HEAD_DIM), k_norm[0].reshape(1, HEAD_DIM), cos_t, sin_t)
    att_p = _attn_call(q, k, v, None, None, SEQ, 0, T_PROMPT, kv_per_step=1)
    ck = cache_k[:, 0].reshape(DEC_BATCH * PAST_LEN, KV_HEADS * HEAD_DIM)
    cv = cache_v[:, 0].reshape(DEC_BATCH * PAST_LEN, KV_HEADS * HEAD_DIM)
    att_s = _attn_call(q, k, v, ck, cv, DEC_SEQ, T_PROMPT, T_SAMPLE, kv_per_step=1)
    yp, ys = _odd_out_call(x, att_p, att_s, w_o_o[0].astype(BF16), mods1, 1, norm_mlp[1:2],
                           w1_all, w2_all)

    new_k = k32.reshape(BATCH, 1, SEQ, KV_HEADS, HEAD_DIM)
    new_v = v32.reshape(BATCH, 1, SEQ, KV_HEADS, HEAD_DIM)
    y_prompt = yp.reshape(BATCH, SEQ, D_MODEL)
    y_sample = ys.reshape(DEC_BATCH, DEC_SEQ, D_MODEL)
    return (y_prompt, y_sample, new_s5_re, new_s5_im, new_gla, new_k, new_v)
```

```python
import functools
import math

import jax
import jax.numpy as jnp
import numpy as np
from jax import lax
from jax.experimental import pallas as pl
from jax.experimental.pallas import tpu as pltpu

F32 = jnp.float32
BF16 = jnp.bfloat16

D_MODEL = 1024
BATCH = 16
SEQ = 256
DEPTH = 2
DEC_BATCH = 4
DEC_SEQ = 1024
PAST_LEN = 512
GRID_W = 64
S5_WIDTH = 512
S5_GROUP_CH = 16
S5_GROUPS = 32
S5_STATE = 64
GLA_HEADS = 4
GLA_VW = 512
GLA_DV = 128
GLA_DK = 64
GLA_QK = 256
GLA_RANK = 16
GLA_TAU = 16.0
GLA_CHUNK = 64
GLA_CPB = 4
GLA_BLK = GLA_CPB * GLA_CHUNK
HEAD_DIM = 128
N_HEADS = 8
KV_HEADS = 2
Q_PER_KV = N_HEADS // KV_HEADS
AXIS_DIM = 64
ROPE_THETA = 10000.0
D_FF = 4096
EPS = 1e-6

T_PROMPT = BATCH * SEQ
T_SAMPLE = DEC_BATCH * DEC_SEQ
T_TOK = T_PROMPT + T_SAMPLE
COND_ROWS = 8
COND_SPAN = 1024
PROMPT_SPANS = T_PROMPT // COND_SPAN

S5_Q = 16
S5_W = S5_Q * S5_GROUP_CH
S5_GPB = 128 // S5_GROUP_CH
S5_ROWS = T_TOK // S5_Q
S5_PROMPT_ROWS = T_PROMPT // S5_Q
S5_PROMPT_CHUNKS = SEQ // S5_Q
S5_SAMPLE_CHUNKS = DEC_SEQ // S5_Q

VMEM_LIMIT = 56 * 1024 * 1024
VMEM_LIMIT_TAIL = 60 * 1024 * 1024

NT_DIMS = (((1,), (1,)), ((), ()))
TN_DIMS = (((0,), (0,)), ((), ()))


def _cond_row(i, tile):
    return jnp.maximum((i * tile) // COND_SPAN - (PROMPT_SPANS - 1), 0)


def _norm_mod(x, gain, shift, scale):
    y = x * lax.rsqrt(jnp.mean(x * x, axis=-1, keepdims=True) + EPS)
    return (y * gain) * (1.0 + scale) + shift


def _dot(a, b):
    return jnp.dot(a, b, preferred_element_type=F32)


def _ada_kernel(cond_ref, w_ref, b_ref, o_ref):
    s = jax.nn.silu(cond_ref[...]).astype(BF16)
    o_ref[:, 0, :] = _dot(s, w_ref[...].astype(BF16)) + b_ref[...]


ADA_SHAPE = jax.ShapeDtypeStruct((COND_ROWS, 1, 6 * D_MODEL), F32)


def _ada_specs(layer, tn):
    return ([pl.BlockSpec((COND_ROWS, D_MODEL), lambda *ids: (0, 0)),
             pl.BlockSpec((None, D_MODEL, tn), lambda *ids: (layer, 0, ids[-1])),
             pl.BlockSpec((None, 1, tn), lambda *ids: (layer, 0, ids[-1]))],
            pl.BlockSpec((COND_ROWS, 1, tn), lambda *ids: (0, 0, ids[-1])))


def _token_specs(tile, width=D_MODEL):
    n_prompt = T_PROMPT // tile
    return [pl.BlockSpec((tile, width), lambda i: (jnp.minimum(i, n_prompt - 1), 0)),
            pl.BlockSpec((tile, width), lambda i: (jnp.maximum(i - n_prompt, 0), 0))]


def _token_tile(xp_ref, xs_ref, tile):
    return jnp.where(pl.program_id(0) < T_PROMPT // tile, xp_ref[...], xs_ref[...])


def _inproj_kernel(xp_ref, xs_ref, gn_ref, m_ref, w_ref, wglr_ref, wg_ref, bg_ref, u_ref, z_ref, g_ref, *, tile):
    x = _token_tile(xp_ref, xs_ref, tile)
    h = _norm_mod(x, gn_ref[...], m_ref[:, 0:D_MODEL], m_ref[:, D_MODEL:2 * D_MODEL]).astype(BF16)
    z = _dot(h, w_ref[...])
    for blk in range(S5_WIDTH // 128):
        u_ref[blk] = z[:, blk * 128:(blk + 1) * 128]
    z_ref[...] = z[:, S5_WIDTH:]
    glr = _dot(h, wglr_ref[...]).astype(BF16)
    pre = _dot(glr, wg_ref[...]) + bg_ref[...]
    g_ref[...] = jax.nn.log_sigmoid(pre) * (1.0 / GLA_TAU)


def _inproj_call(xp, xs, gn, mods, layer, w_main, w_glr, w_gate, b_gate):
    tm = 512
    nz = w_main.shape[1]
    return pl.pallas_call(
        functools.partial(_inproj_kernel, tile=tm),
        out_shape=(jax.ShapeDtypeStruct((S5_WIDTH // 128, T_TOK, 128), F32),
                   jax.ShapeDtypeStruct((T_TOK, nz - S5_WIDTH), F32),
                   jax.ShapeDtypeStruct((T_TOK, 2 * GLA_QK), F32)),
        grid=(T_TOK // tm,),
        in_specs=_token_specs(tm) + [
            pl.BlockSpec((1, D_MODEL), lambda i: (0, 0)),
            pl.BlockSpec((None, 1, 6 * D_MODEL), lambda i: (_cond_row(i, tm), 0, 0)),
            pl.BlockSpec((D_MODEL, nz), lambda i: (0, 0)),
            pl.BlockSpec((D_MODEL, 128), lambda i: (0, 0)),
            pl.BlockSpec((128, 2 * GLA_QK), lambda i: (0, 0)),
            pl.BlockSpec((1, 2 * GLA_QK), lambda i: (0, 0)),
        ],
        out_specs=(pl.BlockSpec((S5_WIDTH // 128, tm, 128), lambda i: (0, i, 0)),
                   pl.BlockSpec((tm, nz - S5_WIDTH), lambda i: (i, 0)),
                   pl.BlockSpec((tm, 2 * GLA_QK), lambda i: (i, 0))),
        compiler_params=pltpu.CompilerParams(vmem_limit_bytes=VMEM_LIMIT),
        name="even_inproj",
    )(xp, xs, gn, mods, w_main, w_glr, w_gate, b_gate)


S5_PREP_GPB = 8
_PREP_LRE, _PREP_LIM, _PREP_LDT = 0, 1, 2
_PREP_BT_RE, _PREP_BT_IM, _PREP_C_RE, _PREP_C_IM, _PREP_ROWS = 8, 24, 40, 56, 72


def _s5_prep_kernel(p_ref, cc_ref, cond_ref, wada_ref, bada_ref, win_ref,
                    t_ref, bq_ref, cqt_ref, be_ref, a_ref, mods_ref, wmain_ref, t_scr, dd_scr):
    _ada_kernel(cond_ref, wada_ref, bada_ref, mods_ref)
    wmain_ref[...] = win_ref[...].astype(BF16)
    for gi in range(S5_PREP_GPB):
        _s5_prep_group(p_ref.at[gi], cc_ref.at[gi], t_ref.at[gi], bq_ref.at[gi], cqt_ref.at[gi], be_ref.at[gi],
                       a_ref.at[gi], t_scr, dd_scr)


def _s5_prep_group(p_ref, cc_ref, t_ref, bq_ref, cqt_ref, be_ref, a_ref, t_scr, dd_scr):
    gch = S5_GROUP_CH
    lre = p_ref[_PREP_LRE:_PREP_LRE + 1]
    lim = p_ref[_PREP_LIM:_PREP_LIM + 1]
    dt = jnp.exp(p_ref[_PREP_LDT:_PREP_LDT + 1])
    a = lre * dt
    th = lim * dt

    def lam_pow(k):
        mag = jnp.exp(k * a)
        return mag * jnp.cos(k * th), mag * jnp.sin(k * th)

    lb_re, lb_im = lam_pow(1.0)
    nr = lb_re - 1.0
    den = lre * lre + lim * lim
    cf_re = (nr * lre + lb_im * lim) / den
    cf_im = (lb_im * lre - nr * lim) / den
    bt_re = p_ref[_PREP_BT_RE:_PREP_BT_RE + gch]
    bt_im = p_ref[_PREP_BT_IM:_PREP_BT_IM + gch]
    bb_re = jnp.tile(cf_re * bt_re - cf_im * bt_im, (S5_Q, 1))
    bb_im = jnp.tile(cf_re * bt_im + cf_im * bt_re, (S5_Q, 1))

    shape = (S5_W, 128)
    pos = lax.shift_right_logical(lax.broadcasted_iota(jnp.int32, shape, 0), 4)
    is_f = lax.broadcasted_iota(jnp.int32, shape, 1) < S5_STATE
    posq = lax.broadcasted_iota(jnp.int32, (S5_Q, 128), 0).astype(F32)
    is_fq = lax.broadcasted_iota(jnp.int32, (S5_Q, 128), 1) < S5_STATE

    def per_channel(tbl):
        return jnp.broadcast_to(tbl[:, None, :], (S5_Q, S5_GROUP_CH, 128)).reshape(shape)

    p_re, p_im = map(per_channel, lam_pow(jnp.where(is_fq, (S5_Q - 1.0) - posq, posq)))
    w_re = p_re * bb_re - p_im * bb_im
    w_im = p_re * bb_im + p_im * bb_re
    bq = jnp.concatenate([w_re, w_im], axis=1)
    bqt = jnp.transpose(bq)
    bq_ref[...] = bqt.astype(BF16)

    edge = pos == jnp.where(is_f, 0, S5_Q - 1)
    be = jnp.concatenate([jnp.where(edge, bb_re, 0.0), jnp.where(edge, bb_im, 0.0)], axis=1)
    be_ref[...] = jnp.transpose(be).astype(BF16)

    q_re, q_im = map(per_channel, lam_pow(jnp.where(is_fq, posq + 1.0, S5_Q - posq)))
    ct_re = jnp.tile(p_ref[_PREP_C_RE:_PREP_C_RE + gch], (S5_Q, 1))
    ct_im = jnp.tile(p_ref[_PREP_C_IM:_PREP_C_IM + gch], (S5_Q, 1))
    g_re = q_re * ct_re - q_im * ct_im
    g_im = q_re * ct_im + q_im * ct_re
    cqt_ref[...] = jnp.concatenate([g_re, -g_im], axis=1).astype(BF16)

    a_re, a_im = lam_pow(float(S5_Q))
    a_ref[...] = jnp.concatenate([a_re, a_im], axis=1)

    kf = jnp.dot(cc_ref[0:gch], bqt, precision=lax.Precision.HIGHEST, preferred_element_type=F32)
    kb = jnp.dot(cc_ref[gch:2 * gch], bqt, precision=lax.Precision.HIGHEST, preferred_element_type=F32)
    lo = S5_W - gch
    dd_scr[:, 0:S5_W] = kf
    dd_scr[:, lo:lo + S5_W] = kb
    dd_scr[:, lo:S5_W] = kf[:, lo:S5_W] + kb[:, 0:gch]
    for t in range(S5_Q):
        c0 = (S5_Q - 1 - t) * gch
        t_scr[t * gch:(t + 1) * gch, :] = dd_scr[:, c0:c0 + S5_W]
    t_ref[...] = t_scr[...].astype(BF16)


def _s5_prep_call(lam_re, lam_im, log_dt, b_re, b_im, c_re, c_im, ada, w_in):
    def fb(p):
        return jnp.transpose(p, (1, 0, 2)).reshape(S5_GROUPS, 1, 2 * S5_STATE)

    def dup(p):
        return jnp.concatenate([p, p], axis=-1)

    ldt = fb(jnp.broadcast_to(log_dt[:, :, None], (2, S5_GROUPS, S5_STATE)))
    pad = jnp.zeros((S5_GROUPS, _PREP_BT_RE - _PREP_LDT - 1, 128), F32)
    packed = jnp.concatenate([fb(lam_re), fb(lam_im), ldt, pad,
                              dup(jnp.transpose(b_re, (0, 2, 1))), dup(jnp.transpose(b_im, (0, 2, 1))),
                              dup(c_re), dup(c_im)], axis=1)
    zero = jnp.zeros_like(c_re)
    cc = jnp.concatenate([jnp.concatenate([c_re, zero, -c_im, zero], axis=-1),
                          jnp.concatenate([zero, c_re, zero, -c_im], axis=-1)], axis=1)

    gpb = S5_PREP_GPB
    nsteps = S5_GROUPS // gpb
    sq = pl.BlockSpec((gpb, S5_W, S5_W), lambda g: (g, 0, 0))
    sq_shape = jax.ShapeDtypeStruct((S5_GROUPS, S5_W, S5_W), BF16)
    cond8, w_ada, b_ada, ada_layer = ada
    ada_in, ada_out = _ada_specs(ada_layer, 6 * D_MODEL // nsteps)
    cast_in, cast_out = _cast_specs(w_in.shape, nsteps)
    *mats, mods, w_main = pl.pallas_call(
        _s5_prep_kernel,
        out_shape=(sq_shape, sq_shape, sq_shape, sq_shape,
                   jax.ShapeDtypeStruct((S5_GROUPS, 1, S5_W), F32), ADA_SHAPE,
                   jax.ShapeDtypeStruct(w_in.shape[1:], BF16)),
        grid=(nsteps,),
        in_specs=[pl.BlockSpec((gpb, _PREP_ROWS, 128), lambda g: (g, 0, 0)),
                  pl.BlockSpec((gpb, 2 * S5_GROUP_CH, S5_W), lambda g: (g, 0, 0))] + ada_in + [cast_in],
        out_specs=(sq, sq, sq, sq, pl.BlockSpec((gpb, 1, S5_W), lambda g: (g, 0, 0)), ada_out, cast_out),
        scratch_shapes=[pltpu.VMEM((S5_W, S5_W), F32), pltpu.VMEM((S5_GROUP_CH, 2 * S5_W), F32)],
        compiler_params=pltpu.CompilerParams(vmem_limit_bytes=VMEM_LIMIT),
        name="s5_prep",
    )(packed, cc, cond8, w_ada, b_ada, w_in)
    return mats, mods, w_main


def _s5_kernel(u_ref, tt_ref, bqt_ref, cqt_ref, bet_ref, a_ref, h0_ref, y_ref, ns_ref,
               ut_scr, x_scr, spf_scr, spb_scr, ne_scr, yt_scr, xt_scr):
    gch = S5_GROUP_CH
    for s in range(S5_Q):
        rows = u_ref[pl.ds(s, S5_ROWS, stride=S5_Q), :]
        rows_t = jnp.transpose(rows).astype(BF16)
        for gl in range(S5_GPB):
            ut_scr[gl, s * gch:(s + 1) * gch, :] = rows_t[gl * gch:(gl + 1) * gch, :]

    for gl in range(S5_GPB):
        ut = ut_scr[gl]
        xt_scr[...] = _dot(bqt_ref[gl], ut)
        x = jnp.transpose(xt_scr[...])
        xt_scr[:, 0:S5_PROMPT_ROWS] = _dot(bet_ref[gl], ut[:, 0:S5_PROMPT_ROWS])
        ne = jnp.transpose(xt_scr[:, 0:S5_PROMPT_ROWS])
        for part in range(2):
            x_scr[part, pl.ds(gl, S5_ROWS, stride=S5_GPB), :] = x[:, part * 128:(part + 1) * 128]
            ne_scr[part, pl.ds(gl, S5_PROMPT_ROWS, stride=S5_GPB), :] = ne[:, part * 128:(part + 1) * 128]

    is_f = lax.broadcasted_iota(jnp.int32, (1, 128), 1) < S5_STATE
    a_re = a_ref[:, 0:128]
    a_im = a_ref[:, 128:256]

    def tile(row):
        return pl.ds(pl.multiple_of(row * S5_GPB, S5_GPB), S5_GPB)

    def scan(base, nseq, nchunk, s_init):
        def body(i, state):
            new = []
            for b in range(nseq):
                s_re, s_im = state[b]
                rows_f = tile(base + b * nchunk + i)
                rows_b = tile(base + b * nchunk + (nchunk - 1 - i))
                spf_scr[0, rows_f, :] = s_re
                spf_scr[1, rows_f, :] = s_im
                spb_scr[0, rows_b, :] = s_re
                spb_scr[1, rows_b, :] = s_im
                x_re = jnp.where(is_f, x_scr[0, rows_f, :], x_scr[0, rows_b, :])
                x_im = jnp.where(is_f, x_scr[1, rows_f, :], x_scr[1, rows_b, :])
                new.append((a_re * s_re - a_im * s_im + x_re, a_re * s_im + a_im * s_re + x_im))
            return tuple(new)

        lax.fori_loop(0, nchunk, body, tuple(s_init))

    zero = jnp.zeros((S5_GPB, 128), F32)
    scan(0, BATCH, S5_PROMPT_CHUNKS, [(zero, zero)] * BATCH)
    scan(S5_PROMPT_ROWS, DEC_BATCH, S5_SAMPLE_CHUNKS,
         [(h0_ref[b, :, 0:128], h0_ref[b, :, 128:256]) for b in range(DEC_BATCH)])

    for b in range(BATCH):
        first = pl.ds(b * S5_PROMPT_CHUNKS * S5_GPB, S5_GPB)
        last = pl.ds(((b + 1) * S5_PROMPT_CHUNKS - 1) * S5_GPB, S5_GPB)
        for part in range(2):
            ns_ref[b, :, part * 128:(part + 1) * 128] = jnp.where(is_f, ne_scr[part, first, :], ne_scr[part, last, :])

    for gl in range(S5_GPB):
        rows = pl.ds(gl, S5_ROWS, stride=S5_GPB)
        carried = jnp.concatenate([jnp.where(is_f, spf_scr[p, rows, :], spb_scr[p, rows, :]) for p in range(2)],
                                  axis=1).astype(BF16)
        yt = _dot(tt_ref[gl], ut_scr[gl]) + lax.dot_general(cqt_ref[gl], carried, NT_DIMS,
                                                            preferred_element_type=F32)
        for t in range(S5_Q):
            yt_scr[t, gl * gch:(gl + 1) * gch, :] = yt[t * gch:(t + 1) * gch, :]
    for t in range(S5_Q):
        y_ref[pl.ds(t, S5_ROWS, stride=S5_Q), :] = jnp.transpose(yt_scr[t])


def _s5_call(u, mats, h0):
    tt_m, bqt_m, cqt_m, bet_m, a_m = mats
    nsteps = S5_GROUPS // S5_GPB
    sq = pl.BlockSpec((S5_GPB, S5_W, S5_W), lambda g: (g, 0, 0))
    state_scr = pltpu.VMEM((2, S5_ROWS * S5_GPB, 128), F32)
    return pl.pallas_call(
        _s5_kernel,
        out_shape=(jax.ShapeDtypeStruct((nsteps, T_TOK, 128), F32),
                   jax.ShapeDtypeStruct((nsteps, BATCH, S5_GPB, S5_W), F32)),
        grid=(nsteps,),
        in_specs=[
            pl.BlockSpec((None, T_TOK, 128), lambda g: (g, 0, 0)),
            sq, sq, sq, sq,
            pl.BlockSpec((S5_GPB, S5_W), lambda g: (g, 0)),
            pl.BlockSpec((None, DEC_BATCH, S5_GPB, S5_W), lambda g: (g, 0, 0, 0)),
        ],
        out_specs=(pl.BlockSpec((None, T_TOK, 128), lambda g: (g, 0, 0)),
                   pl.BlockSpec((None, BATCH, S5_GPB, S5_W), lambda g: (g, 0, 0, 0))),
        scratch_shapes=[pltpu.VMEM((S5_GPB, S5_W, S5_ROWS), BF16), state_scr, state_scr, state_scr,
                        pltpu.VMEM((2, S5_PROMPT_ROWS * S5_GPB, 128), F32),
                        pltpu.VMEM((S5_Q, 128, S5_ROWS), F32), pltpu.VMEM((S5_W, S5_ROWS), F32)],
        compiler_params=pltpu.CompilerParams(vmem_limit_bytes=VMEM_LIMIT),
        name="s5_scan",
    )(u, tt_m, bqt_m, cqt_m, bet_m, a_m.reshape(S5_GROUPS, S5_W), h0)


def _split_bf16(x):
    hi = x.astype(BF16)
    r1 = x - hi.astype(F32)
    mid = r1.astype(BF16)
    lo = (r1 - mid.astype(F32)).astype(BF16)
    return hi, mid, lo


def _cast_specs(shape, nsteps):
    _, rows, cols = shape
    rb = rows // nsteps
    return (pl.BlockSpec((None, rb, cols), lambda *ids: (0, ids[-1], 0)),
            pl.BlockSpec((rb, cols), lambda *ids: (ids[-1], 0)))


def _gla_kernel(*refs, seq_len, nsub, has_s0, has_ada, n_cast):
    rows_refs, gn_ref = refs[:6], refs[6]
    n_ada_in = 7 + has_s0
    n_cast_in = n_ada_in + 3 * has_ada
    n_in = n_cast_in + n_cast
    s0_ref = refs[7] if has_s0 else None
    o_ref, sfin_ref = refs[n_in:n_in + 2]
    n_out = 2 + has_ada + n_cast
    scratch = refs[n_in + n_out:]
    if has_ada:
        _ada_kernel(*refs[n_ada_in:n_cast_in], refs[n_in + 2])
    for src, dst in zip(refs[n_cast_in:n_in], refs[n_in + 2 + has_ada:n_in + n_out]):
        dst[...] = src[...].astype(BF16)
    for j in range(nsub):
        rows = pl.ds(j * seq_len, seq_len)
        _gla_sequence(*[r.at[rows, :] for r in rows_refs], gn_ref, s0_ref.at[j] if has_s0 else None,
                      o_ref.at[rows, :], sfin_ref.at[j], *[s.at[j] for s in scratch], seq_len=seq_len)


def _gla_sequence(q_ref, k_ref, v_ref, gf_ref, gb_ref, r_ref, gn_ref, s0_ref, o_ref, sfin_ref,
                  oi_scr, qd_scr, kv_scr, dec_scr, ss_scr, *, seq_len):
    has_s0 = s0_ref is not None
    nblk = seq_len // GLA_BLK
    nchunk = seq_len // GLA_CHUNK
    cl = GLA_CHUNK
    ti = lax.broadcasted_iota(jnp.int32, (GLA_BLK, GLA_BLK), 0)
    si = lax.broadcasted_iota(jnp.int32, (GLA_BLK, GLA_BLK), 1)
    same = lax.shift_right_logical(ti, 6) == lax.shift_right_logical(si, 6)
    keep = (same & (ti >= si), same & (ti <= si))
    tri = tuple(kp.astype(BF16) for kp in keep)
    lane_head = lax.shift_right_logical(lax.broadcasted_iota(jnp.int32, (cl, GLA_QK), 1), 6)
    zeros_v = jnp.zeros((cl, GLA_DV), BF16)
    heads = [(slice(h * GLA_DK, (h + 1) * GLA_DK), slice(h * GLA_DV, (h + 1) * GLA_DV)) for h in range(GLA_HEADS)]

    for j in range(nblk):
        rows = slice(j * GLA_BLK, (j + 1) * GLA_BLK)
        q = q_ref[rows, :] * (GLA_DK ** -0.5)
        k = k_ref[rows, :]
        v = v_ref[rows, :].astype(BF16)
        qd, kd, k2t = [], [], []
        for d, g_ref in enumerate((gf_ref, gb_ref)):
            b = sum(_dot(tri[d], part) for part in _split_bf16(g_ref[rows, :]))
            last = cl - 1 if d == 0 else 0
            b_last = [b[c * cl + last:c * cl + last + 1] for c in range(GLA_CPB)]
            bl = jnp.concatenate([jnp.broadcast_to(x, (cl, GLA_QK)) for x in b_last], axis=0)
            qd_d = (q * jnp.exp(b)).astype(BF16)
            qd_scr[d, rows, :] = qd_d
            qd.append(qd_d)
            kd.append((k * jnp.exp(-b)).astype(BF16))
            k2t.append(jnp.transpose(k * jnp.exp(bl - b)).astype(BF16))
            for c in range(GLA_CPB):
                dec_scr[d, j * GLA_CPB + c] = jnp.exp(jnp.transpose(jnp.broadcast_to(b_last[c], (GLA_DV, GLA_QK))))
        for h, (ks, vs) in enumerate(heads):
            att = [jnp.where(keep[d], lax.dot_general(qd[d][:, ks], kd[d][:, ks], NT_DIMS,
                                                      preferred_element_type=F32), 0.0) for d in range(2)]
            oi_scr[rows, vs] = _dot((att[0] + att[1]).astype(BF16), v[:, vs])
            vh = v[:, vs]
            vexp = jnp.concatenate(
                [jnp.concatenate([vh[c * cl:(c + 1) * cl] if c2 == c else zeros_v for c2 in range(GLA_CPB)], axis=1)
                 for c in range(GLA_CPB)], axis=0)
            for d in range(2):
                kv_scr[d, j, h] = _dot(k2t[d][ks, :], vexp)

    for d in range(2):
        s = s0_ref[d] if has_s0 else jnp.zeros((GLA_QK, GLA_DV), F32)
        for cg in (range(nchunk) if d == 0 else range(nchunk - 1, -1, -1)):
            j, c = divmod(cg, GLA_CPB)
            ss_scr[d, cg] = s.astype(BF16)
            kv = jnp.concatenate([kv_scr[d, j, h, :, c * GLA_DV:(c + 1) * GLA_DV] for h in range(GLA_HEADS)], axis=0)
            s = s * dec_scr[d, cg] + kv
        sfin_ref[d] = s

    for cg in range(nchunk):
        rows = slice(cg * cl, (cg + 1) * cl)
        inter = []
        for d in range(2):
            qc = qd_scr[d, rows, :]
            qstack = jnp.concatenate([jnp.where(lane_head == h, qc, jnp.zeros_like(qc)) for h in range(GLA_HEADS)],
                                     axis=0)
            inter.append(_dot(qstack, ss_scr[d, cg]))
        gate = jax.nn.silu(r_ref[rows, :])
        for h, (ks, vs) in enumerate(heads):
            hr = slice(h * cl, (h + 1) * cl)
            oh = oi_scr[rows, vs] + inter[0][hr] + inter[1][hr]
            oh = oh * lax.rsqrt(jnp.mean(oh * oh, axis=-1, keepdims=True) + EPS) * gn_ref[...]
            o_ref[rows, vs] = oh * gate[:, vs]


def _gla_call(z, g, gla_norm, s0, seq_len, nseq, row0, nsub, ada=None, casts=()):
    blk = nsub * seq_len
    assert row0 % blk == 0 and nseq % nsub == 0
    r0 = row0 // blk
    has_s0 = s0 is not None
    qk_off = 0
    v_off = 2 * GLA_QK // GLA_VW
    in_specs = [
        pl.BlockSpec((blk, GLA_QK), lambda i: (r0 + i, qk_off)),
        pl.BlockSpec((blk, GLA_QK), lambda i: (r0 + i, qk_off + 1)),
        pl.BlockSpec((blk, GLA_VW), lambda i: (r0 + i, v_off)),
        pl.BlockSpec((blk, GLA_QK), lambda i: (r0 + i, 0)),
        pl.BlockSpec((blk, GLA_QK), lambda i: (r0 + i, 1)),
        pl.BlockSpec((blk, GLA_VW), lambda i: (r0 + i, v_off + 1)),
        pl.BlockSpec((1, GLA_DV), lambda i: (0, 0)),
    ]
    args = [z, z, z, g, g, z, gla_norm]
    state_spec = pl.BlockSpec((nsub, 2, GLA_QK, GLA_DV), lambda i: (i, 0, 0, 0))
    if has_s0:
        in_specs.append(state_spec)
        args.append(s0)
    out_shape = [jax.ShapeDtypeStruct((nseq * seq_len, GLA_VW), F32),
                 jax.ShapeDtypeStruct((nseq, 2, GLA_QK, GLA_DV), F32)]
    out_specs = [pl.BlockSpec((blk, GLA_VW), lambda i: (i, 0)), state_spec]
    nsteps = nseq // nsub
    if ada is not None:
        cond8, w_ada, b_ada, ada_layer = ada
        ada_in, ada_out = _ada_specs(ada_layer, 6 * D_MODEL // nsteps)
        in_specs += ada_in
        args += [cond8, w_ada, b_ada]
        out_shape.append(ADA_SHAPE)
        out_specs.append(ada_out)
    for w in casts:
        cast_in, cast_out = _cast_specs(w.shape, nsteps)
        in_specs.append(cast_in)
        args.append(w)
        out_shape.append(jax.ShapeDtypeStruct(w.shape[1:], BF16))
        out_specs.append(cast_out)
    return pl.pallas_call(
        functools.partial(_gla_kernel, seq_len=seq_len, nsub=nsub, has_s0=has_s0, has_ada=ada is not None,
                          n_cast=len(casts)),
        out_shape=tuple(out_shape),
        grid=(nsteps,),
        in_specs=in_specs,
        out_specs=tuple(out_specs),
        scratch_shapes=[
            pltpu.VMEM((nsub, seq_len, GLA_VW), F32),
            pltpu.VMEM((nsub, 2, seq_len, GLA_QK), BF16),
            pltpu.VMEM((nsub, 2, seq_len // GLA_BLK, GLA_HEADS, GLA_DK, GLA_CPB * GLA_DV), F32),
            pltpu.VMEM((nsub, 2, seq_len // GLA_CHUNK, GLA_QK, GLA_DV), F32),
            pltpu.VMEM((nsub, 2, seq_len // GLA_CHUNK, GLA_QK, GLA_DV), BF16),
        ],
        compiler_params=pltpu.CompilerParams(vmem_limit_bytes=VMEM_LIMIT),
        name=f"gla_len{seq_len}",
    )(*args)


MLP_CHUNK = 512
MLP_LOAD = 256
MLP_SLOTS = 4


class _MlpWeights:
    def __init__(self, w1_hbm, w2_hbm, w1_scr, w2_scr, stage1, stage2, sem, layer):
        self.refs = (w1_hbm, w2_hbm, w1_scr, w2_scr, stage1, stage2, sem)
        self.layer = layer

    def _copies(self, p):
        w1_hbm, w2_hbm, _, _, stage1, stage2, sem = self.refs
        cols = pl.ds(p * MLP_LOAD, MLP_LOAD)
        slot = p % MLP_SLOTS
        return (pltpu.make_async_copy(w1_hbm.at[self.layer, :, cols], stage1.at[slot], sem.at[0, slot]),
                pltpu.make_async_copy(w2_hbm.at[self.layer, cols, :], stage2.at[slot], sem.at[1, slot]))

    def start(self, p):
        for cp in self._copies(p):
            cp.start()

    def prefetch(self):
        for p in range(MLP_SLOTS - 1):
            self.start(p)

    def finish(self, p):
        _, _, w1_scr, w2_scr, stage1, stage2, _ = self.refs
        ahead = p + MLP_SLOTS - 1
        if ahead < D_FF // MLP_LOAD:
            self.start(ahead)
        for cp in self._copies(p):
            cp.wait()
        cols = slice(p * MLP_LOAD, (p + 1) * MLP_LOAD)
        w1_scr[:, cols] = stage1[p % MLP_SLOTS].astype(BF16)
        w2_scr[cols, :] = stage2[p % MLP_SLOTS].astype(BF16)


def _mlp_tail(x, mix, m_ref, gn2_ref, w1_ref, w2_ref, loading=None):
    y1 = x + m_ref[:, 2 * D_MODEL:3 * D_MODEL] * mix
    h = _norm_mod(y1, gn2_ref[...], m_ref[:, 3 * D_MODEL:4 * D_MODEL], m_ref[:, 4 * D_MODEL:5 * D_MODEL]).astype(BF16)
    nchunk = D_FF // MLP_CHUNK
    acc = jnp.zeros(y1.shape, F32)
    for c in range(nchunk):
        cols = slice(c * MLP_CHUNK, (c + 1) * MLP_CHUNK)
        if loading is not None:
            per = MLP_CHUNK // MLP_LOAD
            for p in range(c * per, (c + 1) * per):
                loading.finish(p)
        a = _dot(h, w1_ref[:, cols])
        a = jnp.square(jnp.maximum(a, 0.0)).astype(BF16)
        acc = acc + _dot(a, w2_ref[cols, :])
    return y1 + m_ref[:, 5 * D_MODEL:6 * D_MODEL] * acc


def _run_tail(x, mix, m_ref, gn2_ref, weights, w1_ref, w2_ref, emit):
    first = pl.program_id(0) == 0

    @pl.when(first)
    def _():
        emit(_mlp_tail(x, mix, m_ref, gn2_ref, w1_ref, w2_ref, loading=weights))

    @pl.when(jnp.logical_not(first))
    def _():
        emit(_mlp_tail(x, mix, m_ref, gn2_ref, w1_ref, w2_ref))


def _even_out_kernel(xp_ref, xs_ref, y5_ref, u_ref, dskip_ref, wglu_ref, bglu_ref, glap_ref, glas_ref, wout_ref,
                     m_ref, gn2_ref, w1_hbm, w2_hbm, o_ref, w1_ref, w2_ref, stage1, stage2, sem, *, layer):
    weights = _MlpWeights(w1_hbm, w2_hbm, w1_ref, w2_ref, stage1, stage2, sem, layer)

    @pl.when(pl.program_id(0) == 0)
    def _():
        weights.prefetch()

    nblk = S5_WIDTH // 128
    ys = (jnp.concatenate([y5_ref[b] for b in range(nblk)], axis=1)
          + jnp.concatenate([u_ref[b] for b in range(nblk)], axis=1) * dskip_ref[...])
    gl = jax.nn.gelu(ys)
    s5o = gl * jax.nn.sigmoid(_dot(gl.astype(BF16), wglu_ref[...]) + bglu_ref[...])
    gla = _token_tile(glap_ref, glas_ref, _OUT_TM).astype(BF16)
    mix = _dot(s5o.astype(BF16), wout_ref[0:S5_WIDTH, :]) + _dot(gla, wout_ref[S5_WIDTH:, :])

    def emit(y):
        o_ref[...] = y

    _run_tail(_token_tile(xp_ref, xs_ref, _OUT_TM), mix, m_ref, gn2_ref, weights, w1_ref, w2_ref, emit)


def _odd_out_kernel(x_ref, attp_ref, atts_ref, wo_ref, m_ref, gn2_ref, w1_hbm, w2_hbm, op_ref, os_ref,
                    w1_ref, w2_ref, stage1, stage2, sem, *, layer):
    weights = _MlpWeights(w1_hbm, w2_hbm, w1_ref, w2_ref, stage1, stage2, sem, layer)

    @pl.when(pl.program_id(0) == 0)
    def _():
        weights.prefetch()

    mix = _dot(_token_tile(attp_ref, atts_ref, _OUT_TM), wo_ref[...])
    is_prompt = pl.program_id(0) < T_PROMPT // _OUT_TM

    def emit(y):
        @pl.when(is_prompt)
        def _():
            op_ref[...] = y

        @pl.when(jnp.logical_not(is_prompt))
        def _():
            os_ref[...] = y

    _run_tail(x_ref[...], mix, m_ref, gn2_ref, weights, w1_ref, w2_ref, emit)


_OUT_TM = 512


def _const_spec(shape):
    return pl.BlockSpec(shape, lambda i: (0,) * len(shape), pipeline_mode=pl.Buffered(1))


def _tail_specs(layer):
    tm = _OUT_TM
    return [
        pl.BlockSpec((None, 1, 6 * D_MODEL), lambda i: (_cond_row(i, tm), 0, 0)),
        _const_spec((1, D_MODEL)),
        pl.BlockSpec(memory_space=pl.ANY),
        pl.BlockSpec(memory_space=pl.ANY),
    ]


def _tail_scratch():
    return [pltpu.VMEM((D_MODEL, D_FF), BF16), pltpu.VMEM((D_FF, D_MODEL), BF16),
            pltpu.VMEM((MLP_SLOTS, D_MODEL, MLP_LOAD), F32), pltpu.VMEM((MLP_SLOTS, MLP_LOAD, D_MODEL), F32),
            pltpu.SemaphoreType.DMA((2, MLP_SLOTS))]


_TAIL_PARAMS = dict(dimension_semantics=("arbitrary",), vmem_limit_bytes=VMEM_LIMIT_TAIL)


def _even_out_call(xp, xs, y5, u, d_skip, w_glu, b_glu, gla_p, gla_s, w_out, mods, layer, gn2, w1, w2):
    tm = _OUT_TM
    return pl.pallas_call(
        functools.partial(_even_out_kernel, layer=layer),
        out_shape=jax.ShapeDtypeStruct((T_TOK, D_MODEL), F32),
        grid=(T_TOK // tm,),
        in_specs=_token_specs(tm) + [
            pl.BlockSpec((S5_WIDTH // 128, tm, 128), lambda i: (0, i, 0)),
            pl.BlockSpec((S5_WIDTH // 128, tm, 128), lambda i: (0, i, 0)),
            _const_spec((1, S5_WIDTH)),
            _const_spec((S5_WIDTH, S5_WIDTH)),
            _const_spec((1, S5_WIDTH)),
        ] + _token_specs(tm, GLA_VW) + [
            _const_spec((S5_WIDTH + GLA_VW, D_MODEL)),
        ] + _tail_specs(layer),
        out_specs=pl.BlockSpec((tm, D_MODEL), lambda i: (i, 0)),
        scratch_shapes=_tail_scratch(),
        compiler_params=pltpu.CompilerParams(**_TAIL_PARAMS),
        name="even_out_mlp",
    )(xp, xs, y5, u, d_skip, w_glu, b_glu, gla_p, gla_s, w_out, mods, gn2, w1, w2)


def _odd_out_call(x, att_p, att_s, w_o, mods, layer, gn2, w1, w2):
    tm = _OUT_TM
    return pl.pallas_call(
        functools.partial(_odd_out_kernel, layer=layer),
        out_shape=(jax.ShapeDtypeStruct((T_PROMPT, D_MODEL), F32),
                   jax.ShapeDtypeStruct((T_SAMPLE, D_MODEL), F32)),
        grid=(T_TOK // tm,),
        in_specs=[pl.BlockSpec((tm, D_MODEL), lambda i: (i, 0))] + _token_specs(tm) + [
            _const_spec((D_MODEL, D_MODEL)),
        ] + _tail_specs(layer),
        out_specs=tuple(_token_specs(tm)),
        scratch_shapes=_tail_scratch(),
        compiler_params=pltpu.CompilerParams(**_TAIL_PARAMS),
        name="odd_out_mlp",
    )(x, att_p, att_s, w_o, mods, gn2, w1, w2)


def _qkv_kernel(x_ref, gn_ref, m_ref, w_ref, qn_ref, kn_ref, cos_ref, sin_ref,
                q_ref, kb_ref, vb_ref, k32_ref, v32_ref, *, tile):
    h = _norm_mod(x_ref[...], gn_ref[...], m_ref[:, 0:D_MODEL], m_ref[:, D_MODEL:2 * D_MODEL]).astype(BF16)
    z = _dot(h, w_ref[...])
    v = z[:, (N_HEADS + KV_HEADS) * HEAD_DIM:]
    vb_ref[...] = v.astype(BF16)
    even_lane = (lax.broadcasted_iota(jnp.int32, (1, HEAD_DIM), 1) & 1) == 0

    def heads(rope):
        for hd in range(N_HEADS + KV_HEADS):
            xh = z[:, hd * HEAD_DIM:(hd + 1) * HEAD_DIM]
            gain = qn_ref[...] if hd < N_HEADS else kn_ref[...]
            xh = xh * lax.rsqrt(jnp.mean(xh * xh, axis=-1, keepdims=True) + EPS) * gain
            if rope:
                partner = jnp.where(even_lane, pltpu.roll(xh, HEAD_DIM - 1, 1), pltpu.roll(xh, 1, 1))
                xh = xh * cos_ref[...] + partner * sin_ref[...]
            if hd < N_HEADS:
                q_ref[:, hd * HEAD_DIM:(hd + 1) * HEAD_DIM] = xh.astype(BF16)
            else:
                cols = slice((hd - N_HEADS) * HEAD_DIM, (hd - N_HEADS + 1) * HEAD_DIM)
                kb_ref[:, cols] = xh.astype(BF16)
                if not rope:
                    k32_ref[:, hd - N_HEADS, :] = xh

    is_sample = pl.program_id(0) >= T_PROMPT // tile

    @pl.when(is_sample)
    def _():
        heads(True)

    @pl.when(jnp.logical_not(is_sample))
    def _():
        heads(False)
        for kh in range(KV_HEADS):
            v32_ref[:, kh, :] = v[:, kh * HEAD_DIM:(kh + 1) * HEAD_DIM]


def _qkv_call(x, gn, mods, layer, w_qkv, q_norm, k_norm, cos_t, sin_t):
    tm = 512
    pos_tiles = DEC_SEQ // tm
    n_prompt = T_PROMPT // tm
    kvw = KV_HEADS * HEAD_DIM

    def pos_map(i):
        return (jnp.maximum(i - n_prompt, 0) % pos_tiles, 0)

    def prompt_map(i):
        return (jnp.minimum(i, n_prompt - 1), 0, 0)

    return pl.pallas_call(
        functools.partial(_qkv_kernel, tile=tm),
        out_shape=(jax.ShapeDtypeStruct((T_TOK, N_HEADS * HEAD_DIM), BF16),
                   jax.ShapeDtypeStruct((T_TOK, kvw), BF16),
                   jax.ShapeDtypeStruct((T_TOK, kvw), BF16),
                   jax.ShapeDtypeStruct((T_PROMPT, KV_HEADS, HEAD_DIM), F32),
                   jax.ShapeDtypeStruct((T_PROMPT, KV_HEADS, HEAD_DIM), F32)),
        grid=(T_TOK // tm,),
        in_specs=[
            pl.BlockSpec((tm, D_MODEL), lambda i: (i, 0)),
            pl.BlockSpec((1, D_MODEL), lambda i: (0, 0)),
            pl.BlockSpec((None, 1, 6 * D_MODEL), lambda i: (_cond_row(i, tm), 0, 0)),
            pl.BlockSpec(w_qkv.shape, lambda i: (0, 0)),
            pl.BlockSpec((1, HEAD_DIM), lambda i: (0, 0)),
            pl.BlockSpec((1, HEAD_DIM), lambda i: (0, 0)),
            pl.BlockSpec((tm, HEAD_DIM), pos_map),
            pl.BlockSpec((tm, HEAD_DIM), pos_map),
        ],
        out_specs=(pl.BlockSpec((tm, N_HEADS * HEAD_DIM), lambda i: (i, 0)),
                   pl.BlockSpec((tm, kvw), lambda i: (i, 0)),
                   pl.BlockSpec((tm, kvw), lambda i: (i, 0)),
                   pl.BlockSpec((tm, KV_HEADS, HEAD_DIM), prompt_map),
                   pl.BlockSpec((tm, KV_HEADS, HEAD_DIM), prompt_map)),
        compiler_params=pltpu.CompilerParams(vmem_limit_bytes=VMEM_LIMIT),
        name="odd_qkv",
    )(x, gn, mods, w_qkv, q_norm, k_norm, cos_t, sin_t)


def _rope_tables():
    f32 = np.float32
    rows = DEC_SEQ // GRID_W
    row = np.repeat(np.arange(rows, dtype=f32), GRID_W)
    col = np.tile(np.arange(GRID_W, dtype=f32), rows)
    inv = np.power(f32(ROPE_THETA), -np.arange(0, AXIS_DIM, 2, dtype=f32) / f32(AXIS_DIM)).astype(f32)
    ang = np.concatenate([row[:, None] * inv, col[:, None] * inv], axis=-1).astype(f32)
    cos_t = np.repeat(np.cos(ang), 2, axis=-1).astype(f32)
    sin = np.sin(ang).astype(f32)
    sin_t = np.stack([-sin, sin], axis=-1).reshape(DEC_SEQ, HEAD_DIM)
    return jnp.asarray(cos_t), jnp.asarray(sin_t)


def _attn_kernel(*refs, has_cache):
    q_ref, k_ref, v_ref = refs[:3]
    ck_ref, cv_ref = refs[3:5] if has_cache else (None, None)
    o_ref = refs[-1]
    c = HEAD_DIM ** -0.5 * math.log2(math.e)
    ones_col = (lax.broadcasted_iota(jnp.int32, (1, HEAD_DIM), 1) == 0).astype(BF16)

    def with_ones(v):
        return jnp.concatenate([v, jnp.broadcast_to(ones_col, v.shape)], axis=1)

    for g in range(k_ref.shape[1] // HEAD_DIM):
        kv_cols = slice(g * HEAD_DIM, (g + 1) * HEAD_DIM)
        k = k_ref[:, kv_cols]
        v = with_ones(v_ref[:, kv_cols])
        if has_cache:
            ck = ck_ref[:, kv_cols].astype(BF16)
            cv = with_ones(cv_ref[:, kv_cols].astype(BF16))
        for r in range(Q_PER_KV):
            cs = slice((g * Q_PER_KV + r) * HEAD_DIM, (g * Q_PER_KV + r + 1) * HEAD_DIM)
            q = q_ref[:, cs]
            s = lax.dot_general(q, k, NT_DIMS, preferred_element_type=F32)
            m = jnp.max(s, axis=-1, keepdims=True)
            if has_cache:
                sc = lax.dot_general(q, ck, NT_DIMS, preferred_element_type=F32)
                m = jnp.maximum(m, jnp.max(sc, axis=-1, keepdims=True))
            mc = m * c
            o = _dot(jnp.exp2(s * c - mc).astype(BF16), v)
            if has_cache:
                o = o + _dot(jnp.exp2(sc * c - mc).astype(BF16), cv)
            o_ref[:, cs] = (o[:, 0:HEAD_DIM] / o[:, HEAD_DIM:HEAD_DIM + 1]).astype(BF16)


def _attn_call(q, k, v, cache_k, cache_v, seq_len, row0, nrows, kv_per_step):
    assert row0 % seq_len == 0 and nrows % seq_len == 0 and KV_HEADS % kv_per_step == 0
    has_cache = cache_k is not None
    b0 = row0 // seq_len
    qw = kv_per_step * Q_PER_KV * HEAD_DIM
    kw = kv_per_step * HEAD_DIM
    in_specs = [
        pl.BlockSpec((seq_len, qw), lambda b, g: (b0 + b, g)),
        pl.BlockSpec((seq_len, kw), lambda b, g: (b0 + b, g)),
        pl.BlockSpec((seq_len, kw), lambda b, g: (b0 + b, g)),
    ]
    args = [q, k, v]
    if has_cache:
        in_specs += [pl.BlockSpec((PAST_LEN, kw), lambda b, g: (b, g)),
                     pl.BlockSpec((PAST_LEN, kw), lambda b, g: (b, g))]
        args += [cache_k, cache_v]
    return pl.pallas_call(
        functools.partial(_attn_kernel, has_cache=has_cache),
        out_shape=jax.ShapeDtypeStruct((nrows, N_HEADS * HEAD_DIM), BF16),
        grid=(nrows // seq_len, KV_HEADS // kv_per_step),
        in_specs=in_specs,
        out_specs=pl.BlockSpec((seq_len, qw), lambda b, g: (b, g)),
        compiler_params=pltpu.CompilerParams(vmem_limit_bytes=VMEM_LIMIT),
        name=f"attn_len{seq_len}",
    )(*args)


def kernel(x_prompt, x_sample, state_s5_re, state_s5_im, state_gla, cache_k, cache_v, c, c_ctx, norm_mix, norm_mlp, w_ada, b_ada, w_mlp_in, w_mlp_out, w_in_e, w_out_e, s5_lambda_re, s5_lambda_im, s5_log_dt, s5_b_re, s5_b_im, s5_c_re, s5_c_im, s5_d, s5_w_glu, s5_b_glu, gla_w_gate2, gla_b_gate, gla_norm, w_qkv_o, w_o_o, q_norm, k_norm):
    xp = x_prompt.reshape(T_PROMPT, D_MODEL)
    xs = x_sample.reshape(T_SAMPLE, D_MODEL)
    cond8 = jnp.concatenate([c_ctx[None, :], c, jnp.zeros((COND_ROWS - 1 - DEC_BATCH, D_MODEL), F32)], axis=0)
    b_ada3 = b_ada.reshape(DEPTH, 1, 6 * D_MODEL)
    w1_all, w2_all = w_mlp_in, w_mlp_out
    n_main = S5_WIDTH + 2 * GLA_QK + 2 * GLA_VW
    mats, mods0, w_main = _s5_prep_call(s5_lambda_re[0], s5_lambda_im[0], s5_log_dt[0], s5_b_re[0], s5_b_im[0],
                                        s5_c_re[0], s5_c_im[0], ada=(cond8, w_ada, b_ada3, 0),
                                        w_in=w_in_e[:, :, :n_main])

    w_glr = jnp.pad(w_in_e[0, :, n_main:], ((0, 0), (0, 128 - 2 * GLA_RANK))).astype(BF16)
    zg = jnp.zeros((GLA_RANK, GLA_QK), F32)
    w_gate = jnp.concatenate([jnp.concatenate([gla_w_gate2[0, 0], zg], axis=1),
                              jnp.concatenate([zg, gla_w_gate2[0, 1]], axis=1),
                              jnp.zeros((128 - 2 * GLA_RANK, 2 * GLA_QK), F32)], axis=0).astype(BF16)
    b_gate = gla_b_gate[0].reshape(1, 2 * GLA_QK)
    u, z, g = _inproj_call(xp, xs, norm_mix[0:1], mods0, 0, w_main, w_glr, w_gate, b_gate)

    def state_rows(s):
        return jnp.transpose(s, (2, 0, 1, 3)).reshape(S5_GROUPS, DEC_BATCH, 2 * S5_STATE)

    h0 = jnp.concatenate([state_rows(state_s5_re[:, 0]), state_rows(state_s5_im[:, 0])], axis=-1)
    nsteps = S5_GROUPS // S5_GPB
    h0 = jnp.transpose(h0.reshape(nsteps, S5_GPB, DEC_BATCH, S5_W), (0, 2, 1, 3))
    y5, ns = _s5_call(u, mats, h0)
    ns = jnp.transpose(ns, (0, 2, 1, 3)).reshape(S5_GROUPS, BATCH, S5_W)

    def state_out(n):
        return jnp.transpose(n.reshape(S5_GROUPS, BATCH, 2, S5_STATE), (1, 2, 0, 3))[:, None]

    new_s5_re = state_out(ns[:, :, :2 * S5_STATE])
    new_s5_im = state_out(ns[:, :, 2 * S5_STATE:])

    gn_gla = gla_norm[0].reshape(1, GLA_DV)
    gla_p, sfin, w_glu, w_out, w_qkv, w_o = _gla_call(z, g, gn_gla, None, SEQ, BATCH, 0, nsub=4,
                                                      casts=(s5_w_glu, w_out_e, w_qkv_o, w_o_o))
    s0 = state_gla[:, 0].reshape(DEC_BATCH, 2, GLA_QK, GLA_DV)
    gla_s, _, mods1 = _gla_call(z, g, gn_gla, s0, DEC_SEQ, DEC_BATCH, T_PROMPT, nsub=1,
                                ada=(cond8, w_ada, b_ada3, 1))
    new_gla = sfin.reshape(BATCH, 1, 2, GLA_HEADS, GLA_DK, GLA_DV)

    x = _even_out_call(xp, xs, y5, u, s5_d[0].reshape(1, S5_WIDTH), w_glu,
                       s5_b_glu[0].reshape(1, S5_WIDTH), gla_p, gla_s, w_out, mods0, 0,
                       norm_mlp[0:1], w1_all, w2_all)

    cos_t, sin_t = _rope_tables()
    q, k, v, k32, v32 = _qkv_call(x, norm_mix[1:2], mods1, 1, w_qkv,
                                  q_norm[0].reshape(1, HEAD_DIM), k_norm[0].reshape(1, HEAD_DIM), cos_t, sin_t)
    att_p = _attn_call(q, k, v, None, None, SEQ, 0, T_PROMPT, kv_per_step=1)
    ck = cache_k[:, 0].reshape(DEC_BATCH * PAST_LEN, KV_HEADS * HEAD_DIM)
    cv = cache_v[:, 0].reshape(DEC_BATCH * PAST_LEN, KV_HEADS * HEAD_DIM)
    att_s = _attn_call(q, k, v, ck, cv, DEC_SEQ, T_PROMPT, T_SAMPLE, kv_per_step=1)
    yp, ys = _odd_out_call(x, att_p, att_s, w_o, mods1, 1, norm_mlp[1:2],
                           w1_all, w2_all)

    new_k = k32.reshape(BATCH, 1, SEQ, KV_HEADS, HEAD_DIM)
    new_v = v32.reshape(BATCH, 1, SEQ, KV_HEADS, HEAD_DIM)
    y_prompt = yp.reshape(BATCH, SEQ, D_MODEL)
    y_sample = ys.reshape(DEC_BATCH, DEC_SEQ, D_MODEL)
    return (y_prompt, y_sample, new_s5_re, new_s5_im, new_gla, new_k, new_v)
```

```python
import functools
import math

import jax
import jax.numpy as jnp
import numpy as np
from jax import lax
from jax.experimental import pallas as pl
from jax.experimental.pallas import tpu as pltpu

F32 = jnp.float32
BF16 = jnp.bfloat16

D_MODEL = 1024
BATCH = 16
SEQ = 256
DEPTH = 2
DEC_BATCH = 4
DEC_SEQ = 1024
PAST_LEN = 512
GRID_W = 64
S5_WIDTH = 512
S5_GROUP_CH = 16
S5_GROUPS = 32
S5_STATE = 64
GLA_HEADS = 4
GLA_VW = 512
GLA_DV = 128
GLA_DK = 64
GLA_QK = 256
GLA_RANK = 16
GLA_TAU = 16.0
GLA_CHUNK = 64
GLA_CPB = 4
GLA_BLK = GLA_CPB * GLA_CHUNK
HEAD_DIM = 128
N_HEADS = 8
KV_HEADS = 2
Q_PER_KV = N_HEADS // KV_HEADS
AXIS_DIM = 64
ROPE_THETA = 10000.0
D_FF = 4096
EPS = 1e-6

T_PROMPT = BATCH * SEQ
T_SAMPLE = DEC_BATCH * DEC_SEQ
T_TOK = T_PROMPT + T_SAMPLE
COND_ROWS = 8
COND_SPAN = 1024
PROMPT_SPANS = T_PROMPT // COND_SPAN

S5_Q = 16
S5_W = S5_Q * S5_GROUP_CH
S5_GPB = 128 // S5_GROUP_CH
S5_ROWS = T_TOK // S5_Q
S5_PROMPT_ROWS = T_PROMPT // S5_Q
S5_PROMPT_CHUNKS = SEQ // S5_Q
S5_SAMPLE_CHUNKS = DEC_SEQ // S5_Q

VMEM_LIMIT = 56 * 1024 * 1024
VMEM_LIMIT_TAIL = 60 * 1024 * 1024

NT_DIMS = (((1,), (1,)), ((), ()))
TN_DIMS = (((0,), (0,)), ((), ()))


def _cond_row(i, tile):
    return jnp.maximum((i * tile) // COND_SPAN - (PROMPT_SPANS - 1), 0)


def _norm_mod(x, gain, shift, scale):
    y = x * lax.rsqrt(jnp.mean(x * x, axis=-1, keepdims=True) + EPS)
    return (y * gain) * (1.0 + scale) + shift


def _dot(a, b):
    return jnp.dot(a, b, preferred_element_type=F32)


def _ada_kernel(cond_ref, w_ref, b_ref, o_ref):
    s = jax.nn.silu(cond_ref[...]).astype(BF16)
    o_ref[:, 0, :] = _dot(s, w_ref[...].astype(BF16)) + b_ref[...]


ADA_SHAPE = jax.ShapeDtypeStruct((COND_ROWS, 1, 6 * D_MODEL), F32)


def _ada_specs(layer, tn):
    return ([pl.BlockSpec((COND_ROWS, D_MODEL), lambda *ids: (0, 0)),
             pl.BlockSpec((None, D_MODEL, tn), lambda *ids: (layer, 0, ids[-1])),
             pl.BlockSpec((None, 1, tn), lambda *ids: (layer, 0, ids[-1]))],
            pl.BlockSpec((COND_ROWS, 1, tn), lambda *ids: (0, 0, ids[-1])))


def _token_specs(tile, width=D_MODEL):
    n_prompt = T_PROMPT // tile
    return [pl.BlockSpec((tile, width), lambda i: (jnp.minimum(i, n_prompt - 1), 0)),
            pl.BlockSpec((tile, width), lambda i: (jnp.maximum(i - n_prompt, 0), 0))]


def _token_tile(xp_ref, xs_ref, tile):
    return jnp.where(pl.program_id(0) < T_PROMPT // tile, xp_ref[...], xs_ref[...])


def _inproj_kernel(xp_ref, xs_ref, gn_ref, m_ref, w_ref, wglr_ref, wg_ref, bg_ref, u_ref, z_ref, g_ref, *, tile):
    x = _token_tile(xp_ref, xs_ref, tile)
    h = _norm_mod(x, gn_ref[...], m_ref[:, 0:D_MODEL], m_ref[:, D_MODEL:2 * D_MODEL]).astype(BF16)
    z = _dot(h, w_ref[...])
    for blk in range(S5_WIDTH // 128):
        u_ref[blk] = z[:, blk * 128:(blk + 1) * 128]
    z_ref[...] = z[:, S5_WIDTH:]
    glr = _dot(h, wglr_ref[...]).astype(BF16)
    pre = _dot(glr, wg_ref[...]) + bg_ref[...]
    g_ref[...] = jax.nn.log_sigmoid(pre) * (1.0 / GLA_TAU)


def _inproj_call(xp, xs, gn, mods, layer, w_main, w_glr, w_gate, b_gate):
    tm = 512
    nz = w_main.shape[1]
    return pl.pallas_call(
        functools.partial(_inproj_kernel, tile=tm),
        out_shape=(jax.ShapeDtypeStruct((S5_WIDTH // 128, T_TOK, 128), F32),
                   jax.ShapeDtypeStruct((T_TOK, nz - S5_WIDTH), F32),
                   jax.ShapeDtypeStruct((T_TOK, 2 * GLA_QK), F32)),
        grid=(T_TOK // tm,),
        in_specs=_token_specs(tm) + [
            pl.BlockSpec((1, D_MODEL), lambda i: (0, 0)),
            pl.BlockSpec((None, 1, 6 * D_MODEL), lambda i: (_cond_row(i, tm), 0, 0)),
            pl.BlockSpec((D_MODEL, nz), lambda i: (0, 0)),
            pl.BlockSpec((D_MODEL, 128), lambda i: (0, 0)),
            pl.BlockSpec((128, 2 * GLA_QK), lambda i: (0, 0)),
            pl.BlockSpec((1, 2 * GLA_QK), lambda i: (0, 0)),
        ],
        out_specs=(pl.BlockSpec((S5_WIDTH // 128, tm, 128), lambda i: (0, i, 0)),
                   pl.BlockSpec((tm, nz - S5_WIDTH), lambda i: (i, 0)),
                   pl.BlockSpec((tm, 2 * GLA_QK), lambda i: (i, 0))),
        compiler_params=pltpu.CompilerParams(vmem_limit_bytes=VMEM_LIMIT),
        name="even_inproj",
    )(xp, xs, gn, mods, w_main, w_glr, w_gate, b_gate)


S5_PREP_GPB = 8
_PREP_LRE, _PREP_LIM, _PREP_LDT = 0, 1, 2
_PREP_BT_RE, _PREP_BT_IM, _PREP_C_RE, _PREP_C_IM, _PREP_ROWS = 8, 24, 40, 56, 72


def _s5_prep_kernel(p_ref, cc_ref, cond_ref, wada_ref, bada_ref, t_ref, bq_ref, cqt_ref, be_ref, a_ref, mods_ref,
                    t_scr, dd_scr):
    _ada_kernel(cond_ref, wada_ref, bada_ref, mods_ref)
    for gi in range(S5_PREP_GPB):
        _s5_prep_group(p_ref.at[gi], cc_ref.at[gi], t_ref.at[gi], bq_ref.at[gi], cqt_ref.at[gi], be_ref.at[gi],
                       a_ref.at[gi], t_scr, dd_scr)


def _s5_prep_group(p_ref, cc_ref, t_ref, bq_ref, cqt_ref, be_ref, a_ref, t_scr, dd_scr):
    gch = S5_GROUP_CH
    lre = p_ref[_PREP_LRE:_PREP_LRE + 1]
    lim = p_ref[_PREP_LIM:_PREP_LIM + 1]
    dt = jnp.exp(p_ref[_PREP_LDT:_PREP_LDT + 1])
    a = lre * dt
    th = lim * dt

    def lam_pow(k):
        mag = jnp.exp(k * a)
        return mag * jnp.cos(k * th), mag * jnp.sin(k * th)

    lb_re, lb_im = lam_pow(1.0)
    nr = lb_re - 1.0
    den = lre * lre + lim * lim
    cf_re = (nr * lre + lb_im * lim) / den
    cf_im = (lb_im * lre - nr * lim) / den
    bt_re = p_ref[_PREP_BT_RE:_PREP_BT_RE + gch]
    bt_im = p_ref[_PREP_BT_IM:_PREP_BT_IM + gch]
    bb_re = jnp.tile(cf_re * bt_re - cf_im * bt_im, (S5_Q, 1))
    bb_im = jnp.tile(cf_re * bt_im + cf_im * bt_re, (S5_Q, 1))

    shape = (S5_W, 128)
    pos = lax.shift_right_logical(lax.broadcasted_iota(jnp.int32, shape, 0), 4)
    is_f = lax.broadcasted_iota(jnp.int32, shape, 1) < S5_STATE
    posq = lax.broadcasted_iota(jnp.int32, (S5_Q, 128), 0).astype(F32)
    is_fq = lax.broadcasted_iota(jnp.int32, (S5_Q, 128), 1) < S5_STATE

    def per_channel(tbl):
        return jnp.broadcast_to(tbl[:, None, :], (S5_Q, S5_GROUP_CH, 128)).reshape(shape)

    p_re, p_im = map(per_channel, lam_pow(jnp.where(is_fq, (S5_Q - 1.0) - posq, posq)))
    w_re = p_re * bb_re - p_im * bb_im
    w_im = p_re * bb_im + p_im * bb_re
    bq = jnp.concatenate([w_re, w_im], axis=1)
    bqt = jnp.transpose(bq)
    bq_ref[...] = bqt.astype(BF16)

    edge = pos == jnp.where(is_f, 0, S5_Q - 1)
    be = jnp.concatenate([jnp.where(edge, bb_re, 0.0), jnp.where(edge, bb_im, 0.0)], axis=1)
    be_ref[...] = jnp.transpose(be).astype(BF16)

    q_re, q_im = map(per_channel, lam_pow(jnp.where(is_fq, posq + 1.0, S5_Q - posq)))
    ct_re = jnp.tile(p_ref[_PREP_C_RE:_PREP_C_RE + gch], (S5_Q, 1))
    ct_im = jnp.tile(p_ref[_PREP_C_IM:_PREP_C_IM + gch], (S5_Q, 1))
    g_re = q_re * ct_re - q_im * ct_im
    g_im = q_re * ct_im + q_im * ct_re
    cqt_ref[...] = jnp.concatenate([g_re, -g_im], axis=1).astype(BF16)

    a_re, a_im = lam_pow(float(S5_Q))
    a_ref[...] = jnp.concatenate([a_re, a_im], axis=1)

    kf = jnp.dot(cc_ref[0:gch], bqt, precision=lax.Precision.HIGHEST, preferred_element_type=F32)
    kb = jnp.dot(cc_ref[gch:2 * gch], bqt, precision=lax.Precision.HIGHEST, preferred_element_type=F32)
    lo = S5_W - gch
    dd_scr[:, 0:S5_W] = kf
    dd_scr[:, lo:lo + S5_W] = kb
    dd_scr[:, lo:S5_W] = kf[:, lo:S5_W] + kb[:, 0:gch]
    for t in range(S5_Q):
        c0 = (S5_Q - 1 - t) * gch
        t_scr[t * gch:(t + 1) * gch, :] = dd_scr[:, c0:c0 + S5_W]
    t_ref[...] = t_scr[...].astype(BF16)


def _s5_prep_call(lam_re, lam_im, log_dt, b_re, b_im, c_re, c_im, ada):
    def fb(p):
        return jnp.transpose(p, (1, 0, 2)).reshape(S5_GROUPS, 1, 2 * S5_STATE)

    def dup(p):
        return jnp.concatenate([p, p], axis=-1)

    ldt = fb(jnp.broadcast_to(log_dt[:, :, None], (2, S5_GROUPS, S5_STATE)))
    pad = jnp.zeros((S5_GROUPS, _PREP_BT_RE - _PREP_LDT - 1, 128), F32)
    packed = jnp.concatenate([fb(lam_re), fb(lam_im), ldt, pad,
                              dup(jnp.transpose(b_re, (0, 2, 1))), dup(jnp.transpose(b_im, (0, 2, 1))),
                              dup(c_re), dup(c_im)], axis=1)
    zero = jnp.zeros_like(c_re)
    cc = jnp.concatenate([jnp.concatenate([c_re, zero, -c_im, zero], axis=-1),
                          jnp.concatenate([zero, c_re, zero, -c_im], axis=-1)], axis=1)

    gpb = S5_PREP_GPB
    nsteps = S5_GROUPS // gpb
    sq = pl.BlockSpec((gpb, S5_W, S5_W), lambda g: (g, 0, 0))
    sq_shape = jax.ShapeDtypeStruct((S5_GROUPS, S5_W, S5_W), BF16)
    cond8, w_ada, b_ada, ada_layer = ada
    ada_in, ada_out = _ada_specs(ada_layer, 6 * D_MODEL // nsteps)
    *mats, mods = pl.pallas_call(
        _s5_prep_kernel,
        out_shape=(sq_shape, sq_shape, sq_shape, sq_shape,
                   jax.ShapeDtypeStruct((S5_GROUPS, 1, S5_W), F32), ADA_SHAPE),
        grid=(nsteps,),
        in_specs=[pl.BlockSpec((gpb, _PREP_ROWS, 128), lambda g: (g, 0, 0)),
                  pl.BlockSpec((gpb, 2 * S5_GROUP_CH, S5_W), lambda g: (g, 0, 0))] + ada_in,
        out_specs=(sq, sq, sq, sq, pl.BlockSpec((gpb, 1, S5_W), lambda g: (g, 0, 0)), ada_out),
        scratch_shapes=[pltpu.VMEM((S5_W, S5_W), F32), pltpu.VMEM((S5_GROUP_CH, 2 * S5_W), F32)],
        compiler_params=pltpu.CompilerParams(vmem_limit_bytes=VMEM_LIMIT),
        name="s5_prep",
    )(packed, cc, cond8, w_ada, b_ada)
    return mats, mods


def _s5_kernel(u_ref, tt_ref, bqt_ref, cqt_ref, bet_ref, a_ref, h0_ref, y_ref, ns_ref,
               ut_scr, x_scr, spf_scr, spb_scr, ne_scr, yt_scr, xt_scr):
    gch = S5_GROUP_CH
    for s in range(S5_Q):
        rows = u_ref[pl.ds(s, S5_ROWS, stride=S5_Q), :]
        rows_t = jnp.transpose(rows).astype(BF16)
        for gl in range(S5_GPB):
            ut_scr[gl, s * gch:(s + 1) * gch, :] = rows_t[gl * gch:(gl + 1) * gch, :]

    for gl in range(S5_GPB):
        ut = ut_scr[gl]
        xt_scr[...] = _dot(bqt_ref[gl], ut)
        x = jnp.transpose(xt_scr[...])
        xt_scr[:, 0:S5_PROMPT_ROWS] = _dot(bet_ref[gl], ut[:, 0:S5_PROMPT_ROWS])
        ne = jnp.transpose(xt_scr[:, 0:S5_PROMPT_ROWS])
        for part in range(2):
            x_scr[part, pl.ds(gl, S5_ROWS, stride=S5_GPB), :] = x[:, part * 128:(part + 1) * 128]
            ne_scr[part, pl.ds(gl, S5_PROMPT_ROWS, stride=S5_GPB), :] = ne[:, part * 128:(part + 1) * 128]

    is_f = lax.broadcasted_iota(jnp.int32, (1, 128), 1) < S5_STATE
    a_re = a_ref[:, 0:128]
    a_im = a_ref[:, 128:256]

    def tile(row):
        return pl.ds(pl.multiple_of(row * S5_GPB, S5_GPB), S5_GPB)

    def scan(base, nseq, nchunk, s_init):
        def body(i, state):
            new = []
            for b in range(nseq):
                s_re, s_im = state[b]
                rows_f = tile(base + b * nchunk + i)
                rows_b = tile(base + b * nchunk + (nchunk - 1 - i))
                spf_scr[0, rows_f, :] = s_re
                spf_scr[1, rows_f, :] = s_im
                spb_scr[0, rows_b, :] = s_re
                spb_scr[1, rows_b, :] = s_im
                x_re = jnp.where(is_f, x_scr[0, rows_f, :], x_scr[0, rows_b, :])
                x_im = jnp.where(is_f, x_scr[1, rows_f, :], x_scr[1, rows_b, :])
                new.append((a_re * s_re - a_im * s_im + x_re, a_re * s_im + a_im * s_re + x_im))
            return tuple(new)

        lax.fori_loop(0, nchunk, body, tuple(s_init))

    zero = jnp.zeros((S5_GPB, 128), F32)
    scan(0, BATCH, S5_PROMPT_CHUNKS, [(zero, zero)] * BATCH)
    scan(S5_PROMPT_ROWS, DEC_BATCH, S5_SAMPLE_CHUNKS,
         [(h0_ref[b, :, 0:128], h0_ref[b, :, 128:256]) for b in range(DEC_BATCH)])

    for b in range(BATCH):
        first = pl.ds(b * S5_PROMPT_CHUNKS * S5_GPB, S5_GPB)
        last = pl.ds(((b + 1) * S5_PROMPT_CHUNKS - 1) * S5_GPB, S5_GPB)
        for part in range(2):
            ns_ref[b, :, part * 128:(part + 1) * 128] = jnp.where(is_f, ne_scr[part, first, :], ne_scr[part, last, :])

    for gl in range(S5_GPB):
        rows = pl.ds(gl, S5_ROWS, stride=S5_GPB)
        carried = jnp.concatenate([jnp.where(is_f, spf_scr[p, rows, :], spb_scr[p, rows, :]) for p in range(2)],
                                  axis=1).astype(BF16)
        yt = _dot(tt_ref[gl], ut_scr[gl]) + lax.dot_general(cqt_ref[gl], carried, NT_DIMS,
                                                            preferred_element_type=F32)
        for t in range(S5_Q):
            yt_scr[t, gl * gch:(gl + 1) * gch, :] = yt[t * gch:(t + 1) * gch, :]
    for t in range(S5_Q):
        y_ref[pl.ds(t, S5_ROWS, stride=S5_Q), :] = jnp.transpose(yt_scr[t])


def _s5_call(u, mats, h0):
    tt_m, bqt_m, cqt_m, bet_m, a_m = mats
    nsteps = S5_GROUPS // S5_GPB
    sq = pl.BlockSpec((S5_GPB, S5_W, S5_W), lambda g: (g, 0, 0))
    state_scr = pltpu.VMEM((2, S5_ROWS * S5_GPB, 128), F32)
    return pl.pallas_call(
        _s5_kernel,
        out_shape=(jax.ShapeDtypeStruct((nsteps, T_TOK, 128), F32),
                   jax.ShapeDtypeStruct((nsteps, BATCH, S5_GPB, S5_W), F32)),
        grid=(nsteps,),
        in_specs=[
            pl.BlockSpec((None, T_TOK, 128), lambda g: (g, 0, 0)),
            sq, sq, sq, sq,
            pl.BlockSpec((S5_GPB, S5_W), lambda g: (g, 0)),
            pl.BlockSpec((None, DEC_BATCH, S5_GPB, S5_W), lambda g: (g, 0, 0, 0)),
        ],
        out_specs=(pl.BlockSpec((None, T_TOK, 128), lambda g: (g, 0, 0)),
                   pl.BlockSpec((None, BATCH, S5_GPB, S5_W), lambda g: (g, 0, 0, 0))),
        scratch_shapes=[pltpu.VMEM((S5_GPB, S5_W, S5_ROWS), BF16), state_scr, state_scr, state_scr,
                        pltpu.VMEM((2, S5_PROMPT_ROWS * S5_GPB, 128), F32),
                        pltpu.VMEM((S5_Q, 128, S5_ROWS), F32), pltpu.VMEM((S5_W, S5_ROWS), F32)],
        compiler_params=pltpu.CompilerParams(vmem_limit_bytes=VMEM_LIMIT),
        name="s5_scan",
    )(u, tt_m, bqt_m, cqt_m, bet_m, a_m.reshape(S5_GROUPS, S5_W), h0)


def _split_bf16(x):
    hi = x.astype(BF16)
    r1 = x - hi.astype(F32)
    mid = r1.astype(BF16)
    lo = (r1 - mid.astype(F32)).astype(BF16)
    return hi, mid, lo


def _cast_specs(shape, nsteps):
    _, rows, cols = shape
    rb = rows // nsteps
    return (pl.BlockSpec((None, rb, cols), lambda *ids: (0, ids[-1], 0)),
            pl.BlockSpec((rb, cols), lambda *ids: (ids[-1], 0)))


def _gla_kernel(*refs, seq_len, nsub, has_s0, has_ada, n_cast):
    rows_refs, gn_ref = refs[:6], refs[6]
    n_ada_in = 7 + has_s0
    n_cast_in = n_ada_in + 3 * has_ada
    n_in = n_cast_in + n_cast
    s0_ref = refs[7] if has_s0 else None
    o_ref, sfin_ref = refs[n_in:n_in + 2]
    n_out = 2 + has_ada + n_cast
    scratch = refs[n_in + n_out:]
    if has_ada:
        _ada_kernel(*refs[n_ada_in:n_cast_in], refs[n_in + 2])
    for src, dst in zip(refs[n_cast_in:n_in], refs[n_in + 2 + has_ada:n_in + n_out]):
        dst[...] = src[...].astype(BF16)
    for j in range(nsub):
        rows = pl.ds(j * seq_len, seq_len)
        _gla_sequence(*[r.at[rows, :] for r in rows_refs], gn_ref, s0_ref.at[j] if has_s0 else None,
                      o_ref.at[rows, :], sfin_ref.at[j], *[s.at[j] for s in scratch], seq_len=seq_len)


def _gla_sequence(q_ref, k_ref, v_ref, gf_ref, gb_ref, r_ref, gn_ref, s0_ref, o_ref, sfin_ref,
                  oi_scr, qd_scr, kv_scr, dec_scr, ss_scr, *, seq_len):
    has_s0 = s0_ref is not None
    nblk = seq_len // GLA_BLK
    nchunk = seq_len // GLA_CHUNK
    cl = GLA_CHUNK
    ti = lax.broadcasted_iota(jnp.int32, (GLA_BLK, GLA_BLK), 0)
    si = lax.broadcasted_iota(jnp.int32, (GLA_BLK, GLA_BLK), 1)
    same = lax.shift_right_logical(ti, 6) == lax.shift_right_logical(si, 6)
    keep = (same & (ti >= si), same & (ti <= si))
    tri = tuple(kp.astype(BF16) for kp in keep)
    lane_head = lax.shift_right_logical(lax.broadcasted_iota(jnp.int32, (cl, GLA_QK), 1), 6)
    zeros_v = jnp.zeros((cl, GLA_DV), BF16)
    heads = [(slice(h * GLA_DK, (h + 1) * GLA_DK), slice(h * GLA_DV, (h + 1) * GLA_DV)) for h in range(GLA_HEADS)]

    for j in range(nblk):
        rows = slice(j * GLA_BLK, (j + 1) * GLA_BLK)
        q = q_ref[rows, :] * (GLA_DK ** -0.5)
        k = k_ref[rows, :]
        v = v_ref[rows, :].astype(BF16)
        qd, kd, k2t = [], [], []
        for d, g_ref in enumerate((gf_ref, gb_ref)):
            b = sum(_dot(tri[d], part) for part in _split_bf16(g_ref[rows, :]))
            last = cl - 1 if d == 0 else 0
            b_last = [b[c * cl + last:c * cl + last + 1] for c in range(GLA_CPB)]
            bl = jnp.concatenate([jnp.broadcast_to(x, (cl, GLA_QK)) for x in b_last], axis=0)
            qd_d = (q * jnp.exp(b)).astype(BF16)
            qd_scr[d, rows, :] = qd_d
            qd.append(qd_d)
            kd.append((k * jnp.exp(-b)).astype(BF16))
            k2t.append(jnp.transpose(k * jnp.exp(bl - b)).astype(BF16))
            for c in range(GLA_CPB):
                dec_scr[d, j * GLA_CPB + c] = jnp.exp(jnp.transpose(jnp.broadcast_to(b_last[c], (GLA_DV, GLA_QK))))
        for h, (ks, vs) in enumerate(heads):
            att = [jnp.where(keep[d], lax.dot_general(qd[d][:, ks], kd[d][:, ks], NT_DIMS,
                                                      preferred_element_type=F32), 0.0) for d in range(2)]
            oi_scr[rows, vs] = _dot((att[0] + att[1]).astype(BF16), v[:, vs])
            vh = v[:, vs]
            vexp = jnp.concatenate(
                [jnp.concatenate([vh[c * cl:(c + 1) * cl] if c2 == c else zeros_v for c2 in range(GLA_CPB)], axis=1)
                 for c in range(GLA_CPB)], axis=0)
            for d in range(2):
                kv_scr[d, j, h] = _dot(k2t[d][ks, :], vexp)

    for d in range(2):
        s = s0_ref[d] if has_s0 else jnp.zeros((GLA_QK, GLA_DV), F32)
        for cg in (range(nchunk) if d == 0 else range(nchunk - 1, -1, -1)):
            j, c = divmod(cg, GLA_CPB)
            ss_scr[d, cg] = s.astype(BF16)
            kv = jnp.concatenate([kv_scr[d, j, h, :, c * GLA_DV:(c + 1) * GLA_DV] for h in range(GLA_HEADS)], axis=0)
            s = s * dec_scr[d, cg] + kv
        sfin_ref[d] = s

    for cg in range(nchunk):
        rows = slice(cg * cl, (cg + 1) * cl)
        inter = []
        for d in range(2):
            qc = qd_scr[d, rows, :]
            qstack = jnp.concatenate([jnp.where(lane_head == h, qc, jnp.zeros_like(qc)) for h in range(GLA_HEADS)],
                                     axis=0)
            inter.append(_dot(qstack, ss_scr[d, cg]))
        gate = jax.nn.silu(r_ref[rows, :])
        for h, (ks, vs) in enumerate(heads):
            hr = slice(h * cl, (h + 1) * cl)
            oh = oi_scr[rows, vs] + inter[0][hr] + inter[1][hr]
            oh = oh * lax.rsqrt(jnp.mean(oh * oh, axis=-1, keepdims=True) + EPS) * gn_ref[...]
            o_ref[rows, vs] = oh * gate[:, vs]


def _gla_call(z, g, gla_norm, s0, seq_len, nseq, row0, nsub, ada=None, casts=()):
    blk = nsub * seq_len
    assert row0 % blk == 0 and nseq % nsub == 0
    r0 = row0 // blk
    has_s0 = s0 is not None
    qk_off = 0
    v_off = 2 * GLA_QK // GLA_VW
    in_specs = [
        pl.BlockSpec((blk, GLA_QK), lambda i: (r0 + i, qk_off)),
        pl.BlockSpec((blk, GLA_QK), lambda i: (r0 + i, qk_off + 1)),
        pl.BlockSpec((blk, GLA_VW), lambda i: (r0 + i, v_off)),
        pl.BlockSpec((blk, GLA_QK), lambda i: (r0 + i, 0)),
        pl.BlockSpec((blk, GLA_QK), lambda i: (r0 + i, 1)),
        pl.BlockSpec((blk, GLA_VW), lambda i: (r0 + i, v_off + 1)),
        pl.BlockSpec((1, GLA_DV), lambda i: (0, 0)),
    ]
    args = [z, z, z, g, g, z, gla_norm]
    state_spec = pl.BlockSpec((nsub, 2, GLA_QK, GLA_DV), lambda i: (i, 0, 0, 0))
    if has_s0:
        in_specs.append(state_spec)
        args.append(s0)
    out_shape = [jax.ShapeDtypeStruct((nseq * seq_len, GLA_VW), F32),
                 jax.ShapeDtypeStruct((nseq, 2, GLA_QK, GLA_DV), F32)]
    out_specs = [pl.BlockSpec((blk, GLA_VW), lambda i: (i, 0)), state_spec]
    nsteps = nseq // nsub
    if ada is not None:
        cond8, w_ada, b_ada, ada_layer = ada
        ada_in, ada_out = _ada_specs(ada_layer, 6 * D_MODEL // nsteps)
        in_specs += ada_in
        args += [cond8, w_ada, b_ada]
        out_shape.append(ADA_SHAPE)
        out_specs.append(ada_out)
    for w in casts:
        cast_in, cast_out = _cast_specs(w.shape, nsteps)
        in_specs.append(cast_in)
        args.append(w)
        out_shape.append(jax.ShapeDtypeStruct(w.shape[1:], BF16))
        out_specs.append(cast_out)
    return pl.pallas_call(
        functools.partial(_gla_kernel, seq_len=seq_len, nsub=nsub, has_s0=has_s0, has_ada=ada is not None,
                          n_cast=len(casts)),
        out_shape=tuple(out_shape),
        grid=(nsteps,),
        in_specs=in_specs,
        out_specs=tuple(out_specs),
        scratch_shapes=[
            pltpu.VMEM((nsub, seq_len, GLA_VW), F32),
            pltpu.VMEM((nsub, 2, seq_len, GLA_QK), BF16),
            pltpu.VMEM((nsub, 2, seq_len // GLA_BLK, GLA_HEADS, GLA_DK, GLA_CPB * GLA_DV), F32),
            pltpu.VMEM((nsub, 2, seq_len // GLA_CHUNK, GLA_QK, GLA_DV), F32),
            pltpu.VMEM((nsub, 2, seq_len // GLA_CHUNK, GLA_QK, GLA_DV), BF16),
        ],
        compiler_params=pltpu.CompilerParams(vmem_limit_bytes=VMEM_LIMIT),
        name=f"gla_len{seq_len}",
    )(*args)


MLP_CHUNK = 512
MLP_LOAD = 256
MLP_SLOTS = 4


class _MlpWeights:
    def __init__(self, w1_hbm, w2_hbm, w1_scr, w2_scr, stage1, stage2, sem, layer):
        self.refs = (w1_hbm, w2_hbm, w1_scr, w2_scr, stage1, stage2, sem)
        self.layer = layer

    def _copies(self, p):
        w1_hbm, w2_hbm, _, _, stage1, stage2, sem = self.refs
        cols = pl.ds(p * MLP_LOAD, MLP_LOAD)
        slot = p % MLP_SLOTS
        return (pltpu.make_async_copy(w1_hbm.at[self.layer, :, cols], stage1.at[slot], sem.at[0, slot]),
                pltpu.make_async_copy(w2_hbm.at[self.layer, cols, :], stage2.at[slot], sem.at[1, slot]))

    def start(self, p):
        for cp in self._copies(p):
            cp.start()

    def prefetch(self):
        for p in range(MLP_SLOTS - 1):
            self.start(p)

    def finish(self, p):
        _, _, w1_scr, w2_scr, stage1, stage2, _ = self.refs
        ahead = p + MLP_SLOTS - 1
        if ahead < D_FF // MLP_LOAD:
            self.start(ahead)
        for cp in self._copies(p):
            cp.wait()
        cols = slice(p * MLP_LOAD, (p + 1) * MLP_LOAD)
        w1_scr[:, cols] = stage1[p % MLP_SLOTS].astype(BF16)
        w2_scr[cols, :] = stage2[p % MLP_SLOTS].astype(BF16)


def _mlp_tail(x, mix, m_ref, gn2_ref, w1_ref, w2_ref, loading=None):
    y1 = x + m_ref[:, 2 * D_MODEL:3 * D_MODEL] * mix
    h = _norm_mod(y1, gn2_ref[...], m_ref[:, 3 * D_MODEL:4 * D_MODEL], m_ref[:, 4 * D_MODEL:5 * D_MODEL]).astype(BF16)
    nchunk = D_FF // MLP_CHUNK
    acc = jnp.zeros(y1.shape, F32)
    for c in range(nchunk):
        cols = slice(c * MLP_CHUNK, (c + 1) * MLP_CHUNK)
        if loading is not None:
            per = MLP_CHUNK // MLP_LOAD
            for p in range(c * per, (c + 1) * per):
                loading.finish(p)
        a = _dot(h, w1_ref[:, cols])
        a = jnp.square(jnp.maximum(a, 0.0)).astype(BF16)
        acc = acc + _dot(a, w2_ref[cols, :])
    return y1 + m_ref[:, 5 * D_MODEL:6 * D_MODEL] * acc


def _run_tail(x, mix, m_ref, gn2_ref, weights, w1_ref, w2_ref, emit):
    first = pl.program_id(0) == 0

    @pl.when(first)
    def _():
        emit(_mlp_tail(x, mix, m_ref, gn2_ref, w1_ref, w2_ref, loading=weights))

    @pl.when(jnp.logical_not(first))
    def _():
        emit(_mlp_tail(x, mix, m_ref, gn2_ref, w1_ref, w2_ref))


def _even_out_kernel(xp_ref, xs_ref, y5_ref, u_ref, dskip_ref, wglu_ref, bglu_ref, glap_ref, glas_ref, wout_ref,
                     m_ref, gn2_ref, w1_hbm, w2_hbm, o_ref, w1_ref, w2_ref, stage1, stage2, sem, *, layer):
    weights = _MlpWeights(w1_hbm, w2_hbm, w1_ref, w2_ref, stage1, stage2, sem, layer)

    @pl.when(pl.program_id(0) == 0)
    def _():
        weights.prefetch()

    nblk = S5_WIDTH // 128
    ys = (jnp.concatenate([y5_ref[b] for b in range(nblk)], axis=1)
          + jnp.concatenate([u_ref[b] for b in range(nblk)], axis=1) * dskip_ref[...])
    gl = jax.nn.gelu(ys)
    s5o = gl * jax.nn.sigmoid(_dot(gl.astype(BF16), wglu_ref[...]) + bglu_ref[...])
    gla = _token_tile(glap_ref, glas_ref, _OUT_TM).astype(BF16)
    mix = _dot(s5o.astype(BF16), wout_ref[0:S5_WIDTH, :]) + _dot(gla, wout_ref[S5_WIDTH:, :])

    def emit(y):
        o_ref[...] = y

    _run_tail(_token_tile(xp_ref, xs_ref, _OUT_TM), mix, m_ref, gn2_ref, weights, w1_ref, w2_ref, emit)


def _odd_out_kernel(x_ref, attp_ref, atts_ref, wo_ref, m_ref, gn2_ref, w1_hbm, w2_hbm, op_ref, os_ref,
                    w1_ref, w2_ref, stage1, stage2, sem, *, layer):
    weights = _MlpWeights(w1_hbm, w2_hbm, w1_ref, w2_ref, stage1, stage2, sem, layer)

    @pl.when(pl.program_id(0) == 0)
    def _():
        weights.prefetch()

    mix = _dot(_token_tile(attp_ref, atts_ref, _OUT_TM), wo_ref[...])
    is_prompt = pl.program_id(0) < T_PROMPT // _OUT_TM

    def emit(y):
        @pl.when(is_prompt)
        def _():
            op_ref[...] = y

        @pl.when(jnp.logical_not(is_prompt))
        def _():
            os_ref[...] = y

    _run_tail(x_ref[...], mix, m_ref, gn2_ref, weights, w1_ref, w2_ref, emit)


_OUT_TM = 512


def _const_spec(shape):
    return pl.BlockSpec(shape, lambda i: (0,) * len(shape), pipeline_mode=pl.Buffered(1))


def _tail_specs(layer):
    tm = _OUT_TM
    return [
        pl.BlockSpec((None, 1, 6 * D_MODEL), lambda i: (_cond_row(i, tm), 0, 0)),
        _const_spec((1, D_MODEL)),
        pl.BlockSpec(memory_space=pl.ANY),
        pl.BlockSpec(memory_space=pl.ANY),
    ]


def _tail_scratch():
    return [pltpu.VMEM((D_MODEL, D_FF), BF16), pltpu.VMEM((D_FF, D_MODEL), BF16),
            pltpu.VMEM((MLP_SLOTS, D_MODEL, MLP_LOAD), F32), pltpu.VMEM((MLP_SLOTS, MLP_LOAD, D_MODEL), F32),
            pltpu.SemaphoreType.DMA((2, MLP_SLOTS))]


_TAIL_PARAMS = dict(dimension_semantics=("arbitrary",), vmem_limit_bytes=VMEM_LIMIT_TAIL)


def _even_out_call(xp, xs, y5, u, d_skip, w_glu, b_glu, gla_p, gla_s, w_out, mods, layer, gn2, w1, w2):
    tm = _OUT_TM
    return pl.pallas_call(
        functools.partial(_even_out_kernel, layer=layer),
        out_shape=jax.ShapeDtypeStruct((T_TOK, D_MODEL), F32),
        grid=(T_TOK // tm,),
        in_specs=_token_specs(tm) + [
            pl.BlockSpec((S5_WIDTH // 128, tm, 128), lambda i: (0, i, 0)),
            pl.BlockSpec((S5_WIDTH // 128, tm, 128), lambda i: (0, i, 0)),
            _const_spec((1, S5_WIDTH)),
            _const_spec((S5_WIDTH, S5_WIDTH)),
            _const_spec((1, S5_WIDTH)),
        ] + _token_specs(tm, GLA_VW) + [
            _const_spec((S5_WIDTH + GLA_VW, D_MODEL)),
        ] + _tail_specs(layer),
        out_specs=pl.BlockSpec((tm, D_MODEL), lambda i: (i, 0)),
        scratch_shapes=_tail_scratch(),
        compiler_params=pltpu.CompilerParams(**_TAIL_PARAMS),
        name="even_out_mlp",
    )(xp, xs, y5, u, d_skip, w_glu, b_glu, gla_p, gla_s, w_out, mods, gn2, w1, w2)


def _odd_out_call(x, att_p, att_s, w_o, mods, layer, gn2, w1, w2):
    tm = _OUT_TM
    return pl.pallas_call(
        functools.partial(_odd_out_kernel, layer=layer),
        out_shape=(jax.ShapeDtypeStruct((T_PROMPT, D_MODEL), F32),
                   jax.ShapeDtypeStruct((T_SAMPLE, D_MODEL), F32)),
        grid=(T_TOK // tm,),
        in_specs=[pl.BlockSpec((tm, D_MODEL), lambda i: (i, 0))] + _token_specs(tm) + [
            _const_spec((D_MODEL, D_MODEL)),
        ] + _tail_specs(layer),
        out_specs=tuple(_token_specs(tm)),
        scratch_shapes=_tail_scratch(),
        compiler_params=pltpu.CompilerParams(**_TAIL_PARAMS),
        name="odd_out_mlp",
    )(x, att_p, att_s, w_o, mods, gn2, w1, w2)


def _qkv_kernel(x_ref, gn_ref, m_ref, w_ref, qn_ref, kn_ref, cos_ref, sin_ref,
                q_ref, kb_ref, vb_ref, k32_ref, v32_ref, *, tile):
    h = _norm_mod(x_ref[...], gn_ref[...], m_ref[:, 0:D_MODEL], m_ref[:, D_MODEL:2 * D_MODEL]).astype(BF16)
    z = _dot(h, w_ref[...])
    v = z[:, (N_HEADS + KV_HEADS) * HEAD_DIM:]
    vb_ref[...] = v.astype(BF16)
    even_lane = (lax.broadcasted_iota(jnp.int32, (1, HEAD_DIM), 1) & 1) == 0

    def heads(rope):
        for hd in range(N_HEADS + KV_HEADS):
            xh = z[:, hd * HEAD_DIM:(hd + 1) * HEAD_DIM]
            gain = qn_ref[...] if hd < N_HEADS else kn_ref[...]
            xh = xh * lax.rsqrt(jnp.mean(xh * xh, axis=-1, keepdims=True) + EPS) * gain
            if rope:
                partner = jnp.where(even_lane, pltpu.roll(xh, HEAD_DIM - 1, 1), pltpu.roll(xh, 1, 1))
                xh = xh * cos_ref[...] + partner * sin_ref[...]
            if hd < N_HEADS:
                q_ref[:, hd * HEAD_DIM:(hd + 1) * HEAD_DIM] = xh.astype(BF16)
            else:
                cols = slice((hd - N_HEADS) * HEAD_DIM, (hd - N_HEADS + 1) * HEAD_DIM)
                kb_ref[:, cols] = xh.astype(BF16)
                if not rope:
                    k32_ref[:, hd - N_HEADS, :] = xh

    is_sample = pl.program_id(0) >= T_PROMPT // tile

    @pl.when(is_sample)
    def _():
        heads(True)

    @pl.when(jnp.logical_not(is_sample))
    def _():
        heads(False)
        for kh in range(KV_HEADS):
            v32_ref[:, kh, :] = v[:, kh * HEAD_DIM:(kh + 1) * HEAD_DIM]


def _qkv_call(x, gn, mods, layer, w_qkv, q_norm, k_norm, cos_t, sin_t):
    tm = 512
    pos_tiles = DEC_SEQ // tm
    n_prompt = T_PROMPT // tm
    kvw = KV_HEADS * HEAD_DIM

    def pos_map(i):
        return (jnp.maximum(i - n_prompt, 0) % pos_tiles, 0)

    def prompt_map(i):
        return (jnp.minimum(i, n_prompt - 1), 0, 0)

    return pl.pallas_call(
        functools.partial(_qkv_kernel, tile=tm),
        out_shape=(jax.ShapeDtypeStruct((T_TOK, N_HEADS * HEAD_DIM), BF16),
                   jax.ShapeDtypeStruct((T_TOK, kvw), BF16),
                   jax.ShapeDtypeStruct((T_TOK, kvw), BF16),
                   jax.ShapeDtypeStruct((T_PROMPT, KV_HEADS, HEAD_DIM), F32),
                   jax.ShapeDtypeStruct((T_PROMPT, KV_HEADS, HEAD_DIM), F32)),
        grid=(T_TOK // tm,),
        in_specs=[
            pl.BlockSpec((tm, D_MODEL), lambda i: (i, 0)),
            pl.BlockSpec((1, D_MODEL), lambda i: (0, 0)),
            pl.BlockSpec((None, 1, 6 * D_MODEL), lambda i: (_cond_row(i, tm), 0, 0)),
            pl.BlockSpec(w_qkv.shape, lambda i: (0, 0)),
            pl.BlockSpec((1, HEAD_DIM), lambda i: (0, 0)),
            pl.BlockSpec((1, HEAD_DIM), lambda i: (0, 0)),
            pl.BlockSpec((tm, HEAD_DIM), pos_map),
            pl.BlockSpec((tm, HEAD_DIM), pos_map),
        ],
        out_specs=(pl.BlockSpec((tm, N_HEADS * HEAD_DIM), lambda i: (i, 0)),
                   pl.BlockSpec((tm, kvw), lambda i: (i, 0)),
                   pl.BlockSpec((tm, kvw), lambda i: (i, 0)),
                   pl.BlockSpec((tm, KV_HEADS, HEAD_DIM), prompt_map),
                   pl.BlockSpec((tm, KV_HEADS, HEAD_DIM), prompt_map)),
        compiler_params=pltpu.CompilerParams(vmem_limit_bytes=VMEM_LIMIT),
        name="odd_qkv",
    )(x, gn, mods, w_qkv, q_norm, k_norm, cos_t, sin_t)


def _rope_tables():
    f32 = np.float32
    rows = DEC_SEQ // GRID_W
    row = np.repeat(np.arange(rows, dtype=f32), GRID_W)
    col = np.tile(np.arange(GRID_W, dtype=f32), rows)
    inv = np.power(f32(ROPE_THETA), -np.arange(0, AXIS_DIM, 2, dtype=f32) / f32(AXIS_DIM)).astype(f32)
    ang = np.concatenate([row[:, None] * inv, col[:, None] * inv], axis=-1).astype(f32)
    cos_t = np.repeat(np.cos(ang), 2, axis=-1).astype(f32)
    sin = np.sin(ang).astype(f32)
    sin_t = np.stack([-sin, sin], axis=-1).reshape(DEC_SEQ, HEAD_DIM)
    return jnp.asarray(cos_t), jnp.asarray(sin_t)


def _attn_kernel(*refs, has_cache):
    q_ref, k_ref, v_ref = refs[:3]
    ck_ref, cv_ref = refs[3:5] if has_cache else (None, None)
    o_ref = refs[-1]
    c = HEAD_DIM ** -0.5 * math.log2(math.e)
    ones_col = (lax.broadcasted_iota(jnp.int32, (1, HEAD_DIM), 1) == 0).astype(BF16)

    def with_ones(v):
        return jnp.concatenate([v, jnp.broadcast_to(ones_col, v.shape)], axis=1)

    for g in range(k_ref.shape[1] // HEAD_DIM):
        kv_cols = slice(g * HEAD_DIM, (g + 1) * HEAD_DIM)
        k = k_ref[:, kv_cols]
        v = with_ones(v_ref[:, kv_cols])
        if has_cache:
            ck = ck_ref[:, kv_cols].astype(BF16)
            cv = with_ones(cv_ref[:, kv_cols].astype(BF16))
        for r in range(Q_PER_KV):
            cs = slice((g * Q_PER_KV + r) * HEAD_DIM, (g * Q_PER_KV + r + 1) * HEAD_DIM)
            q = q_ref[:, cs]
            s = lax.dot_general(q, k, NT_DIMS, preferred_element_type=F32)
            m = jnp.max(s, axis=-1, keepdims=True)
            if has_cache:
                sc = lax.dot_general(q, ck, NT_DIMS, preferred_element_type=F32)
                m = jnp.maximum(m, jnp.max(sc, axis=-1, keepdims=True))
            mc = m * c
            o = _dot(jnp.exp2(s * c - mc).astype(BF16), v)
            if has_cache:
                o = o + _dot(jnp.exp2(sc * c - mc).astype(BF16), cv)
            o_ref[:, cs] = (o[:, 0:HEAD_DIM] / o[:, HEAD_DIM:HEAD_DIM + 1]).astype(BF16)


def _attn_call(q, k, v, cache_k, cache_v, seq_len, row0, nrows, kv_per_step):
    assert row0 % seq_len == 0 and nrows % seq_len == 0 and KV_HEADS % kv_per_step == 0
    has_cache = cache_k is not None
    b0 = row0 // seq_len
    qw = kv_per_step * Q_PER_KV * HEAD_DIM
    kw = kv_per_step * HEAD_DIM
    in_specs = [
        pl.BlockSpec((seq_len, qw), lambda b, g: (b0 + b, g)),
        pl.BlockSpec((seq_len, kw), lambda b, g: (b0 + b, g)),
        pl.BlockSpec((seq_len, kw), lambda b, g: (b0 + b, g)),
    ]
    args = [q, k, v]
    if has_cache:
        in_specs += [pl.BlockSpec((PAST_LEN, kw), lambda b, g: (b, g)),
                     pl.BlockSpec((PAST_LEN, kw), lambda b, g: (b, g))]
        args += [cache_k, cache_v]
    return pl.pallas_call(
        functools.partial(_attn_kernel, has_cache=has_cache),
        out_shape=jax.ShapeDtypeStruct((nrows, N_HEADS * HEAD_DIM), BF16),
        grid=(nrows // seq_len, KV_HEADS // kv_per_step),
        in_specs=in_specs,
        out_specs=pl.BlockSpec((seq_len, qw), lambda b, g: (b, g)),
        compiler_params=pltpu.CompilerParams(vmem_limit_bytes=VMEM_LIMIT),
        name=f"attn_len{seq_len}",
    )(*args)


def kernel(x_prompt, x_sample, state_s5_re, state_s5_im, state_gla, cache_k, cache_v, c, c_ctx, norm_mix, norm_mlp, w_ada, b_ada, w_mlp_in, w_mlp_out, w_in_e, w_out_e, s5_lambda_re, s5_lambda_im, s5_log_dt, s5_b_re, s5_b_im, s5_c_re, s5_c_im, s5_d, s5_w_glu, s5_b_glu, gla_w_gate2, gla_b_gate, gla_norm, w_qkv_o, w_o_o, q_norm, k_norm):
    xp = x_prompt.reshape(T_PROMPT, D_MODEL)
    xs = x_sample.reshape(T_SAMPLE, D_MODEL)
    cond8 = jnp.concatenate([c_ctx[None, :], c, jnp.zeros((COND_ROWS - 1 - DEC_BATCH, D_MODEL), F32)], axis=0)
    b_ada3 = b_ada.reshape(DEPTH, 1, 6 * D_MODEL)
    w1_all, w2_all = w_mlp_in, w_mlp_out
    mats, mods0 = _s5_prep_call(s5_lambda_re[0], s5_lambda_im[0], s5_log_dt[0], s5_b_re[0], s5_b_im[0],
                                s5_c_re[0], s5_c_im[0], ada=(cond8, w_ada, b_ada3, 0))

    n_main = S5_WIDTH + 2 * GLA_QK + 2 * GLA_VW
    w_in = w_in_e[0]
    w_main = w_in[:, :n_main].astype(BF16)
    w_glr = jnp.pad(w_in[:, n_main:], ((0, 0), (0, 128 - 2 * GLA_RANK))).astype(BF16)
    zg = jnp.zeros((GLA_RANK, GLA_QK), F32)
    w_gate = jnp.concatenate([jnp.concatenate([gla_w_gate2[0, 0], zg], axis=1),
                              jnp.concatenate([zg, gla_w_gate2[0, 1]], axis=1),
                              jnp.zeros((128 - 2 * GLA_RANK, 2 * GLA_QK), F32)], axis=0).astype(BF16)
    b_gate = gla_b_gate[0].reshape(1, 2 * GLA_QK)
    u, z, g = _inproj_call(xp, xs, norm_mix[0:1], mods0, 0, w_main, w_glr, w_gate, b_gate)

    def state_rows(s):
        return jnp.transpose(s, (2, 0, 1, 3)).reshape(S5_GROUPS, DEC_BATCH, 2 * S5_STATE)

    h0 = jnp.concatenate([state_rows(state_s5_re[:, 0]), state_rows(state_s5_im[:, 0])], axis=-1)
    nsteps = S5_GROUPS // S5_GPB
    h0 = jnp.transpose(h0.reshape(nsteps, S5_GPB, DEC_BATCH, S5_W), (0, 2, 1, 3))
    y5, ns = _s5_call(u, mats, h0)
    ns = jnp.transpose(ns, (0, 2, 1, 3)).reshape(S5_GROUPS, BATCH, S5_W)

    def state_out(n):
        return jnp.transpose(n.reshape(S5_GROUPS, BATCH, 2, S5_STATE), (1, 2, 0, 3))[:, None]

    new_s5_re = state_out(ns[:, :, :2 * S5_STATE])
    new_s5_im = state_out(ns[:, :, 2 * S5_STATE:])

    gn_gla = gla_norm[0].reshape(1, GLA_DV)
    gla_p, sfin, w_glu, w_out, w_qkv, w_o = _gla_call(z, g, gn_gla, None, SEQ, BATCH, 0, nsub=4,
                                                      casts=(s5_w_glu, w_out_e, w_qkv_o, w_o_o))
    s0 = state_gla[:, 0].reshape(DEC_BATCH, 2, GLA_QK, GLA_DV)
    gla_s, _, mods1 = _gla_call(z, g, gn_gla, s0, DEC_SEQ, DEC_BATCH, T_PROMPT, nsub=1,
                                ada=(cond8, w_ada, b_ada3, 1))
    new_gla = sfin.reshape(BATCH, 1, 2, GLA_HEADS, GLA_DK, GLA_DV)

    x = _even_out_call(xp, xs, y5, u, s5_d[0].reshape(1, S5_WIDTH), w_glu,
                       s5_b_glu[0].reshape(1, S5_WIDTH), gla_p, gla_s, w_out, mods0, 0,
                       norm_mlp[0:1], w1_all, w2_all)

    cos_t, sin_t = _rope_tables()
    q, k, v, k32, v32 = _qkv_call(x, norm_mix[1:2], mods1, 1, w_qkv,
                                  q_norm[0].reshape(1, HEAD_DIM), k_norm[0].reshape(1, HEAD_DIM), cos_t, sin_t)
    att_p = _attn_call(q, k, v, None, None, SEQ, 0, T_PROMPT, kv_per_step=1)
    ck = cache_k[:, 0].reshape(DEC_BATCH * PAST_LEN, KV_HEADS * HEAD_DIM)
    cv = cache_v[:, 0].reshape(DEC_BATCH * PAST_LEN, KV_HEADS * HEAD_DIM)
    att_s = _attn_call(q, k, v, ck, cv, DEC_SEQ, T_PROMPT, T_SAMPLE, kv_per_step=1)
    yp, ys = _odd_out_call(x, att_p, att_s, w_o, mods1, 1, norm_mlp[1:2],
                           w1_all, w2_all)

    new_k = k32.reshape(BATCH, 1, SEQ, KV_HEADS, HEAD_DIM)
    new_v = v32.reshape(BATCH, 1, SEQ, KV_HEADS, HEAD_DIM)
    y_prompt = yp.reshape(BATCH, SEQ, D_MODEL)
    y_sample = ys.reshape(DEC_BATCH, DEC_SEQ, D_MODEL)
    return (y_prompt, y_sample, new_s5_re, new_s5_im, new_gla, new_k, new_v)
```

```python
import functools
import math

import jax
import jax.numpy as jnp
import numpy as np
from jax import lax
from jax.experimental import pallas as pl
from jax.experimental.pallas import tpu as pltpu

F32 = jnp.float32
BF16 = jnp.bfloat16

D_MODEL = 1024
BATCH = 16
SEQ = 256
DEPTH = 2
DEC_BATCH = 4
DEC_SEQ = 1024
PAST_LEN = 512
GRID_W = 64
S5_WIDTH = 512
S5_GROUP_CH = 16
S5_GROUPS = 32
S5_STATE = 64
GLA_HEADS = 4
GLA_VW = 512
GLA_DV = 128
GLA_DK = 64
GLA_QK = 256
GLA_RANK = 16
GLA_TAU = 16.0
GLA_CHUNK = 64
GLA_CPB = 4
GLA_BLK = GLA_CPB * GLA_CHUNK
HEAD_DIM = 128
N_HEADS = 8
KV_HEADS = 2
Q_PER_KV = N_HEADS // KV_HEADS
AXIS_DIM = 64
ROPE_THETA = 10000.0
D_FF = 4096
EPS = 1e-6

T_PROMPT = BATCH * SEQ
T_SAMPLE = DEC_BATCH * DEC_SEQ
T_TOK = T_PROMPT + T_SAMPLE
COND_ROWS = 8
COND_SPAN = 1024
PROMPT_SPANS = T_PROMPT // COND_SPAN

S5_Q = 16
S5_W = S5_Q * S5_GROUP_CH
S5_GPB = 128 // S5_GROUP_CH
S5_ROWS = T_TOK // S5_Q
S5_PROMPT_ROWS = T_PROMPT // S5_Q
S5_PROMPT_CHUNKS = SEQ // S5_Q
S5_SAMPLE_CHUNKS = DEC_SEQ // S5_Q

VMEM_LIMIT = 56 * 1024 * 1024
VMEM_LIMIT_TAIL = 60 * 1024 * 1024

NT_DIMS = (((1,), (1,)), ((), ()))
TN_DIMS = (((0,), (0,)), ((), ()))


def _cond_row(i, tile):
    return jnp.maximum((i * tile) // COND_SPAN - (PROMPT_SPANS - 1), 0)


def _norm_mod(x, gain, shift, scale):
    y = x * lax.rsqrt(jnp.mean(x * x, axis=-1, keepdims=True) + EPS)
    return (y * gain) * (1.0 + scale) + shift


def _dot(a, b):
    return jnp.dot(a, b, preferred_element_type=F32)


def _ada_kernel(cond_ref, w_ref, b_ref, o_ref):
    s = jax.nn.silu(cond_ref[...]).astype(BF16)
    o_ref[:, 0, :] = _dot(s, w_ref[...].astype(BF16)) + b_ref[...]


ADA_SHAPE = jax.ShapeDtypeStruct((COND_ROWS, 1, 6 * D_MODEL), F32)


def _ada_specs(layer, tn):
    return ([pl.BlockSpec((COND_ROWS, D_MODEL), lambda *ids: (0, 0)),
             pl.BlockSpec((None, D_MODEL, tn), lambda *ids: (layer, 0, ids[-1])),
             pl.BlockSpec((None, 1, tn), lambda *ids: (layer, 0, ids[-1]))],
            pl.BlockSpec((COND_ROWS, 1, tn), lambda *ids: (0, 0, ids[-1])))


def _token_specs(tile, width=D_MODEL):
    n_prompt = T_PROMPT // tile
    return [pl.BlockSpec((tile, width), lambda i: (jnp.minimum(i, n_prompt - 1), 0)),
            pl.BlockSpec((tile, width), lambda i: (jnp.maximum(i - n_prompt, 0), 0))]


def _token_tile(xp_ref, xs_ref, tile):
    return jnp.where(pl.program_id(0) < T_PROMPT // tile, xp_ref[...], xs_ref[...])


def _inproj_kernel(xp_ref, xs_ref, gn_ref, m_ref, w_ref, wglr_ref, wg_ref, bg_ref, u_ref, z_ref, g_ref, *, tile):
    x = _token_tile(xp_ref, xs_ref, tile)
    h = _norm_mod(x, gn_ref[...], m_ref[:, 0:D_MODEL], m_ref[:, D_MODEL:2 * D_MODEL]).astype(BF16)
    z = _dot(h, w_ref[...])
    for blk in range(S5_WIDTH // 128):
        u_ref[blk] = z[:, blk * 128:(blk + 1) * 128]
    z_ref[...] = z[:, S5_WIDTH:]
    glr = _dot(h, wglr_ref[...]).astype(BF16)
    pre = _dot(glr, wg_ref[...]) + bg_ref[...]
    g_ref[...] = jax.nn.log_sigmoid(pre) * (1.0 / GLA_TAU)


def _inproj_call(xp, xs, gn, mods, layer, w_main, w_glr, w_gate, b_gate):
    tm = 1024
    nz = w_main.shape[1]
    return pl.pallas_call(
        functools.partial(_inproj_kernel, tile=tm),
        out_shape=(jax.ShapeDtypeStruct((S5_WIDTH // 128, T_TOK, 128), F32),
                   jax.ShapeDtypeStruct((T_TOK, nz - S5_WIDTH), F32),
                   jax.ShapeDtypeStruct((T_TOK, 2 * GLA_QK), F32)),
        grid=(T_TOK // tm,),
        in_specs=_token_specs(tm) + [
            pl.BlockSpec((1, D_MODEL), lambda i: (0, 0)),
            pl.BlockSpec((None, 1, 6 * D_MODEL), lambda i: (_cond_row(i, tm), 0, 0)),
            pl.BlockSpec((D_MODEL, nz), lambda i: (0, 0)),
            pl.BlockSpec((D_MODEL, 128), lambda i: (0, 0)),
            pl.BlockSpec((128, 2 * GLA_QK), lambda i: (0, 0)),
            pl.BlockSpec((1, 2 * GLA_QK), lambda i: (0, 0)),
        ],
        out_specs=(pl.BlockSpec((S5_WIDTH // 128, tm, 128), lambda i: (0, i, 0)),
                   pl.BlockSpec((tm, nz - S5_WIDTH), lambda i: (i, 0)),
                   pl.BlockSpec((tm, 2 * GLA_QK), lambda i: (i, 0))),
        compiler_params=pltpu.CompilerParams(vmem_limit_bytes=VMEM_LIMIT),
        name="even_inproj",
    )(xp, xs, gn, mods, w_main, w_glr, w_gate, b_gate)


S5_PREP_GPB = 8
_PREP_LRE, _PREP_LIM, _PREP_LDT = 0, 1, 2
_PREP_BT_RE, _PREP_BT_IM, _PREP_C_RE, _PREP_C_IM, _PREP_ROWS = 8, 24, 40, 56, 72


def _s5_prep_kernel(p_ref, cc_ref, cond_ref, wada_ref, bada_ref, t_ref, bq_ref, cqt_ref, be_ref, a_ref, mods_ref,
                    t_scr, dd_scr):
    _ada_kernel(cond_ref, wada_ref, bada_ref, mods_ref)
    for gi in range(S5_PREP_GPB):
        _s5_prep_group(p_ref.at[gi], cc_ref.at[gi], t_ref.at[gi], bq_ref.at[gi], cqt_ref.at[gi], be_ref.at[gi],
                       a_ref.at[gi], t_scr, dd_scr)


def _s5_prep_group(p_ref, cc_ref, t_ref, bq_ref, cqt_ref, be_ref, a_ref, t_scr, dd_scr):
    gch = S5_GROUP_CH
    lre = p_ref[_PREP_LRE:_PREP_LRE + 1]
    lim = p_ref[_PREP_LIM:_PREP_LIM + 1]
    dt = jnp.exp(p_ref[_PREP_LDT:_PREP_LDT + 1])
    a = lre * dt
    th = lim * dt

    def lam_pow(k):
        mag = jnp.exp(k * a)
        return mag * jnp.cos(k * th), mag * jnp.sin(k * th)

    lb_re, lb_im = lam_pow(1.0)
    nr = lb_re - 1.0
    den = lre * lre + lim * lim
    cf_re = (nr * lre + lb_im * lim) / den
    cf_im = (lb_im * lre - nr * lim) / den
    bt_re = p_ref[_PREP_BT_RE:_PREP_BT_RE + gch]
    bt_im = p_ref[_PREP_BT_IM:_PREP_BT_IM + gch]
    bb_re = jnp.tile(cf_re * bt_re - cf_im * bt_im, (S5_Q, 1))
    bb_im = jnp.tile(cf_re * bt_im + cf_im * bt_re, (S5_Q, 1))

    shape = (S5_W, 128)
    pos = lax.shift_right_logical(lax.broadcasted_iota(jnp.int32, shape, 0), 4)
    is_f = lax.broadcasted_iota(jnp.int32, shape, 1) < S5_STATE
    posq = lax.broadcasted_iota(jnp.int32, (S5_Q, 128), 0).astype(F32)
    is_fq = lax.broadcasted_iota(jnp.int32, (S5_Q, 128), 1) < S5_STATE

    def per_channel(tbl):
        return jnp.broadcast_to(tbl[:, None, :], (S5_Q, S5_GROUP_CH, 128)).reshape(shape)

    p_re, p_im = map(per_channel, lam_pow(jnp.where(is_fq, (S5_Q - 1.0) - posq, posq)))
    w_re = p_re * bb_re - p_im * bb_im
    w_im = p_re * bb_im + p_im * bb_re
    bq = jnp.concatenate([w_re, w_im], axis=1)
    bqt = jnp.transpose(bq)
    bq_ref[...] = bqt.astype(BF16)

    edge = pos == jnp.where(is_f, 0, S5_Q - 1)
    be = jnp.concatenate([jnp.where(edge, bb_re, 0.0), jnp.where(edge, bb_im, 0.0)], axis=1)
    be_ref[...] = jnp.transpose(be).astype(BF16)

    q_re, q_im = map(per_channel, lam_pow(jnp.where(is_fq, posq + 1.0, S5_Q - posq)))
    ct_re = jnp.tile(p_ref[_PREP_C_RE:_PREP_C_RE + gch], (S5_Q, 1))
    ct_im = jnp.tile(p_ref[_PREP_C_IM:_PREP_C_IM + gch], (S5_Q, 1))
    g_re = q_re * ct_re - q_im * ct_im
    g_im = q_re * ct_im + q_im * ct_re
    cqt_ref[...] = jnp.concatenate([g_re, -g_im], axis=1).astype(BF16)

    a_re, a_im = lam_pow(float(S5_Q))
    a_ref[...] = jnp.concatenate([a_re, a_im], axis=1)

    kf = jnp.dot(cc_ref[0:gch], bqt, precision=lax.Precision.HIGHEST, preferred_element_type=F32)
    kb = jnp.dot(cc_ref[gch:2 * gch], bqt, precision=lax.Precision.HIGHEST, preferred_element_type=F32)
    lo = S5_W - gch
    dd_scr[:, 0:S5_W] = kf
    dd_scr[:, lo:lo + S5_W] = kb
    dd_scr[:, lo:S5_W] = kf[:, lo:S5_W] + kb[:, 0:gch]
    for t in range(S5_Q):
        c0 = (S5_Q - 1 - t) * gch
        t_scr[t * gch:(t + 1) * gch, :] = dd_scr[:, c0:c0 + S5_W]
    t_ref[...] = t_scr[...].astype(BF16)


def _s5_prep_call(lam_re, lam_im, log_dt, b_re, b_im, c_re, c_im, ada):
    def fb(p):
        return jnp.transpose(p, (1, 0, 2)).reshape(S5_GROUPS, 1, 2 * S5_STATE)

    def dup(p):
        return jnp.concatenate([p, p], axis=-1)

    ldt = fb(jnp.broadcast_to(log_dt[:, :, None], (2, S5_GROUPS, S5_STATE)))
    pad = jnp.zeros((S5_GROUPS, _PREP_BT_RE - _PREP_LDT - 1, 128), F32)
    packed = jnp.concatenate([fb(lam_re), fb(lam_im), ldt, pad,
                              dup(jnp.transpose(b_re, (0, 2, 1))), dup(jnp.transpose(b_im, (0, 2, 1))),
                              dup(c_re), dup(c_im)], axis=1)
    zero = jnp.zeros_like(c_re)
    cc = jnp.concatenate([jnp.concatenate([c_re, zero, -c_im, zero], axis=-1),
                          jnp.concatenate([zero, c_re, zero, -c_im], axis=-1)], axis=1)

    gpb = S5_PREP_GPB
    nsteps = S5_GROUPS // gpb
    sq = pl.BlockSpec((gpb, S5_W, S5_W), lambda g: (g, 0, 0))
    sq_shape = jax.ShapeDtypeStruct((S5_GROUPS, S5_W, S5_W), BF16)
    cond8, w_ada, b_ada, ada_layer = ada
    ada_in, ada_out = _ada_specs(ada_layer, 6 * D_MODEL // nsteps)
    *mats, mods = pl.pallas_call(
        _s5_prep_kernel,
        out_shape=(sq_shape, sq_shape, sq_shape, sq_shape,
                   jax.ShapeDtypeStruct((S5_GROUPS, 1, S5_W), F32), ADA_SHAPE),
        grid=(nsteps,),
        in_specs=[pl.BlockSpec((gpb, _PREP_ROWS, 128), lambda g: (g, 0, 0)),
                  pl.BlockSpec((gpb, 2 * S5_GROUP_CH, S5_W), lambda g: (g, 0, 0))] + ada_in,
        out_specs=(sq, sq, sq, sq, pl.BlockSpec((gpb, 1, S5_W), lambda g: (g, 0, 0)), ada_out),
        scratch_shapes=[pltpu.VMEM((S5_W, S5_W), F32), pltpu.VMEM((S5_GROUP_CH, 2 * S5_W), F32)],
        compiler_params=pltpu.CompilerParams(vmem_limit_bytes=VMEM_LIMIT),
        name="s5_prep",
    )(packed, cc, cond8, w_ada, b_ada)
    return mats, mods


def _s5_kernel(u_ref, tt_ref, bqt_ref, cqt_ref, bet_ref, a_ref, h0_ref, y_ref, ns_ref,
               ut_scr, x_scr, spf_scr, spb_scr, ne_scr, yt_scr, xt_scr):
    gch = S5_GROUP_CH
    for s in range(S5_Q):
        rows = u_ref[pl.ds(s, S5_ROWS, stride=S5_Q), :]
        rows_t = jnp.transpose(rows).astype(BF16)
        for gl in range(S5_GPB):
            ut_scr[gl, s * gch:(s + 1) * gch, :] = rows_t[gl * gch:(gl + 1) * gch, :]

    for gl in range(S5_GPB):
        ut = ut_scr[gl]
        xt_scr[...] = _dot(bqt_ref[gl], ut)
        x = jnp.transpose(xt_scr[...])
        xt_scr[:, 0:S5_PROMPT_ROWS] = _dot(bet_ref[gl], ut[:, 0:S5_PROMPT_ROWS])
        ne = jnp.transpose(xt_scr[:, 0:S5_PROMPT_ROWS])
        for part in range(2):
            x_scr[part, pl.ds(gl, S5_ROWS, stride=S5_GPB), :] = x[:, part * 128:(part + 1) * 128]
            ne_scr[part, pl.ds(gl, S5_PROMPT_ROWS, stride=S5_GPB), :] = ne[:, part * 128:(part + 1) * 128]

    is_f = lax.broadcasted_iota(jnp.int32, (1, 128), 1) < S5_STATE
    a_re = a_ref[:, 0:128]
    a_im = a_ref[:, 128:256]

    def tile(row):
        return pl.ds(pl.multiple_of(row * S5_GPB, S5_GPB), S5_GPB)

    def scan(base, nseq, nchunk, s_init):
        def body(i, state):
            new = []
            for b in range(nseq):
                s_re, s_im = state[b]
                rows_f = tile(base + b * nchunk + i)
                rows_b = tile(base + b * nchunk + (nchunk - 1 - i))
                spf_scr[0, rows_f, :] = s_re
                spf_scr[1, rows_f, :] = s_im
                spb_scr[0, rows_b, :] = s_re
                spb_scr[1, rows_b, :] = s_im
                x_re = jnp.where(is_f, x_scr[0, rows_f, :], x_scr[0, rows_b, :])
                x_im = jnp.where(is_f, x_scr[1, rows_f, :], x_scr[1, rows_b, :])
                new.append((a_re * s_re - a_im * s_im + x_re, a_re * s_im + a_im * s_re + x_im))
            return tuple(new)

        lax.fori_loop(0, nchunk, body, tuple(s_init))

    zero = jnp.zeros((S5_GPB, 128), F32)
    scan(0, BATCH, S5_PROMPT_CHUNKS, [(zero, zero)] * BATCH)
    scan(S5_PROMPT_ROWS, DEC_BATCH, S5_SAMPLE_CHUNKS,
         [(h0_ref[b, :, 0:128], h0_ref[b, :, 128:256]) for b in range(DEC_BATCH)])

    for b in range(BATCH):
        first = pl.ds(b * S5_PROMPT_CHUNKS * S5_GPB, S5_GPB)
        last = pl.ds(((b + 1) * S5_PROMPT_CHUNKS - 1) * S5_GPB, S5_GPB)
        for part in range(2):
            ns_ref[b, :, part * 128:(part + 1) * 128] = jnp.where(is_f, ne_scr[part, first, :], ne_scr[part, last, :])

    for gl in range(S5_GPB):
        rows = pl.ds(gl, S5_ROWS, stride=S5_GPB)
        carried = jnp.concatenate([jnp.where(is_f, spf_scr[p, rows, :], spb_scr[p, rows, :]) for p in range(2)],
                                  axis=1).astype(BF16)
        yt = _dot(tt_ref[gl], ut_scr[gl]) + lax.dot_general(cqt_ref[gl], carried, NT_DIMS,
                                                            preferred_element_type=F32)
        for t in range(S5_Q):
            yt_scr[t, gl * gch:(gl + 1) * gch, :] = yt[t * gch:(t + 1) * gch, :]
    for t in range(S5_Q):
        y_ref[pl.ds(t, S5_ROWS, stride=S5_Q), :] = jnp.transpose(yt_scr[t])


def _s5_call(u, mats, h0):
    tt_m, bqt_m, cqt_m, bet_m, a_m = mats
    nsteps = S5_GROUPS // S5_GPB
    sq = pl.BlockSpec((S5_GPB, S5_W, S5_W), lambda g: (g, 0, 0))
    state_scr = pltpu.VMEM((2, S5_ROWS * S5_GPB, 128), F32)
    return pl.pallas_call(
        _s5_kernel,
        out_shape=(jax.ShapeDtypeStruct((nsteps, T_TOK, 128), F32),
                   jax.ShapeDtypeStruct((nsteps, BATCH, S5_GPB, S5_W), F32)),
        grid=(nsteps,),
        in_specs=[
            pl.BlockSpec((None, T_TOK, 128), lambda g: (g, 0, 0)),
            sq, sq, sq, sq,
            pl.BlockSpec((S5_GPB, S5_W), lambda g: (g, 0)),
            pl.BlockSpec((None, DEC_BATCH, S5_GPB, S5_W), lambda g: (g, 0, 0, 0)),
        ],
        out_specs=(pl.BlockSpec((None, T_TOK, 128), lambda g: (g, 0, 0)),
                   pl.BlockSpec((None, BATCH, S5_GPB, S5_W), lambda g: (g, 0, 0, 0))),
        scratch_shapes=[pltpu.VMEM((S5_GPB, S5_W, S5_ROWS), BF16), state_scr, state_scr, state_scr,
                        pltpu.VMEM((2, S5_PROMPT_ROWS * S5_GPB, 128), F32),
                        pltpu.VMEM((S5_Q, 128, S5_ROWS), F32), pltpu.VMEM((S5_W, S5_ROWS), F32)],
        compiler_params=pltpu.CompilerParams(vmem_limit_bytes=VMEM_LIMIT),
        name="s5_scan",
    )(u, tt_m, bqt_m, cqt_m, bet_m, a_m.reshape(S5_GROUPS, S5_W), h0)


def _split_bf16(x):
    hi = x.astype(BF16)
    r1 = x - hi.astype(F32)
    mid = r1.astype(BF16)
    lo = (r1 - mid.astype(F32)).astype(BF16)
    return hi, mid, lo


def _cast_specs(shape, nsteps):
    _, rows, cols = shape
    rb = rows // nsteps
    return (pl.BlockSpec((None, rb, cols), lambda *ids: (0, ids[-1], 0)),
            pl.BlockSpec((rb, cols), lambda *ids: (ids[-1], 0)))


def _gla_kernel(*refs, seq_len, nsub, has_s0, has_ada, n_cast):
    rows_refs, gn_ref = refs[:6], refs[6]
    n_ada_in = 7 + has_s0
    n_cast_in = n_ada_in + 3 * has_ada
    n_in = n_cast_in + n_cast
    s0_ref = refs[7] if has_s0 else None
    o_ref, sfin_ref = refs[n_in:n_in + 2]
    n_out = 2 + has_ada + n_cast
    scratch = refs[n_in + n_out:]
    if has_ada:
        _ada_kernel(*refs[n_ada_in:n_cast_in], refs[n_in + 2])
    for src, dst in zip(refs[n_cast_in:n_in], refs[n_in + 2 + has_ada:n_in + n_out]):
        dst[...] = src[...].astype(BF16)
    for j in range(nsub):
        rows = pl.ds(j * seq_len, seq_len)
        _gla_sequence(*[r.at[rows, :] for r in rows_refs], gn_ref, s0_ref.at[j] if has_s0 else None,
                      o_ref.at[rows, :], sfin_ref.at[j], *[s.at[j] for s in scratch], seq_len=seq_len)


def _gla_sequence(q_ref, k_ref, v_ref, gf_ref, gb_ref, r_ref, gn_ref, s0_ref, o_ref, sfin_ref,
                  oi_scr, qd_scr, kv_scr, dec_scr, ss_scr, *, seq_len):
    has_s0 = s0_ref is not None
    nblk = seq_len // GLA_BLK
    nchunk = seq_len // GLA_CHUNK
    cl = GLA_CHUNK
    ti = lax.broadcasted_iota(jnp.int32, (GLA_BLK, GLA_BLK), 0)
    si = lax.broadcasted_iota(jnp.int32, (GLA_BLK, GLA_BLK), 1)
    same = lax.shift_right_logical(ti, 6) == lax.shift_right_logical(si, 6)
    keep = (same & (ti >= si), same & (ti <= si))
    tri = tuple(kp.astype(BF16) for kp in keep)
    lane_head = lax.shift_right_logical(lax.broadcasted_iota(jnp.int32, (cl, GLA_QK), 1), 6)
    zeros_v = jnp.zeros((cl, GLA_DV), BF16)
    heads = [(slice(h * GLA_DK, (h + 1) * GLA_DK), slice(h * GLA_DV, (h + 1) * GLA_DV)) for h in range(GLA_HEADS)]

    for j in range(nblk):
        rows = slice(j * GLA_BLK, (j + 1) * GLA_BLK)
        q = q_ref[rows, :] * (GLA_DK ** -0.5)
        k = k_ref[rows, :]
        v = v_ref[rows, :].astype(BF16)
        qd, kd, k2t = [], [], []
        for d, g_ref in enumerate((gf_ref, gb_ref)):
            b = sum(_dot(tri[d], part) for part in _split_bf16(g_ref[rows, :]))
            last = cl - 1 if d == 0 else 0
            b_last = [b[c * cl + last:c * cl + last + 1] for c in range(GLA_CPB)]
            bl = jnp.concatenate([jnp.broadcast_to(x, (cl, GLA_QK)) for x in b_last], axis=0)
            qd_d = (q * jnp.exp(b)).astype(BF16)
            qd_scr[d, rows, :] = qd_d
            qd.append(qd_d)
            kd.append((k * jnp.exp(-b)).astype(BF16))
            k2t.append(jnp.transpose(k * jnp.exp(bl - b)).astype(BF16))
            for c in range(GLA_CPB):
                dec_scr[d, j * GLA_CPB + c] = jnp.exp(jnp.transpose(jnp.broadcast_to(b_last[c], (GLA_DV, GLA_QK))))
        for h, (ks, vs) in enumerate(heads):
            att = [jnp.where(keep[d], lax.dot_general(qd[d][:, ks], kd[d][:, ks], NT_DIMS,
                                                      preferred_element_type=F32), 0.0) for d in range(2)]
            oi_scr[rows, vs] = _dot((att[0] + att[1]).astype(BF16), v[:, vs])
            vh = v[:, vs]
            vexp = jnp.concatenate(
                [jnp.concatenate([vh[c * cl:(c + 1) * cl] if c2 == c else zeros_v for c2 in range(GLA_CPB)], axis=1)
                 for c in range(GLA_CPB)], axis=0)
            for d in range(2):
                kv_scr[d, j, h] = _dot(k2t[d][ks, :], vexp)

    for d in range(2):
        s = s0_ref[d] if has_s0 else jnp.zeros((GLA_QK, GLA_DV), F32)
        for cg in (range(nchunk) if d == 0 else range(nchunk - 1, -1, -1)):
            j, c = divmod(cg, GLA_CPB)
            ss_scr[d, cg] = s.astype(BF16)
            kv = jnp.concatenate([kv_scr[d, j, h, :, c * GLA_DV:(c + 1) * GLA_DV] for h in range(GLA_HEADS)], axis=0)
            s = s * dec_scr[d, cg] + kv
        sfin_ref[d] = s

    for cg in range(nchunk):
        rows = slice(cg * cl, (cg + 1) * cl)
        inter = []
        for d in range(2):
            qc = qd_scr[d, rows, :]
            qstack = jnp.concatenate([jnp.where(lane_head == h, qc, jnp.zeros_like(qc)) for h in range(GLA_HEADS)],
                                     axis=0)
            inter.append(_dot(qstack, ss_scr[d, cg]))
        gate = jax.nn.silu(r_ref[rows, :])
        for h, (ks, vs) in enumerate(heads):
            hr = slice(h * cl, (h + 1) * cl)
            oh = oi_scr[rows, vs] + inter[0][hr] + inter[1][hr]
            oh = oh * lax.rsqrt(jnp.mean(oh * oh, axis=-1, keepdims=True) + EPS) * gn_ref[...]
            o_ref[rows, vs] = oh * gate[:, vs]


def _gla_call(z, g, gla_norm, s0, seq_len, nseq, row0, nsub, ada=None, casts=()):
    blk = nsub * seq_len
    assert row0 % blk == 0 and nseq % nsub == 0
    r0 = row0 // blk
    has_s0 = s0 is not None
    qk_off = 0
    v_off = 2 * GLA_QK // GLA_VW
    in_specs = [
        pl.BlockSpec((blk, GLA_QK), lambda i: (r0 + i, qk_off)),
        pl.BlockSpec((blk, GLA_QK), lambda i: (r0 + i, qk_off + 1)),
        pl.BlockSpec((blk, GLA_VW), lambda i: (r0 + i, v_off)),
        pl.BlockSpec((blk, GLA_QK), lambda i: (r0 + i, 0)),
        pl.BlockSpec((blk, GLA_QK), lambda i: (r0 + i, 1)),
        pl.BlockSpec((blk, GLA_VW), lambda i: (r0 + i, v_off + 1)),
        pl.BlockSpec((1, GLA_DV), lambda i: (0, 0)),
    ]
    args = [z, z, z, g, g, z, gla_norm]
    state_spec = pl.BlockSpec((nsub, 2, GLA_QK, GLA_DV), lambda i: (i, 0, 0, 0))
    if has_s0:
        in_specs.append(state_spec)
        args.append(s0)
    out_shape = [jax.ShapeDtypeStruct((nseq * seq_len, GLA_VW), F32),
                 jax.ShapeDtypeStruct((nseq, 2, GLA_QK, GLA_DV), F32)]
    out_specs = [pl.BlockSpec((blk, GLA_VW), lambda i: (i, 0)), state_spec]
    nsteps = nseq // nsub
    if ada is not None:
        cond8, w_ada, b_ada, ada_layer = ada
        ada_in, ada_out = _ada_specs(ada_layer, 6 * D_MODEL // nsteps)
        in_specs += ada_in
        args += [cond8, w_ada, b_ada]
        out_shape.append(ADA_SHAPE)
        out_specs.append(ada_out)
    for w in casts:
        cast_in, cast_out = _cast_specs(w.shape, nsteps)
        in_specs.append(cast_in)
        args.append(w)
        out_shape.append(jax.ShapeDtypeStruct(w.shape[1:], BF16))
        out_specs.append(cast_out)
    return pl.pallas_call(
        functools.partial(_gla_kernel, seq_len=seq_len, nsub=nsub, has_s0=has_s0, has_ada=ada is not None,
                          n_cast=len(casts)),
        out_shape=tuple(out_shape),
        grid=(nsteps,),
        in_specs=in_specs,
        out_specs=tuple(out_specs),
        scratch_shapes=[
            pltpu.VMEM((nsub, seq_len, GLA_VW), F32),
            pltpu.VMEM((nsub, 2, seq_len, GLA_QK), BF16),
            pltpu.VMEM((nsub, 2, seq_len // GLA_BLK, GLA_HEADS, GLA_DK, GLA_CPB * GLA_DV), F32),
            pltpu.VMEM((nsub, 2, seq_len // GLA_CHUNK, GLA_QK, GLA_DV), F32),
            pltpu.VMEM((nsub, 2, seq_len // GLA_CHUNK, GLA_QK, GLA_DV), BF16),
        ],
        compiler_params=pltpu.CompilerParams(vmem_limit_bytes=VMEM_LIMIT),
        name=f"gla_len{seq_len}",
    )(*args)


MLP_CHUNK = 512
MLP_LOAD = 256
MLP_SLOTS = 4


class _MlpWeights:
    def __init__(self, w1_hbm, w2_hbm, w1_scr, w2_scr, stage1, stage2, sem, layer):
        self.refs = (w1_hbm, w2_hbm, w1_scr, w2_scr, stage1, stage2, sem)
        self.layer = layer

    def _copies(self, p):
        w1_hbm, w2_hbm, _, _, stage1, stage2, sem = self.refs
        cols = pl.ds(p * MLP_LOAD, MLP_LOAD)
        slot = p % MLP_SLOTS
        return (pltpu.make_async_copy(w1_hbm.at[self.layer, :, cols], stage1.at[slot], sem.at[0, slot]),
                pltpu.make_async_copy(w2_hbm.at[self.layer, cols, :], stage2.at[slot], sem.at[1, slot]))

    def start(self, p):
        for cp in self._copies(p):
            cp.start()

    def prefetch(self):
        for p in range(MLP_SLOTS - 1):
            self.start(p)

    def finish(self, p):
        _, _, w1_scr, w2_scr, stage1, stage2, _ = self.refs
        ahead = p + MLP_SLOTS - 1
        if ahead < D_FF // MLP_LOAD:
            self.start(ahead)
        for cp in self._copies(p):
            cp.wait()
        cols = slice(p * MLP_LOAD, (p + 1) * MLP_LOAD)
        w1_scr[:, cols] = stage1[p % MLP_SLOTS].astype(BF16)
        w2_scr[cols, :] = stage2[p % MLP_SLOTS].astype(BF16)


def _mlp_tail(x, mix, m_ref, gn2_ref, w1_ref, w2_ref, loading=None):
    y1 = x + m_ref[:, 2 * D_MODEL:3 * D_MODEL] * mix
    h = _norm_mod(y1, gn2_ref[...], m_ref[:, 3 * D_MODEL:4 * D_MODEL], m_ref[:, 4 * D_MODEL:5 * D_MODEL]).astype(BF16)
    nchunk = D_FF // MLP_CHUNK
    acc = jnp.zeros(y1.shape, F32)
    for c in range(nchunk):
        cols = slice(c * MLP_CHUNK, (c + 1) * MLP_CHUNK)
        if loading is not None:
            per = MLP_CHUNK // MLP_LOAD
            for p in range(c * per, (c + 1) * per):
                loading.finish(p)
        a = _dot(h, w1_ref[:, cols])
        a = jnp.square(jnp.maximum(a, 0.0)).astype(BF16)
        acc = acc + _dot(a, w2_ref[cols, :])
    return y1 + m_ref[:, 5 * D_MODEL:6 * D_MODEL] * acc


def _run_tail(x, mix, m_ref, gn2_ref, weights, w1_ref, w2_ref, emit):
    first = pl.program_id(0) == 0

    @pl.when(first)
    def _():
        emit(_mlp_tail(x, mix, m_ref, gn2_ref, w1_ref, w2_ref, loading=weights))

    @pl.when(jnp.logical_not(first))
    def _():
        emit(_mlp_tail(x, mix, m_ref, gn2_ref, w1_ref, w2_ref))


def _even_out_kernel(xp_ref, xs_ref, y5_ref, u_ref, dskip_ref, wglu_ref, bglu_ref, glap_ref, glas_ref, wout_ref,
                     m_ref, gn2_ref, w1_hbm, w2_hbm, o_ref, w1_ref, w2_ref, stage1, stage2, sem, *, layer):
    weights = _MlpWeights(w1_hbm, w2_hbm, w1_ref, w2_ref, stage1, stage2, sem, layer)

    @pl.when(pl.program_id(0) == 0)
    def _():
        weights.prefetch()

    nblk = S5_WIDTH // 128
    ys = (jnp.concatenate([y5_ref[b] for b in range(nblk)], axis=1)
          + jnp.concatenate([u_ref[b] for b in range(nblk)], axis=1) * dskip_ref[...])
    gl = jax.nn.gelu(ys)
    s5o = gl * jax.nn.sigmoid(_dot(gl.astype(BF16), wglu_ref[...]) + bglu_ref[...])
    gla = _token_tile(glap_ref, glas_ref, _OUT_TM).astype(BF16)
    mix = _dot(s5o.astype(BF16), wout_ref[0:S5_WIDTH, :]) + _dot(gla, wout_ref[S5_WIDTH:, :])

    def emit(y):
        o_ref[...] = y

    _run_tail(_token_tile(xp_ref, xs_ref, _OUT_TM), mix, m_ref, gn2_ref, weights, w1_ref, w2_ref, emit)


def _odd_out_kernel(x_ref, attp_ref, atts_ref, wo_ref, m_ref, gn2_ref, w1_hbm, w2_hbm, op_ref, os_ref,
                    w1_ref, w2_ref, stage1, stage2, sem, *, layer):
    weights = _MlpWeights(w1_hbm, w2_hbm, w1_ref, w2_ref, stage1, stage2, sem, layer)

    @pl.when(pl.program_id(0) == 0)
    def _():
        weights.prefetch()

    mix = _dot(_token_tile(attp_ref, atts_ref, _OUT_TM), wo_ref[...])
    is_prompt = pl.program_id(0) < T_PROMPT // _OUT_TM

    def emit(y):
        @pl.when(is_prompt)
        def _():
            op_ref[...] = y

        @pl.when(jnp.logical_not(is_prompt))
        def _():
            os_ref[...] = y

    _run_tail(x_ref[...], mix, m_ref, gn2_ref, weights, w1_ref, w2_ref, emit)


_OUT_TM = 512


def _const_spec(shape):
    return pl.BlockSpec(shape, lambda i: (0,) * len(shape), pipeline_mode=pl.Buffered(1))


def _tail_specs(layer):
    tm = _OUT_TM
    return [
        pl.BlockSpec((None, 1, 6 * D_MODEL), lambda i: (_cond_row(i, tm), 0, 0)),
        _const_spec((1, D_MODEL)),
        pl.BlockSpec(memory_space=pl.ANY),
        pl.BlockSpec(memory_space=pl.ANY),
    ]


def _tail_scratch():
    return [pltpu.VMEM((D_MODEL, D_FF), BF16), pltpu.VMEM((D_FF, D_MODEL), BF16),
            pltpu.VMEM((MLP_SLOTS, D_MODEL, MLP_LOAD), F32), pltpu.VMEM((MLP_SLOTS, MLP_LOAD, D_MODEL), F32),
            pltpu.SemaphoreType.DMA((2, MLP_SLOTS))]


_TAIL_PARAMS = dict(dimension_semantics=("arbitrary",), vmem_limit_bytes=VMEM_LIMIT_TAIL)


def _even_out_call(xp, xs, y5, u, d_skip, w_glu, b_glu, gla_p, gla_s, w_out, mods, layer, gn2, w1, w2):
    tm = _OUT_TM
    return pl.pallas_call(
        functools.partial(_even_out_kernel, layer=layer),
        out_shape=jax.ShapeDtypeStruct((T_TOK, D_MODEL), F32),
        grid=(T_TOK // tm,),
        in_specs=_token_specs(tm) + [
            pl.BlockSpec((S5_WIDTH // 128, tm, 128), lambda i: (0, i, 0)),
            pl.BlockSpec((S5_WIDTH // 128, tm, 128), lambda i: (0, i, 0)),
            _const_spec((1, S5_WIDTH)),
            _const_spec((S5_WIDTH, S5_WIDTH)),
            _const_spec((1, S5_WIDTH)),
        ] + _token_specs(tm, GLA_VW) + [
            _const_spec((S5_WIDTH + GLA_VW, D_MODEL)),
        ] + _tail_specs(layer),
        out_specs=pl.BlockSpec((tm, D_MODEL), lambda i: (i, 0)),
        scratch_shapes=_tail_scratch(),
        compiler_params=pltpu.CompilerParams(**_TAIL_PARAMS),
        name="even_out_mlp",
    )(xp, xs, y5, u, d_skip, w_glu, b_glu, gla_p, gla_s, w_out, mods, gn2, w1, w2)


def _odd_out_call(x, att_p, att_s, w_o, mods, layer, gn2, w1, w2):
    tm = _OUT_TM
    return pl.pallas_call(
        functools.partial(_odd_out_kernel, layer=layer),
        out_shape=(jax.ShapeDtypeStruct((T_PROMPT, D_MODEL), F32),
                   jax.ShapeDtypeStruct((T_SAMPLE, D_MODEL), F32)),
        grid=(T_TOK // tm,),
        in_specs=[pl.BlockSpec((tm, D_MODEL), lambda i: (i, 0))] + _token_specs(tm) + [
            _const_spec((D_MODEL, D_MODEL)),
        ] + _tail_specs(layer),
        out_specs=tuple(_token_specs(tm)),
        scratch_shapes=_tail_scratch(),
        compiler_params=pltpu.CompilerParams(**_TAIL_PARAMS),
        name="odd_out_mlp",
    )(x, att_p, att_s, w_o, mods, gn2, w1, w2)


def _qkv_kernel(x_ref, gn_ref, m_ref, w_ref, qn_ref, kn_ref, cos_ref, sin_ref,
                q_ref, kb_ref, vb_ref, k32_ref, v32_ref, *, tile):
    h = _norm_mod(x_ref[...], gn_ref[...], m_ref[:, 0:D_MODEL], m_ref[:, D_MODEL:2 * D_MODEL]).astype(BF16)
    z = _dot(h, w_ref[...])
    v = z[:, (N_HEADS + KV_HEADS) * HEAD_DIM:]
    vb_ref[...] = v.astype(BF16)
    even_lane = (lax.broadcasted_iota(jnp.int32, (1, HEAD_DIM), 1) & 1) == 0

    def heads(rope):
        for hd in range(N_HEADS + KV_HEADS):
            xh = z[:, hd * HEAD_DIM:(hd + 1) * HEAD_DIM]
            gain = qn_ref[...] if hd < N_HEADS else kn_ref[...]
            xh = xh * lax.rsqrt(jnp.mean(xh * xh, axis=-1, keepdims=True) + EPS) * gain
            if rope:
                partner = jnp.where(even_lane, pltpu.roll(xh, HEAD_DIM - 1, 1), pltpu.roll(xh, 1, 1))
                xh = xh * cos_ref[...] + partner * sin_ref[...]
            if hd < N_HEADS:
                q_ref[:, hd * HEAD_DIM:(hd + 1) * HEAD_DIM] = xh.astype(BF16)
            else:
                cols = slice((hd - N_HEADS) * HEAD_DIM, (hd - N_HEADS + 1) * HEAD_DIM)
                kb_ref[:, cols] = xh.astype(BF16)
                if not rope:
                    k32_ref[:, hd - N_HEADS, :] = xh

    is_sample = pl.program_id(0) >= T_PROMPT // tile

    @pl.when(is_sample)
    def _():
        heads(True)

    @pl.when(jnp.logical_not(is_sample))
    def _():
        heads(False)
        for kh in range(KV_HEADS):
            v32_ref[:, kh, :] = v[:, kh * HEAD_DIM:(kh + 1) * HEAD_DIM]


def _qkv_call(x, gn, mods, layer, w_qkv, q_norm, k_norm, cos_t, sin_t):
    tm = 512
    pos_tiles = DEC_SEQ // tm
    n_prompt = T_PROMPT // tm
    kvw = KV_HEADS * HEAD_DIM

    def pos_map(i):
        return (jnp.maximum(i - n_prompt, 0) % pos_tiles, 0)

    def prompt_map(i):
        return (jnp.minimum(i, n_prompt - 1), 0, 0)

    return pl.pallas_call(
        functools.partial(_qkv_kernel, tile=tm),
        out_shape=(jax.ShapeDtypeStruct((T_TOK, N_HEADS * HEAD_DIM), BF16),
                   jax.ShapeDtypeStruct((T_TOK, kvw), BF16),
                   jax.ShapeDtypeStruct((T_TOK, kvw), BF16),
                   jax.ShapeDtypeStruct((T_PROMPT, KV_HEADS, HEAD_DIM), F32),
                   jax.ShapeDtypeStruct((T_PROMPT, KV_HEADS, HEAD_DIM), F32)),
        grid=(T_TOK // tm,),
        in_specs=[
            pl.BlockSpec((tm, D_MODEL), lambda i: (i, 0)),
            pl.BlockSpec((1, D_MODEL), lambda i: (0, 0)),
            pl.BlockSpec((None, 1, 6 * D_MODEL), lambda i: (_cond_row(i, tm), 0, 0)),
            pl.BlockSpec(w_qkv.shape, lambda i: (0, 0)),
            pl.BlockSpec((1, HEAD_DIM), lambda i: (0, 0)),
            pl.BlockSpec((1, HEAD_DIM), lambda i: (0, 0)),
            pl.BlockSpec((tm, HEAD_DIM), pos_map),
            pl.BlockSpec((tm, HEAD_DIM), pos_map),
        ],
        out_specs=(pl.BlockSpec((tm, N_HEADS * HEAD_DIM), lambda i: (i, 0)),
                   pl.BlockSpec((tm, kvw), lambda i: (i, 0)),
                   pl.BlockSpec((tm, kvw), lambda i: (i, 0)),
                   pl.BlockSpec((tm, KV_HEADS, HEAD_DIM), prompt_map),
                   pl.BlockSpec((tm, KV_HEADS, HEAD_DIM), prompt_map)),
        compiler_params=pltpu.CompilerParams(vmem_limit_bytes=VMEM_LIMIT),
        name="odd_qkv",
    )(x, gn, mods, w_qkv, q_norm, k_norm, cos_t, sin_t)


def _rope_tables():
    f32 = np.float32
    rows = DEC_SEQ // GRID_W
    row = np.repeat(np.arange(rows, dtype=f32), GRID_W)
    col = np.tile(np.arange(GRID_W, dtype=f32), rows)
    inv = np.power(f32(ROPE_THETA), -np.arange(0, AXIS_DIM, 2, dtype=f32) / f32(AXIS_DIM)).astype(f32)
    ang = np.concatenate([row[:, None] * inv, col[:, None] * inv], axis=-1).astype(f32)
    cos_t = np.repeat(np.cos(ang), 2, axis=-1).astype(f32)
    sin = np.sin(ang).astype(f32)
    sin_t = np.stack([-sin, sin], axis=-1).reshape(DEC_SEQ, HEAD_DIM)
    return jnp.asarray(cos_t), jnp.asarray(sin_t)


def _attn_kernel(*refs, has_cache):
    q_ref, k_ref, v_ref = refs[:3]
    ck_ref, cv_ref = refs[3:5] if has_cache else (None, None)
    o_ref = refs[-1]
    c = HEAD_DIM ** -0.5 * math.log2(math.e)
    ones_col = (lax.broadcasted_iota(jnp.int32, (1, HEAD_DIM), 1) == 0).astype(BF16)

    def with_ones(v):
        return jnp.concatenate([v, jnp.broadcast_to(ones_col, v.shape)], axis=1)

    for g in range(k_ref.shape[1] // HEAD_DIM):
        kv_cols = slice(g * HEAD_DIM, (g + 1) * HEAD_DIM)
        k = k_ref[:, kv_cols]
        v = with_ones(v_ref[:, kv_cols])
        if has_cache:
            ck = ck_ref[:, kv_cols].astype(BF16)
            cv = with_ones(cv_ref[:, kv_cols].astype(BF16))
        for r in range(Q_PER_KV):
            cs = slice((g * Q_PER_KV + r) * HEAD_DIM, (g * Q_PER_KV + r + 1) * HEAD_DIM)
            q = q_ref[:, cs]
            s = lax.dot_general(q, k, NT_DIMS, preferred_element_type=F32)
            m = jnp.max(s, axis=-1, keepdims=True)
            if has_cache:
                sc = lax.dot_general(q, ck, NT_DIMS, preferred_element_type=F32)
                m = jnp.maximum(m, jnp.max(sc, axis=-1, keepdims=True))
            mc = m * c
            o = _dot(jnp.exp2(s * c - mc).astype(BF16), v)
            if has_cache:
                o = o + _dot(jnp.exp2(sc * c - mc).astype(BF16), cv)
            o_ref[:, cs] = (o[:, 0:HEAD_DIM] / o[:, HEAD_DIM:HEAD_DIM + 1]).astype(BF16)


def _attn_call(q, k, v, cache_k, cache_v, seq_len, row0, nrows, kv_per_step):
    assert row0 % seq_len == 0 and nrows % seq_len == 0 and KV_HEADS % kv_per_step == 0
    has_cache = cache_k is not None
    b0 = row0 // seq_len
    qw = kv_per_step * Q_PER_KV * HEAD_DIM
    kw = kv_per_step * HEAD_DIM
    in_specs = [
        pl.BlockSpec((seq_len, qw), lambda b, g: (b0 + b, g)),
        pl.BlockSpec((seq_len, kw), lambda b, g: (b0 + b, g)),
        pl.BlockSpec((seq_len, kw), lambda b, g: (b0 + b, g)),
    ]
    args = [q, k, v]
    if has_cache:
        in_specs += [pl.BlockSpec((PAST_LEN, kw), lambda b, g: (b, g)),
                     pl.BlockSpec((PAST_LEN, kw), lambda b, g: (b, g))]
        args += [cache_k, cache_v]
    return pl.pallas_call(
        functools.partial(_attn_kernel, has_cache=has_cache),
        out_shape=jax.ShapeDtypeStruct((nrows, N_HEADS * HEAD_DIM), BF16),
        grid=(nrows // seq_len, KV_HEADS // kv_per_step),
        in_specs=in_specs,
        out_specs=pl.BlockSpec((seq_len, qw), lambda b, g: (b, g)),
        compiler_params=pltpu.CompilerParams(vmem_limit_bytes=VMEM_LIMIT),
        name=f"attn_len{seq_len}",
    )(*args)


def kernel(x_prompt, x_sample, state_s5_re, state_s5_im, state_gla, cache_k, cache_v, c, c_ctx, norm_mix, norm_mlp, w_ada, b_ada, w_mlp_in, w_mlp_out, w_in_e, w_out_e, s5_lambda_re, s5_lambda_im, s5_log_dt, s5_b_re, s5_b_im, s5_c_re, s5_c_im, s5_d, s5_w_glu, s5_b_glu, gla_w_gate2, gla_b_gate, gla_norm, w_qkv_o, w_o_o, q_norm, k_norm):
    xp = x_prompt.reshape(T_PROMPT, D_MODEL)
    xs = x_sample.reshape(T_SAMPLE, D_MODEL)
    cond8 = jnp.concatenate([c_ctx[None, :], c, jnp.zeros((COND_ROWS - 1 - DEC_BATCH, D_MODEL), F32)], axis=0)
    b_ada3 = b_ada.reshape(DEPTH, 1, 6 * D_MODEL)
    w1_all, w2_all = w_mlp_in, w_mlp_out
    mats, mods0 = _s5_prep_call(s5_lambda_re[0], s5_lambda_im[0], s5_log_dt[0], s5_b_re[0], s5_b_im[0],
                                s5_c_re[0], s5_c_im[0], ada=(cond8, w_ada, b_ada3, 0))

    n_main = S5_WIDTH + 2 * GLA_QK + 2 * GLA_VW
    w_in = w_in_e[0]
    w_main = w_in[:, :n_main].astype(BF16)
    w_glr = jnp.pad(w_in[:, n_main:], ((0, 0), (0, 128 - 2 * GLA_RANK))).astype(BF16)
    zg = jnp.zeros((GLA_RANK, GLA_QK), F32)
    w_gate = jnp.concatenate([jnp.concatenate([gla_w_gate2[0, 0], zg], axis=1),
                              jnp.concatenate([zg, gla_w_gate2[0, 1]], axis=1),
                              jnp.zeros((128 - 2 * GLA_RANK, 2 * GLA_QK), F32)], axis=0).astype(BF16)
    b_gate = gla_b_gate[0].reshape(1, 2 * GLA_QK)
    u, z, g = _inproj_call(xp, xs, norm_mix[0:1], mods0, 0, w_main, w_glr, w_gate, b_gate)

    def state_rows(s):
        return jnp.transpose(s, (2, 0, 1, 3)).reshape(S5_GROUPS, DEC_BATCH, 2 * S5_STATE)

    h0 = jnp.concatenate([state_rows(state_s5_re[:, 0]), state_rows(state_s5_im[:, 0])], axis=-1)
    nsteps = S5_GROUPS // S5_GPB
    h0 = jnp.transpose(h0.reshape(nsteps, S5_GPB, DEC_BATCH, S5_W), (0, 2, 1, 3))
    y5, ns = _s5_call(u, mats, h0)
    ns = jnp.transpose(ns, (0, 2, 1, 3)).reshape(S5_GROUPS, BATCH, S5_W)

    def state_out(n):
        return jnp.transpose(n.reshape(S5_GROUPS, BATCH, 2, S5_STATE), (1, 2, 0, 3))[:, None]

    new_s5_re = state_out(ns[:, :, :2 * S5_STATE])
    new_s5_im = state_out(ns[:, :, 2 * S5_STATE:])

    gn_gla = gla_norm[0].reshape(1, GLA_DV)
    gla_p, sfin, w_glu, w_out, w_qkv, w_o = _gla_call(z, g, gn_gla, None, SEQ, BATCH, 0, nsub=4,
                                                      casts=(s5_w_glu, w_out_e, w_qkv_o, w_o_o))
    s0 = state_gla[:, 0].reshape(DEC_BATCH, 2, GLA_QK, GLA_DV)
    gla_s, _, mods1 = _gla_call(z, g, gn_gla, s0, DEC_SEQ, DEC_BATCH, T_PROMPT, nsub=1,
                                ada=(cond8, w_ada, b_ada3, 1))
    new_gla = sfin.reshape(BATCH, 1, 2, GLA_HEADS, GLA_DK, GLA_DV)

    x = _even_out_call(xp, xs, y5, u, s5_d[0].reshape(1, S5_WIDTH), w_glu,
                       s5_b_glu[0].reshape(1, S5_WIDTH), gla_p, gla_s, w_out, mods0, 0,
                       norm_mlp[0:1], w1_all, w2_all)

    cos_t, sin_t = _rope_tables()
    q, k, v, k32, v32 = _qkv_call(x, norm_mix[1:2], mods1, 1, w_qkv,
                                  q_norm[0].reshape(1, HEAD_DIM), k_norm[0].reshape(1, HEAD_DIM), cos_t, sin_t)
    att_p = _attn_call(q, k, v, None, None, SEQ, 0, T_PROMPT, kv_per_step=1)
    ck = cache_k[:, 0].reshape(DEC_BATCH * PAST_LEN, KV_HEADS * HEAD_DIM)
    cv = cache_v[:, 0].reshape(DEC_BATCH * PAST_LEN, KV_HEADS * HEAD_DIM)
    att_s = _attn_call(q, k, v, ck, cv, DEC_SEQ, T_PROMPT, T_SAMPLE, kv_per_step=1)
    yp, ys = _odd_out_call(x, att_p, att_s, w_o, mods1, 1, norm_mlp[1:2],
                           w1_all, w2_all)

    new_k = k32.reshape(BATCH, 1, SEQ, KV_HEADS, HEAD_DIM)
    new_v = v32.reshape(BATCH, 1, SEQ, KV_HEADS, HEAD_DIM)
    y_prompt = yp.reshape(BATCH, SEQ, D_MODEL)
    y_sample = ys.reshape(DEC_BATCH, DEC_SEQ, D_MODEL)
    return (y_prompt, y_sample, new_s5_re, new_s5_im, new_gla, new_k, new_v)
```

```python
import functools
import math

import jax
import jax.numpy as jnp
import numpy as np
from jax import lax
from jax.experimental import pallas as pl
from jax.experimental.pallas import tpu as pltpu

F32 = jnp.float32
BF16 = jnp.bfloat16

LANES = 128

D_MODEL = 1024
BATCH = 16
SEQ = 256
DEPTH = 2
DEC_BATCH = 4
DEC_SEQ = 1024
PAST_LEN = 512
GRID_W = 64
S5_WIDTH = 512
S5_GROUP_CH = 16
S5_GROUPS = 32
S5_STATE = 64
GLA_HEADS = 4
GLA_VW = 512
GLA_DV = 128
GLA_DK = 64
GLA_QK = 256
GLA_RANK = 16
GLA_TAU = 16.0
GLA_CHUNK = 64
GLA_CPB = 4
GLA_BLK = GLA_CPB * GLA_CHUNK
HEAD_DIM = 128
N_HEADS = 8
KV_HEADS = 2
Q_PER_KV = N_HEADS // KV_HEADS
AXIS_DIM = 64
ROPE_THETA = 10000.0
D_FF = 4096
EPS = 1e-6

T_PROMPT = BATCH * SEQ
T_SAMPLE = DEC_BATCH * DEC_SEQ
T_TOK = T_PROMPT + T_SAMPLE
COND_ROWS = 8
COND_SPAN = 1024
PROMPT_SPANS = T_PROMPT // COND_SPAN

S5_Q = 16
S5_W = S5_Q * S5_GROUP_CH
S5_GPB = LANES // S5_GROUP_CH
S5_UBLK = S5_WIDTH // LANES
S5_ROWS = T_TOK // S5_Q
S5_PROMPT_ROWS = T_PROMPT // S5_Q
S5_PROMPT_CHUNKS = SEQ // S5_Q
S5_SAMPLE_CHUNKS = DEC_SEQ // S5_Q

VMEM_LIMIT = 56 * 1024 * 1024
VMEM_LIMIT_TAIL = 60 * 1024 * 1024

NT_DIMS = (((1,), (1,)), ((), ()))


def _cond_row(i, tile):
    return jnp.maximum((i * tile) // COND_SPAN - (PROMPT_SPANS - 1), 0)


def _norm_mod(x, gain, shift, scale):
    y = x * lax.rsqrt(jnp.mean(x * x, axis=-1, keepdims=True) + EPS)
    return (y * gain) * (1.0 + scale) + shift


def _dot(a, b):
    return jnp.dot(a, b, preferred_element_type=F32)


def _ada_kernel(cond_ref, w_ref, b_ref, o_ref):
    s = jax.nn.silu(cond_ref[...]).astype(BF16)
    o_ref[:, 0, :] = _dot(s, w_ref[...].astype(BF16)) + b_ref[...]


ADA_SHAPE = jax.ShapeDtypeStruct((COND_ROWS, 1, 6 * D_MODEL), F32)


def _ada_specs(layer, tn):
    return ([pl.BlockSpec((COND_ROWS, D_MODEL), lambda *ids: (0, 0)),
             pl.BlockSpec((None, D_MODEL, tn), lambda *ids: (layer, 0, ids[-1])),
             pl.BlockSpec((None, 1, tn), lambda *ids: (layer, 0, ids[-1]))],
            pl.BlockSpec((COND_ROWS, 1, tn), lambda *ids: (0, 0, ids[-1])))


def _token_specs(tile, width=D_MODEL):
    n_prompt = T_PROMPT // tile
    return [pl.BlockSpec((tile, width), lambda i: (jnp.minimum(i, n_prompt - 1), 0)),
            pl.BlockSpec((tile, width), lambda i: (jnp.maximum(i - n_prompt, 0), 0))]


def _token_tile(xp_ref, xs_ref, tile):
    return jnp.where(pl.program_id(0) < T_PROMPT // tile, xp_ref[...], xs_ref[...])


def _inproj_kernel(xp_ref, xs_ref, gn_ref, m_ref, w_ref, wglr_ref, wg_ref, bg_ref, u_ref, z_ref, g_ref, *, tile):
    x = _token_tile(xp_ref, xs_ref, tile)
    h = _norm_mod(x, gn_ref[...], m_ref[:, 0:D_MODEL], m_ref[:, D_MODEL:2 * D_MODEL]).astype(BF16)
    z = _dot(h, w_ref[...])
    for blk in range(S5_UBLK):
        u_ref[blk] = z[:, blk * LANES:(blk + 1) * LANES]
    z_ref[...] = z[:, S5_WIDTH:]
    glr = _dot(h, wglr_ref[...]).astype(BF16)
    pre = _dot(glr, wg_ref[...]) + bg_ref[...]
    g_ref[...] = jax.nn.log_sigmoid(pre) * (1.0 / GLA_TAU)


def _inproj_call(xp, xs, gn, mods, layer, w_main, w_glr, w_gate, b_gate):
    tm = 512
    nz = w_main.shape[1]
    return pl.pallas_call(
        functools.partial(_inproj_kernel, tile=tm),
        out_shape=(jax.ShapeDtypeStruct((S5_UBLK, T_TOK, LANES), F32),
                   jax.ShapeDtypeStruct((T_TOK, nz - S5_WIDTH), F32),
                   jax.ShapeDtypeStruct((T_TOK, 2 * GLA_QK), F32)),
        grid=(T_TOK // tm,),
        in_specs=_token_specs(tm) + [
            pl.BlockSpec((1, D_MODEL), lambda i: (0, 0)),
            pl.BlockSpec((None, 1, 6 * D_MODEL), lambda i: (_cond_row(i, tm), 0, 0)),
            pl.BlockSpec((D_MODEL, nz), lambda i: (0, 0)),
            pl.BlockSpec((D_MODEL, LANES), lambda i: (0, 0)),
            pl.BlockSpec((LANES, 2 * GLA_QK), lambda i: (0, 0)),
            pl.BlockSpec((1, 2 * GLA_QK), lambda i: (0, 0)),
        ],
        out_specs=(pl.BlockSpec((S5_UBLK, tm, LANES), lambda i: (0, i, 0)),
                   pl.BlockSpec((tm, nz - S5_WIDTH), lambda i: (i, 0)),
                   pl.BlockSpec((tm, 2 * GLA_QK), lambda i: (i, 0))),
        compiler_params=pltpu.CompilerParams(vmem_limit_bytes=VMEM_LIMIT),
        name="even_inproj",
    )(xp, xs, gn, mods, w_main, w_glr, w_gate, b_gate)


S5_PREP_GPB = 8
_PREP_LRE, _PREP_LIM, _PREP_LDT = 0, 1, 2
_PREP_BT_RE, _PREP_BT_IM, _PREP_C_RE, _PREP_C_IM, _PREP_ROWS = 8, 24, 40, 56, 72


def _s5_prep_kernel(p_ref, cc_ref, cond_ref, wada_ref, bada_ref, t_ref, bq_ref, cqt_ref, be_ref, a_ref, mods_ref,
                    t_scr, dd_scr):
    _ada_kernel(cond_ref, wada_ref, bada_ref, mods_ref)
    for gi in range(S5_PREP_GPB):
        _s5_prep_group(p_ref.at[gi], cc_ref.at[gi], t_ref.at[gi], bq_ref.at[gi], cqt_ref.at[gi], be_ref.at[gi],
                       a_ref.at[gi], t_scr, dd_scr)


def _s5_prep_group(p_ref, cc_ref, t_ref, bq_ref, cqt_ref, be_ref, a_ref, t_scr, dd_scr):
    gch = S5_GROUP_CH
    lre = p_ref[_PREP_LRE:_PREP_LRE + 1]
    lim = p_ref[_PREP_LIM:_PREP_LIM + 1]
    dt = jnp.exp(p_ref[_PREP_LDT:_PREP_LDT + 1])
    a = lre * dt
    th = lim * dt

    def lam_pow(k):
        mag = jnp.exp(k * a)
        return mag * jnp.cos(k * th), mag * jnp.sin(k * th)

    lb_re, lb_im = lam_pow(1.0)
    nr = lb_re - 1.0
    den = lre * lre + lim * lim
    cf_re = (nr * lre + lb_im * lim) / den
    cf_im = (lb_im * lre - nr * lim) / den
    bt_re = p_ref[_PREP_BT_RE:_PREP_BT_RE + gch]
    bt_im = p_ref[_PREP_BT_IM:_PREP_BT_IM + gch]
    bb_re = jnp.tile(cf_re * bt_re - cf_im * bt_im, (S5_Q, 1))
    bb_im = jnp.tile(cf_re * bt_im + cf_im * bt_re, (S5_Q, 1))

    shape = (S5_W, LANES)
    pos = lax.shift_right_logical(lax.broadcasted_iota(jnp.int32, shape, 0), 4)
    is_f = lax.broadcasted_iota(jnp.int32, shape, 1) < S5_STATE
    posq = lax.broadcasted_iota(jnp.int32, (S5_Q, LANES), 0).astype(F32)
    is_fq = lax.broadcasted_iota(jnp.int32, (S5_Q, LANES), 1) < S5_STATE

    def per_channel(tbl):
        return jnp.broadcast_to(tbl[:, None, :], (S5_Q, S5_GROUP_CH, LANES)).reshape(shape)

    p_re, p_im = map(per_channel, lam_pow(jnp.where(is_fq, (S5_Q - 1.0) - posq, posq)))
    w_re = p_re * bb_re - p_im * bb_im
    w_im = p_re * bb_im + p_im * bb_re
    bq = jnp.concatenate([w_re, w_im], axis=1)
    bqt = jnp.transpose(bq)
    bq_ref[...] = bqt.astype(BF16)

    edge = pos == jnp.where(is_f, 0, S5_Q - 1)
    be = jnp.concatenate([jnp.where(edge, bb_re, 0.0), jnp.where(edge, bb_im, 0.0)], axis=1)
    be_ref[...] = jnp.transpose(be).astype(BF16)

    q_re, q_im = map(per_channel, lam_pow(jnp.where(is_fq, posq + 1.0, S5_Q - posq)))
    ct_re = jnp.tile(p_ref[_PREP_C_RE:_PREP_C_RE + gch], (S5_Q, 1))
    ct_im = jnp.tile(p_ref[_PREP_C_IM:_PREP_C_IM + gch], (S5_Q, 1))
    g_re = q_re * ct_re - q_im * ct_im
    g_im = q_re * ct_im + q_im * ct_re
    cqt_ref[...] = jnp.concatenate([g_re, -g_im], axis=1).astype(BF16)

    a_re, a_im = lam_pow(float(S5_Q))
    a_ref[...] = jnp.concatenate([a_re, a_im], axis=1)

    kf = jnp.dot(cc_ref[0:gch], bqt, precision=lax.Precision.HIGHEST, preferred_element_type=F32)
    kb = jnp.dot(cc_ref[gch:2 * gch], bqt, precision=lax.Precision.HIGHEST, preferred_element_type=F32)
    lo = S5_W - gch
    dd_scr[:, 0:S5_W] = kf
    dd_scr[:, lo:lo + S5_W] = kb
    dd_scr[:, lo:S5_W] = kf[:, lo:S5_W] + kb[:, 0:gch]
    for t in range(S5_Q):
        c0 = (S5_Q - 1 - t) * gch
        t_scr[t * gch:(t + 1) * gch, :] = dd_scr[:, c0:c0 + S5_W]
    t_ref[...] = t_scr[...].astype(BF16)


def _s5_prep_call(lam_re, lam_im, log_dt, b_re, b_im, c_re, c_im, ada):
    def fb(p):
        return jnp.transpose(p, (1, 0, 2)).reshape(S5_GROUPS, 1, 2 * S5_STATE)

    def dup(p):
        return jnp.concatenate([p, p], axis=-1)

    ldt = fb(jnp.broadcast_to(log_dt[:, :, None], (2, S5_GROUPS, S5_STATE)))
    pad = jnp.zeros((S5_GROUPS, _PREP_BT_RE - _PREP_LDT - 1, LANES), F32)
    packed = jnp.concatenate([fb(lam_re), fb(lam_im), ldt, pad,
                              dup(jnp.transpose(b_re, (0, 2, 1))), dup(jnp.transpose(b_im, (0, 2, 1))),
                              dup(c_re), dup(c_im)], axis=1)
    zero = jnp.zeros_like(c_re)
    cc = jnp.concatenate([jnp.concatenate([c_re, zero, -c_im, zero], axis=-1),
                          jnp.concatenate([zero, c_re, zero, -c_im], axis=-1)], axis=1)

    gpb = S5_PREP_GPB
    nsteps = S5_GROUPS // gpb
    sq = pl.BlockSpec((gpb, S5_W, S5_W), lambda g: (g, 0, 0))
    sq_shape = jax.ShapeDtypeStruct((S5_GROUPS, S5_W, S5_W), BF16)
    cond8, w_ada, b_ada, ada_layer = ada
    ada_in, ada_out = _ada_specs(ada_layer, 6 * D_MODEL // nsteps)
    *mats, mods = pl.pallas_call(
        _s5_prep_kernel,
        out_shape=(sq_shape, sq_shape, sq_shape, sq_shape,
                   jax.ShapeDtypeStruct((S5_GROUPS, 1, S5_W), F32), ADA_SHAPE),
        grid=(nsteps,),
        in_specs=[pl.BlockSpec((gpb, _PREP_ROWS, LANES), lambda g: (g, 0, 0)),
                  pl.BlockSpec((gpb, 2 * S5_GROUP_CH, S5_W), lambda g: (g, 0, 0))] + ada_in,
        out_specs=(sq, sq, sq, sq, pl.BlockSpec((gpb, 1, S5_W), lambda g: (g, 0, 0)), ada_out),
        scratch_shapes=[pltpu.VMEM((S5_W, S5_W), F32), pltpu.VMEM((S5_GROUP_CH, 2 * S5_W), F32)],
        compiler_params=pltpu.CompilerParams(vmem_limit_bytes=VMEM_LIMIT),
        name="s5_prep",
    )(packed, cc, cond8, w_ada, b_ada)
    return mats, mods


def _s5_kernel(u_ref, tt_ref, bqt_ref, cqt_ref, bet_ref, a_ref, h0_ref, y_ref, ns_ref,
               ut_scr, x_scr, spf_scr, spb_scr, ne_scr, yt_scr, xt_scr):
    gch = S5_GROUP_CH
    for s in range(S5_Q):
        rows = u_ref[pl.ds(s, S5_ROWS, stride=S5_Q), :]
        rows_t = jnp.transpose(rows).astype(BF16)
        for gl in range(S5_GPB):
            ut_scr[gl, s * gch:(s + 1) * gch, :] = rows_t[gl * gch:(gl + 1) * gch, :]

    for gl in range(S5_GPB):
        ut = ut_scr[gl]
        xt_scr[...] = _dot(bqt_ref[gl], ut)
        x = jnp.transpose(xt_scr[...])
        xt_scr[:, 0:S5_PROMPT_ROWS] = _dot(bet_ref[gl], ut[:, 0:S5_PROMPT_ROWS])
        ne = jnp.transpose(xt_scr[:, 0:S5_PROMPT_ROWS])
        for part in range(2):
            x_scr[part, pl.ds(gl, S5_ROWS, stride=S5_GPB), :] = x[:, part * LANES:(part + 1) * LANES]
            ne_scr[part, pl.ds(gl, S5_PROMPT_ROWS, stride=S5_GPB), :] = ne[:, part * LANES:(part + 1) * LANES]

    is_f = lax.broadcasted_iota(jnp.int32, (1, LANES), 1) < S5_STATE
    a_re = a_ref[:, 0:LANES]
    a_im = a_ref[:, LANES:2 * LANES]

    def tile(row):
        return pl.ds(pl.multiple_of(row * S5_GPB, S5_GPB), S5_GPB)

    def scan(base, nseq, nchunk, s_init):
        def body(i, state):
            new = []
            for b in range(nseq):
                s_re, s_im = state[b]
                rows_f = tile(base + b * nchunk + i)
                rows_b = tile(base + b * nchunk + (nchunk - 1 - i))
                spf_scr[0, rows_f, :] = s_re
                spf_scr[1, rows_f, :] = s_im
                spb_scr[0, rows_b, :] = s_re
                spb_scr[1, rows_b, :] = s_im
                x_re = jnp.where(is_f, x_scr[0, rows_f, :], x_scr[0, rows_b, :])
                x_im = jnp.where(is_f, x_scr[1, rows_f, :], x_scr[1, rows_b, :])
                new.append((a_re * s_re - a_im * s_im + x_re, a_re * s_im + a_im * s_re + x_im))
            return tuple(new)

        lax.fori_loop(0, nchunk, body, tuple(s_init))

    zero = jnp.zeros((S5_GPB, LANES), F32)
    scan(0, BATCH, S5_PROMPT_CHUNKS, [(zero, zero)] * BATCH)
    scan(S5_PROMPT_ROWS, DEC_BATCH, S5_SAMPLE_CHUNKS,
         [(h0_ref[b, :, 0:LANES], h0_ref[b, :, LANES:2 * LANES]) for b in range(DEC_BATCH)])

    for b in range(BATCH):
        first = pl.ds(b * S5_PROMPT_CHUNKS * S5_GPB, S5_GPB)
        last = pl.ds(((b + 1) * S5_PROMPT_CHUNKS - 1) * S5_GPB, S5_GPB)
        for part in range(2):
            ns_ref[b, :, part * LANES:(part + 1) * LANES] = jnp.where(is_f, ne_scr[part, first, :], ne_scr[part, last, :])

    for gl in range(S5_GPB):
        rows = pl.ds(gl, S5_ROWS, stride=S5_GPB)
        carried = jnp.concatenate([jnp.where(is_f, spf_scr[p, rows, :], spb_scr[p, rows, :]) for p in range(2)],
                                  axis=1).astype(BF16)
        yt = _dot(tt_ref[gl], ut_scr[gl]) + lax.dot_general(cqt_ref[gl], carried, NT_DIMS,
                                                            preferred_element_type=F32)
        for t in range(S5_Q):
            yt_scr[t, gl * gch:(gl + 1) * gch, :] = yt[t * gch:(t + 1) * gch, :]
    for t in range(S5_Q):
        y_ref[pl.ds(t, S5_ROWS, stride=S5_Q), :] = jnp.transpose(yt_scr[t])


def _s5_call(u, mats, h0):
    tt_m, bqt_m, cqt_m, bet_m, a_m = mats
    nsteps = S5_GROUPS // S5_GPB
    sq = pl.BlockSpec((S5_GPB, S5_W, S5_W), lambda g: (g, 0, 0))
    state_scr = pltpu.VMEM((2, S5_ROWS * S5_GPB, LANES), F32)
    return pl.pallas_call(
        _s5_kernel,
        out_shape=(jax.ShapeDtypeStruct((nsteps, T_TOK, LANES), F32),
                   jax.ShapeDtypeStruct((nsteps, BATCH, S5_GPB, S5_W), F32)),
        grid=(nsteps,),
        in_specs=[
            pl.BlockSpec((None, T_TOK, LANES), lambda g: (g, 0, 0)),
            sq, sq, sq, sq,
            pl.BlockSpec((S5_GPB, S5_W), lambda g: (g, 0)),
            pl.BlockSpec((None, DEC_BATCH, S5_GPB, S5_W), lambda g: (g, 0, 0, 0)),
        ],
        out_specs=(pl.BlockSpec((None, T_TOK, LANES), lambda g: (g, 0, 0)),
                   pl.BlockSpec((None, BATCH, S5_GPB, S5_W), lambda g: (g, 0, 0, 0))),
        scratch_shapes=[pltpu.VMEM((S5_GPB, S5_W, S5_ROWS), BF16), state_scr, state_scr, state_scr,
                        pltpu.VMEM((2, S5_PROMPT_ROWS * S5_GPB, LANES), F32),
                        pltpu.VMEM((S5_Q, LANES, S5_ROWS), F32), pltpu.VMEM((S5_W, S5_ROWS), F32)],
        compiler_params=pltpu.CompilerParams(vmem_limit_bytes=VMEM_LIMIT),
        name="s5_scan",
    )(u, tt_m, bqt_m, cqt_m, bet_m, a_m.reshape(S5_GROUPS, S5_W), h0)


def _split_bf16(x):
    hi = x.astype(BF16)
    r1 = x - hi.astype(F32)
    mid = r1.astype(BF16)
    lo = (r1 - mid.astype(F32)).astype(BF16)
    return hi, mid, lo


def _cast_specs(shape, nsteps):
    _, rows, cols = shape
    rb = rows // nsteps
    return (pl.BlockSpec((None, rb, cols), lambda *ids: (0, ids[-1], 0)),
            pl.BlockSpec((rb, cols), lambda *ids: (ids[-1], 0)))


def _gla_kernel(*refs, seq_len, nsub, has_s0, has_ada, n_cast):
    rows_refs, gn_ref = refs[:6], refs[6]
    n_ada_in = 7 + has_s0
    n_cast_in = n_ada_in + 3 * has_ada
    n_in = n_cast_in + n_cast
    s0_ref = refs[7] if has_s0 else None
    o_ref, sfin_ref = refs[n_in:n_in + 2]
    n_out = 2 + has_ada + n_cast
    scratch = refs[n_in + n_out:]
    if has_ada:
        _ada_kernel(*refs[n_ada_in:n_cast_in], refs[n_in + 2])
    for src, dst in zip(refs[n_cast_in:n_in], refs[n_in + 2 + has_ada:n_in + n_out]):
        dst[...] = src[...].astype(BF16)
    for j in range(nsub):
        rows = pl.ds(j * seq_len, seq_len)
        _gla_sequence(*[r.at[rows, :] for r in rows_refs], gn_ref, s0_ref.at[j] if has_s0 else None,
                      o_ref.at[rows, :], sfin_ref.at[j], *[s.at[j] for s in scratch], seq_len=seq_len)


def _gla_sequence(q_ref, k_ref, v_ref, gf_ref, gb_ref, r_ref, gn_ref, s0_ref, o_ref, sfin_ref,
                  oi_scr, qd_scr, kv_scr, dec_scr, ss_scr, *, seq_len):
    has_s0 = s0_ref is not None
    nblk = seq_len // GLA_BLK
    nchunk = seq_len // GLA_CHUNK
    cl = GLA_CHUNK
    ti = lax.broadcasted_iota(jnp.int32, (GLA_BLK, GLA_BLK), 0)
    si = lax.broadcasted_iota(jnp.int32, (GLA_BLK, GLA_BLK), 1)
    same = lax.shift_right_logical(ti, 6) == lax.shift_right_logical(si, 6)
    keep = (same & (ti >= si), same & (ti <= si))
    tri = tuple(kp.astype(BF16) for kp in keep)
    lane_head = lax.shift_right_logical(lax.broadcasted_iota(jnp.int32, (cl, GLA_QK), 1), 6)
    zeros_v = jnp.zeros((cl, GLA_DV), BF16)
    heads = [(slice(h * GLA_DK, (h + 1) * GLA_DK), slice(h * GLA_DV, (h + 1) * GLA_DV)) for h in range(GLA_HEADS)]

    for j in range(nblk):
        rows = slice(j * GLA_BLK, (j + 1) * GLA_BLK)
        q = q_ref[rows, :] * (GLA_DK ** -0.5)
        k = k_ref[rows, :]
        v = v_ref[rows, :].astype(BF16)
        qd, kd, k2t = [], [], []
        for d, g_ref in enumerate((gf_ref, gb_ref)):
            b = sum(_dot(tri[d], part) for part in _split_bf16(g_ref[rows, :]))
            last = cl - 1 if d == 0 else 0
            b_last = [b[c * cl + last:c * cl + last + 1] for c in range(GLA_CPB)]
            bl = jnp.concatenate([jnp.broadcast_to(x, (cl, GLA_QK)) for x in b_last], axis=0)
            qd_d = (q * jnp.exp(b)).astype(BF16)
            qd_scr[d, rows, :] = qd_d
            qd.append(qd_d)
            kd.append((k * jnp.exp(-b)).astype(BF16))
            k2t.append(jnp.transpose(k * jnp.exp(bl - b)).astype(BF16))
            for c in range(GLA_CPB):
                dec_scr[d, j * GLA_CPB + c] = jnp.exp(jnp.transpose(jnp.broadcast_to(b_last[c], (GLA_DV, GLA_QK))))
        for h, (ks, vs) in enumerate(heads):
            att = [jnp.where(keep[d], lax.dot_general(qd[d][:, ks], kd[d][:, ks], NT_DIMS,
                                                      preferred_element_type=F32), 0.0) for d in range(2)]
            oi_scr[rows, vs] = _dot((att[0] + att[1]).astype(BF16), v[:, vs])
            vh = v[:, vs]
            vexp = jnp.concatenate(
                [jnp.concatenate([vh[c * cl:(c + 1) * cl] if c2 == c else zeros_v for c2 in range(GLA_CPB)], axis=1)
                 for c in range(GLA_CPB)], axis=0)
            for d in range(2):
                kv_scr[d, j, h] = _dot(k2t[d][ks, :], vexp)

    for d in range(2):
        s = s0_ref[d] if has_s0 else jnp.zeros((GLA_QK, GLA_DV), F32)
        for cg in (range(nchunk) if d == 0 else range(nchunk - 1, -1, -1)):
            j, c = divmod(cg, GLA_CPB)
            ss_scr[d, cg] = s.astype(BF16)
            kv = jnp.concatenate([kv_scr[d, j, h, :, c * GLA_DV:(c + 1) * GLA_DV] for h in range(GLA_HEADS)], axis=0)
            s = s * dec_scr[d, cg] + kv
        sfin_ref[d] = s

    for cg in range(nchunk):
        rows = slice(cg * cl, (cg + 1) * cl)
        inter = []
        for d in range(2):
            qc = qd_scr[d, rows, :]
            qstack = jnp.concatenate([jnp.where(lane_head == h, qc, jnp.zeros_like(qc)) for h in range(GLA_HEADS)],
                                     axis=0)
            inter.append(_dot(qstack, ss_scr[d, cg]))
        gate = jax.nn.silu(r_ref[rows, :])
        for h, (ks, vs) in enumerate(heads):
            hr = slice(h * cl, (h + 1) * cl)
            oh = oi_scr[rows, vs] + inter[0][hr] + inter[1][hr]
            oh = oh * lax.rsqrt(jnp.mean(oh * oh, axis=-1, keepdims=True) + EPS) * gn_ref[...]
            o_ref[rows, vs] = oh * gate[:, vs]


def _gla_call(z, g, gla_norm, s0, seq_len, nseq, row0, nsub, ada=None, casts=()):
    blk = nsub * seq_len
    assert row0 % blk == 0 and nseq % nsub == 0
    r0 = row0 // blk
    has_s0 = s0 is not None
    qk_off = 0
    v_off = 2 * GLA_QK // GLA_VW
    in_specs = [
        pl.BlockSpec((blk, GLA_QK), lambda i: (r0 + i, qk_off)),
        pl.BlockSpec((blk, GLA_QK), lambda i: (r0 + i, qk_off + 1)),
        pl.BlockSpec((blk, GLA_VW), lambda i: (r0 + i, v_off)),
        pl.BlockSpec((blk, GLA_QK), lambda i: (r0 + i, 0)),
        pl.BlockSpec((blk, GLA_QK), lambda i: (r0 + i, 1)),
        pl.BlockSpec((blk, GLA_VW), lambda i: (r0 + i, v_off + 1)),
        pl.BlockSpec((1, GLA_DV), lambda i: (0, 0)),
    ]
    args = [z, z, z, g, g, z, gla_norm]
    state_spec = pl.BlockSpec((nsub, 2, GLA_QK, GLA_DV), lambda i: (i, 0, 0, 0))
    if has_s0:
        in_specs.append(state_spec)
        args.append(s0)
    out_shape = [jax.ShapeDtypeStruct((nseq * seq_len, GLA_VW), F32),
                 jax.ShapeDtypeStruct((nseq, 2, GLA_QK, GLA_DV), F32)]
    out_specs = [pl.BlockSpec((blk, GLA_VW), lambda i: (i, 0)), state_spec]
    nsteps = nseq // nsub
    if ada is not None:
        cond8, w_ada, b_ada, ada_layer = ada
        ada_in, ada_out = _ada_specs(ada_layer, 6 * D_MODEL // nsteps)
        in_specs += ada_in
        args += [cond8, w_ada, b_ada]
        out_shape.append(ADA_SHAPE)
        out_specs.append(ada_out)
    for w in casts:
        cast_in, cast_out = _cast_specs(w.shape, nsteps)
        in_specs.append(cast_in)
        args.append(w)
        out_shape.append(jax.ShapeDtypeStruct(w.shape[1:], BF16))
        out_specs.append(cast_out)
    return pl.pallas_call(
        functools.partial(_gla_kernel, seq_len=seq_len, nsub=nsub, has_s0=has_s0, has_ada=ada is not None,
                          n_cast=len(casts)),
        out_shape=tuple(out_shape),
        grid=(nsteps,),
        in_specs=in_specs,
        out_specs=tuple(out_specs),
        scratch_shapes=[
            pltpu.VMEM((nsub, seq_len, GLA_VW), F32),
            pltpu.VMEM((nsub, 2, seq_len, GLA_QK), BF16),
            pltpu.VMEM((nsub, 2, seq_len // GLA_BLK, GLA_HEADS, GLA_DK, GLA_CPB * GLA_DV), F32),
            pltpu.VMEM((nsub, 2, seq_len // GLA_CHUNK, GLA_QK, GLA_DV), F32),
            pltpu.VMEM((nsub, 2, seq_len // GLA_CHUNK, GLA_QK, GLA_DV), BF16),
        ],
        compiler_params=pltpu.CompilerParams(vmem_limit_bytes=VMEM_LIMIT),
        name=f"gla_len{seq_len}",
    )(*args)


MLP_CHUNK = 512
MLP_LOAD = 256
MLP_SLOTS = 4


class _MlpWeights:
    def __init__(self, w1_hbm, w2_hbm, w1_scr, w2_scr, stage1, stage2, sem, layer):
        self.refs = (w1_hbm, w2_hbm, w1_scr, w2_scr, stage1, stage2, sem)
        self.layer = layer

    def _copies(self, p):
        w1_hbm, w2_hbm, _, _, stage1, stage2, sem = self.refs
        cols = pl.ds(p * MLP_LOAD, MLP_LOAD)
        slot = p % MLP_SLOTS
        return (pltpu.make_async_copy(w1_hbm.at[self.layer, :, cols], stage1.at[slot], sem.at[0, slot]),
                pltpu.make_async_copy(w2_hbm.at[self.layer, cols, :], stage2.at[slot], sem.at[1, slot]))

    def start(self, p):
        for cp in self._copies(p):
            cp.start()

    def prefetch(self):
        for p in range(MLP_SLOTS - 1):
            self.start(p)

    def finish(self, p):
        _, _, w1_scr, w2_scr, stage1, stage2, _ = self.refs
        ahead = p + MLP_SLOTS - 1
        if ahead < D_FF // MLP_LOAD:
            self.start(ahead)
        for cp in self._copies(p):
            cp.wait()
        cols = slice(p * MLP_LOAD, (p + 1) * MLP_LOAD)
        w1_scr[:, cols] = stage1[p % MLP_SLOTS].astype(BF16)
        w2_scr[cols, :] = stage2[p % MLP_SLOTS].astype(BF16)


def _mlp_tail(x, mix, m_ref, gn2_ref, w1_ref, w2_ref, loading=None):
    y1 = x + m_ref[:, 2 * D_MODEL:3 * D_MODEL] * mix
    h = _norm_mod(y1, gn2_ref[...], m_ref[:, 3 * D_MODEL:4 * D_MODEL], m_ref[:, 4 * D_MODEL:5 * D_MODEL]).astype(BF16)
    nchunk = D_FF // MLP_CHUNK
    acc = jnp.zeros(y1.shape, F32)
    for c in range(nchunk):
        cols = slice(c * MLP_CHUNK, (c + 1) * MLP_CHUNK)
        if loading is not None:
            per = MLP_CHUNK // MLP_LOAD
            for p in range(c * per, (c + 1) * per):
                loading.finish(p)
        a = _dot(h, w1_ref[:, cols])
        a = jnp.square(jnp.maximum(a, 0.0)).astype(BF16)
        acc = acc + _dot(a, w2_ref[cols, :])
    return y1 + m_ref[:, 5 * D_MODEL:6 * D_MODEL] * acc


def _run_tail(x, mix, m_ref, gn2_ref, weights, w1_ref, w2_ref, emit):
    first = pl.program_id(0) == 0

    @pl.when(first)
    def _():
        emit(_mlp_tail(x, mix, m_ref, gn2_ref, w1_ref, w2_ref, loading=weights))

    @pl.when(jnp.logical_not(first))
    def _():
        emit(_mlp_tail(x, mix, m_ref, gn2_ref, w1_ref, w2_ref))


def _even_out_kernel(xp_ref, xs_ref, y5_ref, u_ref, dskip_ref, wglu_ref, bglu_ref, glap_ref, glas_ref, wout_ref,
                     m_ref, gn2_ref, w1_hbm, w2_hbm, o_ref, w1_ref, w2_ref, stage1, stage2, sem, *, layer):
    weights = _MlpWeights(w1_hbm, w2_hbm, w1_ref, w2_ref, stage1, stage2, sem, layer)

    @pl.when(pl.program_id(0) == 0)
    def _():
        weights.prefetch()

    nblk = S5_UBLK
    ys = (jnp.concatenate([y5_ref[b] for b in range(nblk)], axis=1)
          + jnp.concatenate([u_ref[b] for b in range(nblk)], axis=1) * dskip_ref[...])
    gl = jax.nn.gelu(ys)
    s5o = gl * jax.nn.sigmoid(_dot(gl.astype(BF16), wglu_ref[...]) + bglu_ref[...])
    gla = _token_tile(glap_ref, glas_ref, _OUT_TM).astype(BF16)
    mix = _dot(s5o.astype(BF16), wout_ref[0:S5_WIDTH, :]) + _dot(gla, wout_ref[S5_WIDTH:, :])

    def emit(y):
        o_ref[...] = y

    _run_tail(_token_tile(xp_ref, xs_ref, _OUT_TM), mix, m_ref, gn2_ref, weights, w1_ref, w2_ref, emit)


def _odd_out_kernel(x_ref, attp_ref, atts_ref, wo_ref, m_ref, gn2_ref, w1_hbm, w2_hbm, op_ref, os_ref,
                    w1_ref, w2_ref, stage1, stage2, sem, *, layer):
    weights = _MlpWeights(w1_hbm, w2_hbm, w1_ref, w2_ref, stage1, stage2, sem, layer)

    @pl.when(pl.program_id(0) == 0)
    def _():
        weights.prefetch()

    mix = _dot(_token_tile(attp_ref, atts_ref, _OUT_TM), wo_ref[...])
    is_prompt = pl.program_id(0) < T_PROMPT // _OUT_TM

    def emit(y):
        @pl.when(is_prompt)
        def _():
            op_ref[...] = y

        @pl.when(jnp.logical_not(is_prompt))
        def _():
            os_ref[...] = y

    _run_tail(x_ref[...], mix, m_ref, gn2_ref, weights, w1_ref, w2_ref, emit)


_OUT_TM = 512


def _const_spec(shape):
    return pl.BlockSpec(shape, lambda i: (0,) * len(shape), pipeline_mode=pl.Buffered(1))


def _tail_specs(layer):
    tm = _OUT_TM
    return [
        pl.BlockSpec((None, 1, 6 * D_MODEL), lambda i: (_cond_row(i, tm), 0, 0)),
        _const_spec((1, D_MODEL)),
        pl.BlockSpec(memory_space=pl.ANY),
        pl.BlockSpec(memory_space=pl.ANY),
    ]


def _tail_scratch():
    return [pltpu.VMEM((D_MODEL, D_FF), BF16), pltpu.VMEM((D_FF, D_MODEL), BF16),
            pltpu.VMEM((MLP_SLOTS, D_MODEL, MLP_LOAD), F32), pltpu.VMEM((MLP_SLOTS, MLP_LOAD, D_MODEL), F32),
            pltpu.SemaphoreType.DMA((2, MLP_SLOTS))]


_TAIL_PARAMS = dict(dimension_semantics=("arbitrary",), vmem_limit_bytes=VMEM_LIMIT_TAIL)


def _even_out_call(xp, xs, y5, u, d_skip, w_glu, b_glu, gla_p, gla_s, w_out, mods, layer, gn2, w1, w2):
    tm = _OUT_TM
    return pl.pallas_call(
        functools.partial(_even_out_kernel, layer=layer),
        out_shape=jax.ShapeDtypeStruct((T_TOK, D_MODEL), F32),
        grid=(T_TOK // tm,),
        in_specs=_token_specs(tm) + [
            pl.BlockSpec((S5_UBLK, tm, LANES), lambda i: (0, i, 0)),
            pl.BlockSpec((S5_UBLK, tm, LANES), lambda i: (0, i, 0)),
            _const_spec((1, S5_WIDTH)),
            _const_spec((S5_WIDTH, S5_WIDTH)),
            _const_spec((1, S5_WIDTH)),
        ] + _token_specs(tm, GLA_VW) + [
            _const_spec((S5_WIDTH + GLA_VW, D_MODEL)),
        ] + _tail_specs(layer),
        out_specs=pl.BlockSpec((tm, D_MODEL), lambda i: (i, 0)),
        scratch_shapes=_tail_scratch(),
        compiler_params=pltpu.CompilerParams(**_TAIL_PARAMS),
        name="even_out_mlp",
    )(xp, xs, y5, u, d_skip, w_glu, b_glu, gla_p, gla_s, w_out, mods, gn2, w1, w2)


def _odd_out_call(x, att_p, att_s, w_o, mods, layer, gn2, w1, w2):
    tm = _OUT_TM
    return pl.pallas_call(
        functools.partial(_odd_out_kernel, layer=layer),
        out_shape=(jax.ShapeDtypeStruct((T_PROMPT, D_MODEL), F32),
                   jax.ShapeDtypeStruct((T_SAMPLE, D_MODEL), F32)),
        grid=(T_TOK // tm,),
        in_specs=[pl.BlockSpec((tm, D_MODEL), lambda i: (i, 0))] + _token_specs(tm) + [
            _const_spec((D_MODEL, D_MODEL)),
        ] + _tail_specs(layer),
        out_specs=tuple(_token_specs(tm)),
        scratch_shapes=_tail_scratch(),
        compiler_params=pltpu.CompilerParams(**_TAIL_PARAMS),
        name="odd_out_mlp",
    )(x, att_p, att_s, w_o, mods, gn2, w1, w2)


def _qkv_kernel(x_ref, gn_ref, m_ref, w_ref, qn_ref, kn_ref, cos_ref, sin_ref,
                q_ref, kb_ref, vb_ref, k32_ref, v32_ref, *, tile):
    h = _norm_mod(x_ref[...], gn_ref[...], m_ref[:, 0:D_MODEL], m_ref[:, D_MODEL:2 * D_MODEL]).astype(BF16)
    z = _dot(h, w_ref[...])
    v = z[:, (N_HEADS + KV_HEADS) * HEAD_DIM:]
    vb_ref[...] = v.astype(BF16)
    even_lane = (lax.broadcasted_iota(jnp.int32, (1, HEAD_DIM), 1) & 1) == 0

    def heads(rope):
        for hd in range(N_HEADS + KV_HEADS):
            xh = z[:, hd * HEAD_DIM:(hd + 1) * HEAD_DIM]
            gain = qn_ref[...] if hd < N_HEADS else kn_ref[...]
            xh = xh * lax.rsqrt(jnp.mean(xh * xh, axis=-1, keepdims=True) + EPS) * gain
            if rope:
                partner = jnp.where(even_lane, pltpu.roll(xh, HEAD_DIM - 1, 1), pltpu.roll(xh, 1, 1))
                xh = xh * cos_ref[...] + partner * sin_ref[...]
            if hd < N_HEADS:
                q_ref[:, hd * HEAD_DIM:(hd + 1) * HEAD_DIM] = xh.astype(BF16)
            else:
                cols = slice((hd - N_HEADS) * HEAD_DIM, (hd - N_HEADS + 1) * HEAD_DIM)
                kb_ref[:, cols] = xh.astype(BF16)
                if not rope:
                    k32_ref[:, hd - N_HEADS, :] = xh

    is_sample = pl.program_id(0) >= T_PROMPT // tile

    @pl.when(is_sample)
    def _():
        heads(True)

    @pl.when(jnp.logical_not(is_sample))
    def _():
        heads(False)
        for kh in range(KV_HEADS):
            v32_ref[:, kh, :] = v[:, kh * HEAD_DIM:(kh + 1) * HEAD_DIM]


def _qkv_call(x, gn, mods, layer, w_qkv, q_norm, k_norm, cos_t, sin_t):
    tm = 512
    pos_tiles = DEC_SEQ // tm
    n_prompt = T_PROMPT // tm
    kvw = KV_HEADS * HEAD_DIM

    def pos_map(i):
        return (jnp.maximum(i - n_prompt, 0) % pos_tiles, 0)

    def prompt_map(i):
        return (jnp.minimum(i, n_prompt - 1), 0, 0)

    return pl.pallas_call(
        functools.partial(_qkv_kernel, tile=tm),
        out_shape=(jax.ShapeDtypeStruct((T_TOK, N_HEADS * HEAD_DIM), BF16),
                   jax.ShapeDtypeStruct((T_TOK, kvw), BF16),
                   jax.ShapeDtypeStruct((T_TOK, kvw), BF16),
                   jax.ShapeDtypeStruct((T_PROMPT, KV_HEADS, HEAD_DIM), F32),
                   jax.ShapeDtypeStruct((T_PROMPT, KV_HEADS, HEAD_DIM), F32)),
        grid=(T_TOK // tm,),
        in_specs=[
            pl.BlockSpec((tm, D_MODEL), lambda i: (i, 0)),
            pl.BlockSpec((1, D_MODEL), lambda i: (0, 0)),
            pl.BlockSpec((None, 1, 6 * D_MODEL), lambda i: (_cond_row(i, tm), 0, 0)),
            pl.BlockSpec(w_qkv.shape, lambda i: (0, 0)),
            pl.BlockSpec((1, HEAD_DIM), lambda i: (0, 0)),
            pl.BlockSpec((1, HEAD_DIM), lambda i: (0, 0)),
            pl.BlockSpec((tm, HEAD_DIM), pos_map),
            pl.BlockSpec((tm, HEAD_DIM), pos_map),
        ],
        out_specs=(pl.BlockSpec((tm, N_HEADS * HEAD_DIM), lambda i: (i, 0)),
                   pl.BlockSpec((tm, kvw), lambda i: (i, 0)),
                   pl.BlockSpec((tm, kvw), lambda i: (i, 0)),
                   pl.BlockSpec((tm, KV_HEADS, HEAD_DIM), prompt_map),
                   pl.BlockSpec((tm, KV_HEADS, HEAD_DIM), prompt_map)),
        compiler_params=pltpu.CompilerParams(vmem_limit_bytes=VMEM_LIMIT),
        name="odd_qkv",
    )(x, gn, mods, w_qkv, q_norm, k_norm, cos_t, sin_t)


def _rope_tables():
    f32 = np.float32
    rows = DEC_SEQ // GRID_W
    row = np.repeat(np.arange(rows, dtype=f32), GRID_W)
    col = np.tile(np.arange(GRID_W, dtype=f32), rows)
    inv = np.power(f32(ROPE_THETA), -np.arange(0, AXIS_DIM, 2, dtype=f32) / f32(AXIS_DIM)).astype(f32)
    ang = np.concatenate([row[:, None] * inv, col[:, None] * inv], axis=-1).astype(f32)
    cos_t = np.repeat(np.cos(ang), 2, axis=-1).astype(f32)
    sin = np.sin(ang).astype(f32)
    sin_t = np.stack([-sin, sin], axis=-1).reshape(DEC_SEQ, HEAD_DIM)
    return jnp.asarray(cos_t), jnp.asarray(sin_t)


def _attn_kernel(*refs, seq_len, has_cache):
    q_ref, k_ref, v_ref = refs[:3]
    ck_ref, cv_ref = refs[3:5] if has_cache else (None, None)
    o_ref = refs[-1]
    c = HEAD_DIM ** -0.5 * math.log2(math.e)
    ones_col = (lax.broadcasted_iota(jnp.int32, (1, HEAD_DIM), 1) == 0).astype(BF16)

    def with_ones(v):
        return jnp.concatenate([v, jnp.broadcast_to(ones_col, v.shape)], axis=1)

    if has_cache:
        ck = ck_ref[...].astype(BF16)
        cv = with_ones(cv_ref[...].astype(BF16))
    for j in range(q_ref.shape[0] // seq_len):
        rows = slice(j * seq_len, (j + 1) * seq_len)
        k = k_ref[rows, :]
        v = with_ones(v_ref[rows, :])
        for r in range(Q_PER_KV):
            cs = slice(r * HEAD_DIM, (r + 1) * HEAD_DIM)
            q = q_ref[rows, cs]
            s = lax.dot_general(q, k, NT_DIMS, preferred_element_type=F32)
            m = jnp.max(s, axis=-1, keepdims=True)
            if has_cache:
                sc = lax.dot_general(q, ck, NT_DIMS, preferred_element_type=F32)
                m = jnp.maximum(m, jnp.max(sc, axis=-1, keepdims=True))
            mc = m * c
            o = _dot(jnp.exp2(s * c - mc).astype(BF16), v)
            if has_cache:
                o = o + _dot(jnp.exp2(sc * c - mc).astype(BF16), cv)
            o_ref[rows, cs] = (o[:, 0:HEAD_DIM] / o[:, HEAD_DIM:HEAD_DIM + 1]).astype(BF16)


def _attn_call(q, k, v, cache_k, cache_v, seq_len, row0, nrows, nsub):
    blk = nsub * seq_len
    assert row0 % blk == 0 and nrows % blk == 0
    has_cache = cache_k is not None
    assert not has_cache or nsub == 1
    b0 = row0 // blk
    gw = Q_PER_KV * HEAD_DIM
    in_specs = [
        pl.BlockSpec((blk, gw), lambda b, g: (b0 + b, g)),
        pl.BlockSpec((blk, HEAD_DIM), lambda b, g: (b0 + b, g)),
        pl.BlockSpec((blk, HEAD_DIM), lambda b, g: (b0 + b, g)),
    ]
    args = [q, k, v]
    if has_cache:
        in_specs += [pl.BlockSpec((PAST_LEN, HEAD_DIM), lambda b, g: (b, g)),
                     pl.BlockSpec((PAST_LEN, HEAD_DIM), lambda b, g: (b, g))]
        args += [cache_k, cache_v]
    return pl.pallas_call(
        functools.partial(_attn_kernel, seq_len=seq_len, has_cache=has_cache),
        out_shape=jax.ShapeDtypeStruct((nrows, N_HEADS * HEAD_DIM), BF16),
        grid=(nrows // blk, KV_HEADS),
        in_specs=in_specs,
        out_specs=pl.BlockSpec((blk, gw), lambda b, g: (b, g)),
        compiler_params=pltpu.CompilerParams(vmem_limit_bytes=VMEM_LIMIT),
        name=f"attn_len{seq_len}",
    )(*args)


def kernel(x_prompt, x_sample, state_s5_re, state_s5_im, state_gla, cache_k, cache_v, c, c_ctx, norm_mix, norm_mlp, w_ada, b_ada, w_mlp_in, w_mlp_out, w_in_e, w_out_e, s5_lambda_re, s5_lambda_im, s5_log_dt, s5_b_re, s5_b_im, s5_c_re, s5_c_im, s5_d, s5_w_glu, s5_b_glu, gla_w_gate2, gla_b_gate, gla_norm, w_qkv_o, w_o_o, q_norm, k_norm):
    xp = x_prompt.reshape(T_PROMPT, D_MODEL)
    xs = x_sample.reshape(T_SAMPLE, D_MODEL)
    cond8 = jnp.concatenate([c_ctx[None, :], c, jnp.zeros((COND_ROWS - 1 - DEC_BATCH, D_MODEL), F32)], axis=0)
    b_ada3 = b_ada.reshape(DEPTH, 1, 6 * D_MODEL)
    w1_all, w2_all = w_mlp_in, w_mlp_out
    mats, mods0 = _s5_prep_call(s5_lambda_re[0], s5_lambda_im[0], s5_log_dt[0], s5_b_re[0], s5_b_im[0],
                                s5_c_re[0], s5_c_im[0], ada=(cond8, w_ada, b_ada3, 0))

    n_main = S5_WIDTH + 2 * GLA_QK + 2 * GLA_VW
    w_in = w_in_e[0]
    w_main = w_in[:, :n_main].astype(BF16)
    w_glr = jnp.pad(w_in[:, n_main:], ((0, 0), (0, LANES - 2 * GLA_RANK))).astype(BF16)
    zg = jnp.zeros((GLA_RANK, GLA_QK), F32)
    w_gate = jnp.concatenate([jnp.concatenate([gla_w_gate2[0, 0], zg], axis=1),
                              jnp.concatenate([zg, gla_w_gate2[0, 1]], axis=1),
                              jnp.zeros((LANES - 2 * GLA_RANK, 2 * GLA_QK), F32)], axis=0).astype(BF16)
    b_gate = gla_b_gate[0].reshape(1, 2 * GLA_QK)
    u, z, g = _inproj_call(xp, xs, norm_mix[0:1], mods0, 0, w_main, w_glr, w_gate, b_gate)

    def state_rows(s):
        return jnp.transpose(s, (2, 0, 1, 3)).reshape(S5_GROUPS, DEC_BATCH, 2 * S5_STATE)

    h0 = jnp.concatenate([state_rows(state_s5_re[:, 0]), state_rows(state_s5_im[:, 0])], axis=-1)
    nsteps = S5_GROUPS // S5_GPB
    h0 = jnp.transpose(h0.reshape(nsteps, S5_GPB, DEC_BATCH, S5_W), (0, 2, 1, 3))
    y5, ns = _s5_call(u, mats, h0)
    ns = jnp.transpose(ns, (0, 2, 1, 3)).reshape(S5_GROUPS, BATCH, S5_W)

    def state_out(n):
        return jnp.transpose(n.reshape(S5_GROUPS, BATCH, 2, S5_STATE), (1, 2, 0, 3))[:, None]

    new_s5_re = state_out(ns[:, :, :2 * S5_STATE])
    new_s5_im = state_out(ns[:, :, 2 * S5_STATE:])

    gn_gla = gla_norm[0].reshape(1, GLA_DV)
    gla_p, sfin, w_glu, w_out, w_qkv, w_o = _gla_call(z, g, gn_gla, None, SEQ, BATCH, 0, nsub=4,
                                                      casts=(s5_w_glu, w_out_e, w_qkv_o, w_o_o))
    s0 = state_gla[:, 0].reshape(DEC_BATCH, 2, GLA_QK, GLA_DV)
    gla_s, _, mods1 = _gla_call(z, g, gn_gla, s0, DEC_SEQ, DEC_BATCH, T_PROMPT, nsub=1,
                                ada=(cond8, w_ada, b_ada3, 1))
    new_gla = sfin.reshape(BATCH, 1, 2, GLA_HEADS, GLA_DK, GLA_DV)

    x = _even_out_call(xp, xs, y5, u, s5_d[0].reshape(1, S5_WIDTH), w_glu,
                       s5_b_glu[0].reshape(1, S5_WIDTH), gla_p, gla_s, w_out, mods0, 0,
                       norm_mlp[0:1], w1_all, w2_all)

    cos_t, sin_t = _rope_tables()
    q, k, v, k32, v32 = _qkv_call(x, norm_mix[1:2], mods1, 1, w_qkv,
                                  q_norm[0].reshape(1, HEAD_DIM), k_norm[0].reshape(1, HEAD_DIM), cos_t, sin_t)
    att_p = _attn_call(q, k, v, None, None, SEQ, 0, T_PROMPT, nsub=1)
    ck = cache_k[:, 0].reshape(DEC_BATCH * PAST_LEN, KV_HEADS * HEAD_DIM)
    cv = cache_v[:, 0].reshape(DEC_BATCH * PAST_LEN, KV_HEADS * HEAD_DIM)
    att_s = _attn_call(q, k, v, ck, cv, DEC_SEQ, T_PROMPT, T_SAMPLE, nsub=1)
    yp, ys = _odd_out_call(x, att_p, att_s, w_o, mods1, 1, norm_mlp[1:2],
                           w1_all, w2_all)

    new_k = k32.reshape(BATCH, 1, SEQ, KV_HEADS, HEAD_DIM)
    new_v = v32.reshape(BATCH, 1, SEQ, KV_HEADS, HEAD_DIM)
    y_prompt = yp.reshape(BATCH, SEQ, D_MODEL)
    y_sample = ys.reshape(DEC_BATCH, DEC_SEQ, D_MODEL)
    return (y_prompt, y_sample, new_s5_re, new_s5_im, new_gla, new_k, new_v)
```

```python
import functools
import math

import jax
import jax.numpy as jnp
import numpy as np
from jax import lax
from jax.experimental import pallas as pl
from jax.experimental.pallas import tpu as pltpu

F32 = jnp.float32
BF16 = jnp.bfloat16

LANES = 128

D_MODEL = 1024
BATCH = 16
SEQ = 256
DEPTH = 2
DEC_BATCH = 4
DEC_SEQ = 1024
PAST_LEN = 512
GRID_W = 64
S5_WIDTH = 512
S5_GROUP_CH = 16
S5_GROUPS = 32
S5_STATE = 64
GLA_HEADS = 4
GLA_VW = 512
GLA_DV = 128
GLA_DK = 64
GLA_QK = 256
GLA_RANK = 16
GLA_TAU = 16.0
GLA_CHUNK = 64
GLA_CPB = 4
GLA_BLK = GLA_CPB * GLA_CHUNK
HEAD_DIM = 128
N_HEADS = 8
KV_HEADS = 2
Q_PER_KV = N_HEADS // KV_HEADS
AXIS_DIM = 64
ROPE_THETA = 10000.0
D_FF = 4096
EPS = 1e-6

T_PROMPT = BATCH * SEQ
T_SAMPLE = DEC_BATCH * DEC_SEQ
T_TOK = T_PROMPT + T_SAMPLE
COND_ROWS = 8
COND_SPAN = 1024
PROMPT_SPANS = T_PROMPT // COND_SPAN

S5_Q = 16
S5_W = S5_Q * S5_GROUP_CH
S5_GPB = LANES // S5_GROUP_CH
S5_UBLK = S5_WIDTH // LANES
S5_ROWS = T_TOK // S5_Q
S5_PROMPT_ROWS = T_PROMPT // S5_Q
S5_PROMPT_CHUNKS = SEQ // S5_Q
S5_SAMPLE_CHUNKS = DEC_SEQ // S5_Q

VMEM_LIMIT = 56 * 1024 * 1024
VMEM_LIMIT_TAIL = 60 * 1024 * 1024

NT_DIMS = (((1,), (1,)), ((), ()))


def _cond_row(i, tile):
    return jnp.maximum((i * tile) // COND_SPAN - (PROMPT_SPANS - 1), 0)


def _norm_mod(x, gain, shift, scale):
    y = x * lax.rsqrt(jnp.mean(x * x, axis=-1, keepdims=True) + EPS)
    return (y * gain) * (1.0 + scale) + shift


def _dot(a, b):
    return jnp.dot(a, b, preferred_element_type=F32)


def _ada_kernel(cond_ref, w_ref, b_ref, o_ref):
    s = jax.nn.silu(cond_ref[...]).astype(BF16)
    o_ref[:, 0, :] = _dot(s, w_ref[...].astype(BF16)) + b_ref[...]


ADA_SHAPE = jax.ShapeDtypeStruct((COND_ROWS, 1, 6 * D_MODEL), F32)


def _ada_specs(layer, tn):
    return ([pl.BlockSpec((COND_ROWS, D_MODEL), lambda *ids: (0, 0)),
             pl.BlockSpec((None, D_MODEL, tn), lambda *ids: (layer, 0, ids[-1])),
             pl.BlockSpec((None, 1, tn), lambda *ids: (layer, 0, ids[-1]))],
            pl.BlockSpec((COND_ROWS, 1, tn), lambda *ids: (0, 0, ids[-1])))


def _token_specs(tile, width=D_MODEL):
    n_prompt = T_PROMPT // tile
    return [pl.BlockSpec((tile, width), lambda i: (jnp.minimum(i, n_prompt - 1), 0)),
            pl.BlockSpec((tile, width), lambda i: (jnp.maximum(i - n_prompt, 0), 0))]


def _token_tile(xp_ref, xs_ref, tile):
    return jnp.where(pl.program_id(0) < T_PROMPT // tile, xp_ref[...], xs_ref[...])


def _inproj_kernel(xp_ref, xs_ref, gn_ref, m_ref, w_ref, wglr_ref, wg_ref, bg_ref, u_ref, z_ref, g_ref, *, tile):
    x = _token_tile(xp_ref, xs_ref, tile)
    h = _norm_mod(x, gn_ref[...], m_ref[:, 0:D_MODEL], m_ref[:, D_MODEL:2 * D_MODEL]).astype(BF16)
    z = _dot(h, w_ref[...])
    for blk in range(S5_UBLK):
        u_ref[blk] = z[:, blk * LANES:(blk + 1) * LANES]
    z_ref[...] = z[:, S5_WIDTH:]
    glr = _dot(h, wglr_ref[...]).astype(BF16)
    pre = _dot(glr, wg_ref[...]) + bg_ref[...]
    g_ref[...] = jax.nn.log_sigmoid(pre) * (1.0 / GLA_TAU)


def _inproj_call(xp, xs, gn, mods, layer, w_in, w_gate, b_gate):
    tm = 512
    nz = w_in.shape[1] - LANES
    return pl.pallas_call(
        functools.partial(_inproj_kernel, tile=tm),
        out_shape=(jax.ShapeDtypeStruct((S5_UBLK, T_TOK, LANES), F32),
                   jax.ShapeDtypeStruct((T_TOK, nz - S5_WIDTH), F32),
                   jax.ShapeDtypeStruct((T_TOK, 2 * GLA_QK), F32)),
        grid=(T_TOK // tm,),
        in_specs=_token_specs(tm) + [
            pl.BlockSpec((1, D_MODEL), lambda i: (0, 0)),
            pl.BlockSpec((None, 1, 6 * D_MODEL), lambda i: (_cond_row(i, tm), 0, 0)),
            pl.BlockSpec((D_MODEL, nz), lambda i: (0, 0)),
            pl.BlockSpec((D_MODEL, LANES), lambda i: (0, nz // LANES)),
            pl.BlockSpec((LANES, 2 * GLA_QK), lambda i: (0, 0)),
            pl.BlockSpec((1, 2 * GLA_QK), lambda i: (0, 0)),
        ],
        out_specs=(pl.BlockSpec((S5_UBLK, tm, LANES), lambda i: (0, i, 0)),
                   pl.BlockSpec((tm, nz - S5_WIDTH), lambda i: (i, 0)),
                   pl.BlockSpec((tm, 2 * GLA_QK), lambda i: (i, 0))),
        compiler_params=pltpu.CompilerParams(vmem_limit_bytes=VMEM_LIMIT),
        name="even_inproj",
    )(xp, xs, gn, mods, w_in, w_in, w_gate, b_gate)


S5_PREP_GPB = 8
_PREP_LRE, _PREP_LIM, _PREP_LDT = 0, 1, 2
_PREP_BT_RE, _PREP_BT_IM, _PREP_C_RE, _PREP_C_IM, _PREP_ROWS = 8, 24, 40, 56, 72


def _s5_prep_kernel(p_ref, cc_ref, cond_ref, wada_ref, bada_ref, t_ref, bq_ref, cqt_ref, be_ref, a_ref, mods_ref,
                    t_scr, dd_scr):
    _ada_kernel(cond_ref, wada_ref, bada_ref, mods_ref)
    for gi in range(S5_PREP_GPB):
        _s5_prep_group(p_ref.at[gi], cc_ref.at[gi], t_ref.at[gi], bq_ref.at[gi], cqt_ref.at[gi], be_ref.at[gi],
                       a_ref.at[gi], t_scr, dd_scr)


def _s5_prep_group(p_ref, cc_ref, t_ref, bq_ref, cqt_ref, be_ref, a_ref, t_scr, dd_scr):
    gch = S5_GROUP_CH
    lre = p_ref[_PREP_LRE:_PREP_LRE + 1]
    lim = p_ref[_PREP_LIM:_PREP_LIM + 1]
    dt = jnp.exp(p_ref[_PREP_LDT:_PREP_LDT + 1])
    a = lre * dt
    th = lim * dt

    def lam_pow(k):
        mag = jnp.exp(k * a)
        return mag * jnp.cos(k * th), mag * jnp.sin(k * th)

    lb_re, lb_im = lam_pow(1.0)
    nr = lb_re - 1.0
    den = lre * lre + lim * lim
    cf_re = (nr * lre + lb_im * lim) / den
    cf_im = (lb_im * lre - nr * lim) / den
    bt_re = p_ref[_PREP_BT_RE:_PREP_BT_RE + gch]
    bt_im = p_ref[_PREP_BT_IM:_PREP_BT_IM + gch]
    bb_re = jnp.tile(cf_re * bt_re - cf_im * bt_im, (S5_Q, 1))
    bb_im = jnp.tile(cf_re * bt_im + cf_im * bt_re, (S5_Q, 1))

    shape = (S5_W, LANES)
    pos = lax.shift_right_logical(lax.broadcasted_iota(jnp.int32, shape, 0), 4)
    is_f = lax.broadcasted_iota(jnp.int32, shape, 1) < S5_STATE
    posq = lax.broadcasted_iota(jnp.int32, (S5_Q, LANES), 0).astype(F32)
    is_fq = lax.broadcasted_iota(jnp.int32, (S5_Q, LANES), 1) < S5_STATE

    def per_channel(tbl):
        return jnp.broadcast_to(tbl[:, None, :], (S5_Q, S5_GROUP_CH, LANES)).reshape(shape)

    p_re, p_im = map(per_channel, lam_pow(jnp.where(is_fq, (S5_Q - 1.0) - posq, posq)))
    w_re = p_re * bb_re - p_im * bb_im
    w_im = p_re * bb_im + p_im * bb_re
    bq = jnp.concatenate([w_re, w_im], axis=1)
    bqt = jnp.transpose(bq)
    bq_ref[...] = bqt.astype(BF16)

    edge = pos == jnp.where(is_f, 0, S5_Q - 1)
    be = jnp.concatenate([jnp.where(edge, bb_re, 0.0), jnp.where(edge, bb_im, 0.0)], axis=1)
    be_ref[...] = jnp.transpose(be).astype(BF16)

    q_re, q_im = map(per_channel, lam_pow(jnp.where(is_fq, posq + 1.0, S5_Q - posq)))
    ct_re = jnp.tile(p_ref[_PREP_C_RE:_PREP_C_RE + gch], (S5_Q, 1))
    ct_im = jnp.tile(p_ref[_PREP_C_IM:_PREP_C_IM + gch], (S5_Q, 1))
    g_re = q_re * ct_re - q_im * ct_im
    g_im = q_re * ct_im + q_im * ct_re
    cqt_ref[...] = jnp.concatenate([g_re, -g_im], axis=1).astype(BF16)

    a_re, a_im = lam_pow(float(S5_Q))
    a_ref[...] = jnp.concatenate([a_re, a_im], axis=1)

    kf = jnp.dot(cc_ref[0:gch], bqt, precision=lax.Precision.HIGHEST, preferred_element_type=F32)
    kb = jnp.dot(cc_ref[gch:2 * gch], bqt, precision=lax.Precision.HIGHEST, preferred_element_type=F32)
    lo = S5_W - gch
    dd_scr[:, 0:S5_W] = kf
    dd_scr[:, lo:lo + S5_W] = kb
    dd_scr[:, lo:S5_W] = kf[:, lo:S5_W] + kb[:, 0:gch]
    for t in range(S5_Q):
        c0 = (S5_Q - 1 - t) * gch
        t_scr[t * gch:(t + 1) * gch, :] = dd_scr[:, c0:c0 + S5_W]
    t_ref[...] = t_scr[...].astype(BF16)


def _s5_prep_call(lam_re, lam_im, log_dt, b_re, b_im, c_re, c_im, ada):
    per_dir = jnp.stack([lam_re, lam_im, jnp.broadcast_to(log_dt[:, :, None], lam_re.shape)])
    head = jnp.transpose(per_dir, (2, 0, 1, 3)).reshape(S5_GROUPS, 3, 2 * S5_STATE)
    pad = jnp.zeros((S5_GROUPS, _PREP_BT_RE - _PREP_LDT - 1, LANES), F32)
    shared = jnp.concatenate([jnp.transpose(b_re, (0, 2, 1)), jnp.transpose(b_im, (0, 2, 1)), c_re, c_im],
                             axis=1)
    packed = jnp.concatenate([head, pad, jnp.concatenate([shared, shared], axis=-1)], axis=1)
    zero = jnp.zeros_like(c_re)
    cc = jnp.concatenate([jnp.concatenate([c_re, zero, -c_im, zero], axis=-1),
                          jnp.concatenate([zero, c_re, zero, -c_im], axis=-1)], axis=1)

    gpb = S5_PREP_GPB
    nsteps = S5_GROUPS // gpb
    sq = pl.BlockSpec((gpb, S5_W, S5_W), lambda g: (g, 0, 0))
    sq_shape = jax.ShapeDtypeStruct((S5_GROUPS, S5_W, S5_W), BF16)
    cond8, w_ada, b_ada, ada_layer = ada
    ada_in, ada_out = _ada_specs(ada_layer, 6 * D_MODEL // nsteps)
    *mats, mods = pl.pallas_call(
        _s5_prep_kernel,
        out_shape=(sq_shape, sq_shape, sq_shape, sq_shape,
                   jax.ShapeDtypeStruct((S5_GROUPS, 1, S5_W), F32), ADA_SHAPE),
        grid=(nsteps,),
        in_specs=[pl.BlockSpec((gpb, _PREP_ROWS, LANES), lambda g: (g, 0, 0)),
                  pl.BlockSpec((gpb, 2 * S5_GROUP_CH, S5_W), lambda g: (g, 0, 0))] + ada_in,
        out_specs=(sq, sq, sq, sq, pl.BlockSpec((gpb, 1, S5_W), lambda g: (g, 0, 0)), ada_out),
        scratch_shapes=[pltpu.VMEM((S5_W, S5_W), F32), pltpu.VMEM((S5_GROUP_CH, 2 * S5_W), F32)],
        compiler_params=pltpu.CompilerParams(vmem_limit_bytes=VMEM_LIMIT),
        name="s5_prep",
    )(packed, cc, cond8, w_ada, b_ada)
    return mats, mods


def _s5_kernel(u_ref, tt_ref, bqt_ref, cqt_ref, bet_ref, a_ref, h0_ref, y_ref, ns_ref,
               ut_scr, x_scr, spf_scr, spb_scr, ne_scr, yt_scr, xt_scr):
    gch = S5_GROUP_CH
    for s in range(S5_Q):
        rows = u_ref[pl.ds(s, S5_ROWS, stride=S5_Q), :]
        rows_t = jnp.transpose(rows).astype(BF16)
        for gl in range(S5_GPB):
            ut_scr[gl, s * gch:(s + 1) * gch, :] = rows_t[gl * gch:(gl + 1) * gch, :]

    for gl in range(S5_GPB):
        ut = ut_scr[gl]
        xt_scr[...] = _dot(bqt_ref[gl], ut)
        x = jnp.transpose(xt_scr[...])
        xt_scr[:, 0:S5_PROMPT_ROWS] = _dot(bet_ref[gl], ut[:, 0:S5_PROMPT_ROWS])
        ne = jnp.transpose(xt_scr[:, 0:S5_PROMPT_ROWS])
        for part in range(2):
            x_scr[part, pl.ds(gl, S5_ROWS, stride=S5_GPB), :] = x[:, part * LANES:(part + 1) * LANES]
            ne_scr[part, pl.ds(gl, S5_PROMPT_ROWS, stride=S5_GPB), :] = ne[:, part * LANES:(part + 1) * LANES]

    is_f = lax.broadcasted_iota(jnp.int32, (1, LANES), 1) < S5_STATE
    a_re = a_ref[:, 0:LANES]
    a_im = a_ref[:, LANES:2 * LANES]

    def tile(row):
        return pl.ds(pl.multiple_of(row * S5_GPB, S5_GPB), S5_GPB)

    def scan(base, nseq, nchunk, s_init):
        def body(i, state):
            new = []
            for b in range(nseq):
                s_re, s_im = state[b]
                rows_f = tile(base + b * nchunk + i)
                rows_b = tile(base + b * nchunk + (nchunk - 1 - i))
                spf_scr[0, rows_f, :] = s_re
                spf_scr[1, rows_f, :] = s_im
                spb_scr[0, rows_b, :] = s_re
                spb_scr[1, rows_b, :] = s_im
                x_re = jnp.where(is_f, x_scr[0, rows_f, :], x_scr[0, rows_b, :])
                x_im = jnp.where(is_f, x_scr[1, rows_f, :], x_scr[1, rows_b, :])
                new.append((a_re * s_re - a_im * s_im + x_re, a_re * s_im + a_im * s_re + x_im))
            return tuple(new)

        lax.fori_loop(0, nchunk, body, tuple(s_init))

    zero = jnp.zeros((S5_GPB, LANES), F32)
    scan(0, BATCH, S5_PROMPT_CHUNKS, [(zero, zero)] * BATCH)
    scan(S5_PROMPT_ROWS, DEC_BATCH, S5_SAMPLE_CHUNKS,
         [(h0_ref[b, :, 0:LANES], h0_ref[b, :, LANES:2 * LANES]) for b in range(DEC_BATCH)])

    for b in range(BATCH):
        first = pl.ds(b * S5_PROMPT_CHUNKS * S5_GPB, S5_GPB)
        last = pl.ds(((b + 1) * S5_PROMPT_CHUNKS - 1) * S5_GPB, S5_GPB)
        for part in range(2):
            ns_ref[b, :, part * LANES:(part + 1) * LANES] = jnp.where(is_f, ne_scr[part, first, :], ne_scr[part, last, :])

    for gl in range(S5_GPB):
        rows = pl.ds(gl, S5_ROWS, stride=S5_GPB)
        carried = jnp.concatenate([jnp.where(is_f, spf_scr[p, rows, :], spb_scr[p, rows, :]) for p in range(2)],
                                  axis=1).astype(BF16)
        yt = _dot(tt_ref[gl], ut_scr[gl]) + lax.dot_general(cqt_ref[gl], carried, NT_DIMS,
                                                            preferred_element_type=F32)
        for t in range(S5_Q):
            yt_scr[t, gl * gch:(gl + 1) * gch, :] = yt[t * gch:(t + 1) * gch, :]
    for t in range(S5_Q):
        y_ref[pl.ds(t, S5_ROWS, stride=S5_Q), :] = jnp.transpose(yt_scr[t])


def _s5_call(u, mats, h0):
    tt_m, bqt_m, cqt_m, bet_m, a_m = mats
    nsteps = S5_GROUPS // S5_GPB
    sq = pl.BlockSpec((S5_GPB, S5_W, S5_W), lambda g: (g, 0, 0))
    state_scr = pltpu.VMEM((2, S5_ROWS * S5_GPB, LANES), F32)
    return pl.pallas_call(
        _s5_kernel,
        out_shape=(jax.ShapeDtypeStruct((nsteps, T_TOK, LANES), F32),
                   jax.ShapeDtypeStruct((nsteps, BATCH, S5_GPB, S5_W), F32)),
        grid=(nsteps,),
        in_specs=[
            pl.BlockSpec((None, T_TOK, LANES), lambda g: (g, 0, 0)),
            sq, sq, sq, sq,
            pl.BlockSpec((S5_GPB, S5_W), lambda g: (g, 0)),
            pl.BlockSpec((None, DEC_BATCH, S5_GPB, S5_W), lambda g: (g, 0, 0, 0)),
        ],
        out_specs=(pl.BlockSpec((None, T_TOK, LANES), lambda g: (g, 0, 0)),
                   pl.BlockSpec((None, BATCH, S5_GPB, S5_W), lambda g: (g, 0, 0, 0))),
        scratch_shapes=[pltpu.VMEM((S5_GPB, S5_W, S5_ROWS), BF16), state_scr, state_scr, state_scr,
                        pltpu.VMEM((2, S5_PROMPT_ROWS * S5_GPB, LANES), F32),
                        pltpu.VMEM((S5_Q, LANES, S5_ROWS), F32), pltpu.VMEM((S5_W, S5_ROWS), F32)],
        compiler_params=pltpu.CompilerParams(vmem_limit_bytes=VMEM_LIMIT),
        name="s5_scan",
    )(u, tt_m, bqt_m, cqt_m, bet_m, a_m.reshape(S5_GROUPS, S5_W), h0)


def _split_bf16(x):
    hi = x.astype(BF16)
    r1 = x - hi.astype(F32)
    mid = r1.astype(BF16)
    lo = (r1 - mid.astype(F32)).astype(BF16)
    return hi, mid, lo


def _cast_specs(shape, nsteps):
    _, rows, cols = shape
    rb = rows // nsteps
    return (pl.BlockSpec((None, rb, cols), lambda *ids: (0, ids[-1], 0)),
            pl.BlockSpec((rb, cols), lambda *ids: (ids[-1], 0)))


def _gla_kernel(*refs, seq_len, nsub, has_s0, has_ada, n_cast):
    rows_refs, gn_ref = refs[:6], refs[6]
    n_ada_in = 7 + has_s0
    n_cast_in = n_ada_in + 3 * has_ada
    n_in = n_cast_in + n_cast
    s0_ref = refs[7] if has_s0 else None
    o_ref, sfin_ref = refs[n_in:n_in + 2]
    n_out = 2 + has_ada + n_cast
    scratch = refs[n_in + n_out:]
    if has_ada:
        _ada_kernel(*refs[n_ada_in:n_cast_in], refs[n_in + 2])
    for src, dst in zip(refs[n_cast_in:n_in], refs[n_in + 2 + has_ada:n_in + n_out]):
        dst[...] = src[...].astype(BF16)
    for j in range(nsub):
        rows = pl.ds(j * seq_len, seq_len)
        _gla_sequence(*[r.at[rows, :] for r in rows_refs], gn_ref, s0_ref.at[j] if has_s0 else None,
                      o_ref.at[rows, :], sfin_ref.at[j], *[s.at[j] for s in scratch], seq_len=seq_len)


def _gla_sequence(q_ref, k_ref, v_ref, gf_ref, gb_ref, r_ref, gn_ref, s0_ref, o_ref, sfin_ref,
                  oi_scr, qd_scr, kv_scr, dec_scr, ss_scr, *, seq_len):
    has_s0 = s0_ref is not None
    nblk = seq_len // GLA_BLK
    nchunk = seq_len // GLA_CHUNK
    cl = GLA_CHUNK
    ti = lax.broadcasted_iota(jnp.int32, (GLA_BLK, GLA_BLK), 0)
    si = lax.broadcasted_iota(jnp.int32, (GLA_BLK, GLA_BLK), 1)
    same = lax.shift_right_logical(ti, 6) == lax.shift_right_logical(si, 6)
    keep = (same & (ti >= si), same & (ti <= si))
    tri = tuple(kp.astype(BF16) for kp in keep)
    lane_head = lax.shift_right_logical(lax.broadcasted_iota(jnp.int32, (cl, GLA_QK), 1), 6)
    zeros_v = jnp.zeros((cl, GLA_DV), BF16)
    heads = [(slice(h * GLA_DK, (h + 1) * GLA_DK), slice(h * GLA_DV, (h + 1) * GLA_DV)) for h in range(GLA_HEADS)]

    for j in range(nblk):
        rows = slice(j * GLA_BLK, (j + 1) * GLA_BLK)
        q = q_ref[rows, :] * (GLA_DK ** -0.5)
        k = k_ref[rows, :]
        v = v_ref[rows, :].astype(BF16)
        qd, kd, k2t = [], [], []
        for d, g_ref in enumerate((gf_ref, gb_ref)):
            b = sum(_dot(tri[d], part) for part in _split_bf16(g_ref[rows, :]))
            last = cl - 1 if d == 0 else 0
            b_last = [b[c * cl + last:c * cl + last + 1] for c in range(GLA_CPB)]
            bl = jnp.concatenate([jnp.broadcast_to(x, (cl, GLA_QK)) for x in b_last], axis=0)
            qd_d = (q * jnp.exp(b)).astype(BF16)
            qd_scr[d, rows, :] = qd_d
            qd.append(qd_d)
            kd.append((k * jnp.exp(-b)).astype(BF16))
            k2t.append(jnp.transpose(k * jnp.exp(bl - b)).astype(BF16))
            for c in range(GLA_CPB):
                dec_scr[d, j * GLA_CPB + c] = jnp.exp(jnp.transpose(jnp.broadcast_to(b_last[c], (GLA_DV, GLA_QK))))
        for h, (ks, vs) in enumerate(heads):
            att = [jnp.where(keep[d], lax.dot_general(qd[d][:, ks], kd[d][:, ks], NT_DIMS,
                                                      preferred_element_type=F32), 0.0) for d in range(2)]
            oi_scr[rows, vs] = _dot((att[0] + att[1]).astype(BF16), v[:, vs])
            vh = v[:, vs]
            vexp = jnp.concatenate(
                [jnp.concatenate([vh[c * cl:(c + 1) * cl] if c2 == c else zeros_v for c2 in range(GLA_CPB)], axis=1)
                 for c in range(GLA_CPB)], axis=0)
            for d in range(2):
                kv_scr[d, j, h] = _dot(k2t[d][ks, :], vexp)

    for d in range(2):
        s = s0_ref[d] if has_s0 else jnp.zeros((GLA_QK, GLA_DV), F32)
        for cg in (range(nchunk) if d == 0 else range(nchunk - 1, -1, -1)):
            j, c = divmod(cg, GLA_CPB)
            ss_scr[d, cg] = s.astype(BF16)
            kv = jnp.concatenate([kv_scr[d, j, h, :, c * GLA_DV:(c + 1) * GLA_DV] for h in range(GLA_HEADS)], axis=0)
            s = s * dec_scr[d, cg] + kv
        sfin_ref[d] = s

    for cg in range(nchunk):
        rows = slice(cg * cl, (cg + 1) * cl)
        inter = []
        for d in range(2):
            qc = qd_scr[d, rows, :]
            qstack = jnp.concatenate([jnp.where(lane_head == h, qc, jnp.zeros_like(qc)) for h in range(GLA_HEADS)],
                                     axis=0)
            inter.append(_dot(qstack, ss_scr[d, cg]))
        gate = jax.nn.silu(r_ref[rows, :])
        for h, (ks, vs) in enumerate(heads):
            hr = slice(h * cl, (h + 1) * cl)
            oh = oi_scr[rows, vs] + inter[0][hr] + inter[1][hr]
            oh = oh * lax.rsqrt(jnp.mean(oh * oh, axis=-1, keepdims=True) + EPS) * gn_ref[...]
            o_ref[rows, vs] = oh * gate[:, vs]


def _gla_call(z, g, gla_norm, s0, seq_len, nseq, row0, nsub, ada=None, casts=()):
    blk = nsub * seq_len
    assert row0 % blk == 0 and nseq % nsub == 0
    r0 = row0 // blk
    has_s0 = s0 is not None
    qk_off = 0
    v_off = 2 * GLA_QK // GLA_VW
    in_specs = [
        pl.BlockSpec((blk, GLA_QK), lambda i: (r0 + i, qk_off)),
        pl.BlockSpec((blk, GLA_QK), lambda i: (r0 + i, qk_off + 1)),
        pl.BlockSpec((blk, GLA_VW), lambda i: (r0 + i, v_off)),
        pl.BlockSpec((blk, GLA_QK), lambda i: (r0 + i, 0)),
        pl.BlockSpec((blk, GLA_QK), lambda i: (r0 + i, 1)),
        pl.BlockSpec((blk, GLA_VW), lambda i: (r0 + i, v_off + 1)),
        pl.BlockSpec((1, GLA_DV), lambda i: (0, 0)),
    ]
    args = [z, z, z, g, g, z, gla_norm]
    state_spec = pl.BlockSpec((nsub, 2, GLA_QK, GLA_DV), lambda i: (i, 0, 0, 0))
    if has_s0:
        in_specs.append(state_spec)
        args.append(s0)
    out_shape = [jax.ShapeDtypeStruct((nseq * seq_len, GLA_VW), F32),
                 jax.ShapeDtypeStruct((nseq, 2, GLA_QK, GLA_DV), F32)]
    out_specs = [pl.BlockSpec((blk, GLA_VW), lambda i: (i, 0)), state_spec]
    nsteps = nseq // nsub
    if ada is not None:
        cond8, w_ada, b_ada, ada_layer = ada
        ada_in, ada_out = _ada_specs(ada_layer, 6 * D_MODEL // nsteps)
        in_specs += ada_in
        args += [cond8, w_ada, b_ada]
        out_shape.append(ADA_SHAPE)
        out_specs.append(ada_out)
    for w in casts:
        cast_in, cast_out = _cast_specs(w.shape, nsteps)
        in_specs.append(cast_in)
        args.append(w)
        out_shape.append(jax.ShapeDtypeStruct(w.shape[1:], BF16))
        out_specs.append(cast_out)
    return pl.pallas_call(
        functools.partial(_gla_kernel, seq_len=seq_len, nsub=nsub, has_s0=has_s0, has_ada=ada is not None,
                          n_cast=len(casts)),
        out_shape=tuple(out_shape),
        grid=(nsteps,),
        in_specs=in_specs,
        out_specs=tuple(out_specs),
        scratch_shapes=[
            pltpu.VMEM((nsub, seq_len, GLA_VW), F32),
            pltpu.VMEM((nsub, 2, seq_len, GLA_QK), BF16),
            pltpu.VMEM((nsub, 2, seq_len // GLA_BLK, GLA_HEADS, GLA_DK, GLA_CPB * GLA_DV), F32),
            pltpu.VMEM((nsub, 2, seq_len // GLA_CHUNK, GLA_QK, GLA_DV), F32),
            pltpu.VMEM((nsub, 2, seq_len // GLA_CHUNK, GLA_QK, GLA_DV), BF16),
        ],
        compiler_params=pltpu.CompilerParams(vmem_limit_bytes=VMEM_LIMIT),
        name=f"gla_len{seq_len}",
    )(*args)


MLP_CHUNK = 512
MLP_LOAD = 256
MLP_SLOTS = 4


class _MlpWeights:
    def __init__(self, w1_hbm, w2_hbm, w1_scr, w2_scr, stage1, stage2, sem, layer):
        self.refs = (w1_hbm, w2_hbm, w1_scr, w2_scr, stage1, stage2, sem)
        self.layer = layer

    def _copies(self, p):
        w1_hbm, w2_hbm, _, _, stage1, stage2, sem = self.refs
        cols = pl.ds(p * MLP_LOAD, MLP_LOAD)
        slot = p % MLP_SLOTS
        return (pltpu.make_async_copy(w1_hbm.at[self.layer, :, cols], stage1.at[slot], sem.at[0, slot]),
                pltpu.make_async_copy(w2_hbm.at[self.layer, cols, :], stage2.at[slot], sem.at[1, slot]))

    def start(self, p):
        for cp in self._copies(p):
            cp.start()

    def prefetch(self):
        for p in range(MLP_SLOTS - 1):
            self.start(p)

    def finish(self, p):
        _, _, w1_scr, w2_scr, stage1, stage2, _ = self.refs
        ahead = p + MLP_SLOTS - 1
        if ahead < D_FF // MLP_LOAD:
            self.start(ahead)
        for cp in self._copies(p):
            cp.wait()
        cols = slice(p * MLP_LOAD, (p + 1) * MLP_LOAD)
        w1_scr[:, cols] = stage1[p % MLP_SLOTS].astype(BF16)
        w2_scr[cols, :] = stage2[p % MLP_SLOTS].astype(BF16)


def _mlp_tail(x, mix, m_ref, gn2_ref, w1_ref, w2_ref, loading=None):
    y1 = x + m_ref[:, 2 * D_MODEL:3 * D_MODEL] * mix
    h = _norm_mod(y1, gn2_ref[...], m_ref[:, 3 * D_MODEL:4 * D_MODEL], m_ref[:, 4 * D_MODEL:5 * D_MODEL]).astype(BF16)
    nchunk = D_FF // MLP_CHUNK
    acc = jnp.zeros(y1.shape, F32)
    for c in range(nchunk):
        cols = slice(c * MLP_CHUNK, (c + 1) * MLP_CHUNK)
        if loading is not None:
            per = MLP_CHUNK // MLP_LOAD
            for p in range(c * per, (c + 1) * per):
                loading.finish(p)
        a = _dot(h, w1_ref[:, cols])
        a = jnp.square(jnp.maximum(a, 0.0)).astype(BF16)
        acc = acc + _dot(a, w2_ref[cols, :])
    return y1 + m_ref[:, 5 * D_MODEL:6 * D_MODEL] * acc


def _run_tail(x, mix, m_ref, gn2_ref, weights, w1_ref, w2_ref, emit):
    first = pl.program_id(0) == 0

    @pl.when(first)
    def _():
        emit(_mlp_tail(x, mix, m_ref, gn2_ref, w1_ref, w2_ref, loading=weights))

    @pl.when(jnp.logical_not(first))
    def _():
        emit(_mlp_tail(x, mix, m_ref, gn2_ref, w1_ref, w2_ref))


def _even_out_kernel(xp_ref, xs_ref, y5_ref, u_ref, dskip_ref, wglu_ref, bglu_ref, glap_ref, glas_ref, wout_ref,
                     m_ref, gn2_ref, w1_hbm, w2_hbm, o_ref, w1_ref, w2_ref, stage1, stage2, sem, *, layer):
    weights = _MlpWeights(w1_hbm, w2_hbm, w1_ref, w2_ref, stage1, stage2, sem, layer)

    @pl.when(pl.program_id(0) == 0)
    def _():
        weights.prefetch()

    nblk = S5_UBLK
    ys = (jnp.concatenate([y5_ref[b] for b in range(nblk)], axis=1)
          + jnp.concatenate([u_ref[b] for b in range(nblk)], axis=1) * dskip_ref[...])
    gl = jax.nn.gelu(ys)
    s5o = gl * jax.nn.sigmoid(_dot(gl.astype(BF16), wglu_ref[...]) + bglu_ref[...])
    gla = _token_tile(glap_ref, glas_ref, _OUT_TM).astype(BF16)
    mix = _dot(s5o.astype(BF16), wout_ref[0:S5_WIDTH, :]) + _dot(gla, wout_ref[S5_WIDTH:, :])

    def emit(y):
        o_ref[...] = y

    _run_tail(_token_tile(xp_ref, xs_ref, _OUT_TM), mix, m_ref, gn2_ref, weights, w1_ref, w2_ref, emit)


def _odd_out_kernel(x_ref, attp_ref, atts_ref, wo_ref, m_ref, gn2_ref, w1_hbm, w2_hbm, op_ref, os_ref,
                    w1_ref, w2_ref, stage1, stage2, sem, *, layer):
    weights = _MlpWeights(w1_hbm, w2_hbm, w1_ref, w2_ref, stage1, stage2, sem, layer)

    @pl.when(pl.program_id(0) == 0)
    def _():
        weights.prefetch()

    mix = _dot(_token_tile(attp_ref, atts_ref, _OUT_TM), wo_ref[...])
    is_prompt = pl.program_id(0) < T_PROMPT // _OUT_TM

    def emit(y):
        @pl.when(is_prompt)
        def _():
            op_ref[...] = y

        @pl.when(jnp.logical_not(is_prompt))
        def _():
            os_ref[...] = y

    _run_tail(x_ref[...], mix, m_ref, gn2_ref, weights, w1_ref, w2_ref, emit)


_OUT_TM = 512


def _const_spec(shape):
    return pl.BlockSpec(shape, lambda i: (0,) * len(shape), pipeline_mode=pl.Buffered(1))


def _tail_specs(layer):
    tm = _OUT_TM
    return [
        pl.BlockSpec((None, 1, 6 * D_MODEL), lambda i: (_cond_row(i, tm), 0, 0)),
        _const_spec((1, D_MODEL)),
        pl.BlockSpec(memory_space=pl.ANY),
        pl.BlockSpec(memory_space=pl.ANY),
    ]


def _tail_scratch():
    return [pltpu.VMEM((D_MODEL, D_FF), BF16), pltpu.VMEM((D_FF, D_MODEL), BF16),
            pltpu.VMEM((MLP_SLOTS, D_MODEL, MLP_LOAD), F32), pltpu.VMEM((MLP_SLOTS, MLP_LOAD, D_MODEL), F32),
            pltpu.SemaphoreType.DMA((2, MLP_SLOTS))]


_TAIL_PARAMS = dict(dimension_semantics=("arbitrary",), vmem_limit_bytes=VMEM_LIMIT_TAIL)


def _even_out_call(xp, xs, y5, u, d_skip, w_glu, b_glu, gla_p, gla_s, w_out, mods, layer, gn2, w1, w2):
    tm = _OUT_TM
    return pl.pallas_call(
        functools.partial(_even_out_kernel, layer=layer),
        out_shape=jax.ShapeDtypeStruct((T_TOK, D_MODEL), F32),
        grid=(T_TOK // tm,),
        in_specs=_token_specs(tm) + [
            pl.BlockSpec((S5_UBLK, tm, LANES), lambda i: (0, i, 0)),
            pl.BlockSpec((S5_UBLK, tm, LANES), lambda i: (0, i, 0)),
            _const_spec((1, S5_WIDTH)),
            _const_spec((S5_WIDTH, S5_WIDTH)),
            _const_spec((1, S5_WIDTH)),
        ] + _token_specs(tm, GLA_VW) + [
            _const_spec((S5_WIDTH + GLA_VW, D_MODEL)),
        ] + _tail_specs(layer),
        out_specs=pl.BlockSpec((tm, D_MODEL), lambda i: (i, 0)),
        scratch_shapes=_tail_scratch(),
        compiler_params=pltpu.CompilerParams(**_TAIL_PARAMS),
        name="even_out_mlp",
    )(xp, xs, y5, u, d_skip, w_glu, b_glu, gla_p, gla_s, w_out, mods, gn2, w1, w2)


def _odd_out_call(x, att_p, att_s, w_o, mods, layer, gn2, w1, w2):
    tm = _OUT_TM
    return pl.pallas_call(
        functools.partial(_odd_out_kernel, layer=layer),
        out_shape=(jax.ShapeDtypeStruct((T_PROMPT, D_MODEL), F32),
                   jax.ShapeDtypeStruct((T_SAMPLE, D_MODEL), F32)),
        grid=(T_TOK // tm,),
        in_specs=[pl.BlockSpec((tm, D_MODEL), lambda i: (i, 0))] + _token_specs(tm) + [
            _const_spec((D_MODEL, D_MODEL)),
        ] + _tail_specs(layer),
        out_specs=tuple(_token_specs(tm)),
        scratch_shapes=_tail_scratch(),
        compiler_params=pltpu.CompilerParams(**_TAIL_PARAMS),
        name="odd_out_mlp",
    )(x, att_p, att_s, w_o, mods, gn2, w1, w2)


def _qkv_kernel(x_ref, gn_ref, m_ref, w_ref, qn_ref, kn_ref, cos_ref, sin_ref,
                q_ref, kb_ref, vb_ref, k32_ref, v32_ref, *, tile):
    h = _norm_mod(x_ref[...], gn_ref[...], m_ref[:, 0:D_MODEL], m_ref[:, D_MODEL:2 * D_MODEL]).astype(BF16)
    z = _dot(h, w_ref[...])
    v = z[:, (N_HEADS + KV_HEADS) * HEAD_DIM:]
    vb_ref[...] = v.astype(BF16)
    even_lane = (lax.broadcasted_iota(jnp.int32, (1, HEAD_DIM), 1) & 1) == 0

    def heads(rope):
        for hd in range(N_HEADS + KV_HEADS):
            xh = z[:, hd * HEAD_DIM:(hd + 1) * HEAD_DIM]
            gain = qn_ref[...] if hd < N_HEADS else kn_ref[...]
            xh = xh * lax.rsqrt(jnp.mean(xh * xh, axis=-1, keepdims=True) + EPS) * gain
            if rope:
                partner = jnp.where(even_lane, pltpu.roll(xh, HEAD_DIM - 1, 1), pltpu.roll(xh, 1, 1))
                xh = xh * cos_ref[...] + partner * sin_ref[...]
            if hd < N_HEADS:
                q_ref[:, hd * HEAD_DIM:(hd + 1) * HEAD_DIM] = xh.astype(BF16)
            else:
                cols = slice((hd - N_HEADS) * HEAD_DIM, (hd - N_HEADS + 1) * HEAD_DIM)
                kb_ref[:, cols] = xh.astype(BF16)
                if not rope:
                    k32_ref[:, hd - N_HEADS, :] = xh

    is_sample = pl.program_id(0) >= T_PROMPT // tile

    @pl.when(is_sample)
    def _():
        heads(True)

    @pl.when(jnp.logical_not(is_sample))
    def _():
        heads(False)
        for kh in range(KV_HEADS):
            v32_ref[:, kh, :] = v[:, kh * HEAD_DIM:(kh + 1) * HEAD_DIM]


def _qkv_call(x, gn, mods, layer, w_qkv, q_norm, k_norm, cos_t, sin_t):
    tm = 512
    pos_tiles = DEC_SEQ // tm
    n_prompt = T_PROMPT // tm
    kvw = KV_HEADS * HEAD_DIM

    def pos_map(i):
        return (jnp.maximum(i - n_prompt, 0) % pos_tiles, 0)

    def prompt_map(i):
        return (jnp.minimum(i, n_prompt - 1), 0, 0)

    return pl.pallas_call(
        functools.partial(_qkv_kernel, tile=tm),
        out_shape=(jax.ShapeDtypeStruct((T_TOK, N_HEADS * HEAD_DIM), BF16),
                   jax.ShapeDtypeStruct((T_TOK, kvw), BF16),
                   jax.ShapeDtypeStruct((T_TOK, kvw), BF16),
                   jax.ShapeDtypeStruct((T_PROMPT, KV_HEADS, HEAD_DIM), F32),
                   jax.ShapeDtypeStruct((T_PROMPT, KV_HEADS, HEAD_DIM), F32)),
        grid=(T_TOK // tm,),
        in_specs=[
            pl.BlockSpec((tm, D_MODEL), lambda i: (i, 0)),
            pl.BlockSpec((1, D_MODEL), lambda i: (0, 0)),
            pl.BlockSpec((None, 1, 6 * D_MODEL), lambda i: (_cond_row(i, tm), 0, 0)),
            pl.BlockSpec(w_qkv.shape, lambda i: (0, 0)),
            pl.BlockSpec((1, HEAD_DIM), lambda i: (0, 0)),
            pl.BlockSpec((1, HEAD_DIM), lambda i: (0, 0)),
            pl.BlockSpec((tm, HEAD_DIM), pos_map),
            pl.BlockSpec((tm, HEAD_DIM), pos_map),
        ],
        out_specs=(pl.BlockSpec((tm, N_HEADS * HEAD_DIM), lambda i: (i, 0)),
                   pl.BlockSpec((tm, kvw), lambda i: (i, 0)),
                   pl.BlockSpec((tm, kvw), lambda i: (i, 0)),
                   pl.BlockSpec((tm, KV_HEADS, HEAD_DIM), prompt_map),
                   pl.BlockSpec((tm, KV_HEADS, HEAD_DIM), prompt_map)),
        compiler_params=pltpu.CompilerParams(vmem_limit_bytes=VMEM_LIMIT),
        name="odd_qkv",
    )(x, gn, mods, w_qkv, q_norm, k_norm, cos_t, sin_t)


def _rope_tables():
    f32 = np.float32
    rows = DEC_SEQ // GRID_W
    row = np.repeat(np.arange(rows, dtype=f32), GRID_W)
    col = np.tile(np.arange(GRID_W, dtype=f32), rows)
    inv = np.power(f32(ROPE_THETA), -np.arange(0, AXIS_DIM, 2, dtype=f32) / f32(AXIS_DIM)).astype(f32)
    ang = np.concatenate([row[:, None] * inv, col[:, None] * inv], axis=-1).astype(f32)
    cos_t = np.repeat(np.cos(ang), 2, axis=-1).astype(f32)
    sin = np.sin(ang).astype(f32)
    sin_t = np.stack([-sin, sin], axis=-1).reshape(DEC_SEQ, HEAD_DIM)
    return jnp.asarray(cos_t), jnp.asarray(sin_t)


def _attn_kernel(*refs, seq_len, has_cache):
    q_ref, k_ref, v_ref = refs[:3]
    ck_ref, cv_ref = refs[3:5] if has_cache else (None, None)
    o_ref = refs[-1]
    c = HEAD_DIM ** -0.5 * math.log2(math.e)
    ones_col = (lax.broadcasted_iota(jnp.int32, (1, HEAD_DIM), 1) == 0).astype(BF16)

    def with_ones(v):
        return jnp.concatenate([v, jnp.broadcast_to(ones_col, v.shape)], axis=1)

    if has_cache:
        ck = ck_ref[...].astype(BF16)
        cv = with_ones(cv_ref[...].astype(BF16))
    for j in range(q_ref.shape[0] // seq_len):
        rows = slice(j * seq_len, (j + 1) * seq_len)
        k = k_ref[rows, :]
        v = with_ones(v_ref[rows, :])
        for r in range(Q_PER_KV):
            cs = slice(r * HEAD_DIM, (r + 1) * HEAD_DIM)
            q = q_ref[rows, cs]
            s = lax.dot_general(q, k, NT_DIMS, preferred_element_type=F32)
            m = jnp.max(s, axis=-1, keepdims=True)
            if has_cache:
                sc = lax.dot_general(q, ck, NT_DIMS, preferred_element_type=F32)
                m = jnp.maximum(m, jnp.max(sc, axis=-1, keepdims=True))
            mc = m * c
            o = _dot(jnp.exp2(s * c - mc).astype(BF16), v)
            if has_cache:
                o = o + _dot(jnp.exp2(sc * c - mc).astype(BF16), cv)
            o_ref[rows, cs] = (o[:, 0:HEAD_DIM] / o[:, HEAD_DIM:HEAD_DIM + 1]).astype(BF16)


def _attn_call(q, k, v, cache_k, cache_v, seq_len, row0, nrows, nsub):
    blk = nsub * seq_len
    assert row0 % blk == 0 and nrows % blk == 0
    has_cache = cache_k is not None
    assert not has_cache or nsub == 1
    b0 = row0 // blk
    gw = Q_PER_KV * HEAD_DIM
    in_specs = [
        pl.BlockSpec((blk, gw), lambda b, g: (b0 + b, g)),
        pl.BlockSpec((blk, HEAD_DIM), lambda b, g: (b0 + b, g)),
        pl.BlockSpec((blk, HEAD_DIM), lambda b, g: (b0 + b, g)),
    ]
    args = [q, k, v]
    if has_cache:
        in_specs += [pl.BlockSpec((PAST_LEN, HEAD_DIM), lambda b, g: (b, g)),
                     pl.BlockSpec((PAST_LEN, HEAD_DIM), lambda b, g: (b, g))]
        args += [cache_k, cache_v]
    return pl.pallas_call(
        functools.partial(_attn_kernel, seq_len=seq_len, has_cache=has_cache),
        out_shape=jax.ShapeDtypeStruct((nrows, N_HEADS * HEAD_DIM), BF16),
        grid=(nrows // blk, KV_HEADS),
        in_specs=in_specs,
        out_specs=pl.BlockSpec((blk, gw), lambda b, g: (b, g)),
        compiler_params=pltpu.CompilerParams(vmem_limit_bytes=VMEM_LIMIT),
        name=f"attn_len{seq_len}",
    )(*args)


def kernel(x_prompt, x_sample, state_s5_re, state_s5_im, state_gla, cache_k, cache_v, c, c_ctx, norm_mix, norm_mlp, w_ada, b_ada, w_mlp_in, w_mlp_out, w_in_e, w_out_e, s5_lambda_re, s5_lambda_im, s5_log_dt, s5_b_re, s5_b_im, s5_c_re, s5_c_im, s5_d, s5_w_glu, s5_b_glu, gla_w_gate2, gla_b_gate, gla_norm, w_qkv_o, w_o_o, q_norm, k_norm):
    xp = x_prompt.reshape(T_PROMPT, D_MODEL)
    xs = x_sample.reshape(T_SAMPLE, D_MODEL)
    cond8 = jnp.concatenate([c_ctx[None, :], c, jnp.zeros((COND_ROWS - 1 - DEC_BATCH, D_MODEL), F32)], axis=0)
    b_ada3 = b_ada.reshape(DEPTH, 1, 6 * D_MODEL)
    w1_all, w2_all = w_mlp_in, w_mlp_out
    mats, mods0 = _s5_prep_call(s5_lambda_re[0], s5_lambda_im[0], s5_log_dt[0], s5_b_re[0], s5_b_im[0],
                                s5_c_re[0], s5_c_im[0], ada=(cond8, w_ada, b_ada3, 0))

    w_in = jnp.pad(w_in_e[0], ((0, 0), (0, LANES - 2 * GLA_RANK))).astype(BF16)
    zg = jnp.zeros((GLA_RANK, GLA_QK), F32)
    w_gate = jnp.concatenate([jnp.concatenate([gla_w_gate2[0, 0], zg], axis=1),
                              jnp.concatenate([zg, gla_w_gate2[0, 1]], axis=1),
                              jnp.zeros((LANES - 2 * GLA_RANK, 2 * GLA_QK), F32)], axis=0).astype(BF16)
    b_gate = gla_b_gate[0].reshape(1, 2 * GLA_QK)
    u, z, g = _inproj_call(xp, xs, norm_mix[0:1], mods0, 0, w_in, w_gate, b_gate)

    def state_rows(s):
        return jnp.transpose(s, (2, 0, 1, 3)).reshape(S5_GROUPS, DEC_BATCH, 2 * S5_STATE)

    h0 = jnp.concatenate([state_rows(state_s5_re[:, 0]), state_rows(state_s5_im[:, 0])], axis=-1)
    nsteps = S5_GROUPS // S5_GPB
    h0 = jnp.transpose(h0.reshape(nsteps, S5_GPB, DEC_BATCH, S5_W), (0, 2, 1, 3))
    y5, ns = _s5_call(u, mats, h0)
    ns = jnp.transpose(ns, (0, 2, 1, 3)).reshape(S5_GROUPS, BATCH, S5_W)

    def state_out(n):
        return jnp.transpose(n.reshape(S5_GROUPS, BATCH, 2, S5_STATE), (1, 2, 0, 3))[:, None]

    new_s5_re = state_out(ns[:, :, :2 * S5_STATE])
    new_s5_im = state_out(ns[:, :, 2 * S5_STATE:])

    gn_gla = gla_norm[0].reshape(1, GLA_DV)
    gla_p, sfin, w_glu, w_out, w_qkv, w_o = _gla_call(z, g, gn_gla, None, SEQ, BATCH, 0, nsub=4,
                                                      casts=(s5_w_glu, w_out_e, w_qkv_o, w_o_o))
    s0 = state_gla[:, 0].reshape(DEC_BATCH, 2, GLA_QK, GLA_DV)
    gla_s, _, mods1 = _gla_call(z, g, gn_gla, s0, DEC_SEQ, DEC_BATCH, T_PROMPT, nsub=1,
                                ada=(cond8, w_ada, b_ada3, 1))
    new_gla = sfin.reshape(BATCH, 1, 2, GLA_HEADS, GLA_DK, GLA_DV)

    x = _even_out_call(xp, xs, y5, u, s5_d[0].reshape(1, S5_WIDTH), w_glu,
                       s5_b_glu[0].reshape(1, S5_WIDTH), gla_p, gla_s, w_out, mods0, 0,
                       norm_mlp[0:1], w1_all, w2_all)

    cos_t, sin_t = _rope_tables()
    q, k, v, k32, v32 = _qkv_call(x, norm_mix[1:2], mods1, 1, w_qkv,
                                  q_norm[0].reshape(1, HEAD_DIM), k_norm[0].reshape(1, HEAD_DIM), cos_t, sin_t)
    att_p = _attn_call(q, k, v, None, None, SEQ, 0, T_PROMPT, nsub=1)
    ck = cache_k[:, 0].reshape(DEC_BATCH * PAST_LEN, KV_HEADS * HEAD_DIM)
    cv = cache_v[:, 0].reshape(DEC_BATCH * PAST_LEN, KV_HEADS * HEAD_DIM)
    att_s = _attn_call(q, k, v, ck, cv, DEC_SEQ, T_PROMPT, T_SAMPLE, nsub=1)
    yp, ys = _odd_out_call(x, att_p, att_s, w_o, mods1, 1, norm_mlp[1:2],
                           w1_all, w2_all)

    new_k = k32.reshape(BATCH, 1, SEQ, KV_HEADS, HEAD_DIM)
    new_v = v32.reshape(BATCH, 1, SEQ, KV_HEADS, HEAD_DIM)
    y_prompt = yp.reshape(BATCH, SEQ, D_MODEL)
    y_sample = ys.reshape(DEC_BATCH, DEC_SEQ, D_MODEL)
    return (y_prompt, y_sample, new_s5_re, new_s5_im, new_gla, new_k, new_v)
```

```python
import functools
import math

import jax
import jax.numpy as jnp
import numpy as np
from jax import lax
from jax.experimental import pallas as pl
from jax.experimental.pallas import tpu as pltpu

F32 = jnp.float32
BF16 = jnp.bfloat16

LANES = 128

D_MODEL = 1024
BATCH = 16
SEQ = 256
DEPTH = 2
DEC_BATCH = 4
DEC_SEQ = 1024
PAST_LEN = 512
GRID_W = 64
S5_WIDTH = 512
S5_GROUP_CH = 16
S5_GROUPS = 32
S5_STATE = 64
GLA_HEADS = 4
GLA_VW = 512
GLA_DV = 128
GLA_DK = 64
GLA_QK = 256
GLA_RANK = 16
GLA_TAU = 16.0
GLA_CHUNK = 64
GLA_CPB = 4
GLA_BLK = GLA_CPB * GLA_CHUNK
HEAD_DIM = 128
N_HEADS = 8
KV_HEADS = 2
Q_PER_KV = N_HEADS // KV_HEADS
AXIS_DIM = 64
ROPE_THETA = 10000.0
D_FF = 4096
EPS = 1e-6

T_PROMPT = BATCH * SEQ
T_SAMPLE = DEC_BATCH * DEC_SEQ
T_TOK = T_PROMPT + T_SAMPLE
COND_ROWS = 8
COND_SPAN = 1024
PROMPT_SPANS = T_PROMPT // COND_SPAN

S5_Q = 16
S5_W = S5_Q * S5_GROUP_CH
S5_GPB = LANES // S5_GROUP_CH
S5_UBLK = S5_WIDTH // LANES
S5_ROWS = T_TOK // S5_Q
S5_PROMPT_ROWS = T_PROMPT // S5_Q
S5_PROMPT_CHUNKS = SEQ // S5_Q
S5_SAMPLE_CHUNKS = DEC_SEQ // S5_Q

VMEM_LIMIT = 56 * 1024 * 1024
VMEM_LIMIT_TAIL = 60 * 1024 * 1024

NT_DIMS = (((1,), (1,)), ((), ()))


def _cond_row(i, tile):
    return jnp.maximum((i * tile) // COND_SPAN - (PROMPT_SPANS - 1), 0)


def _norm_mod(x, gain, shift, scale):
    y = x * lax.rsqrt(jnp.mean(x * x, axis=-1, keepdims=True) + EPS)
    return (y * gain) * (1.0 + scale) + shift


def _dot(a, b):
    return jnp.dot(a, b, preferred_element_type=F32)


def _ada_kernel(cond_ref, w_ref, b_ref, o_ref):
    s = jax.nn.silu(cond_ref[...]).astype(BF16)
    o_ref[:, 0, :] = _dot(s, w_ref[...].astype(BF16)) + b_ref[...]


ADA_SHAPE = jax.ShapeDtypeStruct((COND_ROWS, 1, 6 * D_MODEL), F32)


def _ada_specs(layer, tn):
    return ([pl.BlockSpec((COND_ROWS, D_MODEL), lambda *ids: (0, 0)),
             pl.BlockSpec((None, D_MODEL, tn), lambda *ids: (layer, 0, ids[-1])),
             pl.BlockSpec((None, 1, tn), lambda *ids: (layer, 0, ids[-1]))],
            pl.BlockSpec((COND_ROWS, 1, tn), lambda *ids: (0, 0, ids[-1])))


def _token_specs(tile, width=D_MODEL):
    n_prompt = T_PROMPT // tile
    return [pl.BlockSpec((tile, width), lambda i: (jnp.minimum(i, n_prompt - 1), 0)),
            pl.BlockSpec((tile, width), lambda i: (jnp.maximum(i - n_prompt, 0), 0))]


def _token_tile(xp_ref, xs_ref, tile):
    return jnp.where(pl.program_id(0) < T_PROMPT // tile, xp_ref[...], xs_ref[...])


def _inproj_kernel(xp_ref, xs_ref, gn_ref, m_ref, w_ref, wglr_ref, wg_ref, bg_ref, u_ref, z_ref, g_ref, *, tile):
    x = _token_tile(xp_ref, xs_ref, tile)
    h = _norm_mod(x, gn_ref[...], m_ref[:, 0:D_MODEL], m_ref[:, D_MODEL:2 * D_MODEL]).astype(BF16)
    z = _dot(h, w_ref[...])
    for blk in range(S5_UBLK):
        u_ref[blk] = z[:, blk * LANES:(blk + 1) * LANES]
    z_ref[...] = z[:, S5_WIDTH:]
    glr = _dot(h, wglr_ref[...]).astype(BF16)
    pre = _dot(glr, wg_ref[...]) + bg_ref[...]
    g_ref[...] = jax.nn.log_sigmoid(pre) * (1.0 / GLA_TAU)


def _inproj_call(xp, xs, gn, mods, layer, w_in, w_gate, b_gate):
    tm = 512
    nz = w_in.shape[1] - LANES
    return pl.pallas_call(
        functools.partial(_inproj_kernel, tile=tm),
        out_shape=(jax.ShapeDtypeStruct((S5_UBLK, T_TOK, LANES), F32),
                   jax.ShapeDtypeStruct((T_TOK, nz - S5_WIDTH), F32),
                   jax.ShapeDtypeStruct((T_TOK, 2 * GLA_QK), F32)),
        grid=(T_TOK // tm,),
        in_specs=_token_specs(tm) + [
            pl.BlockSpec((1, D_MODEL), lambda i: (0, 0)),
            pl.BlockSpec((None, 1, 6 * D_MODEL), lambda i: (_cond_row(i, tm), 0, 0)),
            pl.BlockSpec((D_MODEL, nz), lambda i: (0, 0)),
            pl.BlockSpec((D_MODEL, LANES), lambda i: (0, nz // LANES)),
            pl.BlockSpec((LANES, 2 * GLA_QK), lambda i: (0, 0)),
            pl.BlockSpec((1, 2 * GLA_QK), lambda i: (0, 0)),
        ],
        out_specs=(pl.BlockSpec((S5_UBLK, tm, LANES), lambda i: (0, i, 0)),
                   pl.BlockSpec((tm, nz - S5_WIDTH), lambda i: (i, 0)),
                   pl.BlockSpec((tm, 2 * GLA_QK), lambda i: (i, 0))),
        compiler_params=pltpu.CompilerParams(vmem_limit_bytes=VMEM_LIMIT),
        name="even_inproj",
    )(xp, xs, gn, mods, w_in, w_in, w_gate, b_gate)


S5_PREP_GPB = 8
_PREP_LRE, _PREP_LIM, _PREP_LDT = 0, 1, 2
_PREP_BT_RE, _PREP_BT_IM, _PREP_C_RE, _PREP_C_IM, _PREP_ROWS = 8, 24, 40, 56, 72


def _s5_prep_kernel(p_ref, cc_ref, cond_ref, wada_ref, bada_ref, t_ref, bq_ref, cqt_ref, be_ref, a_ref, mods_ref,
                    t_scr, dd_scr):
    _ada_kernel(cond_ref, wada_ref, bada_ref, mods_ref)
    for gi in range(S5_PREP_GPB):
        _s5_prep_group(p_ref.at[gi], cc_ref.at[gi], t_ref.at[gi], bq_ref.at[gi], cqt_ref.at[gi], be_ref.at[gi],
                       a_ref.at[gi], t_scr, dd_scr)


def _s5_prep_group(p_ref, cc_ref, t_ref, bq_ref, cqt_ref, be_ref, a_ref, t_scr, dd_scr):
    gch = S5_GROUP_CH
    lre = p_ref[_PREP_LRE:_PREP_LRE + 1]
    lim = p_ref[_PREP_LIM:_PREP_LIM + 1]
    dt = jnp.exp(p_ref[_PREP_LDT:_PREP_LDT + 1])
    a = lre * dt
    th = lim * dt

    def lam_pow(k):
        mag = jnp.exp(k * a)
        return mag * jnp.cos(k * th), mag * jnp.sin(k * th)

    lb_re, lb_im = lam_pow(1.0)
    nr = lb_re - 1.0
    den = lre * lre + lim * lim
    cf_re = (nr * lre + lb_im * lim) / den
    cf_im = (lb_im * lre - nr * lim) / den
    bt_re = p_ref[_PREP_BT_RE:_PREP_BT_RE + gch]
    bt_im = p_ref[_PREP_BT_IM:_PREP_BT_IM + gch]
    bb_re = jnp.tile(cf_re * bt_re - cf_im * bt_im, (S5_Q, 1))
    bb_im = jnp.tile(cf_re * bt_im + cf_im * bt_re, (S5_Q, 1))

    shape = (S5_W, LANES)
    pos = lax.shift_right_logical(lax.broadcasted_iota(jnp.int32, shape, 0), 4)
    is_f = lax.broadcasted_iota(jnp.int32, shape, 1) < S5_STATE
    posq = lax.broadcasted_iota(jnp.int32, (S5_Q, LANES), 0).astype(F32)
    is_fq = lax.broadcasted_iota(jnp.int32, (S5_Q, LANES), 1) < S5_STATE

    def per_channel(tbl):
        return jnp.broadcast_to(tbl[:, None, :], (S5_Q, S5_GROUP_CH, LANES)).reshape(shape)

    p_re, p_im = map(per_channel, lam_pow(jnp.where(is_fq, (S5_Q - 1.0) - posq, posq)))
    w_re = p_re * bb_re - p_im * bb_im
    w_im = p_re * bb_im + p_im * bb_re
    bq = jnp.concatenate([w_re, w_im], axis=1)
    bqt = jnp.transpose(bq)
    bq_ref[...] = bqt.astype(BF16)

    edge = pos == jnp.where(is_f, 0, S5_Q - 1)
    be = jnp.concatenate([jnp.where(edge, bb_re, 0.0), jnp.where(edge, bb_im, 0.0)], axis=1)
    be_ref[...] = jnp.transpose(be).astype(BF16)

    q_re, q_im = map(per_channel, lam_pow(jnp.where(is_fq, posq + 1.0, S5_Q - posq)))
    ct_re = jnp.tile(p_ref[_PREP_C_RE:_PREP_C_RE + gch], (S5_Q, 1))
    ct_im = jnp.tile(p_ref[_PREP_C_IM:_PREP_C_IM + gch], (S5_Q, 1))
    g_re = q_re * ct_re - q_im * ct_im
    g_im = q_re * ct_im + q_im * ct_re
    cqt_ref[...] = jnp.concatenate([g_re, -g_im], axis=1).astype(BF16)

    a_re, a_im = lam_pow(float(S5_Q))
    a_ref[...] = jnp.concatenate([a_re, a_im], axis=1)

    kf = jnp.dot(cc_ref[0:gch], bqt, precision=lax.Precision.HIGHEST, preferred_element_type=F32)
    kb = jnp.dot(cc_ref[gch:2 * gch], bqt, precision=lax.Precision.HIGHEST, preferred_element_type=F32)
    lo = S5_W - gch
    dd_scr[:, 0:S5_W] = kf
    dd_scr[:, lo:lo + S5_W] = kb
    dd_scr[:, lo:S5_W] = kf[:, lo:S5_W] + kb[:, 0:gch]
    for t in range(S5_Q):
        c0 = (S5_Q - 1 - t) * gch
        t_scr[t * gch:(t + 1) * gch, :] = dd_scr[:, c0:c0 + S5_W]
    t_ref[...] = t_scr[...].astype(BF16)


def _s5_prep_call(lam_re, lam_im, log_dt, b_re, b_im, c_re, c_im, ada):
    per_dir = jnp.stack([lam_re, lam_im, jnp.broadcast_to(log_dt[:, :, None], lam_re.shape)])
    head = jnp.transpose(per_dir, (2, 0, 1, 3)).reshape(S5_GROUPS, 3, 2 * S5_STATE)
    pad = jnp.zeros((S5_GROUPS, _PREP_BT_RE - _PREP_LDT - 1, LANES), F32)
    shared = jnp.concatenate([jnp.transpose(b_re, (0, 2, 1)), jnp.transpose(b_im, (0, 2, 1)), c_re, c_im],
                             axis=1)
    packed = jnp.concatenate([head, pad, jnp.concatenate([shared, shared], axis=-1)], axis=1)
    zero = jnp.zeros_like(c_re)
    cc = jnp.concatenate([jnp.concatenate([c_re, zero, -c_im, zero], axis=-1),
                          jnp.concatenate([zero, c_re, zero, -c_im], axis=-1)], axis=1)

    gpb = S5_PREP_GPB
    nsteps = S5_GROUPS // gpb
    sq = pl.BlockSpec((gpb, S5_W, S5_W), lambda g: (g, 0, 0))
    sq_shape = jax.ShapeDtypeStruct((S5_GROUPS, S5_W, S5_W), BF16)
    cond8, w_ada, b_ada, ada_layer = ada
    ada_in, ada_out = _ada_specs(ada_layer, 6 * D_MODEL // nsteps)
    *mats, mods = pl.pallas_call(
        _s5_prep_kernel,
        out_shape=(sq_shape, sq_shape, sq_shape, sq_shape,
                   jax.ShapeDtypeStruct((S5_GROUPS, 1, S5_W), F32), ADA_SHAPE),
        grid=(nsteps,),
        in_specs=[pl.BlockSpec((gpb, _PREP_ROWS, LANES), lambda g: (g, 0, 0)),
                  pl.BlockSpec((gpb, 2 * S5_GROUP_CH, S5_W), lambda g: (g, 0, 0))] + ada_in,
        out_specs=(sq, sq, sq, sq, pl.BlockSpec((gpb, 1, S5_W), lambda g: (g, 0, 0)), ada_out),
        scratch_shapes=[pltpu.VMEM((S5_W, S5_W), F32), pltpu.VMEM((S5_GROUP_CH, 2 * S5_W), F32)],
        compiler_params=pltpu.CompilerParams(vmem_limit_bytes=VMEM_LIMIT),
        name="s5_prep",
    )(packed, cc, cond8, w_ada, b_ada)
    return mats, mods


def _s5_kernel(u_ref, tt_ref, bqt_ref, cqt_ref, bet_ref, a_ref, h0_ref, y_ref, ns_ref,
               ut_scr, x_scr, spf_scr, spb_scr, ne_scr, yt_scr, xt_scr):
    gch = S5_GROUP_CH
    for s in range(S5_Q):
        rows = u_ref[pl.ds(s, S5_ROWS, stride=S5_Q), :]
        rows_t = jnp.transpose(rows).astype(BF16)
        for gl in range(S5_GPB):
            ut_scr[gl, s * gch:(s + 1) * gch, :] = rows_t[gl * gch:(gl + 1) * gch, :]

    for gl in range(S5_GPB):
        ut = ut_scr[gl]
        xt_scr[...] = _dot(bqt_ref[gl], ut)
        x = jnp.transpose(xt_scr[...])
        xt_scr[:, 0:S5_PROMPT_ROWS] = _dot(bet_ref[gl], ut[:, 0:S5_PROMPT_ROWS])
        ne = jnp.transpose(xt_scr[:, 0:S5_PROMPT_ROWS])
        for part in range(2):
            x_scr[part, pl.ds(gl, S5_ROWS, stride=S5_GPB), :] = x[:, part * LANES:(part + 1) * LANES]
            ne_scr[part, pl.ds(gl, S5_PROMPT_ROWS, stride=S5_GPB), :] = ne[:, part * LANES:(part + 1) * LANES]

    is_f = lax.broadcasted_iota(jnp.int32, (1, LANES), 1) < S5_STATE
    a_re = a_ref[:, 0:LANES]
    a_im = a_ref[:, LANES:2 * LANES]

    def tile(row):
        return pl.ds(pl.multiple_of(row * S5_GPB, S5_GPB), S5_GPB)

    def scan(base, nseq, nchunk, s_init):
        def body(i, state):
            new = []
            for b in range(nseq):
                s_re, s_im = state[b]
                rows_f = tile(base + b * nchunk + i)
                rows_b = tile(base + b * nchunk + (nchunk - 1 - i))
                spf_scr[0, rows_f, :] = s_re
                spf_scr[1, rows_f, :] = s_im
                spb_scr[0, rows_b, :] = s_re
                spb_scr[1, rows_b, :] = s_im
                x_re = jnp.where(is_f, x_scr[0, rows_f, :], x_scr[0, rows_b, :])
                x_im = jnp.where(is_f, x_scr[1, rows_f, :], x_scr[1, rows_b, :])
                new.append((a_re * s_re - a_im * s_im + x_re, a_re * s_im + a_im * s_re + x_im))
            return tuple(new)

        lax.fori_loop(0, nchunk, body, tuple(s_init))

    zero = jnp.zeros((S5_GPB, LANES), F32)
    scan(0, BATCH, S5_PROMPT_CHUNKS, [(zero, zero)] * BATCH)
    scan(S5_PROMPT_ROWS, DEC_BATCH, S5_SAMPLE_CHUNKS,
         [(h0_ref[b, :, 0:LANES], h0_ref[b, :, LANES:2 * LANES]) for b in range(DEC_BATCH)])

    for b in range(BATCH):
        first = pl.ds(b * S5_PROMPT_CHUNKS * S5_GPB, S5_GPB)
        last = pl.ds(((b + 1) * S5_PROMPT_CHUNKS - 1) * S5_GPB, S5_GPB)
        for part in range(2):
            ns_ref[b, :, part * LANES:(part + 1) * LANES] = jnp.where(is_f, ne_scr[part, first, :], ne_scr[part, last, :])

    for gl in range(S5_GPB):
        rows = pl.ds(gl, S5_ROWS, stride=S5_GPB)
        carried = jnp.concatenate([jnp.where(is_f, spf_scr[p, rows, :], spb_scr[p, rows, :]) for p in range(2)],
                                  axis=1).astype(BF16)
        yt = _dot(tt_ref[gl], ut_scr[gl]) + lax.dot_general(cqt_ref[gl], carried, NT_DIMS,
                                                            preferred_element_type=F32)
        for t in range(S5_Q):
            yt_scr[t, gl * gch:(gl + 1) * gch, :] = yt[t * gch:(t + 1) * gch, :]
    for t in range(S5_Q):
        y_ref[pl.ds(t, S5_ROWS, stride=S5_Q), :] = jnp.transpose(yt_scr[t])


def _s5_call(u, mats, h0):
    tt_m, bqt_m, cqt_m, bet_m, a_m = mats
    nsteps = S5_GROUPS // S5_GPB
    sq = pl.BlockSpec((S5_GPB, S5_W, S5_W), lambda g: (g, 0, 0))
    state_scr = pltpu.VMEM((2, S5_ROWS * S5_GPB, LANES), F32)
    return pl.pallas_call(
        _s5_kernel,
        out_shape=(jax.ShapeDtypeStruct((nsteps, T_TOK, LANES), F32),
                   jax.ShapeDtypeStruct((nsteps, BATCH, S5_GPB, S5_W), F32)),
        grid=(nsteps,),
        in_specs=[
            pl.BlockSpec((None, T_TOK, LANES), lambda g: (g, 0, 0)),
            sq, sq, sq, sq,
            pl.BlockSpec((S5_GPB, S5_W), lambda g: (g, 0)),
            pl.BlockSpec((None, DEC_BATCH, S5_GPB, S5_W), lambda g: (g, 0, 0, 0)),
        ],
        out_specs=(pl.BlockSpec((None, T_TOK, LANES), lambda g: (g, 0, 0)),
                   pl.BlockSpec((None, BATCH, S5_GPB, S5_W), lambda g: (g, 0, 0, 0))),
        scratch_shapes=[pltpu.VMEM((S5_GPB, S5_W, S5_ROWS), BF16), state_scr, state_scr, state_scr,
                        pltpu.VMEM((2, S5_PROMPT_ROWS * S5_GPB, LANES), F32),
                        pltpu.VMEM((S5_Q, LANES, S5_ROWS), F32), pltpu.VMEM((S5_W, S5_ROWS), F32)],
        compiler_params=pltpu.CompilerParams(vmem_limit_bytes=VMEM_LIMIT),
        name="s5_scan",
    )(u, tt_m, bqt_m, cqt_m, bet_m, a_m.reshape(S5_GROUPS, S5_W), h0)


def _split_bf16(x):
    hi = x.astype(BF16)
    r1 = x - hi.astype(F32)
    mid = r1.astype(BF16)
    lo = (r1 - mid.astype(F32)).astype(BF16)
    return hi, mid, lo


def _cast_specs(shape, nsteps):
    _, rows, cols = shape
    rb = rows // nsteps
    return (pl.BlockSpec((None, rb, cols), lambda *ids: (0, ids[-1], 0)),
            pl.BlockSpec((rb, cols), lambda *ids: (ids[-1], 0)))


def _gla_kernel(*refs, seq_len, nsub, has_s0, has_ada, n_cast):
    rows_refs, gn_ref = refs[:6], refs[6]
    n_ada_in = 7 + has_s0
    n_cast_in = n_ada_in + 3 * has_ada
    n_in = n_cast_in + n_cast
    s0_ref = refs[7] if has_s0 else None
    o_ref, sfin_ref = refs[n_in:n_in + 2]
    n_out = 2 + has_ada + n_cast
    scratch = refs[n_in + n_out:]
    if has_ada:
        _ada_kernel(*refs[n_ada_in:n_cast_in], refs[n_in + 2])
    for src, dst in zip(refs[n_cast_in:n_in], refs[n_in + 2 + has_ada:n_in + n_out]):
        dst[...] = src[...].astype(BF16)
    for j in range(nsub):
        rows = pl.ds(j * seq_len, seq_len)
        _gla_sequence(*[r.at[rows, :] for r in rows_refs], gn_ref, s0_ref.at[j] if has_s0 else None,
                      o_ref.at[rows, :], sfin_ref.at[j], *[s.at[j] for s in scratch], seq_len=seq_len)


def _gla_sequence(q_ref, k_ref, v_ref, gf_ref, gb_ref, r_ref, gn_ref, s0_ref, o_ref, sfin_ref,
                  oi_scr, qd_scr, kv_scr, dec_scr, ss_scr, *, seq_len):
    has_s0 = s0_ref is not None
    nblk = seq_len // GLA_BLK
    nchunk = seq_len // GLA_CHUNK
    cl = GLA_CHUNK
    ti = lax.broadcasted_iota(jnp.int32, (GLA_BLK, GLA_BLK), 0)
    si = lax.broadcasted_iota(jnp.int32, (GLA_BLK, GLA_BLK), 1)
    same = lax.shift_right_logical(ti, 6) == lax.shift_right_logical(si, 6)
    keep = (same & (ti >= si), same & (ti <= si))
    tri = tuple(kp.astype(BF16) for kp in keep)
    lane_head = lax.shift_right_logical(lax.broadcasted_iota(jnp.int32, (cl, GLA_QK), 1), 6)
    zeros_v = jnp.zeros((cl, GLA_DV), BF16)
    heads = [(slice(h * GLA_DK, (h + 1) * GLA_DK), slice(h * GLA_DV, (h + 1) * GLA_DV)) for h in range(GLA_HEADS)]

    for j in range(nblk):
        rows = slice(j * GLA_BLK, (j + 1) * GLA_BLK)
        q = q_ref[rows, :] * (GLA_DK ** -0.5)
        k = k_ref[rows, :]
        v = v_ref[rows, :].astype(BF16)
        qd, kd, k2t = [], [], []
        for d, g_ref in enumerate((gf_ref, gb_ref)):
            b = sum(_dot(tri[d], part) for part in _split_bf16(g_ref[rows, :]))
            last = cl - 1 if d == 0 else 0
            b_last = [b[c * cl + last:c * cl + last + 1] for c in range(GLA_CPB)]
            bl = jnp.concatenate([jnp.broadcast_to(x, (cl, GLA_QK)) for x in b_last], axis=0)
            qd_d = (q * jnp.exp(b)).astype(BF16)
            qd_scr[d, rows, :] = qd_d
            qd.append(qd_d)
            kd.append((k * jnp.exp(-b)).astype(BF16))
            k2t.append(jnp.transpose(k * jnp.exp(bl - b)).astype(BF16))
            for c in range(GLA_CPB):
                dec_scr[d, j * GLA_CPB + c] = jnp.exp(jnp.transpose(jnp.broadcast_to(b_last[c], (GLA_DV, GLA_QK))))
        for h, (ks, vs) in enumerate(heads):
            att = [jnp.where(keep[d], lax.dot_general(qd[d][:, ks], kd[d][:, ks], NT_DIMS,
                                                      preferred_element_type=F32), 0.0) for d in range(2)]
            oi_scr[rows, vs] = _dot((att[0] + att[1]).astype(BF16), v[:, vs])
            vh = v[:, vs]
            vexp = jnp.concatenate(
                [jnp.concatenate([vh[c * cl:(c + 1) * cl] if c2 == c else zeros_v for c2 in range(GLA_CPB)], axis=1)
                 for c in range(GLA_CPB)], axis=0)
            for d in range(2):
                kv_scr[d, j, h] = _dot(k2t[d][ks, :], vexp)

    for d in range(2):
        s = s0_ref[d] if has_s0 else jnp.zeros((GLA_QK, GLA_DV), F32)
        for cg in (range(nchunk) if d == 0 else range(nchunk - 1, -1, -1)):
            j, c = divmod(cg, GLA_CPB)
            ss_scr[d, cg] = s.astype(BF16)
            kv = jnp.concatenate([kv_scr[d, j, h, :, c * GLA_DV:(c + 1) * GLA_DV] for h in range(GLA_HEADS)], axis=0)
            s = s * dec_scr[d, cg] + kv
        sfin_ref[d] = s

    for cg in range(nchunk):
        rows = slice(cg * cl, (cg + 1) * cl)
        inter = []
        for d in range(2):
            qc = qd_scr[d, rows, :]
            qstack = jnp.concatenate([jnp.where(lane_head == h, qc, jnp.zeros_like(qc)) for h in range(GLA_HEADS)],
                                     axis=0)
            inter.append(_dot(qstack, ss_scr[d, cg]))
        gate = jax.nn.silu(r_ref[rows, :])
        for h, (ks, vs) in enumerate(heads):
            hr = slice(h * cl, (h + 1) * cl)
            oh = oi_scr[rows, vs] + inter[0][hr] + inter[1][hr]
            oh = oh * lax.rsqrt(jnp.mean(oh * oh, axis=-1, keepdims=True) + EPS) * gn_ref[...]
            o_ref[rows, vs] = oh * gate[:, vs]


def _gla_call(z, g, gla_norm, s0, seq_len, nseq, row0, nsub, ada=None, casts=()):
    blk = nsub * seq_len
    assert row0 % blk == 0 and nseq % nsub == 0
    r0 = row0 // blk
    has_s0 = s0 is not None
    qk_off = 0
    v_off = 2 * GLA_QK // GLA_VW
    in_specs = [
        pl.BlockSpec((blk, GLA_QK), lambda i: (r0 + i, qk_off)),
        pl.BlockSpec((blk, GLA_QK), lambda i: (r0 + i, qk_off + 1)),
        pl.BlockSpec((blk, GLA_VW), lambda i: (r0 + i, v_off)),
        pl.BlockSpec((blk, GLA_QK), lambda i: (r0 + i, 0)),
        pl.BlockSpec((blk, GLA_QK), lambda i: (r0 + i, 1)),
        pl.BlockSpec((blk, GLA_VW), lambda i: (r0 + i, v_off + 1)),
        pl.BlockSpec((1, GLA_DV), lambda i: (0, 0)),
    ]
    args = [z, z, z, g, g, z, gla_norm]
    state_spec = pl.BlockSpec((nsub, 2, GLA_QK, GLA_DV), lambda i: (i, 0, 0, 0))
    if has_s0:
        in_specs.append(state_spec)
        args.append(s0)
    out_shape = [jax.ShapeDtypeStruct((nseq * seq_len, GLA_VW), F32),
                 jax.ShapeDtypeStruct((nseq, 2, GLA_QK, GLA_DV), F32)]
    out_specs = [pl.BlockSpec((blk, GLA_VW), lambda i: (i, 0)), state_spec]
    nsteps = nseq // nsub
    if ada is not None:
        cond8, w_ada, b_ada, ada_layer = ada
        ada_in, ada_out = _ada_specs(ada_layer, 6 * D_MODEL // nsteps)
        in_specs += ada_in
        args += [cond8, w_ada, b_ada]
        out_shape.append(ADA_SHAPE)
        out_specs.append(ada_out)
    for w in casts:
        cast_in, cast_out = _cast_specs(w.shape, nsteps)
        in_specs.append(cast_in)
        args.append(w)
        out_shape.append(jax.ShapeDtypeStruct(w.shape[1:], BF16))
        out_specs.append(cast_out)
    return pl.pallas_call(
        functools.partial(_gla_kernel, seq_len=seq_len, nsub=nsub, has_s0=has_s0, has_ada=ada is not None,
                          n_cast=len(casts)),
        out_shape=tuple(out_shape),
        grid=(nsteps,),
        in_specs=in_specs,
        out_specs=tuple(out_specs),
        scratch_shapes=[
            pltpu.VMEM((nsub, seq_len, GLA_VW), F32),
            pltpu.VMEM((nsub, 2, seq_len, GLA_QK), BF16),
            pltpu.VMEM((nsub, 2, seq_len // GLA_BLK, GLA_HEADS, GLA_DK, GLA_CPB * GLA_DV), F32),
            pltpu.VMEM((nsub, 2, seq_len // GLA_CHUNK, GLA_QK, GLA_DV), F32),
            pltpu.VMEM((nsub, 2, seq_len // GLA_CHUNK, GLA_QK, GLA_DV), BF16),
        ],
        compiler_params=pltpu.CompilerParams(vmem_limit_bytes=VMEM_LIMIT),
        name=f"gla_len{seq_len}",
    )(*args)


MLP_CHUNK = 512
MLP_LOAD = 256
MLP_SLOTS = 4


class _MlpWeights:
    def __init__(self, w1_hbm, w2_hbm, w1_scr, w2_scr, stage1, stage2, sem, layer):
        self.refs = (w1_hbm, w2_hbm, w1_scr, w2_scr, stage1, stage2, sem)
        self.layer = layer

    def _copies(self, p):
        w1_hbm, w2_hbm, _, _, stage1, stage2, sem = self.refs
        cols = pl.ds(p * MLP_LOAD, MLP_LOAD)
        slot = p % MLP_SLOTS
        return (pltpu.make_async_copy(w1_hbm.at[self.layer, :, cols], stage1.at[slot], sem.at[0, slot]),
                pltpu.make_async_copy(w2_hbm.at[self.layer, cols, :], stage2.at[slot], sem.at[1, slot]))

    def start(self, p):
        for cp in self._copies(p):
            cp.start()

    def prefetch(self):
        for p in range(MLP_SLOTS - 1):
            self.start(p)

    def finish(self, p):
        _, _, w1_scr, w2_scr, stage1, stage2, _ = self.refs
        ahead = p + MLP_SLOTS - 1
        if ahead < D_FF // MLP_LOAD:
            self.start(ahead)
        for cp in self._copies(p):
            cp.wait()
        cols = slice(p * MLP_LOAD, (p + 1) * MLP_LOAD)
        w1_scr[:, cols] = stage1[p % MLP_SLOTS].astype(BF16)
        w2_scr[cols, :] = stage2[p % MLP_SLOTS].astype(BF16)


def _mlp_tail(x, mix, m_ref, gn2_ref, w1_ref, w2_ref, loading=None):
    y1 = x + m_ref[:, 2 * D_MODEL:3 * D_MODEL] * mix
    h = _norm_mod(y1, gn2_ref[...], m_ref[:, 3 * D_MODEL:4 * D_MODEL], m_ref[:, 4 * D_MODEL:5 * D_MODEL]).astype(BF16)
    nchunk = D_FF // MLP_CHUNK
    acc = jnp.zeros(y1.shape, F32)
    for c in range(nchunk):
        cols = slice(c * MLP_CHUNK, (c + 1) * MLP_CHUNK)
        if loading is not None:
            per = MLP_CHUNK // MLP_LOAD
            for p in range(c * per, (c + 1) * per):
                loading.finish(p)
        a = _dot(h, w1_ref[:, cols])
        a = jnp.square(jnp.maximum(a, 0.0)).astype(BF16)
        acc = acc + _dot(a, w2_ref[cols, :])
    return y1 + m_ref[:, 5 * D_MODEL:6 * D_MODEL] * acc


def _run_tail(x, mix, m_ref, gn2_ref, weights, w1_ref, w2_ref, emit):
    first = pl.program_id(0) == 0

    @pl.when(first)
    def _():
        emit(_mlp_tail(x, mix, m_ref, gn2_ref, w1_ref, w2_ref, loading=weights))

    @pl.when(jnp.logical_not(first))
    def _():
        emit(_mlp_tail(x, mix, m_ref, gn2_ref, w1_ref, w2_ref))


def _even_out_kernel(xp_ref, xs_ref, y5_ref, u_ref, dskip_ref, wglu_ref, bglu_ref, glap_ref, glas_ref, wout_ref,
                     m_ref, gn2_ref, w1_hbm, w2_hbm, o_ref, w1_ref, w2_ref, stage1, stage2, sem, *, layer):
    weights = _MlpWeights(w1_hbm, w2_hbm, w1_ref, w2_ref, stage1, stage2, sem, layer)

    @pl.when(pl.program_id(0) == 0)
    def _():
        weights.prefetch()

    nblk = S5_UBLK
    ys = (jnp.concatenate([y5_ref[b] for b in range(nblk)], axis=1)
          + jnp.concatenate([u_ref[b] for b in range(nblk)], axis=1) * dskip_ref[...])
    gl = jax.nn.gelu(ys)
    s5o = gl * jax.nn.sigmoid(_dot(gl.astype(BF16), wglu_ref[...]) + bglu_ref[...])
    gla = _token_tile(glap_ref, glas_ref, _OUT_TM).astype(BF16)
    mix = _dot(s5o.astype(BF16), wout_ref[0:S5_WIDTH, :]) + _dot(gla, wout_ref[S5_WIDTH:, :])

    def emit(y):
        o_ref[...] = y

    _run_tail(_token_tile(xp_ref, xs_ref, _OUT_TM), mix, m_ref, gn2_ref, weights, w1_ref, w2_ref, emit)


def _odd_out_kernel(x_ref, attp_ref, atts_ref, wo_ref, m_ref, gn2_ref, w1_hbm, w2_hbm, op_ref, os_ref,
                    w1_ref, w2_ref, stage1, stage2, sem, *, layer):
    weights = _MlpWeights(w1_hbm, w2_hbm, w1_ref, w2_ref, stage1, stage2, sem, layer)

    @pl.when(pl.program_id(0) == 0)
    def _():
        weights.prefetch()

    mix = _dot(_token_tile(attp_ref, atts_ref, _OUT_TM), wo_ref[...])
    is_prompt = pl.program_id(0) < T_PROMPT // _OUT_TM

    def emit(y):
        @pl.when(is_prompt)
        def _():
            op_ref[...] = y

        @pl.when(jnp.logical_not(is_prompt))
        def _():
            os_ref[...] = y

    _run_tail(x_ref[...], mix, m_ref, gn2_ref, weights, w1_ref, w2_ref, emit)


_OUT_TM = 512


def _const_spec(shape):
    return pl.BlockSpec(shape, lambda i: (0,) * len(shape), pipeline_mode=pl.Buffered(1))


def _tail_specs(layer):
    tm = _OUT_TM
    return [
        pl.BlockSpec((None, 1, 6 * D_MODEL), lambda i: (_cond_row(i, tm), 0, 0)),
        _const_spec((1, D_MODEL)),
        pl.BlockSpec(memory_space=pl.ANY),
        pl.BlockSpec(memory_space=pl.ANY),
    ]


def _tail_scratch():
    return [pltpu.VMEM((D_MODEL, D_FF), BF16), pltpu.VMEM((D_FF, D_MODEL), BF16),
            pltpu.VMEM((MLP_SLOTS, D_MODEL, MLP_LOAD), F32), pltpu.VMEM((MLP_SLOTS, MLP_LOAD, D_MODEL), F32),
            pltpu.SemaphoreType.DMA((2, MLP_SLOTS))]


_TAIL_PARAMS = dict(dimension_semantics=("arbitrary",), vmem_limit_bytes=VMEM_LIMIT_TAIL)


def _even_out_call(xp, xs, y5, u, d_skip, w_glu, b_glu, gla_p, gla_s, w_out, mods, layer, gn2, w1, w2):
    tm = _OUT_TM
    return pl.pallas_call(
        functools.partial(_even_out_kernel, layer=layer),
        out_shape=jax.ShapeDtypeStruct((T_TOK, D_MODEL), F32),
        grid=(T_TOK // tm,),
        in_specs=_token_specs(tm) + [
            pl.BlockSpec((S5_UBLK, tm, LANES), lambda i: (0, i, 0)),
            pl.BlockSpec((S5_UBLK, tm, LANES), lambda i: (0, i, 0)),
            _const_spec((1, S5_WIDTH)),
            _const_spec((S5_WIDTH, S5_WIDTH)),
            _const_spec((1, S5_WIDTH)),
        ] + _token_specs(tm, GLA_VW) + [
            _const_spec((S5_WIDTH + GLA_VW, D_MODEL)),
        ] + _tail_specs(layer),
        out_specs=pl.BlockSpec((tm, D_MODEL), lambda i: (i, 0)),
        scratch_shapes=_tail_scratch(),
        compiler_params=pltpu.CompilerParams(**_TAIL_PARAMS),
        name="even_out_mlp",
    )(xp, xs, y5, u, d_skip, w_glu, b_glu, gla_p, gla_s, w_out, mods, gn2, w1, w2)


def _odd_out_call(x, att_p, att_s, w_o, mods, layer, gn2, w1, w2):
    tm = _OUT_TM
    return pl.pallas_call(
        functools.partial(_odd_out_kernel, layer=layer),
        out_shape=(jax.ShapeDtypeStruct((T_PROMPT, D_MODEL), F32),
                   jax.ShapeDtypeStruct((T_SAMPLE, D_MODEL), F32)),
        grid=(T_TOK // tm,),
        in_specs=[pl.BlockSpec((tm, D_MODEL), lambda i: (i, 0))] + _token_specs(tm) + [
            _const_spec((D_MODEL, D_MODEL)),
        ] + _tail_specs(layer),
        out_specs=tuple(_token_specs(tm)),
        scratch_shapes=_tail_scratch(),
        compiler_params=pltpu.CompilerParams(**_TAIL_PARAMS),
        name="odd_out_mlp",
    )(x, att_p, att_s, w_o, mods, gn2, w1, w2)


def _qkv_kernel(x_ref, gn_ref, m_ref, w_ref, qn_ref, kn_ref, cos_ref, sin_ref,
                q_ref, kb_ref, vb_ref, k32_ref, v32_ref, *, tile):
    h = _norm_mod(x_ref[...], gn_ref[...], m_ref[:, 0:D_MODEL], m_ref[:, D_MODEL:2 * D_MODEL]).astype(BF16)
    z = _dot(h, w_ref[...])
    v = z[:, (N_HEADS + KV_HEADS) * HEAD_DIM:]
    vb_ref[...] = v.astype(BF16)
    even_lane = (lax.broadcasted_iota(jnp.int32, (1, HEAD_DIM), 1) & 1) == 0

    def heads(rope):
        for hd in range(N_HEADS + KV_HEADS):
            xh = z[:, hd * HEAD_DIM:(hd + 1) * HEAD_DIM]
            gain = qn_ref[...] if hd < N_HEADS else kn_ref[...]
            xh = xh * lax.rsqrt(jnp.mean(xh * xh, axis=-1, keepdims=True) + EPS) * gain
            if rope:
                partner = jnp.where(even_lane, pltpu.roll(xh, HEAD_DIM - 1, 1), pltpu.roll(xh, 1, 1))
                xh = xh * cos_ref[...] + partner * sin_ref[...]
            if hd < N_HEADS:
                q_ref[:, hd * HEAD_DIM:(hd + 1) * HEAD_DIM] = xh.astype(BF16)
            else:
                cols = slice((hd - N_HEADS) * HEAD_DIM, (hd - N_HEADS + 1) * HEAD_DIM)
                kb_ref[:, cols] = xh.astype(BF16)
                if not rope:
                    k32_ref[:, hd - N_HEADS, :] = xh

    is_sample = pl.program_id(0) >= T_PROMPT // tile

    @pl.when(is_sample)
    def _():
        heads(True)

    @pl.when(jnp.logical_not(is_sample))
    def _():
        heads(False)
        for kh in range(KV_HEADS):
            v32_ref[:, kh, :] = v[:, kh * HEAD_DIM:(kh + 1) * HEAD_DIM]


def _qkv_call(x, gn, mods, layer, w_qkv, q_norm, k_norm, cos_t, sin_t):
    tm = 512
    pos_tiles = DEC_SEQ // tm
    n_prompt = T_PROMPT // tm
    kvw = KV_HEADS * HEAD_DIM

    def pos_map(i):
        return (jnp.maximum(i - n_prompt, 0) % pos_tiles, 0)

    def prompt_map(i):
        return (jnp.minimum(i, n_prompt - 1), 0, 0)

    return pl.pallas_call(
        functools.partial(_qkv_kernel, tile=tm),
        out_shape=(jax.ShapeDtypeStruct((T_TOK, N_HEADS * HEAD_DIM), BF16),
                   jax.ShapeDtypeStruct((T_TOK, kvw), BF16),
                   jax.ShapeDtypeStruct((T_TOK, kvw), BF16),
                   jax.ShapeDtypeStruct((T_PROMPT, KV_HEADS, HEAD_DIM), F32),
                   jax.ShapeDtypeStruct((T_PROMPT, KV_HEADS, HEAD_DIM), F32)),
        grid=(T_TOK // tm,),
        in_specs=[
            pl.BlockSpec((tm, D_MODEL), lambda i: (i, 0)),
            pl.BlockSpec((1, D_MODEL), lambda i: (0, 0)),
            pl.BlockSpec((None, 1, 6 * D_MODEL), lambda i: (_cond_row(i, tm), 0, 0)),
            pl.BlockSpec(w_qkv.shape, lambda i: (0, 0)),
            pl.BlockSpec((1, HEAD_DIM), lambda i: (0, 0)),
            pl.BlockSpec((1, HEAD_DIM), lambda i: (0, 0)),
            pl.BlockSpec((tm, HEAD_DIM), pos_map),
            pl.BlockSpec((tm, HEAD_DIM), pos_map),
        ],
        out_specs=(pl.BlockSpec((tm, N_HEADS * HEAD_DIM), lambda i: (i, 0)),
                   pl.BlockSpec((tm, kvw), lambda i: (i, 0)),
                   pl.BlockSpec((tm, kvw), lambda i: (i, 0)),
                   pl.BlockSpec((tm, KV_HEADS, HEAD_DIM), prompt_map),
                   pl.BlockSpec((tm, KV_HEADS, HEAD_DIM), prompt_map)),
        compiler_params=pltpu.CompilerParams(vmem_limit_bytes=VMEM_LIMIT),
        name="odd_qkv",
    )(x, gn, mods, w_qkv, q_norm, k_norm, cos_t, sin_t)


def _rope_tables():
    f32 = np.float32
    rows = DEC_SEQ // GRID_W
    row = np.repeat(np.arange(rows, dtype=f32), GRID_W)
    col = np.tile(np.arange(GRID_W, dtype=f32), rows)
    inv = np.power(f32(ROPE_THETA), -np.arange(0, AXIS_DIM, 2, dtype=f32) / f32(AXIS_DIM)).astype(f32)
    ang = np.concatenate([row[:, None] * inv, col[:, None] * inv], axis=-1).astype(f32)
    cos_t = np.repeat(np.cos(ang), 2, axis=-1).astype(f32)
    sin = np.sin(ang).astype(f32)
    sin_t = np.stack([-sin, sin], axis=-1).reshape(DEC_SEQ, HEAD_DIM)
    return jnp.asarray(cos_t), jnp.asarray(sin_t)


def _attn_kernel(*refs, seq_len, has_cache):
    q_ref, k_ref, v_ref = refs[:3]
    ck_ref, cv_ref = refs[3:5] if has_cache else (None, None)
    o_ref = refs[-1]
    c = HEAD_DIM ** -0.5 * math.log2(math.e)
    ones_col = (lax.broadcasted_iota(jnp.int32, (1, HEAD_DIM), 1) == 0).astype(BF16)

    def with_ones(v):
        return jnp.concatenate([v, jnp.broadcast_to(ones_col, v.shape)], axis=1)

    if has_cache:
        ck = ck_ref[...].astype(BF16)
        cv = with_ones(cv_ref[...].astype(BF16))
    def one_sequence(rows):
        k = k_ref[rows, :]
        v = with_ones(v_ref[rows, :])
        for r in range(Q_PER_KV):
            cs = slice(r * HEAD_DIM, (r + 1) * HEAD_DIM)
            q = q_ref[rows, cs]
            s = lax.dot_general(q, k, NT_DIMS, preferred_element_type=F32)
            m = jnp.max(s, axis=-1, keepdims=True)
            if has_cache:
                sc = lax.dot_general(q, ck, NT_DIMS, preferred_element_type=F32)
                m = jnp.maximum(m, jnp.max(sc, axis=-1, keepdims=True))
            mc = m * c
            o = _dot(jnp.exp2(s * c - mc).astype(BF16), v)
            if has_cache:
                o = o + _dot(jnp.exp2(sc * c - mc).astype(BF16), cv)
            o_ref[rows, cs] = (o[:, 0:HEAD_DIM] / o[:, HEAD_DIM:HEAD_DIM + 1]).astype(BF16)

    nsub = q_ref.shape[0] // seq_len
    if nsub == 1:
        one_sequence(slice(0, seq_len))
    else:
        def body(j, carry):
            one_sequence(pl.ds(pl.multiple_of(j * seq_len, seq_len), seq_len))
            return carry

        lax.fori_loop(0, nsub, body, 0)


def _attn_call(q, k, v, cache_k, cache_v, seq_len, row0, nrows, nsub):
    blk = nsub * seq_len
    assert row0 % blk == 0 and nrows % blk == 0
    has_cache = cache_k is not None
    assert not has_cache or nsub == 1
    b0 = row0 // blk
    gw = Q_PER_KV * HEAD_DIM
    in_specs = [
        pl.BlockSpec((blk, gw), lambda b, g: (b0 + b, g)),
        pl.BlockSpec((blk, HEAD_DIM), lambda b, g: (b0 + b, g)),
        pl.BlockSpec((blk, HEAD_DIM), lambda b, g: (b0 + b, g)),
    ]
    args = [q, k, v]
    if has_cache:
        in_specs += [pl.BlockSpec((PAST_LEN, HEAD_DIM), lambda b, g: (b, g)),
                     pl.BlockSpec((PAST_LEN, HEAD_DIM), lambda b, g: (b, g))]
        args += [cache_k, cache_v]
    return pl.pallas_call(
        functools.partial(_attn_kernel, seq_len=seq_len, has_cache=has_cache),
        out_shape=jax.ShapeDtypeStruct((nrows, N_HEADS * HEAD_DIM), BF16),
        grid=(nrows // blk, KV_HEADS),
        in_specs=in_specs,
        out_specs=pl.BlockSpec((blk, gw), lambda b, g: (b, g)),
        compiler_params=pltpu.CompilerParams(vmem_limit_bytes=VMEM_LIMIT),
        name=f"attn_len{seq_len}",
    )(*args)


def kernel(x_prompt, x_sample, state_s5_re, state_s5_im, state_gla, cache_k, cache_v, c, c_ctx, norm_mix, norm_mlp, w_ada, b_ada, w_mlp_in, w_mlp_out, w_in_e, w_out_e, s5_lambda_re, s5_lambda_im, s5_log_dt, s5_b_re, s5_b_im, s5_c_re, s5_c_im, s5_d, s5_w_glu, s5_b_glu, gla_w_gate2, gla_b_gate, gla_norm, w_qkv_o, w_o_o, q_norm, k_norm):
    xp = x_prompt.reshape(T_PROMPT, D_MODEL)
    xs = x_sample.reshape(T_SAMPLE, D_MODEL)
    cond8 = jnp.concatenate([c_ctx[None, :], c, jnp.zeros((COND_ROWS - 1 - DEC_BATCH, D_MODEL), F32)], axis=0)
    b_ada3 = b_ada.reshape(DEPTH, 1, 6 * D_MODEL)
    w1_all, w2_all = w_mlp_in, w_mlp_out
    mats, mods0 = _s5_prep_call(s5_lambda_re[0], s5_lambda_im[0], s5_log_dt[0], s5_b_re[0], s5_b_im[0],
                                s5_c_re[0], s5_c_im[0], ada=(cond8, w_ada, b_ada3, 0))

    w_in = jnp.pad(w_in_e[0], ((0, 0), (0, LANES - 2 * GLA_RANK))).astype(BF16)
    zg = jnp.zeros((GLA_RANK, GLA_QK), F32)
    w_gate = jnp.concatenate([jnp.concatenate([gla_w_gate2[0, 0], zg], axis=1),
                              jnp.concatenate([zg, gla_w_gate2[0, 1]], axis=1),
                              jnp.zeros((LANES - 2 * GLA_RANK, 2 * GLA_QK), F32)], axis=0).astype(BF16)
    b_gate = gla_b_gate[0].reshape(1, 2 * GLA_QK)
    u, z, g = _inproj_call(xp, xs, norm_mix[0:1], mods0, 0, w_in, w_gate, b_gate)

    def state_rows(s):
        return jnp.transpose(s, (2, 0, 1, 3)).reshape(S5_GROUPS, DEC_BATCH, 2 * S5_STATE)

    h0 = jnp.concatenate([state_rows(state_s5_re[:, 0]), state_rows(state_s5_im[:, 0])], axis=-1)
    nsteps = S5_GROUPS // S5_GPB
    h0 = jnp.transpose(h0.reshape(nsteps, S5_GPB, DEC_BATCH, S5_W), (0, 2, 1, 3))
    y5, ns = _s5_call(u, mats, h0)
    ns = jnp.transpose(ns, (0, 2, 1, 3)).reshape(S5_GROUPS, BATCH, S5_W)

    def state_out(n):
        return jnp.transpose(n.reshape(S5_GROUPS, BATCH, 2, S5_STATE), (1, 2, 0, 3))[:, None]

    new_s5_re = state_out(ns[:, :, :2 * S5_STATE])
    new_s5_im = state_out(ns[:, :, 2 * S5_STATE:])

    gn_gla = gla_norm[0].reshape(1, GLA_DV)
    gla_p, sfin, w_glu, w_out, w_qkv, w_o = _gla_call(z, g, gn_gla, None, SEQ, BATCH, 0, nsub=4,
                                                      casts=(s5_w_glu, w_out_e, w_qkv_o, w_o_o))
    s0 = state_gla[:, 0].reshape(DEC_BATCH, 2, GLA_QK, GLA_DV)
    gla_s, _, mods1 = _gla_call(z, g, gn_gla, s0, DEC_SEQ, DEC_BATCH, T_PROMPT, nsub=1,
                                ada=(cond8, w_ada, b_ada3, 1))
    new_gla = sfin.reshape(BATCH, 1, 2, GLA_HEADS, GLA_DK, GLA_DV)

    x = _even_out_call(xp, xs, y5, u, s5_d[0].reshape(1, S5_WIDTH), w_glu,
                       s5_b_glu[0].reshape(1, S5_WIDTH), gla_p, gla_s, w_out, mods0, 0,
                       norm_mlp[0:1], w1_all, w2_all)

    cos_t, sin_t = _rope_tables()
    q, k, v, k32, v32 = _qkv_call(x, norm_mix[1:2], mods1, 1, w_qkv,
                                  q_norm[0].reshape(1, HEAD_DIM), k_norm[0].reshape(1, HEAD_DIM), cos_t, sin_t)
    att_p = _attn_call(q, k, v, None, None, SEQ, 0, T_PROMPT, nsub=4)
    ck = cache_k[:, 0].reshape(DEC_BATCH * PAST_LEN, KV_HEADS * HEAD_DIM)
    cv = cache_v[:, 0].reshape(DEC_BATCH * PAST_LEN, KV_HEADS * HEAD_DIM)
    att_s = _attn_call(q, k, v, ck, cv, DEC_SEQ, T_PROMPT, T_SAMPLE, nsub=1)
    yp, ys = _odd_out_call(x, att_p, att_s, w_o, mods1, 1, norm_mlp[1:2],
                           w1_all, w2_all)

    new_k = k32.reshape(BATCH, 1, SEQ, KV_HEADS, HEAD_DIM)
    new_v = v32.reshape(BATCH, 1, SEQ, KV_HEADS, HEAD_DIM)
    y_prompt = yp.reshape(BATCH, SEQ, D_MODEL)
    y_sample = ys.reshape(DEC_BATCH, DEC_SEQ, D_MODEL)
    return (y_prompt, y_sample, new_s5_re, new_s5_im, new_gla, new_k, new_v)
```

```python
import functools
import math

import jax
import jax.numpy as jnp
import numpy as np
from jax import lax
from jax.experimental import pallas as pl
from jax.experimental.pallas import tpu as pltpu

F32 = jnp.float32
BF16 = jnp.bfloat16

LANES = 128

D_MODEL = 1024
BATCH = 16
SEQ = 256
DEPTH = 2
DEC_BATCH = 4
DEC_SEQ = 1024
PAST_LEN = 512
GRID_W = 64
S5_WIDTH = 512
S5_GROUP_CH = 16
S5_GROUPS = 32
S5_STATE = 64
GLA_HEADS = 4
GLA_VW = 512
GLA_DV = 128
GLA_DK = 64
GLA_QK = 256
GLA_RANK = 16
GLA_TAU = 16.0
GLA_CHUNK = 64
GLA_CPB = 4
GLA_BLK = GLA_CPB * GLA_CHUNK
HEAD_DIM = 128
N_HEADS = 8
KV_HEADS = 2
Q_PER_KV = N_HEADS // KV_HEADS
AXIS_DIM = 64
ROPE_THETA = 10000.0
D_FF = 4096
EPS = 1e-6

T_PROMPT = BATCH * SEQ
T_SAMPLE = DEC_BATCH * DEC_SEQ
T_TOK = T_PROMPT + T_SAMPLE
COND_ROWS = 8
COND_SPAN = 1024
PROMPT_SPANS = T_PROMPT // COND_SPAN

S5_Q = 16
S5_W = S5_Q * S5_GROUP_CH
S5_GPB = LANES // S5_GROUP_CH
S5_UBLK = S5_WIDTH // LANES
S5_ROWS = T_TOK // S5_Q
S5_PROMPT_ROWS = T_PROMPT // S5_Q
S5_PROMPT_CHUNKS = SEQ // S5_Q
S5_SAMPLE_CHUNKS = DEC_SEQ // S5_Q

VMEM_LIMIT = 56 * 1024 * 1024
VMEM_LIMIT_TAIL = 60 * 1024 * 1024

NT_DIMS = (((1,), (1,)), ((), ()))


def _cond_row(i, tile):
    return jnp.maximum((i * tile) // COND_SPAN - (PROMPT_SPANS - 1), 0)


def _norm_mod(x, gain, shift, scale):
    y = x * lax.rsqrt(jnp.mean(x * x, axis=-1, keepdims=True) + EPS)
    return (y * gain) * (1.0 + scale) + shift


def _dot(a, b):
    return jnp.dot(a, b, preferred_element_type=F32)


def _ada_kernel(cond_ref, w_ref, b_ref, o_ref):
    s = jax.nn.silu(cond_ref[...]).astype(BF16)
    o_ref[:, 0, :] = _dot(s, w_ref[...].astype(BF16)) + b_ref[...]


ADA_SHAPE = jax.ShapeDtypeStruct((COND_ROWS, 1, 6 * D_MODEL), F32)


def _ada_specs(layer, tn):
    return ([pl.BlockSpec((COND_ROWS, D_MODEL), lambda *ids: (0, 0)),
             pl.BlockSpec((None, D_MODEL, tn), lambda *ids: (layer, 0, ids[-1])),
             pl.BlockSpec((None, 1, tn), lambda *ids: (layer, 0, ids[-1]))],
            pl.BlockSpec((COND_ROWS, 1, tn), lambda *ids: (0, 0, ids[-1])))


def _token_specs(tile, width=D_MODEL):
    n_prompt = T_PROMPT // tile
    return [pl.BlockSpec((tile, width), lambda i: (jnp.minimum(i, n_prompt - 1), 0)),
            pl.BlockSpec((tile, width), lambda i: (jnp.maximum(i - n_prompt, 0), 0))]


def _token_tile(xp_ref, xs_ref, tile):
    return jnp.where(pl.program_id(0) < T_PROMPT // tile, xp_ref[...], xs_ref[...])


def _inproj_kernel(xp_ref, xs_ref, gn_ref, m_ref, w_ref, wglr_ref, wg_ref, bg_ref, u_ref, z_ref, g_ref, *, tile):
    x = _token_tile(xp_ref, xs_ref, tile)
    h = _norm_mod(x, gn_ref[...], m_ref[:, 0:D_MODEL], m_ref[:, D_MODEL:2 * D_MODEL]).astype(BF16)
    z = _dot(h, w_ref[...])
    for blk in range(S5_UBLK):
        u_ref[blk] = z[:, blk * LANES:(blk + 1) * LANES]
    z_ref[...] = z[:, S5_WIDTH:]
    glr = _dot(h, wglr_ref[...]).astype(BF16)
    pre = _dot(glr, wg_ref[...]) + bg_ref[...]
    g_ref[...] = jax.nn.log_sigmoid(pre) * (1.0 / GLA_TAU)


def _inproj_call(xp, xs, gn, mods, layer, w_in, w_gate, b_gate):
    tm = 512
    nz = w_in.shape[1] - LANES
    return pl.pallas_call(
        functools.partial(_inproj_kernel, tile=tm),
        out_shape=(jax.ShapeDtypeStruct((S5_UBLK, T_TOK, LANES), F32),
                   jax.ShapeDtypeStruct((T_TOK, nz - S5_WIDTH), F32),
                   jax.ShapeDtypeStruct((T_TOK, 2 * GLA_QK), F32)),
        grid=(T_TOK // tm,),
        in_specs=_token_specs(tm) + [
            pl.BlockSpec((1, D_MODEL), lambda i: (0, 0)),
            pl.BlockSpec((None, 1, 6 * D_MODEL), lambda i: (_cond_row(i, tm), 0, 0)),
            pl.BlockSpec((D_MODEL, nz), lambda i: (0, 0)),
            pl.BlockSpec((D_MODEL, LANES), lambda i: (0, nz // LANES)),
            pl.BlockSpec((LANES, 2 * GLA_QK), lambda i: (0, 0)),
            pl.BlockSpec((1, 2 * GLA_QK), lambda i: (0, 0)),
        ],
        out_specs=(pl.BlockSpec((S5_UBLK, tm, LANES), lambda i: (0, i, 0)),
                   pl.BlockSpec((tm, nz - S5_WIDTH), lambda i: (i, 0)),
                   pl.BlockSpec((tm, 2 * GLA_QK), lambda i: (i, 0))),
        compiler_params=pltpu.CompilerParams(vmem_limit_bytes=VMEM_LIMIT),
        name="even_inproj",
    )(xp, xs, gn, mods, w_in, w_in, w_gate, b_gate)


S5_PREP_GPB = 8
_PREP_LRE, _PREP_LIM, _PREP_LDT = 0, 1, 2
_PREP_BT_RE, _PREP_BT_IM, _PREP_C_RE, _PREP_C_IM, _PREP_ROWS = 8, 24, 40, 56, 72


def _s5_prep_kernel(p_ref, cc_ref, cond_ref, wada_ref, bada_ref, t_ref, bq_ref, cqt_ref, be_ref, a_ref, mods_ref,
                    t_scr, dd_scr):
    _ada_kernel(cond_ref, wada_ref, bada_ref, mods_ref)
    for gi in range(S5_PREP_GPB):
        _s5_prep_group(p_ref.at[gi], cc_ref.at[gi], t_ref.at[gi], bq_ref.at[gi], cqt_ref.at[gi], be_ref.at[gi],
                       a_ref.at[gi], t_scr, dd_scr)


def _s5_prep_group(p_ref, cc_ref, t_ref, bq_ref, cqt_ref, be_ref, a_ref, t_scr, dd_scr):
    gch = S5_GROUP_CH
    lre = p_ref[_PREP_LRE:_PREP_LRE + 1]
    lim = p_ref[_PREP_LIM:_PREP_LIM + 1]
    dt = jnp.exp(p_ref[_PREP_LDT:_PREP_LDT + 1])
    a = lre * dt
    th = lim * dt

    def lam_pow(k):
        mag = jnp.exp(k * a)
        return mag * jnp.cos(k * th), mag * jnp.sin(k * th)

    lb_re, lb_im = lam_pow(1.0)
    nr = lb_re - 1.0
    den = lre * lre + lim * lim
    cf_re = (nr * lre + lb_im * lim) / den
    cf_im = (lb_im * lre - nr * lim) / den
    bt_re = p_ref[_PREP_BT_RE:_PREP_BT_RE + gch]
    bt_im = p_ref[_PREP_BT_IM:_PREP_BT_IM + gch]
    bb_re = jnp.tile(cf_re * bt_re - cf_im * bt_im, (S5_Q, 1))
    bb_im = jnp.tile(cf_re * bt_im + cf_im * bt_re, (S5_Q, 1))

    shape = (S5_W, LANES)
    pos = lax.shift_right_logical(lax.broadcasted_iota(jnp.int32, shape, 0), 4)
    is_f = lax.broadcasted_iota(jnp.int32, shape, 1) < S5_STATE
    posq = lax.broadcasted_iota(jnp.int32, (S5_Q, LANES), 0).astype(F32)
    is_fq = lax.broadcasted_iota(jnp.int32, (S5_Q, LANES), 1) < S5_STATE

    def per_channel(tbl):
        return jnp.broadcast_to(tbl[:, None, :], (S5_Q, S5_GROUP_CH, LANES)).reshape(shape)

    p_re, p_im = map(per_channel, lam_pow(jnp.where(is_fq, (S5_Q - 1.0) - posq, posq)))
    w_re = p_re * bb_re - p_im * bb_im
    w_im = p_re * bb_im + p_im * bb_re
    bq = jnp.concatenate([w_re, w_im], axis=1)
    bqt = jnp.transpose(bq)
    bq_ref[...] = bqt.astype(BF16)

    edge = pos == jnp.where(is_f, 0, S5_Q - 1)
    be = jnp.concatenate([jnp.where(edge, bb_re, 0.0), jnp.where(edge, bb_im, 0.0)], axis=1)
    be_ref[...] = jnp.transpose(be).astype(BF16)

    q_re, q_im = map(per_channel, lam_pow(jnp.where(is_fq, posq + 1.0, S5_Q - posq)))
    ct_re = jnp.tile(p_ref[_PREP_C_RE:_PREP_C_RE + gch], (S5_Q, 1))
    ct_im = jnp.tile(p_ref[_PREP_C_IM:_PREP_C_IM + gch], (S5_Q, 1))
    g_re = q_re * ct_re - q_im * ct_im
    g_im = q_re * ct_im + q_im * ct_re
    cqt_ref[...] = jnp.concatenate([g_re, -g_im], axis=1).astype(BF16)

    a_re, a_im = lam_pow(float(S5_Q))
    a_ref[...] = jnp.concatenate([a_re, a_im], axis=1)

    kf = jnp.dot(cc_ref[0:gch], bqt, precision=lax.Precision.HIGHEST, preferred_element_type=F32)
    kb = jnp.dot(cc_ref[gch:2 * gch], bqt, precision=lax.Precision.HIGHEST, preferred_element_type=F32)
    lo = S5_W - gch
    dd_scr[:, 0:S5_W] = kf
    dd_scr[:, lo:lo + S5_W] = kb
    dd_scr[:, lo:S5_W] = kf[:, lo:S5_W] + kb[:, 0:gch]
    for t in range(S5_Q):
        c0 = (S5_Q - 1 - t) * gch
        t_scr[t * gch:(t + 1) * gch, :] = dd_scr[:, c0:c0 + S5_W]
    t_ref[...] = t_scr[...].astype(BF16)


def _s5_prep_call(lam_re, lam_im, log_dt, b_re, b_im, c_re, c_im, ada):
    per_dir = jnp.stack([lam_re, lam_im, jnp.broadcast_to(log_dt[:, :, None], lam_re.shape)])
    head = jnp.transpose(per_dir, (2, 0, 1, 3)).reshape(S5_GROUPS, 3, 2 * S5_STATE)
    pad = jnp.zeros((S5_GROUPS, _PREP_BT_RE - _PREP_LDT - 1, LANES), F32)
    shared = jnp.concatenate([jnp.transpose(b_re, (0, 2, 1)), jnp.transpose(b_im, (0, 2, 1)), c_re, c_im],
                             axis=1)
    packed = jnp.concatenate([head, pad, jnp.concatenate([shared, shared], axis=-1)], axis=1)
    zero = jnp.zeros_like(c_re)
    cc = jnp.concatenate([jnp.concatenate([c_re, zero, -c_im, zero], axis=-1),
                          jnp.concatenate([zero, c_re, zero, -c_im], axis=-1)], axis=1)

    gpb = S5_PREP_GPB
    nsteps = S5_GROUPS // gpb
    sq = pl.BlockSpec((gpb, S5_W, S5_W), lambda g: (g, 0, 0))
    sq_shape = jax.ShapeDtypeStruct((S5_GROUPS, S5_W, S5_W), BF16)
    cond8, w_ada, b_ada, ada_layer = ada
    ada_in, ada_out = _ada_specs(ada_layer, 6 * D_MODEL // nsteps)
    *mats, mods = pl.pallas_call(
        _s5_prep_kernel,
        out_shape=(sq_shape, sq_shape, sq_shape, sq_shape,
                   jax.ShapeDtypeStruct((S5_GROUPS, 1, S5_W), F32), ADA_SHAPE),
        grid=(nsteps,),
        in_specs=[pl.BlockSpec((gpb, _PREP_ROWS, LANES), lambda g: (g, 0, 0)),
                  pl.BlockSpec((gpb, 2 * S5_GROUP_CH, S5_W), lambda g: (g, 0, 0))] + ada_in,
        out_specs=(sq, sq, sq, sq, pl.BlockSpec((gpb, 1, S5_W), lambda g: (g, 0, 0)), ada_out),
        scratch_shapes=[pltpu.VMEM((S5_W, S5_W), F32), pltpu.VMEM((S5_GROUP_CH, 2 * S5_W), F32)],
        compiler_params=pltpu.CompilerParams(vmem_limit_bytes=VMEM_LIMIT),
        name="s5_prep",
    )(packed, cc, cond8, w_ada, b_ada)
    return mats, mods


def _s5_kernel(u_ref, tt_ref, bqt_ref, cqt_ref, bet_ref, a_ref, h0_ref, y_ref, ns_ref,
               ut_scr, x_scr, spf_scr, spb_scr, ne_scr, yt_scr, xt_scr):
    gch = S5_GROUP_CH
    for s in range(S5_Q):
        rows = u_ref[pl.ds(s, S5_ROWS, stride=S5_Q), :]
        rows_t = jnp.transpose(rows).astype(BF16)
        for gl in range(S5_GPB):
            ut_scr[gl, s * gch:(s + 1) * gch, :] = rows_t[gl * gch:(gl + 1) * gch, :]

    for gl in range(S5_GPB):
        ut = ut_scr[gl]
        xt_scr[...] = _dot(bqt_ref[gl], ut)
        x = jnp.transpose(xt_scr[...])
        xt_scr[:, 0:S5_PROMPT_ROWS] = _dot(bet_ref[gl], ut[:, 0:S5_PROMPT_ROWS])
        ne = jnp.transpose(xt_scr[:, 0:S5_PROMPT_ROWS])
        for part in range(2):
            x_scr[part, pl.ds(gl, S5_ROWS, stride=S5_GPB), :] = x[:, part * LANES:(part + 1) * LANES]
            ne_scr[part, pl.ds(gl, S5_PROMPT_ROWS, stride=S5_GPB), :] = ne[:, part * LANES:(part + 1) * LANES]

    is_f = lax.broadcasted_iota(jnp.int32, (1, LANES), 1) < S5_STATE
    a_re = a_ref[:, 0:LANES]
    a_im = a_ref[:, LANES:2 * LANES]

    def tile(row):
        return pl.ds(pl.multiple_of(row * S5_GPB, S5_GPB), S5_GPB)

    def scan(base, nseq, nchunk, s_init):
        def body(i, state):
            new = []
            for b in range(nseq):
                s_re, s_im = state[b]
                rows_f = tile(base + b * nchunk + i)
                rows_b = tile(base + b * nchunk + (nchunk - 1 - i))
                spf_scr[0, rows_f, :] = s_re
                spf_scr[1, rows_f, :] = s_im
                spb_scr[0, rows_b, :] = s_re
                spb_scr[1, rows_b, :] = s_im
                x_re = jnp.where(is_f, x_scr[0, rows_f, :], x_scr[0, rows_b, :])
                x_im = jnp.where(is_f, x_scr[1, rows_f, :], x_scr[1, rows_b, :])
                new.append((a_re * s_re - a_im * s_im + x_re, a_re * s_im + a_im * s_re + x_im))
            return tuple(new)

        lax.fori_loop(0, nchunk, body, tuple(s_init))

    zero = jnp.zeros((S5_GPB, LANES), F32)
    scan(0, BATCH, S5_PROMPT_CHUNKS, [(zero, zero)] * BATCH)
    scan(S5_PROMPT_ROWS, DEC_BATCH, S5_SAMPLE_CHUNKS,
         [(h0_ref[b, :, 0:LANES], h0_ref[b, :, LANES:2 * LANES]) for b in range(DEC_BATCH)])

    for b in range(BATCH):
        first = pl.ds(b * S5_PROMPT_CHUNKS * S5_GPB, S5_GPB)
        last = pl.ds(((b + 1) * S5_PROMPT_CHUNKS - 1) * S5_GPB, S5_GPB)
        for part in range(2):
            ns_ref[b, :, part * LANES:(part + 1) * LANES] = jnp.where(is_f, ne_scr[part, first, :], ne_scr[part, last, :])

    for gl in range(S5_GPB):
        rows = pl.ds(gl, S5_ROWS, stride=S5_GPB)
        carried = jnp.concatenate([jnp.where(is_f, spf_scr[p, rows, :], spb_scr[p, rows, :]) for p in range(2)],
                                  axis=1).astype(BF16)
        yt = _dot(tt_ref[gl], ut_scr[gl]) + lax.dot_general(cqt_ref[gl], carried, NT_DIMS,
                                                            preferred_element_type=F32)
        for t in range(S5_Q):
            yt_scr[t, gl * gch:(gl + 1) * gch, :] = yt[t * gch:(t + 1) * gch, :]
    for t in range(S5_Q):
        y_ref[pl.ds(t, S5_ROWS, stride=S5_Q), :] = jnp.transpose(yt_scr[t])


def _s5_call(u, mats, h0):
    tt_m, bqt_m, cqt_m, bet_m, a_m = mats
    nsteps = S5_GROUPS // S5_GPB
    sq = pl.BlockSpec((S5_GPB, S5_W, S5_W), lambda g: (g, 0, 0))
    state_scr = pltpu.VMEM((2, S5_ROWS * S5_GPB, LANES), F32)
    return pl.pallas_call(
        _s5_kernel,
        out_shape=(jax.ShapeDtypeStruct((nsteps, T_TOK, LANES), F32),
                   jax.ShapeDtypeStruct((nsteps, BATCH, S5_GPB, S5_W), F32)),
        grid=(nsteps,),
        in_specs=[
            pl.BlockSpec((None, T_TOK, LANES), lambda g: (g, 0, 0)),
            sq, sq, sq, sq,
            pl.BlockSpec((S5_GPB, S5_W), lambda g: (g, 0)),
            pl.BlockSpec((None, DEC_BATCH, S5_GPB, S5_W), lambda g: (g, 0, 0, 0)),
        ],
        out_specs=(pl.BlockSpec((None, T_TOK, LANES), lambda g: (g, 0, 0)),
                   pl.BlockSpec((None, BATCH, S5_GPB, S5_W), lambda g: (g, 0, 0, 0))),
        scratch_shapes=[pltpu.VMEM((S5_GPB, S5_W, S5_ROWS), BF16), state_scr, state_scr, state_scr,
                        pltpu.VMEM((2, S5_PROMPT_ROWS * S5_GPB, LANES), F32),
                        pltpu.VMEM((S5_Q, LANES, S5_ROWS), F32), pltpu.VMEM((S5_W, S5_ROWS), F32)],
        compiler_params=pltpu.CompilerParams(vmem_limit_bytes=VMEM_LIMIT),
        name="s5_scan",
    )(u, tt_m, bqt_m, cqt_m, bet_m, a_m.reshape(S5_GROUPS, S5_W), h0)


def _split_bf16(x):
    hi = x.astype(BF16)
    r1 = x - hi.astype(F32)
    mid = r1.astype(BF16)
    lo = (r1 - mid.astype(F32)).astype(BF16)
    return hi, mid, lo


def _cast_specs(shape, nsteps):
    _, rows, cols = shape
    rb = rows // nsteps
    return (pl.BlockSpec((None, rb, cols), lambda *ids: (0, ids[-1], 0)),
            pl.BlockSpec((rb, cols), lambda *ids: (ids[-1], 0)))


def _gla_kernel(*refs, seq_len, nsub, has_s0, has_ada, n_cast):
    rows_refs, gn_ref = refs[:6], refs[6]
    n_ada_in = 7 + has_s0
    n_cast_in = n_ada_in + 3 * has_ada
    n_in = n_cast_in + n_cast
    s0_ref = refs[7] if has_s0 else None
    o_ref, sfin_ref = refs[n_in:n_in + 2]
    n_out = 2 + has_ada + n_cast
    scratch = refs[n_in + n_out:]
    if has_ada:
        _ada_kernel(*refs[n_ada_in:n_cast_in], refs[n_in + 2])
    for src, dst in zip(refs[n_cast_in:n_in], refs[n_in + 2 + has_ada:n_in + n_out]):
        dst[...] = src[...].astype(BF16)
    for j in range(nsub):
        rows = pl.ds(j * seq_len, seq_len)
        _gla_sequence(*[r.at[rows, :] for r in rows_refs], gn_ref, s0_ref.at[j] if has_s0 else None,
                      o_ref.at[rows, :], sfin_ref.at[j], *[s.at[j] for s in scratch], seq_len=seq_len)


def _gla_sequence(q_ref, k_ref, v_ref, gf_ref, gb_ref, r_ref, gn_ref, s0_ref, o_ref, sfin_ref,
                  oi_scr, qd_scr, kv_scr, dec_scr, ss_scr, *, seq_len):
    has_s0 = s0_ref is not None
    nblk = seq_len // GLA_BLK
    nchunk = seq_len // GLA_CHUNK
    cl = GLA_CHUNK
    ti = lax.broadcasted_iota(jnp.int32, (GLA_BLK, GLA_BLK), 0)
    si = lax.broadcasted_iota(jnp.int32, (GLA_BLK, GLA_BLK), 1)
    same = lax.shift_right_logical(ti, 6) == lax.shift_right_logical(si, 6)
    keep = (same & (ti >= si), same & (ti <= si))
    tri = tuple(kp.astype(BF16) for kp in keep)
    lane_head = lax.shift_right_logical(lax.broadcasted_iota(jnp.int32, (cl, GLA_QK), 1), 6)
    zeros_v = jnp.zeros((cl, GLA_DV), BF16)
    heads = [(slice(h * GLA_DK, (h + 1) * GLA_DK), slice(h * GLA_DV, (h + 1) * GLA_DV)) for h in range(GLA_HEADS)]

    for j in range(nblk):
        rows = slice(j * GLA_BLK, (j + 1) * GLA_BLK)
        q = q_ref[rows, :] * (GLA_DK ** -0.5)
        k = k_ref[rows, :]
        v = v_ref[rows, :].astype(BF16)
        qd, kd, k2t = [], [], []
        for d, g_ref in enumerate((gf_ref, gb_ref)):
            b = sum(_dot(tri[d], part) for part in _split_bf16(g_ref[rows, :]))
            last = cl - 1 if d == 0 else 0
            b_last = [b[c * cl + last:c * cl + last + 1] for c in range(GLA_CPB)]
            bl = jnp.concatenate([jnp.broadcast_to(x, (cl, GLA_QK)) for x in b_last], axis=0)
            qd_d = (q * jnp.exp(b)).astype(BF16)
            qd_scr[d, rows, :] = qd_d
            qd.append(qd_d)
            kd.append((k * jnp.exp(-b)).astype(BF16))
            k2t.append(jnp.transpose(k * jnp.exp(bl - b)).astype(BF16))
            for c in range(GLA_CPB):
                dec_scr[d, j * GLA_CPB + c] = jnp.exp(jnp.transpose(jnp.broadcast_to(b_last[c], (GLA_DV, GLA_QK))))
        for h, (ks, vs) in enumerate(heads):
            att = [jnp.where(keep[d], lax.dot_general(qd[d][:, ks], kd[d][:, ks], NT_DIMS,
                                                      preferred_element_type=F32), 0.0) for d in range(2)]
            oi_scr[rows, vs] = _dot((att[0] + att[1]).astype(BF16), v[:, vs])
            vh = v[:, vs]
            vexp = jnp.concatenate(
                [jnp.concatenate([vh[c * cl:(c + 1) * cl] if c2 == c else zeros_v for c2 in range(GLA_CPB)], axis=1)
                 for c in range(GLA_CPB)], axis=0)
            for d in range(2):
                kv_scr[d, j, h] = _dot(k2t[d][ks, :], vexp)

    for d in range(2):
        s = s0_ref[d] if has_s0 else jnp.zeros((GLA_QK, GLA_DV), F32)
        for cg in (range(nchunk) if d == 0 else range(nchunk - 1, -1, -1)):
            j, c = divmod(cg, GLA_CPB)
            ss_scr[d, cg] = s.astype(BF16)
            kv = jnp.concatenate([kv_scr[d, j, h, :, c * GLA_DV:(c + 1) * GLA_DV] for h in range(GLA_HEADS)], axis=0)
            s = s * dec_scr[d, cg] + kv
        sfin_ref[d] = s

    for cg in range(nchunk):
        rows = slice(cg * cl, (cg + 1) * cl)
        inter = []
        for d in range(2):
            qc = qd_scr[d, rows, :]
            qstack = jnp.concatenate([jnp.where(lane_head == h, qc, jnp.zeros_like(qc)) for h in range(GLA_HEADS)],
                                     axis=0)
            inter.append(_dot(qstack, ss_scr[d, cg]))
        gate = jax.nn.silu(r_ref[rows, :])
        for h, (ks, vs) in enumerate(heads):
            hr = slice(h * cl, (h + 1) * cl)
            oh = oi_scr[rows, vs] + inter[0][hr] + inter[1][hr]
            oh = oh * lax.rsqrt(jnp.mean(oh * oh, axis=-1, keepdims=True) + EPS) * gn_ref[...]
            o_ref[rows, vs] = oh * gate[:, vs]


def _gla_call(z, g, gla_norm, s0, seq_len, nseq, row0, nsub, ada=None, casts=()):
    blk = nsub * seq_len
    assert row0 % blk == 0 and nseq % nsub == 0
    r0 = row0 // blk
    has_s0 = s0 is not None
    qk_off = 0
    v_off = 2 * GLA_QK // GLA_VW
    in_specs = [
        pl.BlockSpec((blk, GLA_QK), lambda i: (r0 + i, qk_off)),
        pl.BlockSpec((blk, GLA_QK), lambda i: (r0 + i, qk_off + 1)),
        pl.BlockSpec((blk, GLA_VW), lambda i: (r0 + i, v_off)),
        pl.BlockSpec((blk, GLA_QK), lambda i: (r0 + i, 0)),
        pl.BlockSpec((blk, GLA_QK), lambda i: (r0 + i, 1)),
        pl.BlockSpec((blk, GLA_VW), lambda i: (r0 + i, v_off + 1)),
        pl.BlockSpec((1, GLA_DV), lambda i: (0, 0)),
    ]
    args = [z, z, z, g, g, z, gla_norm]
    state_spec = pl.BlockSpec((nsub, 2, GLA_QK, GLA_DV), lambda i: (i, 0, 0, 0))
    if has_s0:
        in_specs.append(state_spec)
        args.append(s0)
    out_shape = [jax.ShapeDtypeStruct((nseq * seq_len, GLA_VW), F32),
                 jax.ShapeDtypeStruct((nseq, 2, GLA_QK, GLA_DV), F32)]
    out_specs = [pl.BlockSpec((blk, GLA_VW), lambda i: (i, 0)), state_spec]
    nsteps = nseq // nsub
    if ada is not None:
        cond8, w_ada, b_ada, ada_layer = ada
        ada_in, ada_out = _ada_specs(ada_layer, 6 * D_MODEL // nsteps)
        in_specs += ada_in
        args += [cond8, w_ada, b_ada]
        out_shape.append(ADA_SHAPE)
        out_specs.append(ada_out)
    for w in casts:
        cast_in, cast_out = _cast_specs(w.shape, nsteps)
        in_specs.append(cast_in)
        args.append(w)
        out_shape.append(jax.ShapeDtypeStruct(w.shape[1:], BF16))
        out_specs.append(cast_out)
    return pl.pallas_call(
        functools.partial(_gla_kernel, seq_len=seq_len, nsub=nsub, has_s0=has_s0, has_ada=ada is not None,
                          n_cast=len(casts)),
        out_shape=tuple(out_shape),
        grid=(nsteps,),
        in_specs=in_specs,
        out_specs=tuple(out_specs),
        scratch_shapes=[
            pltpu.VMEM((nsub, seq_len, GLA_VW), F32),
            pltpu.VMEM((nsub, 2, seq_len, GLA_QK), BF16),
            pltpu.VMEM((nsub, 2, seq_len // GLA_BLK, GLA_HEADS, GLA_DK, GLA_CPB * GLA_DV), F32),
            pltpu.VMEM((nsub, 2, seq_len // GLA_CHUNK, GLA_QK, GLA_DV), F32),
            pltpu.VMEM((nsub, 2, seq_len // GLA_CHUNK, GLA_QK, GLA_DV), BF16),
        ],
        compiler_params=pltpu.CompilerParams(vmem_limit_bytes=VMEM_LIMIT),
        name=f"gla_len{seq_len}",
    )(*args)


MLP_CHUNK = 512
MLP_LOAD = 256
MLP_SLOTS = 4


class _MlpWeights:
    def __init__(self, w1_hbm, w2_hbm, w1_scr, w2_scr, stage1, stage2, sem, layer):
        self.refs = (w1_hbm, w2_hbm, w1_scr, w2_scr, stage1, stage2, sem)
        self.layer = layer

    def _copies(self, p):
        w1_hbm, w2_hbm, _, _, stage1, stage2, sem = self.refs
        cols = pl.ds(p * MLP_LOAD, MLP_LOAD)
        slot = p % MLP_SLOTS
        return (pltpu.make_async_copy(w1_hbm.at[self.layer, :, cols], stage1.at[slot], sem.at[0, slot]),
                pltpu.make_async_copy(w2_hbm.at[self.layer, cols, :], stage2.at[slot], sem.at[1, slot]))

    def start(self, p):
        for cp in self._copies(p):
            cp.start()

    def prefetch(self):
        for p in range(MLP_SLOTS - 1):
            self.start(p)

    def finish(self, p):
        _, _, w1_scr, w2_scr, stage1, stage2, _ = self.refs
        ahead = p + MLP_SLOTS - 1
        if ahead < D_FF // MLP_LOAD:
            self.start(ahead)
        for cp in self._copies(p):
            cp.wait()
        cols = slice(p * MLP_LOAD, (p + 1) * MLP_LOAD)
        w1_scr[:, cols] = stage1[p % MLP_SLOTS].astype(BF16)
        w2_scr[cols, :] = stage2[p % MLP_SLOTS].astype(BF16)


def _mlp_tail(x, mix, m_ref, gn2_ref, w1_ref, w2_ref, loading=None):
    y1 = x + m_ref[:, 2 * D_MODEL:3 * D_MODEL] * mix
    h = _norm_mod(y1, gn2_ref[...], m_ref[:, 3 * D_MODEL:4 * D_MODEL], m_ref[:, 4 * D_MODEL:5 * D_MODEL]).astype(BF16)
    nchunk = D_FF // MLP_CHUNK
    acc = jnp.zeros(y1.shape, F32)
    for c in range(nchunk):
        cols = slice(c * MLP_CHUNK, (c + 1) * MLP_CHUNK)
        if loading is not None:
            per = MLP_CHUNK // MLP_LOAD
            for p in range(c * per, (c + 1) * per):
                loading.finish(p)
        a = _dot(h, w1_ref[:, cols])
        a = jnp.square(jnp.maximum(a, 0.0)).astype(BF16)
        acc = acc + _dot(a, w2_ref[cols, :])
    return y1 + m_ref[:, 5 * D_MODEL:6 * D_MODEL] * acc


def _run_tail(x_and_mix, m_ref, gn2_ref, weights, w1_ref, w2_ref, emit):
    first = pl.program_id(0) == 0

    @pl.when(first)
    def _():
        weights.prefetch()
        x, mix = x_and_mix()
        emit(_mlp_tail(x, mix, m_ref, gn2_ref, w1_ref, w2_ref, loading=weights))

    @pl.when(jnp.logical_not(first))
    def _():
        x, mix = x_and_mix()
        emit(_mlp_tail(x, mix, m_ref, gn2_ref, w1_ref, w2_ref))


def _even_out_kernel(xp_ref, xs_ref, y5_ref, u_ref, dskip_ref, wglu_ref, bglu_ref, glap_ref, glas_ref, wout_ref,
                     m_ref, gn2_ref, w1_hbm, w2_hbm, o_ref, w1_ref, w2_ref, stage1, stage2, sem, *, layer):
    weights = _MlpWeights(w1_hbm, w2_hbm, w1_ref, w2_ref, stage1, stage2, sem, layer)

    def x_and_mix():
        nblk = S5_UBLK
        ys = (jnp.concatenate([y5_ref[b] for b in range(nblk)], axis=1)
              + jnp.concatenate([u_ref[b] for b in range(nblk)], axis=1) * dskip_ref[...])
        gl = jax.nn.gelu(ys)
        s5o = gl * jax.nn.sigmoid(_dot(gl.astype(BF16), wglu_ref[...]) + bglu_ref[...])
        gla = _token_tile(glap_ref, glas_ref, _OUT_TM).astype(BF16)
        mix = _dot(s5o.astype(BF16), wout_ref[0:S5_WIDTH, :]) + _dot(gla, wout_ref[S5_WIDTH:, :])
        return _token_tile(xp_ref, xs_ref, _OUT_TM), mix

    def emit(y):
        o_ref[...] = y

    _run_tail(x_and_mix, m_ref, gn2_ref, weights, w1_ref, w2_ref, emit)


def _odd_out_kernel(x_ref, attp_ref, atts_ref, wo_ref, m_ref, gn2_ref, w1_hbm, w2_hbm, op_ref, os_ref,
                    w1_ref, w2_ref, stage1, stage2, sem, *, layer):
    weights = _MlpWeights(w1_hbm, w2_hbm, w1_ref, w2_ref, stage1, stage2, sem, layer)
    is_prompt = pl.program_id(0) < T_PROMPT // _OUT_TM

    def x_and_mix():
        return x_ref[...], _dot(_token_tile(attp_ref, atts_ref, _OUT_TM), wo_ref[...])

    def emit(y):
        @pl.when(is_prompt)
        def _():
            op_ref[...] = y

        @pl.when(jnp.logical_not(is_prompt))
        def _():
            os_ref[...] = y

    _run_tail(x_and_mix, m_ref, gn2_ref, weights, w1_ref, w2_ref, emit)


_OUT_TM = 512


def _const_spec(shape):
    return pl.BlockSpec(shape, lambda i: (0,) * len(shape), pipeline_mode=pl.Buffered(1))


def _tail_specs(layer):
    tm = _OUT_TM
    return [
        pl.BlockSpec((None, 1, 6 * D_MODEL), lambda i: (_cond_row(i, tm), 0, 0)),
        _const_spec((1, D_MODEL)),
        pl.BlockSpec(memory_space=pl.ANY),
        pl.BlockSpec(memory_space=pl.ANY),
    ]


def _tail_scratch():
    return [pltpu.VMEM((D_MODEL, D_FF), BF16), pltpu.VMEM((D_FF, D_MODEL), BF16),
            pltpu.VMEM((MLP_SLOTS, D_MODEL, MLP_LOAD), F32), pltpu.VMEM((MLP_SLOTS, MLP_LOAD, D_MODEL), F32),
            pltpu.SemaphoreType.DMA((2, MLP_SLOTS))]


_TAIL_PARAMS = dict(dimension_semantics=("arbitrary",), vmem_limit_bytes=VMEM_LIMIT_TAIL)


def _even_out_call(xp, xs, y5, u, d_skip, w_glu, b_glu, gla_p, gla_s, w_out, mods, layer, gn2, w1, w2):
    tm = _OUT_TM
    return pl.pallas_call(
        functools.partial(_even_out_kernel, layer=layer),
        out_shape=jax.ShapeDtypeStruct((T_TOK, D_MODEL), F32),
        grid=(T_TOK // tm,),
        in_specs=_token_specs(tm) + [
            pl.BlockSpec((S5_UBLK, tm, LANES), lambda i: (0, i, 0)),
            pl.BlockSpec((S5_UBLK, tm, LANES), lambda i: (0, i, 0)),
            _const_spec((1, S5_WIDTH)),
            _const_spec((S5_WIDTH, S5_WIDTH)),
            _const_spec((1, S5_WIDTH)),
        ] + _token_specs(tm, GLA_VW) + [
            _const_spec((S5_WIDTH + GLA_VW, D_MODEL)),
        ] + _tail_specs(layer),
        out_specs=pl.BlockSpec((tm, D_MODEL), lambda i: (i, 0)),
        scratch_shapes=_tail_scratch(),
        compiler_params=pltpu.CompilerParams(**_TAIL_PARAMS),
        name="even_out_mlp",
    )(xp, xs, y5, u, d_skip, w_glu, b_glu, gla_p, gla_s, w_out, mods, gn2, w1, w2)


def _odd_out_call(x, att_p, att_s, w_o, mods, layer, gn2, w1, w2):
    tm = _OUT_TM
    return pl.pallas_call(
        functools.partial(_odd_out_kernel, layer=layer),
        out_shape=(jax.ShapeDtypeStruct((T_PROMPT, D_MODEL), F32),
                   jax.ShapeDtypeStruct((T_SAMPLE, D_MODEL), F32)),
        grid=(T_TOK // tm,),
        in_specs=[pl.BlockSpec((tm, D_MODEL), lambda i: (i, 0))] + _token_specs(tm) + [
            _const_spec((D_MODEL, D_MODEL)),
        ] + _tail_specs(layer),
        out_specs=tuple(_token_specs(tm)),
        scratch_shapes=_tail_scratch(),
        compiler_params=pltpu.CompilerParams(**_TAIL_PARAMS),
        name="odd_out_mlp",
    )(x, att_p, att_s, w_o, mods, gn2, w1, w2)


def _qkv_kernel(x_ref, gn_ref, m_ref, w_ref, qn_ref, kn_ref, cos_ref, sin_ref,
                q_ref, kb_ref, vb_ref, k32_ref, v32_ref, *, tile):
    h = _norm_mod(x_ref[...], gn_ref[...], m_ref[:, 0:D_MODEL], m_ref[:, D_MODEL:2 * D_MODEL]).astype(BF16)
    z = _dot(h, w_ref[...])
    v = z[:, (N_HEADS + KV_HEADS) * HEAD_DIM:]
    vb_ref[...] = v.astype(BF16)
    even_lane = (lax.broadcasted_iota(jnp.int32, (1, HEAD_DIM), 1) & 1) == 0

    def heads(rope):
        for hd in range(N_HEADS + KV_HEADS):
            xh = z[:, hd * HEAD_DIM:(hd + 1) * HEAD_DIM]
            gain = qn_ref[...] if hd < N_HEADS else kn_ref[...]
            xh = xh * lax.rsqrt(jnp.mean(xh * xh, axis=-1, keepdims=True) + EPS) * gain
            if rope:
                partner = jnp.where(even_lane, pltpu.roll(xh, HEAD_DIM - 1, 1), pltpu.roll(xh, 1, 1))
                xh = xh * cos_ref[...] + partner * sin_ref[...]
            if hd < N_HEADS:
                q_ref[:, hd * HEAD_DIM:(hd + 1) * HEAD_DIM] = xh.astype(BF16)
            else:
                cols = slice((hd - N_HEADS) * HEAD_DIM, (hd - N_HEADS + 1) * HEAD_DIM)
                kb_ref[:, cols] = xh.astype(BF16)
                if not rope:
                    k32_ref[:, hd - N_HEADS, :] = xh

    is_sample = pl.program_id(0) >= T_PROMPT // tile

    @pl.when(is_sample)
    def _():
        heads(True)

    @pl.when(jnp.logical_not(is_sample))
    def _():
        heads(False)
        for kh in range(KV_HEADS):
            v32_ref[:, kh, :] = v[:, kh * HEAD_DIM:(kh + 1) * HEAD_DIM]


def _qkv_call(x, gn, mods, layer, w_qkv, q_norm, k_norm, cos_t, sin_t):
    tm = 512
    pos_tiles = DEC_SEQ // tm
    n_prompt = T_PROMPT // tm
    kvw = KV_HEADS * HEAD_DIM

    def pos_map(i):
        return (jnp.maximum(i - n_prompt, 0) % pos_tiles, 0)

    def prompt_map(i):
        return (jnp.minimum(i, n_prompt - 1), 0, 0)

    return pl.pallas_call(
        functools.partial(_qkv_kernel, tile=tm),
        out_shape=(jax.ShapeDtypeStruct((T_TOK, N_HEADS * HEAD_DIM), BF16),
                   jax.ShapeDtypeStruct((T_TOK, kvw), BF16),
                   jax.ShapeDtypeStruct((T_TOK, kvw), BF16),
                   jax.ShapeDtypeStruct((T_PROMPT, KV_HEADS, HEAD_DIM), F32),
                   jax.ShapeDtypeStruct((T_PROMPT, KV_HEADS, HEAD_DIM), F32)),
        grid=(T_TOK // tm,),
        in_specs=[
            pl.BlockSpec((tm, D_MODEL), lambda i: (i, 0)),
            pl.BlockSpec((1, D_MODEL), lambda i: (0, 0)),
            pl.BlockSpec((None, 1, 6 * D_MODEL), lambda i: (_cond_row(i, tm), 0, 0)),
            pl.BlockSpec(w_qkv.shape, lambda i: (0, 0)),
            pl.BlockSpec((1, HEAD_DIM), lambda i: (0, 0)),
            pl.BlockSpec((1, HEAD_DIM), lambda i: (0, 0)),
            pl.BlockSpec((tm, HEAD_DIM), pos_map),
            pl.BlockSpec((tm, HEAD_DIM), pos_map),
        ],
        out_specs=(pl.BlockSpec((tm, N_HEADS * HEAD_DIM), lambda i: (i, 0)),
                   pl.BlockSpec((tm, kvw), lambda i: (i, 0)),
                   pl.BlockSpec((tm, kvw), lambda i: (i, 0)),
                   pl.BlockSpec((tm, KV_HEADS, HEAD_DIM), prompt_map),
                   pl.BlockSpec((tm, KV_HEADS, HEAD_DIM), prompt_map)),
        compiler_params=pltpu.CompilerParams(vmem_limit_bytes=VMEM_LIMIT),
        name="odd_qkv",
    )(x, gn, mods, w_qkv, q_norm, k_norm, cos_t, sin_t)


def _rope_tables():
    f32 = np.float32
    rows = DEC_SEQ // GRID_W
    row = np.repeat(np.arange(rows, dtype=f32), GRID_W)
    col = np.tile(np.arange(GRID_W, dtype=f32), rows)
    inv = np.power(f32(ROPE_THETA), -np.arange(0, AXIS_DIM, 2, dtype=f32) / f32(AXIS_DIM)).astype(f32)
    ang = np.concatenate([row[:, None] * inv, col[:, None] * inv], axis=-1).astype(f32)
    cos_t = np.repeat(np.cos(ang), 2, axis=-1).astype(f32)
    sin = np.sin(ang).astype(f32)
    sin_t = np.stack([-sin, sin], axis=-1).reshape(DEC_SEQ, HEAD_DIM)
    return jnp.asarray(cos_t), jnp.asarray(sin_t)


def _attn_kernel(*refs, seq_len, has_cache):
    q_ref, k_ref, v_ref = refs[:3]
    ck_ref, cv_ref = refs[3:5] if has_cache else (None, None)
    o_ref = refs[-1]
    c = HEAD_DIM ** -0.5 * math.log2(math.e)
    ones_col = (lax.broadcasted_iota(jnp.int32, (1, HEAD_DIM), 1) == 0).astype(BF16)

    def with_ones(v):
        return jnp.concatenate([v, jnp.broadcast_to(ones_col, v.shape)], axis=1)

    if has_cache:
        ck = ck_ref[...].astype(BF16)
        cv = with_ones(cv_ref[...].astype(BF16))
    def one_sequence(rows):
        k = k_ref[rows, :]
        v = with_ones(v_ref[rows, :])
        for r in range(Q_PER_KV):
            cs = slice(r * HEAD_DIM, (r + 1) * HEAD_DIM)
            q = q_ref[rows, cs]
            s = lax.dot_general(q, k, NT_DIMS, preferred_element_type=F32)
            m = jnp.max(s, axis=-1, keepdims=True)
            if has_cache:
                sc = lax.dot_general(q, ck, NT_DIMS, preferred_element_type=F32)
                m = jnp.maximum(m, jnp.max(sc, axis=-1, keepdims=True))
            mc = m * c
            o = _dot(jnp.exp2(s * c - mc).astype(BF16), v)
            if has_cache:
                o = o + _dot(jnp.exp2(sc * c - mc).astype(BF16), cv)
            o_ref[rows, cs] = (o[:, 0:HEAD_DIM] / o[:, HEAD_DIM:HEAD_DIM + 1]).astype(BF16)

    nsub = q_ref.shape[0] // seq_len
    if nsub == 1:
        one_sequence(slice(0, seq_len))
    else:
        def body(j, carry):
            one_sequence(pl.ds(pl.multiple_of(j * seq_len, seq_len), seq_len))
            return carry

        lax.fori_loop(0, nsub, body, 0)


def _attn_call(q, k, v, cache_k, cache_v, seq_len, row0, nrows, nsub):
    blk = nsub * seq_len
    assert row0 % blk == 0 and nrows % blk == 0
    has_cache = cache_k is not None
    assert not has_cache or nsub == 1
    b0 = row0 // blk
    gw = Q_PER_KV * HEAD_DIM
    in_specs = [
        pl.BlockSpec((blk, gw), lambda b, g: (b0 + b, g)),
        pl.BlockSpec((blk, HEAD_DIM), lambda b, g: (b0 + b, g)),
        pl.BlockSpec((blk, HEAD_DIM), lambda b, g: (b0 + b, g)),
    ]
    args = [q, k, v]
    if has_cache:
        in_specs += [pl.BlockSpec((PAST_LEN, HEAD_DIM), lambda b, g: (b, g)),
                     pl.BlockSpec((PAST_LEN, HEAD_DIM), lambda b, g: (b, g))]
        args += [cache_k, cache_v]
    return pl.pallas_call(
        functools.partial(_attn_kernel, seq_len=seq_len, has_cache=has_cache),
        out_shape=jax.ShapeDtypeStruct((nrows, N_HEADS * HEAD_DIM), BF16),
        grid=(nrows // blk, KV_HEADS),
        in_specs=in_specs,
        out_specs=pl.BlockSpec((blk, gw), lambda b, g: (b, g)),
        compiler_params=pltpu.CompilerParams(vmem_limit_bytes=VMEM_LIMIT),
        name=f"attn_len{seq_len}",
    )(*args)


def kernel(x_prompt, x_sample, state_s5_re, state_s5_im, state_gla, cache_k, cache_v, c, c_ctx, norm_mix, norm_mlp, w_ada, b_ada, w_mlp_in, w_mlp_out, w_in_e, w_out_e, s5_lambda_re, s5_lambda_im, s5_log_dt, s5_b_re, s5_b_im, s5_c_re, s5_c_im, s5_d, s5_w_glu, s5_b_glu, gla_w_gate2, gla_b_gate, gla_norm, w_qkv_o, w_o_o, q_norm, k_norm):
    xp = x_prompt.reshape(T_PROMPT, D_MODEL)
    xs = x_sample.reshape(T_SAMPLE, D_MODEL)
    cond8 = jnp.concatenate([c_ctx[None, :], c, jnp.zeros((COND_ROWS - 1 - DEC_BATCH, D_MODEL), F32)], axis=0)
    b_ada3 = b_ada.reshape(DEPTH, 1, 6 * D_MODEL)
    w1_all, w2_all = w_mlp_in, w_mlp_out
    mats, mods0 = _s5_prep_call(s5_lambda_re[0], s5_lambda_im[0], s5_log_dt[0], s5_b_re[0], s5_b_im[0],
                                s5_c_re[0], s5_c_im[0], ada=(cond8, w_ada, b_ada3, 0))

    w_in = jnp.pad(w_in_e[0], ((0, 0), (0, LANES - 2 * GLA_RANK))).astype(BF16)
    zg = jnp.zeros((GLA_RANK, GLA_QK), F32)
    w_gate = jnp.concatenate([jnp.concatenate([gla_w_gate2[0, 0], zg], axis=1),
                              jnp.concatenate([zg, gla_w_gate2[0, 1]], axis=1),
                              jnp.zeros((LANES - 2 * GLA_RANK, 2 * GLA_QK), F32)], axis=0).astype(BF16)
    b_gate = gla_b_gate[0].reshape(1, 2 * GLA_QK)
    u, z, g = _inproj_call(xp, xs, norm_mix[0:1], mods0, 0, w_in, w_gate, b_gate)

    def state_rows(s):
        return jnp.transpose(s, (2, 0, 1, 3)).reshape(S5_GROUPS, DEC_BATCH, 2 * S5_STATE)

    h0 = jnp.concatenate([state_rows(state_s5_re[:, 0]), state_rows(state_s5_im[:, 0])], axis=-1)
    nsteps = S5_GROUPS // S5_GPB
    h0 = jnp.transpose(h0.reshape(nsteps, S5_GPB, DEC_BATCH, S5_W), (0, 2, 1, 3))
    y5, ns = _s5_call(u, mats, h0)
    ns = jnp.transpose(ns, (0, 2, 1, 3)).reshape(S5_GROUPS, BATCH, S5_W)

    def state_out(n):
        return jnp.transpose(n.reshape(S5_GROUPS, BATCH, 2, S5_STATE), (1, 2, 0, 3))[:, None]

    new_s5_re = state_out(ns[:, :, :2 * S5_STATE])
    new_s5_im = state_out(ns[:, :, 2 * S5_STATE:])

    gn_gla = gla_norm[0].reshape(1, GLA_DV)
    gla_p, sfin, w_glu, w_out, w_qkv, w_o = _gla_call(z, g, gn_gla, None, SEQ, BATCH, 0, nsub=4,
                                                      casts=(s5_w_glu, w_out_e, w_qkv_o, w_o_o))
    s0 = state_gla[:, 0].reshape(DEC_BATCH, 2, GLA_QK, GLA_DV)
    gla_s, _, mods1 = _gla_call(z, g, gn_gla, s0, DEC_SEQ, DEC_BATCH, T_PROMPT, nsub=1,
                                ada=(cond8, w_ada, b_ada3, 1))
    new_gla = sfin.reshape(BATCH, 1, 2, GLA_HEADS, GLA_DK, GLA_DV)

    x = _even_out_call(xp, xs, y5, u, s5_d[0].reshape(1, S5_WIDTH), w_glu,
                       s5_b_glu[0].reshape(1, S5_WIDTH), gla_p, gla_s, w_out, mods0, 0,
                       norm_mlp[0:1], w1_all, w2_all)

    cos_t, sin_t = _rope_tables()
    q, k, v, k32, v32 = _qkv_call(x, norm_mix[1:2], mods1, 1, w_qkv,
                                  q_norm[0].reshape(1, HEAD_DIM), k_norm[0].reshape(1, HEAD_DIM), cos_t, sin_t)
    att_p = _attn_call(q, k, v, None, None, SEQ, 0, T_PROMPT, nsub=4)
    ck = cache_k[:, 0].reshape(DEC_BATCH * PAST_LEN, KV_HEADS * HEAD_DIM)
    cv = cache_v[:, 0].reshape(DEC_BATCH * PAST_LEN, KV_HEADS * HEAD_DIM)
    att_s = _attn_call(q, k, v, ck, cv, DEC_SEQ, T_PROMPT, T_SAMPLE, nsub=1)
    yp, ys = _odd_out_call(x, att_p, att_s, w_o, mods1, 1, norm_mlp[1:2],
                           w1_all, w2_all)

    new_k = k32.reshape(BATCH, 1, SEQ, KV_HEADS, HEAD_DIM)
    new_v = v32.reshape(BATCH, 1, SEQ, KV_HEADS, HEAD_DIM)
    y_prompt = yp.reshape(BATCH, SEQ, D_MODEL)
    y_sample = ys.reshape(DEC_BATCH, DEC_SEQ, D_MODEL)
    return (y_prompt, y_sample, new_s5_re, new_s5_im, new_gla, new_k, new_v)
```

```python
import functools
import math

import jax
import jax.numpy as jnp
import numpy as np
from jax import lax
from jax.experimental import pallas as pl
from jax.experimental.pallas import tpu as pltpu

F32 = jnp.float32
BF16 = jnp.bfloat16

LANES = 128

D_MODEL = 1024
BATCH = 16
SEQ = 256
DEPTH = 2
DEC_BATCH = 4
DEC_SEQ = 1024
PAST_LEN = 512
GRID_W = 64
S5_WIDTH = 512
S5_GROUP_CH = 16
S5_GROUPS = 32
S5_STATE = 64
GLA_HEADS = 4
GLA_VW = 512
GLA_DV = 128
GLA_DK = 64
GLA_QK = 256
GLA_RANK = 16
GLA_TAU = 16.0
GLA_CHUNK = 64
GLA_CPB = 4
GLA_BLK = GLA_CPB * GLA_CHUNK
HEAD_DIM = 128
N_HEADS = 8
KV_HEADS = 2
Q_PER_KV = N_HEADS // KV_HEADS
AXIS_DIM = 64
ROPE_THETA = 10000.0
D_FF = 4096
EPS = 1e-6

T_PROMPT = BATCH * SEQ
T_SAMPLE = DEC_BATCH * DEC_SEQ
T_TOK = T_PROMPT + T_SAMPLE
COND_ROWS = 8
COND_SPAN = 1024
PROMPT_SPANS = T_PROMPT // COND_SPAN

S5_Q = 16
S5_W = S5_Q * S5_GROUP_CH
S5_GPB = LANES // S5_GROUP_CH
S5_UBLK = S5_WIDTH // LANES
S5_ROWS = T_TOK // S5_Q
S5_PROMPT_ROWS = T_PROMPT // S5_Q
S5_PROMPT_CHUNKS = SEQ // S5_Q
S5_SAMPLE_CHUNKS = DEC_SEQ // S5_Q

VMEM_LIMIT = 56 * 1024 * 1024
VMEM_LIMIT_TAIL = 60 * 1024 * 1024

NT_DIMS = (((1,), (1,)), ((), ()))


def _cond_row(i, tile):
    return jnp.maximum((i * tile) // COND_SPAN - (PROMPT_SPANS - 1), 0)


def _norm_mod(x, gain, shift, scale):
    y = x * lax.rsqrt(jnp.mean(x * x, axis=-1, keepdims=True) + EPS)
    return (y * gain) * (1.0 + scale) + shift


def _dot(a, b):
    return jnp.dot(a, b, preferred_element_type=F32)


def _ada_kernel(cond_ref, w_ref, b_ref, o_ref):
    s = jax.nn.silu(cond_ref[...]).astype(BF16)
    o_ref[:, 0, :] = _dot(s, w_ref[...].astype(BF16)) + b_ref[...]


ADA_SHAPE = jax.ShapeDtypeStruct((COND_ROWS, 1, 6 * D_MODEL), F32)


def _ada_specs(layer, tn):
    return ([pl.BlockSpec((COND_ROWS, D_MODEL), lambda *ids: (0, 0)),
             pl.BlockSpec((None, D_MODEL, tn), lambda *ids: (layer, 0, ids[-1])),
             pl.BlockSpec((None, 1, tn), lambda *ids: (layer, 0, ids[-1]))],
            pl.BlockSpec((COND_ROWS, 1, tn), lambda *ids: (0, 0, ids[-1])))


def _token_specs(tile, width=D_MODEL):
    n_prompt = T_PROMPT // tile
    return [pl.BlockSpec((tile, width), lambda i: (jnp.minimum(i, n_prompt - 1), 0)),
            pl.BlockSpec((tile, width), lambda i: (jnp.maximum(i - n_prompt, 0), 0))]


def _token_tile(xp_ref, xs_ref, tile):
    return jnp.where(pl.program_id(0) < T_PROMPT // tile, xp_ref[...], xs_ref[...])


def _inproj_kernel(xp_ref, xs_ref, gn_ref, m_ref, w_ref, wglr_ref, wg_ref, bg_ref, u_ref, z_ref, g_ref, *, tile):
    x = _token_tile(xp_ref, xs_ref, tile)
    h = _norm_mod(x, gn_ref[...], m_ref[:, 0:D_MODEL], m_ref[:, D_MODEL:2 * D_MODEL]).astype(BF16)
    z = _dot(h, w_ref[...])
    for blk in range(S5_UBLK):
        u_ref[blk] = z[:, blk * LANES:(blk + 1) * LANES]
    z_ref[...] = z[:, S5_WIDTH:]
    glr = _dot(h, wglr_ref[...]).astype(BF16)
    pre = _dot(glr, wg_ref[...]) + bg_ref[...]
    g_ref[...] = jax.nn.log_sigmoid(pre) * (1.0 / GLA_TAU)


def _inproj_call(xp, xs, gn, mods, layer, w_in, w_gate, b_gate):
    tm = 512
    nz = w_in.shape[1] - LANES
    return pl.pallas_call(
        functools.partial(_inproj_kernel, tile=tm),
        out_shape=(jax.ShapeDtypeStruct((S5_UBLK, T_TOK, LANES), F32),
                   jax.ShapeDtypeStruct((T_TOK, nz - S5_WIDTH), F32),
                   jax.ShapeDtypeStruct((T_TOK, 2 * GLA_QK), F32)),
        grid=(T_TOK // tm,),
        in_specs=_token_specs(tm) + [
            pl.BlockSpec((1, D_MODEL), lambda i: (0, 0)),
            pl.BlockSpec((None, 1, 6 * D_MODEL), lambda i: (_cond_row(i, tm), 0, 0)),
            pl.BlockSpec((D_MODEL, nz), lambda i: (0, 0)),
            pl.BlockSpec((D_MODEL, LANES), lambda i: (0, nz // LANES)),
            pl.BlockSpec((LANES, 2 * GLA_QK), lambda i: (0, 0)),
            pl.BlockSpec((1, 2 * GLA_QK), lambda i: (0, 0)),
        ],
        out_specs=(pl.BlockSpec((S5_UBLK, tm, LANES), lambda i: (0, i, 0)),
                   pl.BlockSpec((tm, nz - S5_WIDTH), lambda i: (i, 0)),
                   pl.BlockSpec((tm, 2 * GLA_QK), lambda i: (i, 0))),
        compiler_params=pltpu.CompilerParams(vmem_limit_bytes=VMEM_LIMIT),
        name="even_inproj",
    )(xp, xs, gn, mods, w_in, w_in, w_gate, b_gate)


S5_PREP_GPB = 8
_PREP_LRE, _PREP_LIM, _PREP_LDT = 0, 1, 2
_PREP_BT_RE, _PREP_BT_IM, _PREP_C_RE, _PREP_C_IM, _PREP_ROWS = 8, 24, 40, 56, 72


def _s5_prep_kernel(p_ref, cc_ref, cond_ref, wada_ref, bada_ref, t_ref, bq_ref, cqt_ref, be_ref, a_ref, mods_ref,
                    t_scr, dd_scr):
    _ada_kernel(cond_ref, wada_ref, bada_ref, mods_ref)
    for gi in range(S5_PREP_GPB):
        _s5_prep_group(p_ref.at[gi], cc_ref.at[gi], t_ref.at[gi], bq_ref.at[gi], cqt_ref.at[gi], be_ref.at[gi],
                       a_ref.at[gi], t_scr, dd_scr)


def _s5_prep_group(p_ref, cc_ref, t_ref, bq_ref, cqt_ref, be_ref, a_ref, t_scr, dd_scr):
    gch = S5_GROUP_CH
    lre = p_ref[_PREP_LRE:_PREP_LRE + 1]
    lim = p_ref[_PREP_LIM:_PREP_LIM + 1]
    dt = jnp.exp(p_ref[_PREP_LDT:_PREP_LDT + 1])
    a = lre * dt
    th = lim * dt

    def lam_pow(k):
        mag = jnp.exp(k * a)
        return mag * jnp.cos(k * th), mag * jnp.sin(k * th)

    lb_re, lb_im = lam_pow(1.0)
    nr = lb_re - 1.0
    den = lre * lre + lim * lim
    cf_re = (nr * lre + lb_im * lim) / den
    cf_im = (lb_im * lre - nr * lim) / den
    bt_re = p_ref[_PREP_BT_RE:_PREP_BT_RE + gch]
    bt_im = p_ref[_PREP_BT_IM:_PREP_BT_IM + gch]
    bb_re = jnp.tile(cf_re * bt_re - cf_im * bt_im, (S5_Q, 1))
    bb_im = jnp.tile(cf_re * bt_im + cf_im * bt_re, (S5_Q, 1))

    shape = (S5_W, LANES)
    pos = lax.shift_right_logical(lax.broadcasted_iota(jnp.int32, shape, 0), 4)
    is_f = lax.broadcasted_iota(jnp.int32, shape, 1) < S5_STATE
    posq = lax.broadcasted_iota(jnp.int32, (S5_Q, LANES), 0).astype(F32)
    is_fq = lax.broadcasted_iota(jnp.int32, (S5_Q, LANES), 1) < S5_STATE

    def per_channel(tbl):
        return jnp.broadcast_to(tbl[:, None, :], (S5_Q, S5_GROUP_CH, LANES)).reshape(shape)

    p_re, p_im = map(per_channel, lam_pow(jnp.where(is_fq, (S5_Q - 1.0) - posq, posq)))
    w_re = p_re * bb_re - p_im * bb_im
    w_im = p_re * bb_im + p_im * bb_re
    bq = jnp.concatenate([w_re, w_im], axis=1)
    bqt = jnp.transpose(bq)
    bq_ref[...] = bqt.astype(BF16)

    edge = pos == jnp.where(is_f, 0, S5_Q - 1)
    be = jnp.concatenate([jnp.where(edge, bb_re, 0.0), jnp.where(edge, bb_im, 0.0)], axis=1)
    be_ref[...] = jnp.transpose(be).astype(BF16)

    q_re, q_im = map(per_channel, lam_pow(jnp.where(is_fq, posq + 1.0, S5_Q - posq)))
    ct_re = jnp.tile(p_ref[_PREP_C_RE:_PREP_C_RE + gch], (S5_Q, 1))
    ct_im = jnp.tile(p_ref[_PREP_C_IM:_PREP_C_IM + gch], (S5_Q, 1))
    g_re = q_re * ct_re - q_im * ct_im
    g_im = q_re * ct_im + q_im * ct_re
    cqt_ref[...] = jnp.concatenate([g_re, -g_im], axis=1).astype(BF16)

    a_re, a_im = lam_pow(float(S5_Q))
    a_ref[...] = jnp.concatenate([a_re, a_im], axis=1)

    kf = jnp.dot(cc_ref[0:gch], bqt, precision=lax.Precision.HIGHEST, preferred_element_type=F32)
    kb = jnp.dot(cc_ref[gch:2 * gch], bqt, precision=lax.Precision.HIGHEST, preferred_element_type=F32)
    lo = S5_W - gch
    dd_scr[:, 0:S5_W] = kf
    dd_scr[:, lo:lo + S5_W] = kb
    dd_scr[:, lo:S5_W] = kf[:, lo:S5_W] + kb[:, 0:gch]
    for t in range(S5_Q):
        c0 = (S5_Q - 1 - t) * gch
        t_scr[t * gch:(t + 1) * gch, :] = dd_scr[:, c0:c0 + S5_W]
    t_ref[...] = t_scr[...].astype(BF16)


def _s5_prep_call(lam_re, lam_im, log_dt, b_re, b_im, c_re, c_im, ada):
    per_dir = jnp.stack([lam_re, lam_im, jnp.broadcast_to(log_dt[:, :, None], lam_re.shape)])
    head = jnp.transpose(per_dir, (2, 0, 1, 3)).reshape(S5_GROUPS, 3, 2 * S5_STATE)
    pad = jnp.zeros((S5_GROUPS, _PREP_BT_RE - _PREP_LDT - 1, LANES), F32)
    shared = jnp.concatenate([jnp.transpose(b_re, (0, 2, 1)), jnp.transpose(b_im, (0, 2, 1)), c_re, c_im],
                             axis=1)
    packed = jnp.concatenate([head, pad, jnp.concatenate([shared, shared], axis=-1)], axis=1)
    zero = jnp.zeros_like(c_re)
    cc = jnp.concatenate([jnp.concatenate([c_re, zero, -c_im, zero], axis=-1),
                          jnp.concatenate([zero, c_re, zero, -c_im], axis=-1)], axis=1)

    gpb = S5_PREP_GPB
    nsteps = S5_GROUPS // gpb
    sq = pl.BlockSpec((gpb, S5_W, S5_W), lambda g: (g, 0, 0))
    sq_shape = jax.ShapeDtypeStruct((S5_GROUPS, S5_W, S5_W), BF16)
    cond8, w_ada, b_ada, ada_layer = ada
    ada_in, ada_out = _ada_specs(ada_layer, 6 * D_MODEL // nsteps)
    *mats, mods = pl.pallas_call(
        _s5_prep_kernel,
        out_shape=(sq_shape, sq_shape, sq_shape, sq_shape,
                   jax.ShapeDtypeStruct((S5_GROUPS, 1, S5_W), F32), ADA_SHAPE),
        grid=(nsteps,),
        in_specs=[pl.BlockSpec((gpb, _PREP_ROWS, LANES), lambda g: (g, 0, 0)),
                  pl.BlockSpec((gpb, 2 * S5_GROUP_CH, S5_W), lambda g: (g, 0, 0))] + ada_in,
        out_specs=(sq, sq, sq, sq, pl.BlockSpec((gpb, 1, S5_W), lambda g: (g, 0, 0)), ada_out),
        scratch_shapes=[pltpu.VMEM((S5_W, S5_W), F32), pltpu.VMEM((S5_GROUP_CH, 2 * S5_W), F32)],
        compiler_params=pltpu.CompilerParams(vmem_limit_bytes=VMEM_LIMIT),
        name="s5_prep",
    )(packed, cc, cond8, w_ada, b_ada)
    return mats, mods


def _s5_kernel(u_ref, tt_ref, bqt_ref, cqt_ref, bet_ref, a_ref, h0_ref, y_ref, ns_ref,
               ut_scr, x_scr, spf_scr, spb_scr, ne_scr, yt_scr, xt_scr):
    gch = S5_GROUP_CH
    for s in range(S5_Q):
        rows = u_ref[pl.ds(s, S5_ROWS, stride=S5_Q), :]
        rows_t = jnp.transpose(rows).astype(BF16)
        for gl in range(S5_GPB):
            ut_scr[gl, s * gch:(s + 1) * gch, :] = rows_t[gl * gch:(gl + 1) * gch, :]

    for gl in range(S5_GPB):
        ut = ut_scr[gl]
        xt_scr[...] = _dot(bqt_ref[gl], ut)
        x = jnp.transpose(xt_scr[...])
        xt_scr[:, 0:S5_PROMPT_ROWS] = _dot(bet_ref[gl], ut[:, 0:S5_PROMPT_ROWS])
        ne = jnp.transpose(xt_scr[:, 0:S5_PROMPT_ROWS])
        for part in range(2):
            x_scr[part, pl.ds(gl, S5_ROWS, stride=S5_GPB), :] = x[:, part * LANES:(part + 1) * LANES]
            ne_scr[part, pl.ds(gl, S5_PROMPT_ROWS, stride=S5_GPB), :] = ne[:, part * LANES:(part + 1) * LANES]

    is_f = lax.broadcasted_iota(jnp.int32, (1, LANES), 1) < S5_STATE
    a_re = a_ref[:, 0:LANES]
    a_im = a_ref[:, LANES:2 * LANES]

    def tile(row):
        return pl.ds(pl.multiple_of(row * S5_GPB, S5_GPB), S5_GPB)

    def scan(base, nseq, nchunk, s_init):
        def body(i, state):
            new = []
            for b in range(nseq):
                s_re, s_im = state[b]
                rows_f = tile(base + b * nchunk + i)
                rows_b = tile(base + b * nchunk + (nchunk - 1 - i))
                spf_scr[0, rows_f, :] = s_re
                spf_scr[1, rows_f, :] = s_im
                spb_scr[0, rows_b, :] = s_re
                spb_scr[1, rows_b, :] = s_im
                x_re = jnp.where(is_f, x_scr[0, rows_f, :], x_scr[0, rows_b, :])
                x_im = jnp.where(is_f, x_scr[1, rows_f, :], x_scr[1, rows_b, :])
                new.append((a_re * s_re - a_im * s_im + x_re, a_re * s_im + a_im * s_re + x_im))
            return tuple(new)

        lax.fori_loop(0, nchunk, body, tuple(s_init))

    zero = jnp.zeros((S5_GPB, LANES), F32)
    scan(0, BATCH, S5_PROMPT_CHUNKS, [(zero, zero)] * BATCH)
    scan(S5_PROMPT_ROWS, DEC_BATCH, S5_SAMPLE_CHUNKS,
         [(h0_ref[b, :, 0:LANES], h0_ref[b, :, LANES:2 * LANES]) for b in range(DEC_BATCH)])

    for b in range(BATCH):
        first = pl.ds(b * S5_PROMPT_CHUNKS * S5_GPB, S5_GPB)
        last = pl.ds(((b + 1) * S5_PROMPT_CHUNKS - 1) * S5_GPB, S5_GPB)
        for part in range(2):
            ns_ref[b, :, part * LANES:(part + 1) * LANES] = jnp.where(is_f, ne_scr[part, first, :], ne_scr[part, last, :])

    for gl in range(S5_GPB):
        rows = pl.ds(gl, S5_ROWS, stride=S5_GPB)
        carried = jnp.concatenate([jnp.where(is_f, spf_scr[p, rows, :], spb_scr[p, rows, :]) for p in range(2)],
                                  axis=1).astype(BF16)
        yt = _dot(tt_ref[gl], ut_scr[gl]) + lax.dot_general(cqt_ref[gl], carried, NT_DIMS,
                                                            preferred_element_type=F32)
        for t in range(S5_Q):
            yt_scr[t, gl * gch:(gl + 1) * gch, :] = yt[t * gch:(t + 1) * gch, :]
    for t in range(S5_Q):
        y_ref[pl.ds(t, S5_ROWS, stride=S5_Q), :] = jnp.transpose(yt_scr[t])


def _s5_call(u, mats, h0):
    tt_m, bqt_m, cqt_m, bet_m, a_m = mats
    nsteps = S5_GROUPS // S5_GPB
    sq = pl.BlockSpec((S5_GPB, S5_W, S5_W), lambda g: (g, 0, 0))
    state_scr = pltpu.VMEM((2, S5_ROWS * S5_GPB, LANES), F32)
    return pl.pallas_call(
        _s5_kernel,
        out_shape=(jax.ShapeDtypeStruct((nsteps, T_TOK, LANES), F32),
                   jax.ShapeDtypeStruct((nsteps, BATCH, S5_GPB, S5_W), F32)),
        grid=(nsteps,),
        in_specs=[
            pl.BlockSpec((None, T_TOK, LANES), lambda g: (g, 0, 0)),
            sq, sq, sq, sq,
            pl.BlockSpec((S5_GPB, S5_W), lambda g: (g, 0)),
            pl.BlockSpec((None, DEC_BATCH, S5_GPB, S5_W), lambda g: (g, 0, 0, 0)),
        ],
        out_specs=(pl.BlockSpec((None, T_TOK, LANES), lambda g: (g, 0, 0)),
                   pl.BlockSpec((None, BATCH, S5_GPB, S5_W), lambda g: (g, 0, 0, 0))),
        scratch_shapes=[pltpu.VMEM((S5_GPB, S5_W, S5_ROWS), BF16), state_scr, state_scr, state_scr,
                        pltpu.VMEM((2, S5_PROMPT_ROWS * S5_GPB, LANES), F32),
                        pltpu.VMEM((S5_Q, LANES, S5_ROWS), F32), pltpu.VMEM((S5_W, S5_ROWS), F32)],
        compiler_params=pltpu.CompilerParams(vmem_limit_bytes=VMEM_LIMIT),
        name="s5_scan",
    )(u, tt_m, bqt_m, cqt_m, bet_m, a_m.reshape(S5_GROUPS, S5_W), h0)


def _split_bf16(x):
    hi = x.astype(BF16)
    r1 = x - hi.astype(F32)
    mid = r1.astype(BF16)
    lo = (r1 - mid.astype(F32)).astype(BF16)
    return hi, mid, lo


def _cast_specs(shape, nsteps):
    _, rows, cols = shape
    rb = rows // nsteps
    return (pl.BlockSpec((None, rb, cols), lambda *ids: (0, ids[-1], 0)),
            pl.BlockSpec((rb, cols), lambda *ids: (ids[-1], 0)))


def _gla_kernel(*refs, seq_len, nsub, has_s0, has_ada, n_cast):
    rows_refs, gn_ref = refs[:6], refs[6]
    n_ada_in = 7 + has_s0
    n_cast_in = n_ada_in + 3 * has_ada
    n_in = n_cast_in + n_cast
    s0_ref = refs[7] if has_s0 else None
    o_ref, sfin_ref = refs[n_in:n_in + 2]
    n_out = 2 + has_ada + n_cast
    scratch = refs[n_in + n_out:]
    if has_ada:
        _ada_kernel(*refs[n_ada_in:n_cast_in], refs[n_in + 2])
    for src, dst in zip(refs[n_cast_in:n_in], refs[n_in + 2 + has_ada:n_in + n_out]):
        dst[...] = src[...].astype(BF16)
    for j in range(nsub):
        rows = pl.ds(j * seq_len, seq_len)
        _gla_sequence(*[r.at[rows, :] for r in rows_refs], gn_ref, s0_ref.at[j] if has_s0 else None,
                      o_ref.at[rows, :], sfin_ref.at[j], *[s.at[j] for s in scratch], seq_len=seq_len)


def _gla_sequence(q_ref, k_ref, v_ref, gf_ref, gb_ref, r_ref, gn_ref, s0_ref, o_ref, sfin_ref,
                  oi_scr, qd_scr, kv_scr, dec_scr, ss_scr, *, seq_len):
    has_s0 = s0_ref is not None
    nblk = seq_len // GLA_BLK
    nchunk = seq_len // GLA_CHUNK
    cl = GLA_CHUNK
    ti = lax.broadcasted_iota(jnp.int32, (GLA_BLK, GLA_BLK), 0)
    si = lax.broadcasted_iota(jnp.int32, (GLA_BLK, GLA_BLK), 1)
    same = lax.shift_right_logical(ti, 6) == lax.shift_right_logical(si, 6)
    keep = (same & (ti >= si), same & (ti <= si))
    tri = tuple(kp.astype(BF16) for kp in keep)
    lane_head = lax.shift_right_logical(lax.broadcasted_iota(jnp.int32, (cl, GLA_QK), 1), 6)
    zeros_v = jnp.zeros((cl, GLA_DV), BF16)
    heads = [(slice(h * GLA_DK, (h + 1) * GLA_DK), slice(h * GLA_DV, (h + 1) * GLA_DV)) for h in range(GLA_HEADS)]

    for j in range(nblk):
        rows = slice(j * GLA_BLK, (j + 1) * GLA_BLK)
        q = q_ref[rows, :] * (GLA_DK ** -0.5)
        k = k_ref[rows, :]
        v = v_ref[rows, :].astype(BF16)
        qd, kd, k2t = [], [], []
        for d, g_ref in enumerate((gf_ref, gb_ref)):
            b = sum(_dot(tri[d], part) for part in _split_bf16(g_ref[rows, :]))
            last = cl - 1 if d == 0 else 0
            b_last = [b[c * cl + last:c * cl + last + 1] for c in range(GLA_CPB)]
            bl = jnp.concatenate([jnp.broadcast_to(x, (cl, GLA_QK)) for x in b_last], axis=0)
            qd_d = (q * jnp.exp(b)).astype(BF16)
            qd_scr[d, rows, :] = qd_d
            qd.append(qd_d)
            kd.append((k * jnp.exp(-b)).astype(BF16))
            k2t.append(jnp.transpose(k * jnp.exp(bl - b)).astype(BF16))
            for c in range(GLA_CPB):
                dec_scr[d, j * GLA_CPB + c] = jnp.exp(jnp.transpose(jnp.broadcast_to(b_last[c], (GLA_DV, GLA_QK))))
        for h, (ks, vs) in enumerate(heads):
            att = [jnp.where(keep[d], lax.dot_general(qd[d][:, ks], kd[d][:, ks], NT_DIMS,
                                                      preferred_element_type=F32), 0.0) for d in range(2)]
            oi_scr[rows, vs] = _dot((att[0] + att[1]).astype(BF16), v[:, vs])
            vh = v[:, vs]
            vexp = jnp.concatenate(
                [jnp.concatenate([vh[c * cl:(c + 1) * cl] if c2 == c else zeros_v for c2 in range(GLA_CPB)], axis=1)
                 for c in range(GLA_CPB)], axis=0)
            for d in range(2):
                kv_scr[d, j, h] = _dot(k2t[d][ks, :], vexp)

    for d in range(2):
        s = s0_ref[d] if has_s0 else jnp.zeros((GLA_QK, GLA_DV), F32)
        for cg in (range(nchunk) if d == 0 else range(nchunk - 1, -1, -1)):
            j, c = divmod(cg, GLA_CPB)
            ss_scr[d, cg] = s.astype(BF16)
            kv = jnp.concatenate([kv_scr[d, j, h, :, c * GLA_DV:(c + 1) * GLA_DV] for h in range(GLA_HEADS)], axis=0)
            s = s * dec_scr[d, cg] + kv
        sfin_ref[d] = s

    for cg in range(nchunk):
        rows = slice(cg * cl, (cg + 1) * cl)
        inter = []
        for d in range(2):
            qc = qd_scr[d, rows, :]
            qstack = jnp.concatenate([jnp.where(lane_head == h, qc, jnp.zeros_like(qc)) for h in range(GLA_HEADS)],
                                     axis=0)
            inter.append(_dot(qstack, ss_scr[d, cg]))
        gate = jax.nn.silu(r_ref[rows, :])
        for h, (ks, vs) in enumerate(heads):
            hr = slice(h * cl, (h + 1) * cl)
            oh = oi_scr[rows, vs] + inter[0][hr] + inter[1][hr]
            oh = oh * lax.rsqrt(jnp.mean(oh * oh, axis=-1, keepdims=True) + EPS) * gn_ref[...]
            o_ref[rows, vs] = oh * gate[:, vs]


def _gla_call(z, g, gla_norm, s0, seq_len, nseq, row0, nsub, ada=None, casts=()):
    blk = nsub * seq_len
    assert row0 % blk == 0 and nseq % nsub == 0
    r0 = row0 // blk
    has_s0 = s0 is not None
    qk_off = 0
    v_off = 2 * GLA_QK // GLA_VW
    in_specs = [
        pl.BlockSpec((blk, GLA_QK), lambda i: (r0 + i, qk_off)),
        pl.BlockSpec((blk, GLA_QK), lambda i: (r0 + i, qk_off + 1)),
        pl.BlockSpec((blk, GLA_VW), lambda i: (r0 + i, v_off)),
        pl.BlockSpec((blk, GLA_QK), lambda i: (r0 + i, 0)),
        pl.BlockSpec((blk, GLA_QK), lambda i: (r0 + i, 1)),
        pl.BlockSpec((blk, GLA_VW), lambda i: (r0 + i, v_off + 1)),
        pl.BlockSpec((1, GLA_DV), lambda i: (0, 0)),
    ]
    args = [z, z, z, g, g, z, gla_norm]
    state_spec = pl.BlockSpec((nsub, 2, GLA_QK, GLA_DV), lambda i: (i, 0, 0, 0))
    if has_s0:
        in_specs.append(state_spec)
        args.append(s0)
    out_shape = [jax.ShapeDtypeStruct((nseq * seq_len, GLA_VW), F32),
                 jax.ShapeDtypeStruct((nseq, 2, GLA_QK, GLA_DV), F32)]
    out_specs = [pl.BlockSpec((blk, GLA_VW), lambda i: (i, 0)), state_spec]
    nsteps = nseq // nsub
    if ada is not None:
        cond8, w_ada, b_ada, ada_layer = ada
        ada_in, ada_out = _ada_specs(ada_layer, 6 * D_MODEL // nsteps)
        in_specs += ada_in
        args += [cond8, w_ada, b_ada]
        out_shape.append(ADA_SHAPE)
        out_specs.append(ada_out)
    for w in casts:
        cast_in, cast_out = _cast_specs(w.shape, nsteps)
        in_specs.append(cast_in)
        args.append(w)
        out_shape.append(jax.ShapeDtypeStruct(w.shape[1:], BF16))
        out_specs.append(cast_out)
    return pl.pallas_call(
        functools.partial(_gla_kernel, seq_len=seq_len, nsub=nsub, has_s0=has_s0, has_ada=ada is not None,
                          n_cast=len(casts)),
        out_shape=tuple(out_shape),
        grid=(nsteps,),
        in_specs=in_specs,
        out_specs=tuple(out_specs),
        scratch_shapes=[
            pltpu.VMEM((nsub, seq_len, GLA_VW), F32),
            pltpu.VMEM((nsub, 2, seq_len, GLA_QK), BF16),
            pltpu.VMEM((nsub, 2, seq_len // GLA_BLK, GLA_HEADS, GLA_DK, GLA_CPB * GLA_DV), F32),
            pltpu.VMEM((nsub, 2, seq_len // GLA_CHUNK, GLA_QK, GLA_DV), F32),
            pltpu.VMEM((nsub, 2, seq_len // GLA_CHUNK, GLA_QK, GLA_DV), BF16),
        ],
        compiler_params=pltpu.CompilerParams(vmem_limit_bytes=VMEM_LIMIT),
        name=f"gla_len{seq_len}",
    )(*args)


MLP_CHUNK = 512
MLP_LOAD = 256
MLP_SLOTS = 4


class _MlpWeights:
    def __init__(self, w1_hbm, w2_hbm, w1_scr, w2_scr, stage1, stage2, sem, layer):
        self.refs = (w1_hbm, w2_hbm, w1_scr, w2_scr, stage1, stage2, sem)
        self.layer = layer

    def _copies(self, p):
        w1_hbm, w2_hbm, _, _, stage1, stage2, sem = self.refs
        cols = pl.ds(p * MLP_LOAD, MLP_LOAD)
        slot = p % MLP_SLOTS
        return (pltpu.make_async_copy(w1_hbm.at[self.layer, :, cols], stage1.at[slot], sem.at[0, slot]),
                pltpu.make_async_copy(w2_hbm.at[self.layer, cols, :], stage2.at[slot], sem.at[1, slot]))

    def start(self, p):
        for cp in self._copies(p):
            cp.start()

    def prefetch(self):
        for p in range(MLP_SLOTS - 1):
            self.start(p)

    def finish(self, p):
        _, _, w1_scr, w2_scr, stage1, stage2, _ = self.refs
        ahead = p + MLP_SLOTS - 1
        if ahead < D_FF // MLP_LOAD:
            self.start(ahead)
        for cp in self._copies(p):
            cp.wait()
        cols = slice(p * MLP_LOAD, (p + 1) * MLP_LOAD)
        w1_scr[:, cols] = stage1[p % MLP_SLOTS].astype(BF16)
        w2_scr[cols, :] = stage2[p % MLP_SLOTS].astype(BF16)


def _mlp_tail(x, mix, m_ref, gn2_ref, w1_ref, w2_ref, loading=None):
    y1 = x + m_ref[:, 2 * D_MODEL:3 * D_MODEL] * mix
    h = _norm_mod(y1, gn2_ref[...], m_ref[:, 3 * D_MODEL:4 * D_MODEL], m_ref[:, 4 * D_MODEL:5 * D_MODEL]).astype(BF16)
    nchunk = D_FF // MLP_CHUNK
    acc = jnp.zeros(y1.shape, F32)
    for c in range(nchunk):
        cols = slice(c * MLP_CHUNK, (c + 1) * MLP_CHUNK)
        if loading is not None:
            per = MLP_CHUNK // MLP_LOAD
            for p in range(c * per, (c + 1) * per):
                loading.finish(p)
        a = _dot(h, w1_ref[:, cols])
        a = jnp.square(jnp.maximum(a, 0.0)).astype(BF16)
        acc = acc + _dot(a, w2_ref[cols, :])
    return y1 + m_ref[:, 5 * D_MODEL:6 * D_MODEL] * acc


def _run_tail(x_and_mix, m_ref, gn2_ref, weights, w1_ref, w2_ref, emit):
    first = pl.program_id(0) == 0

    @pl.when(first)
    def _():
        weights.prefetch()
        x, mix = x_and_mix()
        emit(_mlp_tail(x, mix, m_ref, gn2_ref, w1_ref, w2_ref, loading=weights))

    @pl.when(jnp.logical_not(first))
    def _():
        x, mix = x_and_mix()
        emit(_mlp_tail(x, mix, m_ref, gn2_ref, w1_ref, w2_ref))


def _even_out_kernel(xp_ref, xs_ref, y5_ref, u_ref, dskip_ref, wglu_ref, bglu_ref, glap_ref, glas_ref, wout_ref,
                     m_ref, gn2_ref, w1_hbm, w2_hbm, o_ref, w1_ref, w2_ref, stage1, stage2, sem, *, layer):
    weights = _MlpWeights(w1_hbm, w2_hbm, w1_ref, w2_ref, stage1, stage2, sem, layer)

    def x_and_mix():
        nblk = S5_UBLK
        ys = (jnp.concatenate([y5_ref[b] for b in range(nblk)], axis=1)
              + jnp.concatenate([u_ref[b] for b in range(nblk)], axis=1) * dskip_ref[...])
        gl = jax.nn.gelu(ys)
        s5o = gl * jax.nn.sigmoid(_dot(gl.astype(BF16), wglu_ref[...]) + bglu_ref[...])
        gla = _token_tile(glap_ref, glas_ref, _OUT_TM).astype(BF16)
        mix = _dot(s5o.astype(BF16), wout_ref[0:S5_WIDTH, :]) + _dot(gla, wout_ref[S5_WIDTH:, :])
        return _token_tile(xp_ref, xs_ref, _OUT_TM), mix

    def emit(y):
        o_ref[...] = y

    _run_tail(x_and_mix, m_ref, gn2_ref, weights, w1_ref, w2_ref, emit)


def _odd_out_kernel(x_ref, attp_ref, atts_ref, wo_ref, m_ref, gn2_ref, w1_hbm, w2_hbm, op_ref, os_ref,
                    w1_ref, w2_ref, stage1, stage2, sem, *, layer):
    weights = _MlpWeights(w1_hbm, w2_hbm, w1_ref, w2_ref, stage1, stage2, sem, layer)
    is_prompt = pl.program_id(0) < T_PROMPT // _OUT_TM

    def x_and_mix():
        return x_ref[...], _dot(_token_tile(attp_ref, atts_ref, _OUT_TM), wo_ref[...])

    def emit(y):
        @pl.when(is_prompt)
        def _():
            op_ref[...] = y

        @pl.when(jnp.logical_not(is_prompt))
        def _():
            os_ref[...] = y

    _run_tail(x_and_mix, m_ref, gn2_ref, weights, w1_ref, w2_ref, emit)


_OUT_TM = 512


def _const_spec(shape):
    return pl.BlockSpec(shape, lambda i: (0,) * len(shape), pipeline_mode=pl.Buffered(1))


def _tail_specs(layer):
    tm = _OUT_TM
    return [
        pl.BlockSpec((None, 1, 6 * D_MODEL), lambda i: (_cond_row(i, tm), 0, 0)),
        _const_spec((1, D_MODEL)),
        pl.BlockSpec(memory_space=pl.ANY),
        pl.BlockSpec(memory_space=pl.ANY),
    ]


def _tail_scratch():
    return [pltpu.VMEM((D_MODEL, D_FF), BF16), pltpu.VMEM((D_FF, D_MODEL), BF16),
            pltpu.VMEM((MLP_SLOTS, D_MODEL, MLP_LOAD), F32), pltpu.VMEM((MLP_SLOTS, MLP_LOAD, D_MODEL), F32),
            pltpu.SemaphoreType.DMA((2, MLP_SLOTS))]


_TAIL_PARAMS = dict(dimension_semantics=("arbitrary",), vmem_limit_bytes=VMEM_LIMIT_TAIL)


def _even_out_call(xp, xs, y5, u, d_skip, w_glu, b_glu, gla_p, gla_s, w_out, mods, layer, gn2, w1, w2):
    tm = _OUT_TM
    return pl.pallas_call(
        functools.partial(_even_out_kernel, layer=layer),
        out_shape=jax.ShapeDtypeStruct((T_TOK, D_MODEL), F32),
        grid=(T_TOK // tm,),
        in_specs=_token_specs(tm) + [
            pl.BlockSpec((S5_UBLK, tm, LANES), lambda i: (0, i, 0)),
            pl.BlockSpec((S5_UBLK, tm, LANES), lambda i: (0, i, 0)),
            _const_spec((1, S5_WIDTH)),
            _const_spec((S5_WIDTH, S5_WIDTH)),
            _const_spec((1, S5_WIDTH)),
        ] + _token_specs(tm, GLA_VW) + [
            _const_spec((S5_WIDTH + GLA_VW, D_MODEL)),
        ] + _tail_specs(layer),
        out_specs=pl.BlockSpec((tm, D_MODEL), lambda i: (i, 0)),
        scratch_shapes=_tail_scratch(),
        compiler_params=pltpu.CompilerParams(**_TAIL_PARAMS),
        name="even_out_mlp",
    )(xp, xs, y5, u, d_skip, w_glu, b_glu, gla_p, gla_s, w_out, mods, gn2, w1, w2)


def _odd_out_call(x, att_p, att_s, w_o, mods, layer, gn2, w1, w2):
    tm = _OUT_TM
    return pl.pallas_call(
        functools.partial(_odd_out_kernel, layer=layer),
        out_shape=(jax.ShapeDtypeStruct((T_PROMPT, D_MODEL), F32),
                   jax.ShapeDtypeStruct((T_SAMPLE, D_MODEL), F32)),
        grid=(T_TOK // tm,),
        in_specs=[pl.BlockSpec((tm, D_MODEL), lambda i: (i, 0))] + _token_specs(tm) + [
            _const_spec((D_MODEL, D_MODEL)),
        ] + _tail_specs(layer),
        out_specs=tuple(_token_specs(tm)),
        scratch_shapes=_tail_scratch(),
        compiler_params=pltpu.CompilerParams(**_TAIL_PARAMS),
        name="odd_out_mlp",
    )(x, att_p, att_s, w_o, mods, gn2, w1, w2)


def _qkv_kernel(x_ref, gn_ref, m_ref, w_ref, qn_ref, kn_ref, cos_ref, sin_ref,
                q_ref, kb_ref, vb_ref, k32_ref, v32_ref, *, tile):
    even_lane = (lax.broadcasted_iota(jnp.int32, (1, HEAD_DIM), 1) & 1) == 0

    def heads(rope):
        h = _norm_mod(x_ref[...], gn_ref[...], m_ref[:, 0:D_MODEL], m_ref[:, D_MODEL:2 * D_MODEL]).astype(BF16)
        z = _dot(h, w_ref[...])
        v = z[:, (N_HEADS + KV_HEADS) * HEAD_DIM:]
        vb_ref[...] = v.astype(BF16)
        if not rope:
            for kh in range(KV_HEADS):
                v32_ref[:, kh, :] = v[:, kh * HEAD_DIM:(kh + 1) * HEAD_DIM]
        for hd in range(N_HEADS + KV_HEADS):
            xh = z[:, hd * HEAD_DIM:(hd + 1) * HEAD_DIM]
            gain = qn_ref[...] if hd < N_HEADS else kn_ref[...]
            xh = xh * lax.rsqrt(jnp.mean(xh * xh, axis=-1, keepdims=True) + EPS) * gain
            if rope:
                partner = jnp.where(even_lane, pltpu.roll(xh, HEAD_DIM - 1, 1), pltpu.roll(xh, 1, 1))
                xh = xh * cos_ref[...] + partner * sin_ref[...]
            if hd < N_HEADS:
                q_ref[:, hd * HEAD_DIM:(hd + 1) * HEAD_DIM] = xh.astype(BF16)
            else:
                cols = slice((hd - N_HEADS) * HEAD_DIM, (hd - N_HEADS + 1) * HEAD_DIM)
                kb_ref[:, cols] = xh.astype(BF16)
                if not rope:
                    k32_ref[:, hd - N_HEADS, :] = xh

    is_sample = pl.program_id(0) >= T_PROMPT // tile

    @pl.when(is_sample)
    def _():
        heads(True)

    @pl.when(jnp.logical_not(is_sample))
    def _():
        heads(False)


def _qkv_call(x, gn, mods, layer, w_qkv, q_norm, k_norm, cos_t, sin_t):
    tm = 512
    pos_tiles = DEC_SEQ // tm
    n_prompt = T_PROMPT // tm
    kvw = KV_HEADS * HEAD_DIM

    def pos_map(i):
        return (jnp.maximum(i - n_prompt, 0) % pos_tiles, 0)

    def prompt_map(i):
        return (jnp.minimum(i, n_prompt - 1), 0, 0)

    return pl.pallas_call(
        functools.partial(_qkv_kernel, tile=tm),
        out_shape=(jax.ShapeDtypeStruct((T_TOK, N_HEADS * HEAD_DIM), BF16),
                   jax.ShapeDtypeStruct((T_TOK, kvw), BF16),
                   jax.ShapeDtypeStruct((T_TOK, kvw), BF16),
                   jax.ShapeDtypeStruct((T_PROMPT, KV_HEADS, HEAD_DIM), F32),
                   jax.ShapeDtypeStruct((T_PROMPT, KV_HEADS, HEAD_DIM), F32)),
        grid=(T_TOK // tm,),
        in_specs=[
            pl.BlockSpec((tm, D_MODEL), lambda i: (i, 0)),
            pl.BlockSpec((1, D_MODEL), lambda i: (0, 0)),
            pl.BlockSpec((None, 1, 6 * D_MODEL), lambda i: (_cond_row(i, tm), 0, 0)),
            pl.BlockSpec(w_qkv.shape, lambda i: (0, 0)),
            pl.BlockSpec((1, HEAD_DIM), lambda i: (0, 0)),
            pl.BlockSpec((1, HEAD_DIM), lambda i: (0, 0)),
            pl.BlockSpec((tm, HEAD_DIM), pos_map),
            pl.BlockSpec((tm, HEAD_DIM), pos_map),
        ],
        out_specs=(pl.BlockSpec((tm, N_HEADS * HEAD_DIM), lambda i: (i, 0)),
                   pl.BlockSpec((tm, kvw), lambda i: (i, 0)),
                   pl.BlockSpec((tm, kvw), lambda i: (i, 0)),
                   pl.BlockSpec((tm, KV_HEADS, HEAD_DIM), prompt_map),
                   pl.BlockSpec((tm, KV_HEADS, HEAD_DIM), prompt_map)),
        compiler_params=pltpu.CompilerParams(vmem_limit_bytes=VMEM_LIMIT),
        name="odd_qkv",
    )(x, gn, mods, w_qkv, q_norm, k_norm, cos_t, sin_t)


def _rope_tables():
    f32 = np.float32
    rows = DEC_SEQ // GRID_W
    row = np.repeat(np.arange(rows, dtype=f32), GRID_W)
    col = np.tile(np.arange(GRID_W, dtype=f32), rows)
    inv = np.power(f32(ROPE_THETA), -np.arange(0, AXIS_DIM, 2, dtype=f32) / f32(AXIS_DIM)).astype(f32)
    ang = np.concatenate([row[:, None] * inv, col[:, None] * inv], axis=-1).astype(f32)
    cos_t = np.repeat(np.cos(ang), 2, axis=-1).astype(f32)
    sin = np.sin(ang).astype(f32)
    sin_t = np.stack([-sin, sin], axis=-1).reshape(DEC_SEQ, HEAD_DIM)
    return jnp.asarray(cos_t), jnp.asarray(sin_t)


def _attn_kernel(*refs, seq_len, has_cache):
    q_ref, k_ref, v_ref = refs[:3]
    ck_ref, cv_ref = refs[3:5] if has_cache else (None, None)
    o_ref = refs[-1]
    c = HEAD_DIM ** -0.5 * math.log2(math.e)
    ones_col = (lax.broadcasted_iota(jnp.int32, (1, HEAD_DIM), 1) == 0).astype(BF16)

    def with_ones(v):
        return jnp.concatenate([v, jnp.broadcast_to(ones_col, v.shape)], axis=1)

    if has_cache:
        ck = ck_ref[...].astype(BF16)
        cv = with_ones(cv_ref[...].astype(BF16))
    def one_sequence(rows):
        k = k_ref[rows, :]
        v = with_ones(v_ref[rows, :])
        for r in range(Q_PER_KV):
            cs = slice(r * HEAD_DIM, (r + 1) * HEAD_DIM)
            q = q_ref[rows, cs]
            s = lax.dot_general(q, k, NT_DIMS, preferred_element_type=F32)
            m = jnp.max(s, axis=-1, keepdims=True)
            if has_cache:
                sc = lax.dot_general(q, ck, NT_DIMS, preferred_element_type=F32)
                m = jnp.maximum(m, jnp.max(sc, axis=-1, keepdims=True))
            mc = m * c
            o = _dot(jnp.exp2(s * c - mc).astype(BF16), v)
            if has_cache:
                o = o + _dot(jnp.exp2(sc * c - mc).astype(BF16), cv)
            o_ref[rows, cs] = (o[:, 0:HEAD_DIM] / o[:, HEAD_DIM:HEAD_DIM + 1]).astype(BF16)

    nsub = q_ref.shape[0] // seq_len
    if nsub == 1:
        one_sequence(slice(0, seq_len))
    else:
        def body(j, carry):
            one_sequence(pl.ds(pl.multiple_of(j * seq_len, seq_len), seq_len))
            return carry

        lax.fori_loop(0, nsub, body, 0)


def _attn_call(q, k, v, cache_k, cache_v, seq_len, row0, nrows, nsub):
    blk = nsub * seq_len
    assert row0 % blk == 0 and nrows % blk == 0
    has_cache = cache_k is not None
    assert not has_cache or nsub == 1
    b0 = row0 // blk
    gw = Q_PER_KV * HEAD_DIM
    in_specs = [
        pl.BlockSpec((blk, gw), lambda b, g: (b0 + b, g)),
        pl.BlockSpec((blk, HEAD_DIM), lambda b, g: (b0 + b, g)),
        pl.BlockSpec((blk, HEAD_DIM), lambda b, g: (b0 + b, g)),
    ]
    args = [q, k, v]
    if has_cache:
        in_specs += [pl.BlockSpec((PAST_LEN, HEAD_DIM), lambda b, g: (b, g)),
                     pl.BlockSpec((PAST_LEN, HEAD_DIM), lambda b, g: (b, g))]
        args += [cache_k, cache_v]
    return pl.pallas_call(
        functools.partial(_attn_kernel, seq_len=seq_len, has_cache=has_cache),
        out_shape=jax.ShapeDtypeStruct((nrows, N_HEADS * HEAD_DIM), BF16),
        grid=(nrows // blk, KV_HEADS),
        in_specs=in_specs,
        out_specs=pl.BlockSpec((blk, gw), lambda b, g: (b, g)),
        compiler_params=pltpu.CompilerParams(vmem_limit_bytes=VMEM_LIMIT),
        name=f"attn_len{seq_len}",
    )(*args)


def kernel(x_prompt, x_sample, state_s5_re, state_s5_im, state_gla, cache_k, cache_v, c, c_ctx, norm_mix, norm_mlp, w_ada, b_ada, w_mlp_in, w_mlp_out, w_in_e, w_out_e, s5_lambda_re, s5_lambda_im, s5_log_dt, s5_b_re, s5_b_im, s5_c_re, s5_c_im, s5_d, s5_w_glu, s5_b_glu, gla_w_gate2, gla_b_gate, gla_norm, w_qkv_o, w_o_o, q_norm, k_norm):
    xp = x_prompt.reshape(T_PROMPT, D_MODEL)
    xs = x_sample.reshape(T_SAMPLE, D_MODEL)
    cond8 = jnp.concatenate([c_ctx[None, :], c, jnp.zeros((COND_ROWS - 1 - DEC_BATCH, D_MODEL), F32)], axis=0)
    b_ada3 = b_ada.reshape(DEPTH, 1, 6 * D_MODEL)
    w1_all, w2_all = w_mlp_in, w_mlp_out
    mats, mods0 = _s5_prep_call(s5_lambda_re[0], s5_lambda_im[0], s5_log_dt[0], s5_b_re[0], s5_b_im[0],
                                s5_c_re[0], s5_c_im[0], ada=(cond8, w_ada, b_ada3, 0))

    w_in = jnp.pad(w_in_e[0], ((0, 0), (0, LANES - 2 * GLA_RANK))).astype(BF16)
    zg = jnp.zeros((GLA_RANK, GLA_QK), F32)
    w_gate = jnp.concatenate([jnp.concatenate([gla_w_gate2[0, 0], zg], axis=1),
                              jnp.concatenate([zg, gla_w_gate2[0, 1]], axis=1),
                              jnp.zeros((LANES - 2 * GLA_RANK, 2 * GLA_QK), F32)], axis=0).astype(BF16)
    b_gate = gla_b_gate[0].reshape(1, 2 * GLA_QK)
    u, z, g = _inproj_call(xp, xs, norm_mix[0:1], mods0, 0, w_in, w_gate, b_gate)

    def state_rows(s):
        return jnp.transpose(s, (2, 0, 1, 3)).reshape(S5_GROUPS, DEC_BATCH, 2 * S5_STATE)

    h0 = jnp.concatenate([state_rows(state_s5_re[:, 0]), state_rows(state_s5_im[:, 0])], axis=-1)
    nsteps = S5_GROUPS // S5_GPB
    h0 = jnp.transpose(h0.reshape(nsteps, S5_GPB, DEC_BATCH, S5_W), (0, 2, 1, 3))
    y5, ns = _s5_call(u, mats, h0)
    ns = jnp.transpose(ns, (0, 2, 1, 3)).reshape(S5_GROUPS, BATCH, S5_W)

    def state_out(n):
        return jnp.transpose(n.reshape(S5_GROUPS, BATCH, 2, S5_STATE), (1, 2, 0, 3))[:, None]

    new_s5_re = state_out(ns[:, :, :2 * S5_STATE])
    new_s5_im = state_out(ns[:, :, 2 * S5_STATE:])

    gn_gla = gla_norm[0].reshape(1, GLA_DV)
    gla_p, sfin, w_glu, w_out, w_qkv, w_o = _gla_call(z, g, gn_gla, None, SEQ, BATCH, 0, nsub=4,
                                                      casts=(s5_w_glu, w_out_e, w_qkv_o, w_o_o))
    s0 = state_gla[:, 0].reshape(DEC_BATCH, 2, GLA_QK, GLA_DV)
    gla_s, _, mods1 = _gla_call(z, g, gn_gla, s0, DEC_SEQ, DEC_BATCH, T_PROMPT, nsub=1,
                                ada=(cond8, w_ada, b_ada3, 1))
    new_gla = sfin.reshape(BATCH, 1, 2, GLA_HEADS, GLA_DK, GLA_DV)

    x = _even_out_call(xp, xs, y5, u, s5_d[0].reshape(1, S5_WIDTH), w_glu,
                       s5_b_glu[0].reshape(1, S5_WIDTH), gla_p, gla_s, w_out, mods0, 0,
                       norm_mlp[0:1], w1_all, w2_all)

    cos_t, sin_t = _rope_tables()
    q, k, v, k32, v32 = _qkv_call(x, norm_mix[1:2], mods1, 1, w_qkv,
                                  q_norm[0].reshape(1, HEAD_DIM), k_norm[0].reshape(1, HEAD_DIM), cos_t, sin_t)
    att_p = _attn_call(q, k, v, None, None, SEQ, 0, T_PROMPT, nsub=4)
    ck = cache_k[:, 0].reshape(DEC_BATCH * PAST_LEN, KV_HEADS * HEAD_DIM)
    cv = cache_v[:, 0].reshape(DEC_BATCH * PAST_LEN, KV_HEADS * HEAD_DIM)
    att_s = _attn_call(q, k, v, ck, cv, DEC_SEQ, T_PROMPT, T_SAMPLE, nsub=1)
    yp, ys = _odd_out_call(x, att_p, att_s, w_o, mods1, 1, norm_mlp[1:2],
                           w1_all, w2_all)

    new_k = k32.reshape(BATCH, 1, SEQ, KV_HEADS, HEAD_DIM)
    new_v = v32.reshape(BATCH, 1, SEQ, KV_HEADS, HEAD_DIM)
    y_prompt = yp.reshape(BATCH, SEQ, D_MODEL)
    y_sample = ys.reshape(DEC_BATCH, DEC_SEQ, D_MODEL)
    return (y_prompt, y_sample, new_s5_re, new_s5_im, new_gla, new_k, new_v)
```

```python
import functools
import math

import jax
import jax.numpy as jnp
import numpy as np
from jax import lax
from jax.experimental import pallas as pl
from jax.experimental.pallas import tpu as pltpu

F32 = jnp.float32
BF16 = jnp.bfloat16

LANES = 128

D_MODEL = 1024
BATCH = 16
SEQ = 256
DEPTH = 2
DEC_BATCH = 4
DEC_SEQ = 1024
PAST_LEN = 512
GRID_W = 64
S5_WIDTH = 512
S5_GROUP_CH = 16
S5_GROUPS = 32
S5_STATE = 64
GLA_HEADS = 4
GLA_VW = 512
GLA_DV = 128
GLA_DK = 64
GLA_QK = 256
GLA_RANK = 16
GLA_TAU = 16.0
GLA_CHUNK = 64
GLA_CPB = 4
GLA_BLK = GLA_CPB * GLA_CHUNK
HEAD_DIM = 128
N_HEADS = 8
KV_HEADS = 2
Q_PER_KV = N_HEADS // KV_HEADS
AXIS_DIM = 64
ROPE_THETA = 10000.0
D_FF = 4096
EPS = 1e-6

T_PROMPT = BATCH * SEQ
T_SAMPLE = DEC_BATCH * DEC_SEQ
T_TOK = T_PROMPT + T_SAMPLE
COND_ROWS = 8
COND_SPAN = 1024
PROMPT_SPANS = T_PROMPT // COND_SPAN

S5_Q = 16
S5_W = S5_Q * S5_GROUP_CH
S5_GPB = LANES // S5_GROUP_CH
S5_UBLK = S5_WIDTH // LANES
S5_ROWS = T_TOK // S5_Q
S5_PROMPT_ROWS = T_PROMPT // S5_Q
S5_PROMPT_CHUNKS = SEQ // S5_Q
S5_SAMPLE_CHUNKS = DEC_SEQ // S5_Q

VMEM_LIMIT = 56 * 1024 * 1024
VMEM_LIMIT_TAIL = 60 * 1024 * 1024

NT_DIMS = (((1,), (1,)), ((), ()))


def _cond_row(i, tile):
    return jnp.maximum((i * tile) // COND_SPAN - (PROMPT_SPANS - 1), 0)


def _norm_mod(x, gain, shift, scale):
    y = x * lax.rsqrt(jnp.mean(x * x, axis=-1, keepdims=True) + EPS)
    return (y * gain) * (1.0 + scale) + shift


def _dot(a, b):
    return jnp.dot(a, b, preferred_element_type=F32)


def _ada_kernel(cond_ref, w_ref, b_ref, o_ref):
    s = jax.nn.silu(cond_ref[...]).astype(BF16)
    o_ref[:, 0, :] = _dot(s, w_ref[...].astype(BF16)) + b_ref[...]


ADA_SHAPE = jax.ShapeDtypeStruct((COND_ROWS, 1, 6 * D_MODEL), F32)


def _ada_specs(layer, tn):
    return ([pl.BlockSpec((COND_ROWS, D_MODEL), lambda *ids: (0, 0)),
             pl.BlockSpec((None, D_MODEL, tn), lambda *ids: (layer, 0, ids[-1])),
             pl.BlockSpec((None, 1, tn), lambda *ids: (layer, 0, ids[-1]))],
            pl.BlockSpec((COND_ROWS, 1, tn), lambda *ids: (0, 0, ids[-1])))


def _token_specs(tile, width=D_MODEL):
    n_prompt = T_PROMPT // tile
    return [pl.BlockSpec((tile, width), lambda i: (jnp.minimum(i, n_prompt - 1), 0)),
            pl.BlockSpec((tile, width), lambda i: (jnp.maximum(i - n_prompt, 0), 0))]


def _token_tile(xp_ref, xs_ref, tile):
    return jnp.where(pl.program_id(0) < T_PROMPT // tile, xp_ref[...], xs_ref[...])


def _inproj_kernel(xp_ref, xs_ref, gn_ref, m_ref, w_ref, wglr_ref, wg_ref, bg_ref, u_ref, z_ref, g_ref, *, tile):
    x = _token_tile(xp_ref, xs_ref, tile)
    h = _norm_mod(x, gn_ref[...], m_ref[:, 0:D_MODEL], m_ref[:, D_MODEL:2 * D_MODEL]).astype(BF16)
    z = _dot(h, w_ref[...])
    for blk in range(S5_UBLK):
        u_ref[blk] = z[:, blk * LANES:(blk + 1) * LANES]
    z_ref[...] = z[:, S5_WIDTH:]
    glr = _dot(h, wglr_ref[...]).astype(BF16)
    pre = _dot(glr, wg_ref[...]) + bg_ref[...]
    g_ref[...] = jax.nn.log_sigmoid(pre) * (1.0 / GLA_TAU)


def _inproj_call(xp, xs, gn, mods, layer, w_in, w_gate, b_gate):
    tm = 512
    nz = w_in.shape[1] - LANES
    return pl.pallas_call(
        functools.partial(_inproj_kernel, tile=tm),
        out_shape=(jax.ShapeDtypeStruct((S5_UBLK, T_TOK, LANES), F32),
                   jax.ShapeDtypeStruct((T_TOK, nz - S5_WIDTH), F32),
                   jax.ShapeDtypeStruct((T_TOK, 2 * GLA_QK), F32)),
        grid=(T_TOK // tm,),
        in_specs=_token_specs(tm) + [
            pl.BlockSpec((1, D_MODEL), lambda i: (0, 0)),
            pl.BlockSpec((None, 1, 6 * D_MODEL), lambda i: (_cond_row(i, tm), 0, 0)),
            pl.BlockSpec((D_MODEL, nz), lambda i: (0, 0)),
            pl.BlockSpec((D_MODEL, LANES), lambda i: (0, nz // LANES)),
            pl.BlockSpec((LANES, 2 * GLA_QK), lambda i: (0, 0)),
            pl.BlockSpec((1, 2 * GLA_QK), lambda i: (0, 0)),
        ],
        out_specs=(pl.BlockSpec((S5_UBLK, tm, LANES), lambda i: (0, i, 0)),
                   pl.BlockSpec((tm, nz - S5_WIDTH), lambda i: (i, 0)),
                   pl.BlockSpec((tm, 2 * GLA_QK), lambda i: (i, 0))),
        compiler_params=pltpu.CompilerParams(vmem_limit_bytes=VMEM_LIMIT),
        name="even_inproj",
    )(xp, xs, gn, mods, w_in, w_in, w_gate, b_gate)


S5_PREP_GPB = 8
_PREP_LRE, _PREP_LIM, _PREP_LDT = 0, 1, 2
_PREP_BT_RE, _PREP_BT_IM, _PREP_C_RE, _PREP_C_IM, _PREP_ROWS = 8, 24, 40, 56, 72


def _s5_prep_kernel(p_ref, cc_ref, cond_ref, wada_ref, bada_ref, t_ref, bq_ref, cqt_ref, be_ref, a_ref, mods_ref,
                    t_scr, dd_scr):
    _ada_kernel(cond_ref, wada_ref, bada_ref, mods_ref)
    for gi in range(S5_PREP_GPB):
        _s5_prep_group(p_ref.at[gi], cc_ref.at[gi], t_ref.at[gi], bq_ref.at[gi], cqt_ref.at[gi], be_ref.at[gi],
                       a_ref.at[gi], t_scr, dd_scr)


def _s5_prep_group(p_ref, cc_ref, t_ref, bq_ref, cqt_ref, be_ref, a_ref, t_scr, dd_scr):
    gch = S5_GROUP_CH
    lre = p_ref[_PREP_LRE:_PREP_LRE + 1]
    lim = p_ref[_PREP_LIM:_PREP_LIM + 1]
    dt = jnp.exp(p_ref[_PREP_LDT:_PREP_LDT + 1])
    a = lre * dt
    th = lim * dt

    def lam_pow(k):
        mag = jnp.exp(k * a)
        return mag * jnp.cos(k * th), mag * jnp.sin(k * th)

    lb_re, lb_im = lam_pow(1.0)
    nr = lb_re - 1.0
    den = lre * lre + lim * lim
    cf_re = (nr * lre + lb_im * lim) / den
    cf_im = (lb_im * lre - nr * lim) / den
    bt_re = p_ref[_PREP_BT_RE:_PREP_BT_RE + gch]
    bt_im = p_ref[_PREP_BT_IM:_PREP_BT_IM + gch]
    bb_re = jnp.tile(cf_re * bt_re - cf_im * bt_im, (S5_Q, 1))
    bb_im = jnp.tile(cf_re * bt_im + cf_im * bt_re, (S5_Q, 1))

    shape = (S5_W, LANES)
    pos = lax.shift_right_logical(lax.broadcasted_iota(jnp.int32, shape, 0), 4)
    is_f = lax.broadcasted_iota(jnp.int32, shape, 1) < S5_STATE
    posq = lax.broadcasted_iota(jnp.int32, (S5_Q, LANES), 0).astype(F32)
    is_fq = lax.broadcasted_iota(jnp.int32, (S5_Q, LANES), 1) < S5_STATE

    def per_channel(tbl):
        return jnp.broadcast_to(tbl[:, None, :], (S5_Q, S5_GROUP_CH, LANES)).reshape(shape)

    p_re, p_im = map(per_channel, lam_pow(jnp.where(is_fq, (S5_Q - 1.0) - posq, posq)))
    w_re = p_re * bb_re - p_im * bb_im
    w_im = p_re * bb_im + p_im * bb_re
    bq = jnp.concatenate([w_re, w_im], axis=1)
    bqt = jnp.transpose(bq)
    bq_ref[...] = bqt.astype(BF16)

    edge = pos == jnp.where(is_f, 0, S5_Q - 1)
    be = jnp.concatenate([jnp.where(edge, bb_re, 0.0), jnp.where(edge, bb_im, 0.0)], axis=1)
    be_ref[...] = jnp.transpose(be).astype(BF16)

    q_re, q_im = map(per_channel, lam_pow(jnp.where(is_fq, posq + 1.0, S5_Q - posq)))
    ct_re = jnp.tile(p_ref[_PREP_C_RE:_PREP_C_RE + gch], (S5_Q, 1))
    ct_im = jnp.tile(p_ref[_PREP_C_IM:_PREP_C_IM + gch], (S5_Q, 1))
    g_re = q_re * ct_re - q_im * ct_im
    g_im = q_re * ct_im + q_im * ct_re
    cqt_ref[...] = jnp.concatenate([g_re, -g_im], axis=1).astype(BF16)

    a_re, a_im = lam_pow(float(S5_Q))
    a_ref[...] = jnp.concatenate([a_re, a_im], axis=1)

    kf = jnp.dot(cc_ref[0:gch], bqt, precision=lax.Precision.HIGHEST, preferred_element_type=F32)
    kb = jnp.dot(cc_ref[gch:2 * gch], bqt, precision=lax.Precision.HIGHEST, preferred_element_type=F32)
    lo = S5_W - gch
    dd_scr[:, 0:S5_W] = kf
    dd_scr[:, lo:lo + S5_W] = kb
    dd_scr[:, lo:S5_W] = kf[:, lo:S5_W] + kb[:, 0:gch]
    for t in range(S5_Q):
        c0 = (S5_Q - 1 - t) * gch
        t_scr[t * gch:(t + 1) * gch, :] = dd_scr[:, c0:c0 + S5_W]
    t_ref[...] = t_scr[...].astype(BF16)


def _s5_prep_call(lam_re, lam_im, log_dt, b_re, b_im, c_re, c_im, ada):
    per_dir = jnp.stack([lam_re, lam_im, jnp.broadcast_to(log_dt[:, :, None], lam_re.shape)])
    head = jnp.transpose(per_dir, (2, 0, 1, 3)).reshape(S5_GROUPS, 3, 2 * S5_STATE)
    pad = jnp.zeros((S5_GROUPS, _PREP_BT_RE - _PREP_LDT - 1, LANES), F32)
    shared = jnp.concatenate([jnp.transpose(b_re, (0, 2, 1)), jnp.transpose(b_im, (0, 2, 1)), c_re, c_im],
                             axis=1)
    packed = jnp.concatenate([head, pad, jnp.concatenate([shared, shared], axis=-1)], axis=1)
    zero = jnp.zeros_like(c_re)
    cc = jnp.concatenate([jnp.concatenate([c_re, zero, -c_im, zero], axis=-1),
                          jnp.concatenate([zero, c_re, zero, -c_im], axis=-1)], axis=1)

    gpb = S5_PREP_GPB
    nsteps = S5_GROUPS // gpb
    sq = pl.BlockSpec((gpb, S5_W, S5_W), lambda g: (g, 0, 0))
    sq_shape = jax.ShapeDtypeStruct((S5_GROUPS, S5_W, S5_W), BF16)
    cond8, w_ada, b_ada, ada_layer = ada
    ada_in, ada_out = _ada_specs(ada_layer, 6 * D_MODEL // nsteps)
    *mats, mods = pl.pallas_call(
        _s5_prep_kernel,
        out_shape=(sq_shape, sq_shape, sq_shape, sq_shape,
                   jax.ShapeDtypeStruct((S5_GROUPS, 1, S5_W), F32), ADA_SHAPE),
        grid=(nsteps,),
        in_specs=[pl.BlockSpec((gpb, _PREP_ROWS, LANES), lambda g: (g, 0, 0)),
                  pl.BlockSpec((gpb, 2 * S5_GROUP_CH, S5_W), lambda g: (g, 0, 0))] + ada_in,
        out_specs=(sq, sq, sq, sq, pl.BlockSpec((gpb, 1, S5_W), lambda g: (g, 0, 0)), ada_out),
        scratch_shapes=[pltpu.VMEM((S5_W, S5_W), F32), pltpu.VMEM((S5_GROUP_CH, 2 * S5_W), F32)],
        compiler_params=pltpu.CompilerParams(vmem_limit_bytes=VMEM_LIMIT),
        name="s5_prep",
    )(packed, cc, cond8, w_ada, b_ada)
    return mats, mods


def _s5_kernel(u_ref, tt_ref, bqt_ref, cqt_ref, bet_ref, a_ref, h0_ref, y_ref, ns_ref,
               ut_scr, x_scr, spf_scr, spb_scr, ne_scr, yt_scr, xt_scr):
    gch = S5_GROUP_CH
    for s in range(S5_Q):
        rows = u_ref[pl.ds(s, S5_ROWS, stride=S5_Q), :]
        rows_t = jnp.transpose(rows).astype(BF16)
        for gl in range(S5_GPB):
            ut_scr[gl, s * gch:(s + 1) * gch, :] = rows_t[gl * gch:(gl + 1) * gch, :]

    for gl in range(S5_GPB):
        ut = ut_scr[gl]
        xt_scr[...] = _dot(bqt_ref[gl], ut)
        x = jnp.transpose(xt_scr[...])
        xt_scr[:, 0:S5_PROMPT_ROWS] = _dot(bet_ref[gl], ut[:, 0:S5_PROMPT_ROWS])
        ne = jnp.transpose(xt_scr[:, 0:S5_PROMPT_ROWS])
        for part in range(2):
            x_scr[part, pl.ds(gl, S5_ROWS, stride=S5_GPB), :] = x[:, part * LANES:(part + 1) * LANES]
            ne_scr[part, pl.ds(gl, S5_PROMPT_ROWS, stride=S5_GPB), :] = ne[:, part * LANES:(part + 1) * LANES]

    is_f = lax.broadcasted_iota(jnp.int32, (1, LANES), 1) < S5_STATE
    a_re = a_ref[:, 0:LANES]
    a_im = a_ref[:, LANES:2 * LANES]

    def tile(row):
        return pl.ds(pl.multiple_of(row * S5_GPB, S5_GPB), S5_GPB)

    def scan(base, nseq, nchunk, s_init):
        def body(i, state):
            new = []
            for b in range(nseq):
                s_re, s_im = state[b]
                rows_f = tile(base + b * nchunk + i)
                rows_b = tile(base + b * nchunk + (nchunk - 1 - i))
                spf_scr[0, rows_f, :] = s_re
                spf_scr[1, rows_f, :] = s_im
                spb_scr[0, rows_b, :] = s_re
                spb_scr[1, rows_b, :] = s_im
                x_re = jnp.where(is_f, x_scr[0, rows_f, :], x_scr[0, rows_b, :])
                x_im = jnp.where(is_f, x_scr[1, rows_f, :], x_scr[1, rows_b, :])
                new.append((a_re * s_re - a_im * s_im + x_re, a_re * s_im + a_im * s_re + x_im))
            return tuple(new)

        lax.fori_loop(0, nchunk, body, tuple(s_init))

    zero = jnp.zeros((S5_GPB, LANES), F32)
    scan(0, BATCH, S5_PROMPT_CHUNKS, [(zero, zero)] * BATCH)
    scan(S5_PROMPT_ROWS, DEC_BATCH, S5_SAMPLE_CHUNKS,
         [(h0_ref[b, :, 0:LANES], h0_ref[b, :, LANES:2 * LANES]) for b in range(DEC_BATCH)])

    for b in range(BATCH):
        first = pl.ds(b * S5_PROMPT_CHUNKS * S5_GPB, S5_GPB)
        last = pl.ds(((b + 1) * S5_PROMPT_CHUNKS - 1) * S5_GPB, S5_GPB)
        for part in range(2):
            ns_ref[b, :, part * LANES:(part + 1) * LANES] = jnp.where(is_f, ne_scr[part, first, :], ne_scr[part, last, :])

    for gl in range(S5_GPB):
        rows = pl.ds(gl, S5_ROWS, stride=S5_GPB)
        carried = jnp.concatenate([jnp.where(is_f, spf_scr[p, rows, :], spb_scr[p, rows, :]) for p in range(2)],
                                  axis=1).astype(BF16)
        yt = _dot(tt_ref[gl], ut_scr[gl]) + lax.dot_general(cqt_ref[gl], carried, NT_DIMS,
                                                            preferred_element_type=F32)
        for t in range(S5_Q):
            yt_scr[t, gl * gch:(gl + 1) * gch, :] = yt[t * gch:(t + 1) * gch, :]
    for t in range(S5_Q):
        y_ref[pl.ds(t, S5_ROWS, stride=S5_Q), :] = jnp.transpose(yt_scr[t])


def _s5_call(u, mats, h0):
    tt_m, bqt_m, cqt_m, bet_m, a_m = mats
    nsteps = S5_GROUPS // S5_GPB
    sq = pl.BlockSpec((S5_GPB, S5_W, S5_W), lambda g: (g, 0, 0))
    state_scr = pltpu.VMEM((2, S5_ROWS * S5_GPB, LANES), F32)
    return pl.pallas_call(
        _s5_kernel,
        out_shape=(jax.ShapeDtypeStruct((nsteps, T_TOK, LANES), F32),
                   jax.ShapeDtypeStruct((nsteps, BATCH, S5_GPB, S5_W), F32)),
        grid=(nsteps,),
        in_specs=[
            pl.BlockSpec((None, T_TOK, LANES), lambda g: (g, 0, 0)),
            sq, sq, sq, sq,
            pl.BlockSpec((S5_GPB, S5_W), lambda g: (g, 0)),
            pl.BlockSpec((None, DEC_BATCH, S5_GPB, S5_W), lambda g: (g, 0, 0, 0)),
        ],
        out_specs=(pl.BlockSpec((None, T_TOK, LANES), lambda g: (g, 0, 0)),
                   pl.BlockSpec((None, BATCH, S5_GPB, S5_W), lambda g: (g, 0, 0, 0))),
        scratch_shapes=[pltpu.VMEM((S5_GPB, S5_W, S5_ROWS), BF16), state_scr, state_scr, state_scr,
                        pltpu.VMEM((2, S5_PROMPT_ROWS * S5_GPB, LANES), F32),
                        pltpu.VMEM((S5_Q, LANES, S5_ROWS), F32), pltpu.VMEM((S5_W, S5_ROWS), F32)],
        compiler_params=pltpu.CompilerParams(vmem_limit_bytes=VMEM_LIMIT),
        name="s5_scan",
    )(u, tt_m, bqt_m, cqt_m, bet_m, a_m.reshape(S5_GROUPS, S5_W), h0)


def _split_bf16(x):
    hi = x.astype(BF16)
    r1 = x - hi.astype(F32)
    mid = r1.astype(BF16)
    lo = (r1 - mid.astype(F32)).astype(BF16)
    return hi, mid, lo


def _cast_specs(shape, nsteps):
    _, rows, cols = shape
    rb = rows // nsteps
    return (pl.BlockSpec((None, rb, cols), lambda *ids: (0, ids[-1], 0)),
            pl.BlockSpec((rb, cols), lambda *ids: (ids[-1], 0)))


def _gla_kernel(*refs, seq_len, nsub, has_s0, has_ada, n_cast):
    rows_refs, gn_ref = refs[:6], refs[6]
    n_ada_in = 7 + has_s0
    n_cast_in = n_ada_in + 3 * has_ada
    n_in = n_cast_in + n_cast
    s0_ref = refs[7] if has_s0 else None
    o_ref, sfin_ref = refs[n_in:n_in + 2]
    n_out = 2 + has_ada + n_cast
    scratch = refs[n_in + n_out:]
    if has_ada:
        _ada_kernel(*refs[n_ada_in:n_cast_in], refs[n_in + 2])
    for src, dst in zip(refs[n_cast_in:n_in], refs[n_in + 2 + has_ada:n_in + n_out]):
        dst[...] = src[...].astype(BF16)
    for j in range(nsub):
        rows = pl.ds(j * seq_len, seq_len)
        _gla_sequence(*[r.at[rows, :] for r in rows_refs], gn_ref, s0_ref.at[j] if has_s0 else None,
                      o_ref.at[rows, :], sfin_ref.at[j], *[s.at[j] for s in scratch], seq_len=seq_len)


def _gla_sequence(q_ref, k_ref, v_ref, gf_ref, gb_ref, r_ref, gn_ref, s0_ref, o_ref, sfin_ref,
                  oi_scr, qd_scr, kv_scr, dec_scr, ss_scr, *, seq_len):
    has_s0 = s0_ref is not None
    nblk = seq_len // GLA_BLK
    nchunk = seq_len // GLA_CHUNK
    cl = GLA_CHUNK
    ti = lax.broadcasted_iota(jnp.int32, (GLA_BLK, GLA_BLK), 0)
    si = lax.broadcasted_iota(jnp.int32, (GLA_BLK, GLA_BLK), 1)
    same = lax.shift_right_logical(ti, 6) == lax.shift_right_logical(si, 6)
    keep = (same & (ti >= si), same & (ti <= si))
    tri = tuple(kp.astype(BF16) for kp in keep)
    lane_head = lax.shift_right_logical(lax.broadcasted_iota(jnp.int32, (cl, GLA_QK), 1), 6)
    zeros_v = jnp.zeros((cl, GLA_DV), BF16)
    heads = [(slice(h * GLA_DK, (h + 1) * GLA_DK), slice(h * GLA_DV, (h + 1) * GLA_DV)) for h in range(GLA_HEADS)]

    for j in range(nblk):
        rows = slice(j * GLA_BLK, (j + 1) * GLA_BLK)
        q = q_ref[rows, :] * (GLA_DK ** -0.5)
        k = k_ref[rows, :]
        v = v_ref[rows, :].astype(BF16)
        qd, kd, k2t = [], [], []
        for d, g_ref in enumerate((gf_ref, gb_ref)):
            b = sum(_dot(tri[d], part) for part in _split_bf16(g_ref[rows, :]))
            last = cl - 1 if d == 0 else 0
            b_last = [b[c * cl + last:c * cl + last + 1] for c in range(GLA_CPB)]
            bl = jnp.concatenate([jnp.broadcast_to(x, (cl, GLA_QK)) for x in b_last], axis=0)
            qd_d = (q * jnp.exp(b)).astype(BF16)
            qd_scr[d, rows, :] = qd_d
            qd.append(qd_d)
            kd.append((k * jnp.exp(-b)).astype(BF16))
            k2t.append(jnp.transpose(k * jnp.exp(bl - b)).astype(BF16))
            for c in range(GLA_CPB):
                dec_scr[d, j * GLA_CPB + c] = jnp.exp(jnp.transpose(jnp.broadcast_to(b_last[c], (GLA_DV, GLA_QK))))
        for h, (ks, vs) in enumerate(heads):
            att = [jnp.where(keep[d], lax.dot_general(qd[d][:, ks], kd[d][:, ks], NT_DIMS,
                                                      preferred_element_type=F32), 0.0) for d in range(2)]
            oi_scr[rows, vs] = _dot((att[0] + att[1]).astype(BF16), v[:, vs])
            vh = v[:, vs]
            vexp = jnp.concatenate(
                [jnp.concatenate([vh[c * cl:(c + 1) * cl] if c2 == c else zeros_v for c2 in range(GLA_CPB)], axis=1)
                 for c in range(GLA_CPB)], axis=0)
            for d in range(2):
                kv_scr[d, j, h] = _dot(k2t[d][ks, :], vexp)

    for d in range(2):
        s = s0_ref[d] if has_s0 else jnp.zeros((GLA_QK, GLA_DV), F32)
        for cg in (range(nchunk) if d == 0 else range(nchunk - 1, -1, -1)):
            j, c = divmod(cg, GLA_CPB)
            ss_scr[d, cg] = s.astype(BF16)
            kv = jnp.concatenate([kv_scr[d, j, h, :, c * GLA_DV:(c + 1) * GLA_DV] for h in range(GLA_HEADS)], axis=0)
            s = s * dec_scr[d, cg] + kv
        sfin_ref[d] = s

    for cg in range(nchunk):
        rows = slice(cg * cl, (cg + 1) * cl)
        inter = []
        for d in range(2):
            qc = qd_scr[d, rows, :]
            qstack = jnp.concatenate([jnp.where(lane_head == h, qc, jnp.zeros_like(qc)) for h in range(GLA_HEADS)],
                                     axis=0)
            inter.append(_dot(qstack, ss_scr[d, cg]))
        gate = jax.nn.silu(r_ref[rows, :])
        for h, (ks, vs) in enumerate(heads):
            hr = slice(h * cl, (h + 1) * cl)
            oh = oi_scr[rows, vs] + inter[0][hr] + inter[1][hr]
            oh = oh * lax.rsqrt(jnp.mean(oh * oh, axis=-1, keepdims=True) + EPS) * gn_ref[...]
            o_ref[rows, vs] = oh * gate[:, vs]


def _gla_call(z, g, gla_norm, s0, seq_len, nseq, row0, nsub, ada=None, casts=()):
    blk = nsub * seq_len
    assert row0 % blk == 0 and nseq % nsub == 0
    r0 = row0 // blk
    has_s0 = s0 is not None
    qk_off = 0
    v_off = 2 * GLA_QK // GLA_VW
    in_specs = [
        pl.BlockSpec((blk, GLA_QK), lambda i: (r0 + i, qk_off)),
        pl.BlockSpec((blk, GLA_QK), lambda i: (r0 + i, qk_off + 1)),
        pl.BlockSpec((blk, GLA_VW), lambda i: (r0 + i, v_off)),
        pl.BlockSpec((blk, GLA_QK), lambda i: (r0 + i, 0)),
        pl.BlockSpec((blk, GLA_QK), lambda i: (r0 + i, 1)),
        pl.BlockSpec((blk, GLA_VW), lambda i: (r0 + i, v_off + 1)),
        pl.BlockSpec((1, GLA_DV), lambda i: (0, 0)),
    ]
    args = [z, z, z, g, g, z, gla_norm]
    state_spec = pl.BlockSpec((nsub, 2, GLA_QK, GLA_DV), lambda i: (i, 0, 0, 0))
    if has_s0:
        in_specs.append(state_spec)
        args.append(s0)
    out_shape = [jax.ShapeDtypeStruct((nseq * seq_len, GLA_VW), F32),
                 jax.ShapeDtypeStruct((nseq, 2, GLA_QK, GLA_DV), F32)]
    out_specs = [pl.BlockSpec((blk, GLA_VW), lambda i: (i, 0)), state_spec]
    nsteps = nseq // nsub
    if ada is not None:
        cond8, w_ada, b_ada, ada_layer = ada
        ada_in, ada_out = _ada_specs(ada_layer, 6 * D_MODEL // nsteps)
        in_specs += ada_in
        args += [cond8, w_ada, b_ada]
        out_shape.append(ADA_SHAPE)
        out_specs.append(ada_out)
    for w in casts:
        cast_in, cast_out = _cast_specs(w.shape, nsteps)
        in_specs.append(cast_in)
        args.append(w)
        out_shape.append(jax.ShapeDtypeStruct(w.shape[1:], BF16))
        out_specs.append(cast_out)
    return pl.pallas_call(
        functools.partial(_gla_kernel, seq_len=seq_len, nsub=nsub, has_s0=has_s0, has_ada=ada is not None,
                          n_cast=len(casts)),
        out_shape=tuple(out_shape),
        grid=(nsteps,),
        in_specs=in_specs,
        out_specs=tuple(out_specs),
        scratch_shapes=[
            pltpu.VMEM((nsub, seq_len, GLA_VW), F32),
            pltpu.VMEM((nsub, 2, seq_len, GLA_QK), BF16),
            pltpu.VMEM((nsub, 2, seq_len // GLA_BLK, GLA_HEADS, GLA_DK, GLA_CPB * GLA_DV), F32),
            pltpu.VMEM((nsub, 2, seq_len // GLA_CHUNK, GLA_QK, GLA_DV), F32),
            pltpu.VMEM((nsub, 2, seq_len // GLA_CHUNK, GLA_QK, GLA_DV), BF16),
        ],
        compiler_params=pltpu.CompilerParams(vmem_limit_bytes=VMEM_LIMIT),
        name=f"gla_len{seq_len}",
    )(*args)


MLP_CHUNK = 512
MLP_LOAD = 256
MLP_SLOTS = 4


class _MlpWeights:
    def __init__(self, w1_hbm, w2_hbm, w1_scr, w2_scr, stage1, stage2, sem, layer):
        self.refs = (w1_hbm, w2_hbm, w1_scr, w2_scr, stage1, stage2, sem)
        self.layer = layer

    def _copies(self, p):
        w1_hbm, w2_hbm, _, _, stage1, stage2, sem = self.refs
        cols = pl.ds(p * MLP_LOAD, MLP_LOAD)
        slot = p % MLP_SLOTS
        return (pltpu.make_async_copy(w1_hbm.at[self.layer, :, cols], stage1.at[slot], sem.at[0, slot]),
                pltpu.make_async_copy(w2_hbm.at[self.layer, cols, :], stage2.at[slot], sem.at[1, slot]))

    def start(self, p):
        for cp in self._copies(p):
            cp.start()

    def prefetch(self):
        for p in range(MLP_SLOTS - 1):
            self.start(p)

    def finish(self, p):
        _, _, w1_scr, w2_scr, stage1, stage2, _ = self.refs
        ahead = p + MLP_SLOTS - 1
        if ahead < D_FF // MLP_LOAD:
            self.start(ahead)
        for cp in self._copies(p):
            cp.wait()
        cols = slice(p * MLP_LOAD, (p + 1) * MLP_LOAD)
        w1_scr[:, cols] = stage1[p % MLP_SLOTS].astype(BF16)
        w2_scr[cols, :] = stage2[p % MLP_SLOTS].astype(BF16)


def _mlp_tail(x, mix, m_ref, gn2_ref, w1_ref, w2_ref, loading=None):
    y1 = x + m_ref[:, 2 * D_MODEL:3 * D_MODEL] * mix
    h = _norm_mod(y1, gn2_ref[...], m_ref[:, 3 * D_MODEL:4 * D_MODEL], m_ref[:, 4 * D_MODEL:5 * D_MODEL]).astype(BF16)
    nchunk = D_FF // MLP_CHUNK
    acc = jnp.zeros(y1.shape, F32)
    for c in range(nchunk):
        cols = slice(c * MLP_CHUNK, (c + 1) * MLP_CHUNK)
        if loading is not None:
            per = MLP_CHUNK // MLP_LOAD
            for p in range(c * per, (c + 1) * per):
                loading.finish(p)
        a = _dot(h, w1_ref[:, cols])
        a = jnp.square(jnp.maximum(a, 0.0)).astype(BF16)
        acc = acc + _dot(a, w2_ref[cols, :])
    return y1 + m_ref[:, 5 * D_MODEL:6 * D_MODEL] * acc


def _run_tail(x_and_mix, m_ref, gn2_ref, weights, w1_ref, w2_ref, emit, emit_sample=None):
    pid = pl.program_id(0)
    first = pid == 0
    later = jnp.logical_not(first)

    @pl.when(first)
    def _():
        weights.prefetch()
        x, mix = x_and_mix()
        emit(_mlp_tail(x, mix, m_ref, gn2_ref, w1_ref, w2_ref, loading=weights))

    def plain(store):
        x, mix = x_and_mix()
        store(_mlp_tail(x, mix, m_ref, gn2_ref, w1_ref, w2_ref))

    if emit_sample is None:
        pl.when(later)(lambda: plain(emit))
    else:
        is_sample = pid >= T_PROMPT // _OUT_TM
        pl.when(later & jnp.logical_not(is_sample))(lambda: plain(emit))
        pl.when(is_sample)(lambda: plain(emit_sample))


def _even_out_kernel(xp_ref, xs_ref, y5_ref, u_ref, dskip_ref, wglu_ref, bglu_ref, glap_ref, glas_ref, wout_ref,
                     m_ref, gn2_ref, w1_hbm, w2_hbm, o_ref, w1_ref, w2_ref, stage1, stage2, sem, *, layer):
    weights = _MlpWeights(w1_hbm, w2_hbm, w1_ref, w2_ref, stage1, stage2, sem, layer)

    def x_and_mix():
        nblk = S5_UBLK
        ys = (jnp.concatenate([y5_ref[b] for b in range(nblk)], axis=1)
              + jnp.concatenate([u_ref[b] for b in range(nblk)], axis=1) * dskip_ref[...])
        gl = jax.nn.gelu(ys)
        s5o = gl * jax.nn.sigmoid(_dot(gl.astype(BF16), wglu_ref[...]) + bglu_ref[...])
        gla = _token_tile(glap_ref, glas_ref, _OUT_TM).astype(BF16)
        mix = _dot(s5o.astype(BF16), wout_ref[0:S5_WIDTH, :]) + _dot(gla, wout_ref[S5_WIDTH:, :])
        return _token_tile(xp_ref, xs_ref, _OUT_TM), mix

    def emit(y):
        o_ref[...] = y

    _run_tail(x_and_mix, m_ref, gn2_ref, weights, w1_ref, w2_ref, emit)


def _odd_out_kernel(x_ref, attp_ref, atts_ref, wo_ref, m_ref, gn2_ref, w1_hbm, w2_hbm, op_ref, os_ref,
                    w1_ref, w2_ref, stage1, stage2, sem, *, layer):
    weights = _MlpWeights(w1_hbm, w2_hbm, w1_ref, w2_ref, stage1, stage2, sem, layer)

    def x_and_mix():
        return x_ref[...], _dot(_token_tile(attp_ref, atts_ref, _OUT_TM), wo_ref[...])

    def emit_prompt(y):
        op_ref[...] = y

    def emit_sample(y):
        os_ref[...] = y

    _run_tail(x_and_mix, m_ref, gn2_ref, weights, w1_ref, w2_ref, emit_prompt, emit_sample)


_OUT_TM = 512


def _const_spec(shape):
    return pl.BlockSpec(shape, lambda i: (0,) * len(shape), pipeline_mode=pl.Buffered(1))


def _tail_specs(layer):
    tm = _OUT_TM
    return [
        pl.BlockSpec((None, 1, 6 * D_MODEL), lambda i: (_cond_row(i, tm), 0, 0)),
        _const_spec((1, D_MODEL)),
        pl.BlockSpec(memory_space=pl.ANY),
        pl.BlockSpec(memory_space=pl.ANY),
    ]


def _tail_scratch():
    return [pltpu.VMEM((D_MODEL, D_FF), BF16), pltpu.VMEM((D_FF, D_MODEL), BF16),
            pltpu.VMEM((MLP_SLOTS, D_MODEL, MLP_LOAD), F32), pltpu.VMEM((MLP_SLOTS, MLP_LOAD, D_MODEL), F32),
            pltpu.SemaphoreType.DMA((2, MLP_SLOTS))]


_TAIL_PARAMS = dict(dimension_semantics=("arbitrary",), vmem_limit_bytes=VMEM_LIMIT_TAIL)


def _even_out_call(xp, xs, y5, u, d_skip, w_glu, b_glu, gla_p, gla_s, w_out, mods, layer, gn2, w1, w2):
    tm = _OUT_TM
    return pl.pallas_call(
        functools.partial(_even_out_kernel, layer=layer),
        out_shape=jax.ShapeDtypeStruct((T_TOK, D_MODEL), F32),
        grid=(T_TOK // tm,),
        in_specs=_token_specs(tm) + [
            pl.BlockSpec((S5_UBLK, tm, LANES), lambda i: (0, i, 0)),
            pl.BlockSpec((S5_UBLK, tm, LANES), lambda i: (0, i, 0)),
            _const_spec((1, S5_WIDTH)),
            _const_spec((S5_WIDTH, S5_WIDTH)),
            _const_spec((1, S5_WIDTH)),
        ] + _token_specs(tm, GLA_VW) + [
            _const_spec((S5_WIDTH + GLA_VW, D_MODEL)),
        ] + _tail_specs(layer),
        out_specs=pl.BlockSpec((tm, D_MODEL), lambda i: (i, 0)),
        scratch_shapes=_tail_scratch(),
        compiler_params=pltpu.CompilerParams(**_TAIL_PARAMS),
        name="even_out_mlp",
    )(xp, xs, y5, u, d_skip, w_glu, b_glu, gla_p, gla_s, w_out, mods, gn2, w1, w2)


def _odd_out_call(x, att_p, att_s, w_o, mods, layer, gn2, w1, w2):
    tm = _OUT_TM
    return pl.pallas_call(
        functools.partial(_odd_out_kernel, layer=layer),
        out_shape=(jax.ShapeDtypeStruct((T_PROMPT, D_MODEL), F32),
                   jax.ShapeDtypeStruct((T_SAMPLE, D_MODEL), F32)),
        grid=(T_TOK // tm,),
        in_specs=[pl.BlockSpec((tm, D_MODEL), lambda i: (i, 0))] + _token_specs(tm) + [
            _const_spec((D_MODEL, D_MODEL)),
        ] + _tail_specs(layer),
        out_specs=tuple(_token_specs(tm)),
        scratch_shapes=_tail_scratch(),
        compiler_params=pltpu.CompilerParams(**_TAIL_PARAMS),
        name="odd_out_mlp",
    )(x, att_p, att_s, w_o, mods, gn2, w1, w2)


def _qkv_kernel(x_ref, gn_ref, m_ref, w_ref, qn_ref, kn_ref, cos_ref, sin_ref,
                q_ref, kb_ref, vb_ref, k32_ref, v32_ref, *, tile):
    even_lane = (lax.broadcasted_iota(jnp.int32, (1, HEAD_DIM), 1) & 1) == 0

    def heads(rope):
        h = _norm_mod(x_ref[...], gn_ref[...], m_ref[:, 0:D_MODEL], m_ref[:, D_MODEL:2 * D_MODEL]).astype(BF16)
        z = _dot(h, w_ref[...])
        v = z[:, (N_HEADS + KV_HEADS) * HEAD_DIM:]
        vb_ref[...] = v.astype(BF16)
        if not rope:
            for kh in range(KV_HEADS):
                v32_ref[:, kh, :] = v[:, kh * HEAD_DIM:(kh + 1) * HEAD_DIM]
        for hd in range(N_HEADS + KV_HEADS):
            xh = z[:, hd * HEAD_DIM:(hd + 1) * HEAD_DIM]
            gain = qn_ref[...] if hd < N_HEADS else kn_ref[...]
            xh = xh * lax.rsqrt(jnp.mean(xh * xh, axis=-1, keepdims=True) + EPS) * gain
            if rope:
                partner = jnp.where(even_lane, pltpu.roll(xh, HEAD_DIM - 1, 1), pltpu.roll(xh, 1, 1))
                xh = xh * cos_ref[...] + partner * sin_ref[...]
            if hd < N_HEADS:
                q_ref[:, hd * HEAD_DIM:(hd + 1) * HEAD_DIM] = xh.astype(BF16)
            else:
                cols = slice((hd - N_HEADS) * HEAD_DIM, (hd - N_HEADS + 1) * HEAD_DIM)
                kb_ref[:, cols] = xh.astype(BF16)
                if not rope:
                    k32_ref[:, hd - N_HEADS, :] = xh

    is_sample = pl.program_id(0) >= T_PROMPT // tile

    @pl.when(is_sample)
    def _():
        heads(True)

    @pl.when(jnp.logical_not(is_sample))
    def _():
        heads(False)


def _qkv_call(x, gn, mods, layer, w_qkv, q_norm, k_norm, cos_t, sin_t):
    tm = 512
    pos_tiles = DEC_SEQ // tm
    n_prompt = T_PROMPT // tm
    kvw = KV_HEADS * HEAD_DIM

    def pos_map(i):
        return (jnp.maximum(i - n_prompt, 0) % pos_tiles, 0)

    def prompt_map(i):
        return (jnp.minimum(i, n_prompt - 1), 0, 0)

    return pl.pallas_call(
        functools.partial(_qkv_kernel, tile=tm),
        out_shape=(jax.ShapeDtypeStruct((T_TOK, N_HEADS * HEAD_DIM), BF16),
                   jax.ShapeDtypeStruct((T_TOK, kvw), BF16),
                   jax.ShapeDtypeStruct((T_TOK, kvw), BF16),
                   jax.ShapeDtypeStruct((T_PROMPT, KV_HEADS, HEAD_DIM), F32),
                   jax.ShapeDtypeStruct((T_PROMPT, KV_HEADS, HEAD_DIM), F32)),
        grid=(T_TOK // tm,),
        in_specs=[
            pl.BlockSpec((tm, D_MODEL), lambda i: (i, 0)),
            pl.BlockSpec((1, D_MODEL), lambda i: (0, 0)),
            pl.BlockSpec((None, 1, 6 * D_MODEL), lambda i: (_cond_row(i, tm), 0, 0)),
            pl.BlockSpec(w_qkv.shape, lambda i: (0, 0)),
            pl.BlockSpec((1, HEAD_DIM), lambda i: (0, 0)),
            pl.BlockSpec((1, HEAD_DIM), lambda i: (0, 0)),
            pl.BlockSpec((tm, HEAD_DIM), pos_map),
            pl.BlockSpec((tm, HEAD_DIM), pos_map),
        ],
        out_specs=(pl.BlockSpec((tm, N_HEADS * HEAD_DIM), lambda i: (i, 0)),
                   pl.BlockSpec((tm, kvw), lambda i: (i, 0)),
                   pl.BlockSpec((tm, kvw), lambda i: (i, 0)),
                   pl.BlockSpec((tm, KV_HEADS, HEAD_DIM), prompt_map),
                   pl.BlockSpec((tm, KV_HEADS, HEAD_DIM), prompt_map)),
        compiler_params=pltpu.CompilerParams(vmem_limit_bytes=VMEM_LIMIT),
        name="odd_qkv",
    )(x, gn, mods, w_qkv, q_norm, k_norm, cos_t, sin_t)


def _rope_tables():
    f32 = np.float32
    rows = DEC_SEQ // GRID_W
    row = np.repeat(np.arange(rows, dtype=f32), GRID_W)
    col = np.tile(np.arange(GRID_W, dtype=f32), rows)
    inv = np.power(f32(ROPE_THETA), -np.arange(0, AXIS_DIM, 2, dtype=f32) / f32(AXIS_DIM)).astype(f32)
    ang = np.concatenate([row[:, None] * inv, col[:, None] * inv], axis=-1).astype(f32)
    cos_t = np.repeat(np.cos(ang), 2, axis=-1).astype(f32)
    sin = np.sin(ang).astype(f32)
    sin_t = np.stack([-sin, sin], axis=-1).reshape(DEC_SEQ, HEAD_DIM)
    return jnp.asarray(cos_t), jnp.asarray(sin_t)


def _attn_kernel(*refs, seq_len, has_cache):
    q_ref, k_ref, v_ref = refs[:3]
    ck_ref, cv_ref = refs[3:5] if has_cache else (None, None)
    o_ref = refs[-1]
    c = HEAD_DIM ** -0.5 * math.log2(math.e)
    ones_col = (lax.broadcasted_iota(jnp.int32, (1, HEAD_DIM), 1) == 0).astype(BF16)

    def with_ones(v):
        return jnp.concatenate([v, jnp.broadcast_to(ones_col, v.shape)], axis=1)

    if has_cache:
        ck = ck_ref[...].astype(BF16)
        cv = with_ones(cv_ref[...].astype(BF16))
    def one_sequence(rows):
        k = k_ref[rows, :]
        v = with_ones(v_ref[rows, :])
        for r in range(Q_PER_KV):
            cs = slice(r * HEAD_DIM, (r + 1) * HEAD_DIM)
            q = q_ref[rows, cs]
            s = lax.dot_general(q, k, NT_DIMS, preferred_element_type=F32)
            m = jnp.max(s, axis=-1, keepdims=True)
            if has_cache:
                sc = lax.dot_general(q, ck, NT_DIMS, preferred_element_type=F32)
                m = jnp.maximum(m, jnp.max(sc, axis=-1, keepdims=True))
            mc = m * c
            o = _dot(jnp.exp2(s * c - mc).astype(BF16), v)
            if has_cache:
                o = o + _dot(jnp.exp2(sc * c - mc).astype(BF16), cv)
            o_ref[rows, cs] = (o[:, 0:HEAD_DIM] / o[:, HEAD_DIM:HEAD_DIM + 1]).astype(BF16)

    nsub = q_ref.shape[0] // seq_len
    if nsub == 1:
        one_sequence(slice(0, seq_len))
    else:
        def body(j, carry):
            one_sequence(pl.ds(pl.multiple_of(j * seq_len, seq_len), seq_len))
            return carry

        lax.fori_loop(0, nsub, body, 0)


def _attn_call(q, k, v, cache_k, cache_v, seq_len, row0, nrows, nsub):
    blk = nsub * seq_len
    assert row0 % blk == 0 and nrows % blk == 0
    has_cache = cache_k is not None
    assert not has_cache or nsub == 1
    b0 = row0 // blk
    gw = Q_PER_KV * HEAD_DIM
    in_specs = [
        pl.BlockSpec((blk, gw), lambda b, g: (b0 + b, g)),
        pl.BlockSpec((blk, HEAD_DIM), lambda b, g: (b0 + b, g)),
        pl.BlockSpec((blk, HEAD_DIM), lambda b, g: (b0 + b, g)),
    ]
    args = [q, k, v]
    if has_cache:
        in_specs += [pl.BlockSpec((PAST_LEN, HEAD_DIM), lambda b, g: (b, g)),
                     pl.BlockSpec((PAST_LEN, HEAD_DIM), lambda b, g: (b, g))]
        args += [cache_k, cache_v]
    return pl.pallas_call(
        functools.partial(_attn_kernel, seq_len=seq_len, has_cache=has_cache),
        out_shape=jax.ShapeDtypeStruct((nrows, N_HEADS * HEAD_DIM), BF16),
        grid=(nrows // blk, KV_HEADS),
        in_specs=in_specs,
        out_specs=pl.BlockSpec((blk, gw), lambda b, g: (b, g)),
        compiler_params=pltpu.CompilerParams(vmem_limit_bytes=VMEM_LIMIT),
        name=f"attn_len{seq_len}",
    )(*args)


def kernel(x_prompt, x_sample, state_s5_re, state_s5_im, state_gla, cache_k, cache_v, c, c_ctx, norm_mix, norm_mlp, w_ada, b_ada, w_mlp_in, w_mlp_out, w_in_e, w_out_e, s5_lambda_re, s5_lambda_im, s5_log_dt, s5_b_re, s5_b_im, s5_c_re, s5_c_im, s5_d, s5_w_glu, s5_b_glu, gla_w_gate2, gla_b_gate, gla_norm, w_qkv_o, w_o_o, q_norm, k_norm):
    xp = x_prompt.reshape(T_PROMPT, D_MODEL)
    xs = x_sample.reshape(T_SAMPLE, D_MODEL)
    cond8 = jnp.concatenate([c_ctx[None, :], c, jnp.zeros((COND_ROWS - 1 - DEC_BATCH, D_MODEL), F32)], axis=0)
    b_ada3 = b_ada.reshape(DEPTH, 1, 6 * D_MODEL)
    w1_all, w2_all = w_mlp_in, w_mlp_out
    mats, mods0 = _s5_prep_call(s5_lambda_re[0], s5_lambda_im[0], s5_log_dt[0], s5_b_re[0], s5_b_im[0],
                                s5_c_re[0], s5_c_im[0], ada=(cond8, w_ada, b_ada3, 0))

    w_in = jnp.pad(w_in_e[0], ((0, 0), (0, LANES - 2 * GLA_RANK))).astype(BF16)
    zg = jnp.zeros((GLA_RANK, GLA_QK), F32)
    w_gate = jnp.concatenate([jnp.concatenate([gla_w_gate2[0, 0], zg], axis=1),
                              jnp.concatenate([zg, gla_w_gate2[0, 1]], axis=1),
                              jnp.zeros((LANES - 2 * GLA_RANK, 2 * GLA_QK), F32)], axis=0).astype(BF16)
    b_gate = gla_b_gate[0].reshape(1, 2 * GLA_QK)
    u, z, g = _inproj_call(xp, xs, norm_mix[0:1], mods0, 0, w_in, w_gate, b_gate)

    def state_rows(s):
        return jnp.transpose(s, (2, 0, 1, 3)).reshape(S5_GROUPS, DEC_BATCH, 2 * S5_STATE)

    h0 = jnp.concatenate([state_rows(state_s5_re[:, 0]), state_rows(state_s5_im[:, 0])], axis=-1)
    nsteps = S5_GROUPS // S5_GPB
    h0 = jnp.transpose(h0.reshape(nsteps, S5_GPB, DEC_BATCH, S5_W), (0, 2, 1, 3))
    y5, ns = _s5_call(u, mats, h0)
    ns = jnp.transpose(ns, (0, 2, 1, 3)).reshape(S5_GROUPS, BATCH, S5_W)

    def state_out(n):
        return jnp.transpose(n.reshape(S5_GROUPS, BATCH, 2, S5_STATE), (1, 2, 0, 3))[:, None]

    new_s5_re = state_out(ns[:, :, :2 * S5_STATE])
    new_s5_im = state_out(ns[:, :, 2 * S5_STATE:])

    gn_gla = gla_norm[0].reshape(1, GLA_DV)
    gla_p, sfin, w_glu, w_out, w_qkv, w_o = _gla_call(z, g, gn_gla, None, SEQ, BATCH, 0, nsub=4,
                                                      casts=(s5_w_glu, w_out_e, w_qkv_o, w_o_o))
    s0 = state_gla[:, 0].reshape(DEC_BATCH, 2, GLA_QK, GLA_DV)
    gla_s, _, mods1 = _gla_call(z, g, gn_gla, s0, DEC_SEQ, DEC_BATCH, T_PROMPT, nsub=1,
                                ada=(cond8, w_ada, b_ada3, 1))
    new_gla = sfin.reshape(BATCH, 1, 2, GLA_HEADS, GLA_DK, GLA_DV)

    x = _even_out_call(xp, xs, y5, u, s5_d[0].reshape(1, S5_WIDTH), w_glu,
                       s5_b_glu[0].reshape(1, S5_WIDTH), gla_p, gla_s, w_out, mods0, 0,
                       norm_mlp[0:1], w1_all, w2_all)

    cos_t, sin_t = _rope_tables()
    q, k, v, k32, v32 = _qkv_call(x, norm_mix[1:2], mods1, 1, w_qkv,
                                  q_norm[0].reshape(1, HEAD_DIM), k_norm[0].reshape(1, HEAD_DIM), cos_t, sin_t)
    att_p = _attn_call(q, k, v, None, None, SEQ, 0, T_PROMPT, nsub=4)
    ck = cache_k[:, 0].reshape(DEC_BATCH * PAST_LEN, KV_HEADS * HEAD_DIM)
    cv = cache_v[:, 0].reshape(DEC_BATCH * PAST_LEN, KV_HEADS * HEAD_DIM)
    att_s = _attn_call(q, k, v, ck, cv, DEC_SEQ, T_PROMPT, T_SAMPLE, nsub=1)
    yp, ys = _odd_out_call(x, att_p, att_s, w_o, mods1, 1, norm_mlp[1:2],
                           w1_all, w2_all)

    new_k = k32.reshape(BATCH, 1, SEQ, KV_HEADS, HEAD_DIM)
    new_v = v32.reshape(BATCH, 1, SEQ, KV_HEADS, HEAD_DIM)
    y_prompt = yp.reshape(BATCH, SEQ, D_MODEL)
    y_sample = ys.reshape(DEC_BATCH, DEC_SEQ, D_MODEL)
    return (y_prompt, y_sample, new_s5_re, new_s5_im, new_gla, new_k, new_v)
```
